```python
import math
import jax, jax.numpy as jnp
from jax import lax
import numpy as np

D_MODEL = 1024
BATCH = 8
SEQ = 2048
DEPTH = 2

MLA_HEADS = 8
QK_NOPE_DIM = 64
QK_ROPE_DIM = 32
QK_HEAD_DIM = QK_NOPE_DIM + QK_ROPE_DIM
V_HEAD_DIM = 64
Q_LORA_RANK = 384
KV_LORA_RANK = 256
ROPE_THETA = 10000.0
ATTN_BLOCK = 128
MLA_OUT = MLA_HEADS * V_HEAD_DIM
POOL_WINDOWS = (2, 4, 8, 16)
POOL_GROUP_DIM = 128
POOL_GROUPS = len(POOL_WINDOWS)
POOL_DIM = POOL_GROUPS * POOL_GROUP_DIM
SSD_HEADS = 16
SSD_HEAD_DIM = 64
SSD_INNER = SSD_HEADS * SSD_HEAD_DIM
SSD_GROUPS = 2
SSD_STATE = 128
SSD_CONV = 4
SSD_CHUNK = 128
SSD_CONV_DIM = SSD_INNER + 2 * SSD_GROUPS * SSD_STATE
FFN_DIM = 2816
FFN_CONV = 3
N_BRANCH = 3
BRANCH_DIM = MLA_OUT + POOL_DIM + SSD_INNER
IN_SIZES = (Q_LORA_RANK, KV_LORA_RANK + QK_ROPE_DIM, POOL_DIM, SSD_INNER, SSD_CONV_DIM, SSD_HEADS, N_BRANCH * D_MODEL)
IN_DIM = sum(IN_SIZES)
EPS = 1e-6

kernel_name = 'hybrid_mla_pool_ssd_gated_block'


def _split(x, sizes):
    outs, start = [], 0
    for size in sizes:
        outs.append(x[..., start:start + size])
        start += size
    return outs


def rms_norm(x, w):
    xf = x.astype(jnp.float32)
    y = xf * lax.rsqrt(jnp.mean(xf * xf, axis=-1, keepdims=True) + EPS)
    return (y * w.astype(jnp.float32)).astype(x.dtype)


def modulate(h, shift, scale):
    return h * (1 + scale[:, None, :]) + shift[:, None, :]


def causal_dwconv(x, w, b):
    k = w.shape[0]
    y = lax.conv_general_dilated(x, w[:, None, :].astype(x.dtype), window_strides=(1,), padding=[(k - 1, 0)],
                                 dimension_numbers=('NWC', 'WIO', 'NWC'), feature_group_count=x.shape[-1])
    return y + b.astype(x.dtype)


def rope_tables(positions):
    inv_freq = ROPE_THETA ** (-jnp.arange(0, QK_ROPE_DIM, 2, dtype=jnp.float32) / QK_ROPE_DIM)
    ang = positions.astype(jnp.float32)[..., None] * inv_freq
    return jnp.cos(ang), jnp.sin(ang)


def apply_rope(x, cos, sin):
    half = x.shape[-1] // 2
    x1, x2 = x[..., :half], x[..., half:]
    return jnp.concatenate([x1 * cos - x2 * sin, x2 * cos + x1 * sin], axis=-1).astype(x.dtype)


def mla_branch(q_lat, kv_lat, cos, sin, q_a_norm, w_q_b, kv_a_norm, w_kv_b, q_norm, k_norm):
    b, s, _ = q_lat.shape
    q = (rms_norm(q_lat, q_a_norm) @ w_q_b).reshape(b, s, MLA_HEADS, QK_HEAD_DIM)
    q_nope, q_rope = q[..., :QK_NOPE_DIM], q[..., QK_NOPE_DIM:]
    c_kv, k_rope = kv_lat[..., :KV_LORA_RANK], kv_lat[..., KV_LORA_RANK:]
    kv = (rms_norm(c_kv, kv_a_norm) @ w_kv_b).reshape(b, s, MLA_HEADS, QK_NOPE_DIM + V_HEAD_DIM)
    k_nope, v = kv[..., :QK_NOPE_DIM], kv[..., QK_NOPE_DIM:]
    q_nope = rms_norm(q_nope, q_norm[:QK_NOPE_DIM])
    q_rope = apply_rope(rms_norm(q_rope, q_norm[QK_NOPE_DIM:]), cos[:, :, None], sin[:, :, None])
    k_nope = rms_norm(k_nope, k_norm[:QK_NOPE_DIM])
    k_rope = apply_rope(rms_norm(k_rope, k_norm[QK_NOPE_DIM:]), cos, sin)
    n_blk = s // ATTN_BLOCK
    qn = q_nope.reshape(b, n_blk, ATTN_BLOCK, MLA_HEADS, QK_NOPE_DIM).swapaxes(0, 1)
    qr = q_rope.reshape(b, n_blk, ATTN_BLOCK, MLA_HEADS, QK_ROPE_DIM).swapaxes(0, 1)
    key_pos = jnp.arange(s)
    scale = QK_HEAD_DIM ** -0.5

    def attend(args):
        qn_b, qr_b, blk = args
        sc = jnp.einsum('bqhd,bkhd->bhqk', qn_b, k_nope) + jnp.einsum('bqhd,bkd->bhqk', qr_b, k_rope)
        sc = sc.astype(jnp.float32) * scale
        q_pos = blk * ATTN_BLOCK + jnp.arange(ATTN_BLOCK)
        sc = jnp.where(key_pos[None, :] <= q_pos[:, None], sc, -jnp.inf)
        p = jax.nn.softmax(sc, axis=-1).astype(v.dtype)
        return jnp.einsum('bhqk,bkhd->bqhd', p, v)

    o = lax.map(attend, (qn, qr, jnp.arange(n_blk)))
    return o.swapaxes(0, 1).reshape(b, s, MLA_OUT)


def pool_branch(u, pool_w, pool_scale):
    b, s, _ = u.shape
    uf = u.astype(jnp.float32).reshape(b, s, POOL_GROUPS, POOL_GROUP_DIM)
    cs = jnp.cumsum(uf, axis=1)
    t = jnp.arange(s)
    outs = []
    for g, w in enumerate(POOL_WINDOWS):
        csg = cs[:, :, g]
        lagged = jnp.pad(csg, ((0, 0), (w, 0), (0, 0)))[:, :s]
        count = jnp.minimum(t + 1, w).astype(jnp.float32)[None, :, None]
        outs.append((csg - lagged) / count - uf[:, :, g])
    pooled = jnp.stack(outs, axis=2).astype(u.dtype)
    mixed = jnp.einsum('bsgc,gcd->bsgd', pooled, pool_w)
    return mixed.reshape(b, s, POOL_DIM) * pool_scale


def ssd_branch(z, xbc, dt_raw, conv_w, conv_b, dt_bias, a_log, d_skip, norm_w):
    b, s, _ = z.shape
    nc, lc = s // SSD_CHUNK, SSD_CHUNK
    r = SSD_HEADS // SSD_GROUPS
    xbc = jax.nn.silu(causal_dwconv(xbc, conv_w, conv_b))
    xs, bm, cm = _split(xbc, (SSD_INNER, SSD_GROUPS * SSD_STATE, SSD_GROUPS * SSD_STATE))
    dt = jax.nn.softplus(dt_raw.astype(jnp.float32) + dt_bias.astype(jnp.float32))
    a = -jnp.exp(a_log.astype(jnp.float32))
    x_h = xs.reshape(b, s, SSD_HEADS, SSD_HEAD_DIM)
    xdt = (x_h.astype(jnp.float32) * dt[..., None]).reshape(b, nc, lc, SSD_GROUPS, r, SSD_HEAD_DIM)
    bc = bm.astype(jnp.float32).reshape(b, nc, lc, SSD_GROUPS, SSD_STATE)
    cc = cm.astype(jnp.float32).reshape(b, nc, lc, SSD_GROUPS, SSD_STATE)
    da = (dt * a).reshape(b, nc, lc, SSD_GROUPS, r).transpose(0, 3, 4, 1, 2)
    a_cs = jnp.cumsum(da, axis=-1)
    causal = jnp.tril(jnp.ones((lc, lc), dtype=bool))
    seg = a_cs[..., :, None] - a_cs[..., None, :]
    decay = jnp.exp(jnp.where(causal, seg, -jnp.inf))
    cb = jnp.einsum('bclgn,bcsgn->bcgls', cc, bc)
    y_diag = jnp.einsum('bcgls,bgrcls,bcsgrp->bclgrp', cb, decay, xdt)
    decay_states = jnp.exp(a_cs[..., -1:] - a_cs)
    states = jnp.einsum('bclgn,bgrcl,bclgrp->bcgrpn', bc, decay_states, xdt)
    chunk_decay = jnp.exp(a_cs[..., -1])

    def step(h, inp):
        st, dc = inp
        return h * dc[..., None, None] + st, h

    h0 = jnp.zeros((b, SSD_GROUPS, r, SSD_HEAD_DIM, SSD_STATE), jnp.float32)
    _, prev = lax.scan(step, h0, (states.transpose(1, 0, 2, 3, 4, 5), chunk_decay.transpose(3, 0, 1, 2)))
    y_off = jnp.einsum('bclgn,cbgrpn,bgrcl->bclgrp', cc, prev, jnp.exp(a_cs))
    y = (y_diag + y_off).reshape(b, s, SSD_HEADS, SSD_HEAD_DIM) + x_h * d_skip[:, None]
    gated = y.reshape(b, s, SSD_INNER) * jax.nn.silu(z.astype(jnp.float32))
    gated = gated.reshape(b, s, SSD_GROUPS, SSD_INNER // SSD_GROUPS)
    gated = gated * lax.rsqrt(jnp.mean(gated * gated, axis=-1, keepdims=True) + EPS)
    return (gated.reshape(b, s, SSD_INNER) * norm_w.astype(jnp.float32)).astype(z.dtype)


def _fwd_setup_inputs(seed: int = 0) -> dict:
    key = jax.random.key(seed)
    ks = iter(jax.random.split(key, 40))
    L, D = DEPTH, D_MODEL

    def nrm(shape, scale):
        return jax.random.normal(next(ks), shape, jnp.float32) * scale

    x = nrm((BATCH, SEQ, D), 1.0)
    c = nrm((BATCH, D), 1.0)
    offset = jax.random.randint(next(ks), (BATCH, 1), 0, 4096, dtype=jnp.int32)
    positions = offset + jnp.arange(SEQ, dtype=jnp.int32)[None, :]
    ada_w = nrm((L, D, 6 * D), D ** -0.5)
    ada_b = nrm((L, 6 * D), 0.02)
    norm1_w = 1.0 + nrm((L, D), 0.02)
    w_in = nrm((L, D, IN_DIM), D ** -0.5)
    q_a_norm = 1.0 + nrm((L, Q_LORA_RANK), 0.02)
    w_q_b = nrm((L, Q_LORA_RANK, MLA_HEADS * QK_HEAD_DIM), Q_LORA_RANK ** -0.5)
    kv_a_norm = 1.0 + nrm((L, KV_LORA_RANK), 0.02)
    w_kv_b = nrm((L, KV_LORA_RANK, MLA_HEADS * (QK_NOPE_DIM + V_HEAD_DIM)), KV_LORA_RANK ** -0.5)
    q_norm = 1.0 + nrm((L, QK_HEAD_DIM), 0.02)
    k_norm = 1.0 + nrm((L, QK_HEAD_DIM), 0.02)
    pool_w = nrm((L, POOL_GROUPS, POOL_GROUP_DIM, POOL_GROUP_DIM), POOL_GROUP_DIM ** -0.5)
    pool_scale = 1.0 + nrm((L, POOL_DIM), 0.1)
    ssd_conv_w = nrm((L, SSD_CONV, SSD_CONV_DIM), SSD_CONV ** -0.5)
    ssd_conv_b = nrm((L, SSD_CONV_DIM), 0.02)
    dt0 = jnp.exp(jax.random.uniform(next(ks), (L, SSD_HEADS), jnp.float32, math.log(1e-3), math.log(1e-1)))
    ssd_dt_bias = dt0 + jnp.log(-jnp.expm1(-dt0))
    ssd_a_log = jnp.log(jax.random.uniform(next(ks), (L, SSD_HEADS), jnp.float32, 1.0, 16.0))
    ssd_d = 1.0 + nrm((L, SSD_HEADS), 0.1)
    ssd_norm_w = 1.0 + nrm((L, SSD_INNER), 0.02)
    w_branch = nrm((L, BRANCH_DIM, D), MLA_OUT ** -0.5)
    w_out = nrm((L, D, D), D ** -0.5)
    norm2_w = 1.0 + nrm((L, D), 0.02)
    ffn_up = nrm((L, D, 2 * FFN_DIM), D ** -0.5)
    ffn_conv_w = nrm((L, FFN_CONV, 2 * FFN_DIM), FFN_CONV ** -0.5)
    ffn_conv_b = nrm((L, 2 * FFN_DIM), 0.02)
    ffn_down = nrm((L, FFN_DIM, D), FFN_DIM ** -0.5)
    return {'x': x, 'c': c, 'positions': positions, 'ada_w': ada_w, 'ada_b': ada_b, 'norm1_w': norm1_w,
            'w_in': w_in, 'q_a_norm': q_a_norm, 'w_q_b': w_q_b, 'kv_a_norm': kv_a_norm, 'w_kv_b': w_kv_b,
            'q_norm': q_norm, 'k_norm': k_norm, 'pool_w': pool_w, 'pool_scale': pool_scale,
            'ssd_conv_w': ssd_conv_w, 'ssd_conv_b': ssd_conv_b, 'ssd_dt_bias': ssd_dt_bias,
            'ssd_a_log': ssd_a_log, 'ssd_d': ssd_d, 'ssd_norm_w': ssd_norm_w, 'w_branch': w_branch,
            'w_out': w_out, 'norm2_w': norm2_w, 'ffn_up': ffn_up, 'ffn_conv_w': ffn_conv_w,
            'ffn_conv_b': ffn_conv_b, 'ffn_down': ffn_down}


def _fwd_reference(x, c, positions, ada_w, ada_b, norm1_w, w_in, q_a_norm, w_q_b, kv_a_norm, w_kv_b, q_norm, k_norm,
              pool_w, pool_scale, ssd_conv_w, ssd_conv_b, ssd_dt_bias, ssd_a_log, ssd_d, ssd_norm_w, w_branch,
              w_out, norm2_w, ffn_up, ffn_conv_w, ffn_conv_b, ffn_down):
    cos, sin = rope_tables(positions)
    c_act = jax.nn.silu(c)
    for l in range(DEPTH):
        mod = c_act @ ada_w[l] + ada_b[l]
        sh1, sc1, g1, sh2, sc2, g2 = jnp.split(mod, 6, axis=-1)
        h = modulate(rms_norm(x, norm1_w[l]), sh1, sc1)
        proj = h @ w_in[l]
        q_lat, kv_lat, u_pool, z, xbc, dt_raw, gate_logits = _split(proj, IN_SIZES)
        o_a = mla_branch(q_lat, kv_lat, cos, sin, q_a_norm[l], w_q_b[l], kv_a_norm[l], w_kv_b[l],
                         q_norm[l], k_norm[l])
        o_b = pool_branch(u_pool, pool_w[l], pool_scale[l])
        o_c = ssd_branch(z, xbc, dt_raw, ssd_conv_w[l], ssd_conv_b[l], ssd_dt_bias[l], ssd_a_log[l],
                         ssd_d[l], ssd_norm_w[l])
        wb = w_branch[l]
        y_a = o_a @ wb[:MLA_OUT]
        y_b = o_b @ wb[MLA_OUT:MLA_OUT + POOL_DIM]
        y_c = o_c @ wb[MLA_OUT + POOL_DIM:]
        gate_a, gate_b, gate_c = jnp.split(jax.nn.sigmoid(gate_logits), N_BRANCH, axis=-1)
        merged = gate_a * y_a + gate_b * y_b + gate_c * y_c
        x = x + g1[:, None, :] * (merged @ w_out[l])
        h = modulate(rms_norm(x, norm2_w[l]), sh2, sc2)
        up = causal_dwconv(h @ ffn_up[l], ffn_conv_w[l], ffn_conv_b[l])
        u_gate, u_val = jnp.split(up, 2, axis=-1)
        x = x + g2[:, None, :] * ((jax.nn.silu(u_gate) * u_val) @ ffn_down[l])
    return x


import jax as _jax
import jax.numpy as _jnp

TWIN_FORMAT = 'train_step'
FWD_PARAMS = ['x', 'c', 'positions', 'ada_w', 'ada_b', 'norm1_w', 'w_in', 'q_a_norm', 'w_q_b', 'kv_a_norm', 'w_kv_b', 'q_norm', 'k_norm', 'pool_w', 'pool_scale', 'ssd_conv_w', 'ssd_conv_b', 'ssd_dt_bias', 'ssd_a_log', 'ssd_d', 'ssd_norm_w', 'w_branch', 'w_out', 'norm2_w', 'ffn_up', 'ffn_conv_w', 'ffn_conv_b', 'ffn_down']
TWIN_WEIGHTS = ['ada_w', 'ada_b', 'norm1_w', 'w_in', 'q_a_norm', 'w_q_b', 'kv_a_norm', 'w_kv_b', 'q_norm', 'k_norm', 'pool_w', 'pool_scale', 'ssd_conv_w', 'ssd_conv_b', 'ssd_dt_bias', 'ssd_a_log', 'ssd_d', 'ssd_norm_w', 'w_branch', 'w_out', 'norm2_w', 'ffn_up', 'ffn_conv_w', 'ffn_conv_b', 'ffn_down']
TWIN_DIFF_INPUT = 'x'
TWIN_INPUTS = ['x', 'c', 'positions', 'ada_w', 'ada_b', 'norm1_w', 'w_in', 'q_a_norm', 'w_q_b', 'kv_a_norm', 'w_kv_b', 'q_norm', 'k_norm', 'pool_w', 'pool_scale', 'ssd_conv_w', 'ssd_conv_b', 'ssd_dt_bias', 'ssd_a_log', 'ssd_d', 'ssd_norm_w', 'w_branch', 'w_out', 'norm2_w', 'ffn_up', 'ffn_conv_w', 'ffn_conv_b', 'ffn_down', 'loss_target', 'm_ada_w', 'm_ada_b', 'm_norm1_w', 'm_w_in', 'm_q_a_norm', 'm_w_q_b', 'm_kv_a_norm', 'm_w_kv_b', 'm_q_norm', 'm_k_norm', 'm_pool_w', 'm_pool_scale', 'm_ssd_conv_w', 'm_ssd_conv_b', 'm_ssd_dt_bias', 'm_ssd_a_log', 'm_ssd_d', 'm_ssd_norm_w', 'm_w_branch', 'm_w_out', 'm_norm2_w', 'm_ffn_up', 'm_ffn_conv_w', 'm_ffn_conv_b', 'm_ffn_down', 'v_ada_w', 'v_ada_b', 'v_norm1_w', 'v_w_in', 'v_q_a_norm', 'v_w_q_b', 'v_kv_a_norm', 'v_w_kv_b', 'v_q_norm', 'v_k_norm', 'v_pool_w', 'v_pool_scale', 'v_ssd_conv_w', 'v_ssd_conv_b', 'v_ssd_dt_bias', 'v_ssd_a_log', 'v_ssd_d', 'v_ssd_norm_w', 'v_w_branch', 'v_w_out', 'v_norm2_w', 'v_ffn_up', 'v_ffn_conv_w', 'v_ffn_conv_b', 'v_ffn_down']
TWIN_OUTPUTS = ['loss', 'grad_x', 'grad_ada_w', 'grad_ada_b', 'grad_norm1_w', 'grad_w_in', 'grad_q_a_norm', 'grad_w_q_b', 'grad_kv_a_norm', 'grad_w_kv_b', 'grad_q_norm', 'grad_k_norm', 'grad_pool_w', 'grad_pool_scale', 'grad_ssd_conv_w', 'grad_ssd_conv_b', 'grad_ssd_dt_bias', 'grad_ssd_a_log', 'grad_ssd_d', 'grad_ssd_norm_w', 'grad_w_branch', 'grad_w_out', 'grad_norm2_w', 'grad_ffn_up', 'grad_ffn_conv_w', 'grad_ffn_conv_b', 'grad_ffn_down', 'delta_ada_w', 'delta_ada_b', 'delta_norm1_w', 'delta_w_in', 'delta_q_a_norm', 'delta_w_q_b', 'delta_kv_a_norm', 'delta_w_kv_b', 'delta_q_norm', 'delta_k_norm', 'delta_pool_w', 'delta_pool_scale', 'delta_ssd_conv_w', 'delta_ssd_conv_b', 'delta_ssd_dt_bias', 'delta_ssd_a_log', 'delta_ssd_d', 'delta_ssd_norm_w', 'delta_w_branch', 'delta_w_out', 'delta_norm2_w', 'delta_ffn_up', 'delta_ffn_conv_w', 'delta_ffn_conv_b', 'delta_ffn_down', 'new_m_ada_w', 'new_m_ada_b', 'new_m_norm1_w', 'new_m_w_in', 'new_m_q_a_norm', 'new_m_w_q_b', 'new_m_kv_a_norm', 'new_m_w_kv_b', 'new_m_q_norm', 'new_m_k_norm', 'new_m_pool_w', 'new_m_pool_scale', 'new_m_ssd_conv_w', 'new_m_ssd_conv_b', 'new_m_ssd_dt_bias', 'new_m_ssd_a_log', 'new_m_ssd_d', 'new_m_ssd_norm_w', 'new_m_w_branch', 'new_m_w_out', 'new_m_norm2_w', 'new_m_ffn_up', 'new_m_ffn_conv_w', 'new_m_ffn_conv_b', 'new_m_ffn_down', 'new_v_ada_w', 'new_v_ada_b', 'new_v_norm1_w', 'new_v_w_in', 'new_v_q_a_norm', 'new_v_w_q_b', 'new_v_kv_a_norm', 'new_v_w_kv_b', 'new_v_q_norm', 'new_v_k_norm', 'new_v_pool_w', 'new_v_pool_scale', 'new_v_ssd_conv_w', 'new_v_ssd_conv_b', 'new_v_ssd_dt_bias', 'new_v_ssd_a_log', 'new_v_ssd_d', 'new_v_ssd_norm_w', 'new_v_w_branch', 'new_v_w_out', 'new_v_norm2_w', 'new_v_ffn_up', 'new_v_ffn_conv_w', 'new_v_ffn_conv_b', 'new_v_ffn_down']
TWIN_LEAF_KINDS = {'loss': 'loss', 'grad_x': 'grad_x', 'grad_ada_w': 'grad_w', 'grad_ada_b': 'grad_w', 'grad_norm1_w': 'grad_w', 'grad_w_in': 'grad_w', 'grad_q_a_norm': 'grad_w', 'grad_w_q_b': 'grad_w', 'grad_kv_a_norm': 'grad_w', 'grad_w_kv_b': 'grad_w', 'grad_q_norm': 'grad_w', 'grad_k_norm': 'grad_w', 'grad_pool_w': 'grad_w', 'grad_pool_scale': 'grad_w', 'grad_ssd_conv_w': 'grad_w', 'grad_ssd_conv_b': 'grad_w', 'grad_ssd_dt_bias': 'grad_w', 'grad_ssd_a_log': 'grad_w', 'grad_ssd_d': 'grad_w', 'grad_ssd_norm_w': 'grad_w', 'grad_w_branch': 'grad_w', 'grad_w_out': 'grad_w', 'grad_norm2_w': 'grad_w', 'grad_ffn_up': 'grad_w', 'grad_ffn_conv_w': 'grad_w', 'grad_ffn_conv_b': 'grad_w', 'grad_ffn_down': 'grad_w', 'delta_ada_w': 'delta_w', 'delta_ada_b': 'delta_w', 'delta_norm1_w': 'delta_w', 'delta_w_in': 'delta_w', 'delta_q_a_norm': 'delta_w', 'delta_w_q_b': 'delta_w', 'delta_kv_a_norm': 'delta_w', 'delta_w_kv_b': 'delta_w', 'delta_q_norm': 'delta_w', 'delta_k_norm': 'delta_w', 'delta_pool_w': 'delta_w', 'delta_pool_scale': 'delta_w', 'delta_ssd_conv_w': 'delta_w', 'delta_ssd_conv_b': 'delta_w', 'delta_ssd_dt_bias': 'delta_w', 'delta_ssd_a_log': 'delta_w', 'delta_ssd_d': 'delta_w', 'delta_ssd_norm_w': 'delta_w', 'delta_w_branch': 'delta_w', 'delta_w_out': 'delta_w', 'delta_norm2_w': 'delta_w', 'delta_ffn_up': 'delta_w', 'delta_ffn_conv_w': 'delta_w', 'delta_ffn_conv_b': 'delta_w', 'delta_ffn_down': 'delta_w', 'new_m_ada_w': 'new_m', 'new_m_ada_b': 'new_m', 'new_m_norm1_w': 'new_m', 'new_m_w_in': 'new_m', 'new_m_q_a_norm': 'new_m', 'new_m_w_q_b': 'new_m', 'new_m_kv_a_norm': 'new_m', 'new_m_w_kv_b': 'new_m', 'new_m_q_norm': 'new_m', 'new_m_k_norm': 'new_m', 'new_m_pool_w': 'new_m', 'new_m_pool_scale': 'new_m', 'new_m_ssd_conv_w': 'new_m', 'new_m_ssd_conv_b': 'new_m', 'new_m_ssd_dt_bias': 'new_m', 'new_m_ssd_a_log': 'new_m', 'new_m_ssd_d': 'new_m', 'new_m_ssd_norm_w': 'new_m', 'new_m_w_branch': 'new_m', 'new_m_w_out': 'new_m', 'new_m_norm2_w': 'new_m', 'new_m_ffn_up': 'new_m', 'new_m_ffn_conv_w': 'new_m', 'new_m_ffn_conv_b': 'new_m', 'new_m_ffn_down': 'new_m', 'new_v_ada_w': 'new_v', 'new_v_ada_b': 'new_v', 'new_v_norm1_w': 'new_v', 'new_v_w_in': 'new_v', 'new_v_q_a_norm': 'new_v', 'new_v_w_q_b': 'new_v', 'new_v_kv_a_norm': 'new_v', 'new_v_w_kv_b': 'new_v', 'new_v_q_norm': 'new_v', 'new_v_k_norm': 'new_v', 'new_v_pool_w': 'new_v', 'new_v_pool_scale': 'new_v', 'new_v_ssd_conv_w': 'new_v', 'new_v_ssd_conv_b': 'new_v', 'new_v_ssd_dt_bias': 'new_v', 'new_v_ssd_a_log': 'new_v', 'new_v_ssd_d': 'new_v', 'new_v_ssd_norm_w': 'new_v', 'new_v_w_branch': 'new_v', 'new_v_w_out': 'new_v', 'new_v_norm2_w': 'new_v', 'new_v_ffn_up': 'new_v', 'new_v_ffn_conv_w': 'new_v', 'new_v_ffn_conv_b': 'new_v', 'new_v_ffn_down': 'new_v'}


def _forward(args):
    return _fwd_reference(*[args[k] for k in FWD_PARAMS])


def _output_shape():
    out = _jax.eval_shape(lambda: _forward(_fwd_setup_inputs(0)))
    return out.shape, out.dtype

N_MICROBATCH = 1
ADAM_LR = 0.001
ADAM_B1 = 0.9
ADAM_B2 = 0.999
ADAM_EPS = 1e-08
ADAM_WD = 0.01
ADAM_STEP = 10
PER_EXAMPLE_BATCH_AXIS = {'x': 0, 'c': 0, 'positions': 0, 'loss_target': 0}
SHARED_INPUTS = []
_WEIGHT_DTYPES = {'ada_w': _jnp.float32, 'ada_b': _jnp.float32, 'norm1_w': _jnp.float32, 'w_in': _jnp.float32, 'q_a_norm': _jnp.float32, 'w_q_b': _jnp.float32, 'kv_a_norm': _jnp.float32, 'w_kv_b': _jnp.float32, 'q_norm': _jnp.float32, 'k_norm': _jnp.float32, 'pool_w': _jnp.float32, 'pool_scale': _jnp.float32, 'ssd_conv_w': _jnp.float32, 'ssd_conv_b': _jnp.float32, 'ssd_dt_bias': _jnp.float32, 'ssd_a_log': _jnp.float32, 'ssd_d': _jnp.float32, 'ssd_norm_w': _jnp.float32, 'w_branch': _jnp.float32, 'w_out': _jnp.float32, 'norm2_w': _jnp.float32, 'ffn_up': _jnp.float32, 'ffn_conv_w': _jnp.float32, 'ffn_conv_b': _jnp.float32, 'ffn_down': _jnp.float32}
MOMENT_SCALE = {'ada_w': 3.524314e+00, 'ada_b': 7.233912e+00, 'norm1_w': 2.708022e+00, 'w_in': 7.961083e-01, 'q_a_norm': 8.452666e-02, 'w_q_b': 5.764541e-02, 'kv_a_norm': 3.210103e+00, 'w_kv_b': 1.310060e+00, 'q_norm': 1.895009e-01, 'k_norm': 1.880586e-01, 'pool_w': 4.691827e-01, 'pool_scale': 4.151556e+00, 'ssd_conv_w': 1.015410e+00, 'ssd_conv_b': 1.428871e+00, 'ssd_dt_bias': 1.945068e+00, 'ssd_a_log': 5.179517e+00, 'ssd_d': 4.631800e+00, 'ssd_norm_w': 8.174450e+00, 'w_branch': 9.982663e-01, 'w_out': 1.951080e+00, 'norm2_w': 1.587091e+01, 'ffn_up': 1.618452e+00, 'ffn_conv_w': 3.250044e+00, 'ffn_conv_b': 2.254034e+00, 'ffn_down': 1.216371e+00}


def _to_microbatches(a, axis):
    t = _jnp.moveaxis(a, axis, 0)
    t = t.reshape((N_MICROBATCH, t.shape[0] // N_MICROBATCH) + t.shape[1:])
    return _jnp.moveaxis(t, 1, axis + 1)


def setup_inputs(seed: int = 0) -> dict:
    inp = _fwd_setup_inputs(seed)
    key = _jax.random.fold_in(_jax.random.key(seed), 7919)
    shape, _ = _output_shape()
    out = dict(inp)
    out["loss_target"] = _jax.random.normal(_jax.random.fold_in(key, 0), shape, _jnp.float32)
    for i, name in enumerate(TWIN_WEIGHTS):
        w = inp[name].astype(_jnp.float32)
        if MOMENT_SCALE is None:
            s = _jnp.sqrt(_jnp.mean(_jnp.square(w)) + 1e-30)
        else:
            s = MOMENT_SCALE[name]
        km, kv = _jax.random.split(_jax.random.fold_in(key, i + 1))
        out[name] = w
        out["m_" + name] = s * _jax.random.normal(km, w.shape, _jnp.float32)
        out["v_" + name] = (s * s) * _jax.random.uniform(kv, w.shape, _jnp.float32, 0.5, 1.5)
    if N_MICROBATCH > 1:
        for name, axis in PER_EXAMPLE_BATCH_AXIS.items():
            out[name] = _to_microbatches(out[name], axis)
    return {'x': out['x'], 'c': out['c'], 'positions': out['positions'], 'ada_w': out['ada_w'], 'ada_b': out['ada_b'], 'norm1_w': out['norm1_w'], 'w_in': out['w_in'], 'q_a_norm': out['q_a_norm'], 'w_q_b': out['w_q_b'], 'kv_a_norm': out['kv_a_norm'], 'w_kv_b': out['w_kv_b'], 'q_norm': out['q_norm'], 'k_norm': out['k_norm'], 'pool_w': out['pool_w'], 'pool_scale': out['pool_scale'], 'ssd_conv_w': out['ssd_conv_w'], 'ssd_conv_b': out['ssd_conv_b'], 'ssd_dt_bias': out['ssd_dt_bias'], 'ssd_a_log': out['ssd_a_log'], 'ssd_d': out['ssd_d'], 'ssd_norm_w': out['ssd_norm_w'], 'w_branch': out['w_branch'], 'w_out': out['w_out'], 'norm2_w': out['norm2_w'], 'ffn_up': out['ffn_up'], 'ffn_conv_w': out['ffn_conv_w'], 'ffn_conv_b': out['ffn_conv_b'], 'ffn_down': out['ffn_down'], 'loss_target': out['loss_target'], 'm_ada_w': out['m_ada_w'], 'm_ada_b': out['m_ada_b'], 'm_norm1_w': out['m_norm1_w'], 'm_w_in': out['m_w_in'], 'm_q_a_norm': out['m_q_a_norm'], 'm_w_q_b': out['m_w_q_b'], 'm_kv_a_norm': out['m_kv_a_norm'], 'm_w_kv_b': out['m_w_kv_b'], 'm_q_norm': out['m_q_norm'], 'm_k_norm': out['m_k_norm'], 'm_pool_w': out['m_pool_w'], 'm_pool_scale': out['m_pool_scale'], 'm_ssd_conv_w': out['m_ssd_conv_w'], 'm_ssd_conv_b': out['m_ssd_conv_b'], 'm_ssd_dt_bias': out['m_ssd_dt_bias'], 'm_ssd_a_log': out['m_ssd_a_log'], 'm_ssd_d': out['m_ssd_d'], 'm_ssd_norm_w': out['m_ssd_norm_w'], 'm_w_branch': out['m_w_branch'], 'm_w_out': out['m_w_out'], 'm_norm2_w': out['m_norm2_w'], 'm_ffn_up': out['m_ffn_up'], 'm_ffn_conv_w': out['m_ffn_conv_w'], 'm_ffn_conv_b': out['m_ffn_conv_b'], 'm_ffn_down': out['m_ffn_down'], 'v_ada_w': out['v_ada_w'], 'v_ada_b': out['v_ada_b'], 'v_norm1_w': out['v_norm1_w'], 'v_w_in': out['v_w_in'], 'v_q_a_norm': out['v_q_a_norm'], 'v_w_q_b': out['v_w_q_b'], 'v_kv_a_norm': out['v_kv_a_norm'], 'v_w_kv_b': out['v_w_kv_b'], 'v_q_norm': out['v_q_norm'], 'v_k_norm': out['v_k_norm'], 'v_pool_w': out['v_pool_w'], 'v_pool_scale': out['v_pool_scale'], 'v_ssd_conv_w': out['v_ssd_conv_w'], 'v_ssd_conv_b': out['v_ssd_conv_b'], 'v_ssd_dt_bias': out['v_ssd_dt_bias'], 'v_ssd_a_log': out['v_ssd_a_log'], 'v_ssd_d': out['v_ssd_d'], 'v_ssd_norm_w': out['v_ssd_norm_w'], 'v_w_branch': out['v_w_branch'], 'v_w_out': out['v_w_out'], 'v_norm2_w': out['v_norm2_w'], 'v_ffn_up': out['v_ffn_up'], 'v_ffn_conv_w': out['v_ffn_conv_w'], 'v_ffn_conv_b': out['v_ffn_conv_b'], 'v_ffn_down': out['v_ffn_down']}


def _loss(weights, diff, rest, loss_target):
    with _jax.named_scope("forward"):
        args = {**rest, TWIN_DIFF_INPUT: diff, **{k: w.astype(_WEIGHT_DTYPES[k]) for k, w in weights.items()}}
        y = _forward(args)
    with _jax.named_scope("loss_head"):
        err = _jnp.square(y.astype(_jnp.float32) - loss_target)
        return 0.5 * _jnp.sum(_jnp.mean(err, axis=-1)) if err.ndim else 0.5 * err


def _adamw(w, g, m, v):
    m = ADAM_B1 * m + (1.0 - ADAM_B1) * g
    v = ADAM_B2 * v + (1.0 - ADAM_B2) * _jnp.square(g)
    m_hat = m / (1.0 - ADAM_B1 ** ADAM_STEP)
    v_hat = v / (1.0 - ADAM_B2 ** ADAM_STEP)
    delta = -ADAM_LR * (m_hat / (_jnp.sqrt(v_hat) + ADAM_EPS) + ADAM_WD * w)
    return delta, m, v


def reference(x, c, positions, ada_w, ada_b, norm1_w, w_in, q_a_norm, w_q_b, kv_a_norm, w_kv_b, q_norm, k_norm, pool_w, pool_scale, ssd_conv_w, ssd_conv_b, ssd_dt_bias, ssd_a_log, ssd_d, ssd_norm_w, w_branch, w_out, norm2_w, ffn_up, ffn_conv_w, ffn_conv_b, ffn_down, loss_target, m_ada_w, m_ada_b, m_norm1_w, m_w_in, m_q_a_norm, m_w_q_b, m_kv_a_norm, m_w_kv_b, m_q_norm, m_k_norm, m_pool_w, m_pool_scale, m_ssd_conv_w, m_ssd_conv_b, m_ssd_dt_bias, m_ssd_a_log, m_ssd_d, m_ssd_norm_w, m_w_branch, m_w_out, m_norm2_w, m_ffn_up, m_ffn_conv_w, m_ffn_conv_b, m_ffn_down, v_ada_w, v_ada_b, v_norm1_w, v_w_in, v_q_a_norm, v_w_q_b, v_kv_a_norm, v_w_kv_b, v_q_norm, v_k_norm, v_pool_w, v_pool_scale, v_ssd_conv_w, v_ssd_conv_b, v_ssd_dt_bias, v_ssd_a_log, v_ssd_d, v_ssd_norm_w, v_w_branch, v_w_out, v_norm2_w, v_ffn_up, v_ffn_conv_w, v_ffn_conv_b, v_ffn_down):
    given = dict(x=x, c=c, positions=positions, ada_w=ada_w, ada_b=ada_b, norm1_w=norm1_w, w_in=w_in, q_a_norm=q_a_norm, w_q_b=w_q_b, kv_a_norm=kv_a_norm, w_kv_b=w_kv_b, q_norm=q_norm, k_norm=k_norm, pool_w=pool_w, pool_scale=pool_scale, ssd_conv_w=ssd_conv_w, ssd_conv_b=ssd_conv_b, ssd_dt_bias=ssd_dt_bias, ssd_a_log=ssd_a_log, ssd_d=ssd_d, ssd_norm_w=ssd_norm_w, w_branch=w_branch, w_out=w_out, norm2_w=norm2_w, ffn_up=ffn_up, ffn_conv_w=ffn_conv_w, ffn_conv_b=ffn_conv_b, ffn_down=ffn_down, loss_target=loss_target, m_ada_w=m_ada_w, m_ada_b=m_ada_b, m_norm1_w=m_norm1_w, m_w_in=m_w_in, m_q_a_norm=m_q_a_norm, m_w_q_b=m_w_q_b, m_kv_a_norm=m_kv_a_norm, m_w_kv_b=m_w_kv_b, m_q_norm=m_q_norm, m_k_norm=m_k_norm, m_pool_w=m_pool_w, m_pool_scale=m_pool_scale, m_ssd_conv_w=m_ssd_conv_w, m_ssd_conv_b=m_ssd_conv_b, m_ssd_dt_bias=m_ssd_dt_bias, m_ssd_a_log=m_ssd_a_log, m_ssd_d=m_ssd_d, m_ssd_norm_w=m_ssd_norm_w, m_w_branch=m_w_branch, m_w_out=m_w_out, m_norm2_w=m_norm2_w, m_ffn_up=m_ffn_up, m_ffn_conv_w=m_ffn_conv_w, m_ffn_conv_b=m_ffn_conv_b, m_ffn_down=m_ffn_down, v_ada_w=v_ada_w, v_ada_b=v_ada_b, v_norm1_w=v_norm1_w, v_w_in=v_w_in, v_q_a_norm=v_q_a_norm, v_w_q_b=v_w_q_b, v_kv_a_norm=v_kv_a_norm, v_w_kv_b=v_w_kv_b, v_q_norm=v_q_norm, v_k_norm=v_k_norm, v_pool_w=v_pool_w, v_pool_scale=v_pool_scale, v_ssd_conv_w=v_ssd_conv_w, v_ssd_conv_b=v_ssd_conv_b, v_ssd_dt_bias=v_ssd_dt_bias, v_ssd_a_log=v_ssd_a_log, v_ssd_d=v_ssd_d, v_ssd_norm_w=v_ssd_norm_w, v_w_branch=v_w_branch, v_w_out=v_w_out, v_norm2_w=v_norm2_w, v_ffn_up=v_ffn_up, v_ffn_conv_w=v_ffn_conv_w, v_ffn_conv_b=v_ffn_conv_b, v_ffn_down=v_ffn_down)
    weights = {n: given[n] for n in TWIN_WEIGHTS}
    shared = {n: given[n] for n in SHARED_INPUTS}
    per_example = {n: given[n] for n in ['x', 'c', 'positions']}
    grad_fn = _jax.value_and_grad(_loss, argnums=(0, 1))

    def one_microbatch(ex, loss_target):
        ex = dict(ex)
        diff = ex.pop(TWIN_DIFF_INPUT)
        return grad_fn(weights, diff, {**shared, **ex}, loss_target)

    if N_MICROBATCH == 1:
        loss, (grad_w, grad_x) = one_microbatch(per_example, given["loss_target"])
    else:
        def body(carry, xs):
            loss_sum, grad_sum = carry
            l_k, (gw_k, gx_k) = one_microbatch(xs[0], xs[1])
            with _jax.named_scope("update"):
                return (loss_sum + l_k, _jax.tree.map(_jnp.add, grad_sum, gw_k)), gx_k

        init = (_jnp.zeros((), _jnp.float32), _jax.tree.map(_jnp.zeros_like, weights))
        (loss, grad_w), grad_x = _jax.lax.scan(body, init, (per_example, given["loss_target"]))
    with _jax.named_scope("update"):
        delta_w, new_m, new_v = {}, {}, {}
        for n in TWIN_WEIGHTS:
            delta_w[n], new_m[n], new_v[n] = _adamw(weights[n], grad_w[n], given["m_" + n], given["v_" + n])
    return (loss, grad_x, *[grad_w[n] for n in TWIN_WEIGHTS], *[delta_w[n] for n in TWIN_WEIGHTS],
            *[new_m[n] for n in TWIN_WEIGHTS], *[new_v[n] for n in TWIN_WEIGHTS])
```

```python
import functools

import jax
import jax.numpy as jnp
from jax import lax
from jax.experimental import pallas as pl
from jax.experimental.pallas import tpu as pltpu

F32 = jnp.float32
BF16 = jnp.bfloat16
MESH = pl.DeviceIdType.MESH
HI = lax.Precision.HIGHEST

D = 1024
N_HEADS = 8
NOPE, ROPE_DIM = 64, 32
Q_RANK, KV_RANK = 384, 256
POOL_WINDOWS = (2, 4, 8, 16)
SSD_HEADS, SSD_P, SSD_N, SSD_L = 16, 64, 128, 128
SSD_INNER = 1024
SSD_CONV_DIM = 1536
FFN = 2816
EPS = 1e-6
ROPE_THETA = 10000.0
OG, OZ, OX, OP, OQ, OKV, ODT, IN_PAD = 0, 3072, 4096, 5632, 6144, 6528, 6912, 7040
IN_DIM = 6832
ADAM_LR, ADAM_B1, ADAM_B2, ADAM_EPS, ADAM_WD, ADAM_STEP = 0.001, 0.9, 0.999, 1e-08, 0.01, 10

_ARB = pltpu.CompilerParams(dimension_semantics=("arbitrary",))
_PAR = pltpu.CompilerParams(dimension_semantics=("parallel",))


def _pick(n, pref):
    if n <= pref:
        return n
    best = None
    for t in range(128, pref + 1, 128):
        if n % t == 0:
            best = t
    assert best is not None, (n, pref)
    return best


def _sds(shape, dtype=F32):
    return jax.ShapeDtypeStruct(tuple(shape), dtype)


def _iota(shape, dim):
    return lax.broadcasted_iota(jnp.int32, shape, dim)


def _sigmoid(x):
    return 1.0 / (1.0 + jnp.exp(-x))


def _silu(x):
    return x * _sigmoid(x)


def _dsilu(x):
    s = _sigmoid(x)
    return s * (1.0 + x * (1.0 - s))


def _dot(a, b, dims):
    return lax.dot_general(a, b, (dims, ((), ())), preferred_element_type=F32)


_NN, _NT, _TN = ((1,), (0,)), ((1,), (1,)), ((0,), (0,))


def _dot_hi(a, b, dims=_NN):
    return lax.dot_general(a, b, (dims, ((), ())), preferred_element_type=F32, precision=HI)


def _shift_down(x, j):
    n = x.shape[0]
    return jnp.where(_iota(x.shape, 0) >= j, pltpu.roll(x, j, 0), 0.0)


def _shift_up(x, j):
    n = x.shape[0]
    return jnp.where(_iota(x.shape, 0) < n - j, pltpu.roll(x, n - j, 0), 0.0)


def _mm(a, b, mode="nn", out_dtype=F32, tm=512, tn=512, tk=512, res=None, gate=None, name="mm"):
    if mode == "nn":
        (M, K), (K2, N) = a.shape, b.shape
    elif mode == "nt":
        (M, K), (N, K2) = a.shape, b.shape
    else:
        (K, M), (K2, N) = a.shape, b.shape
    assert K == K2, (a.shape, b.shape, mode)
    tm, tn, tk = _pick(M, tm), _pick(N, tn), _pick(K, tk)
    nk = K // tk
    dims = {"nn": _NN, "nt": _NT, "tn": _TN}[mode]
    fused = res is not None

    def body(*refs):
        if fused:
            a_ref, b_ref, r_ref, g_ref, o_ref, raw_ref, acc_ref = refs
        else:
            a_ref, b_ref, o_ref, acc_ref = refs
        k = pl.program_id(2)

        @pl.when(k == 0)
        def _():
            acc_ref[...] = jnp.zeros_like(acc_ref)

        acc_ref[...] += _dot(a_ref[...].astype(BF16), b_ref[...].astype(BF16), dims)

        @pl.when(k == nk - 1)
        def _():
            if fused:
                raw_ref[...] = acc_ref[...]
                o_ref[...] = r_ref[...] + g_ref[...] * acc_ref[...]
            else:
                o_ref[...] = acc_ref[...].astype(out_dtype)

    if mode == "nn":
        a_spec = pl.BlockSpec((tm, tk), lambda i, j, k: (i, k))
        b_spec = pl.BlockSpec((tk, tn), lambda i, j, k: (k, j))
    elif mode == "nt":
        a_spec = pl.BlockSpec((tm, tk), lambda i, j, k: (i, k))
        b_spec = pl.BlockSpec((tn, tk), lambda i, j, k: (j, k))
    else:
        a_spec = pl.BlockSpec((tk, tm), lambda i, j, k: (k, i))
        b_spec = pl.BlockSpec((tk, tn), lambda i, j, k: (k, j))
    o_spec = pl.BlockSpec((tm, tn), lambda i, j, k: (i, j))
    in_specs, args = [a_spec, b_spec], [a, b]
    out_shape, out_specs = _sds((M, N), out_dtype), o_spec
    if fused:
        in_specs += [o_spec, pl.BlockSpec((1, tn), lambda i, j, k: (0, j))]
        args += [res, gate]
        out_shape, out_specs = (_sds((M, N)), _sds((M, N))), (o_spec, o_spec)
    return pl.pallas_call(
        body, grid=(M // tm, N // tn, nk), in_specs=in_specs, out_specs=out_specs, out_shape=out_shape,
        scratch_shapes=[pltpu.VMEM((tm, tn), F32)], name=name,
        compiler_params=pltpu.CompilerParams(dimension_semantics=("parallel", "parallel", "arbitrary")),
    )(*args)


def _row_spec(tm, n):
    return pl.BlockSpec((tm, n), lambda i: (i, 0))


def _vec_spec(n, rows=1):
    return pl.BlockSpec((rows, n), lambda i: (0, 0))


def _ln_mod(x, nw, sc, sh, name, tm=256):
    S = x.shape[0]

    def body(x_ref, nw_ref, sc_ref, sh_ref, o_ref):
        xv = x_ref[...]
        r = lax.rsqrt(jnp.mean(xv * xv, -1, keepdims=True) + EPS)
        o_ref[...] = ((xv * r * nw_ref[...]) * (1.0 + sc_ref[...]) + sh_ref[...]).astype(BF16)

    return pl.pallas_call(
        body, grid=(S // tm,), in_specs=[_row_spec(tm, D)] + [_vec_spec(D)] * 3, out_specs=_row_spec(tm, D),
        out_shape=_sds((S, D), BF16), name=name, compiler_params=_PAR)(x, nw, sc, sh)


def _ln_mod_bwd(x, dh, dres, nw, sc, name, tm=256):
    S = x.shape[0]

    def body(x_ref, dh_ref, dres_ref, nw_ref, sc_ref, dx_ref, st_ref):
        @pl.when(pl.program_id(0) == 0)
        def _():
            st_ref[...] = jnp.zeros_like(st_ref)

        xv, dhv, nwv = x_ref[...], dh_ref[...], nw_ref[...]
        r = lax.rsqrt(jnp.mean(xv * xv, -1, keepdims=True) + EPS)
        xhat = xv * r
        dn = dhv * (1.0 + sc_ref[...])
        g = dn * nwv
        dx_ref[...] = dres_ref[...] + r * (g - xhat * jnp.mean(g * xhat, -1, keepdims=True))
        st_ref[0:1, :] += jnp.sum(dhv, 0, keepdims=True)
        st_ref[1:2, :] += jnp.sum(dhv * (xhat * nwv), 0, keepdims=True)
        st_ref[2:3, :] += jnp.sum(dn * xhat, 0, keepdims=True)

    return pl.pallas_call(
        body, grid=(S // tm,), in_specs=[_row_spec(tm, D)] * 3 + [_vec_spec(D)] * 2,
        out_specs=(_row_spec(tm, D), _vec_spec(D, 8)), out_shape=(_sds((S, D)), _sds((8, D))),
        name=name, compiler_params=_ARB)(x, dh, dres, nw, sc)


def _gate_bwd(dx, out, g, name, tm=256):
    S = dx.shape[0]

    def body(dx_ref, o_ref, g_ref, dz_ref, dg_ref):
        @pl.when(pl.program_id(0) == 0)
        def _():
            dg_ref[...] = jnp.zeros_like(dg_ref)

        dxv = dx_ref[...]
        dz_ref[...] = (dxv * g_ref[...]).astype(BF16)
        dg_ref[0:1, :] += jnp.sum(dxv * o_ref[...], 0, keepdims=True)

    return pl.pallas_call(
        body, grid=(S // tm,), in_specs=[_row_spec(tm, D)] * 2 + [_vec_spec(D)],
        out_specs=(_row_spec(tm, D), _vec_spec(D, 8)), out_shape=(_sds((S, D), BF16), _sds((8, D))),
        name=name, compiler_params=_ARB)(dx, out, g)


def _loss_grad(y, t, tm=256):
    S = y.shape[0]

    def body(y_ref, t_ref, dy_ref, l_ref):
        @pl.when(pl.program_id(0) == 0)
        def _():
            l_ref[...] = jnp.zeros_like(l_ref)

        e = y_ref[...] - t_ref[...]
        dy_ref[...] = e * (1.0 / D)
        l_ref[...] += 0.5 * jnp.sum(jnp.mean(e * e, -1, keepdims=True), 0, keepdims=True)

    return pl.pallas_call(
        body, grid=(S // tm,), in_specs=[_row_spec(tm, D)] * 2,
        out_specs=(_row_spec(tm, D), pl.BlockSpec((8, 128), lambda i: (0, 0))),
        out_shape=(_sds((S, D)), _sds((8, 128))), name="loss_grad", compiler_params=_ARB)(y, t)


def _conv(x, w, b):
    K = w.shape[0]
    acc = x * w[K - 1:K, :] + b
    for j in range(1, K):
        acc = acc + _shift_down(x, j) * w[K - 1 - j:K - j, :]
    return acc


def _conv_bwd(x, w, dc):
    K = w.shape[0]
    dx = dc * w[K - 1:K, :]
    dws = [jnp.sum(dc * x, 0, keepdims=True)]
    for j in range(1, K):
        dx = dx + _shift_up(dc, j) * w[K - 1 - j:K - j, :]
        dws.append(jnp.sum(dc * _shift_down(x, j), 0, keepdims=True))
    return dx, dws[::-1], jnp.sum(dc, 0, keepdims=True)


def _col_spec(S, tc, off=0):
    return pl.BlockSpec((S, tc), lambda j: (0, j + off))


def _ssd_pre(proj, cw, cb, tc=256):
    S, n = proj.shape[0], SSD_CONV_DIM

    def body(x_ref, w_ref, b_ref, o_ref):
        o_ref[...] = _silu(_conv(x_ref[...], w_ref[...], b_ref[...]))

    return pl.pallas_call(
        body, grid=(n // tc,),
        in_specs=[_col_spec(S, tc, OX // tc), pl.BlockSpec((4, tc), lambda j: (0, j)), pl.BlockSpec((1, tc), lambda j: (0, j))],
        out_specs=_col_spec(S, tc), out_shape=_sds((S, n)), name="ssd_pre", compiler_params=_PAR)(proj, cw, cb)


def _ssd_pre_bwd(proj, dxc, cw, cb, tc=256):
    S, n = proj.shape[0], SSD_CONV_DIM

    def body(x_ref, d_ref, w_ref, b_ref, dx_ref, dw_ref, db_ref):
        xv, wv = x_ref[...], w_ref[...]
        dc = d_ref[...] * _dsilu(_conv(xv, wv, b_ref[...]))
        dx, dw, db = _conv_bwd(xv, wv, dc)
        dx_ref[...] = dx.astype(BF16)
        for k, row in enumerate(dw):
            dw_ref[k:k + 1, :] = row
        db_ref[...] = db

    wspec, bspec = pl.BlockSpec((4, tc), lambda j: (0, j)), pl.BlockSpec((1, tc), lambda j: (0, j))
    return pl.pallas_call(
        body, grid=(n // tc,), in_specs=[_col_spec(S, tc, OX // tc), _col_spec(S, tc), wspec, bspec],
        out_specs=(_col_spec(S, tc), wspec, bspec), out_shape=(_sds((S, n), BF16), _sds((4, n)), _sds((1, n))),
        name="ssd_pre_bwd", compiler_params=_PAR)(proj, dxc, cw, cb)


def _ffn_act(up, cw, cb, tc=256):
    S, nb = up.shape[0], FFN // tc

    def body(g_ref, v_ref, wg_ref, wv_ref, bg_ref, bv_ref, o_ref):
        o_ref[...] = (_silu(_conv(g_ref[...], wg_ref[...], bg_ref[...])) * _conv(v_ref[...], wv_ref[...], bv_ref[...])).astype(BF16)

    def wspec(off):
        return pl.BlockSpec((3, tc), lambda j: (0, j + off))

    def bspec(off):
        return pl.BlockSpec((1, tc), lambda j: (0, j + off))

    return pl.pallas_call(
        body, grid=(nb,), in_specs=[_col_spec(S, tc), _col_spec(S, tc, nb), wspec(0), wspec(nb), bspec(0), bspec(nb)],
        out_specs=_col_spec(S, tc), out_shape=_sds((S, FFN), BF16), name="ffn_act", compiler_params=_PAR)(up, up, cw, cw, cb, cb)


def _ffn_act_bwd(up, dact, cw, cb, tc=256):
    S, nb = up.shape[0], FFN // tc

    def body(g_ref, v_ref, d_ref, wg_ref, wv_ref, bg_ref, bv_ref, dg_ref, dv_ref, dwg_ref, dwv_ref, dbg_ref, dbv_ref):
        gv, vv, wg, wv, da = g_ref[...], v_ref[...], wg_ref[...], wv_ref[...], d_ref[...]
        cg, cv = _conv(gv, wg, bg_ref[...]), _conv(vv, wv, bv_ref[...])
        dxg, dwg, dbg = _conv_bwd(gv, wg, da * cv * _dsilu(cg))
        dxv, dwv, dbv = _conv_bwd(vv, wv, da * _silu(cg))
        dg_ref[...], dv_ref[...] = dxg.astype(BF16), dxv.astype(BF16)
        for k in range(3):
            dwg_ref[k:k + 1, :], dwv_ref[k:k + 1, :] = dwg[k], dwv[k]
        dbg_ref[...], dbv_ref[...] = dbg, dbv

    def wspec(off):
        return pl.BlockSpec((3, tc), lambda j: (0, j + off))

    def bspec(off):
        return pl.BlockSpec((1, tc), lambda j: (0, j + off))

    cs = _col_spec(S, tc)
    return pl.pallas_call(
        body, grid=(nb,), in_specs=[cs, _col_spec(S, tc, nb), cs, wspec(0), wspec(nb), bspec(0), bspec(nb)],
        out_specs=(cs, cs, wspec(0), wspec(0), bspec(0), bspec(0)),
        out_shape=(_sds((S, FFN), BF16), _sds((S, FFN), BF16), _sds((3, FFN)), _sds((3, FFN)), _sds((1, FFN)), _sds((1, FFN))),
        name="ffn_act_bwd", compiler_params=_PAR)(up, up, dact, cw, cw, cb, cb)


def _window_sum(x, w, up=False):
    shift = _shift_up if up else _shift_down
    j = 1
    while j < w:
        x = x + shift(x, j)
        j *= 2
    return x


def _pool_fwd(proj, pool_w, pool_scale):
    S = proj.shape[0]

    def body(u_ref, w_ref, s_ref, o_ref):
        cnt_row = (_iota((S, 128), 0) + 1).astype(F32)
        for g, w in enumerate(POOL_WINDOWS):
            sl = slice(g * 128, (g + 1) * 128)
            u = u_ref[:, sl]
            pooled = _window_sum(u, w) / jnp.minimum(cnt_row, float(w)) - u
            mixed = _dot(pooled.astype(BF16), w_ref[g].astype(BF16), _NN)
            o_ref[:, sl] = (mixed * s_ref[:, sl]).astype(BF16)

    return pl.pallas_call(
        body, grid=(1,),
        in_specs=[pl.BlockSpec((S, 512), lambda i: (0, OP // 512)), pl.BlockSpec((4, 128, 128), lambda i: (0, 0, 0)), _vec_spec(512)],
        out_specs=pl.BlockSpec((S, 512), lambda i: (0, 0)), out_shape=_sds((S, 512), BF16), name="pool_fwd",
        compiler_params=_ARB)(proj, pool_w, pool_scale)


def _pool_bwd(proj, dob, pool_w, pool_scale):
    S = proj.shape[0]

    def body(u_ref, d_ref, w_ref, s_ref, du_ref, dw_ref, ds_ref):
        cnt_row = (_iota((S, 128), 0) + 1).astype(F32)
        for g, w in enumerate(POOL_WINDOWS):
            sl = slice(g * 128, (g + 1) * 128)
            u, dv, wv = u_ref[:, sl], d_ref[:, sl], w_ref[g].astype(BF16)
            cnt = jnp.minimum(cnt_row, float(w))
            pooled = (_window_sum(u, w) / cnt - u).astype(BF16)
            ds_ref[:, sl] = jnp.sum(dv * _dot(pooled, wv, _NN), 0, keepdims=True)
            dmix = (dv * s_ref[:, sl]).astype(BF16)
            dw_ref[g] = _dot(pooled, dmix, _TN)
            dp = _dot(dmix, wv, _NT)
            du_ref[:, sl] = (_window_sum(dp / cnt, w, up=True) - dp).astype(BF16)

    blk = pl.BlockSpec((S, 512), lambda i: (0, 0))
    wspec = pl.BlockSpec((4, 128, 128), lambda i: (0, 0, 0))
    return pl.pallas_call(
        body, grid=(1,), in_specs=[pl.BlockSpec((S, 512), lambda i: (0, OP // 512)), blk, wspec, _vec_spec(512)],
        out_specs=(blk, wspec, _vec_spec(512)), out_shape=(_sds((S, 512), BF16), _sds((4, 128, 128)), _sds((1, 512))),
        name="pool_bwd", compiler_params=_ARB)(proj, dob, pool_w, pool_scale)


def _merge_fwd(oa, ob, oc, proj, wa, wb, wc, tm=256):
    S = oa.shape[0]

    def body(oa_ref, ob_ref, oc_ref, gl_ref, wa_ref, wb_ref, wc_ref, o_ref):
        acc = _sigmoid(gl_ref[:, 0:D]) * _dot(oa_ref[...], wa_ref[...], _NN)
        acc += _sigmoid(gl_ref[:, D:2 * D]) * _dot(ob_ref[...], wb_ref[...], _NN)
        acc += _sigmoid(gl_ref[:, 2 * D:3 * D]) * _dot(oc_ref[...], wc_ref[...], _NN)
        o_ref[...] = acc.astype(BF16)

    full = lambda r: pl.BlockSpec((r, D), lambda i: (0, 0))
    return pl.pallas_call(
        body, grid=(S // tm,),
        in_specs=[_row_spec(tm, D), _row_spec(tm, 512), _row_spec(tm, D), _row_spec(tm, 3 * D), full(D), full(512), full(D)],
        out_specs=_row_spec(tm, D), out_shape=_sds((S, D), BF16), name="merge_fwd", compiler_params=_PAR)(oa, ob, oc, proj, wa, wb, wc)


def _merge_bwd(dm, oa, ob, oc, proj, wa, wb, wc, tm=256):
    S = oa.shape[0]

    def body(dm_ref, oa_ref, ob_ref, oc_ref, gl_ref, wa_ref, wb_ref, wc_ref, dya_ref, dyb_ref, dyc_ref, dgl_ref, doa_ref, dob_ref, doc_ref):
        dmv = dm_ref[...]
        for i, (o_ref, w_ref, dy_ref, do_ref) in enumerate(
                ((oa_ref, wa_ref, dya_ref, doa_ref), (ob_ref, wb_ref, dyb_ref, dob_ref), (oc_ref, wc_ref, dyc_ref, doc_ref))):
            gt = _sigmoid(gl_ref[:, i * D:(i + 1) * D])
            wv = w_ref[...]
            yv = _dot(o_ref[...], wv, _NN)
            dy = (dmv * gt).astype(BF16)
            dy_ref[...] = dy
            dgl_ref[:, i * D:(i + 1) * D] = (dmv * yv * gt * (1.0 - gt)).astype(BF16)
            do_ref[...] = _dot(dy, wv, _NT)

    full = lambda r: pl.BlockSpec((r, D), lambda i: (0, 0))
    rs = _row_spec
    return pl.pallas_call(
        body, grid=(S // tm,),
        in_specs=[rs(tm, D), rs(tm, D), rs(tm, 512), rs(tm, D), rs(tm, 3 * D), full(D), full(512), full(D)],
        out_specs=(rs(tm, D), rs(tm, D), rs(tm, D), rs(tm, 3 * D), rs(tm, D), rs(tm, 512), rs(tm, D)),
        out_shape=(_sds((S, D), BF16), _sds((S, D), BF16), _sds((S, D), BF16), _sds((S, 3 * D), BF16), _sds((S, D)), _sds((S, 512)), _sds((S, D))),
        name="merge_bwd", compiler_params=_PAR)(dm, oa, ob, oc, proj, wa, wb, wc)


def _rope_tab(posb, invf, tm=256):
    S = posb.shape[0]

    def body(p_ref, f_ref, c_ref, s1_ref, s2_ref):
        ang = p_ref[...] * f_ref[...]
        lane = _iota(ang.shape, 1)
        cs, sn = jnp.cos(ang), jnp.sin(ang)
        c_ref[...] = jnp.where(lane < NOPE, 1.0, cs)
        s1_ref[...] = jnp.where((lane >= 64) & (lane < 80), -sn, 0.0)
        s2_ref[...] = jnp.where((lane >= 80) & (lane < 96), sn, 0.0)

    rs = _row_spec(tm, 128)
    return pl.pallas_call(body, grid=(S // tm,), in_specs=[rs, _vec_spec(128)], out_specs=(rs, rs, rs),
                          out_shape=(_sds((S, 128)),) * 3, name="rope_tab", compiler_params=_PAR)(posb, invf)


def _rope(u, C, S1, S2):
    return u * C + pltpu.roll(u, 112, 1) * S1 + pltpu.roll(u, 16, 1) * S2


def _rope_t(dy, C, S1, S2):
    return dy * C + pltpu.roll(dy * S1, 16, 1) + pltpu.roll(dy * S2, 112, 1)


def _seg_sum(v, mask):
    return jnp.sum(jnp.where(mask, v, 0.0), -1, keepdims=True)


def _mla_latents(pq_ref, pkv_ref, wqb_ref, wkvb_ref, qan_ref, kvan_ref):
    ql, kvl = pq_ref[...], pkv_ref[...]
    ckv, kr = kvl[:, 0:KV_RANK], kvl[:, KV_RANK:KV_RANK + 128]
    rq = lax.rsqrt(jnp.mean(ql * ql, -1, keepdims=True) + EPS)
    rkv = lax.rsqrt(jnp.mean(ckv * ckv, -1, keepdims=True) + EPS)
    nq = (ql * rq * qan_ref[...]).astype(BF16)
    nkv = (ckv * rkv * kvan_ref[...]).astype(BF16)
    return ql, ckv, kr, rq, rkv, nq, nkv, _dot(nq, wqb_ref[...], _NN), _dot(nkv, wkvb_ref[...], _NN)


def _mla_specs(tm):
    full = lambda r, n: pl.BlockSpec((r, n), lambda i: (0, 0))
    return ([pl.BlockSpec((tm, 384), lambda i: (i, OQ // 384)), pl.BlockSpec((tm, 384), lambda i: (i, OKV // 384))],
            [full(Q_RANK, D), full(KV_RANK, D), _vec_spec(Q_RANK), _vec_spec(KV_RANK), _vec_spec(128), _vec_spec(128)]
            + [_row_spec(tm, 128)] * 3)


def _mla_prep(proj, wqb, wkvb, qan, kvan, wq, wk, rope, tm=256):
    S = proj.shape[0]

    def body(pq_ref, pkv_ref, wqb_ref, wkvb_ref, qan_ref, kvan_ref, wq_ref, wk_ref, c_ref, s1_ref, s2_ref, qf_ref, kf_ref, ve_ref):
        _, _, kr, _, _, _, _, q, kv = _mla_latents(pq_ref, pkv_ref, wqb_ref, wkvb_ref, qan_ref, kvan_ref)
        C, S1, S2, wqv, wkv = c_ref[...], s1_ref[...], s2_ref[...], wq_ref[...], wk_ref[...]
        lane = _iota((tm, 128), 1)
        mn, mr = lane < 64, (lane >= 64) & (lane < 96)
        rrk = lax.rsqrt(_seg_sum(kr * kr, mr) / ROPE_DIM + EPS)
        ykr = _rope(jnp.where(mr, kr * rrk * wkv, 0.0), C, S1, S2)
        for h in range(N_HEADS):
            sl = slice(h * 128, (h + 1) * 128)
            t = q[:, sl]
            rn = lax.rsqrt(_seg_sum(t * t, mn) / NOPE + EPS)
            rr = lax.rsqrt(_seg_sum(t * t, mr) / ROPE_DIM + EPS)
            qf_ref[:, sl] = _rope(t * jnp.where(mn, rn, jnp.where(mr, rr, 0.0)) * wqv, C, S1, S2).astype(BF16)
            t = kv[:, sl]
            rn = lax.rsqrt(_seg_sum(t * t, mn) / NOPE + EPS)
            kf_ref[:, sl] = (jnp.where(mn, t * rn * wkv, 0.0) + ykr).astype(BF16)
            ve_ref[:, sl] = jnp.where(mn, 0.0, t).astype(BF16)

    pspecs, wspecs = _mla_specs(tm)
    rs = _row_spec(tm, D)
    return pl.pallas_call(body, grid=(S // tm,), in_specs=pspecs + wspecs, out_specs=(rs, rs, rs),
                          out_shape=(_sds((S, D), BF16),) * 3, name="mla_prep", compiler_params=_PAR)(
        proj, proj, wqb, wkvb, qan, kvan, wq, wk, *rope)


def _mla_prep_bwd(proj, dqf, dkf, dve, wqb, wkvb, qan, kvan, wq, wk, rope, tm=256):
    S = proj.shape[0]

    def body(pq_ref, pkv_ref, wqb_ref, wkvb_ref, qan_ref, kvan_ref, wq_ref, wk_ref, c_ref, s1_ref, s2_ref,
             dqf_ref, dkf_ref, dve_ref, dqs_ref, dkvs_ref, dwqb_ref, dwkvb_ref, st_ref, dq_scr, dkv_scr):
        @pl.when(pl.program_id(0) == 0)
        def _():
            dwqb_ref[...] = jnp.zeros_like(dwqb_ref)
            dwkvb_ref[...] = jnp.zeros_like(dwkvb_ref)
            st_ref[...] = jnp.zeros_like(st_ref)

        ql, ckv, kr, rq, rkv, nq, nkv, q, kv = _mla_latents(pq_ref, pkv_ref, wqb_ref, wkvb_ref, qan_ref, kvan_ref)
        C, S1, S2, wqv, wkv = c_ref[...], s1_ref[...], s2_ref[...], wq_ref[...], wk_ref[...]
        lane = _iota((tm, 128), 1)
        mn, mr = lane < 64, (lane >= 64) & (lane < 96)
        dwq = jnp.zeros((1, 128), F32)
        dwk = jnp.zeros((1, 128), F32)
        dykr = jnp.zeros((tm, 128), F32)
        for h in range(N_HEADS):
            sl = slice(h * 128, (h + 1) * 128)
            t = q[:, sl]
            rn = lax.rsqrt(_seg_sum(t * t, mn) / NOPE + EPS)
            rr = lax.rsqrt(_seg_sum(t * t, mr) / ROPE_DIM + EPS)
            scale = jnp.where(mn, rn, jnp.where(mr, rr, 0.0))
            that = t * scale
            du = _rope_t(dqf_ref[:, sl], C, S1, S2)
            dwq += jnp.sum(du * that, 0, keepdims=True)
            g = du * wqv
            gt = g * that
            dq_scr[:, sl] = scale * (g - that * jnp.where(mn, _seg_sum(gt, mn) / NOPE, _seg_sum(gt, mr) / ROPE_DIM))
            t = kv[:, sl]
            rn = lax.rsqrt(_seg_sum(t * t, mn) / NOPE + EPS)
            that = jnp.where(mn, t * rn, 0.0)
            dkf = dkf_ref[:, sl]
            dykr += jnp.where(mr, dkf, 0.0)
            dkn = jnp.where(mn, dkf, 0.0)
            dwk += jnp.sum(dkn * that, 0, keepdims=True)
            g = dkn * wkv
            dkv_scr[:, sl] = jnp.where(mn, rn * (g - that * (jnp.sum(g * that, -1, keepdims=True) / NOPE)), dve_ref[:, sl])
        rrk = lax.rsqrt(_seg_sum(kr * kr, mr) / ROPE_DIM + EPS)
        that = jnp.where(mr, kr * rrk, 0.0)
        dukr = jnp.where(mr, _rope_t(dykr, C, S1, S2), 0.0)
        dwk += jnp.sum(dukr * that, 0, keepdims=True)
        g = dukr * wkv
        dkr = rrk * (g - that * (jnp.sum(g * that, -1, keepdims=True) / ROPE_DIM))
        dqv, dkvv = dq_scr[...].astype(BF16), dkv_scr[...].astype(BF16)
        dnq = _dot(dqv, wqb_ref[...], _NT)
        dwqb_ref[...] += _dot(nq, dqv, _TN)
        dnkv = _dot(dkvv, wkvb_ref[...], _NT)
        dwkvb_ref[...] += _dot(nkv, dkvv, _TN)
        xhat = ql * rq
        st_ref[0:1, 0:Q_RANK] += jnp.sum(dnq * xhat, 0, keepdims=True)
        g = dnq * qan_ref[...]
        dqs_ref[...] = (rq * (g - xhat * jnp.mean(g * xhat, -1, keepdims=True))).astype(BF16)
        xhat = ckv * rkv
        st_ref[1:2, 0:KV_RANK] += jnp.sum(dnkv * xhat, 0, keepdims=True)
        g = dnkv * kvan_ref[...]
        dkvs_ref[:, 0:KV_RANK] = (rkv * (g - xhat * jnp.mean(g * xhat, -1, keepdims=True))).astype(BF16)
        dkvs_ref[:, KV_RANK:KV_RANK + 128] = dkr.astype(BF16)
        st_ref[2:3, 0:128] += dwq
        st_ref[3:4, 0:128] += dwk

    pspecs, wspecs = _mla_specs(tm)
    rs = _row_spec(tm, D)
    full = lambda r, n: pl.BlockSpec((r, n), lambda i: (0, 0))
    return pl.pallas_call(
        body, grid=(S // tm,), in_specs=pspecs + wspecs + [rs, rs, rs],
        out_specs=(_row_spec(tm, 384), _row_spec(tm, 384), full(Q_RANK, D), full(KV_RANK, D), full(8, D)),
        out_shape=(_sds((S, 384), BF16), _sds((S, 384), BF16), _sds((Q_RANK, D)), _sds((KV_RANK, D)), _sds((8, D))),
        scratch_shapes=[pltpu.VMEM((tm, D), F32), pltpu.VMEM((tm, D), F32)], name="mla_prep_bwd", compiler_params=_ARB)(
        proj, proj, wqb, wkvb, qan, kvan, wq, wk, *rope, dqf, dkf, dve)


_ATT_SCALE = (NOPE + ROPE_DIM) ** -0.5


def _att_probs(q, k, i, tq):
    n = k.shape[0]
    s = _dot(q, k, _NT) * _ATT_SCALE
    s = jnp.where(_iota((tq, n), 1) <= i * tq + _iota((tq, n), 0), s, -1e30)
    p = jnp.exp(s - jnp.max(s, -1, keepdims=True))
    return p / jnp.sum(p, -1, keepdims=True)


def _attn_fwd(qf, kf, ve, tq=256):
    S = qf.shape[0]

    def body(q_ref, k_ref, v_ref, o_ref):
        for i in range(S // tq):
            n, rows = (i + 1) * tq, slice(i * tq, (i + 1) * tq)
            p = _att_probs(q_ref[rows, :], k_ref[0:n, :], i, tq)
            o_ref[rows, :] = _dot(p.astype(BF16), v_ref[0:n, :], _NN).astype(BF16)

    hs = pl.BlockSpec((S, 128), lambda h: (0, h))
    return pl.pallas_call(body, grid=(N_HEADS,), in_specs=[hs, hs, hs], out_specs=hs, out_shape=_sds((S, D), BF16),
                          name="attn_fwd", compiler_params=_PAR)(qf, kf, ve)


def _attn_bwd(qf, kf, ve, do, tq=256):
    S = qf.shape[0]

    def body(q_ref, k_ref, v_ref, do_ref, dq_ref, dk_ref, dv_ref):
        dk_ref[...] = jnp.zeros_like(dk_ref)
        dv_ref[...] = jnp.zeros_like(dv_ref)
        for i in range(S // tq):
            n, rows = (i + 1) * tq, slice(i * tq, (i + 1) * tq)
            q, k = q_ref[rows, :], k_ref[0:n, :]
            p = _att_probs(q, k, i, tq)
            dob = do_ref[rows, :].astype(BF16)
            dv_ref[0:n, :] += _dot(p.astype(BF16), dob, _TN)
            dp = _dot(dob, v_ref[0:n, :], _NT)
            ds = (p * (dp - jnp.sum(dp * p, -1, keepdims=True)) * _ATT_SCALE).astype(BF16)
            dq_ref[rows, :] = _dot(ds, k, _NN)
            dk_ref[0:n, :] += _dot(ds, q, _TN)

    hs = pl.BlockSpec((S, 128), lambda h: (0, h))
    return pl.pallas_call(body, grid=(N_HEADS,), in_specs=[hs, hs, hs, hs], out_specs=(hs, hs, hs),
                          out_shape=(_sds((S, D)),) * 3, name="attn_bwd", compiler_params=_PAR)(qf, kf, ve, do)


def _softplus(x):
    return jnp.maximum(x, 0.0) + jnp.log1p(jnp.exp(-jnp.abs(x)))


def _ssd_chunk(xc_ref, dtr_ref, dtb_ref, al_ref, e_ref):
    L = SSD_L
    a = -jnp.exp(al_ref[...])
    dtp = _softplus(dtr_ref[...] + dtb_ref[...])
    causal = _iota((L, L), 1) <= _iota((L, L), 0)
    cs = _dot_hi(causal.astype(F32), dtp * a)
    E = e_ref[...]
    dtx, csx = _dot_hi(dtp, E), _dot_hi(cs, E)
    X = xc_ref[:, 0:SSD_INNER]
    Xd = X * dtx
    dec_out = jnp.exp(csx)
    dec_st = jnp.exp(csx[L - 1:L, :] - csx)
    return a, dtp, causal, cs, cs.T, dtx, X, Xd, dec_out, dec_st


def _ssd_decay(causal, cs, cs_row, h):
    diff = cs[:, h:h + 1] - cs_row[h:h + 1, :]
    return jnp.where(causal, jnp.exp(jnp.where(causal, diff, 0.0)), 0.0)


def _ssd_groups(xc_ref, g):
    b0, c0 = SSD_INNER + g * SSD_N, SSD_INNER + 2 * SSD_N + g * SSD_N
    return xc_ref[:, b0:b0 + SSD_N].astype(BF16), xc_ref[:, c0:c0 + SSD_N].astype(BF16)


def _pair_decay(cs, pair):
    L = SSD_L
    return jnp.where(_iota((128, 128), 0) < 64, jnp.exp(cs[L - 1:L, 2 * pair:2 * pair + 1]), jnp.exp(cs[L - 1:L, 2 * pair + 1:2 * pair + 2]))


def _ssd_in_specs(nc, rev):
    idx = (lambda c: nc - 1 - c) if rev else (lambda c: c)
    return [pl.BlockSpec((SSD_L, SSD_CONV_DIM), lambda c: (idx(c), 0)), pl.BlockSpec((SSD_L, 128), lambda c: (idx(c), ODT // 128)),
            _vec_spec(128), _vec_spec(128), _vec_spec(SSD_INNER), pl.BlockSpec((128, SSD_INNER), lambda c: (0, 0))]


def _ssd_core(xc, proj, dtb, alog, dskip, E):
    S = xc.shape[0]
    nc = S // SSD_L

    def body(xc_ref, dtr_ref, dtb_ref, al_ref, dx_ref, e_ref, y_ref, hp_ref, h_scr):
        @pl.when(pl.program_id(0) == 0)
        def _():
            h_scr[...] = jnp.zeros_like(h_scr)

        hp_ref[0] = h_scr[...]
        _, _, causal, cs, cs_row, _, X, Xd, dec_out, dec_st = _ssd_chunk(xc_ref, dtr_ref, dtb_ref, al_ref, e_ref)
        Xs = Xd * dec_st
        lane = _iota((SSD_L, 128), 1)
        for g in range(2):
            Bg, Cg = _ssd_groups(xc_ref, g)
            CB = _dot(Cg, Bg, _NT)
            for pr in range(4):
                pair = g * 4 + pr
                psl = slice(pair * 128, (pair + 1) * 128)
                Xdp = Xd[:, psl].astype(BF16)
                r0 = _dot((CB * _ssd_decay(causal, cs, cs_row, 2 * pair)).astype(BF16), Xdp, _NN)
                r1 = _dot((CB * _ssd_decay(causal, cs, cs_row, 2 * pair + 1)).astype(BF16), Xdp, _NN)
                Hp = h_scr[psl, :]
                W = _dot(Cg, Hp.astype(BF16), _NT)
                y_ref[:, psl] = jnp.where(lane < 64, r0, r1) + W * dec_out[:, psl] + X[:, psl] * dx_ref[:, psl]
                h_scr[psl, :] = Hp * _pair_decay(cs, pair) + _dot(Xs[:, psl].astype(BF16), Bg, _TN)

    return pl.pallas_call(
        body, grid=(nc,), in_specs=_ssd_in_specs(nc, False),
        out_specs=(pl.BlockSpec((SSD_L, SSD_INNER), lambda c: (c, 0)), pl.BlockSpec((1, SSD_INNER, SSD_N), lambda c: (c, 0, 0))),
        out_shape=(_sds((S, SSD_INNER)), _sds((nc, SSD_INNER, SSD_N))), scratch_shapes=[pltpu.VMEM((SSD_INNER, SSD_N), F32)],
        name="ssd_core", compiler_params=_ARB)(xc, proj, dtb, alog, dskip, E)


def _ssd_core_bwd(xc, proj, hprev, dy, dtb, alog, dskip, E, ET):
    S = xc.shape[0]
    nc = S // SSD_L
    L = SSD_L

    def body(xc_ref, dtr_ref, dtb_ref, al_ref, dx_ref, e_ref, et_ref, hp_ref, dy_ref, dxc_ref, ddt_ref, st_ref, dh_scr, acc_scr):
        step = pl.program_id(0)

        @pl.when(step == 0)
        def _():
            dh_scr[...] = jnp.zeros_like(dh_scr)
            acc_scr[...] = jnp.zeros_like(acc_scr)
            st_ref[...] = jnp.zeros_like(st_ref)

        a, dtp, causal, cs, cs_row, dtx, X, Xd, dec_out, dec_st = _ssd_chunk(xc_ref, dtr_ref, dtb_ref, al_ref, e_ref)
        lane = _iota((L, 128), 1)
        sub = _iota((128, L), 0)
        dcs_col = jnp.zeros((L, 128), F32)
        dcs_row = jnp.zeros((128, L), F32)
        dcs_last = jnp.zeros((1, 128), F32)
        dcsx, ddtx, dlastx = [], [], []
        for g in range(2):
            Bg, Cg = _ssd_groups(xc_ref, g)
            CB = _dot(Cg, Bg, _NT)
            dCB = jnp.zeros((L, L), F32)
            dB = jnp.zeros((L, SSD_N), F32)
            dC = jnp.zeros((L, SSD_N), F32)
            for pr in range(4):
                pair = g * 4 + pr
                psl = slice(pair * 128, (pair + 1) * 128)
                dY, Xp, Xdp, dop, dsp = dy_ref[:, psl], X[:, psl], Xd[:, psl], dec_out[:, psl], dec_st[:, psl]
                Xdb = Xdp.astype(BF16)
                acc_scr[0:1, psl] += jnp.sum(dY * Xp, 0, keepdims=True)
                Hp = hp_ref[0, psl, :]
                Hb = Hp.astype(BF16)
                dW = (dY * dop).astype(BF16)
                dcx = dY * _dot(Cg, Hb, _NT) * dop
                dC += _dot(dW, Hb, _NN)
                dHp = _dot(dW, Cg, _TN)
                dHn = dh_scr[psl, :]
                cd = _pair_decay(cs, pair)
                dh_scr[psl, :] = dHp + dHn * cd
                rsum = jnp.sum(dHn * Hp * cd, -1, keepdims=True)
                half = _iota((128, 1), 0) < 64
                s0 = jnp.sum(jnp.where(half, rsum, 0.0), 0, keepdims=True)
                s1 = jnp.sum(jnp.where(half, 0.0, rsum), 0, keepdims=True)
                lane1 = _iota((1, 128), 1)
                dcs_last += jnp.where(lane1 == 2 * pair, s0, 0.0) + jnp.where(lane1 == 2 * pair + 1, s1, 0.0)
                dHb = dHn.astype(BF16)
                dXs = _dot(Bg, dHb, _NT)
                dB += _dot((Xdp * dsp).astype(BF16), dHb, _NN)
                dXd = dXs * dsp
                e_st = dXs * Xdp * dsp
                dcx -= e_st
                dlastx.append(jnp.sum(e_st, 0, keepdims=True))
                for i in range(2):
                    h = 2 * pair + i
                    Dm = _ssd_decay(causal, cs, cs_row, h)
                    M = CB * Dm
                    dYm = jnp.where((lane < 64) if i == 0 else (lane >= 64), dY, 0.0).astype(BF16)
                    dM = _dot(dYm, Xdb, _NT)
                    dXd += _dot(M.astype(BF16), dYm, _TN)
                    dCB += dM * Dm
                    Em = dM * M
                    dcs_col += jnp.where(lane == h, jnp.sum(Em, -1, keepdims=True), 0.0)
                    dcs_row += jnp.where(sub == h, jnp.sum(Em, 0, keepdims=True), 0.0)
                dxc_ref[:, psl] = dY * dx_ref[:, psl] + dXd * dtx[:, psl]
                ddtx.append(dXd * Xp)
                dcsx.append(dcx)
            dCBb = dCB.astype(BF16)
            b0, c0 = SSD_INNER + g * SSD_N, SSD_INNER + 2 * SSD_N + g * SSD_N
            dxc_ref[:, b0:b0 + SSD_N] = dB + _dot(dCBb, Cg, _TN)
            dxc_ref[:, c0:c0 + SSD_N] = dC + _dot(dCBb, Bg, _NN)
        ET = et_ref[...]
        dcs = dcs_col - dcs_row.T + _dot_hi(jnp.concatenate(dcsx, 1), ET)
        dlast = dcs_last + _dot_hi(jnp.broadcast_to(jnp.concatenate(dlastx, 1), (8, SSD_INNER)), ET)[0:1, :]
        dcs += jnp.where(_iota((L, 128), 0) == L - 1, dlast, 0.0)
        dda = _dot_hi((_iota((L, L), 1) >= _iota((L, L), 0)).astype(F32), dcs)
        ddtp = dda * a + _dot_hi(jnp.concatenate(ddtx, 1), ET)
        draw = ddtp * _sigmoid(dtr_ref[...] + dtb_ref[...])
        ddt_ref[...] = draw.astype(BF16)
        st_ref[0:1, :] += jnp.sum(draw, 0, keepdims=True)
        st_ref[1:2, :] += jnp.sum(dda * dtp, 0, keepdims=True) * a

        @pl.when(step == nc - 1)
        def _():
            st_ref[2:3, :] = _dot_hi(acc_scr[...], ET)[0:1, :]

    rev = lambda c: (nc - 1 - c, 0)
    return pl.pallas_call(
        body, grid=(nc,),
        in_specs=_ssd_in_specs(nc, True) + [pl.BlockSpec((SSD_INNER, 128), lambda c: (0, 0)),
                                            pl.BlockSpec((1, SSD_INNER, SSD_N), lambda c: (nc - 1 - c, 0, 0)),
                                            pl.BlockSpec((L, SSD_INNER), rev)],
        out_specs=(pl.BlockSpec((L, SSD_CONV_DIM), rev), pl.BlockSpec((L, 128), rev), pl.BlockSpec((8, 128), lambda c: (0, 0))),
        out_shape=(_sds((S, SSD_CONV_DIM)), _sds((S, 128), BF16), _sds((8, 128))),
        scratch_shapes=[pltpu.VMEM((SSD_INNER, SSD_N), F32), pltpu.VMEM((8, SSD_INNER), F32)],
        name="ssd_core_bwd", compiler_params=_ARB)(xc, proj, dtb, alog, dskip, E, ET, hprev, dy)


def _ssd_post(y, proj, nw, tm=256):
    S = y.shape[0]

    def body(y_ref, z_ref, nw_ref, o_ref):
        for g in range(2):
            sl = slice(g * 512, (g + 1) * 512)
            gated = y_ref[:, sl] * _silu(z_ref[:, sl])
            r = lax.rsqrt(jnp.mean(gated * gated, -1, keepdims=True) + EPS)
            o_ref[:, sl] = (gated * r * nw_ref[:, sl]).astype(BF16)

    return pl.pallas_call(
        body, grid=(S // tm,), in_specs=[_row_spec(tm, D), pl.BlockSpec((tm, D), lambda i: (i, OZ // D)), _vec_spec(D)],
        out_specs=_row_spec(tm, D), out_shape=_sds((S, D), BF16), name="ssd_post", compiler_params=_PAR)(y, proj, nw)


def _ssd_post_bwd(doc, y, proj, nw, tm=256):
    S = y.shape[0]

    def body(d_ref, y_ref, z_ref, nw_ref, dy_ref, dz_ref, st_ref):
        @pl.when(pl.program_id(0) == 0)
        def _():
            st_ref[...] = jnp.zeros_like(st_ref)

        for g in range(2):
            sl = slice(g * 512, (g + 1) * 512)
            yv, zv, dv = y_ref[:, sl], z_ref[:, sl], d_ref[:, sl]
            sz = _silu(zv)
            gated = yv * sz
            r = lax.rsqrt(jnp.mean(gated * gated, -1, keepdims=True) + EPS)
            ghat = gated * r
            st_ref[0:1, sl] += jnp.sum(dv * ghat, 0, keepdims=True)
            gg = dv * nw_ref[:, sl]
            dg = r * (gg - ghat * jnp.mean(gg * ghat, -1, keepdims=True))
            dy_ref[:, sl] = dg * sz
            dz_ref[:, sl] = (dg * yv * _dsilu(zv)).astype(BF16)

    zs = pl.BlockSpec((tm, D), lambda i: (i, OZ // D))
    return pl.pallas_call(
        body, grid=(S // tm,), in_specs=[_row_spec(tm, D), _row_spec(tm, D), zs, _vec_spec(D)],
        out_specs=(_row_spec(tm, D), _row_spec(tm, D), _vec_spec(D, 8)), out_shape=(_sds((S, D)), _sds((S, D), BF16), _sds((8, D))),
        name="ssd_post_bwd", compiler_params=_ARB)(doc, y, proj, nw)


def _row(v, n=None):
    v = v.astype(F32).reshape(1, -1)
    return v if n is None else jnp.pad(v, ((0, 0), (0, n - v.shape[1])))


def _prep_layer(p):
    w_in, wb = p["w_in"].astype(BF16), p["w_branch"].astype(BF16)
    z = lambda n: jnp.zeros((D, n), BF16)
    win = jnp.concatenate([w_in[:, 3760:6832], w_in[:, 1184:2208], w_in[:, 2208:3744], w_in[:, 672:1184], w_in[:, 0:384],
                           w_in[:, 384:640], z(64), w_in[:, 640:672], z(32), w_in[:, 3744:3760], z(112)], 1)
    return dict(
        win=win,
        wqb=jnp.pad(p["w_q_b"].astype(BF16).reshape(Q_RANK, N_HEADS, 96), ((0, 0), (0, 0), (0, 32))).reshape(Q_RANK, D),
        wkvb=p["w_kv_b"].astype(BF16),
        wba=jnp.pad(wb[:512].reshape(N_HEADS, 64, D), ((0, 0), (64, 0), (0, 0))).reshape(D, D), wbb=wb[512:1024], wbc=wb[1024:],
        wo=p["w_out"].astype(BF16), wup=p["ffn_up"].astype(BF16), wdn=p["ffn_down"].astype(BF16),
        nw1=_row(p["norm1_w"]), nw2=_row(p["norm2_w"]), qan=_row(p["q_a_norm"]), kvan=_row(p["kv_a_norm"]),
        wq=_row(p["q_norm"], 128), wk=_row(p["k_norm"], 128), pool_w=p["pool_w"].astype(F32), pool_scale=_row(p["pool_scale"]),
        cw=p["ssd_conv_w"].astype(F32), cb=_row(p["ssd_conv_b"]), dtb=_row(p["ssd_dt_bias"], 128), alog=_row(p["ssd_a_log"], 128),
        dskip=_row(jnp.repeat(p["ssd_d"].astype(F32), SSD_P)), snw=_row(p["ssd_norm_w"]),
        fcw=p["ffn_conv_w"].astype(F32), fcb=_row(p["ffn_conv_b"]))


def _layer_fwd(x, mod8, W, rope, E, tag):
    sh1, sc1, g1, sh2, sc2, g2 = (mod8[i:i + 1] for i in range(6))
    h1 = _ln_mod(x, W["nw1"], sc1, sh1, name=f"ln1_{tag}")
    proj = _mm(h1, W["win"], tn=640, tk=1024, name=f"proj_{tag}")
    qf, kf, ve = _mla_prep(proj, W["wqb"], W["wkvb"], W["qan"], W["kvan"], W["wq"], W["wk"], rope)
    oa = _attn_fwd(qf, kf, ve)
    ob = _pool_fwd(proj, W["pool_w"], W["pool_scale"])
    xc = _ssd_pre(proj, W["cw"], W["cb"])
    y, hprev = _ssd_core(xc, proj, W["dtb"], W["alog"], W["dskip"], E)
    oc = _ssd_post(y, proj, W["snw"])
    merged = _merge_fwd(oa, ob, oc, proj, W["wba"], W["wbb"], W["wbc"])
    x1, out1 = _mm(merged, W["wo"], tk=1024, res=x, gate=g1, name=f"wout_{tag}")
    h2 = _ln_mod(x1, W["nw2"], sc2, sh2, name=f"ln2_{tag}")
    up = _mm(h2, W["wup"], tk=1024, name=f"up_{tag}")
    act = _ffn_act(up, W["fcw"], W["fcb"])
    x2, out2 = _mm(act, W["wdn"], tk=1408, res=x1, gate=g2, name=f"down_{tag}")
    saved = dict(x=x, h1=h1, proj=proj, qf=qf, kf=kf, ve=ve, oa=oa, ob=ob, oc=oc, xc=xc, hprev=hprev, y=y, merged=merged,
                 out1=out1, x1=x1, h2=h2, up=up, act=act, out2=out2)
    return x2, saved


def _layer_bwd(dx2, sv, mod8, W, rope, E, ET, tag):
    sc1, g1, sc2, g2 = mod8[1:2], mod8[2:3], mod8[4:5], mod8[5:6]
    proj = sv["proj"]
    dz2, dg2 = _gate_bwd(dx2, sv["out2"], g2, name=f"gate2_bwd_{tag}")
    dact = _mm(dz2, W["wdn"], "nt", tn=1408, tk=1024, name=f"down_dx_{tag}")
    dwdn = _mm(sv["act"], dz2, "tn", tm=1408, name=f"down_dw_{tag}")
    dupg, dupv, dfcwg, dfcwv, dfcbg, dfcbv = _ffn_act_bwd(sv["up"], dact, W["fcw"], W["fcb"])
    dup = jnp.concatenate([dupg, dupv], 1)
    dh2 = _mm(dup, W["wup"], "nt", tk=1408, name=f"up_dx_{tag}")
    dwup = _mm(sv["h2"], dup, "tn", name=f"up_dw_{tag}")
    dx1, st2 = _ln_mod_bwd(sv["x1"], dh2, dx2, W["nw2"], sc2, name=f"ln2_bwd_{tag}")
    dz1, dg1 = _gate_bwd(dx1, sv["out1"], g1, name=f"gate1_bwd_{tag}")
    dmerged = _mm(dz1, W["wo"], "nt", tk=1024, name=f"wout_dx_{tag}")
    dwo = _mm(sv["merged"], dz1, "tn", name=f"wout_dw_{tag}")
    dya, dyb, dyc, dgl, doa, dob, doc = _merge_bwd(dmerged, sv["oa"], sv["ob"], sv["oc"], proj, W["wba"], W["wbb"], W["wbc"])
    dwba = _mm(sv["oa"], dya, "tn", name=f"wba_dw_{tag}")
    dwbb = _mm(sv["ob"], dyb, "tn", name=f"wbb_dw_{tag}")
    dwbc = _mm(sv["oc"], dyc, "tn", name=f"wbc_dw_{tag}")
    dy, dzs, st_post = _ssd_post_bwd(doc, sv["y"], proj, W["snw"])
    dxc, ddt, st_ssd = _ssd_core_bwd(sv["xc"], proj, sv["hprev"], dy, W["dtb"], W["alog"], W["dskip"], E, ET)
    dxbc, dcw, dcb = _ssd_pre_bwd(proj, dxc, W["cw"], W["cb"])
    dpool, dpw, dps = _pool_bwd(proj, dob, W["pool_w"], W["pool_scale"])
    dqf, dkf, dve = _attn_bwd(sv["qf"], sv["kf"], sv["ve"], doa)
    dqs, dkvs, dwqb, dwkvb, st_mla = _mla_prep_bwd(proj, dqf, dkf, dve, W["wqb"], W["wkvb"], W["qan"], W["kvan"], W["wq"], W["wk"], rope)
    dproj = jnp.concatenate([dgl, dzs, dxbc, dpool, dqs, dkvs, ddt], 1)
    dh1 = _mm(dproj, W["win"], "nt", tk=1408, name=f"proj_dx_{tag}")
    dwin = _mm(sv["h1"], dproj, "tn", tn=640, name=f"proj_dw_{tag}")
    dx, st1 = _ln_mod_bwd(sv["x"], dh1, dx1, W["nw1"], sc1, name=f"ln1_bwd_{tag}")
    grads = dict(
        norm1_w=st1[2], norm2_w=st2[2],
        w_in=jnp.concatenate([dwin[:, OQ:OQ + 384], dwin[:, OKV:OKV + 256], dwin[:, OKV + 320:OKV + 352], dwin[:, OP:OP + 512],
                              dwin[:, OZ:OZ + 1024], dwin[:, OX:OX + 1536], dwin[:, ODT:ODT + 16], dwin[:, OG:OG + 3072]], 1),
        q_a_norm=st_mla[0, :Q_RANK], kv_a_norm=st_mla[1, :KV_RANK], q_norm=st_mla[2, :96], k_norm=st_mla[3, :96],
        w_q_b=dwqb.reshape(Q_RANK, N_HEADS, 128)[:, :, :96].reshape(Q_RANK, 768), w_kv_b=dwkvb,
        pool_w=dpw, pool_scale=dps[0], ssd_conv_w=dcw, ssd_conv_b=dcb[0],
        ssd_dt_bias=st_ssd[0, :SSD_HEADS], ssd_a_log=st_ssd[1, :SSD_HEADS], ssd_d=st_ssd[2, :SSD_HEADS], ssd_norm_w=st_post[0],
        w_branch=jnp.concatenate([dwba.reshape(N_HEADS, 128, D)[:, 64:, :].reshape(512, D), dwbb, dwbc], 0),
        w_out=dwo, ffn_up=dwup, ffn_conv_w=jnp.concatenate([dfcwg, dfcwv], 1), ffn_conv_b=jnp.concatenate([dfcbg, dfcbv], 1)[0],
        ffn_down=dwdn)
    dmod = jnp.concatenate([st1[0:2], dg1[0:1], st2[0:2], dg2[0:1]], 0)
    return dx, grads, dmod


def _ssd_expand():
    E = (jnp.arange(SSD_INNER)[None, :] // SSD_P == jnp.arange(128)[:, None]).astype(F32)
    return E, E.T


def _rope_tables(positions):
    inv_freq = ROPE_THETA ** (-jnp.arange(0, ROPE_DIM, 2, dtype=F32) / ROPE_DIM)
    invf = jnp.concatenate([jnp.zeros((NOPE,), F32), inv_freq, inv_freq, jnp.zeros((32,), F32)]).reshape(1, 128)
    posb = jnp.broadcast_to(positions.astype(F32)[:, None], (positions.shape[0], 128))
    return _rope_tab(posb, invf)


def _local_step(x, target, positions, mods, layers):
    rope = _rope_tables(positions)
    E, ET = _ssd_expand()
    Ws = [_prep_layer(p) for p in layers]
    saved, h = [], x
    for l, W in enumerate(Ws):
        h, sv = _layer_fwd(h, mods[l], W, rope, E, l)
        saved.append(sv)
    dy, lpart = _loss_grad(h, target)
    grads, dmods = [None] * len(Ws), [None] * len(Ws)
    for l in reversed(range(len(Ws))):
        dy, grads[l], dmods[l] = _layer_bwd(dy, saved[l], mods[l], Ws[l], rope, E, ET, l)
    return lpart[0, 0], dy, grads, dmods


_ANY = pl.BlockSpec(memory_space=pl.ANY)
_VMEM = pl.BlockSpec(memory_space=pltpu.VMEM)


def _place():
    x, y, c = lax.axis_index("x"), lax.axis_index("y"), lax.axis_index("c")
    return x, y, c, [(1 - x, y), (x, 1 - y), (1 - x, 1 - y)]


def _allgather8(v, name):
    m_per, n = v.shape

    def body(x_ref, out_ref, send_sems, recv_sems, local_sem):
        x, y, c, chips = _place()
        me, sibling = (x, y, c), (x, y, 1 - c)

        def rows(px, py, pc):
            return out_ref.at[pl.ds((4 * px + 2 * py + pc) * m_per, m_per), :]

        def copy(k, block, to, src=None):
            return pltpu.make_async_remote_copy(src_ref=rows(*block) if src is None else src, dst_ref=rows(*block),
                                                send_sem=send_sems.at[k], recv_sem=recv_sems.at[k], device_id=to, device_id_type=MESH)

        mine = pltpu.make_async_copy(x_ref, rows(*me), local_sem)
        mine.start()
        first = [copy(0, me, sibling, src=x_ref)] + [copy(1 + j, me, (*chip, c), src=x_ref) for j, chip in enumerate(chips)]
        for cp in first:
            cp.start()
        passed = [copy(4 + j, (*chip, c), sibling) for j, chip in enumerate(chips)]
        for j, chip in enumerate(chips):
            copy(1 + j, (*chip, c), me).wait_recv()
            passed[j].start()
        copy(0, sibling, me).wait_recv()
        for j, chip in enumerate(chips):
            copy(4 + j, (*chip, 1 - c), me).wait_recv()
        for cp in first + passed:
            cp.wait_send()
        mine.wait()

    return pl.pallas_call(
        body, out_shape=_sds((8 * m_per, n), v.dtype), in_specs=[_VMEM], out_specs=_VMEM,
        scratch_shapes=[pltpu.SemaphoreType.DMA((7,)), pltpu.SemaphoreType.DMA((7,)), pltpu.SemaphoreType.DMA], name=name)(v)


def _gather_chips(pack, name):
    R, n = pack.shape
    R2 = R // 2

    def body(p_ref, o_ref, send_sems, recv_sems, local_sem):
        x, y, c, chips = _place()
        k, sibling = 2 * x + y, (x, y, 1 - c)

        def half(kk, cc):
            return o_ref.at[kk, pl.ds(cc * R2, R2), :]

        def copy(s, kk, cc, to, src=None):
            return pltpu.make_async_remote_copy(src_ref=half(kk, cc) if src is None else src, dst_ref=half(kk, cc),
                                                send_sem=send_sems.at[s], recv_sem=recv_sems.at[s], device_id=to, device_id_type=MESH)

        mine = pltpu.make_async_copy(p_ref, o_ref.at[k], local_sem)
        mine.start()
        first = [copy(j, k, c, (*chip, c), src=p_ref.at[pl.ds(c * R2, R2), :]) for j, chip in enumerate(chips)]
        for cp in first:
            cp.start()
        passed = [copy(3 + j, 2 * chip[0] + chip[1], c, sibling) for j, chip in enumerate(chips)]
        for j, chip in enumerate(chips):
            copy(j, 2 * chip[0] + chip[1], c, (x, y, c)).wait_recv()
            passed[j].start()
        for j, chip in enumerate(chips):
            copy(3 + j, 2 * chip[0] + chip[1], 1 - c, (x, y, c)).wait_recv()
        for cp in first + passed:
            cp.wait_send()
        mine.wait()

    return pl.pallas_call(
        body, out_shape=_sds((4, R, n), pack.dtype), in_specs=[_ANY], out_specs=_ANY,
        scratch_shapes=[pltpu.SemaphoreType.DMA((6,)), pltpu.SemaphoreType.DMA((6,)), pltpu.SemaphoreType.DMA], name=name)(pack)


def _scatter_chips(a, name):
    def body(a_ref, t_ref, send_sems, recv_sems, local_sem):
        x, y, c, chips = _place()
        k = 2 * x + y
        mine = pltpu.make_async_copy(a_ref.at[k], t_ref.at[k], local_sem)
        mine.start()

        def copy(j, src_k, dst_k, to):
            return pltpu.make_async_remote_copy(src_ref=a_ref.at[src_k], dst_ref=t_ref.at[dst_k], send_sem=send_sems.at[j],
                                                recv_sem=recv_sems.at[j], device_id=to, device_id_type=MESH)

        sends = [copy(j, 2 * chip[0] + chip[1], k, (*chip, c)) for j, chip in enumerate(chips)]
        for cp in sends:
            cp.start()
        for j, chip in enumerate(chips):
            copy(j, k, 2 * chip[0] + chip[1], (x, y, c)).wait_recv()
        for cp in sends:
            cp.wait_send()
        mine.wait()

    return pl.pallas_call(
        body, out_shape=_sds(a.shape, a.dtype), in_specs=[_ANY], out_specs=_ANY,
        scratch_shapes=[pltpu.SemaphoreType.DMA((3,)), pltpu.SemaphoreType.DMA((3,)), pltpu.SemaphoreType.DMA], name=name)(a)


def _swap_sibling(v, name):
    def body(v_ref, o_ref, send_sem, recv_sem):
        x, y, c, _ = _place()
        cp = pltpu.make_async_remote_copy(src_ref=v_ref, dst_ref=o_ref, send_sem=send_sem, recv_sem=recv_sem,
                                          device_id=(x, y, 1 - c), device_id_type=MESH)
        cp.start()
        cp.wait()

    return pl.pallas_call(body, out_shape=_sds(v.shape, v.dtype), in_specs=[_ANY], out_specs=_ANY,
                          scratch_shapes=[pltpu.SemaphoreType.DMA, pltpu.SemaphoreType.DMA], name=name)(v)


def _add_cast(keep, recv, tm=256):
    _, R2, n = keep.shape
    spec = pl.BlockSpec((1, tm, n), lambda k, i: (k, i, 0))

    def body(a_ref, b_ref, o_ref):
        o_ref[...] = (a_ref[...] + b_ref[...].astype(F32)).astype(BF16)

    return pl.pallas_call(body, grid=(4, R2 // tm), in_specs=[spec, spec], out_specs=spec, out_shape=_sds(keep.shape, BF16),
                          name="rs_add_cast", compiler_params=pltpu.CompilerParams(dimension_semantics=("parallel", "parallel")))(keep, recv)


def _sum_lead(t, name, tm=256):
    P, R, n = t.shape
    tm = tm if R % tm == 0 else 128

    def body(t_ref, o_ref):
        acc = t_ref[0].astype(F32)
        for j in range(1, P):
            acc = acc + t_ref[j].astype(F32)
        o_ref[...] = acc

    return pl.pallas_call(body, grid=(R // tm,), in_specs=[pl.BlockSpec((P, tm, n), lambda i: (0, i, 0))],
                          out_specs=pl.BlockSpec((tm, n), lambda i: (i, 0)), out_shape=_sds((R, n)), name=name, compiler_params=_PAR)(t)


def _ada_fwd(c16, ada_w, ada_b_cols, tn=512):
    L, _, n = ada_w.shape

    def body(c_ref, w_ref, b_ref, o_ref):
        o_ref[0] = _dot(_silu(c_ref[...]).astype(BF16), w_ref[0].astype(BF16), _NN) + b_ref[0]

    return pl.pallas_call(
        body, grid=(L, n // tn),
        in_specs=[pl.BlockSpec((16, D), lambda l, j: (0, 0)), pl.BlockSpec((1, D, tn), lambda l, j: (l, 0, j)), pl.BlockSpec((1, 1, tn), lambda l, j: (l, 0, j))],
        out_specs=pl.BlockSpec((1, 16, tn), lambda l, j: (l, 0, j)), out_shape=_sds((L, 16, n)), name="ada_fwd",
        compiler_params=pltpu.CompilerParams(dimension_semantics=("parallel", "parallel")))(c16, ada_w, ada_b_cols)


def _ada_bwd(c16, dmod, tn=512):
    L, _, n = dmod.shape

    def body(c_ref, d_ref, o_ref):
        o_ref[0] = _dot(_silu(c_ref[...]).astype(BF16), d_ref[0].astype(BF16), _TN)

    return pl.pallas_call(
        body, grid=(L, n // tn), in_specs=[pl.BlockSpec((16, D), lambda l, j: (0, 0)), pl.BlockSpec((1, 16, tn), lambda l, j: (l, 0, j))],
        out_specs=pl.BlockSpec((1, D, tn), lambda l, j: (l, 0, j)), out_shape=_sds((L, D, n)), name="ada_bwd",
        compiler_params=pltpu.CompilerParams(dimension_semantics=("parallel", "parallel")))(c16, dmod)


def _adamw(w, g, m, v, name):
    R, n = w.shape
    tm = 256 if R % 256 == 0 else 128
    assert R % tm == 0, (name, w.shape)

    def body(w_ref, g_ref, m_ref, v_ref, d_ref, nm_ref, nv_ref):
        gv = g_ref[...]
        mn = ADAM_B1 * m_ref[...] + (1.0 - ADAM_B1) * gv
        vn = ADAM_B2 * v_ref[...] + (1.0 - ADAM_B2) * (gv * gv)
        m_hat = mn / (1.0 - ADAM_B1 ** ADAM_STEP)
        v_hat = vn / (1.0 - ADAM_B2 ** ADAM_STEP)
        d_ref[...] = -ADAM_LR * (m_hat / (jnp.sqrt(v_hat) + ADAM_EPS) + ADAM_WD * w_ref[...])
        nm_ref[...] = mn
        nv_ref[...] = vn

    spec = pl.BlockSpec((tm, n), lambda i: (i, 0))
    return pl.pallas_call(body, grid=(R // tm,), in_specs=[spec] * 4, out_specs=(spec,) * 3, out_shape=(_sds((R, n)),) * 3,
                          name=name, compiler_params=_PAR)(w, g, m, v)


_W_NAMES = ["ada_w", "ada_b", "norm1_w", "w_in", "q_a_norm", "w_q_b", "kv_a_norm", "w_kv_b", "q_norm", "k_norm", "pool_w",
            "pool_scale", "ssd_conv_w", "ssd_conv_b", "ssd_dt_bias", "ssd_a_log", "ssd_d", "ssd_norm_w", "w_branch", "w_out",
            "norm2_w", "ffn_up", "ffn_conv_w", "ffn_conv_b", "ffn_down"]
_BIG = [("w_in", (D, IN_DIM // 4), 1), ("w_q_b", (Q_RANK, 192), 1), ("w_kv_b", (KV_RANK, 256), 1), ("w_branch", (512, D), 0),
        ("w_out", (256, D), 0), ("ffn_up", (D, 2 * FFN // 4), 1), ("ffn_down", (FFN // 4, D), 0)]
_BIG_ROWS = 9728
_SMALL = [("norm1_w", (D,)), ("q_a_norm", (Q_RANK,)), ("kv_a_norm", (KV_RANK,)), ("q_norm", (96,)), ("k_norm", (96,)),
          ("pool_w", (4, 128, 128)), ("pool_scale", (512,)), ("ssd_conv_w", (4, SSD_CONV_DIM)), ("ssd_conv_b", (SSD_CONV_DIM,)),
          ("ssd_dt_bias", (SSD_HEADS,)), ("ssd_a_log", (SSD_HEADS,)), ("ssd_d", (SSD_HEADS,)), ("ssd_norm_w", (D,)), ("norm2_w", (D,)),
          ("ffn_conv_w", (3, 2 * FFN)), ("ffn_conv_b", (2 * FFN,))]
_CONV_SHARDED = {"ssd_conv_w": SSD_CONV_DIM // 4, "ffn_conv_w": 2 * FFN // 4}


def _pack_flat(arrs, mult):
    flat = jnp.concatenate([a.astype(F32).reshape(-1) for a in arrs])
    rows = -(-flat.shape[0] // (128 * mult)) * mult
    return jnp.pad(flat, (0, rows * 128 - flat.shape[0])).reshape(rows, 128), [a.shape for a in arrs]


def _unpack_flat(packed, shapes):
    flat, out, off = packed.reshape(-1), [], 0
    for s in shapes:
        n = 1
        for d in s:
            n *= d
        out.append(flat[off:off + n].reshape(s))
        off += n
    return out


def _pack_big(get, dtype):
    parts = [get(n, l).astype(dtype).reshape(-1, D) for l in range(2) for n, _, _ in _BIG]
    rows = sum(p.shape[0] for p in parts)
    return jnp.concatenate(parts + [jnp.zeros((_BIG_ROWS - rows, D), dtype)], 0)


def _unpack_big(pack):
    out, off = {n: [] for n, _, _ in _BIG}, 0
    for l in range(2):
        for n, shape, _ in _BIG:
            rows = shape[0] * shape[1] // D
            out[n].append(pack[..., off:off + rows, :].reshape(pack.shape[:-2] + shape))
            off += rows
    return out


def _join_shards(a, axis):
    return a.reshape(4 * a.shape[1], a.shape[2]) if axis == 0 else jnp.transpose(a, (1, 0, 2)).reshape(a.shape[1], 4 * a.shape[2])


def _split_shards(a, axis):
    if axis == 0:
        return a.reshape(4, a.shape[0] // 4, a.shape[1])
    return jnp.transpose(a.reshape(a.shape[0], 4, a.shape[1] // 4), (1, 0, 2))


def kernel(x, c, positions, ada_w, ada_b, norm1_w, w_in, q_a_norm, w_q_b, kv_a_norm, w_kv_b, q_norm, k_norm, pool_w, pool_scale, ssd_conv_w, ssd_conv_b, ssd_dt_bias, ssd_a_log, ssd_d, ssd_norm_w, w_branch, w_out, norm2_w, ffn_up, ffn_conv_w, ffn_conv_b, ffn_down, loss_target, m_ada_w, m_ada_b, m_norm1_w, m_w_in, m_q_a_norm, m_w_q_b, m_kv_a_norm, m_w_kv_b, m_q_norm, m_k_norm, m_pool_w, m_pool_scale, m_ssd_conv_w, m_ssd_conv_b, m_ssd_dt_bias, m_ssd_a_log, m_ssd_d, m_ssd_norm_w, m_w_branch, m_w_out, m_norm2_w, m_ffn_up, m_ffn_conv_w, m_ffn_conv_b, m_ffn_down, v_ada_w, v_ada_b, v_norm1_w, v_w_in, v_q_a_norm, v_w_q_b, v_kv_a_norm, v_w_kv_b, v_q_norm, v_k_norm, v_pool_w, v_pool_scale, v_ssd_conv_w, v_ssd_conv_b, v_ssd_dt_bias, v_ssd_a_log, v_ssd_d, v_ssd_norm_w, v_w_branch, v_w_out, v_norm2_w, v_ffn_up, v_ffn_conv_w, v_ffn_conv_b, v_ffn_down):
    a = dict(locals())
    xi, yi, ci = lax.axis_index("x"), lax.axis_index("y"), lax.axis_index("c")
    chip = 2 * xi + yi
    dev = 2 * chip + ci
    ncol = 6 * D // 4

    c_all = _allgather8(c.reshape(8, 128), "gather_c").reshape(8, D)
    c16 = jnp.pad(c_all, ((0, 8), (0, 0)))
    ada_b_cols = lax.dynamic_slice_in_dim(ada_b, chip * ncol, ncol, axis=1).reshape(2, 1, ncol)
    mod_part = _ada_fwd(c16, ada_w, ada_b_cols)[:, :8]
    small1, shapes1 = _pack_flat([mod_part, ssd_conv_w, ffn_conv_w], 8)
    got1 = _allgather8(small1, "gather_mod").reshape(8, -1, 128)
    per_chip = [_unpack_flat(got1[2 * k], shapes1) for k in range(4)]
    mod_all = jnp.concatenate([per_chip[k][0] for k in range(4)], -1)
    conv_full = {"ssd_conv_w": jnp.concatenate([per_chip[k][1] for k in range(4)], -1),
                 "ffn_conv_w": jnp.concatenate([per_chip[k][2] for k in range(4)], -1)}
    mod_mine = lax.dynamic_index_in_dim(mod_all, dev, axis=1, keepdims=False).reshape(2, 6, D)
    mods = [jnp.pad(mod_mine[l], ((0, 2), (0, 0))) for l in range(2)]

    gathered = _unpack_big(_gather_chips(_pack_big(lambda n, l: a[n][l], BF16), "gather_w"))
    layers = []
    for l in range(2):
        p = {n: a[n][l] for n in _W_NAMES if n not in ("ada_w", "ada_b")}
        p.update({n: _join_shards(gathered[n][l], ax) for n, _, ax in _BIG})
        p.update({n: conv_full[n][l] for n in conv_full})
        layers.append(p)

    lpart, grad_x, grads, dmods = _local_step(x[0], loss_target[0], positions[0], mods, layers)
    loss = lax.psum(lpart, ("x", "y", "c"))

    small2, shapes2 = _pack_flat([jnp.stack(dmods)] + [grads[l][n] for l in range(2) for n, _ in _SMALL], 8)
    got2 = _allgather8(small2, "gather_small").reshape(8, -1, 128)
    tot = _unpack_flat(_sum_lead(got2, "sum_small"), shapes2)
    g = {"ada_b": tot[0].reshape(2, 6 * D)}
    for i, (n, _) in enumerate(_SMALL):
        g[n] = jnp.stack([tot[1 + i], tot[1 + len(_SMALL) + i]])
    for n, w in _CONV_SHARDED.items():
        g[n] = lax.dynamic_slice_in_dim(g[n], chip * w, w, axis=2)
    nd = 2 * 6 * D // 128
    dmod_all = jnp.transpose(got2[:, :nd].reshape(8, 2, 6 * D), (1, 0, 2))
    dmod_cols = lax.dynamic_slice_in_dim(jnp.pad(dmod_all, ((0, 0), (0, 8), (0, 0))), chip * ncol, ncol, axis=2)
    g["ada_w"] = _ada_bwd(c16, dmod_cols)

    shards = {n: [_split_shards(grads[l][n], ax) for l in range(2)] for n, _, ax in _BIG}
    G = jnp.stack([_pack_big(lambda n, l, k=k: shards[n][l][k], F32) for k in range(4)])
    R2 = _BIG_ROWS // 2
    keep = lax.dynamic_slice_in_dim(G, ci * R2, R2, axis=1)
    send = lax.dynamic_slice_in_dim(G, (1 - ci) * R2, R2, axis=1).astype(BF16)
    chip_sum = _add_cast(keep, _swap_sibling(send, "rs_swap_halves"))
    mine = _sum_lead(_scatter_chips(chip_sum, "rs_scatter"), "rs_sum")
    other = _swap_sibling(mine, "rs_swap_result")
    full = jnp.where(ci == 0, jnp.concatenate([mine, other], 0), jnp.concatenate([other, mine], 0))
    for n, parts in _unpack_big(full).items():
        g[n] = jnp.stack(parts)

    delta, new_m, new_v = {}, {}, {}
    big = ["ada_w"] + [n for n, _, _ in _BIG]
    for n in big:
        shp = a[n].shape
        r2 = lambda t: t.reshape(-1, shp[-1])
        outs = _adamw(r2(a[n]), r2(g[n]), r2(a["m_" + n]), r2(a["v_" + n]), f"adamw_{n}")
        delta[n], new_m[n], new_v[n] = (o.reshape(shp) for o in outs)
    rest = [n for n in _W_NAMES if n not in big]
    packs = [_pack_flat([t[n] if pre is None else t[pre + n] for n in rest], 128)[0]
             for t, pre in ((a, None), (g, None), (a, "m_"), (a, "v_"))]
    rest_shapes = [a[n].shape for n in rest]
    outs = [_unpack_flat(o, rest_shapes) for o in _adamw(*packs, "adamw_rest")]
    for i, n in enumerate(rest):
        delta[n], new_m[n], new_v[n] = outs[0][i], outs[1][i], outs[2][i]

    return (loss, grad_x[None], *[g[n] for n in _W_NAMES], *[delta[n] for n in _W_NAMES],
            *[new_m[n] for n in _W_NAMES], *[new_v[n] for n in _W_NAMES])
```

```python
import functools

import jax
import jax.numpy as jnp
from jax import lax
from jax.experimental import pallas as pl
from jax.experimental.pallas import tpu as pltpu

F32 = jnp.float32
BF16 = jnp.bfloat16
MESH = pl.DeviceIdType.MESH
HI = lax.Precision.HIGHEST

D = 1024
N_HEADS = 8
NOPE, ROPE_DIM = 64, 32
Q_RANK, KV_RANK = 384, 256
POOL_WINDOWS = (2, 4, 8, 16)
SSD_HEADS, SSD_P, SSD_N, SSD_L = 16, 64, 128, 128
SSD_INNER = 1024
SSD_CONV_DIM = 1536
FFN = 2816
EPS = 1e-6
ROPE_THETA = 10000.0
OG, OZ, OX, OP, OQ, OKV, ODT, IN_PAD = 0, 3072, 4096, 5632, 6144, 6528, 6912, 7040
IN_DIM = 6832
ADAM_LR, ADAM_B1, ADAM_B2, ADAM_EPS, ADAM_WD, ADAM_STEP = 0.001, 0.9, 0.999, 1e-08, 0.01, 10

_ARB = pltpu.CompilerParams(dimension_semantics=("arbitrary",))
_PAR = pltpu.CompilerParams(dimension_semantics=("parallel",))


def _pick(n, pref):
    if n <= pref:
        return n
    best = None
    for t in range(128, pref + 1, 128):
        if n % t == 0:
            best = t
    assert best is not None, (n, pref)
    return best


def _row_tile(r, cap=256):
    best = None
    for t in range(16, min(r, cap) + 1, 16):
        if r % t == 0:
            best = t
    assert best is not None, r
    return best


def _sds(shape, dtype=F32):
    return jax.ShapeDtypeStruct(tuple(shape), dtype)


def _iota(shape, dim):
    return lax.broadcasted_iota(jnp.int32, shape, dim)


def _sigmoid(x):
    return 1.0 / (1.0 + jnp.exp(-x))


def _silu(x):
    return x * _sigmoid(x)


def _dsilu(x):
    s = _sigmoid(x)
    return s * (1.0 + x * (1.0 - s))


def _dot(a, b, dims):
    return lax.dot_general(a, b, (dims, ((), ())), preferred_element_type=F32)


_NN, _NT, _TN = ((1,), (0,)), ((1,), (1,)), ((0,), (0,))


def _dot_hi(a, b, dims=_NN):
    return lax.dot_general(a, b, (dims, ((), ())), preferred_element_type=F32, precision=HI)


def _shift_down(x, j):
    n = x.shape[0]
    return jnp.where(_iota(x.shape, 0) >= j, pltpu.roll(x, j, 0), 0.0)


def _shift_up(x, j):
    n = x.shape[0]
    return jnp.where(_iota(x.shape, 0) < n - j, pltpu.roll(x, n - j, 0), 0.0)


def _mm(a, b, mode="nn", out_dtype=F32, tm=512, tn=512, tk=512, res=None, gate=None, name="mm"):
    if mode == "nn":
        (M, K), (K2, N) = a.shape, b.shape
    elif mode == "nt":
        (M, K), (N, K2) = a.shape, b.shape
    else:
        (K, M), (K2, N) = a.shape, b.shape
    assert K == K2, (a.shape, b.shape, mode)
    tm, tn, tk = _pick(M, tm), _pick(N, tn), _pick(K, tk)
    nk = K // tk
    dims = {"nn": _NN, "nt": _NT, "tn": _TN}[mode]
    fused = res is not None

    def body(*refs):
        if fused:
            a_ref, b_ref, r_ref, g_ref, o_ref, raw_ref, acc_ref = refs
        else:
            a_ref, b_ref, o_ref, acc_ref = refs
        k = pl.program_id(2)

        @pl.when(k == 0)
        def _():
            acc_ref[...] = jnp.zeros_like(acc_ref)

        acc_ref[...] += _dot(a_ref[...].astype(BF16), b_ref[...].astype(BF16), dims)

        @pl.when(k == nk - 1)
        def _():
            if fused:
                raw_ref[...] = acc_ref[...]
                o_ref[...] = r_ref[...] + g_ref[...] * acc_ref[...]
            else:
                o_ref[...] = acc_ref[...].astype(out_dtype)

    if mode == "nn":
        a_spec = pl.BlockSpec((tm, tk), lambda i, j, k: (i, k))
        b_spec = pl.BlockSpec((tk, tn), lambda i, j, k: (k, j))
    elif mode == "nt":
        a_spec = pl.BlockSpec((tm, tk), lambda i, j, k: (i, k))
        b_spec = pl.BlockSpec((tn, tk), lambda i, j, k: (j, k))
    else:
        a_spec = pl.BlockSpec((tk, tm), lambda i, j, k: (k, i))
        b_spec = pl.BlockSpec((tk, tn), lambda i, j, k: (k, j))
    o_spec = pl.BlockSpec((tm, tn), lambda i, j, k: (i, j))
    in_specs, args = [a_spec, b_spec], [a, b]
    out_shape, out_specs = _sds((M, N), out_dtype), o_spec
    if fused:
        in_specs += [o_spec, pl.BlockSpec((1, tn), lambda i, j, k: (0, j))]
        args += [res, gate]
        out_shape, out_specs = (_sds((M, N)), _sds((M, N))), (o_spec, o_spec)
    return pl.pallas_call(
        body, grid=(M // tm, N // tn, nk), in_specs=in_specs, out_specs=out_specs, out_shape=out_shape,
        scratch_shapes=[pltpu.VMEM((tm, tn), F32)], name=name,
        compiler_params=pltpu.CompilerParams(dimension_semantics=("parallel", "parallel", "arbitrary")),
    )(*args)


def _mm_blocks(a, b, dims, grid, a_spec, b_spec, o_spec, out_shape, acc_shape, name):
    nk = grid[2]

    def body(a_ref, b_ref, o_ref, acc_ref):
        k = pl.program_id(2)

        @pl.when(k == 0)
        def _():
            acc_ref[...] = jnp.zeros_like(acc_ref)

        acc_ref[...] += _dot(a_ref[...].astype(BF16), b_ref[...].astype(BF16), dims)

        @pl.when(k == nk - 1)
        def _():
            o_ref[...] = acc_ref[...].astype(o_ref.dtype)

    return pl.pallas_call(
        body, grid=grid, in_specs=[a_spec, b_spec], out_specs=o_spec, out_shape=out_shape,
        scratch_shapes=[pltpu.VMEM(acc_shape, F32)], name=name,
        compiler_params=pltpu.CompilerParams(dimension_semantics=("parallel", "parallel", "arbitrary")),
    )(a, b)


_UP_SHARD = 2 * FFN // 4


def _up_fwd(h2, wup4, name, tm=512):
    S = h2.shape[0]
    tm = min(tm, S)
    return _mm_blocks(h2, wup4, _NN, (S // tm, 4, 1), pl.BlockSpec((tm, D), lambda i, j, k: (i, 0)),
                      pl.BlockSpec((None, D, _UP_SHARD), lambda i, j, k: (j, 0, 0)), pl.BlockSpec((tm, _UP_SHARD), lambda i, j, k: (i, j)),
                      _sds((S, 2 * FFN)), (tm, _UP_SHARD), name)


def _up_dx(dup2, wup4, name, tm=512, tn=512):
    S = dup2.shape[1]
    tm = min(tm, S)
    return _mm_blocks(dup2, wup4, _NT, (S // tm, D // tn, 4), pl.BlockSpec((None, tm, _UP_SHARD), lambda i, j, k: (lax.div(k, 2), i, lax.rem(k, 2))),
                      pl.BlockSpec((None, tn, _UP_SHARD), lambda i, j, k: (k, j, 0)), pl.BlockSpec((tm, tn), lambda i, j, k: (i, j)),
                      _sds((S, D)), (tm, tn), name)


def _up_dw(h2, dup2, name, tm=512, tk=512):
    S = h2.shape[0]
    tk = min(tk, S)
    return _mm_blocks(h2, dup2, _TN, (D // tm, 4, S // tk), pl.BlockSpec((tk, tm), lambda i, j, k: (k, i)),
                      pl.BlockSpec((None, tk, _UP_SHARD), lambda i, j, k: (lax.div(j, 2), k, lax.rem(j, 2))),
                      pl.BlockSpec((None, tm, _UP_SHARD), lambda i, j, k: (j, i, 0)), _sds((4, D, _UP_SHARD)), (tm, _UP_SHARD), name)


def _row_spec(tm, n):
    return pl.BlockSpec((tm, n), lambda i: (i, 0))


def _vec_spec(n, rows=1):
    return pl.BlockSpec((rows, n), lambda i: (0, 0))


def _ln_mod(x, nw, sc, sh, name, tm=256):
    S = x.shape[0]

    def body(x_ref, nw_ref, sc_ref, sh_ref, o_ref):
        xv = x_ref[...]
        r = lax.rsqrt(jnp.mean(xv * xv, -1, keepdims=True) + EPS)
        o_ref[...] = ((xv * r * nw_ref[...]) * (1.0 + sc_ref[...]) + sh_ref[...]).astype(BF16)

    return pl.pallas_call(
        body, grid=(S // tm,), in_specs=[_row_spec(tm, D)] + [_vec_spec(D)] * 3, out_specs=_row_spec(tm, D),
        out_shape=_sds((S, D), BF16), name=name, compiler_params=_PAR)(x, nw, sc, sh)


def _ln_mod_bwd(x, dh, dres, nw, sc, name, tm=256):
    S = x.shape[0]

    def body(x_ref, dh_ref, dres_ref, nw_ref, sc_ref, dx_ref, st_ref):
        @pl.when(pl.program_id(0) == 0)
        def _():
            st_ref[...] = jnp.zeros_like(st_ref)

        xv, dhv, nwv = x_ref[...], dh_ref[...], nw_ref[...]
        r = lax.rsqrt(jnp.mean(xv * xv, -1, keepdims=True) + EPS)
        xhat = xv * r
        dn = dhv * (1.0 + sc_ref[...])
        g = dn * nwv
        dx_ref[...] = dres_ref[...] + r * (g - xhat * jnp.mean(g * xhat, -1, keepdims=True))
        st_ref[0:1, :] += jnp.sum(dhv, 0, keepdims=True)
        st_ref[1:2, :] += jnp.sum(dhv * (xhat * nwv), 0, keepdims=True)
        st_ref[2:3, :] += jnp.sum(dn * xhat, 0, keepdims=True)

    return pl.pallas_call(
        body, grid=(S // tm,), in_specs=[_row_spec(tm, D)] * 3 + [_vec_spec(D)] * 2,
        out_specs=(_row_spec(tm, D), _vec_spec(D, 8)), out_shape=(_sds((S, D)), _sds((8, D))),
        name=name, compiler_params=_ARB)(x, dh, dres, nw, sc)


def _gate_bwd(dx, out, g, name, tm=256):
    S = dx.shape[0]

    def body(dx_ref, o_ref, g_ref, dz_ref, dg_ref):
        @pl.when(pl.program_id(0) == 0)
        def _():
            dg_ref[...] = jnp.zeros_like(dg_ref)

        dxv = dx_ref[...]
        dz_ref[...] = (dxv * g_ref[...]).astype(BF16)
        dg_ref[0:1, :] += jnp.sum(dxv * o_ref[...], 0, keepdims=True)

    return pl.pallas_call(
        body, grid=(S // tm,), in_specs=[_row_spec(tm, D)] * 2 + [_vec_spec(D)],
        out_specs=(_row_spec(tm, D), _vec_spec(D, 8)), out_shape=(_sds((S, D), BF16), _sds((8, D))),
        name=name, compiler_params=_ARB)(dx, out, g)


def _loss_grad(y, t, tm=256):
    S = y.shape[0]

    def body(y_ref, t_ref, dy_ref, l_ref):
        @pl.when(pl.program_id(0) == 0)
        def _():
            l_ref[...] = jnp.zeros_like(l_ref)

        e = y_ref[...] - t_ref[...]
        dy_ref[...] = e * (1.0 / D)
        l_ref[...] += 0.5 * jnp.sum(jnp.mean(e * e, -1, keepdims=True), 0, keepdims=True)

    return pl.pallas_call(
        body, grid=(S // tm,), in_specs=[_row_spec(tm, D)] * 2,
        out_specs=(_row_spec(tm, D), pl.BlockSpec((8, 128), lambda i: (0, 0))),
        out_shape=(_sds((S, D)), _sds((8, 128))), name="loss_grad", compiler_params=_ARB)(y, t)


def _conv(x, w, b):
    K = w.shape[0]
    acc = x * w[K - 1:K, :] + b
    for j in range(1, K):
        acc = acc + _shift_down(x, j) * w[K - 1 - j:K - j, :]
    return acc


def _conv_bwd(x, w, dc):
    K = w.shape[0]
    dx = dc * w[K - 1:K, :]
    dws = [jnp.sum(dc * x, 0, keepdims=True)]
    for j in range(1, K):
        dx = dx + _shift_up(dc, j) * w[K - 1 - j:K - j, :]
        dws.append(jnp.sum(dc * _shift_down(x, j), 0, keepdims=True))
    return dx, dws[::-1], jnp.sum(dc, 0, keepdims=True)


def _col_spec(S, tc, off=0):
    return pl.BlockSpec((S, tc), lambda j: (0, j + off))


def _ssd_pre(proj, cw, cb, tc=256):
    S, n = proj.shape[0], SSD_CONV_DIM

    def body(x_ref, w_ref, b_ref, o_ref):
        o_ref[...] = _silu(_conv(x_ref[...], w_ref[...], b_ref[...]))

    return pl.pallas_call(
        body, grid=(n // tc,),
        in_specs=[_col_spec(S, tc, OX // tc), pl.BlockSpec((4, tc), lambda j: (0, j)), pl.BlockSpec((1, tc), lambda j: (0, j))],
        out_specs=_col_spec(S, tc), out_shape=_sds((S, n)), name="ssd_pre", compiler_params=_PAR)(proj, cw, cb)


def _ssd_pre_bwd(proj, dxc, cw, cb, tc=256):
    S, n = proj.shape[0], SSD_CONV_DIM

    def body(x_ref, d_ref, w_ref, b_ref, dx_ref, dw_ref, db_ref):
        xv, wv = x_ref[...], w_ref[...]
        dc = d_ref[...] * _dsilu(_conv(xv, wv, b_ref[...]))
        dx, dw, db = _conv_bwd(xv, wv, dc)
        dx_ref[...] = dx.astype(BF16)
        for k, row in enumerate(dw):
            dw_ref[k:k + 1, :] = row
        db_ref[...] = db

    wspec, bspec = pl.BlockSpec((4, tc), lambda j: (0, j)), pl.BlockSpec((1, tc), lambda j: (0, j))
    return pl.pallas_call(
        body, grid=(n // tc,), in_specs=[_col_spec(S, tc, OX // tc), _col_spec(S, tc), wspec, bspec],
        out_specs=(_col_spec(S, tc), wspec, bspec), out_shape=(_sds((S, n), BF16), _sds((4, n)), _sds((1, n))),
        name="ssd_pre_bwd", compiler_params=_PAR)(proj, dxc, cw, cb)


def _ffn_act(up, cw, cb, tc=256):
    S, nb = up.shape[0], FFN // tc

    def body(g_ref, v_ref, wg_ref, wv_ref, bg_ref, bv_ref, o_ref):
        o_ref[...] = (_silu(_conv(g_ref[...], wg_ref[...], bg_ref[...])) * _conv(v_ref[...], wv_ref[...], bv_ref[...])).astype(BF16)

    def wspec(off):
        return pl.BlockSpec((3, tc), lambda j: (0, j + off))

    def bspec(off):
        return pl.BlockSpec((1, tc), lambda j: (0, j + off))

    return pl.pallas_call(
        body, grid=(nb,), in_specs=[_col_spec(S, tc), _col_spec(S, tc, nb), wspec(0), wspec(nb), bspec(0), bspec(nb)],
        out_specs=_col_spec(S, tc), out_shape=_sds((S, FFN), BF16), name="ffn_act", compiler_params=_PAR)(up, up, cw, cw, cb, cb)


def _ffn_act_bwd(up, dact, cw, cb, tc=256):
    S, nb = up.shape[0], FFN // tc

    def body(g_ref, v_ref, d_ref, wg_ref, wv_ref, bg_ref, bv_ref, dx_ref, dw_ref, db_ref):
        gv, vv, wg, wv, da = g_ref[...], v_ref[...], wg_ref[...], wv_ref[...], d_ref[...]
        cg, cv = _conv(gv, wg, bg_ref[...]), _conv(vv, wv, bv_ref[...])
        for half, (dx, dw, db) in enumerate((_conv_bwd(gv, wg, da * cv * _dsilu(cg)), _conv_bwd(vv, wv, da * _silu(cg)))):
            dx_ref[half] = dx.astype(BF16)
            for k in range(3):
                dw_ref[half, k:k + 1, :] = dw[k]
            db_ref[half] = db

    def wspec(off):
        return pl.BlockSpec((3, tc), lambda j: (0, j + off))

    def bspec(off):
        return pl.BlockSpec((1, tc), lambda j: (0, j + off))

    cs = _col_spec(S, tc)
    both = lambda r: pl.BlockSpec((2, r, tc), lambda j: (0, 0, j))
    return pl.pallas_call(
        body, grid=(nb,), in_specs=[cs, _col_spec(S, tc, nb), cs, wspec(0), wspec(nb), bspec(0), bspec(nb)],
        out_specs=(both(S), both(3), both(1)), out_shape=(_sds((2, S, FFN), BF16), _sds((2, 3, FFN)), _sds((2, 1, FFN))),
        name="ffn_act_bwd", compiler_params=_PAR)(up, up, dact, cw, cw, cb, cb)


def _window_sum(x, w, up=False):
    shift = _shift_up if up else _shift_down
    j = 1
    while j < w:
        x = x + shift(x, j)
        j *= 2
    return x


def _pool_fwd(proj, pool_w, pool_scale):
    S = proj.shape[0]

    def body(u_ref, w_ref, s_ref, o_ref):
        cnt_row = (_iota((S, 128), 0) + 1).astype(F32)
        for g, w in enumerate(POOL_WINDOWS):
            sl = slice(g * 128, (g + 1) * 128)
            u = u_ref[:, sl]
            pooled = _window_sum(u, w) / jnp.minimum(cnt_row, float(w)) - u
            mixed = _dot(pooled.astype(BF16), w_ref[g].astype(BF16), _NN)
            o_ref[:, sl] = (mixed * s_ref[:, sl]).astype(BF16)

    return pl.pallas_call(
        body, grid=(1,),
        in_specs=[pl.BlockSpec((S, 512), lambda i: (0, OP // 512)), pl.BlockSpec((4, 128, 128), lambda i: (0, 0, 0)), _vec_spec(512)],
        out_specs=pl.BlockSpec((S, 512), lambda i: (0, 0)), out_shape=_sds((S, 512), BF16), name="pool_fwd",
        compiler_params=_ARB)(proj, pool_w, pool_scale)


def _pool_bwd(proj, dob, pool_w, pool_scale):
    S = proj.shape[0]

    def body(u_ref, d_ref, w_ref, s_ref, du_ref, dw_ref, ds_ref):
        cnt_row = (_iota((S, 128), 0) + 1).astype(F32)
        for g, w in enumerate(POOL_WINDOWS):
            sl = slice(g * 128, (g + 1) * 128)
            u, dv, wv = u_ref[:, sl], d_ref[:, sl], w_ref[g].astype(BF16)
            cnt = jnp.minimum(cnt_row, float(w))
            pooled = (_window_sum(u, w) / cnt - u).astype(BF16)
            ds_ref[:, sl] = jnp.sum(dv * _dot(pooled, wv, _NN), 0, keepdims=True)
            dmix = (dv * s_ref[:, sl]).astype(BF16)
            dw_ref[g] = _dot(pooled, dmix, _TN)
            dp = _dot(dmix, wv, _NT)
            du_ref[:, sl] = (_window_sum(dp / cnt, w, up=True) - dp).astype(BF16)

    blk = pl.BlockSpec((S, 512), lambda i: (0, 0))
    wspec = pl.BlockSpec((4, 128, 128), lambda i: (0, 0, 0))
    return pl.pallas_call(
        body, grid=(1,), in_specs=[pl.BlockSpec((S, 512), lambda i: (0, OP // 512)), blk, wspec, _vec_spec(512)],
        out_specs=(blk, wspec, _vec_spec(512)), out_shape=(_sds((S, 512), BF16), _sds((4, 128, 128)), _sds((1, 512))),
        name="pool_bwd", compiler_params=_ARB)(proj, dob, pool_w, pool_scale)


def _branch_specs():
    return [pl.BlockSpec((512, D), lambda i: (0, 0)), pl.BlockSpec((512, D), lambda i: (1, 0)), pl.BlockSpec((1024, D), lambda i: (1, 0))]


def _merge_fwd(oa, ob, oc, proj, wbr, tm=256):
    S = oa.shape[0]

    def body(oa_ref, ob_ref, oc_ref, gl_ref, wa_ref, wb_ref, wc_ref, o_ref):
        acc = _sigmoid(gl_ref[:, 0:D]) * _dot(oa_ref[...], wa_ref[...], _NN)
        acc += _sigmoid(gl_ref[:, D:2 * D]) * _dot(ob_ref[...], wb_ref[...], _NN)
        acc += _sigmoid(gl_ref[:, 2 * D:3 * D]) * _dot(oc_ref[...], wc_ref[...], _NN)
        o_ref[...] = acc.astype(BF16)

    return pl.pallas_call(
        body, grid=(S // tm,),
        in_specs=[_row_spec(tm, 512), _row_spec(tm, 512), _row_spec(tm, D), _row_spec(tm, 3 * D)] + _branch_specs(),
        out_specs=_row_spec(tm, D), out_shape=_sds((S, D), BF16), name="merge_fwd", compiler_params=_PAR)(oa, ob, oc, proj, wbr, wbr, wbr)


def _merge_bwd(dm, oa, ob, oc, proj, wbr, tm=256):
    S = oa.shape[0]

    def body(dm_ref, oa_ref, ob_ref, oc_ref, gl_ref, wa_ref, wb_ref, wc_ref, dya_ref, dyb_ref, dyc_ref, dgl_ref, doa_ref, dob_ref, doc_ref):
        dmv = dm_ref[...]
        for i, (o_ref, w_ref, dy_ref, do_ref) in enumerate(
                ((oa_ref, wa_ref, dya_ref, doa_ref), (ob_ref, wb_ref, dyb_ref, dob_ref), (oc_ref, wc_ref, dyc_ref, doc_ref))):
            gt = _sigmoid(gl_ref[:, i * D:(i + 1) * D])
            wv = w_ref[...]
            yv = _dot(o_ref[...], wv, _NN)
            dy = (dmv * gt).astype(BF16)
            dy_ref[...] = dy
            dgl_ref[:, i * D:(i + 1) * D] = (dmv * yv * gt * (1.0 - gt)).astype(BF16)
            do_ref[...] = _dot(dy, wv, _NT)

    rs = _row_spec
    return pl.pallas_call(
        body, grid=(S // tm,),
        in_specs=[rs(tm, D), rs(tm, 512), rs(tm, 512), rs(tm, D), rs(tm, 3 * D)] + _branch_specs(),
        out_specs=(rs(tm, D), rs(tm, D), rs(tm, D), rs(tm, 3 * D), rs(tm, 512), rs(tm, 512), rs(tm, D)),
        out_shape=(_sds((S, D), BF16), _sds((S, D), BF16), _sds((S, D), BF16), _sds((S, 3 * D), BF16), _sds((S, 512)), _sds((S, 512)), _sds((S, D))),
        name="merge_bwd", compiler_params=_PAR)(dm, oa, ob, oc, proj, wbr, wbr, wbr)


def _rope_tab(posb, invf, tm=256):
    S = posb.shape[0]

    def body(p_ref, f_ref, c_ref, s1_ref, s2_ref):
        ang = p_ref[...] * f_ref[...]
        lane = _iota(ang.shape, 1)
        cs, sn = jnp.cos(ang), jnp.sin(ang)
        c_ref[...] = jnp.where(lane < NOPE, 1.0, cs)
        s1_ref[...] = jnp.where((lane >= 64) & (lane < 80), -sn, 0.0)
        s2_ref[...] = jnp.where((lane >= 80) & (lane < 96), sn, 0.0)

    rs = _row_spec(tm, 128)
    return pl.pallas_call(body, grid=(S // tm,), in_specs=[rs, _vec_spec(128)], out_specs=(rs, rs, rs),
                          out_shape=(_sds((S, 128)),) * 3, name="rope_tab", compiler_params=_PAR)(posb, invf)


def _rope(u, C, S1, S2):
    return u * C + pltpu.roll(u, 112, 1) * S1 + pltpu.roll(u, 16, 1) * S2


def _rope_t(dy, C, S1, S2):
    return dy * C + pltpu.roll(dy * S1, 16, 1) + pltpu.roll(dy * S2, 112, 1)


def _seg_sum(v, mask):
    return jnp.sum(jnp.where(mask, v, 0.0), -1, keepdims=True)


def _mla_latents(pq_ref, pkv_ref, wqb_ref, wkvb_ref, qan_ref, kvan_ref):
    ql, kvl = pq_ref[...], pkv_ref[...]
    ckv, kr = kvl[:, 0:KV_RANK], kvl[:, KV_RANK:KV_RANK + 128]
    rq = lax.rsqrt(jnp.mean(ql * ql, -1, keepdims=True) + EPS)
    rkv = lax.rsqrt(jnp.mean(ckv * ckv, -1, keepdims=True) + EPS)
    nq = (ql * rq * qan_ref[...]).astype(BF16)
    nkv = (ckv * rkv * kvan_ref[...]).astype(BF16)
    kv = jnp.concatenate([_dot(nkv, wkvb_ref[k], _NN) for k in range(4)], 1)
    return ql, ckv, kr, rq, rkv, nq, nkv, _dot(nq, wqb_ref[...], _NN), kv


def _mla_specs(tm):
    full = lambda r, n: pl.BlockSpec((r, n), lambda i: (0, 0))
    return ([pl.BlockSpec((tm, 384), lambda i: (i, OQ // 384)), pl.BlockSpec((tm, 384), lambda i: (i, OKV // 384))],
            [full(Q_RANK, D), pl.BlockSpec((4, KV_RANK, 256), lambda i: (0, 0, 0)), _vec_spec(Q_RANK), _vec_spec(KV_RANK), _vec_spec(128), _vec_spec(128)]
            + [_row_spec(tm, 128)] * 3)


def _mla_prep(proj, wqb, wkvb, qan, kvan, wq, wk, rope, tm=256):
    S = proj.shape[0]

    def body(pq_ref, pkv_ref, wqb_ref, wkvb_ref, qan_ref, kvan_ref, wq_ref, wk_ref, c_ref, s1_ref, s2_ref, qf_ref, kf_ref, ve_ref):
        _, _, kr, _, _, _, _, q, kv = _mla_latents(pq_ref, pkv_ref, wqb_ref, wkvb_ref, qan_ref, kvan_ref)
        C, S1, S2, wqv, wkv = c_ref[...], s1_ref[...], s2_ref[...], wq_ref[...], wk_ref[...]
        lane = _iota((tm, 128), 1)
        mn, mr = lane < 64, (lane >= 64) & (lane < 96)
        rrk = lax.rsqrt(_seg_sum(kr * kr, mr) / ROPE_DIM + EPS)
        ykr = _rope(jnp.where(mr, kr * rrk * wkv, 0.0), C, S1, S2)
        for h in range(N_HEADS):
            sl = slice(h * 128, (h + 1) * 128)
            t = q[:, sl]
            rn = lax.rsqrt(_seg_sum(t * t, mn) / NOPE + EPS)
            rr = lax.rsqrt(_seg_sum(t * t, mr) / ROPE_DIM + EPS)
            qf_ref[:, sl] = _rope(t * jnp.where(mn, rn, jnp.where(mr, rr, 0.0)) * wqv, C, S1, S2).astype(BF16)
            t = kv[:, sl]
            rn = lax.rsqrt(_seg_sum(t * t, mn) / NOPE + EPS)
            kf_ref[:, sl] = (jnp.where(mn, t * rn * wkv, 0.0) + ykr).astype(BF16)
            ve_ref[:, sl] = (jnp.where(mn, pltpu.roll(t, 64, 1), 0.0) if h % 2 == 0 else jnp.where(mn, 0.0, t)).astype(BF16)

    pspecs, wspecs = _mla_specs(tm)
    rs = _row_spec(tm, D)
    return pl.pallas_call(body, grid=(S // tm,), in_specs=pspecs + wspecs, out_specs=(rs, rs, rs),
                          out_shape=(_sds((S, D), BF16),) * 3, name="mla_prep", compiler_params=_PAR)(
        proj, proj, wqb, wkvb, qan, kvan, wq, wk, *rope)


def _mla_prep_bwd(proj, dqf, dkf, dve, wqb, wkvb, qan, kvan, wq, wk, rope, tm=256):
    S = proj.shape[0]

    def body(pq_ref, pkv_ref, wqb_ref, wkvb_ref, qan_ref, kvan_ref, wq_ref, wk_ref, c_ref, s1_ref, s2_ref,
             dqf_ref, dkf_ref, dve_ref, dqs_ref, dkvs_ref, dwqb_ref, dwkvb_ref, st_ref, dq_scr, dkv_scr):
        @pl.when(pl.program_id(0) == 0)
        def _():
            dwqb_ref[...] = jnp.zeros_like(dwqb_ref)
            dwkvb_ref[...] = jnp.zeros_like(dwkvb_ref)
            st_ref[...] = jnp.zeros_like(st_ref)

        ql, ckv, kr, rq, rkv, nq, nkv, q, kv = _mla_latents(pq_ref, pkv_ref, wqb_ref, wkvb_ref, qan_ref, kvan_ref)
        C, S1, S2, wqv, wkv = c_ref[...], s1_ref[...], s2_ref[...], wq_ref[...], wk_ref[...]
        lane = _iota((tm, 128), 1)
        mn, mr = lane < 64, (lane >= 64) & (lane < 96)
        dwq = jnp.zeros((1, 128), F32)
        dwk = jnp.zeros((1, 128), F32)
        dykr = jnp.zeros((tm, 128), F32)
        for h in range(N_HEADS):
            sl = slice(h * 128, (h + 1) * 128)
            t = q[:, sl]
            rn = lax.rsqrt(_seg_sum(t * t, mn) / NOPE + EPS)
            rr = lax.rsqrt(_seg_sum(t * t, mr) / ROPE_DIM + EPS)
            scale = jnp.where(mn, rn, jnp.where(mr, rr, 0.0))
            that = t * scale
            du = _rope_t(dqf_ref[:, sl], C, S1, S2)
            dwq += jnp.sum(du * that, 0, keepdims=True)
            g = du * wqv
            gt = g * that
            dq_scr[:, sl] = scale * (g - that * jnp.where(mn, _seg_sum(gt, mn) / NOPE, _seg_sum(gt, mr) / ROPE_DIM))
            t = kv[:, sl]
            rn = lax.rsqrt(_seg_sum(t * t, mn) / NOPE + EPS)
            that = jnp.where(mn, t * rn, 0.0)
            dkf = dkf_ref[:, sl]
            dykr += jnp.where(mr, dkf, 0.0)
            dkn = jnp.where(mn, dkf, 0.0)
            dwk += jnp.sum(dkn * that, 0, keepdims=True)
            g = dkn * wkv
            dve = dve_ref[:, sl]
            dkv_scr[:, sl] = jnp.where(mn, rn * (g - that * (jnp.sum(g * that, -1, keepdims=True) / NOPE)),
                                       pltpu.roll(dve, 64, 1) if h % 2 == 0 else dve)
        rrk = lax.rsqrt(_seg_sum(kr * kr, mr) / ROPE_DIM + EPS)
        that = jnp.where(mr, kr * rrk, 0.0)
        dukr = jnp.where(mr, _rope_t(dykr, C, S1, S2), 0.0)
        dwk += jnp.sum(dukr * that, 0, keepdims=True)
        g = dukr * wkv
        dkr = rrk * (g - that * (jnp.sum(g * that, -1, keepdims=True) / ROPE_DIM))
        dqv, dkvv = dq_scr[...].astype(BF16), dkv_scr[...].astype(BF16)
        dnq = _dot(dqv, wqb_ref[...], _NT)
        dwqb_ref[...] += _dot(nq, dqv, _TN)
        dnkv = jnp.zeros((tm, KV_RANK), F32)
        for k in range(4):
            dnkv += _dot(dkvv[:, k * 256:(k + 1) * 256], wkvb_ref[k], _NT)
            dwkvb_ref[k] += _dot(nkv, dkvv[:, k * 256:(k + 1) * 256], _TN)
        xhat = ql * rq
        st_ref[0:1, 0:Q_RANK] += jnp.sum(dnq * xhat, 0, keepdims=True)
        g = dnq * qan_ref[...]
        dqs_ref[...] = (rq * (g - xhat * jnp.mean(g * xhat, -1, keepdims=True))).astype(BF16)
        xhat = ckv * rkv
        st_ref[1:2, 0:KV_RANK] += jnp.sum(dnkv * xhat, 0, keepdims=True)
        g = dnkv * kvan_ref[...]
        dkvs_ref[:, 0:KV_RANK] = (rkv * (g - xhat * jnp.mean(g * xhat, -1, keepdims=True))).astype(BF16)
        dkvs_ref[:, KV_RANK:KV_RANK + 128] = dkr.astype(BF16)
        st_ref[2:3, 0:128] += dwq
        st_ref[3:4, 0:128] += dwk

    pspecs, wspecs = _mla_specs(tm)
    rs = _row_spec(tm, D)
    full = lambda r, n: pl.BlockSpec((r, n), lambda i: (0, 0))
    return pl.pallas_call(
        body, grid=(S // tm,), in_specs=pspecs + wspecs + [rs, rs, rs],
        out_specs=(_row_spec(tm, 384), _row_spec(tm, 384), full(Q_RANK, D), pl.BlockSpec((4, KV_RANK, 256), lambda i: (0, 0, 0)), full(8, D)),
        out_shape=(_sds((S, 384), BF16), _sds((S, 384), BF16), _sds((Q_RANK, D)), _sds((4, KV_RANK, 256)), _sds((8, D))),
        scratch_shapes=[pltpu.VMEM((tm, D), F32), pltpu.VMEM((tm, D), F32)], name="mla_prep_bwd", compiler_params=_ARB)(
        proj, proj, wqb, wkvb, qan, kvan, wq, wk, *rope, dqf, dkf, dve)


_ATT_SCALE = (NOPE + ROPE_DIM) ** -0.5


def _att_probs(q, k, i, tq):
    n = k.shape[0]
    s = _dot(q, k, _NT) * _ATT_SCALE
    s = jnp.where(_iota((tq, n), 1) <= i * tq + _iota((tq, n), 0), s, -1e30)
    p = jnp.exp(s - jnp.max(s, -1, keepdims=True))
    return p / jnp.sum(p, -1, keepdims=True)


def _attn_fwd(qf, kf, ve, tq=256):
    S = qf.shape[0]

    def body(q_ref, k_ref, v_ref, o_ref):
        for i in range(S // tq):
            n, rows = (i + 1) * tq, slice(i * tq, (i + 1) * tq)
            acc = jnp.zeros((tq, 128), F32)
            for hh in range(2):
                sl = slice(hh * 128, (hh + 1) * 128)
                p = _att_probs(q_ref[rows, sl], k_ref[0:n, sl], i, tq)
                acc += _dot(p.astype(BF16), v_ref[0:n, sl], _NN)
            o_ref[rows, :] = acc.astype(BF16)

    ps = pl.BlockSpec((S, 256), lambda h: (0, h))
    return pl.pallas_call(body, grid=(N_HEADS // 2,), in_specs=[ps, ps, ps], out_specs=pl.BlockSpec((S, 128), lambda h: (0, h)),
                          out_shape=_sds((S, 512), BF16), name="attn_fwd", compiler_params=_PAR)(qf, kf, ve)


def _attn_bwd(qf, kf, ve, do, tq=256):
    S = qf.shape[0]

    def body(q_ref, k_ref, v_ref, do_ref, dq_ref, dk_ref, dv_ref):
        dk_ref[...] = jnp.zeros_like(dk_ref)
        dv_ref[...] = jnp.zeros_like(dv_ref)
        for i in range(S // tq):
            n, rows = (i + 1) * tq, slice(i * tq, (i + 1) * tq)
            dob = do_ref[rows, :].astype(BF16)
            for hh in range(2):
                sl = slice(hh * 128, (hh + 1) * 128)
                q, k = q_ref[rows, sl], k_ref[0:n, sl]
                p = _att_probs(q, k, i, tq)
                dv_ref[0:n, sl] += _dot(p.astype(BF16), dob, _TN)
                dp = _dot(dob, v_ref[0:n, sl], _NT)
                ds = (p * (dp - jnp.sum(dp * p, -1, keepdims=True)) * _ATT_SCALE).astype(BF16)
                dq_ref[rows, sl] = _dot(ds, k, _NN)
                dk_ref[0:n, sl] += _dot(ds, q, _TN)

    ps = pl.BlockSpec((S, 256), lambda h: (0, h))
    return pl.pallas_call(body, grid=(N_HEADS // 2,), in_specs=[ps, ps, ps, pl.BlockSpec((S, 128), lambda h: (0, h))], out_specs=(ps, ps, ps),
                          out_shape=(_sds((S, D)),) * 3, name="attn_bwd", compiler_params=_PAR)(qf, kf, ve, do)


def _softplus(x):
    return jnp.maximum(x, 0.0) + jnp.log1p(jnp.exp(-jnp.abs(x)))


def _ssd_chunk(xc_ref, dtr_ref, dtb_ref, al_ref, e_ref):
    L = SSD_L
    a = -jnp.exp(al_ref[...])
    dtp = _softplus(dtr_ref[...] + dtb_ref[...])
    causal = _iota((L, L), 1) <= _iota((L, L), 0)
    cs = _dot_hi(causal.astype(F32), dtp * a)
    E = e_ref[...]
    dtx, csx = _dot_hi(dtp, E), _dot_hi(cs, E)
    X = xc_ref[:, 0:SSD_INNER]
    Xd = X * dtx
    dec_out = jnp.exp(csx)
    dec_st = jnp.exp(csx[L - 1:L, :] - csx)
    return a, dtp, causal, cs, cs.T, dtx, X, Xd, dec_out, dec_st


def _ssd_decay(causal, cs, cs_row, h):
    diff = cs[:, h:h + 1] - cs_row[h:h + 1, :]
    return jnp.where(causal, jnp.exp(jnp.where(causal, diff, 0.0)), 0.0)


def _ssd_groups(xc_ref, g):
    b0, c0 = SSD_INNER + g * SSD_N, SSD_INNER + 2 * SSD_N + g * SSD_N
    return xc_ref[:, b0:b0 + SSD_N].astype(BF16), xc_ref[:, c0:c0 + SSD_N].astype(BF16)


def _pair_decay(cs, pair):
    L = SSD_L
    return jnp.where(_iota((128, 128), 0) < 64, jnp.exp(cs[L - 1:L, 2 * pair:2 * pair + 1]), jnp.exp(cs[L - 1:L, 2 * pair + 1:2 * pair + 2]))


def _ssd_in_specs(nc, rev):
    idx = (lambda c: nc - 1 - c) if rev else (lambda c: c)
    return [pl.BlockSpec((SSD_L, SSD_CONV_DIM), lambda c: (idx(c), 0)), pl.BlockSpec((SSD_L, 128), lambda c: (idx(c), ODT // 128)),
            _vec_spec(128), _vec_spec(128), _vec_spec(SSD_INNER), pl.BlockSpec((128, SSD_INNER), lambda c: (0, 0))]


def _ssd_core(xc, proj, dtb, alog, dskip, E):
    S = xc.shape[0]
    nc = S // SSD_L

    def body(xc_ref, dtr_ref, dtb_ref, al_ref, dx_ref, e_ref, y_ref, hp_ref, h_scr):
        @pl.when(pl.program_id(0) == 0)
        def _():
            h_scr[...] = jnp.zeros_like(h_scr)

        hp_ref[0] = h_scr[...]
        _, _, causal, cs, cs_row, _, X, Xd, dec_out, dec_st = _ssd_chunk(xc_ref, dtr_ref, dtb_ref, al_ref, e_ref)
        Xs = Xd * dec_st
        lane = _iota((SSD_L, 128), 1)
        for g in range(2):
            Bg, Cg = _ssd_groups(xc_ref, g)
            CB = _dot(Cg, Bg, _NT)
            for pr in range(4):
                pair = g * 4 + pr
                psl = slice(pair * 128, (pair + 1) * 128)
                Xdp = Xd[:, psl].astype(BF16)
                r0 = _dot((CB * _ssd_decay(causal, cs, cs_row, 2 * pair)).astype(BF16), Xdp, _NN)
                r1 = _dot((CB * _ssd_decay(causal, cs, cs_row, 2 * pair + 1)).astype(BF16), Xdp, _NN)
                Hp = h_scr[psl, :]
                W = _dot(Cg, Hp.astype(BF16), _NT)
                y_ref[:, psl] = jnp.where(lane < 64, r0, r1) + W * dec_out[:, psl] + X[:, psl] * dx_ref[:, psl]
                h_scr[psl, :] = Hp * _pair_decay(cs, pair) + _dot(Xs[:, psl].astype(BF16), Bg, _TN)

    return pl.pallas_call(
        body, grid=(nc,), in_specs=_ssd_in_specs(nc, False),
        out_specs=(pl.BlockSpec((SSD_L, SSD_INNER), lambda c: (c, 0)), pl.BlockSpec((1, SSD_INNER, SSD_N), lambda c: (c, 0, 0))),
        out_shape=(_sds((S, SSD_INNER)), _sds((nc, SSD_INNER, SSD_N))), scratch_shapes=[pltpu.VMEM((SSD_INNER, SSD_N), F32)],
        name="ssd_core", compiler_params=_ARB)(xc, proj, dtb, alog, dskip, E)


def _ssd_core_bwd(xc, proj, hprev, dy, dtb, alog, dskip, E, ET):
    S = xc.shape[0]
    nc = S // SSD_L
    L = SSD_L

    def body(xc_ref, dtr_ref, dtb_ref, al_ref, dx_ref, e_ref, et_ref, hp_ref, dy_ref, dxc_ref, ddt_ref, st_ref, dh_scr, acc_scr):
        step = pl.program_id(0)

        @pl.when(step == 0)
        def _():
            dh_scr[...] = jnp.zeros_like(dh_scr)
            acc_scr[...] = jnp.zeros_like(acc_scr)
            st_ref[...] = jnp.zeros_like(st_ref)

        a, dtp, causal, cs, cs_row, dtx, X, Xd, dec_out, dec_st = _ssd_chunk(xc_ref, dtr_ref, dtb_ref, al_ref, e_ref)
        lane = _iota((L, 128), 1)
        sub = _iota((128, L), 0)
        dcs_col = jnp.zeros((L, 128), F32)
        dcs_row = jnp.zeros((128, L), F32)
        dcs_last = jnp.zeros((1, 128), F32)
        dcsx, ddtx, dlastx = [], [], []
        for g in range(2):
            Bg, Cg = _ssd_groups(xc_ref, g)
            CB = _dot(Cg, Bg, _NT)
            dCB = jnp.zeros((L, L), F32)
            dB = jnp.zeros((L, SSD_N), F32)
            dC = jnp.zeros((L, SSD_N), F32)
            for pr in range(4):
                pair = g * 4 + pr
                psl = slice(pair * 128, (pair + 1) * 128)
                dY, Xp, Xdp, dop, dsp = dy_ref[:, psl], X[:, psl], Xd[:, psl], dec_out[:, psl], dec_st[:, psl]
                Xdb = Xdp.astype(BF16)
                acc_scr[0:1, psl] += jnp.sum(dY * Xp, 0, keepdims=True)
                Hp = hp_ref[0, psl, :]
                Hb = Hp.astype(BF16)
                dW = (dY * dop).astype(BF16)
                dcx = dY * _dot(Cg, Hb, _NT) * dop
                dC += _dot(dW, Hb, _NN)
                dHp = _dot(dW, Cg, _TN)
                dHn = dh_scr[psl, :]
                cd = _pair_decay(cs, pair)
                dh_scr[psl, :] = dHp + dHn * cd
                rsum = jnp.sum(dHn * Hp * cd, -1, keepdims=True)
                half = _iota((128, 1), 0) < 64
                s0 = jnp.sum(jnp.where(half, rsum, 0.0), 0, keepdims=True)
                s1 = jnp.sum(jnp.where(half, 0.0, rsum), 0, keepdims=True)
                lane1 = _iota((1, 128), 1)
                dcs_last += jnp.where(lane1 == 2 * pair, s0, 0.0) + jnp.where(lane1 == 2 * pair + 1, s1, 0.0)
                dHb = dHn.astype(BF16)
                dXs = _dot(Bg, dHb, _NT)
                dB += _dot((Xdp * dsp).astype(BF16), dHb, _NN)
                dXd = dXs * dsp
                e_st = dXs * Xdp * dsp
                dcx -= e_st
                dlastx.append(jnp.sum(e_st, 0, keepdims=True))
                for i in range(2):
                    h = 2 * pair + i
                    Dm = _ssd_decay(causal, cs, cs_row, h)
                    M = CB * Dm
                    dYm = jnp.where((lane < 64) if i == 0 else (lane >= 64), dY, 0.0).astype(BF16)
                    dM = _dot(dYm, Xdb, _NT)
                    dXd += _dot(M.astype(BF16), dYm, _TN)
                    dCB += dM * Dm
                    Em = dM * M
                    dcs_col += jnp.where(lane == h, jnp.sum(Em, -1, keepdims=True), 0.0)
                    dcs_row += jnp.where(sub == h, jnp.sum(Em, 0, keepdims=True), 0.0)
                dxc_ref[:, psl] = dY * dx_ref[:, psl] + dXd * dtx[:, psl]
                ddtx.append(dXd * Xp)
                dcsx.append(dcx)
            dCBb = dCB.astype(BF16)
            b0, c0 = SSD_INNER + g * SSD_N, SSD_INNER + 2 * SSD_N + g * SSD_N
            dxc_ref[:, b0:b0 + SSD_N] = dB + _dot(dCBb, Cg, _TN)
            dxc_ref[:, c0:c0 + SSD_N] = dC + _dot(dCBb, Bg, _NN)
        ET = et_ref[...]
        dcs = dcs_col - dcs_row.T + _dot_hi(jnp.concatenate(dcsx, 1), ET)
        dlast = dcs_last + _dot_hi(jnp.broadcast_to(jnp.concatenate(dlastx, 1), (8, SSD_INNER)), ET)[0:1, :]
        dcs += jnp.where(_iota((L, 128), 0) == L - 1, dlast, 0.0)
        dda = _dot_hi((_iota((L, L), 1) >= _iota((L, L), 0)).astype(F32), dcs)
        ddtp = dda * a + _dot_hi(jnp.concatenate(ddtx, 1), ET)
        draw = ddtp * _sigmoid(dtr_ref[...] + dtb_ref[...])
        ddt_ref[...] = draw.astype(BF16)
        st_ref[0:1, :] += jnp.sum(draw, 0, keepdims=True)
        st_ref[1:2, :] += jnp.sum(dda * dtp, 0, keepdims=True) * a

        @pl.when(step == nc - 1)
        def _():
            st_ref[2:3, :] = _dot_hi(acc_scr[...], ET)[0:1, :]

    rev = lambda c: (nc - 1 - c, 0)
    return pl.pallas_call(
        body, grid=(nc,),
        in_specs=_ssd_in_specs(nc, True) + [pl.BlockSpec((SSD_INNER, 128), lambda c: (0, 0)),
                                            pl.BlockSpec((1, SSD_INNER, SSD_N), lambda c: (nc - 1 - c, 0, 0)),
                                            pl.BlockSpec((L, SSD_INNER), rev)],
        out_specs=(pl.BlockSpec((L, SSD_CONV_DIM), rev), pl.BlockSpec((L, 128), rev), pl.BlockSpec((8, 128), lambda c: (0, 0))),
        out_shape=(_sds((S, SSD_CONV_DIM)), _sds((S, 128), BF16), _sds((8, 128))),
        scratch_shapes=[pltpu.VMEM((SSD_INNER, SSD_N), F32), pltpu.VMEM((8, SSD_INNER), F32)],
        name="ssd_core_bwd", compiler_params=_ARB)(xc, proj, dtb, alog, dskip, E, ET, hprev, dy)


def _ssd_post(y, proj, nw, tm=256):
    S = y.shape[0]

    def body(y_ref, z_ref, nw_ref, o_ref):
        for g in range(2):
            sl = slice(g * 512, (g + 1) * 512)
            gated = y_ref[:, sl] * _silu(z_ref[:, sl])
            r = lax.rsqrt(jnp.mean(gated * gated, -1, keepdims=True) + EPS)
            o_ref[:, sl] = (gated * r * nw_ref[:, sl]).astype(BF16)

    return pl.pallas_call(
        body, grid=(S // tm,), in_specs=[_row_spec(tm, D), pl.BlockSpec((tm, D), lambda i: (i, OZ // D)), _vec_spec(D)],
        out_specs=_row_spec(tm, D), out_shape=_sds((S, D), BF16), name="ssd_post", compiler_params=_PAR)(y, proj, nw)


def _ssd_post_bwd(doc, y, proj, nw, tm=256):
    S = y.shape[0]

    def body(d_ref, y_ref, z_ref, nw_ref, dy_ref, dz_ref, st_ref):
        @pl.when(pl.program_id(0) == 0)
        def _():
            st_ref[...] = jnp.zeros_like(st_ref)

        for g in range(2):
            sl = slice(g * 512, (g + 1) * 512)
            yv, zv, dv = y_ref[:, sl], z_ref[:, sl], d_ref[:, sl]
            sz = _silu(zv)
            gated = yv * sz
            r = lax.rsqrt(jnp.mean(gated * gated, -1, keepdims=True) + EPS)
            ghat = gated * r
            st_ref[0:1, sl] += jnp.sum(dv * ghat, 0, keepdims=True)
            gg = dv * nw_ref[:, sl]
            dg = r * (gg - ghat * jnp.mean(gg * ghat, -1, keepdims=True))
            dy_ref[:, sl] = dg * sz
            dz_ref[:, sl] = (dg * yv * _dsilu(zv)).astype(BF16)

    zs = pl.BlockSpec((tm, D), lambda i: (i, OZ // D))
    return pl.pallas_call(
        body, grid=(S // tm,), in_specs=[_row_spec(tm, D), _row_spec(tm, D), zs, _vec_spec(D)],
        out_specs=(_row_spec(tm, D), _row_spec(tm, D), _vec_spec(D, 8)), out_shape=(_sds((S, D)), _sds((S, D), BF16), _sds((8, D))),
        name="ssd_post_bwd", compiler_params=_ARB)(doc, y, proj, nw)


def _row(v, n=None):
    v = v.astype(F32).reshape(1, -1)
    return v if n is None else jnp.pad(v, ((0, 0), (0, n - v.shape[1])))


def _prep_layer(p):
    w_in = jnp.transpose(p["w_in4"].astype(BF16), (1, 0, 2)).reshape(D, IN_DIM)
    z = lambda n: jnp.zeros((D, n), BF16)
    win = jnp.concatenate([w_in[:, 3760:6832], w_in[:, 1184:2208], w_in[:, 2208:3744], w_in[:, 672:1184], w_in[:, 0:384],
                           w_in[:, 384:640], z(64), w_in[:, 640:672], z(32), w_in[:, 3744:3760], z(112)], 1)
    wqb = jnp.transpose(p["w_q_b4"].astype(BF16).reshape(4, Q_RANK, 2, 96), (1, 0, 2, 3))
    return dict(
        win=win, wqb=jnp.pad(wqb, ((0, 0), (0, 0), (0, 0), (0, 32))).reshape(Q_RANK, D), wkvb4=p["w_kv_b4"].astype(BF16),
        wbr=p["w_branch"].astype(BF16), wo=p["w_out"].astype(BF16), wup4=p["ffn_up4"].astype(BF16), wdn=p["ffn_down"].astype(BF16),
        nw1=_row(p["norm1_w"]), nw2=_row(p["norm2_w"]), qan=_row(p["q_a_norm"]), kvan=_row(p["kv_a_norm"]),
        wq=_row(p["q_norm"], 128), wk=_row(p["k_norm"], 128), pool_w=p["pool_w"].astype(F32), pool_scale=_row(p["pool_scale"]),
        cw=p["ssd_conv_w"].astype(F32), cb=_row(p["ssd_conv_b"]), dtb=_row(p["ssd_dt_bias"], 128), alog=_row(p["ssd_a_log"], 128),
        dskip=_row(jnp.repeat(p["ssd_d"].astype(F32), SSD_P)), snw=_row(p["ssd_norm_w"]),
        fcw=p["ffn_conv_w"].astype(F32), fcb=_row(p["ffn_conv_b"]))


def _layer_fwd(x, mod8, W, rope, E, tag):
    sh1, sc1, g1, sh2, sc2, g2 = (mod8[i:i + 1] for i in range(6))
    h1 = _ln_mod(x, W["nw1"], sc1, sh1, name=f"ln1_{tag}")
    proj = _mm(h1, W["win"], tn=640, tk=1024, name=f"proj_{tag}")
    qf, kf, ve = _mla_prep(proj, W["wqb"], W["wkvb4"], W["qan"], W["kvan"], W["wq"], W["wk"], rope)
    oa = _attn_fwd(qf, kf, ve)
    ob = _pool_fwd(proj, W["pool_w"], W["pool_scale"])
    xc = _ssd_pre(proj, W["cw"], W["cb"])
    y, hprev = _ssd_core(xc, proj, W["dtb"], W["alog"], W["dskip"], E)
    oc = _ssd_post(y, proj, W["snw"])
    merged = _merge_fwd(oa, ob, oc, proj, W["wbr"])
    x1, out1 = _mm(merged, W["wo"], tk=1024, res=x, gate=g1, name=f"wout_{tag}")
    h2 = _ln_mod(x1, W["nw2"], sc2, sh2, name=f"ln2_{tag}")
    up = _up_fwd(h2, W["wup4"], name=f"up_{tag}")
    act = _ffn_act(up, W["fcw"], W["fcb"])
    x2, out2 = _mm(act, W["wdn"], tk=1408, res=x1, gate=g2, name=f"down_{tag}")
    saved = dict(x=x, h1=h1, proj=proj, qf=qf, kf=kf, ve=ve, oa=oa, ob=ob, oc=oc, xc=xc, hprev=hprev, y=y, merged=merged,
                 out1=out1, x1=x1, h2=h2, up=up, act=act, out2=out2)
    return x2, saved


def _layer_bwd(dx2, sv, mod8, W, rope, E, ET, tag):
    sc1, g1, sc2, g2 = mod8[1:2], mod8[2:3], mod8[4:5], mod8[5:6]
    proj = sv["proj"]
    dz2, dg2 = _gate_bwd(dx2, sv["out2"], g2, name=f"gate2_bwd_{tag}")
    dact = _mm(dz2, W["wdn"], "nt", tn=1408, tk=1024, name=f"down_dx_{tag}")
    dwdn = _mm(sv["act"], dz2, "tn", tm=1408, name=f"down_dw_{tag}")
    dup2, dfcw, dfcb = _ffn_act_bwd(sv["up"], dact, W["fcw"], W["fcb"])
    dh2 = _up_dx(dup2, W["wup4"], name=f"up_dx_{tag}")
    dwup4 = _up_dw(sv["h2"], dup2, name=f"up_dw_{tag}")
    dx1, st2 = _ln_mod_bwd(sv["x1"], dh2, dx2, W["nw2"], sc2, name=f"ln2_bwd_{tag}")
    dz1, dg1 = _gate_bwd(dx1, sv["out1"], g1, name=f"gate1_bwd_{tag}")
    dmerged = _mm(dz1, W["wo"], "nt", tk=1024, name=f"wout_dx_{tag}")
    dwo = _mm(sv["merged"], dz1, "tn", name=f"wout_dw_{tag}")
    dya, dyb, dyc, dgl, doa, dob, doc = _merge_bwd(dmerged, sv["oa"], sv["ob"], sv["oc"], proj, W["wbr"])
    dwba = _mm(sv["oa"], dya, "tn", name=f"wba_dw_{tag}")
    dwbb = _mm(sv["ob"], dyb, "tn", name=f"wbb_dw_{tag}")
    dwbc = _mm(sv["oc"], dyc, "tn", name=f"wbc_dw_{tag}")
    dy, dzs, st_post = _ssd_post_bwd(doc, sv["y"], proj, W["snw"])
    dxc, ddt, st_ssd = _ssd_core_bwd(sv["xc"], proj, sv["hprev"], dy, W["dtb"], W["alog"], W["dskip"], E, ET)
    dxbc, dcw, dcb = _ssd_pre_bwd(proj, dxc, W["cw"], W["cb"])
    dpool, dpw, dps = _pool_bwd(proj, dob, W["pool_w"], W["pool_scale"])
    dqf, dkf, dve = _attn_bwd(sv["qf"], sv["kf"], sv["ve"], doa)
    dqs, dkvs, dwqb, dwkvb4, st_mla = _mla_prep_bwd(proj, dqf, dkf, dve, W["wqb"], W["wkvb4"], W["qan"], W["kvan"], W["wq"], W["wk"], rope)
    dproj = jnp.concatenate([dgl, dzs, dxbc, dpool, dqs, dkvs, ddt], 1)
    dh1 = _mm(dproj, W["win"], "nt", tk=1408, name=f"proj_dx_{tag}")
    dwin = _mm(sv["h1"], dproj, "tn", tn=640, name=f"proj_dw_{tag}")
    dx, st1 = _ln_mod_bwd(sv["x"], dh1, dx1, W["nw1"], sc1, name=f"ln1_bwd_{tag}")
    dw_in = jnp.concatenate([dwin[:, OQ:OQ + 384], dwin[:, OKV:OKV + 256], dwin[:, OKV + 320:OKV + 352], dwin[:, OP:OP + 512],
                             dwin[:, OZ:OZ + 1024], dwin[:, OX:OX + 1536], dwin[:, ODT:ODT + 16], dwin[:, OG:OG + 3072]], 1)
    grads = dict(
        norm1_w=st1[2], norm2_w=st2[2],
        w_in=jnp.transpose(dw_in.reshape(D, 4, IN_DIM // 4), (1, 0, 2)),
        q_a_norm=st_mla[0, :Q_RANK], kv_a_norm=st_mla[1, :KV_RANK], q_norm=st_mla[2, :96], k_norm=st_mla[3, :96],
        w_q_b=jnp.transpose(dwqb.reshape(Q_RANK, 4, 2, 128)[:, :, :, :96], (1, 0, 2, 3)).reshape(4, Q_RANK, 192), w_kv_b=dwkvb4,
        pool_w=dpw, pool_scale=dps[0], ssd_conv_w=dcw, ssd_conv_b=dcb[0],
        ssd_dt_bias=st_ssd[0, :SSD_HEADS], ssd_a_log=st_ssd[1, :SSD_HEADS], ssd_d=st_ssd[2, :SSD_HEADS], ssd_norm_w=st_post[0],
        w_branch=jnp.concatenate([dwba, dwbb, dwbc], 0).reshape(4, 512, D), w_out=dwo.reshape(4, 256, D), ffn_up=dwup4,
        ffn_conv_w=jnp.transpose(dfcw, (1, 0, 2)).reshape(3, 2 * FFN), ffn_conv_b=dfcb.reshape(2 * FFN), ffn_down=dwdn.reshape(4, FFN // 4, D))
    dmod = jnp.concatenate([st1[0:2], dg1[0:1], st2[0:2], dg2[0:1]], 0)
    return dx, grads, dmod


def _ssd_expand():
    E = (jnp.arange(SSD_INNER)[None, :] // SSD_P == jnp.arange(128)[:, None]).astype(F32)
    return E, E.T


def _rope_tables(positions):
    inv_freq = ROPE_THETA ** (-jnp.arange(0, ROPE_DIM, 2, dtype=F32) / ROPE_DIM)
    invf = jnp.concatenate([jnp.zeros((NOPE,), F32), inv_freq, inv_freq, jnp.zeros((32,), F32)]).reshape(1, 128)
    posb = jnp.broadcast_to(positions.astype(F32)[:, None], (positions.shape[0], 128))
    return _rope_tab(posb, invf)


def _local_step(x, target, positions, mods, layers):
    rope = _rope_tables(positions)
    E, ET = _ssd_expand()
    Ws = [_prep_layer(p) for p in layers]
    saved, h = [], x
    for l, W in enumerate(Ws):
        h, sv = _layer_fwd(h, mods[l], W, rope, E, l)
        saved.append(sv)
    dy, lpart = _loss_grad(h, target)
    grads, dmods = [None] * len(Ws), [None] * len(Ws)
    for l in reversed(range(len(Ws))):
        dy, grads[l], dmods[l] = _layer_bwd(dy, saved[l], mods[l], Ws[l], rope, E, ET, l)
    return lpart[0, 0], dy, grads, dmods


_ANY = pl.BlockSpec(memory_space=pl.ANY)
_VMEM = pl.BlockSpec(memory_space=pltpu.VMEM)


def _place():
    x, y, c = lax.axis_index("x"), lax.axis_index("y"), lax.axis_index("c")
    return x, y, c, [(1 - x, y), (x, 1 - y), (1 - x, 1 - y)]


def _allgather8(v, name):
    m_per, n = v.shape

    def body(x_ref, out_ref, send_sems, recv_sems, local_sem):
        x, y, c, chips = _place()
        me, sibling = (x, y, c), (x, y, 1 - c)

        def rows(px, py, pc):
            return out_ref.at[pl.ds((4 * px + 2 * py + pc) * m_per, m_per), :]

        def copy(k, block, to, src=None):
            return pltpu.make_async_remote_copy(src_ref=rows(*block) if src is None else src, dst_ref=rows(*block),
                                                send_sem=send_sems.at[k], recv_sem=recv_sems.at[k], device_id=to, device_id_type=MESH)

        mine = pltpu.make_async_copy(x_ref, rows(*me), local_sem)
        mine.start()
        first = [copy(0, me, sibling, src=x_ref)] + [copy(1 + j, me, (*chip, c), src=x_ref) for j, chip in enumerate(chips)]
        for cp in first:
            cp.start()
        passed = [copy(4 + j, (*chip, c), sibling) for j, chip in enumerate(chips)]
        for j, chip in enumerate(chips):
            copy(1 + j, (*chip, c), me).wait_recv()
            passed[j].start()
        copy(0, sibling, me).wait_recv()
        for j, chip in enumerate(chips):
            copy(4 + j, (*chip, 1 - c), me).wait_recv()
        for cp in first + passed:
            cp.wait_send()
        mine.wait()

    return pl.pallas_call(
        body, out_shape=_sds((8 * m_per, n), v.dtype), in_specs=[_VMEM], out_specs=_VMEM,
        scratch_shapes=[pltpu.SemaphoreType.DMA((7,)), pltpu.SemaphoreType.DMA((7,)), pltpu.SemaphoreType.DMA], name=name)(v)


def _sems(n):
    return [pltpu.SemaphoreType.DMA((n,)), pltpu.SemaphoreType.DMA((n,))]


def _gather_chips(arrs, name):
    na = len(arrs)

    def body(*refs):
        p_refs, o_refs, (send_sems, recv_sems, local_sems) = refs[:na], refs[na:2 * na], refs[2 * na:]
        x, y, c, chips = _place()
        k, sibling = 2 * x + y, (x, y, 1 - c)

        def copy(a, s, kk, cc, to, from_shard=False):
            r2 = p_refs[a].shape[0] // 2
            dst = o_refs[a].at[kk, pl.ds(cc * r2, r2), :]
            return pltpu.make_async_remote_copy(src_ref=p_refs[a].at[pl.ds(cc * r2, r2), :] if from_shard else dst, dst_ref=dst,
                                                send_sem=send_sems.at[6 * a + s], recv_sem=recv_sems.at[6 * a + s],
                                                device_id=to, device_id_type=MESH)

        own = [pltpu.make_async_copy(p_refs[a], o_refs[a].at[k], local_sems.at[a]) for a in range(na)]
        first = [copy(a, j, k, c, (*chip, c), from_shard=True) for a in range(na) for j, chip in enumerate(chips)]
        for cp in own + first:
            cp.start()
        passed = []
        for a in range(na):
            for j, chip in enumerate(chips):
                copy(a, j, 2 * chip[0] + chip[1], c, (x, y, c)).wait_recv()
                passed.append(copy(a, 3 + j, 2 * chip[0] + chip[1], c, sibling))
                passed[-1].start()
        for a in range(na):
            for j, chip in enumerate(chips):
                copy(a, 3 + j, 2 * chip[0] + chip[1], 1 - c, (x, y, c)).wait_recv()
        for cp in first + passed:
            cp.wait_send()
        for cp in own:
            cp.wait()

    return pl.pallas_call(
        body, out_shape=[_sds((4,) + v.shape, v.dtype) for v in arrs], in_specs=[_ANY] * na, out_specs=[_ANY] * na,
        scratch_shapes=_sems(6 * na) + [pltpu.SemaphoreType.DMA((na,))], name=name)(*arrs)


def _send_halves(gs, name):
    na = len(gs)

    def body(*refs):
        g_refs, o_refs, (send_sems, recv_sems) = refs[:na], refs[na:2 * na], refs[2 * na:]
        x, y, c, _ = _place()
        cps = []
        for a in range(na):
            r2 = g_refs[a].shape[1] // 2
            cps.append(pltpu.make_async_remote_copy(src_ref=g_refs[a].at[:, pl.ds((1 - c) * r2, r2), :], dst_ref=o_refs[a],
                                                    send_sem=send_sems.at[a], recv_sem=recv_sems.at[a],
                                                    device_id=(x, y, 1 - c), device_id_type=MESH))
        for cp in cps:
            cp.start()
        for cp in cps:
            cp.wait()

    return pl.pallas_call(
        body, out_shape=[_sds((4, v.shape[1] // 2, v.shape[2]), v.dtype) for v in gs], in_specs=[_ANY] * na, out_specs=[_ANY] * na,
        scratch_shapes=_sems(na), name=name)(*gs)


def _scatter_chips(arrs, name):
    na = len(arrs)

    def body(*refs):
        a_refs, t_refs, (send_sems, recv_sems, local_sems) = refs[:na], refs[na:2 * na], refs[2 * na:]
        x, y, c, chips = _place()
        k = 2 * x + y

        def copy(a, j, src_k, dst_k, to):
            return pltpu.make_async_remote_copy(src_ref=a_refs[a].at[src_k], dst_ref=t_refs[a].at[dst_k], send_sem=send_sems.at[3 * a + j],
                                                recv_sem=recv_sems.at[3 * a + j], device_id=to, device_id_type=MESH)

        own = [pltpu.make_async_copy(a_refs[a].at[k], t_refs[a].at[k], local_sems.at[a]) for a in range(na)]
        sends = [copy(a, j, 2 * chip[0] + chip[1], k, (*chip, c)) for a in range(na) for j, chip in enumerate(chips)]
        for cp in own + sends:
            cp.start()
        for a in range(na):
            for j, chip in enumerate(chips):
                copy(a, j, k, 2 * chip[0] + chip[1], (x, y, c)).wait_recv()
        for cp in sends:
            cp.wait_send()
        for cp in own:
            cp.wait()

    return pl.pallas_call(
        body, out_shape=[_sds(v.shape, v.dtype) for v in arrs], in_specs=[_ANY] * na, out_specs=[_ANY] * na,
        scratch_shapes=_sems(3 * na) + [pltpu.SemaphoreType.DMA((na,))], name=name)(*arrs)


def _join_halves(fs, name):
    na = len(fs)

    def body(*refs):
        f_refs, o_refs, (send_sems, recv_sems, local_sems) = refs[:na], refs[na:2 * na], refs[2 * na:]
        x, y, c, _ = _place()
        own, cps = [], []
        for a in range(na):
            r2 = f_refs[a].shape[0]
            dst = o_refs[a].at[pl.ds(c * r2, r2), :]
            own.append(pltpu.make_async_copy(f_refs[a], dst, local_sems.at[a]))
            cps.append(pltpu.make_async_remote_copy(src_ref=f_refs[a], dst_ref=dst, send_sem=send_sems.at[a], recv_sem=recv_sems.at[a],
                                                    device_id=(x, y, 1 - c), device_id_type=MESH))
        for cp in own + cps:
            cp.start()
        for a in range(na):
            r2 = f_refs[a].shape[0]
            pltpu.make_async_remote_copy(src_ref=f_refs[a], dst_ref=o_refs[a].at[pl.ds((1 - c) * r2, r2), :], send_sem=send_sems.at[a],
                                         recv_sem=recv_sems.at[a], device_id=(x, y, c), device_id_type=MESH).wait_recv()
        for cp in cps:
            cp.wait_send()
        for cp in own:
            cp.wait()

    return pl.pallas_call(
        body, out_shape=[_sds((2 * v.shape[0], v.shape[1]), v.dtype) for v in fs], in_specs=[_ANY] * na, out_specs=[_ANY] * na,
        scratch_shapes=_sems(na) + [pltpu.SemaphoreType.DMA((na,))], name=name)(*fs)


def _add_cast(g, recv, c, name):
    _, r2, n = recv.shape

    def body(c_ref, a_ref, b_ref, o_ref):
        o_ref[...] = (a_ref[...] + b_ref[...]).astype(BF16)

    spec = pl.BlockSpec((None, r2, n), lambda k, c_ref: (k, 0, 0))
    return pl.pallas_call(
        body, grid_spec=pltpu.PrefetchScalarGridSpec(
            num_scalar_prefetch=1, grid=(4,), in_specs=[pl.BlockSpec((None, r2, n), lambda k, c_ref: (k, c_ref[0], 0)), spec], out_specs=spec),
        out_shape=_sds(recv.shape, BF16), name=name, compiler_params=_PAR)(c.reshape(1).astype(jnp.int32), g, recv)


def _sum_lead(t, name, tm=256):
    P, R, n = t.shape
    tm = _row_tile(R, tm)

    def body(t_ref, o_ref):
        acc = t_ref[0].astype(F32)
        for j in range(1, P):
            acc = acc + t_ref[j].astype(F32)
        o_ref[...] = acc

    return pl.pallas_call(body, grid=(R // tm,), in_specs=[pl.BlockSpec((P, tm, n), lambda i: (0, i, 0))],
                          out_specs=pl.BlockSpec((tm, n), lambda i: (i, 0)), out_shape=_sds((R, n)), name=name, compiler_params=_PAR)(t)


def _ada_fwd(c16, ada_w, ada_b_cols, tn=512):
    L, _, n = ada_w.shape

    def body(c_ref, w_ref, b_ref, o_ref):
        o_ref[0] = _dot(_silu(c_ref[...]).astype(BF16), w_ref[0].astype(BF16), _NN) + b_ref[0]

    return pl.pallas_call(
        body, grid=(L, n // tn),
        in_specs=[pl.BlockSpec((16, D), lambda l, j: (0, 0)), pl.BlockSpec((1, D, tn), lambda l, j: (l, 0, j)), pl.BlockSpec((1, 1, tn), lambda l, j: (l, 0, j))],
        out_specs=pl.BlockSpec((1, 16, tn), lambda l, j: (l, 0, j)), out_shape=_sds((L, 16, n)), name="ada_fwd",
        compiler_params=pltpu.CompilerParams(dimension_semantics=("parallel", "parallel")))(c16, ada_w, ada_b_cols)


def _ada_bwd(c16, dmod, tn=512):
    L, _, n = dmod.shape

    def body(c_ref, d_ref, o_ref):
        o_ref[0] = _dot(_silu(c_ref[...]).astype(BF16), d_ref[0].astype(BF16), _TN)

    return pl.pallas_call(
        body, grid=(L, n // tn), in_specs=[pl.BlockSpec((16, D), lambda l, j: (0, 0)), pl.BlockSpec((1, 16, tn), lambda l, j: (l, 0, j))],
        out_specs=pl.BlockSpec((1, D, tn), lambda l, j: (l, 0, j)), out_shape=_sds((L, D, n)), name="ada_bwd",
        compiler_params=pltpu.CompilerParams(dimension_semantics=("parallel", "parallel")))(c16, dmod)


def _adam_math(w, g, m, v):
    mn = ADAM_B1 * m + (1.0 - ADAM_B1) * g
    vn = ADAM_B2 * v + (1.0 - ADAM_B2) * (g * g)
    m_hat = mn / (1.0 - ADAM_B1 ** ADAM_STEP)
    v_hat = vn / (1.0 - ADAM_B2 ** ADAM_STEP)
    return -ADAM_LR * (m_hat / (jnp.sqrt(v_hat) + ADAM_EPS) + ADAM_WD * w), mn, vn


def _adamw(w, g, m, v, name):
    R, n = w.shape
    tm = _row_tile(R)

    def body(w_ref, g_ref, m_ref, v_ref, d_ref, nm_ref, nv_ref):
        d_ref[...], nm_ref[...], nv_ref[...] = _adam_math(w_ref[...], g_ref[...], m_ref[...], v_ref[...])

    spec = pl.BlockSpec((tm, n), lambda i: (i, 0))
    return pl.pallas_call(body, grid=(R // tm,), in_specs=[spec] * 4, out_specs=(spec,) * 3, out_shape=(_sds((R, n)),) * 3,
                          name=name, compiler_params=_PAR)(w, g, m, v)


def _adamw_layers(w, g0, g1, m, v, name):
    _, r, n = w.shape
    tm = _row_tile(r)
    nb = r // tm

    def body(w_ref, g0_ref, g1_ref, m_ref, v_ref, g_ref, d_ref, nm_ref, nv_ref):
        gv = jnp.where(pl.program_id(0) == 0, g0_ref[...], g1_ref[...])
        g_ref[...] = gv
        d_ref[...], nm_ref[...], nv_ref[...] = _adam_math(w_ref[...], gv, m_ref[...], v_ref[...])

    spec = pl.BlockSpec((None, tm, n), lambda l, i: (l, i, 0))
    g0_spec = pl.BlockSpec((tm, n), lambda l, i: (i * (1 - l) + (nb - 1) * l, 0))
    g1_spec = pl.BlockSpec((tm, n), lambda l, i: (i * l, 0))
    return pl.pallas_call(body, grid=(2, nb), in_specs=[spec, g0_spec, g1_spec, spec, spec], out_specs=(spec,) * 4,
                          out_shape=(_sds(w.shape),) * 4, name=name,
                          compiler_params=pltpu.CompilerParams(dimension_semantics=("arbitrary", "arbitrary")))(w, g0, g1, m, v)


_W_NAMES = ["ada_w", "ada_b", "norm1_w", "w_in", "q_a_norm", "w_q_b", "kv_a_norm", "w_kv_b", "q_norm", "k_norm", "pool_w",
            "pool_scale", "ssd_conv_w", "ssd_conv_b", "ssd_dt_bias", "ssd_a_log", "ssd_d", "ssd_norm_w", "w_branch", "w_out",
            "norm2_w", "ffn_up", "ffn_conv_w", "ffn_conv_b", "ffn_down"]
_BIG = [("w_in", (D, IN_DIM // 4), 1), ("w_q_b", (Q_RANK, 192), 1), ("w_kv_b", (KV_RANK, 256), 1), ("w_branch", (512, D), 0),
        ("w_out", (256, D), 0), ("ffn_up", (D, 2 * FFN // 4), 1), ("ffn_down", (FFN // 4, D), 0)]

_SMALL = [("norm1_w", (D,)), ("q_a_norm", (Q_RANK,)), ("kv_a_norm", (KV_RANK,)), ("q_norm", (96,)), ("k_norm", (96,)),
          ("pool_w", (4, 128, 128)), ("pool_scale", (512,)), ("ssd_conv_w", (4, SSD_CONV_DIM)), ("ssd_conv_b", (SSD_CONV_DIM,)),
          ("ssd_dt_bias", (SSD_HEADS,)), ("ssd_a_log", (SSD_HEADS,)), ("ssd_d", (SSD_HEADS,)), ("ssd_norm_w", (D,)), ("norm2_w", (D,)),
          ("ffn_conv_w", (3, 2 * FFN)), ("ffn_conv_b", (2 * FFN,))]
_CONV_SHARDED = {"ssd_conv_w": SSD_CONV_DIM // 4, "ffn_conv_w": 2 * FFN // 4}


def _pack_flat(arrs, mult):
    flat = jnp.concatenate([a.astype(F32).reshape(-1) for a in arrs])
    rows = -(-flat.shape[0] // (128 * mult)) * mult
    return jnp.pad(flat, (0, rows * 128 - flat.shape[0])).reshape(rows, 128), [a.shape for a in arrs]


def _unpack_flat(packed, shapes):
    flat, out, off = packed.reshape(-1), [], 0
    for s in shapes:
        n = 1
        for d in s:
            n *= d
        out.append(flat[off:off + n].reshape(s))
        off += n
    return out


def _layer_weights(a, l, stacks, conv_full):
    p = {n: a[n][l] for n in _W_NAMES if n not in ("ada_w", "ada_b")}
    p.update({n: conv_full[n][l] for n in conv_full})
    p.update(w_in4=stacks["w_in"], w_q_b4=stacks["w_q_b"], w_kv_b4=stacks["w_kv_b"], ffn_up4=stacks["ffn_up"],
             w_branch=stacks["w_branch"].reshape(2048, D), w_out=stacks["w_out"].reshape(D, D), ffn_down=stacks["ffn_down"].reshape(FFN, D))
    return p


def _reduce_chips(gs, ci, tag):
    recv = _send_halves(gs, f"rs_halves_{tag}")
    chip_sum = [_add_cast(g, r, ci, f"rs_add_{tag}") for g, r in zip(gs, recv)]
    got = _scatter_chips(chip_sum, f"rs_scatter_{tag}")
    return _join_halves([_sum_lead(t, f"rs_sum_{tag}") for t in got], f"rs_join_{tag}")


def kernel(x, c, positions, ada_w, ada_b, norm1_w, w_in, q_a_norm, w_q_b, kv_a_norm, w_kv_b, q_norm, k_norm, pool_w, pool_scale, ssd_conv_w, ssd_conv_b, ssd_dt_bias, ssd_a_log, ssd_d, ssd_norm_w, w_branch, w_out, norm2_w, ffn_up, ffn_conv_w, ffn_conv_b, ffn_down, loss_target, m_ada_w, m_ada_b, m_norm1_w, m_w_in, m_q_a_norm, m_w_q_b, m_kv_a_norm, m_w_kv_b, m_q_norm, m_k_norm, m_pool_w, m_pool_scale, m_ssd_conv_w, m_ssd_conv_b, m_ssd_dt_bias, m_ssd_a_log, m_ssd_d, m_ssd_norm_w, m_w_branch, m_w_out, m_norm2_w, m_ffn_up, m_ffn_conv_w, m_ffn_conv_b, m_ffn_down, v_ada_w, v_ada_b, v_norm1_w, v_w_in, v_q_a_norm, v_w_q_b, v_kv_a_norm, v_w_kv_b, v_q_norm, v_k_norm, v_pool_w, v_pool_scale, v_ssd_conv_w, v_ssd_conv_b, v_ssd_dt_bias, v_ssd_a_log, v_ssd_d, v_ssd_norm_w, v_w_branch, v_w_out, v_norm2_w, v_ffn_up, v_ffn_conv_w, v_ffn_conv_b, v_ffn_down):
    a = dict(locals())
    xi, yi, ci = lax.axis_index("x"), lax.axis_index("y"), lax.axis_index("c")
    chip = 2 * xi + yi
    dev = 2 * chip + ci
    ncol = 6 * D // 4

    c_all = _allgather8(c.reshape(8, 128), "gather_c").reshape(8, D)
    c16 = jnp.pad(c_all, ((0, 8), (0, 0)))
    ada_b_cols = lax.dynamic_slice_in_dim(ada_b, chip * ncol, ncol, axis=1).reshape(2, 1, ncol)
    mod_part = _ada_fwd(c16, ada_w, ada_b_cols)[:, :8]
    small1, shapes1 = _pack_flat([mod_part, ssd_conv_w, ffn_conv_w], 8)
    got1 = _allgather8(small1, "gather_mod").reshape(8, -1, 128)
    per_chip = [_unpack_flat(got1[2 * k], shapes1) for k in range(4)]
    mod_all = jnp.concatenate([per_chip[k][0] for k in range(4)], -1)
    conv_full = {"ssd_conv_w": jnp.concatenate([per_chip[k][1] for k in range(4)], -1),
                 "ffn_conv_w": jnp.concatenate([per_chip[k][2] for k in range(4)], -1)}
    mod_mine = lax.dynamic_index_in_dim(mod_all, dev, axis=1, keepdims=False).reshape(2, 6, D)
    mods = [jnp.pad(mod_mine[l], ((0, 2), (0, 0))) for l in range(2)]

    big = [n for n, _, _ in _BIG]
    layers = []
    for l in range(2):
        stacks = _gather_chips([a[n][l].astype(BF16) for n in big], f"gather_w{l}")
        layers.append(_layer_weights(a, l, dict(zip(big, stacks)), conv_full))

    lpart, grad_x, grads, dmods = _local_step(x[0], loss_target[0], positions[0], mods, layers)
    loss = lax.psum(lpart, ("x", "y", "c"))

    small2, shapes2 = _pack_flat([jnp.stack(dmods)] + [grads[l][n] for l in range(2) for n, _ in _SMALL], 16)
    got2 = _allgather8(small2, "gather_small").reshape(8, -1, 128)
    tot = _unpack_flat(_sum_lead(got2, "sum_small"), shapes2)
    g = {"ada_b": tot[0].reshape(2, 6 * D)}
    for i, (n, _) in enumerate(_SMALL):
        g[n] = jnp.stack([tot[1 + i], tot[1 + len(_SMALL) + i]])
    for n, w in _CONV_SHARDED.items():
        g[n] = lax.dynamic_slice_in_dim(g[n], chip * w, w, axis=2)
    nd = 2 * 6 * D // 128
    dmod_all = jnp.transpose(got2[:, :nd].reshape(8, 2, 6 * D), (1, 0, 2))
    dmod_cols = lax.dynamic_slice_in_dim(jnp.pad(dmod_all, ((0, 0), (0, 8), (0, 0))), chip * ncol, ncol, axis=2)
    g["ada_w"] = _ada_bwd(c16, dmod_cols)

    delta, new_m, new_v = {}, {}, {}
    reduced = [_reduce_chips([grads[l][n] for n in big], ci, l) for l in (1, 0)][::-1]
    for i, n in enumerate(big):
        g[n], delta[n], new_m[n], new_v[n] = _adamw_layers(a[n], reduced[0][i], reduced[1][i], a["m_" + n], a["v_" + n], f"adamw_{n}")
    shp = ada_w.shape
    r2 = lambda t: t.reshape(-1, shp[-1])
    delta["ada_w"], new_m["ada_w"], new_v["ada_w"] = (
        o.reshape(shp) for o in _adamw(r2(ada_w), r2(g["ada_w"]), r2(m_ada_w), r2(v_ada_w), "adamw_ada_w"))
    rest = [n for n in _W_NAMES if n not in big and n != "ada_w"]
    packs = [_pack_flat([t[n] if pre is None else t[pre + n] for n in rest], 128)[0]
             for t, pre in ((a, None), (g, None), (a, "m_"), (a, "v_"))]
    rest_shapes = [a[n].shape for n in rest]
    outs = [_unpack_flat(o, rest_shapes) for o in _adamw(*packs, "adamw_rest")]
    for i, n in enumerate(rest):
        delta[n], new_m[n], new_v[n] = outs[0][i], outs[1][i], outs[2][i]

    return (loss, grad_x[None], *[g[n] for n in _W_NAMES], *[delta[n] for n in _W_NAMES],
            *[new_m[n] for n in _W_NAMES], *[new_v[n] for n in _W_NAMES])
```

```python
import functools

import jax
import jax.numpy as jnp
from jax import lax
from jax.experimental import pallas as pl
from jax.experimental.pallas import tpu as pltpu

F32 = jnp.float32
BF16 = jnp.bfloat16
MESH = pl.DeviceIdType.MESH
HI = lax.Precision.HIGHEST

D = 1024
N_HEADS = 8
NOPE, ROPE_DIM = 64, 32
Q_RANK, KV_RANK = 384, 256
POOL_WINDOWS = (2, 4, 8, 16)
SSD_HEADS, SSD_P, SSD_N, SSD_L = 16, 64, 128, 128
SSD_INNER = 1024
SSD_CONV_DIM = 1536
FFN = 2816
EPS = 1e-6
ROPE_THETA = 10000.0
OG, OZ, OX, OP, OQ, OKV, ODT, IN_PAD = 0, 3072, 4096, 5632, 6144, 6528, 6912, 7040
IN_DIM = 6832
ADAM_LR, ADAM_B1, ADAM_B2, ADAM_EPS, ADAM_WD, ADAM_STEP = 0.001, 0.9, 0.999, 1e-08, 0.01, 10

_ARB = pltpu.CompilerParams(dimension_semantics=("arbitrary",))
_PAR = pltpu.CompilerParams(dimension_semantics=("parallel",))


def _pick(n, pref):
    if n <= pref:
        return n
    best = None
    for t in range(128, pref + 1, 128):
        if n % t == 0:
            best = t
    assert best is not None, (n, pref)
    return best


def _row_tile(r, cap=256):
    best = None
    for t in range(16, min(r, cap) + 1, 16):
        if r % t == 0:
            best = t
    assert best is not None, r
    return best


def _sds(shape, dtype=F32):
    return jax.ShapeDtypeStruct(tuple(shape), dtype)


def _iota(shape, dim):
    return lax.broadcasted_iota(jnp.int32, shape, dim)


def _sigmoid(x):
    return 0.5 * jnp.tanh(0.5 * x) + 0.5


def _silu(x):
    return x * _sigmoid(x)


def _dsilu(x):
    s = _sigmoid(x)
    return s * (1.0 + x * (1.0 - s))


def _dot(a, b, dims):
    return lax.dot_general(a, b, (dims, ((), ())), preferred_element_type=F32)


_NN, _NT, _TN = ((1,), (0,)), ((1,), (1,)), ((0,), (0,))


def _dot_hi(a, b, dims=_NN):
    return lax.dot_general(a, b, (dims, ((), ())), preferred_element_type=F32, precision=HI)


def _shift_down(x, j):
    n = x.shape[0]
    return jnp.where(_iota(x.shape, 0) >= j, pltpu.roll(x, j, 0), 0.0)


def _shift_up(x, j):
    n = x.shape[0]
    return jnp.where(_iota(x.shape, 0) < n - j, pltpu.roll(x, n - j, 0), 0.0)


def _mm(a, b, mode="nn", out_dtype=F32, tm=512, tn=512, tk=4096, res=None, gate=None, name="mm"):
    if mode == "nn":
        (M, K), (K2, N) = a.shape, b.shape
    elif mode == "nt":
        (M, K), (N, K2) = a.shape, b.shape
    else:
        (K, M), (K2, N) = a.shape, b.shape
    assert K == K2, (a.shape, b.shape, mode)
    tm, tn, tk = _pick(M, tm), _pick(N, tn), _pick(K, tk)
    nk = K // tk
    dims = {"nn": _NN, "nt": _NT, "tn": _TN}[mode]
    fused = res is not None

    def body(*refs):
        a_ref, b_ref = refs[:2]

        def finish(acc):
            if fused:
                r_ref, g_ref, o_ref, raw_ref = refs[2:6]
                raw_ref[...] = acc
                o_ref[...] = r_ref[...] + g_ref[...] * acc
            else:
                refs[2][...] = acc.astype(out_dtype)

        _mm_steps(a_ref, b_ref, dims, nk, refs[-1] if nk > 1 else None, finish)

    if mode == "nn":
        a_spec = pl.BlockSpec((tm, tk), lambda i, j, k: (i, k))
        b_spec = pl.BlockSpec((tk, tn), lambda i, j, k: (k, j))
    elif mode == "nt":
        a_spec = pl.BlockSpec((tm, tk), lambda i, j, k: (i, k))
        b_spec = pl.BlockSpec((tn, tk), lambda i, j, k: (j, k))
    else:
        a_spec = pl.BlockSpec((tk, tm), lambda i, j, k: (k, i))
        b_spec = pl.BlockSpec((tk, tn), lambda i, j, k: (k, j))
    o_spec = pl.BlockSpec((tm, tn), lambda i, j, k: (i, j))
    in_specs, args = [a_spec, b_spec], [a, b]
    out_shape, out_specs = _sds((M, N), out_dtype), o_spec
    if fused:
        in_specs += [o_spec, pl.BlockSpec((1, tn), lambda i, j, k: (0, j))]
        args += [res, gate]
        out_shape, out_specs = (_sds((M, N)), _sds((M, N))), (o_spec, o_spec)
    return pl.pallas_call(
        body, grid=(M // tm, N // tn, nk), in_specs=in_specs, out_specs=out_specs, out_shape=out_shape,
        scratch_shapes=[pltpu.VMEM((tm, tn), F32)] if nk > 1 else [], name=name,
        compiler_params=pltpu.CompilerParams(dimension_semantics=("parallel", "parallel", "arbitrary")),
    )(*args)


def _mm_steps(a_ref, b_ref, dims, nk, acc_ref, finish):
    part = _dot(a_ref[...].astype(BF16), b_ref[...].astype(BF16), dims)
    if nk == 1:
        finish(part)
        return
    k = pl.program_id(2)

    @pl.when(k == 0)
    def _():
        acc_ref[...] = part

    @pl.when(k > 0)
    def _():
        acc_ref[...] += part

    @pl.when(k == nk - 1)
    def _():
        finish(acc_ref[...])


def _mm_blocks(a, b, dims, grid, a_spec, b_spec, o_spec, out_shape, acc_shape, name):
    nk = grid[2]

    def body(a_ref, b_ref, o_ref, *scratch):
        def finish(acc):
            o_ref[...] = acc.astype(o_ref.dtype)

        _mm_steps(a_ref, b_ref, dims, nk, scratch[0] if nk > 1 else None, finish)

    return pl.pallas_call(
        body, grid=grid, in_specs=[a_spec, b_spec], out_specs=o_spec, out_shape=out_shape,
        scratch_shapes=[pltpu.VMEM(acc_shape, F32)] if nk > 1 else [], name=name,
        compiler_params=pltpu.CompilerParams(dimension_semantics=("parallel", "parallel", "arbitrary")),
    )(a, b)


_UP_SHARD = 2 * FFN // 4


def _up_fwd(h2, wup4, name, tm=512):
    S = h2.shape[0]
    tm = min(tm, S)
    return _mm_blocks(h2, wup4, _NN, (S // tm, 4, 1), pl.BlockSpec((tm, D), lambda i, j, k: (i, 0)),
                      pl.BlockSpec((None, D, _UP_SHARD), lambda i, j, k: (j, 0, 0)), pl.BlockSpec((tm, _UP_SHARD), lambda i, j, k: (i, j)),
                      _sds((S, 2 * FFN)), (tm, _UP_SHARD), name)


def _up_dx(dup2, wup4, name, tm=512, tn=512):
    S = dup2.shape[1]
    tm = min(tm, S)
    return _mm_blocks(dup2, wup4, _NT, (S // tm, D // tn, 4), pl.BlockSpec((None, tm, _UP_SHARD), lambda i, j, k: (lax.div(k, 2), i, lax.rem(k, 2))),
                      pl.BlockSpec((None, tn, _UP_SHARD), lambda i, j, k: (k, j, 0)), pl.BlockSpec((tm, tn), lambda i, j, k: (i, j)),
                      _sds((S, D)), (tm, tn), name)


def _up_dw(h2, dup2, name, tm=512, tk=4096):
    S = h2.shape[0]
    tk = min(tk, S)
    return _mm_blocks(h2, dup2, _TN, (D // tm, 4, S // tk), pl.BlockSpec((tk, tm), lambda i, j, k: (k, i)),
                      pl.BlockSpec((None, tk, _UP_SHARD), lambda i, j, k: (lax.div(j, 2), k, lax.rem(j, 2))),
                      pl.BlockSpec((None, tm, _UP_SHARD), lambda i, j, k: (j, i, 0)), _sds((4, D, _UP_SHARD)), (tm, _UP_SHARD), name)


def _row_spec(tm, n):
    return pl.BlockSpec((tm, n), lambda i: (i, 0))


def _vec_spec(n, rows=1):
    return pl.BlockSpec((rows, n), lambda i: (0, 0))


def _ln_mod(x, nw, sc, sh, name, tm=256):
    S = x.shape[0]

    def body(x_ref, nw_ref, sc_ref, sh_ref, o_ref):
        xv = x_ref[...]
        r = lax.rsqrt(jnp.mean(xv * xv, -1, keepdims=True) + EPS)
        o_ref[...] = ((xv * r * nw_ref[...]) * (1.0 + sc_ref[...]) + sh_ref[...]).astype(BF16)

    return pl.pallas_call(
        body, grid=(S // tm,), in_specs=[_row_spec(tm, D)] + [_vec_spec(D)] * 3, out_specs=_row_spec(tm, D),
        out_shape=_sds((S, D), BF16), name=name, compiler_params=_PAR)(x, nw, sc, sh)


def _ln_mod_bwd(x, dh, dres, nw, sc, name, tm=256):
    S = x.shape[0]

    def body(x_ref, dh_ref, dres_ref, nw_ref, sc_ref, dx_ref, st_ref):
        @pl.when(pl.program_id(0) == 0)
        def _():
            st_ref[...] = jnp.zeros_like(st_ref)

        xv, dhv, nwv = x_ref[...], dh_ref[...], nw_ref[...]
        r = lax.rsqrt(jnp.mean(xv * xv, -1, keepdims=True) + EPS)
        xhat = xv * r
        dn = dhv * (1.0 + sc_ref[...])
        g = dn * nwv
        dx_ref[...] = dres_ref[...] + r * (g - xhat * jnp.mean(g * xhat, -1, keepdims=True))
        st_ref[0:1, :] += jnp.sum(dhv, 0, keepdims=True)
        st_ref[1:2, :] += jnp.sum(dhv * (xhat * nwv), 0, keepdims=True)
        st_ref[2:3, :] += jnp.sum(dn * xhat, 0, keepdims=True)

    return pl.pallas_call(
        body, grid=(S // tm,), in_specs=[_row_spec(tm, D)] * 3 + [_vec_spec(D)] * 2,
        out_specs=(_row_spec(tm, D), _vec_spec(D, 8)), out_shape=(_sds((S, D)), _sds((8, D))),
        name=name, compiler_params=_ARB)(x, dh, dres, nw, sc)


def _gate_bwd(dx, out, g, name, tm=256):
    S = dx.shape[0]

    def body(dx_ref, o_ref, g_ref, dz_ref, dg_ref):
        @pl.when(pl.program_id(0) == 0)
        def _():
            dg_ref[...] = jnp.zeros_like(dg_ref)

        dxv = dx_ref[...]
        dz_ref[...] = (dxv * g_ref[...]).astype(BF16)
        dg_ref[0:1, :] += jnp.sum(dxv * o_ref[...], 0, keepdims=True)

    return pl.pallas_call(
        body, grid=(S // tm,), in_specs=[_row_spec(tm, D)] * 2 + [_vec_spec(D)],
        out_specs=(_row_spec(tm, D), _vec_spec(D, 8)), out_shape=(_sds((S, D), BF16), _sds((8, D))),
        name=name, compiler_params=_ARB)(dx, out, g)


def _loss_grad(y, t, tm=256):
    S = y.shape[0]

    def body(y_ref, t_ref, dy_ref, l_ref):
        @pl.when(pl.program_id(0) == 0)
        def _():
            l_ref[...] = jnp.zeros_like(l_ref)

        e = y_ref[...] - t_ref[...]
        dy_ref[...] = e * (1.0 / D)
        l_ref[...] += 0.5 * jnp.sum(jnp.mean(e * e, -1, keepdims=True), 0, keepdims=True)

    return pl.pallas_call(
        body, grid=(S // tm,), in_specs=[_row_spec(tm, D)] * 2,
        out_specs=(_row_spec(tm, D), pl.BlockSpec((8, 128), lambda i: (0, 0))),
        out_shape=(_sds((S, D)), _sds((8, 128))), name="loss_grad", compiler_params=_ARB)(y, t)


def _conv(x, w, b):
    K = w.shape[0]
    acc = x * w[K - 1:K, :] + b
    for j in range(1, K):
        acc = acc + _shift_down(x, j) * w[K - 1 - j:K - j, :]
    return acc


def _conv_bwd(x, w, dc):
    K = w.shape[0]
    dx = dc * w[K - 1:K, :]
    dws = [jnp.sum(dc * x, 0, keepdims=True)]
    for j in range(1, K):
        dx = dx + _shift_up(dc, j) * w[K - 1 - j:K - j, :]
        dws.append(jnp.sum(dc * _shift_down(x, j), 0, keepdims=True))
    return dx, dws[::-1], jnp.sum(dc, 0, keepdims=True)


def _col_spec(S, tc, off=0):
    return pl.BlockSpec((S, tc), lambda j: (0, j + off))


def _ssd_pre(proj, cw, cb, tc=256):
    S, n = proj.shape[0], SSD_CONV_DIM

    def body(x_ref, w_ref, b_ref, o_ref):
        o_ref[...] = _silu(_conv(x_ref[...], w_ref[...], b_ref[...]))

    return pl.pallas_call(
        body, grid=(n // tc,),
        in_specs=[_col_spec(S, tc, OX // tc), pl.BlockSpec((4, tc), lambda j: (0, j)), pl.BlockSpec((1, tc), lambda j: (0, j))],
        out_specs=_col_spec(S, tc), out_shape=_sds((S, n)), name="ssd_pre", compiler_params=_PAR)(proj, cw, cb)


def _ssd_pre_bwd(proj, dxc, cw, cb, tc=256):
    S, n = proj.shape[0], SSD_CONV_DIM

    def body(x_ref, d_ref, w_ref, b_ref, dx_ref, dw_ref, db_ref):
        xv, wv = x_ref[...], w_ref[...]
        dc = d_ref[...] * _dsilu(_conv(xv, wv, b_ref[...]))
        dx, dw, db = _conv_bwd(xv, wv, dc)
        dx_ref[...] = dx.astype(BF16)
        for k, row in enumerate(dw):
            dw_ref[k:k + 1, :] = row
        db_ref[...] = db

    wspec, bspec = pl.BlockSpec((4, tc), lambda j: (0, j)), pl.BlockSpec((1, tc), lambda j: (0, j))
    return pl.pallas_call(
        body, grid=(n // tc,), in_specs=[_col_spec(S, tc, OX // tc), _col_spec(S, tc), wspec, bspec],
        out_specs=(_col_spec(S, tc), wspec, bspec), out_shape=(_sds((S, n), BF16), _sds((4, n)), _sds((1, n))),
        name="ssd_pre_bwd", compiler_params=_PAR)(proj, dxc, cw, cb)


def _ffn_act(up, cw, cb, tc=256):
    S, nb = up.shape[0], FFN // tc

    def body(g_ref, v_ref, wg_ref, wv_ref, bg_ref, bv_ref, o_ref):
        o_ref[...] = (_silu(_conv(g_ref[...], wg_ref[...], bg_ref[...])) * _conv(v_ref[...], wv_ref[...], bv_ref[...])).astype(BF16)

    def wspec(off):
        return pl.BlockSpec((3, tc), lambda j: (0, j + off))

    def bspec(off):
        return pl.BlockSpec((1, tc), lambda j: (0, j + off))

    return pl.pallas_call(
        body, grid=(nb,), in_specs=[_col_spec(S, tc), _col_spec(S, tc, nb), wspec(0), wspec(nb), bspec(0), bspec(nb)],
        out_specs=_col_spec(S, tc), out_shape=_sds((S, FFN), BF16), name="ffn_act", compiler_params=_PAR)(up, up, cw, cw, cb, cb)


def _ffn_act_bwd(up, dact, cw, cb, tc=256):
    S, nb = up.shape[0], FFN // tc

    def body(g_ref, v_ref, d_ref, wg_ref, wv_ref, bg_ref, bv_ref, dx_ref, dw_ref, db_ref):
        gv, vv, wg, wv, da = g_ref[...], v_ref[...], wg_ref[...], wv_ref[...], d_ref[...]
        cg, cv = _conv(gv, wg, bg_ref[...]), _conv(vv, wv, bv_ref[...])
        for half, (dx, dw, db) in enumerate((_conv_bwd(gv, wg, da * cv * _dsilu(cg)), _conv_bwd(vv, wv, da * _silu(cg)))):
            dx_ref[half] = dx.astype(BF16)
            for k in range(3):
                dw_ref[half, k:k + 1, :] = dw[k]
            db_ref[half] = db

    def wspec(off):
        return pl.BlockSpec((3, tc), lambda j: (0, j + off))

    def bspec(off):
        return pl.BlockSpec((1, tc), lambda j: (0, j + off))

    cs = _col_spec(S, tc)
    both = lambda r: pl.BlockSpec((2, r, tc), lambda j: (0, 0, j))
    return pl.pallas_call(
        body, grid=(nb,), in_specs=[cs, _col_spec(S, tc, nb), cs, wspec(0), wspec(nb), bspec(0), bspec(nb)],
        out_specs=(both(S), both(3), both(1)), out_shape=(_sds((2, S, FFN), BF16), _sds((2, 3, FFN)), _sds((2, 1, FFN))),
        name="ffn_act_bwd", compiler_params=_PAR)(up, up, dact, cw, cw, cb, cb)


def _window_sum(x, w, up=False):
    shift = _shift_up if up else _shift_down
    j = 1
    while j < w:
        x = x + shift(x, j)
        j *= 2
    return x


def _pool_fwd(proj, pool_w, pool_scale):
    S = proj.shape[0]

    def body(u_ref, w_ref, s_ref, o_ref):
        cnt_row = (_iota((S, 128), 0) + 1).astype(F32)
        for g, w in enumerate(POOL_WINDOWS):
            sl = slice(g * 128, (g + 1) * 128)
            u = u_ref[:, sl]
            pooled = _window_sum(u, w) / jnp.minimum(cnt_row, float(w)) - u
            mixed = _dot(pooled.astype(BF16), w_ref[g].astype(BF16), _NN)
            o_ref[:, sl] = (mixed * s_ref[:, sl]).astype(BF16)

    return pl.pallas_call(
        body, grid=(1,),
        in_specs=[pl.BlockSpec((S, 512), lambda i: (0, OP // 512)), pl.BlockSpec((4, 128, 128), lambda i: (0, 0, 0)), _vec_spec(512)],
        out_specs=pl.BlockSpec((S, 512), lambda i: (0, 0)), out_shape=_sds((S, 512), BF16), name="pool_fwd",
        compiler_params=_ARB)(proj, pool_w, pool_scale)


def _pool_bwd(proj, dob, pool_w, pool_scale):
    S = proj.shape[0]

    def body(u_ref, d_ref, w_ref, s_ref, du_ref, dw_ref, ds_ref):
        cnt_row = (_iota((S, 128), 0) + 1).astype(F32)
        for g, w in enumerate(POOL_WINDOWS):
            sl = slice(g * 128, (g + 1) * 128)
            u, dv, wv = u_ref[:, sl], d_ref[:, sl], w_ref[g].astype(BF16)
            cnt = jnp.minimum(cnt_row, float(w))
            pooled = (_window_sum(u, w) / cnt - u).astype(BF16)
            ds_ref[:, sl] = jnp.sum(dv * _dot(pooled, wv, _NN), 0, keepdims=True)
            dmix = (dv * s_ref[:, sl]).astype(BF16)
            dw_ref[g] = _dot(pooled, dmix, _TN)
            dp = _dot(dmix, wv, _NT)
            du_ref[:, sl] = (_window_sum(dp / cnt, w, up=True) - dp).astype(BF16)

    blk = pl.BlockSpec((S, 512), lambda i: (0, 0))
    wspec = pl.BlockSpec((4, 128, 128), lambda i: (0, 0, 0))
    return pl.pallas_call(
        body, grid=(1,), in_specs=[pl.BlockSpec((S, 512), lambda i: (0, OP // 512)), blk, wspec, _vec_spec(512)],
        out_specs=(blk, wspec, _vec_spec(512)), out_shape=(_sds((S, 512), BF16), _sds((4, 128, 128)), _sds((1, 512))),
        name="pool_bwd", compiler_params=_ARB)(proj, dob, pool_w, pool_scale)


def _branch_specs():
    return [pl.BlockSpec((512, D), lambda i: (0, 0)), pl.BlockSpec((512, D), lambda i: (1, 0)), pl.BlockSpec((1024, D), lambda i: (1, 0))]


def _merge_fwd(oa, ob, oc, proj, wbr, tm=256):
    S = oa.shape[0]

    def body(oa_ref, ob_ref, oc_ref, gl_ref, wa_ref, wb_ref, wc_ref, o_ref):
        acc = _sigmoid(gl_ref[:, 0:D]) * _dot(oa_ref[...], wa_ref[...], _NN)
        acc += _sigmoid(gl_ref[:, D:2 * D]) * _dot(ob_ref[...], wb_ref[...], _NN)
        acc += _sigmoid(gl_ref[:, 2 * D:3 * D]) * _dot(oc_ref[...], wc_ref[...], _NN)
        o_ref[...] = acc.astype(BF16)

    return pl.pallas_call(
        body, grid=(S // tm,),
        in_specs=[_row_spec(tm, 512), _row_spec(tm, 512), _row_spec(tm, D), _row_spec(tm, 3 * D)] + _branch_specs(),
        out_specs=_row_spec(tm, D), out_shape=_sds((S, D), BF16), name="merge_fwd", compiler_params=_PAR)(oa, ob, oc, proj, wbr, wbr, wbr)


def _merge_bwd(dm, oa, ob, oc, proj, wbr, tm=256):
    S = oa.shape[0]

    def body(dm_ref, oa_ref, ob_ref, oc_ref, gl_ref, wa_ref, wb_ref, wc_ref, dya_ref, dyb_ref, dyc_ref, dgl_ref, doa_ref, dob_ref, doc_ref):
        dmv = dm_ref[...]
        for i, (o_ref, w_ref, dy_ref, do_ref) in enumerate(
                ((oa_ref, wa_ref, dya_ref, doa_ref), (ob_ref, wb_ref, dyb_ref, dob_ref), (oc_ref, wc_ref, dyc_ref, doc_ref))):
            gt = _sigmoid(gl_ref[:, i * D:(i + 1) * D])
            wv = w_ref[...]
            yv = _dot(o_ref[...], wv, _NN)
            dy = (dmv * gt).astype(BF16)
            dy_ref[...] = dy
            dgl_ref[:, i * D:(i + 1) * D] = (dmv * yv * gt * (1.0 - gt)).astype(BF16)
            do_ref[...] = _dot(dy, wv, _NT)

    rs = _row_spec
    return pl.pallas_call(
        body, grid=(S // tm,),
        in_specs=[rs(tm, D), rs(tm, 512), rs(tm, 512), rs(tm, D), rs(tm, 3 * D)] + _branch_specs(),
        out_specs=(rs(tm, D), rs(tm, D), rs(tm, D), rs(tm, 3 * D), rs(tm, 512), rs(tm, 512), rs(tm, D)),
        out_shape=(_sds((S, D), BF16), _sds((S, D), BF16), _sds((S, D), BF16), _sds((S, 3 * D), BF16), _sds((S, 512)), _sds((S, 512)), _sds((S, D))),
        name="merge_bwd", compiler_params=_PAR)(dm, oa, ob, oc, proj, wbr, wbr, wbr)


def _rope_tab(posb, invf, tm=256):
    S = posb.shape[0]

    def body(p_ref, f_ref, c_ref, s1_ref, s2_ref):
        ang = p_ref[...] * f_ref[...]
        lane = _iota(ang.shape, 1)
        cs, sn = jnp.cos(ang), jnp.sin(ang)
        c_ref[...] = jnp.where(lane < NOPE, 1.0, cs)
        s1_ref[...] = jnp.where((lane >= 64) & (lane < 80), -sn, 0.0)
        s2_ref[...] = jnp.where((lane >= 80) & (lane < 96), sn, 0.0)

    rs = _row_spec(tm, 128)
    return pl.pallas_call(body, grid=(S // tm,), in_specs=[rs, _vec_spec(128)], out_specs=(rs, rs, rs),
                          out_shape=(_sds((S, 128)),) * 3, name="rope_tab", compiler_params=_PAR)(posb, invf)


def _rope(u, C, S1, S2):
    return u * C + pltpu.roll(u, 112, 1) * S1 + pltpu.roll(u, 16, 1) * S2


def _rope_t(dy, C, S1, S2):
    return dy * C + pltpu.roll(dy * S1, 16, 1) + pltpu.roll(dy * S2, 112, 1)


def _seg_sum(v, mask):
    return jnp.sum(jnp.where(mask, v, 0.0), -1, keepdims=True)


def _mla_latents(pq_ref, pkv_ref, wqb_ref, wkvb_ref, qan_ref, kvan_ref):
    ql, kvl = pq_ref[...], pkv_ref[...]
    ckv, kr = kvl[:, 0:KV_RANK], kvl[:, KV_RANK:KV_RANK + 128]
    rq = lax.rsqrt(jnp.mean(ql * ql, -1, keepdims=True) + EPS)
    rkv = lax.rsqrt(jnp.mean(ckv * ckv, -1, keepdims=True) + EPS)
    nq = (ql * rq * qan_ref[...]).astype(BF16)
    nkv = (ckv * rkv * kvan_ref[...]).astype(BF16)
    kv = jnp.concatenate([_dot(nkv, wkvb_ref[k], _NN) for k in range(4)], 1)
    return ql, ckv, kr, rq, rkv, nq, nkv, _dot(nq, wqb_ref[...], _NN), kv


def _mla_specs(tm):
    full = lambda r, n: pl.BlockSpec((r, n), lambda i: (0, 0))
    return ([pl.BlockSpec((tm, 384), lambda i: (i, OQ // 384)), pl.BlockSpec((tm, 384), lambda i: (i, OKV // 384))],
            [full(Q_RANK, D), pl.BlockSpec((4, KV_RANK, 256), lambda i: (0, 0, 0)), _vec_spec(Q_RANK), _vec_spec(KV_RANK), _vec_spec(128), _vec_spec(128)]
            + [_row_spec(tm, 128)] * 3)


def _mla_prep(proj, wqb, wkvb, qan, kvan, wq, wk, rope, tm=256):
    S = proj.shape[0]

    def body(pq_ref, pkv_ref, wqb_ref, wkvb_ref, qan_ref, kvan_ref, wq_ref, wk_ref, c_ref, s1_ref, s2_ref, qf_ref, kf_ref, ve_ref):
        _, _, kr, _, _, _, _, q, kv = _mla_latents(pq_ref, pkv_ref, wqb_ref, wkvb_ref, qan_ref, kvan_ref)
        C, S1, S2, wqv, wkv = c_ref[...], s1_ref[...], s2_ref[...], wq_ref[...], wk_ref[...]
        lane = _iota((tm, 128), 1)
        mn, mr = lane < 64, (lane >= 64) & (lane < 96)
        rrk = lax.rsqrt(_seg_sum(kr * kr, mr) / ROPE_DIM + EPS)
        ykr = _rope(jnp.where(mr, kr * rrk * wkv, 0.0), C, S1, S2)
        for h in range(N_HEADS):
            sl = slice(h * 128, (h + 1) * 128)
            t = q[:, sl]
            rn = lax.rsqrt(_seg_sum(t * t, mn) / NOPE + EPS)
            rr = lax.rsqrt(_seg_sum(t * t, mr) / ROPE_DIM + EPS)
            qf_ref[:, sl] = _rope(t * jnp.where(mn, rn, jnp.where(mr, rr, 0.0)) * wqv, C, S1, S2).astype(BF16)
            t = kv[:, sl]
            rn = lax.rsqrt(_seg_sum(t * t, mn) / NOPE + EPS)
            kf_ref[:, sl] = (jnp.where(mn, t * rn * wkv, 0.0) + ykr).astype(BF16)
            ve_ref[:, sl] = (jnp.where(mn, pltpu.roll(t, 64, 1), 0.0) if h % 2 == 0 else jnp.where(mn, 0.0, t)).astype(BF16)

    pspecs, wspecs = _mla_specs(tm)
    rs = _row_spec(tm, D)
    return pl.pallas_call(body, grid=(S // tm,), in_specs=pspecs + wspecs, out_specs=(rs, rs, rs),
                          out_shape=(_sds((S, D), BF16),) * 3, name="mla_prep", compiler_params=_PAR)(
        proj, proj, wqb, wkvb, qan, kvan, wq, wk, *rope)


def _mla_prep_bwd(proj, dqf, dkf, dve, wqb, wkvb, qan, kvan, wq, wk, rope, tm=256):
    S = proj.shape[0]

    def body(pq_ref, pkv_ref, wqb_ref, wkvb_ref, qan_ref, kvan_ref, wq_ref, wk_ref, c_ref, s1_ref, s2_ref,
             dqf_ref, dkf_ref, dve_ref, dqs_ref, dkvs_ref, dwqb_ref, dwkvb_ref, st_ref, dq_scr, dkv_scr):
        @pl.when(pl.program_id(0) == 0)
        def _():
            dwqb_ref[...] = jnp.zeros_like(dwqb_ref)
            dwkvb_ref[...] = jnp.zeros_like(dwkvb_ref)
            st_ref[...] = jnp.zeros_like(st_ref)

        ql, ckv, kr, rq, rkv, nq, nkv, q, kv = _mla_latents(pq_ref, pkv_ref, wqb_ref, wkvb_ref, qan_ref, kvan_ref)
        C, S1, S2, wqv, wkv = c_ref[...], s1_ref[...], s2_ref[...], wq_ref[...], wk_ref[...]
        lane = _iota((tm, 128), 1)
        mn, mr = lane < 64, (lane >= 64) & (lane < 96)
        dwq = jnp.zeros((1, 128), F32)
        dwk = jnp.zeros((1, 128), F32)
        dykr = jnp.zeros((tm, 128), F32)
        for h in range(N_HEADS):
            sl = slice(h * 128, (h + 1) * 128)
            t = q[:, sl]
            rn = lax.rsqrt(_seg_sum(t * t, mn) / NOPE + EPS)
            rr = lax.rsqrt(_seg_sum(t * t, mr) / ROPE_DIM + EPS)
            scale = jnp.where(mn, rn, jnp.where(mr, rr, 0.0))
            that = t * scale
            du = _rope_t(dqf_ref[:, sl], C, S1, S2)
            dwq += jnp.sum(du * that, 0, keepdims=True)
            g = du * wqv
            gt = g * that
            dq_scr[:, sl] = scale * (g - that * jnp.where(mn, _seg_sum(gt, mn) / NOPE, _seg_sum(gt, mr) / ROPE_DIM))
            t = kv[:, sl]
            rn = lax.rsqrt(_seg_sum(t * t, mn) / NOPE + EPS)
            that = jnp.where(mn, t * rn, 0.0)
            dkf = dkf_ref[:, sl]
            dykr += jnp.where(mr, dkf, 0.0)
            dkn = jnp.where(mn, dkf, 0.0)
            dwk += jnp.sum(dkn * that, 0, keepdims=True)
            g = dkn * wkv
            dve = dve_ref[:, sl]
            dkv_scr[:, sl] = jnp.where(mn, rn * (g - that * (jnp.sum(g * that, -1, keepdims=True) / NOPE)),
                                       pltpu.roll(dve, 64, 1) if h % 2 == 0 else dve)
        rrk = lax.rsqrt(_seg_sum(kr * kr, mr) / ROPE_DIM + EPS)
        that = jnp.where(mr, kr * rrk, 0.0)
        dukr = jnp.where(mr, _rope_t(dykr, C, S1, S2), 0.0)
        dwk += jnp.sum(dukr * that, 0, keepdims=True)
        g = dukr * wkv
        dkr = rrk * (g - that * (jnp.sum(g * that, -1, keepdims=True) / ROPE_DIM))
        dqv, dkvv = dq_scr[...].astype(BF16), dkv_scr[...].astype(BF16)
        dnq = _dot(dqv, wqb_ref[...], _NT)
        dwqb_ref[...] += _dot(nq, dqv, _TN)
        dnkv = jnp.zeros((tm, KV_RANK), F32)
        for k in range(4):
            dnkv += _dot(dkvv[:, k * 256:(k + 1) * 256], wkvb_ref[k], _NT)
            dwkvb_ref[k] += _dot(nkv, dkvv[:, k * 256:(k + 1) * 256], _TN)
        xhat = ql * rq
        st_ref[0:1, 0:Q_RANK] += jnp.sum(dnq * xhat, 0, keepdims=True)
        g = dnq * qan_ref[...]
        dqs_ref[...] = (rq * (g - xhat * jnp.mean(g * xhat, -1, keepdims=True))).astype(BF16)
        xhat = ckv * rkv
        st_ref[1:2, 0:KV_RANK] += jnp.sum(dnkv * xhat, 0, keepdims=True)
        g = dnkv * kvan_ref[...]
        dkvs_ref[:, 0:KV_RANK] = (rkv * (g - xhat * jnp.mean(g * xhat, -1, keepdims=True))).astype(BF16)
        dkvs_ref[:, KV_RANK:KV_RANK + 128] = dkr.astype(BF16)
        st_ref[2:3, 0:128] += dwq
        st_ref[3:4, 0:128] += dwk

    pspecs, wspecs = _mla_specs(tm)
    rs = _row_spec(tm, D)
    full = lambda r, n: pl.BlockSpec((r, n), lambda i: (0, 0))
    return pl.pallas_call(
        body, grid=(S // tm,), in_specs=pspecs + wspecs + [rs, rs, rs],
        out_specs=(_row_spec(tm, 384), _row_spec(tm, 384), full(Q_RANK, D), pl.BlockSpec((4, KV_RANK, 256), lambda i: (0, 0, 0)), full(8, D)),
        out_shape=(_sds((S, 384), BF16), _sds((S, 384), BF16), _sds((Q_RANK, D)), _sds((4, KV_RANK, 256)), _sds((8, D))),
        scratch_shapes=[pltpu.VMEM((tm, D), F32), pltpu.VMEM((tm, D), F32)], name="mla_prep_bwd", compiler_params=_ARB)(
        proj, proj, wqb, wkvb, qan, kvan, wq, wk, *rope, dqf, dkf, dve)


_ATT_SCALE = (NOPE + ROPE_DIM) ** -0.5


def _att_probs(q, k, i, tq):
    n = k.shape[0]
    s = _dot(q, k, _NT) * _ATT_SCALE
    s = jnp.where(_iota((tq, n), 1) <= i * tq + _iota((tq, n), 0), s, -1e30)
    p = jnp.exp(s - jnp.max(s, -1, keepdims=True))
    return p / jnp.sum(p, -1, keepdims=True)


def _attn_fwd(qf, kf, ve, tq=256):
    S = qf.shape[0]

    def body(q_ref, k_ref, v_ref, o_ref):
        for i in range(S // tq):
            n, rows = (i + 1) * tq, slice(i * tq, (i + 1) * tq)
            acc = jnp.zeros((tq, 128), F32)
            for hh in range(2):
                sl = slice(hh * 128, (hh + 1) * 128)
                p = _att_probs(q_ref[rows, sl], k_ref[0:n, sl], i, tq)
                acc += _dot(p.astype(BF16), v_ref[0:n, sl], _NN)
            o_ref[rows, :] = acc.astype(BF16)

    ps = pl.BlockSpec((S, 256), lambda h: (0, h))
    return pl.pallas_call(body, grid=(N_HEADS // 2,), in_specs=[ps, ps, ps], out_specs=pl.BlockSpec((S, 128), lambda h: (0, h)),
                          out_shape=_sds((S, 512), BF16), name="attn_fwd", compiler_params=_PAR)(qf, kf, ve)


def _attn_bwd(qf, kf, ve, do, tq=256):
    S = qf.shape[0]

    def body(q_ref, k_ref, v_ref, do_ref, dq_ref, dk_ref, dv_ref):
        dk_ref[...] = jnp.zeros_like(dk_ref)
        dv_ref[...] = jnp.zeros_like(dv_ref)
        for i in range(S // tq):
            n, rows = (i + 1) * tq, slice(i * tq, (i + 1) * tq)
            dob = do_ref[rows, :].astype(BF16)
            for hh in range(2):
                sl = slice(hh * 128, (hh + 1) * 128)
                q, k = q_ref[rows, sl], k_ref[0:n, sl]
                p = _att_probs(q, k, i, tq)
                dv_ref[0:n, sl] += _dot(p.astype(BF16), dob, _TN)
                dp = _dot(dob, v_ref[0:n, sl], _NT)
                ds = (p * (dp - jnp.sum(dp * p, -1, keepdims=True)) * _ATT_SCALE).astype(BF16)
                dq_ref[rows, sl] = _dot(ds, k, _NN)
                dk_ref[0:n, sl] += _dot(ds, q, _TN)

    ps = pl.BlockSpec((S, 256), lambda h: (0, h))
    return pl.pallas_call(body, grid=(N_HEADS // 2,), in_specs=[ps, ps, ps, pl.BlockSpec((S, 128), lambda h: (0, h))], out_specs=(ps, ps, ps),
                          out_shape=(_sds((S, D)),) * 3, name="attn_bwd", compiler_params=_PAR)(qf, kf, ve, do)


def _softplus(x):
    return jnp.maximum(x, 0.0) + jnp.log1p(jnp.exp(-jnp.abs(x)))


def _ssd_chunk(xc_ref, dtr_ref, dtb_ref, al_ref, e_ref):
    L = SSD_L
    a = -jnp.exp(al_ref[...])
    dtp = _softplus(dtr_ref[...] + dtb_ref[...])
    causal = _iota((L, L), 1) <= _iota((L, L), 0)
    cs = _dot_hi(causal.astype(F32), dtp * a)
    E = e_ref[...]
    dtx, csx = _dot_hi(dtp, E), _dot_hi(cs, E)
    X = xc_ref[:, 0:SSD_INNER]
    Xd = X * dtx
    dec_out = jnp.exp(csx)
    dec_st = jnp.exp(csx[L - 1:L, :] - csx)
    return a, dtp, causal, cs, cs.T, dtx, X, Xd, dec_out, dec_st


def _ssd_decay(causal, cs, cs_row, h):
    diff = cs[:, h:h + 1] - cs_row[h:h + 1, :]
    return jnp.where(causal, jnp.exp(jnp.where(causal, diff, 0.0)), 0.0)


def _ssd_groups(xc_ref, g):
    b0, c0 = SSD_INNER + g * SSD_N, SSD_INNER + 2 * SSD_N + g * SSD_N
    return xc_ref[:, b0:b0 + SSD_N].astype(BF16), xc_ref[:, c0:c0 + SSD_N].astype(BF16)


def _pair_decay(cs, pair):
    L = SSD_L
    return jnp.where(_iota((128, 128), 0) < 64, jnp.exp(cs[L - 1:L, 2 * pair:2 * pair + 1]), jnp.exp(cs[L - 1:L, 2 * pair + 1:2 * pair + 2]))


def _ssd_in_specs(nc, rev):
    idx = (lambda c: nc - 1 - c) if rev else (lambda c: c)
    return [pl.BlockSpec((SSD_L, SSD_CONV_DIM), lambda c: (idx(c), 0)), pl.BlockSpec((SSD_L, 128), lambda c: (idx(c), ODT // 128)),
            _vec_spec(128), _vec_spec(128), _vec_spec(SSD_INNER), pl.BlockSpec((128, SSD_INNER), lambda c: (0, 0))]


def _ssd_core(xc, proj, dtb, alog, dskip, E):
    S = xc.shape[0]
    nc = S // SSD_L

    def body(xc_ref, dtr_ref, dtb_ref, al_ref, dx_ref, e_ref, y_ref, hp_ref, h_scr):
        @pl.when(pl.program_id(0) == 0)
        def _():
            h_scr[...] = jnp.zeros_like(h_scr)

        hp_ref[0] = h_scr[...]
        _, _, causal, cs, cs_row, _, X, Xd, dec_out, dec_st = _ssd_chunk(xc_ref, dtr_ref, dtb_ref, al_ref, e_ref)
        Xs = Xd * dec_st
        lane = _iota((SSD_L, 128), 1)
        for g in range(2):
            Bg, Cg = _ssd_groups(xc_ref, g)
            CB = _dot(Cg, Bg, _NT)
            for pr in range(4):
                pair = g * 4 + pr
                psl = slice(pair * 128, (pair + 1) * 128)
                Xdp = Xd[:, psl].astype(BF16)
                r0 = _dot((CB * _ssd_decay(causal, cs, cs_row, 2 * pair)).astype(BF16), Xdp, _NN)
                r1 = _dot((CB * _ssd_decay(causal, cs, cs_row, 2 * pair + 1)).astype(BF16), Xdp, _NN)
                Hp = h_scr[psl, :]
                W = _dot(Cg, Hp.astype(BF16), _NT)
                y_ref[:, psl] = jnp.where(lane < 64, r0, r1) + W * dec_out[:, psl] + X[:, psl] * dx_ref[:, psl]
                h_scr[psl, :] = Hp * _pair_decay(cs, pair) + _dot(Xs[:, psl].astype(BF16), Bg, _TN)

    return pl.pallas_call(
        body, grid=(nc,), in_specs=_ssd_in_specs(nc, False),
        out_specs=(pl.BlockSpec((SSD_L, SSD_INNER), lambda c: (c, 0)), pl.BlockSpec((1, SSD_INNER, SSD_N), lambda c: (c, 0, 0))),
        out_shape=(_sds((S, SSD_INNER)), _sds((nc, SSD_INNER, SSD_N))), scratch_shapes=[pltpu.VMEM((SSD_INNER, SSD_N), F32)],
        name="ssd_core", compiler_params=_ARB)(xc, proj, dtb, alog, dskip, E)


def _ssd_core_bwd(xc, proj, hprev, dy, dtb, alog, dskip, E, ET):
    S = xc.shape[0]
    nc = S // SSD_L
    L = SSD_L

    def body(xc_ref, dtr_ref, dtb_ref, al_ref, dx_ref, e_ref, et_ref, hp_ref, dy_ref, dxc_ref, ddt_ref, st_ref, dh_scr, acc_scr):
        step = pl.program_id(0)

        @pl.when(step == 0)
        def _():
            dh_scr[...] = jnp.zeros_like(dh_scr)
            acc_scr[...] = jnp.zeros_like(acc_scr)
            st_ref[...] = jnp.zeros_like(st_ref)

        a, dtp, causal, cs, cs_row, dtx, X, Xd, dec_out, dec_st = _ssd_chunk(xc_ref, dtr_ref, dtb_ref, al_ref, e_ref)
        lane = _iota((L, 128), 1)
        sub = _iota((128, L), 0)
        dcs_col = jnp.zeros((L, 128), F32)
        dcs_row = jnp.zeros((128, L), F32)
        dcs_last = jnp.zeros((1, 128), F32)
        dcsx, ddtx, dlastx = [], [], []
        for g in range(2):
            Bg, Cg = _ssd_groups(xc_ref, g)
            CB = _dot(Cg, Bg, _NT)
            dCB = jnp.zeros((L, L), F32)
            dB = jnp.zeros((L, SSD_N), F32)
            dC = jnp.zeros((L, SSD_N), F32)
            for pr in range(4):
                pair = g * 4 + pr
                psl = slice(pair * 128, (pair + 1) * 128)
                dY, Xp, Xdp, dop, dsp = dy_ref[:, psl], X[:, psl], Xd[:, psl], dec_out[:, psl], dec_st[:, psl]
                Xdb = Xdp.astype(BF16)
                acc_scr[0:1, psl] += jnp.sum(dY * Xp, 0, keepdims=True)
                Hp = hp_ref[0, psl, :]
                Hb = Hp.astype(BF16)
                dW = (dY * dop).astype(BF16)
                dcx = dY * _dot(Cg, Hb, _NT) * dop
                dC += _dot(dW, Hb, _NN)
                dHp = _dot(dW, Cg, _TN)
                dHn = dh_scr[psl, :]
                cd = _pair_decay(cs, pair)
                dh_scr[psl, :] = dHp + dHn * cd
                rsum = jnp.sum(dHn * Hp * cd, -1, keepdims=True)
                half = _iota((128, 1), 0) < 64
                s0 = jnp.sum(jnp.where(half, rsum, 0.0), 0, keepdims=True)
                s1 = jnp.sum(jnp.where(half, 0.0, rsum), 0, keepdims=True)
                lane1 = _iota((1, 128), 1)
                dcs_last += jnp.where(lane1 == 2 * pair, s0, 0.0) + jnp.where(lane1 == 2 * pair + 1, s1, 0.0)
                dHb = dHn.astype(BF16)
                dXs = _dot(Bg, dHb, _NT)
                dB += _dot((Xdp * dsp).astype(BF16), dHb, _NN)
                dXd = dXs * dsp
                e_st = dXs * Xdp * dsp
                dcx -= e_st
                dlastx.append(jnp.sum(e_st, 0, keepdims=True))
                for i in range(2):
                    h = 2 * pair + i
                    Dm = _ssd_decay(causal, cs, cs_row, h)
                    M = CB * Dm
                    dYm = jnp.where((lane < 64) if i == 0 else (lane >= 64), dY, 0.0).astype(BF16)
                    dM = _dot(dYm, Xdb, _NT)
                    dXd += _dot(M.astype(BF16), dYm, _TN)
                    dCB += dM * Dm
                    Em = dM * M
                    dcs_col += jnp.where(lane == h, jnp.sum(Em, -1, keepdims=True), 0.0)
                    dcs_row += jnp.where(sub == h, jnp.sum(Em, 0, keepdims=True), 0.0)
                dxc_ref[:, psl] = dY * dx_ref[:, psl] + dXd * dtx[:, psl]
                ddtx.append(dXd * Xp)
                dcsx.append(dcx)
            dCBb = dCB.astype(BF16)
            b0, c0 = SSD_INNER + g * SSD_N, SSD_INNER + 2 * SSD_N + g * SSD_N
            dxc_ref[:, b0:b0 + SSD_N] = dB + _dot(dCBb, Cg, _TN)
            dxc_ref[:, c0:c0 + SSD_N] = dC + _dot(dCBb, Bg, _NN)
        ET = et_ref[...]
        dcs = dcs_col - dcs_row.T + _dot_hi(jnp.concatenate(dcsx, 1), ET)
        dlast = dcs_last + _dot_hi(jnp.broadcast_to(jnp.concatenate(dlastx, 1), (8, SSD_INNER)), ET)[0:1, :]
        dcs += jnp.where(_iota((L, 128), 0) == L - 1, dlast, 0.0)
        dda = _dot_hi((_iota((L, L), 1) >= _iota((L, L), 0)).astype(F32), dcs)
        ddtp = dda * a + _dot_hi(jnp.concatenate(ddtx, 1), ET)
        draw = ddtp * _sigmoid(dtr_ref[...] + dtb_ref[...])
        ddt_ref[...] = draw.astype(BF16)
        st_ref[0:1, :] += jnp.sum(draw, 0, keepdims=True)
        st_ref[1:2, :] += jnp.sum(dda * dtp, 0, keepdims=True) * a

        @pl.when(step == nc - 1)
        def _():
            st_ref[2:3, :] = _dot_hi(acc_scr[...], ET)[0:1, :]

    rev = lambda c: (nc - 1 - c, 0)
    return pl.pallas_call(
        body, grid=(nc,),
        in_specs=_ssd_in_specs(nc, True) + [pl.BlockSpec((SSD_INNER, 128), lambda c: (0, 0)),
                                            pl.BlockSpec((1, SSD_INNER, SSD_N), lambda c: (nc - 1 - c, 0, 0)),
                                            pl.BlockSpec((L, SSD_INNER), rev)],
        out_specs=(pl.BlockSpec((L, SSD_CONV_DIM), rev), pl.BlockSpec((L, 128), rev), pl.BlockSpec((8, 128), lambda c: (0, 0))),
        out_shape=(_sds((S, SSD_CONV_DIM)), _sds((S, 128), BF16), _sds((8, 128))),
        scratch_shapes=[pltpu.VMEM((SSD_INNER, SSD_N), F32), pltpu.VMEM((8, SSD_INNER), F32)],
        name="ssd_core_bwd", compiler_params=_ARB)(xc, proj, dtb, alog, dskip, E, ET, hprev, dy)


def _ssd_post(y, proj, nw, tm=256):
    S = y.shape[0]

    def body(y_ref, z_ref, nw_ref, o_ref):
        for g in range(2):
            sl = slice(g * 512, (g + 1) * 512)
            gated = y_ref[:, sl] * _silu(z_ref[:, sl])
            r = lax.rsqrt(jnp.mean(gated * gated, -1, keepdims=True) + EPS)
            o_ref[:, sl] = (gated * r * nw_ref[:, sl]).astype(BF16)

    return pl.pallas_call(
        body, grid=(S // tm,), in_specs=[_row_spec(tm, D), pl.BlockSpec((tm, D), lambda i: (i, OZ // D)), _vec_spec(D)],
        out_specs=_row_spec(tm, D), out_shape=_sds((S, D), BF16), name="ssd_post", compiler_params=_PAR)(y, proj, nw)


def _ssd_post_bwd(doc, y, proj, nw, tm=256):
    S = y.shape[0]

    def body(d_ref, y_ref, z_ref, nw_ref, dy_ref, dz_ref, st_ref):
        @pl.when(pl.program_id(0) == 0)
        def _():
            st_ref[...] = jnp.zeros_like(st_ref)

        for g in range(2):
            sl = slice(g * 512, (g + 1) * 512)
            yv, zv, dv = y_ref[:, sl], z_ref[:, sl], d_ref[:, sl]
            sz = _silu(zv)
            gated = yv * sz
            r = lax.rsqrt(jnp.mean(gated * gated, -1, keepdims=True) + EPS)
            ghat = gated * r
            st_ref[0:1, sl] += jnp.sum(dv * ghat, 0, keepdims=True)
            gg = dv * nw_ref[:, sl]
            dg = r * (gg - ghat * jnp.mean(gg * ghat, -1, keepdims=True))
            dy_ref[:, sl] = dg * sz
            dz_ref[:, sl] = (dg * yv * _dsilu(zv)).astype(BF16)

    zs = pl.BlockSpec((tm, D), lambda i: (i, OZ // D))
    return pl.pallas_call(
        body, grid=(S // tm,), in_specs=[_row_spec(tm, D), _row_spec(tm, D), zs, _vec_spec(D)],
        out_specs=(_row_spec(tm, D), _row_spec(tm, D), _vec_spec(D, 8)), out_shape=(_sds((S, D)), _sds((S, D), BF16), _sds((8, D))),
        name="ssd_post_bwd", compiler_params=_ARB)(doc, y, proj, nw)


def _row(v, n=None):
    v = v.astype(F32).reshape(1, -1)
    return v if n is None else jnp.pad(v, ((0, 0), (0, n - v.shape[1])))


def _prep_layer(p):
    w_in = jnp.transpose(p["w_in4"].astype(BF16), (1, 0, 2)).reshape(D, IN_DIM)
    z = lambda n: jnp.zeros((D, n), BF16)
    win = jnp.concatenate([w_in[:, 3760:6832], w_in[:, 1184:2208], w_in[:, 2208:3744], w_in[:, 672:1184], w_in[:, 0:384],
                           w_in[:, 384:640], z(64), w_in[:, 640:672], z(32), w_in[:, 3744:3760], z(112)], 1)
    wqb = jnp.transpose(p["w_q_b4"].astype(BF16).reshape(4, Q_RANK, 2, 96), (1, 0, 2, 3))
    return dict(
        win=win, wqb=jnp.pad(wqb, ((0, 0), (0, 0), (0, 0), (0, 32))).reshape(Q_RANK, D), wkvb4=p["w_kv_b4"].astype(BF16),
        wbr=p["w_branch"].astype(BF16), wo=p["w_out"].astype(BF16), wup4=p["ffn_up4"].astype(BF16), wdn=p["ffn_down"].astype(BF16),
        nw1=_row(p["norm1_w"]), nw2=_row(p["norm2_w"]), qan=_row(p["q_a_norm"]), kvan=_row(p["kv_a_norm"]),
        wq=_row(p["q_norm"], 128), wk=_row(p["k_norm"], 128), pool_w=p["pool_w"].astype(F32), pool_scale=_row(p["pool_scale"]),
        cw=p["ssd_conv_w"].astype(F32), cb=_row(p["ssd_conv_b"]), dtb=_row(p["ssd_dt_bias"], 128), alog=_row(p["ssd_a_log"], 128),
        dskip=_row(jnp.repeat(p["ssd_d"].astype(F32), SSD_P)), snw=_row(p["ssd_norm_w"]),
        fcw=p["ffn_conv_w"].astype(F32), fcb=_row(p["ffn_conv_b"]))


def _layer_fwd(x, mod8, W, rope, E, tag):
    sh1, sc1, g1, sh2, sc2, g2 = (mod8[i:i + 1] for i in range(6))
    h1 = _ln_mod(x, W["nw1"], sc1, sh1, name=f"ln1_{tag}")
    proj = _mm(h1, W["win"], tn=640, tk=1024, name=f"proj_{tag}")
    qf, kf, ve = _mla_prep(proj, W["wqb"], W["wkvb4"], W["qan"], W["kvan"], W["wq"], W["wk"], rope)
    oa = _attn_fwd(qf, kf, ve)
    ob = _pool_fwd(proj, W["pool_w"], W["pool_scale"])
    xc = _ssd_pre(proj, W["cw"], W["cb"])
    y, hprev = _ssd_core(xc, proj, W["dtb"], W["alog"], W["dskip"], E)
    oc = _ssd_post(y, proj, W["snw"])
    merged = _merge_fwd(oa, ob, oc, proj, W["wbr"])
    x1, out1 = _mm(merged, W["wo"], tk=1024, res=x, gate=g1, name=f"wout_{tag}")
    h2 = _ln_mod(x1, W["nw2"], sc2, sh2, name=f"ln2_{tag}")
    up = _up_fwd(h2, W["wup4"], name=f"up_{tag}")
    act = _ffn_act(up, W["fcw"], W["fcb"])
    x2, out2 = _mm(act, W["wdn"], res=x1, gate=g2, name=f"down_{tag}")
    saved = dict(x=x, h1=h1, proj=proj, qf=qf, kf=kf, ve=ve, oa=oa, ob=ob, oc=oc, xc=xc, hprev=hprev, y=y, merged=merged,
                 out1=out1, x1=x1, h2=h2, up=up, act=act, out2=out2)
    return x2, saved


def _layer_bwd(dx2, sv, mod8, W, rope, E, ET, tag):
    sc1, g1, sc2, g2 = mod8[1:2], mod8[2:3], mod8[4:5], mod8[5:6]
    proj = sv["proj"]
    dz2, dg2 = _gate_bwd(dx2, sv["out2"], g2, name=f"gate2_bwd_{tag}")
    dact = _mm(dz2, W["wdn"], "nt", tn=1408, tk=1024, name=f"down_dx_{tag}")
    dwdn = _mm(sv["act"], dz2, "tn", tm=1408, name=f"down_dw_{tag}")
    dup2, dfcw, dfcb = _ffn_act_bwd(sv["up"], dact, W["fcw"], W["fcb"])
    dh2 = _up_dx(dup2, W["wup4"], name=f"up_dx_{tag}")
    dwup4 = _up_dw(sv["h2"], dup2, name=f"up_dw_{tag}")
    dx1, st2 = _ln_mod_bwd(sv["x1"], dh2, dx2, W["nw2"], sc2, name=f"ln2_bwd_{tag}")
    dz1, dg1 = _gate_bwd(dx1, sv["out1"], g1, name=f"gate1_bwd_{tag}")
    dmerged = _mm(dz1, W["wo"], "nt", tk=1024, name=f"wout_dx_{tag}")
    dwo = _mm(sv["merged"], dz1, "tn", name=f"wout_dw_{tag}")
    dya, dyb, dyc, dgl, doa, dob, doc = _merge_bwd(dmerged, sv["oa"], sv["ob"], sv["oc"], proj, W["wbr"])
    dwba = _mm(sv["oa"], dya, "tn", name=f"wba_dw_{tag}")
    dwbb = _mm(sv["ob"], dyb, "tn", name=f"wbb_dw_{tag}")
    dwbc = _mm(sv["oc"], dyc, "tn", name=f"wbc_dw_{tag}")
    dy, dzs, st_post = _ssd_post_bwd(doc, sv["y"], proj, W["snw"])
    dxc, ddt, st_ssd = _ssd_core_bwd(sv["xc"], proj, sv["hprev"], dy, W["dtb"], W["alog"], W["dskip"], E, ET)
    dxbc, dcw, dcb = _ssd_pre_bwd(proj, dxc, W["cw"], W["cb"])
    dpool, dpw, dps = _pool_bwd(proj, dob, W["pool_w"], W["pool_scale"])
    dqf, dkf, dve = _attn_bwd(sv["qf"], sv["kf"], sv["ve"], doa)
    dqs, dkvs, dwqb, dwkvb4, st_mla = _mla_prep_bwd(proj, dqf, dkf, dve, W["wqb"], W["wkvb4"], W["qan"], W["kvan"], W["wq"], W["wk"], rope)
    dproj = jnp.concatenate([dgl, dzs, dxbc, dpool, dqs, dkvs, ddt], 1)
    dh1 = _mm(dproj, W["win"], "nt", tk=IN_PAD, name=f"proj_dx_{tag}")
    dwin = _mm(sv["h1"], dproj, "tn", tn=640, name=f"proj_dw_{tag}")
    dx, st1 = _ln_mod_bwd(sv["x"], dh1, dx1, W["nw1"], sc1, name=f"ln1_bwd_{tag}")
    dw_in = jnp.concatenate([dwin[:, OQ:OQ + 384], dwin[:, OKV:OKV + 256], dwin[:, OKV + 320:OKV + 352], dwin[:, OP:OP + 512],
                             dwin[:, OZ:OZ + 1024], dwin[:, OX:OX + 1536], dwin[:, ODT:ODT + 16], dwin[:, OG:OG + 3072]], 1)
    grads = dict(
        norm1_w=st1[2], norm2_w=st2[2],
        w_in=jnp.transpose(dw_in.reshape(D, 4, IN_DIM // 4), (1, 0, 2)),
        q_a_norm=st_mla[0, :Q_RANK], kv_a_norm=st_mla[1, :KV_RANK], q_norm=st_mla[2, :96], k_norm=st_mla[3, :96],
        w_q_b=jnp.transpose(dwqb.reshape(Q_RANK, 4, 2, 128)[:, :, :, :96], (1, 0, 2, 3)).reshape(4, Q_RANK, 192), w_kv_b=dwkvb4,
        pool_w=dpw, pool_scale=dps[0], ssd_conv_w=dcw, ssd_conv_b=dcb[0],
        ssd_dt_bias=st_ssd[0, :SSD_HEADS], ssd_a_log=st_ssd[1, :SSD_HEADS], ssd_d=st_ssd[2, :SSD_HEADS], ssd_norm_w=st_post[0],
        w_branch=jnp.concatenate([dwba, dwbb, dwbc], 0).reshape(4, 512, D), w_out=dwo.reshape(4, 256, D), ffn_up=dwup4,
        ffn_conv_w=jnp.transpose(dfcw, (1, 0, 2)).reshape(3, 2 * FFN), ffn_conv_b=dfcb.reshape(2 * FFN), ffn_down=dwdn.reshape(4, FFN // 4, D))
    dmod = jnp.concatenate([st1[0:2], dg1[0:1], st2[0:2], dg2[0:1]], 0)
    return dx, grads, dmod


def _ssd_expand():
    E = (jnp.arange(SSD_INNER)[None, :] // SSD_P == jnp.arange(128)[:, None]).astype(F32)
    return E, E.T


def _rope_tables(positions):
    inv_freq = ROPE_THETA ** (-jnp.arange(0, ROPE_DIM, 2, dtype=F32) / ROPE_DIM)
    invf = jnp.concatenate([jnp.zeros((NOPE,), F32), inv_freq, inv_freq, jnp.zeros((32,), F32)]).reshape(1, 128)
    posb = jnp.broadcast_to(positions.astype(F32)[:, None], (positions.shape[0], 128))
    return _rope_tab(posb, invf)


def _local_step(x, target, positions, mods, layers):
    rope = _rope_tables(positions)
    E, ET = _ssd_expand()
    Ws = [_prep_layer(p) for p in layers]
    saved, h = [], x
    for l, W in enumerate(Ws):
        h, sv = _layer_fwd(h, mods[l], W, rope, E, l)
        saved.append(sv)
    dy, lpart = _loss_grad(h, target)
    grads, dmods = [None] * len(Ws), [None] * len(Ws)
    for l in reversed(range(len(Ws))):
        dy, grads[l], dmods[l] = _layer_bwd(dy, saved[l], mods[l], Ws[l], rope, E, ET, l)
    return lpart[0, 0], dy, grads, dmods


_ANY = pl.BlockSpec(memory_space=pl.ANY)
_VMEM = pl.BlockSpec(memory_space=pltpu.VMEM)


def _place():
    x, y, c = lax.axis_index("x"), lax.axis_index("y"), lax.axis_index("c")
    return x, y, c, [(1 - x, y), (x, 1 - y), (1 - x, 1 - y)]


def _allgather8(v, name):
    m_per, n = v.shape

    def body(x_ref, out_ref, send_sems, recv_sems, local_sem):
        x, y, c, chips = _place()
        me, sibling = (x, y, c), (x, y, 1 - c)

        def rows(px, py, pc):
            return out_ref.at[pl.ds((4 * px + 2 * py + pc) * m_per, m_per), :]

        def copy(k, block, to, src=None):
            return pltpu.make_async_remote_copy(src_ref=rows(*block) if src is None else src, dst_ref=rows(*block),
                                                send_sem=send_sems.at[k], recv_sem=recv_sems.at[k], device_id=to, device_id_type=MESH)

        mine = pltpu.make_async_copy(x_ref, rows(*me), local_sem)
        mine.start()
        first = [copy(0, me, sibling, src=x_ref)] + [copy(1 + j, me, (*chip, c), src=x_ref) for j, chip in enumerate(chips)]
        for cp in first:
            cp.start()
        passed = [copy(4 + j, (*chip, c), sibling) for j, chip in enumerate(chips)]
        for j, chip in enumerate(chips):
            copy(1 + j, (*chip, c), me).wait_recv()
            passed[j].start()
        copy(0, sibling, me).wait_recv()
        for j, chip in enumerate(chips):
            copy(4 + j, (*chip, 1 - c), me).wait_recv()
        for cp in first + passed:
            cp.wait_send()
        mine.wait()

    return pl.pallas_call(
        body, out_shape=_sds((8 * m_per, n), v.dtype), in_specs=[_VMEM], out_specs=_VMEM,
        scratch_shapes=[pltpu.SemaphoreType.DMA((7,)), pltpu.SemaphoreType.DMA((7,)), pltpu.SemaphoreType.DMA], name=name)(v)


def _sems(n):
    return [pltpu.SemaphoreType.DMA((n,)), pltpu.SemaphoreType.DMA((n,))]


def _gather_chips(arrs, name):
    na = len(arrs)

    def body(*refs):
        p_refs, o_refs, (send_sems, recv_sems) = refs[:na], refs[na:2 * na], refs[2 * na:]
        x, y, c, chips = _place()
        k, sibling = 2 * x + y, (x, y, 1 - c)

        def copy(a, s, kk, cc, to, from_shard=False):
            r2 = p_refs[a].shape[0] // 2
            dst = o_refs[a].at[kk, pl.ds(cc * r2, r2), :]
            return pltpu.make_async_remote_copy(src_ref=p_refs[a].at[pl.ds(cc * r2, r2), :] if from_shard else dst, dst_ref=dst,
                                                send_sem=send_sems.at[6 * a + s], recv_sem=recv_sems.at[6 * a + s],
                                                device_id=to, device_id_type=MESH)

        first = [copy(a, j, k, c, (*chip, c), from_shard=True) for a in range(na) for j, chip in enumerate(chips)]
        for cp in first:
            cp.start()
        passed = []
        for a in range(na):
            for j, chip in enumerate(chips):
                copy(a, j, 2 * chip[0] + chip[1], c, (x, y, c)).wait_recv()
                passed.append(copy(a, 3 + j, 2 * chip[0] + chip[1], c, sibling))
                passed[-1].start()
        for a in range(na):
            for j, chip in enumerate(chips):
                copy(a, 3 + j, 2 * chip[0] + chip[1], 1 - c, (x, y, c)).wait_recv()
        for cp in first + passed:
            cp.wait_send()

    stacks = pl.pallas_call(
        body, out_shape=[_sds((4,) + v.shape, v.dtype) for v in arrs], in_specs=[_ANY] * na, out_specs=[_ANY] * na,
        scratch_shapes=_sems(6 * na), name=name)(*arrs)
    chip = 2 * lax.axis_index("x") + lax.axis_index("y")
    return [lax.dynamic_update_slice(s, v[None], (chip, 0, 0)) for s, v in zip(stacks, arrs)]


def _send_halves(gs, name):
    na = len(gs)

    def body(*refs):
        g_refs, o_refs, (send_sems, recv_sems) = refs[:na], refs[na:2 * na], refs[2 * na:]
        x, y, c, _ = _place()
        cps = []
        for a in range(na):
            r2 = g_refs[a].shape[1] // 2
            cps.append(pltpu.make_async_remote_copy(src_ref=g_refs[a].at[:, pl.ds((1 - c) * r2, r2), :], dst_ref=o_refs[a],
                                                    send_sem=send_sems.at[a], recv_sem=recv_sems.at[a],
                                                    device_id=(x, y, 1 - c), device_id_type=MESH))
        for cp in cps:
            cp.start()
        for cp in cps:
            cp.wait()

    return pl.pallas_call(
        body, out_shape=[_sds((4, v.shape[1] // 2, v.shape[2]), v.dtype) for v in gs], in_specs=[_ANY] * na, out_specs=[_ANY] * na,
        scratch_shapes=_sems(na), name=name)(*gs)


def _scatter_chips(arrs, name):
    na = len(arrs)

    def body(*refs):
        a_refs, t_refs, (send_sems, recv_sems) = refs[:na], refs[na:2 * na], refs[2 * na:]
        x, y, c, chips = _place()
        k = 2 * x + y

        def copy(a, j, src_k, dst_k, to):
            return pltpu.make_async_remote_copy(src_ref=a_refs[a].at[src_k], dst_ref=t_refs[a].at[dst_k], send_sem=send_sems.at[3 * a + j],
                                                recv_sem=recv_sems.at[3 * a + j], device_id=to, device_id_type=MESH)

        sends = [copy(a, j, 2 * chip[0] + chip[1], k, (*chip, c)) for a in range(na) for j, chip in enumerate(chips)]
        for cp in sends:
            cp.start()
        for a in range(na):
            for j, chip in enumerate(chips):
                copy(a, j, k, 2 * chip[0] + chip[1], (x, y, c)).wait_recv()
        for cp in sends:
            cp.wait_send()

    return pl.pallas_call(
        body, out_shape=[_sds(v.shape, v.dtype) for v in arrs], in_specs=[_ANY] * na, out_specs=[_ANY] * na,
        scratch_shapes=_sems(3 * na), name=name)(*arrs)


def _join_halves(fs, name):
    na = len(fs)

    def body(*refs):
        f_refs, o_refs, (send_sems, recv_sems) = refs[:na], refs[na:2 * na], refs[2 * na:]
        x, y, c, _ = _place()

        def copy(a, cc, to):
            r2 = f_refs[a].shape[0]
            return pltpu.make_async_remote_copy(src_ref=f_refs[a], dst_ref=o_refs[a].at[pl.ds(cc * r2, r2), :], send_sem=send_sems.at[a],
                                                recv_sem=recv_sems.at[a], device_id=to, device_id_type=MESH)

        cps = [copy(a, c, (x, y, 1 - c)) for a in range(na)]
        for cp in cps:
            cp.start()
        for a in range(na):
            copy(a, 1 - c, (x, y, c)).wait_recv()
        for cp in cps:
            cp.wait_send()

    outs = pl.pallas_call(
        body, out_shape=[_sds((2 * v.shape[0], v.shape[1]), v.dtype) for v in fs], in_specs=[_ANY] * na, out_specs=[_ANY] * na,
        scratch_shapes=_sems(na), name=name)(*fs)
    ci = lax.axis_index("c")
    return [lax.dynamic_update_slice(o, f, (ci * f.shape[0], 0)) for o, f in zip(outs, fs)]


def _sum_chips(a, t, chip, name):
    _, r2, n = t.shape
    tm = _row_tile(r2)

    def body(k_ref, a_ref, t1_ref, t2_ref, t3_ref, o_ref):
        o_ref[...] = ((a_ref[...].astype(F32) + t1_ref[...].astype(F32)) + t2_ref[...].astype(F32)) + t3_ref[...].astype(F32)

    def slot(j):
        return pl.BlockSpec((None, tm, n), lambda i, k_ref: (lax.rem(k_ref[0] + j, 4), i, 0))

    return pl.pallas_call(
        body, grid_spec=pltpu.PrefetchScalarGridSpec(num_scalar_prefetch=1, grid=(r2 // tm,), in_specs=[slot(0), slot(1), slot(2), slot(3)],
                                                     out_specs=pl.BlockSpec((tm, n), lambda i, k_ref: (i, 0))),
        out_shape=_sds((r2, n)), name=name, compiler_params=_PAR)(chip.reshape(1).astype(jnp.int32), a, t, t, t)


def _add_cast(g, recv, c, name):
    _, r2, n = recv.shape

    def body(c_ref, a_ref, b_ref, o_ref):
        o_ref[...] = (a_ref[...] + b_ref[...]).astype(BF16)

    spec = pl.BlockSpec((None, r2, n), lambda k, c_ref: (k, 0, 0))
    return pl.pallas_call(
        body, grid_spec=pltpu.PrefetchScalarGridSpec(
            num_scalar_prefetch=1, grid=(4,), in_specs=[pl.BlockSpec((None, r2, n), lambda k, c_ref: (k, c_ref[0], 0)), spec], out_specs=spec),
        out_shape=_sds(recv.shape, BF16), name=name, compiler_params=_PAR)(c.reshape(1).astype(jnp.int32), g, recv)


def _sum_lead(t, name, tm=256):
    P, R, n = t.shape
    tm = _row_tile(R, tm)

    def body(t_ref, o_ref):
        acc = t_ref[0].astype(F32)
        for j in range(1, P):
            acc = acc + t_ref[j].astype(F32)
        o_ref[...] = acc

    return pl.pallas_call(body, grid=(R // tm,), in_specs=[pl.BlockSpec((P, tm, n), lambda i: (0, i, 0))],
                          out_specs=pl.BlockSpec((tm, n), lambda i: (i, 0)), out_shape=_sds((R, n)), name=name, compiler_params=_PAR)(t)


def _ada_fwd(c16, ada_w, ada_b_cols, tn=512):
    L, _, n = ada_w.shape

    def body(c_ref, w_ref, b_ref, o_ref):
        o_ref[0] = _dot(_silu(c_ref[...]).astype(BF16), w_ref[0].astype(BF16), _NN) + b_ref[0]

    return pl.pallas_call(
        body, grid=(L, n // tn),
        in_specs=[pl.BlockSpec((16, D), lambda l, j: (0, 0)), pl.BlockSpec((1, D, tn), lambda l, j: (l, 0, j)), pl.BlockSpec((1, 1, tn), lambda l, j: (l, 0, j))],
        out_specs=pl.BlockSpec((1, 16, tn), lambda l, j: (l, 0, j)), out_shape=_sds((L, 16, n)), name="ada_fwd",
        compiler_params=pltpu.CompilerParams(dimension_semantics=("parallel", "parallel")))(c16, ada_w, ada_b_cols)


def _ada_bwd(c16, dmod, tn=512):
    L, _, n = dmod.shape

    def body(c_ref, d_ref, o_ref):
        o_ref[0] = _dot(_silu(c_ref[...]).astype(BF16), d_ref[0].astype(BF16), _TN)

    return pl.pallas_call(
        body, grid=(L, n // tn), in_specs=[pl.BlockSpec((16, D), lambda l, j: (0, 0)), pl.BlockSpec((1, 16, tn), lambda l, j: (l, 0, j))],
        out_specs=pl.BlockSpec((1, D, tn), lambda l, j: (l, 0, j)), out_shape=_sds((L, D, n)), name="ada_bwd",
        compiler_params=pltpu.CompilerParams(dimension_semantics=("parallel", "parallel")))(c16, dmod)


def _adam_math(w, g, m, v):
    mn = ADAM_B1 * m + (1.0 - ADAM_B1) * g
    vn = ADAM_B2 * v + (1.0 - ADAM_B2) * (g * g)
    m_hat = mn / (1.0 - ADAM_B1 ** ADAM_STEP)
    v_hat = vn / (1.0 - ADAM_B2 ** ADAM_STEP)
    return -ADAM_LR * (m_hat / (jnp.sqrt(v_hat) + ADAM_EPS) + ADAM_WD * w), mn, vn


def _adamw(w, g, m, v, name):
    R, n = w.shape
    tm = _row_tile(R)

    def body(w_ref, g_ref, m_ref, v_ref, d_ref, nm_ref, nv_ref):
        d_ref[...], nm_ref[...], nv_ref[...] = _adam_math(w_ref[...], g_ref[...], m_ref[...], v_ref[...])

    spec = pl.BlockSpec((tm, n), lambda i: (i, 0))
    return pl.pallas_call(body, grid=(R // tm,), in_specs=[spec] * 4, out_specs=(spec,) * 3, out_shape=(_sds((R, n)),) * 3,
                          name=name, compiler_params=_PAR)(w, g, m, v)


def _adamw_layers(w, g0, g1, m, v, name):
    _, r, n = w.shape
    tm = _row_tile(r)
    nb = r // tm

    def body(w_ref, g0_ref, g1_ref, m_ref, v_ref, g_ref, d_ref, nm_ref, nv_ref):
        gv = jnp.where(pl.program_id(0) == 0, g0_ref[...], g1_ref[...])
        g_ref[...] = gv
        d_ref[...], nm_ref[...], nv_ref[...] = _adam_math(w_ref[...], gv, m_ref[...], v_ref[...])

    spec = pl.BlockSpec((None, tm, n), lambda l, i: (l, i, 0))
    g0_spec = pl.BlockSpec((tm, n), lambda l, i: (i * (1 - l) + (nb - 1) * l, 0))
    g1_spec = pl.BlockSpec((tm, n), lambda l, i: (i * l, 0))
    return pl.pallas_call(body, grid=(2, nb), in_specs=[spec, g0_spec, g1_spec, spec, spec], out_specs=(spec,) * 4,
                          out_shape=(_sds(w.shape),) * 4, name=name,
                          compiler_params=pltpu.CompilerParams(dimension_semantics=("arbitrary", "arbitrary")))(w, g0, g1, m, v)


_W_NAMES = ["ada_w", "ada_b", "norm1_w", "w_in", "q_a_norm", "w_q_b", "kv_a_norm", "w_kv_b", "q_norm", "k_norm", "pool_w",
            "pool_scale", "ssd_conv_w", "ssd_conv_b", "ssd_dt_bias", "ssd_a_log", "ssd_d", "ssd_norm_w", "w_branch", "w_out",
            "norm2_w", "ffn_up", "ffn_conv_w", "ffn_conv_b", "ffn_down"]
_BIG = [("w_in", (D, IN_DIM // 4), 1), ("w_q_b", (Q_RANK, 192), 1), ("w_kv_b", (KV_RANK, 256), 1), ("w_branch", (512, D), 0),
        ("w_out", (256, D), 0), ("ffn_up", (D, 2 * FFN // 4), 1), ("ffn_down", (FFN // 4, D), 0)]

_SMALL = [("norm1_w", (D,)), ("q_a_norm", (Q_RANK,)), ("kv_a_norm", (KV_RANK,)), ("q_norm", (96,)), ("k_norm", (96,)),
          ("pool_w", (4, 128, 128)), ("pool_scale", (512,)), ("ssd_conv_w", (4, SSD_CONV_DIM)), ("ssd_conv_b", (SSD_CONV_DIM,)),
          ("ssd_dt_bias", (SSD_HEADS,)), ("ssd_a_log", (SSD_HEADS,)), ("ssd_d", (SSD_HEADS,)), ("ssd_norm_w", (D,)), ("norm2_w", (D,)),
          ("ffn_conv_w", (3, 2 * FFN)), ("ffn_conv_b", (2 * FFN,))]
_CONV_SHARDED = {"ssd_conv_w": SSD_CONV_DIM // 4, "ffn_conv_w": 2 * FFN // 4}


def _pack_flat(arrs, mult):
    flat = jnp.concatenate([a.astype(F32).reshape(-1) for a in arrs])
    rows = -(-flat.shape[0] // (128 * mult)) * mult
    return jnp.pad(flat, (0, rows * 128 - flat.shape[0])).reshape(rows, 128), [a.shape for a in arrs]


def _unpack_flat(packed, shapes):
    flat, out, off = packed.reshape(-1), [], 0
    for s in shapes:
        n = 1
        for d in s:
            n *= d
        out.append(flat[off:off + n].reshape(s))
        off += n
    return out


def _layer_weights(a, l, stacks, conv_full):
    p = {n: a[n][l] for n in _W_NAMES if n not in ("ada_w", "ada_b")}
    p.update({n: conv_full[n][l] for n in conv_full})
    p.update(w_in4=stacks["w_in"], w_q_b4=stacks["w_q_b"], w_kv_b4=stacks["w_kv_b"], ffn_up4=stacks["ffn_up"],
             w_branch=stacks["w_branch"].reshape(2048, D), w_out=stacks["w_out"].reshape(D, D), ffn_down=stacks["ffn_down"].reshape(FFN, D))
    return p


def _reduce_chips(gs, ci, tag):
    recv = _send_halves(gs, f"rs_halves_{tag}")
    chip_sum = [_add_cast(g, r, ci, f"rs_add_{tag}") for g, r in zip(gs, recv)]
    got = _scatter_chips(chip_sum, f"rs_scatter_{tag}")
    chip = 2 * lax.axis_index("x") + lax.axis_index("y")
    return _join_halves([_sum_chips(s, t, chip, f"rs_sum_{tag}") for s, t in zip(chip_sum, got)], f"rs_join_{tag}")


def kernel(x, c, positions, ada_w, ada_b, norm1_w, w_in, q_a_norm, w_q_b, kv_a_norm, w_kv_b, q_norm, k_norm, pool_w, pool_scale, ssd_conv_w, ssd_conv_b, ssd_dt_bias, ssd_a_log, ssd_d, ssd_norm_w, w_branch, w_out, norm2_w, ffn_up, ffn_conv_w, ffn_conv_b, ffn_down, loss_target, m_ada_w, m_ada_b, m_norm1_w, m_w_in, m_q_a_norm, m_w_q_b, m_kv_a_norm, m_w_kv_b, m_q_norm, m_k_norm, m_pool_w, m_pool_scale, m_ssd_conv_w, m_ssd_conv_b, m_ssd_dt_bias, m_ssd_a_log, m_ssd_d, m_ssd_norm_w, m_w_branch, m_w_out, m_norm2_w, m_ffn_up, m_ffn_conv_w, m_ffn_conv_b, m_ffn_down, v_ada_w, v_ada_b, v_norm1_w, v_w_in, v_q_a_norm, v_w_q_b, v_kv_a_norm, v_w_kv_b, v_q_norm, v_k_norm, v_pool_w, v_pool_scale, v_ssd_conv_w, v_ssd_conv_b, v_ssd_dt_bias, v_ssd_a_log, v_ssd_d, v_ssd_norm_w, v_w_branch, v_w_out, v_norm2_w, v_ffn_up, v_ffn_conv_w, v_ffn_conv_b, v_ffn_down):
    a = dict(locals())
    xi, yi, ci = lax.axis_index("x"), lax.axis_index("y"), lax.axis_index("c")
    chip = 2 * xi + yi
    dev = 2 * chip + ci
    ncol = 6 * D // 4

    c_all = _allgather8(c.reshape(8, 128), "gather_c").reshape(8, D)
    c16 = jnp.pad(c_all, ((0, 8), (0, 0)))
    ada_b_cols = lax.dynamic_slice_in_dim(ada_b, chip * ncol, ncol, axis=1).reshape(2, 1, ncol)
    mod_part = _ada_fwd(c16, ada_w, ada_b_cols)[:, :8]
    small1, shapes1 = _pack_flat([mod_part, ssd_conv_w, ffn_conv_w], 8)
    got1 = _allgather8(small1, "gather_mod").reshape(8, -1, 128)
    per_chip = [_unpack_flat(got1[2 * k], shapes1) for k in range(4)]
    mod_all = jnp.concatenate([per_chip[k][0] for k in range(4)], -1)
    conv_full = {"ssd_conv_w": jnp.concatenate([per_chip[k][1] for k in range(4)], -1),
                 "ffn_conv_w": jnp.concatenate([per_chip[k][2] for k in range(4)], -1)}
    mod_mine = lax.dynamic_index_in_dim(mod_all, dev, axis=1, keepdims=False).reshape(2, 6, D)
    mods = [jnp.pad(mod_mine[l], ((0, 2), (0, 0))) for l in range(2)]

    big = [n for n, _, _ in _BIG]
    layers = []
    for l in range(2):
        stacks = _gather_chips([a[n][l].astype(BF16) for n in big], f"gather_w{l}")
        layers.append(_layer_weights(a, l, dict(zip(big, stacks)), conv_full))

    lpart, grad_x, grads, dmods = _local_step(x[0], loss_target[0], positions[0], mods, layers)
    loss = lax.psum(lpart, ("x", "y", "c"))

    small2, shapes2 = _pack_flat([jnp.stack(dmods)] + [grads[l][n] for l in range(2) for n, _ in _SMALL], 16)
    got2 = _allgather8(small2, "gather_small").reshape(8, -1, 128)
    tot = _unpack_flat(_sum_lead(got2, "sum_small"), shapes2)
    g = {"ada_b": tot[0].reshape(2, 6 * D)}
    for i, (n, _) in enumerate(_SMALL):
        g[n] = jnp.stack([tot[1 + i], tot[1 + len(_SMALL) + i]])
    for n, w in _CONV_SHARDED.items():
        g[n] = lax.dynamic_slice_in_dim(g[n], chip * w, w, axis=2)
    nd = 2 * 6 * D // 128
    dmod_all = jnp.transpose(got2[:, :nd].reshape(8, 2, 6 * D), (1, 0, 2))
    dmod_cols = lax.dynamic_slice_in_dim(jnp.pad(dmod_all, ((0, 0), (0, 8), (0, 0))), chip * ncol, ncol, axis=2)
    g["ada_w"] = _ada_bwd(c16, dmod_cols)

    delta, new_m, new_v = {}, {}, {}
    reduced = [_reduce_chips([grads[l][n] for n in big], ci, l) for l in (1, 0)][::-1]
    for i, n in enumerate(big):
        g[n], delta[n], new_m[n], new_v[n] = _adamw_layers(a[n], reduced[0][i], reduced[1][i], a["m_" + n], a["v_" + n], f"adamw_{n}")
    shp = ada_w.shape
    r2 = lambda t: t.reshape(-1, shp[-1])
    delta["ada_w"], new_m["ada_w"], new_v["ada_w"] = (
        o.reshape(shp) for o in _adamw(r2(ada_w), r2(g["ada_w"]), r2(m_ada_w), r2(v_ada_w), "adamw_ada_w"))
    rest = [n for n in _W_NAMES if n not in big and n != "ada_w"]
    packs = [_pack_flat([t[n] if pre is None else t[pre + n] for n in rest], 128)[0]
             for t, pre in ((a, None), (g, None), (a, "m_"), (a, "v_"))]
    rest_shapes = [a[n].shape for n in rest]
    outs = [_unpack_flat(o, rest_shapes) for o in _adamw(*packs, "adamw_rest")]
    for i, n in enumerate(rest):
        delta[n], new_m[n], new_v[n] = outs[0][i], outs[1][i], outs[2][i]

    return (loss, grad_x[None], *[g[n] for n in _W_NAMES], *[delta[n] for n in _W_NAMES],
            *[new_m[n] for n in _W_NAMES], *[new_v[n] for n in _W_NAMES])
```

```python
import functools

import jax
import jax.numpy as jnp
from jax import lax
from jax.experimental import pallas as pl
from jax.experimental.pallas import tpu as pltpu

F32 = jnp.float32
BF16 = jnp.bfloat16
MESH = pl.DeviceIdType.MESH
HI = lax.Precision.HIGHEST

D = 1024
N_HEADS = 8
NOPE, ROPE_DIM = 64, 32
Q_RANK, KV_RANK = 384, 256
POOL_WINDOWS = (2, 4, 8, 16)
SSD_HEADS, SSD_P, SSD_N, SSD_L = 16, 64, 128, 128
SSD_INNER = 1024
SSD_CONV_DIM = 1536
FFN = 2816
EPS = 1e-6
ROPE_THETA = 10000.0
OG, OZ, OX, OP, OQ, OKV, ODT, IN_PAD = 0, 3072, 4096, 5632, 6144, 6528, 6912, 7040
IN_DIM = 6832
ADAM_LR, ADAM_B1, ADAM_B2, ADAM_EPS, ADAM_WD, ADAM_STEP = 0.001, 0.9, 0.999, 1e-08, 0.01, 10

_ARB = pltpu.CompilerParams(dimension_semantics=("arbitrary",))
_PAR = pltpu.CompilerParams(dimension_semantics=("parallel",))


def _pick(n, pref):
    if n <= pref:
        return n
    best = None
    for t in range(128, pref + 1, 128):
        if n % t == 0:
            best = t
    assert best is not None, (n, pref)
    return best


def _row_tile(r, cap=256):
    best = None
    for t in range(16, min(r, cap) + 1, 16):
        if r % t == 0:
            best = t
    assert best is not None, r
    return best


def _sds(shape, dtype=F32):
    return jax.ShapeDtypeStruct(tuple(shape), dtype)


def _iota(shape, dim):
    return lax.broadcasted_iota(jnp.int32, shape, dim)


def _sigmoid(x):
    return 0.5 * jnp.tanh(0.5 * x) + 0.5


def _silu(x):
    return x * _sigmoid(x)


def _dsilu(x):
    s = _sigmoid(x)
    return s * (1.0 + x * (1.0 - s))


def _dot(a, b, dims):
    return lax.dot_general(a, b, (dims, ((), ())), preferred_element_type=F32)


_NN, _NT, _TN = ((1,), (0,)), ((1,), (1,)), ((0,), (0,))


def _dot_hi(a, b, dims=_NN):
    return lax.dot_general(a, b, (dims, ((), ())), preferred_element_type=F32, precision=HI)


def _shift_down(x, j):
    n = x.shape[0]
    return jnp.where(_iota(x.shape, 0) >= j, pltpu.roll(x, j, 0), 0.0)


def _shift_up(x, j):
    n = x.shape[0]
    return jnp.where(_iota(x.shape, 0) < n - j, pltpu.roll(x, n - j, 0), 0.0)


def _mm(a, b, mode="nn", out_dtype=F32, tm=512, tn=512, tk=4096, res=None, gate=None, name="mm"):
    if mode == "nn":
        (M, K), (K2, N) = a.shape, b.shape
    elif mode == "nt":
        (M, K), (N, K2) = a.shape, b.shape
    else:
        (K, M), (K2, N) = a.shape, b.shape
    assert K == K2, (a.shape, b.shape, mode)
    tm, tn, tk = _pick(M, tm), _pick(N, tn), _pick(K, tk)
    nk = K // tk
    dims = {"nn": _NN, "nt": _NT, "tn": _TN}[mode]
    fused = res is not None

    def body(*refs):
        a_ref, b_ref = refs[:2]

        def finish(acc):
            if fused:
                r_ref, g_ref, o_ref, raw_ref = refs[2:6]
                raw_ref[...] = acc
                o_ref[...] = r_ref[...] + g_ref[...] * acc
            else:
                refs[2][...] = acc.astype(out_dtype)

        _mm_steps(a_ref, b_ref, dims, nk, refs[-1] if nk > 1 else None, finish)

    if mode == "nn":
        a_spec = pl.BlockSpec((tm, tk), lambda i, j, k: (i, k))
        b_spec = pl.BlockSpec((tk, tn), lambda i, j, k: (k, j))
    elif mode == "nt":
        a_spec = pl.BlockSpec((tm, tk), lambda i, j, k: (i, k))
        b_spec = pl.BlockSpec((tn, tk), lambda i, j, k: (j, k))
    else:
        a_spec = pl.BlockSpec((tk, tm), lambda i, j, k: (k, i))
        b_spec = pl.BlockSpec((tk, tn), lambda i, j, k: (k, j))
    o_spec = pl.BlockSpec((tm, tn), lambda i, j, k: (i, j))
    in_specs, args = [a_spec, b_spec], [a, b]
    out_shape, out_specs = _sds((M, N), out_dtype), o_spec
    if fused:
        in_specs += [o_spec, pl.BlockSpec((1, tn), lambda i, j, k: (0, j))]
        args += [res, gate]
        out_shape, out_specs = (_sds((M, N)), _sds((M, N))), (o_spec, o_spec)
    return pl.pallas_call(
        body, grid=(M // tm, N // tn, nk), in_specs=in_specs, out_specs=out_specs, out_shape=out_shape,
        scratch_shapes=[pltpu.VMEM((tm, tn), F32)] if nk > 1 else [], name=name,
        compiler_params=pltpu.CompilerParams(dimension_semantics=("parallel", "parallel", "arbitrary")),
    )(*args)


def _mm_steps(a_ref, b_ref, dims, nk, acc_ref, finish):
    part = _dot(a_ref[...].astype(BF16), b_ref[...].astype(BF16), dims)
    if nk == 1:
        finish(part)
        return
    k = pl.program_id(2)

    @pl.when(k == 0)
    def _():
        acc_ref[...] = part

    @pl.when(k > 0)
    def _():
        acc_ref[...] += part

    @pl.when(k == nk - 1)
    def _():
        finish(acc_ref[...])


def _mm_blocks(a, b, dims, grid, a_spec, b_spec, o_spec, out_shape, acc_shape, name):
    nk = grid[2]

    def body(a_ref, b_ref, o_ref, *scratch):
        def finish(acc):
            o_ref[...] = acc.astype(o_ref.dtype)

        _mm_steps(a_ref, b_ref, dims, nk, scratch[0] if nk > 1 else None, finish)

    return pl.pallas_call(
        body, grid=grid, in_specs=[a_spec, b_spec], out_specs=o_spec, out_shape=out_shape,
        scratch_shapes=[pltpu.VMEM(acc_shape, F32)] if nk > 1 else [], name=name,
        compiler_params=pltpu.CompilerParams(dimension_semantics=("parallel", "parallel", "arbitrary")),
    )(a, b)


_UP_SHARD = 2 * FFN // 4


def _up_fwd(h2, wup4, name, tm=512):
    S = h2.shape[0]
    tm = min(tm, S)
    return _mm_blocks(h2, wup4, _NN, (S // tm, 4, 1), pl.BlockSpec((tm, D), lambda i, j, k: (i, 0)),
                      pl.BlockSpec((None, D, _UP_SHARD), lambda i, j, k: (j, 0, 0)), pl.BlockSpec((tm, _UP_SHARD), lambda i, j, k: (i, j)),
                      _sds((S, 2 * FFN)), (tm, _UP_SHARD), name)


def _up_dx(dup2, wup4, name, tm=512, tn=512):
    S = dup2.shape[1]
    tm = min(tm, S)
    return _mm_blocks(dup2, wup4, _NT, (S // tm, D // tn, 4), pl.BlockSpec((None, tm, _UP_SHARD), lambda i, j, k: (lax.div(k, 2), i, lax.rem(k, 2))),
                      pl.BlockSpec((None, tn, _UP_SHARD), lambda i, j, k: (k, j, 0)), pl.BlockSpec((tm, tn), lambda i, j, k: (i, j)),
                      _sds((S, D)), (tm, tn), name)


def _up_dw(h2, dup2, name, tm=512, tk=4096):
    S = h2.shape[0]
    tk = min(tk, S)
    return _mm_blocks(h2, dup2, _TN, (D // tm, 4, S // tk), pl.BlockSpec((tk, tm), lambda i, j, k: (k, i)),
                      pl.BlockSpec((None, tk, _UP_SHARD), lambda i, j, k: (lax.div(j, 2), k, lax.rem(j, 2))),
                      pl.BlockSpec((None, tm, _UP_SHARD), lambda i, j, k: (j, i, 0)), _sds((4, D, _UP_SHARD)), (tm, _UP_SHARD), name)


def _row_spec(tm, n):
    return pl.BlockSpec((tm, n), lambda i: (i, 0))


def _vec_spec(n, rows=1):
    return pl.BlockSpec((rows, n), lambda i: (0, 0))


def _ln_mod(x, nw, sc, sh, name, tm=256):
    S = x.shape[0]

    def body(x_ref, nw_ref, sc_ref, sh_ref, o_ref):
        xv = x_ref[...]
        r = lax.rsqrt(jnp.mean(xv * xv, -1, keepdims=True) + EPS)
        o_ref[...] = ((xv * r * nw_ref[...]) * (1.0 + sc_ref[...]) + sh_ref[...]).astype(BF16)

    return pl.pallas_call(
        body, grid=(S // tm,), in_specs=[_row_spec(tm, D)] + [_vec_spec(D)] * 3, out_specs=_row_spec(tm, D),
        out_shape=_sds((S, D), BF16), name=name, compiler_params=_PAR)(x, nw, sc, sh)


def _ln_mod_bwd(x, dh, dres, nw, sc, name, tm=256):
    S = x.shape[0]

    def body(x_ref, dh_ref, dres_ref, nw_ref, sc_ref, dx_ref, st_ref):
        @pl.when(pl.program_id(0) == 0)
        def _():
            st_ref[...] = jnp.zeros_like(st_ref)

        xv, dhv, nwv = x_ref[...], dh_ref[...], nw_ref[...]
        r = lax.rsqrt(jnp.mean(xv * xv, -1, keepdims=True) + EPS)
        xhat = xv * r
        dn = dhv * (1.0 + sc_ref[...])
        g = dn * nwv
        dx_ref[...] = dres_ref[...] + r * (g - xhat * jnp.mean(g * xhat, -1, keepdims=True))
        st_ref[0:1, :] += jnp.sum(dhv, 0, keepdims=True)
        st_ref[1:2, :] += jnp.sum(dhv * (xhat * nwv), 0, keepdims=True)
        st_ref[2:3, :] += jnp.sum(dn * xhat, 0, keepdims=True)

    return pl.pallas_call(
        body, grid=(S // tm,), in_specs=[_row_spec(tm, D)] * 3 + [_vec_spec(D)] * 2,
        out_specs=(_row_spec(tm, D), _vec_spec(D, 8)), out_shape=(_sds((S, D)), _sds((8, D))),
        name=name, compiler_params=_ARB)(x, dh, dres, nw, sc)


def _gate_bwd(dx, out, g, name, tm=256):
    S = dx.shape[0]

    def body(dx_ref, o_ref, g_ref, dz_ref, dg_ref):
        @pl.when(pl.program_id(0) == 0)
        def _():
            dg_ref[...] = jnp.zeros_like(dg_ref)

        dxv = dx_ref[...]
        dz_ref[...] = (dxv * g_ref[...]).astype(BF16)
        dg_ref[0:1, :] += jnp.sum(dxv * o_ref[...], 0, keepdims=True)

    return pl.pallas_call(
        body, grid=(S // tm,), in_specs=[_row_spec(tm, D)] * 2 + [_vec_spec(D)],
        out_specs=(_row_spec(tm, D), _vec_spec(D, 8)), out_shape=(_sds((S, D), BF16), _sds((8, D))),
        name=name, compiler_params=_ARB)(dx, out, g)


def _loss_grad(y, t, tm=256):
    S = y.shape[0]

    def body(y_ref, t_ref, dy_ref, l_ref):
        @pl.when(pl.program_id(0) == 0)
        def _():
            l_ref[...] = jnp.zeros_like(l_ref)

        e = y_ref[...] - t_ref[...]
        dy_ref[...] = e * (1.0 / D)
        l_ref[...] += 0.5 * jnp.sum(jnp.mean(e * e, -1, keepdims=True), 0, keepdims=True)

    return pl.pallas_call(
        body, grid=(S // tm,), in_specs=[_row_spec(tm, D)] * 2,
        out_specs=(_row_spec(tm, D), pl.BlockSpec((8, 128), lambda i: (0, 0))),
        out_shape=(_sds((S, D)), _sds((8, 128))), name="loss_grad", compiler_params=_ARB)(y, t)


def _conv(x, w, b):
    K = w.shape[0]
    acc = x * w[K - 1:K, :] + b
    for j in range(1, K):
        acc = acc + _shift_down(x, j) * w[K - 1 - j:K - j, :]
    return acc


def _conv_bwd(x, w, dc):
    K = w.shape[0]
    dx = dc * w[K - 1:K, :]
    dws = [jnp.sum(dc * x, 0, keepdims=True)]
    for j in range(1, K):
        dx = dx + _shift_up(dc, j) * w[K - 1 - j:K - j, :]
        dws.append(jnp.sum(dc * _shift_down(x, j), 0, keepdims=True))
    return dx, dws[::-1], jnp.sum(dc, 0, keepdims=True)


def _col_spec(S, tc, off=0):
    return pl.BlockSpec((S, tc), lambda j: (0, j + off))


def _ssd_pre(proj, cw, cb, tc=256):
    S, n = proj.shape[0], SSD_CONV_DIM

    def body(x_ref, w_ref, b_ref, o_ref):
        o_ref[...] = _silu(_conv(x_ref[...], w_ref[...], b_ref[...]))

    return pl.pallas_call(
        body, grid=(n // tc,),
        in_specs=[_col_spec(S, tc, OX // tc), pl.BlockSpec((4, tc), lambda j: (0, j)), pl.BlockSpec((1, tc), lambda j: (0, j))],
        out_specs=_col_spec(S, tc), out_shape=_sds((S, n)), name="ssd_pre", compiler_params=_PAR)(proj, cw, cb)


def _ssd_pre_bwd(proj, dxc, cw, cb, tc=256):
    S, n = proj.shape[0], SSD_CONV_DIM

    def body(x_ref, d_ref, w_ref, b_ref, dx_ref, dw_ref, db_ref):
        xv, wv = x_ref[...], w_ref[...]
        dc = d_ref[...] * _dsilu(_conv(xv, wv, b_ref[...]))
        dx, dw, db = _conv_bwd(xv, wv, dc)
        dx_ref[...] = dx.astype(BF16)
        for k, row in enumerate(dw):
            dw_ref[k:k + 1, :] = row
        db_ref[...] = db

    wspec, bspec = pl.BlockSpec((4, tc), lambda j: (0, j)), pl.BlockSpec((1, tc), lambda j: (0, j))
    return pl.pallas_call(
        body, grid=(n // tc,), in_specs=[_col_spec(S, tc, OX // tc), _col_spec(S, tc), wspec, bspec],
        out_specs=(_col_spec(S, tc), wspec, bspec), out_shape=(_sds((S, n), BF16), _sds((4, n)), _sds((1, n))),
        name="ssd_pre_bwd", compiler_params=_PAR)(proj, dxc, cw, cb)


def _ffn_act(up, cw, cb, tc=256):
    S, nb = up.shape[0], FFN // tc

    def body(g_ref, v_ref, wg_ref, wv_ref, bg_ref, bv_ref, o_ref):
        o_ref[...] = (_silu(_conv(g_ref[...], wg_ref[...], bg_ref[...])) * _conv(v_ref[...], wv_ref[...], bv_ref[...])).astype(BF16)

    def wspec(off):
        return pl.BlockSpec((3, tc), lambda j: (0, j + off))

    def bspec(off):
        return pl.BlockSpec((1, tc), lambda j: (0, j + off))

    return pl.pallas_call(
        body, grid=(nb,), in_specs=[_col_spec(S, tc), _col_spec(S, tc, nb), wspec(0), wspec(nb), bspec(0), bspec(nb)],
        out_specs=_col_spec(S, tc), out_shape=_sds((S, FFN), BF16), name="ffn_act", compiler_params=_PAR)(up, up, cw, cw, cb, cb)


def _ffn_act_bwd(up, dact, cw, cb, tc=256):
    S, nb = up.shape[0], FFN // tc

    def body(g_ref, v_ref, d_ref, wg_ref, wv_ref, bg_ref, bv_ref, dx_ref, dw_ref, db_ref):
        gv, vv, wg, wv, da = g_ref[...], v_ref[...], wg_ref[...], wv_ref[...], d_ref[...]
        cg, cv = _conv(gv, wg, bg_ref[...]), _conv(vv, wv, bv_ref[...])
        for half, (dx, dw, db) in enumerate((_conv_bwd(gv, wg, da * cv * _dsilu(cg)), _conv_bwd(vv, wv, da * _silu(cg)))):
            dx_ref[half] = dx.astype(BF16)
            for k in range(3):
                dw_ref[half, k:k + 1, :] = dw[k]
            db_ref[half] = db

    def wspec(off):
        return pl.BlockSpec((3, tc), lambda j: (0, j + off))

    def bspec(off):
        return pl.BlockSpec((1, tc), lambda j: (0, j + off))

    cs = _col_spec(S, tc)
    both = lambda r: pl.BlockSpec((2, r, tc), lambda j: (0, 0, j))
    return pl.pallas_call(
        body, grid=(nb,), in_specs=[cs, _col_spec(S, tc, nb), cs, wspec(0), wspec(nb), bspec(0), bspec(nb)],
        out_specs=(both(S), both(3), both(1)), out_shape=(_sds((2, S, FFN), BF16), _sds((2, 3, FFN)), _sds((2, 1, FFN))),
        name="ffn_act_bwd", compiler_params=_PAR)(up, up, dact, cw, cw, cb, cb)


def _window_sum(x, w, up=False):
    shift = _shift_up if up else _shift_down
    j = 1
    while j < w:
        x = x + shift(x, j)
        j *= 2
    return x


def _pool_fwd(proj, pool_w, pool_scale):
    S = proj.shape[0]

    def body(u_ref, w_ref, s_ref, o_ref):
        cnt_row = (_iota((S, 128), 0) + 1).astype(F32)
        for g, w in enumerate(POOL_WINDOWS):
            sl = slice(g * 128, (g + 1) * 128)
            u = u_ref[:, sl]
            pooled = _window_sum(u, w) / jnp.minimum(cnt_row, float(w)) - u
            mixed = _dot(pooled.astype(BF16), w_ref[g].astype(BF16), _NN)
            o_ref[:, sl] = (mixed * s_ref[:, sl]).astype(BF16)

    return pl.pallas_call(
        body, grid=(1,),
        in_specs=[pl.BlockSpec((S, 512), lambda i: (0, OP // 512)), pl.BlockSpec((4, 128, 128), lambda i: (0, 0, 0)), _vec_spec(512)],
        out_specs=pl.BlockSpec((S, 512), lambda i: (0, 0)), out_shape=_sds((S, 512), BF16), name="pool_fwd",
        compiler_params=_ARB)(proj, pool_w, pool_scale)


def _pool_bwd(proj, dob, pool_w, pool_scale):
    S = proj.shape[0]

    def body(u_ref, d_ref, w_ref, s_ref, du_ref, dw_ref, ds_ref):
        cnt_row = (_iota((S, 128), 0) + 1).astype(F32)
        for g, w in enumerate(POOL_WINDOWS):
            sl = slice(g * 128, (g + 1) * 128)
            u, dv, wv = u_ref[:, sl], d_ref[:, sl], w_ref[g].astype(BF16)
            cnt = jnp.minimum(cnt_row, float(w))
            pooled = (_window_sum(u, w) / cnt - u).astype(BF16)
            ds_ref[:, sl] = jnp.sum(dv * _dot(pooled, wv, _NN), 0, keepdims=True)
            dmix = (dv * s_ref[:, sl]).astype(BF16)
            dw_ref[g] = _dot(pooled, dmix, _TN)
            dp = _dot(dmix, wv, _NT)
            du_ref[:, sl] = (_window_sum(dp / cnt, w, up=True) - dp).astype(BF16)

    blk = pl.BlockSpec((S, 512), lambda i: (0, 0))
    wspec = pl.BlockSpec((4, 128, 128), lambda i: (0, 0, 0))
    return pl.pallas_call(
        body, grid=(1,), in_specs=[pl.BlockSpec((S, 512), lambda i: (0, OP // 512)), blk, wspec, _vec_spec(512)],
        out_specs=(blk, wspec, _vec_spec(512)), out_shape=(_sds((S, 512), BF16), _sds((4, 128, 128)), _sds((1, 512))),
        name="pool_bwd", compiler_params=_ARB)(proj, dob, pool_w, pool_scale)


def _branch_specs():
    return [pl.BlockSpec((512, D), lambda i: (0, 0)), pl.BlockSpec((512, D), lambda i: (1, 0)), pl.BlockSpec((1024, D), lambda i: (1, 0))]


def _merge_fwd(oa, ob, oc, proj, wbr, tm=256):
    S = oa.shape[0]

    def body(oa_ref, ob_ref, oc_ref, gl_ref, wa_ref, wb_ref, wc_ref, o_ref):
        acc = _sigmoid(gl_ref[:, 0:D]) * _dot(oa_ref[...], wa_ref[...], _NN)
        acc += _sigmoid(gl_ref[:, D:2 * D]) * _dot(ob_ref[...], wb_ref[...], _NN)
        acc += _sigmoid(gl_ref[:, 2 * D:3 * D]) * _dot(oc_ref[...], wc_ref[...], _NN)
        o_ref[...] = acc.astype(BF16)

    return pl.pallas_call(
        body, grid=(S // tm,),
        in_specs=[_row_spec(tm, 512), _row_spec(tm, 512), _row_spec(tm, D), _row_spec(tm, 3 * D)] + _branch_specs(),
        out_specs=_row_spec(tm, D), out_shape=_sds((S, D), BF16), name="merge_fwd", compiler_params=_PAR)(oa, ob, oc, proj, wbr, wbr, wbr)


def _merge_bwd(dm, oa, ob, oc, proj, wbr, tm=256):
    S = oa.shape[0]

    def body(dm_ref, oa_ref, ob_ref, oc_ref, gl_ref, wa_ref, wb_ref, wc_ref, dya_ref, dyb_ref, dyc_ref, dgl_ref, doa_ref, dob_ref, doc_ref):
        dmv = dm_ref[...]
        for i, (o_ref, w_ref, dy_ref, do_ref) in enumerate(
                ((oa_ref, wa_ref, dya_ref, doa_ref), (ob_ref, wb_ref, dyb_ref, dob_ref), (oc_ref, wc_ref, dyc_ref, doc_ref))):
            gt = _sigmoid(gl_ref[:, i * D:(i + 1) * D])
            wv = w_ref[...]
            yv = _dot(o_ref[...], wv, _NN)
            dy = (dmv * gt).astype(BF16)
            dy_ref[...] = dy
            dgl_ref[:, i * D:(i + 1) * D] = (dmv * yv * gt * (1.0 - gt)).astype(BF16)
            do_ref[...] = _dot(dy, wv, _NT)

    rs = _row_spec
    return pl.pallas_call(
        body, grid=(S // tm,),
        in_specs=[rs(tm, D), rs(tm, 512), rs(tm, 512), rs(tm, D), rs(tm, 3 * D)] + _branch_specs(),
        out_specs=(rs(tm, D), rs(tm, D), rs(tm, D), rs(tm, 3 * D), rs(tm, 512), rs(tm, 512), rs(tm, D)),
        out_shape=(_sds((S, D), BF16), _sds((S, D), BF16), _sds((S, D), BF16), _sds((S, 3 * D), BF16), _sds((S, 512)), _sds((S, 512)), _sds((S, D))),
        name="merge_bwd", compiler_params=_PAR)(dm, oa, ob, oc, proj, wbr, wbr, wbr)


def _rope_tab(posb, invf, tm=256):
    S = posb.shape[0]

    def body(p_ref, f_ref, c_ref, s1_ref, s2_ref):
        ang = p_ref[...] * f_ref[...]
        lane = _iota(ang.shape, 1)
        cs, sn = jnp.cos(ang), jnp.sin(ang)
        c_ref[...] = jnp.where(lane < NOPE, 1.0, cs)
        s1_ref[...] = jnp.where((lane >= 64) & (lane < 80), -sn, 0.0)
        s2_ref[...] = jnp.where((lane >= 80) & (lane < 96), sn, 0.0)

    rs = _row_spec(tm, 128)
    return pl.pallas_call(body, grid=(S // tm,), in_specs=[rs, _vec_spec(128)], out_specs=(rs, rs, rs),
                          out_shape=(_sds((S, 128)),) * 3, name="rope_tab", compiler_params=_PAR)(posb, invf)


def _rope(u, C, S1, S2):
    return u * C + pltpu.roll(u, 112, 1) * S1 + pltpu.roll(u, 16, 1) * S2


def _rope_t(dy, C, S1, S2):
    return dy * C + pltpu.roll(dy * S1, 16, 1) + pltpu.roll(dy * S2, 112, 1)


def _seg_sum(v, mask):
    return jnp.sum(jnp.where(mask, v, 0.0), -1, keepdims=True)


def _mla_latents(pq_ref, pkv_ref, wqb_ref, wkvb_ref, qan_ref, kvan_ref):
    ql, kvl = pq_ref[...], pkv_ref[...]
    ckv, kr = kvl[:, 0:KV_RANK], kvl[:, KV_RANK:KV_RANK + 128]
    rq = lax.rsqrt(jnp.mean(ql * ql, -1, keepdims=True) + EPS)
    rkv = lax.rsqrt(jnp.mean(ckv * ckv, -1, keepdims=True) + EPS)
    nq = (ql * rq * qan_ref[...]).astype(BF16)
    nkv = (ckv * rkv * kvan_ref[...]).astype(BF16)
    kv = jnp.concatenate([_dot(nkv, wkvb_ref[k], _NN) for k in range(4)], 1)
    return ql, ckv, kr, rq, rkv, nq, nkv, _dot(nq, wqb_ref[...], _NN), kv


def _mla_specs(tm):
    full = lambda r, n: pl.BlockSpec((r, n), lambda i: (0, 0))
    return ([pl.BlockSpec((tm, 384), lambda i: (i, OQ // 384)), pl.BlockSpec((tm, 384), lambda i: (i, OKV // 384))],
            [full(Q_RANK, D), pl.BlockSpec((4, KV_RANK, 256), lambda i: (0, 0, 0)), _vec_spec(Q_RANK), _vec_spec(KV_RANK), _vec_spec(128), _vec_spec(128)]
            + [_row_spec(tm, 128)] * 3)


def _mla_prep(proj, wqb, wkvb, qan, kvan, wq, wk, rope, tm=256):
    S = proj.shape[0]

    def body(pq_ref, pkv_ref, wqb_ref, wkvb_ref, qan_ref, kvan_ref, wq_ref, wk_ref, c_ref, s1_ref, s2_ref, qf_ref, kf_ref, ve_ref):
        _, _, kr, _, _, _, _, q, kv = _mla_latents(pq_ref, pkv_ref, wqb_ref, wkvb_ref, qan_ref, kvan_ref)
        C, S1, S2, wqv, wkv = c_ref[...], s1_ref[...], s2_ref[...], wq_ref[...], wk_ref[...]
        lane = _iota((tm, 128), 1)
        mn, mr = lane < 64, (lane >= 64) & (lane < 96)
        rrk = lax.rsqrt(_seg_sum(kr * kr, mr) / ROPE_DIM + EPS)
        ykr = _rope(jnp.where(mr, kr * rrk * wkv, 0.0), C, S1, S2)
        for h in range(N_HEADS):
            sl = slice(h * 128, (h + 1) * 128)
            t = q[:, sl]
            rn = lax.rsqrt(_seg_sum(t * t, mn) / NOPE + EPS)
            rr = lax.rsqrt(_seg_sum(t * t, mr) / ROPE_DIM + EPS)
            qf_ref[:, sl] = _rope(t * jnp.where(mn, rn, jnp.where(mr, rr, 0.0)) * wqv, C, S1, S2).astype(BF16)
            t = kv[:, sl]
            rn = lax.rsqrt(_seg_sum(t * t, mn) / NOPE + EPS)
            kf_ref[:, sl] = (jnp.where(mn, t * rn * wkv, 0.0) + ykr).astype(BF16)
            ve_ref[:, sl] = (jnp.where(mn, pltpu.roll(t, 64, 1), 0.0) if h % 2 == 0 else jnp.where(mn, 0.0, t)).astype(BF16)

    pspecs, wspecs = _mla_specs(tm)
    rs = _row_spec(tm, D)
    return pl.pallas_call(body, grid=(S // tm,), in_specs=pspecs + wspecs, out_specs=(rs, rs, rs),
                          out_shape=(_sds((S, D), BF16),) * 3, name="mla_prep", compiler_params=_PAR)(
        proj, proj, wqb, wkvb, qan, kvan, wq, wk, *rope)


def _mla_prep_bwd(proj, dqf, dkf, dve, wqb, wkvb, qan, kvan, wq, wk, rope, tm=256):
    S = proj.shape[0]

    def body(pq_ref, pkv_ref, wqb_ref, wkvb_ref, qan_ref, kvan_ref, wq_ref, wk_ref, c_ref, s1_ref, s2_ref,
             dqf_ref, dkf_ref, dve_ref, dqs_ref, dkvs_ref, dwqb_ref, dwkvb_ref, st_ref, dq_scr, dkv_scr):
        @pl.when(pl.program_id(0) == 0)
        def _():
            dwqb_ref[...] = jnp.zeros_like(dwqb_ref)
            dwkvb_ref[...] = jnp.zeros_like(dwkvb_ref)
            st_ref[...] = jnp.zeros_like(st_ref)

        ql, ckv, kr, rq, rkv, nq, nkv, q, kv = _mla_latents(pq_ref, pkv_ref, wqb_ref, wkvb_ref, qan_ref, kvan_ref)
        C, S1, S2, wqv, wkv = c_ref[...], s1_ref[...], s2_ref[...], wq_ref[...], wk_ref[...]
        lane = _iota((tm, 128), 1)
        mn, mr = lane < 64, (lane >= 64) & (lane < 96)
        dwq = jnp.zeros((1, 128), F32)
        dwk = jnp.zeros((1, 128), F32)
        dykr = jnp.zeros((tm, 128), F32)
        for h in range(N_HEADS):
            sl = slice(h * 128, (h + 1) * 128)
            t = q[:, sl]
            rn = lax.rsqrt(_seg_sum(t * t, mn) / NOPE + EPS)
            rr = lax.rsqrt(_seg_sum(t * t, mr) / ROPE_DIM + EPS)
            scale = jnp.where(mn, rn, jnp.where(mr, rr, 0.0))
            that = t * scale
            du = _rope_t(dqf_ref[:, sl], C, S1, S2)
            dwq += jnp.sum(du * that, 0, keepdims=True)
            g = du * wqv
            gt = g * that
            dq_scr[:, sl] = scale * (g - that * jnp.where(mn, _seg_sum(gt, mn) / NOPE, _seg_sum(gt, mr) / ROPE_DIM))
            t = kv[:, sl]
            rn = lax.rsqrt(_seg_sum(t * t, mn) / NOPE + EPS)
            that = jnp.where(mn, t * rn, 0.0)
            dkf = dkf_ref[:, sl]
            dykr += jnp.where(mr, dkf, 0.0)
            dkn = jnp.where(mn, dkf, 0.0)
            dwk += jnp.sum(dkn * that, 0, keepdims=True)
            g = dkn * wkv
            dve = dve_ref[:, sl]
            dkv_scr[:, sl] = jnp.where(mn, rn * (g - that * (jnp.sum(g * that, -1, keepdims=True) / NOPE)),
                                       pltpu.roll(dve, 64, 1) if h % 2 == 0 else dve)
        rrk = lax.rsqrt(_seg_sum(kr * kr, mr) / ROPE_DIM + EPS)
        that = jnp.where(mr, kr * rrk, 0.0)
        dukr = jnp.where(mr, _rope_t(dykr, C, S1, S2), 0.0)
        dwk += jnp.sum(dukr * that, 0, keepdims=True)
        g = dukr * wkv
        dkr = rrk * (g - that * (jnp.sum(g * that, -1, keepdims=True) / ROPE_DIM))
        dqv, dkvv = dq_scr[...].astype(BF16), dkv_scr[...].astype(BF16)
        dnq = _dot(dqv, wqb_ref[...], _NT)
        dwqb_ref[...] += _dot(nq, dqv, _TN)
        dnkv = jnp.zeros((tm, KV_RANK), F32)
        for k in range(4):
            dnkv += _dot(dkvv[:, k * 256:(k + 1) * 256], wkvb_ref[k], _NT)
            dwkvb_ref[k] += _dot(nkv, dkvv[:, k * 256:(k + 1) * 256], _TN)
        xhat = ql * rq
        st_ref[0:1, 0:Q_RANK] += jnp.sum(dnq * xhat, 0, keepdims=True)
        g = dnq * qan_ref[...]
        dqs_ref[...] = (rq * (g - xhat * jnp.mean(g * xhat, -1, keepdims=True))).astype(BF16)
        xhat = ckv * rkv
        st_ref[1:2, 0:KV_RANK] += jnp.sum(dnkv * xhat, 0, keepdims=True)
        g = dnkv * kvan_ref[...]
        dkvs_ref[:, 0:KV_RANK] = (rkv * (g - xhat * jnp.mean(g * xhat, -1, keepdims=True))).astype(BF16)
        dkvs_ref[:, KV_RANK:KV_RANK + 128] = dkr.astype(BF16)
        st_ref[2:3, 0:128] += dwq
        st_ref[3:4, 0:128] += dwk

    pspecs, wspecs = _mla_specs(tm)
    rs = _row_spec(tm, D)
    full = lambda r, n: pl.BlockSpec((r, n), lambda i: (0, 0))
    return pl.pallas_call(
        body, grid=(S // tm,), in_specs=pspecs + wspecs + [rs, rs, rs],
        out_specs=(_row_spec(tm, 384), _row_spec(tm, 384), full(Q_RANK, D), pl.BlockSpec((4, KV_RANK, 256), lambda i: (0, 0, 0)), full(8, D)),
        out_shape=(_sds((S, 384), BF16), _sds((S, 384), BF16), _sds((Q_RANK, D)), _sds((4, KV_RANK, 256)), _sds((8, D))),
        scratch_shapes=[pltpu.VMEM((tm, D), F32), pltpu.VMEM((tm, D), F32)], name="mla_prep_bwd", compiler_params=_ARB)(
        proj, proj, wqb, wkvb, qan, kvan, wq, wk, *rope, dqf, dkf, dve)


_ATT_SCALE = (NOPE + ROPE_DIM) ** -0.5


def _att_probs(q, k, i, tq):
    n = k.shape[0]
    s = _dot(q, k, _NT) * _ATT_SCALE
    s = jnp.where(_iota((tq, n), 1) <= i * tq + _iota((tq, n), 0), s, -1e30)
    p = jnp.exp(s - jnp.max(s, -1, keepdims=True))
    return p / jnp.sum(p, -1, keepdims=True)


def _attn_fwd(qf, kf, ve, tq=256):
    S = qf.shape[0]

    def body(q_ref, k_ref, v_ref, o_ref):
        for i in range(S // tq):
            n, rows = (i + 1) * tq, slice(i * tq, (i + 1) * tq)
            acc = jnp.zeros((tq, 128), F32)
            for hh in range(2):
                sl = slice(hh * 128, (hh + 1) * 128)
                p = _att_probs(q_ref[rows, sl], k_ref[0:n, sl], i, tq)
                acc += _dot(p.astype(BF16), v_ref[0:n, sl], _NN)
            o_ref[rows, :] = acc.astype(BF16)

    ps = pl.BlockSpec((S, 256), lambda h: (0, h))
    return pl.pallas_call(body, grid=(N_HEADS // 2,), in_specs=[ps, ps, ps], out_specs=pl.BlockSpec((S, 128), lambda h: (0, h)),
                          out_shape=_sds((S, 512), BF16), name="attn_fwd", compiler_params=_PAR)(qf, kf, ve)


def _attn_bwd(qf, kf, ve, do, tq=256):
    S = qf.shape[0]

    def body(q_ref, k_ref, v_ref, do_ref, dq_ref, dk_ref, dv_ref):
        dk_ref[...] = jnp.zeros_like(dk_ref)
        dv_ref[...] = jnp.zeros_like(dv_ref)
        for i in range(S // tq):
            n, rows = (i + 1) * tq, slice(i * tq, (i + 1) * tq)
            dob = do_ref[rows, :].astype(BF16)
            for hh in range(2):
                sl = slice(hh * 128, (hh + 1) * 128)
                q, k = q_ref[rows, sl], k_ref[0:n, sl]
                p = _att_probs(q, k, i, tq)
                dv_ref[0:n, sl] += _dot(p.astype(BF16), dob, _TN)
                dp = _dot(dob, v_ref[0:n, sl], _NT)
                ds = (p * (dp - jnp.sum(dp * p, -1, keepdims=True)) * _ATT_SCALE).astype(BF16)
                dq_ref[rows, sl] = _dot(ds, k, _NN)
                dk_ref[0:n, sl] += _dot(ds, q, _TN)

    ps = pl.BlockSpec((S, 256), lambda h: (0, h))
    return pl.pallas_call(body, grid=(N_HEADS // 2,), in_specs=[ps, ps, ps, pl.BlockSpec((S, 128), lambda h: (0, h))], out_specs=(ps, ps, ps),
                          out_shape=(_sds((S, D)),) * 3, name="attn_bwd", compiler_params=_PAR)(qf, kf, ve, do)


def _softplus(x):
    return jnp.maximum(x, 0.0) + jnp.log1p(jnp.exp(-jnp.abs(x)))


def _ssd_chunk(xc_ref, dtr_ref, dtb_ref, al_ref, e_ref):
    L = SSD_L
    a = -jnp.exp(al_ref[...])
    dtp = _softplus(dtr_ref[...] + dtb_ref[...])
    causal = _iota((L, L), 1) <= _iota((L, L), 0)
    cs = _dot_hi(causal.astype(F32), dtp * a)
    E = e_ref[...]
    dtx, csx = _dot_hi(dtp, E), _dot_hi(cs, E)
    X = xc_ref[:, 0:SSD_INNER]
    Xd = X * dtx
    dec_out = jnp.exp(csx)
    dec_st = jnp.exp(csx[L - 1:L, :] - csx)
    return a, dtp, causal, cs, cs.T, dtx, X, Xd, dec_out, dec_st


def _ssd_decay(causal, cs, cs_row, h):
    diff = cs[:, h:h + 1] - cs_row[h:h + 1, :]
    return jnp.where(causal, jnp.exp(jnp.where(causal, diff, 0.0)), 0.0)


def _ssd_groups(xc_ref, g):
    b0, c0 = SSD_INNER + g * SSD_N, SSD_INNER + 2 * SSD_N + g * SSD_N
    return xc_ref[:, b0:b0 + SSD_N].astype(BF16), xc_ref[:, c0:c0 + SSD_N].astype(BF16)


def _pair_decay(cs, pair):
    L = SSD_L
    return jnp.where(_iota((128, 128), 0) < 64, jnp.exp(cs[L - 1:L, 2 * pair:2 * pair + 1]), jnp.exp(cs[L - 1:L, 2 * pair + 1:2 * pair + 2]))


def _ssd_in_specs(nc, rev):
    idx = (lambda c: nc - 1 - c) if rev else (lambda c: c)
    return [pl.BlockSpec((SSD_L, SSD_CONV_DIM), lambda c: (idx(c), 0)), pl.BlockSpec((SSD_L, 128), lambda c: (idx(c), ODT // 128)),
            _vec_spec(128), _vec_spec(128), _vec_spec(SSD_INNER), pl.BlockSpec((128, SSD_INNER), lambda c: (0, 0))]


def _ssd_core(xc, proj, dtb, alog, dskip, E):
    S = xc.shape[0]
    nc = S // SSD_L

    def body(xc_ref, dtr_ref, dtb_ref, al_ref, dx_ref, e_ref, y_ref, hp_ref, h_scr):
        @pl.when(pl.program_id(0) == 0)
        def _():
            h_scr[...] = jnp.zeros_like(h_scr)

        hp_ref[0] = h_scr[...]
        _, _, causal, cs, cs_row, _, X, Xd, dec_out, dec_st = _ssd_chunk(xc_ref, dtr_ref, dtb_ref, al_ref, e_ref)
        Xs = Xd * dec_st
        lane = _iota((SSD_L, 128), 1)
        for g in range(2):
            Bg, Cg = _ssd_groups(xc_ref, g)
            CB = _dot(Cg, Bg, _NT)
            for pr in range(4):
                pair = g * 4 + pr
                psl = slice(pair * 128, (pair + 1) * 128)
                Xdp = Xd[:, psl].astype(BF16)
                r0 = _dot((CB * _ssd_decay(causal, cs, cs_row, 2 * pair)).astype(BF16), Xdp, _NN)
                r1 = _dot((CB * _ssd_decay(causal, cs, cs_row, 2 * pair + 1)).astype(BF16), Xdp, _NN)
                Hp = h_scr[psl, :]
                W = _dot(Cg, Hp.astype(BF16), _NT)
                y_ref[:, psl] = jnp.where(lane < 64, r0, r1) + W * dec_out[:, psl] + X[:, psl] * dx_ref[:, psl]
                h_scr[psl, :] = Hp * _pair_decay(cs, pair) + _dot(Xs[:, psl].astype(BF16), Bg, _TN)

    return pl.pallas_call(
        body, grid=(nc,), in_specs=_ssd_in_specs(nc, False),
        out_specs=(pl.BlockSpec((SSD_L, SSD_INNER), lambda c: (c, 0)), pl.BlockSpec((1, SSD_INNER, SSD_N), lambda c: (c, 0, 0))),
        out_shape=(_sds((S, SSD_INNER)), _sds((nc, SSD_INNER, SSD_N))), scratch_shapes=[pltpu.VMEM((SSD_INNER, SSD_N), F32)],
        name="ssd_core", compiler_params=_ARB)(xc, proj, dtb, alog, dskip, E)


def _ssd_core_bwd(xc, proj, hprev, dy, dtb, alog, dskip, E, ET):
    S = xc.shape[0]
    nc = S // SSD_L
    L = SSD_L

    def body(xc_ref, dtr_ref, dtb_ref, al_ref, dx_ref, e_ref, et_ref, hp_ref, dy_ref, dxc_ref, ddt_ref, st_ref, dh_scr, acc_scr):
        step = pl.program_id(0)

        @pl.when(step == 0)
        def _():
            dh_scr[...] = jnp.zeros_like(dh_scr)
            acc_scr[...] = jnp.zeros_like(acc_scr)
            st_ref[...] = jnp.zeros_like(st_ref)

        a, dtp, causal, cs, cs_row, dtx, X, Xd, dec_out, dec_st = _ssd_chunk(xc_ref, dtr_ref, dtb_ref, al_ref, e_ref)
        lane = _iota((L, 128), 1)
        sub = _iota((128, L), 0)
        dcs_col = jnp.zeros((L, 128), F32)
        dcs_row = jnp.zeros((128, L), F32)
        dcs_last = jnp.zeros((1, 128), F32)
        dcsx, ddtx, dlastx = [], [], []
        for g in range(2):
            Bg, Cg = _ssd_groups(xc_ref, g)
            CB = _dot(Cg, Bg, _NT)
            dCB = jnp.zeros((L, L), F32)
            dB = jnp.zeros((L, SSD_N), F32)
            dC = jnp.zeros((L, SSD_N), F32)
            for pr in range(4):
                pair = g * 4 + pr
                psl = slice(pair * 128, (pair + 1) * 128)
                dY, Xp, Xdp, dop, dsp = dy_ref[:, psl], X[:, psl], Xd[:, psl], dec_out[:, psl], dec_st[:, psl]
                Xdb = Xdp.astype(BF16)
                acc_scr[0:1, psl] += jnp.sum(dY * Xp, 0, keepdims=True)
                Hp = hp_ref[0, psl, :]
                Hb = Hp.astype(BF16)
                dW = (dY * dop).astype(BF16)
                dcx = dY * _dot(Cg, Hb, _NT) * dop
                dC += _dot(dW, Hb, _NN)
                dHp = _dot(dW, Cg, _TN)
                dHn = dh_scr[psl, :]
                cd = _pair_decay(cs, pair)
                dh_scr[psl, :] = dHp + dHn * cd
                rsum = jnp.sum(dHn * Hp * cd, -1, keepdims=True)
                half = _iota((128, 1), 0) < 64
                s0 = jnp.sum(jnp.where(half, rsum, 0.0), 0, keepdims=True)
                s1 = jnp.sum(jnp.where(half, 0.0, rsum), 0, keepdims=True)
                lane1 = _iota((1, 128), 1)
                dcs_last += jnp.where(lane1 == 2 * pair, s0, 0.0) + jnp.where(lane1 == 2 * pair + 1, s1, 0.0)
                dHb = dHn.astype(BF16)
                dXs = _dot(Bg, dHb, _NT)
                dB += _dot((Xdp * dsp).astype(BF16), dHb, _NN)
                dXd = dXs * dsp
                e_st = dXs * Xdp * dsp
                dcx -= e_st
                dlastx.append(jnp.sum(e_st, 0, keepdims=True))
                for i in range(2):
                    h = 2 * pair + i
                    Dm = _ssd_decay(causal, cs, cs_row, h)
                    M = CB * Dm
                    dYm = jnp.where((lane < 64) if i == 0 else (lane >= 64), dY, 0.0).astype(BF16)
                    dM = _dot(dYm, Xdb, _NT)
                    dXd += _dot(M.astype(BF16), dYm, _TN)
                    dCB += dM * Dm
                    Em = dM * M
                    dcs_col += jnp.where(lane == h, jnp.sum(Em, -1, keepdims=True), 0.0)
                    dcs_row += jnp.where(sub == h, jnp.sum(Em, 0, keepdims=True), 0.0)
                dxc_ref[:, psl] = dY * dx_ref[:, psl] + dXd * dtx[:, psl]
                ddtx.append(dXd * Xp)
                dcsx.append(dcx)
            dCBb = dCB.astype(BF16)
            b0, c0 = SSD_INNER + g * SSD_N, SSD_INNER + 2 * SSD_N + g * SSD_N
            dxc_ref[:, b0:b0 + SSD_N] = dB + _dot(dCBb, Cg, _TN)
            dxc_ref[:, c0:c0 + SSD_N] = dC + _dot(dCBb, Bg, _NN)
        ET = et_ref[...]
        dcs = dcs_col - dcs_row.T + _dot_hi(jnp.concatenate(dcsx, 1), ET)
        dlast = dcs_last + _dot_hi(jnp.broadcast_to(jnp.concatenate(dlastx, 1), (8, SSD_INNER)), ET)[0:1, :]
        dcs += jnp.where(_iota((L, 128), 0) == L - 1, dlast, 0.0)
        dda = _dot_hi((_iota((L, L), 1) >= _iota((L, L), 0)).astype(F32), dcs)
        ddtp = dda * a + _dot_hi(jnp.concatenate(ddtx, 1), ET)
        draw = ddtp * _sigmoid(dtr_ref[...] + dtb_ref[...])
        ddt_ref[...] = draw.astype(BF16)
        st_ref[0:1, :] += jnp.sum(draw, 0, keepdims=True)
        st_ref[1:2, :] += jnp.sum(dda * dtp, 0, keepdims=True) * a

        @pl.when(step == nc - 1)
        def _():
            st_ref[2:3, :] = _dot_hi(acc_scr[...], ET)[0:1, :]

    rev = lambda c: (nc - 1 - c, 0)
    return pl.pallas_call(
        body, grid=(nc,),
        in_specs=_ssd_in_specs(nc, True) + [pl.BlockSpec((SSD_INNER, 128), lambda c: (0, 0)),
                                            pl.BlockSpec((1, SSD_INNER, SSD_N), lambda c: (nc - 1 - c, 0, 0)),
                                            pl.BlockSpec((L, SSD_INNER), rev)],
        out_specs=(pl.BlockSpec((L, SSD_CONV_DIM), rev), pl.BlockSpec((L, 128), rev), pl.BlockSpec((8, 128), lambda c: (0, 0))),
        out_shape=(_sds((S, SSD_CONV_DIM)), _sds((S, 128), BF16), _sds((8, 128))),
        scratch_shapes=[pltpu.VMEM((SSD_INNER, SSD_N), F32), pltpu.VMEM((8, SSD_INNER), F32)],
        name="ssd_core_bwd", compiler_params=_ARB)(xc, proj, dtb, alog, dskip, E, ET, hprev, dy)


def _ssd_post(y, proj, nw, tm=256):
    S = y.shape[0]

    def body(y_ref, z_ref, nw_ref, o_ref):
        for g in range(2):
            sl = slice(g * 512, (g + 1) * 512)
            gated = y_ref[:, sl] * _silu(z_ref[:, sl])
            r = lax.rsqrt(jnp.mean(gated * gated, -1, keepdims=True) + EPS)
            o_ref[:, sl] = (gated * r * nw_ref[:, sl]).astype(BF16)

    return pl.pallas_call(
        body, grid=(S // tm,), in_specs=[_row_spec(tm, D), pl.BlockSpec((tm, D), lambda i: (i, OZ // D)), _vec_spec(D)],
        out_specs=_row_spec(tm, D), out_shape=_sds((S, D), BF16), name="ssd_post", compiler_params=_PAR)(y, proj, nw)


def _ssd_post_bwd(doc, y, proj, nw, tm=256):
    S = y.shape[0]

    def body(d_ref, y_ref, z_ref, nw_ref, dy_ref, dz_ref, st_ref):
        @pl.when(pl.program_id(0) == 0)
        def _():
            st_ref[...] = jnp.zeros_like(st_ref)

        for g in range(2):
            sl = slice(g * 512, (g + 1) * 512)
            yv, zv, dv = y_ref[:, sl], z_ref[:, sl], d_ref[:, sl]
            sz = _silu(zv)
            gated = yv * sz
            r = lax.rsqrt(jnp.mean(gated * gated, -1, keepdims=True) + EPS)
            ghat = gated * r
            st_ref[0:1, sl] += jnp.sum(dv * ghat, 0, keepdims=True)
            gg = dv * nw_ref[:, sl]
            dg = r * (gg - ghat * jnp.mean(gg * ghat, -1, keepdims=True))
            dy_ref[:, sl] = dg * sz
            dz_ref[:, sl] = (dg * yv * _dsilu(zv)).astype(BF16)

    zs = pl.BlockSpec((tm, D), lambda i: (i, OZ // D))
    return pl.pallas_call(
        body, grid=(S // tm,), in_specs=[_row_spec(tm, D), _row_spec(tm, D), zs, _vec_spec(D)],
        out_specs=(_row_spec(tm, D), _row_spec(tm, D), _vec_spec(D, 8)), out_shape=(_sds((S, D)), _sds((S, D), BF16), _sds((8, D))),
        name="ssd_post_bwd", compiler_params=_ARB)(doc, y, proj, nw)


def _row(v, n=None):
    v = v.astype(F32).reshape(1, -1)
    return v if n is None else jnp.pad(v, ((0, 0), (0, n - v.shape[1])))


_IN_SEGMENTS = [(0, 384, OQ), (384, 640, OKV), (640, 672, OKV + 320), (672, 1184, OP), (1184, 2208, OZ), (2208, 3744, OX),
                (3744, 3760, ODT), (3760, IN_DIM, OG)]
_IN_ZEROS = [(OKV + 256, OKV + 320), (OKV + 352, OKV + 384), (ODT + 16, ODT + 128)]


def _in_pieces():
    w, out = IN_DIM // 4, []
    for a, b, d in _IN_SEGMENTS:
        while a < b:
            k = a // w
            e = min(b, (k + 1) * w)
            out.append((k, a - k * w, e - k * w, d))
            d, a = d + e - a, e
    return out


def _win_layout(w_in4, tm=256):
    def body(w_ref, o_ref):
        for k, s0, s1, d in _in_pieces():
            o_ref[:, d:d + s1 - s0] = w_ref[k, :, s0:s1]
        for z0, z1 in _IN_ZEROS:
            o_ref[:, z0:z1] = jnp.zeros((tm, z1 - z0), o_ref.dtype)

    return pl.pallas_call(
        body, grid=(D // tm,), in_specs=[pl.BlockSpec((4, tm, IN_DIM // 4), lambda i: (0, i, 0))],
        out_specs=pl.BlockSpec((tm, IN_PAD), lambda i: (i, 0)), out_shape=_sds((D, IN_PAD), w_in4.dtype), name="win_layout",
        compiler_params=_PAR)(w_in4)


def _win_unlayout(dwin, tm=256):
    def body(d_ref, o_ref):
        for k, s0, s1, d in _in_pieces():
            o_ref[k, :, s0:s1] = d_ref[:, d:d + s1 - s0]

    return pl.pallas_call(
        body, grid=(D // tm,), in_specs=[pl.BlockSpec((tm, IN_PAD), lambda i: (i, 0))],
        out_specs=pl.BlockSpec((4, tm, IN_DIM // 4), lambda i: (0, i, 0)), out_shape=_sds((4, D, IN_DIM // 4), dwin.dtype),
        name="win_unlayout", compiler_params=_PAR)(dwin)


def _prep_late(p):
    return dict(wbr=p["w_branch"].astype(BF16), wo=p["w_out"].astype(BF16), wup4=p["ffn_up4"].astype(BF16), wdn=p["ffn_down"].astype(BF16))


def _prep_layer(p, late=None):
    win = _win_layout(p["w_in4"].astype(BF16))
    wqb = jnp.transpose(p["w_q_b4"].astype(BF16).reshape(4, Q_RANK, 2, 96), (1, 0, 2, 3))
    return dict(
        win=win, wqb=jnp.pad(wqb, ((0, 0), (0, 0), (0, 0), (0, 32))).reshape(Q_RANK, D), wkvb4=p["w_kv_b4"].astype(BF16),
        late=late if late is not None else (lambda after: _prep_late(p)),
        nw1=_row(p["norm1_w"]), nw2=_row(p["norm2_w"]), qan=_row(p["q_a_norm"]), kvan=_row(p["kv_a_norm"]),
        wq=_row(p["q_norm"], 128), wk=_row(p["k_norm"], 128), pool_w=p["pool_w"].astype(F32), pool_scale=_row(p["pool_scale"]),
        cw=p["ssd_conv_w"].astype(F32), cb=_row(p["ssd_conv_b"]), dtb=_row(p["ssd_dt_bias"], 128), alog=_row(p["ssd_a_log"], 128),
        dskip=_row(jnp.repeat(p["ssd_d"].astype(F32), SSD_P)), snw=_row(p["ssd_norm_w"]),
        fcw=p["ffn_conv_w"].astype(F32), fcb=_row(p["ffn_conv_b"]))


def _layer_fwd(x, mod8, W, rope, E, tag):
    sh1, sc1, g1, sh2, sc2, g2 = (mod8[i:i + 1] for i in range(6))
    h1 = _ln_mod(x, W["nw1"], sc1, sh1, name=f"ln1_{tag}")
    proj = _mm(h1, W["win"], tn=640, tk=1024, name=f"proj_{tag}")
    qf, kf, ve = _mla_prep(proj, W["wqb"], W["wkvb4"], W["qan"], W["kvan"], W["wq"], W["wk"], rope)
    oa = _attn_fwd(qf, kf, ve)
    ob = _pool_fwd(proj, W["pool_w"], W["pool_scale"])
    xc = _ssd_pre(proj, W["cw"], W["cb"])
    y, hprev = _ssd_core(xc, proj, W["dtb"], W["alog"], W["dskip"], E)
    oc = _ssd_post(y, proj, W["snw"])
    W.update(W["late"](oc))
    merged = _merge_fwd(oa, ob, oc, proj, W["wbr"])
    x1, out1 = _mm(merged, W["wo"], tk=1024, res=x, gate=g1, name=f"wout_{tag}")
    h2 = _ln_mod(x1, W["nw2"], sc2, sh2, name=f"ln2_{tag}")
    up = _up_fwd(h2, W["wup4"], name=f"up_{tag}")
    act = _ffn_act(up, W["fcw"], W["fcb"])
    x2, out2 = _mm(act, W["wdn"], res=x1, gate=g2, name=f"down_{tag}")
    saved = dict(x=x, h1=h1, proj=proj, qf=qf, kf=kf, ve=ve, oa=oa, ob=ob, oc=oc, xc=xc, hprev=hprev, y=y, merged=merged,
                 out1=out1, x1=x1, h2=h2, up=up, act=act, out2=out2)
    return x2, saved


def _layer_bwd(dx2, sv, mod8, W, rope, E, ET, tag, emit=None):
    sc1, g1, sc2, g2 = mod8[1:2], mod8[2:3], mod8[4:5], mod8[5:6]
    proj = sv["proj"]
    dz2, dg2 = _gate_bwd(dx2, sv["out2"], g2, name=f"gate2_bwd_{tag}")
    dact = _mm(dz2, W["wdn"], "nt", tn=1408, tk=1024, name=f"down_dx_{tag}")
    dwdn = _mm(sv["act"], dz2, "tn", tm=1408, name=f"down_dw_{tag}")
    dup2, dfcw, dfcb = _ffn_act_bwd(sv["up"], dact, W["fcw"], W["fcb"])
    dh2 = _up_dx(dup2, W["wup4"], name=f"up_dx_{tag}")
    dwup4 = _up_dw(sv["h2"], dup2, name=f"up_dw_{tag}")
    dx1, st2 = _ln_mod_bwd(sv["x1"], dh2, dx2, W["nw2"], sc2, name=f"ln2_bwd_{tag}")
    dz1, dg1 = _gate_bwd(dx1, sv["out1"], g1, name=f"gate1_bwd_{tag}")
    dmerged = _mm(dz1, W["wo"], "nt", tk=1024, name=f"wout_dx_{tag}")
    dwo = _mm(sv["merged"], dz1, "tn", name=f"wout_dw_{tag}")
    dya, dyb, dyc, dgl, doa, dob, doc = _merge_bwd(dmerged, sv["oa"], sv["ob"], sv["oc"], proj, W["wbr"])
    dwba = _mm(sv["oa"], dya, "tn", name=f"wba_dw_{tag}")
    dwbb = _mm(sv["ob"], dyb, "tn", name=f"wbb_dw_{tag}")
    dwbc = _mm(sv["oc"], dyc, "tn", name=f"wbc_dw_{tag}")
    late = dict(w_branch=jnp.concatenate([dwba, dwbb, dwbc], 0).reshape(4, 512, D), w_out=dwo.reshape(4, 256, D), ffn_up=dwup4,
                ffn_down=dwdn.reshape(4, FFN // 4, D))
    if emit is not None:
        emit(late)
    dy, dzs, st_post = _ssd_post_bwd(doc, sv["y"], proj, W["snw"])
    dxc, ddt, st_ssd = _ssd_core_bwd(sv["xc"], proj, sv["hprev"], dy, W["dtb"], W["alog"], W["dskip"], E, ET)
    dxbc, dcw, dcb = _ssd_pre_bwd(proj, dxc, W["cw"], W["cb"])
    dpool, dpw, dps = _pool_bwd(proj, dob, W["pool_w"], W["pool_scale"])
    dqf, dkf, dve = _attn_bwd(sv["qf"], sv["kf"], sv["ve"], doa)
    dqs, dkvs, dwqb, dwkvb4, st_mla = _mla_prep_bwd(proj, dqf, dkf, dve, W["wqb"], W["wkvb4"], W["qan"], W["kvan"], W["wq"], W["wk"], rope)
    dproj = jnp.concatenate([dgl, dzs, dxbc, dpool, dqs, dkvs, ddt], 1)
    dh1 = _mm(dproj, W["win"], "nt", tk=IN_PAD, name=f"proj_dx_{tag}")
    dwin = _mm(sv["h1"], dproj, "tn", tn=640, name=f"proj_dw_{tag}")
    dx, st1 = _ln_mod_bwd(sv["x"], dh1, dx1, W["nw1"], sc1, name=f"ln1_bwd_{tag}")
    grads = dict(
        norm1_w=st1[2], norm2_w=st2[2], w_in=_win_unlayout(dwin),
        q_a_norm=st_mla[0, :Q_RANK], kv_a_norm=st_mla[1, :KV_RANK], q_norm=st_mla[2, :96], k_norm=st_mla[3, :96],
        w_q_b=jnp.transpose(dwqb.reshape(Q_RANK, 4, 2, 128)[:, :, :, :96], (1, 0, 2, 3)).reshape(4, Q_RANK, 192), w_kv_b=dwkvb4,
        pool_w=dpw, pool_scale=dps[0], ssd_conv_w=dcw, ssd_conv_b=dcb[0],
        ssd_dt_bias=st_ssd[0, :SSD_HEADS], ssd_a_log=st_ssd[1, :SSD_HEADS], ssd_d=st_ssd[2, :SSD_HEADS], ssd_norm_w=st_post[0],
        ffn_conv_w=jnp.transpose(dfcw, (1, 0, 2)).reshape(3, 2 * FFN), ffn_conv_b=dfcb.reshape(2 * FFN), **late)
    dmod = jnp.concatenate([st1[0:2], dg1[0:1], st2[0:2], dg2[0:1]], 0)
    return dx, grads, dmod


def _ssd_expand():
    E = (jnp.arange(SSD_INNER)[None, :] // SSD_P == jnp.arange(128)[:, None]).astype(F32)
    return E, E.T


def _rope_tables(positions):
    inv_freq = ROPE_THETA ** (-jnp.arange(0, ROPE_DIM, 2, dtype=F32) / ROPE_DIM)
    invf = jnp.concatenate([jnp.zeros((NOPE,), F32), inv_freq, inv_freq, jnp.zeros((32,), F32)]).reshape(1, 128)
    posb = jnp.broadcast_to(positions.astype(F32)[:, None], (positions.shape[0], 128))
    return _rope_tab(posb, invf)


def _local_step(x, target, positions, mods, get_layer, bwd_mod=None, emit=None, done=None):
    rope = _rope_tables(positions)
    E, ET = _ssd_expand()
    Ws, saved, h = [], [], x
    for l in range(2):
        Ws.append(_prep_layer(*get_layer(l, h)))
        h, sv = _layer_fwd(h, mods[l], Ws[l], rope, E, l)
        saved.append(sv)
    dy, lpart = _loss_grad(h, target)
    grads, dmods = [None, None], [None, None]
    for l in (1, 0):
        mod8 = mods[l] if bwd_mod is None else bwd_mod(l)
        dy, grads[l], dmods[l] = _layer_bwd(dy, saved[l], mod8, Ws[l], rope, E, ET, l, None if emit is None else functools.partial(emit, l))
        if done is not None:
            done(l, grads[l])
    return lpart[0, 0], dy, grads, dmods


_ANY = pl.BlockSpec(memory_space=pl.ANY)
_VMEM = pl.BlockSpec(memory_space=pltpu.VMEM)


def _place():
    x, y, c = lax.axis_index("x"), lax.axis_index("y"), lax.axis_index("c")
    return x, y, c, [(1 - x, y), (x, 1 - y), (1 - x, 1 - y)]


def _allgather8(v, name):
    m_per, n = v.shape

    def body(x_ref, out_ref, send_sems, recv_sems, local_sem):
        x, y, c, chips = _place()
        me, sibling = (x, y, c), (x, y, 1 - c)

        def rows(px, py, pc):
            return out_ref.at[pl.ds((4 * px + 2 * py + pc) * m_per, m_per), :]

        def copy(k, block, to, src=None):
            return pltpu.make_async_remote_copy(src_ref=rows(*block) if src is None else src, dst_ref=rows(*block),
                                                send_sem=send_sems.at[k], recv_sem=recv_sems.at[k], device_id=to, device_id_type=MESH)

        mine = pltpu.make_async_copy(x_ref, rows(*me), local_sem)
        mine.start()
        first = [copy(0, me, sibling, src=x_ref)] + [copy(1 + j, me, (*chip, c), src=x_ref) for j, chip in enumerate(chips)]
        for cp in first:
            cp.start()
        passed = [copy(4 + j, (*chip, c), sibling) for j, chip in enumerate(chips)]
        for j, chip in enumerate(chips):
            copy(1 + j, (*chip, c), me).wait_recv()
            passed[j].start()
        copy(0, sibling, me).wait_recv()
        for j, chip in enumerate(chips):
            copy(4 + j, (*chip, 1 - c), me).wait_recv()
        for cp in first + passed:
            cp.wait_send()
        mine.wait()

    return pl.pallas_call(
        body, out_shape=_sds((8 * m_per, n), v.dtype), in_specs=[_VMEM], out_specs=_VMEM,
        scratch_shapes=[pltpu.SemaphoreType.DMA((7,)), pltpu.SemaphoreType.DMA((7,)), pltpu.SemaphoreType.DMA], name=name)(v)


def _sems(n):
    return [pltpu.SemaphoreType.DMA((n,)), pltpu.SemaphoreType.DMA((n,))]


_HBM = pl.BlockSpec(memory_space=pltpu.HBM)
_SEM = pl.BlockSpec(memory_space=pltpu.SEMAPHORE)
_EFFECT = pltpu.CompilerParams(has_side_effects=pltpu.SideEffectType.DATAFLOW_SIDE_EFFECTING)


def _ici_copy(src_refs, land_refs, send_sems, recv_sems, a, j, slices, incoming):
    x, y, c, chips = _place()
    me, other = 2 * x + y, 2 * chips[j][0] + chips[j][1]
    src, dst = slices(src_refs[a], land_refs[a], other, me, c) if incoming else slices(src_refs[a], land_refs[a], me, other, c)
    return pltpu.make_async_remote_copy(src_ref=src, dst_ref=dst, send_sem=send_sems.at[3 * a + j], recv_sem=recv_sems.at[3 * a + j],
                                        device_id=(*chips[j], c), device_id_type=MESH)


def _ici_start(srcs, land_shapes, slices, after, name):
    na = len(srcs)

    def body(*refs):
        src_refs, land_refs, send_sems, recv_sems = refs[:na], refs[na:2 * na], refs[2 * na + 1], refs[2 * na + 2]
        for a in range(na):
            for j in range(3):
                _ici_copy(src_refs, land_refs, send_sems, recv_sems, a, j, slices, False).start()
        refs[-1][...] = jnp.zeros_like(refs[-1])

    hbm = lambda v: pltpu.with_memory_space_constraint(v, pltpu.HBM)
    lands = [hbm(lax.empty(s.shape, s.dtype)) for s in land_shapes]
    return pl.pallas_call(
        body, name=name,
        out_shape=(pltpu.SemaphoreType.DMA((3 * na,)), pltpu.SemaphoreType.DMA((3 * na,)), *[pltpu.HBM(v.shape, v.dtype) for v in srcs],
                   *[pltpu.HBM(s.shape, s.dtype) for s in land_shapes], _sds((8, 128))),
        in_specs=[_HBM] * (2 * na) + [_ANY], out_specs=(_SEM, _SEM, *[_HBM] * (2 * na), _VMEM),
        input_output_aliases={i: 2 + i for i in range(2 * na)}, compiler_params=_EFFECT)(*[hbm(v) for v in srcs], *lands, after)


def _ici_wait(handle, slices, after, name):
    na = (len(handle) - 3) // 2

    def body(*refs):
        src_refs, land_refs, send_sems, recv_sems = refs[:na], refs[na:2 * na], refs[2 * na], refs[2 * na + 1]
        for a in range(na):
            for j in range(3):
                _ici_copy(src_refs, land_refs, send_sems, recv_sems, a, j, slices, False).wait_send()
                _ici_copy(src_refs, land_refs, send_sems, recv_sems, a, j, slices, True).wait_recv()

    thru = handle[2:2 + 2 * na]
    outs = pl.pallas_call(
        body, name=name, out_shape=[pltpu.HBM(v.shape, v.dtype) for v in thru], in_specs=[_HBM] * (2 * na) + [_SEM, _SEM, _ANY],
        out_specs=[_HBM] * (2 * na), input_output_aliases={i: i for i in range(2 * na)}, compiler_params=_EFFECT)(
        *thru, handle[0], handle[1], after)
    return outs[:na], outs[na:]


def _gather_slices(p_ref, land_ref, sender, receiver, c):
    r2 = p_ref.shape[0] // 2
    return p_ref.at[pl.ds(c * r2, r2), :], land_ref.at[sender, pl.ds(c * r2, r2), :]


def _scatter_slices(a_ref, t_ref, sender, receiver, c):
    return a_ref.at[receiver], t_ref.at[sender]


def _gather_start(arrs, after, name):
    return _ici_start(arrs, [_sds((4,) + v.shape, v.dtype) for v in arrs], _gather_slices, after, name)


def _gather_finish(handle, after, name):
    arrs, stacks = _ici_wait(handle, _gather_slices, after, name + "_wait")
    na = len(stacks)

    def body(*refs):
        s_refs, o_refs, (send_sems, recv_sems) = refs[:na], refs[na:2 * na], refs[2 * na:]
        x, y, c, chips = _place()

        def copy(a, j, cc, to):
            r2 = s_refs[a].shape[1] // 2
            at = (2 * chips[j][0] + chips[j][1], pl.ds(cc * r2, r2), slice(None))
            return pltpu.make_async_remote_copy(src_ref=s_refs[a].at[at], dst_ref=o_refs[a].at[at], send_sem=send_sems.at[3 * a + j],
                                                recv_sem=recv_sems.at[3 * a + j], device_id=to, device_id_type=MESH)

        passed = [copy(a, j, c, (x, y, 1 - c)) for a in range(na) for j in range(3)]
        for cp in passed:
            cp.start()
        for a in range(na):
            for j in range(3):
                copy(a, j, 1 - c, (x, y, c)).wait_recv()
        for cp in passed:
            cp.wait_send()

    stacks = pl.pallas_call(
        body, out_shape=[_sds(v.shape, v.dtype) for v in stacks], in_specs=[_ANY] * na, out_specs=[_ANY] * na,
        input_output_aliases={i: i for i in range(na)}, scratch_shapes=_sems(3 * na), name=name + "_pass")(*stacks)
    chip = 2 * lax.axis_index("x") + lax.axis_index("y")
    return [lax.dynamic_update_slice(s, v[None], (chip, 0, 0)) for s, v in zip(stacks, arrs)]


def _send_halves(gs, name):
    na = len(gs)

    def body(*refs):
        g_refs, o_refs, (send_sems, recv_sems) = refs[:na], refs[na:2 * na], refs[2 * na:]
        x, y, c, _ = _place()
        cps = []
        for a in range(na):
            r2 = g_refs[a].shape[1] // 2
            cps.append(pltpu.make_async_remote_copy(src_ref=g_refs[a].at[:, pl.ds((1 - c) * r2, r2), :], dst_ref=o_refs[a],
                                                    send_sem=send_sems.at[a], recv_sem=recv_sems.at[a],
                                                    device_id=(x, y, 1 - c), device_id_type=MESH))
        for cp in cps:
            cp.start()
        for cp in cps:
            cp.wait()

    return pl.pallas_call(
        body, out_shape=[_sds((4, v.shape[1] // 2, v.shape[2]), v.dtype) for v in gs], in_specs=[_ANY] * na, out_specs=[_ANY] * na,
        scratch_shapes=_sems(na), name=name)(*gs)


def _join_halves(fs, name):
    na = len(fs)

    def body(*refs):
        f_refs, o_refs, (send_sems, recv_sems) = refs[:na], refs[na:2 * na], refs[2 * na:]
        x, y, c, _ = _place()

        def copy(a, cc, to):
            r2 = f_refs[a].shape[0]
            return pltpu.make_async_remote_copy(src_ref=f_refs[a], dst_ref=o_refs[a].at[pl.ds(cc * r2, r2), :], send_sem=send_sems.at[a],
                                                recv_sem=recv_sems.at[a], device_id=to, device_id_type=MESH)

        cps = [copy(a, c, (x, y, 1 - c)) for a in range(na)]
        for cp in cps:
            cp.start()
        for a in range(na):
            copy(a, 1 - c, (x, y, c)).wait_recv()
        for cp in cps:
            cp.wait_send()

    outs = pl.pallas_call(
        body, out_shape=[_sds((2 * v.shape[0], v.shape[1]), v.dtype) for v in fs], in_specs=[_ANY] * na, out_specs=[_ANY] * na,
        scratch_shapes=_sems(na), name=name)(*fs)
    ci = lax.axis_index("c")
    return [lax.dynamic_update_slice(o, f, (ci * f.shape[0], 0)) for o, f in zip(outs, fs)]


def _sum_chips(a, t, chip, name):
    _, r2, n = t.shape
    tm = _row_tile(r2)

    def body(k_ref, a_ref, t1_ref, t2_ref, t3_ref, o_ref):
        o_ref[...] = ((a_ref[...].astype(F32) + t1_ref[...].astype(F32)) + t2_ref[...].astype(F32)) + t3_ref[...].astype(F32)

    def slot(j):
        return pl.BlockSpec((None, tm, n), lambda i, k_ref: (lax.rem(k_ref[0] + j, 4), i, 0))

    return pl.pallas_call(
        body, grid_spec=pltpu.PrefetchScalarGridSpec(num_scalar_prefetch=1, grid=(r2 // tm,), in_specs=[slot(0), slot(1), slot(2), slot(3)],
                                                     out_specs=pl.BlockSpec((tm, n), lambda i, k_ref: (i, 0))),
        out_shape=_sds((r2, n)), name=name, compiler_params=_PAR)(chip.reshape(1).astype(jnp.int32), a, t, t, t)


def _add_cast(g, recv, c, name):
    _, r2, n = recv.shape

    def body(c_ref, a_ref, b_ref, o_ref):
        o_ref[...] = (a_ref[...] + b_ref[...]).astype(BF16)

    spec = pl.BlockSpec((None, r2, n), lambda k, c_ref: (k, 0, 0))
    return pl.pallas_call(
        body, grid_spec=pltpu.PrefetchScalarGridSpec(
            num_scalar_prefetch=1, grid=(4,), in_specs=[pl.BlockSpec((None, r2, n), lambda k, c_ref: (k, c_ref[0], 0)), spec], out_specs=spec),
        out_shape=_sds(recv.shape, BF16), name=name, compiler_params=_PAR)(c.reshape(1).astype(jnp.int32), g, recv)


def _sum_lead(t, name, tm=256):
    P, R, n = t.shape
    tm = _row_tile(R, tm)

    def body(t_ref, o_ref):
        acc = t_ref[0].astype(F32)
        for j in range(1, P):
            acc = acc + t_ref[j].astype(F32)
        o_ref[...] = acc

    return pl.pallas_call(body, grid=(R // tm,), in_specs=[pl.BlockSpec((P, tm, n), lambda i: (0, i, 0))],
                          out_specs=pl.BlockSpec((tm, n), lambda i: (i, 0)), out_shape=_sds((R, n)), name=name, compiler_params=_PAR)(t)


def _ada_fwd(c16, ada_w, ada_b_cols, tn=512):
    L, _, n = ada_w.shape

    def body(c_ref, w_ref, b_ref, o_ref):
        o_ref[0] = _dot(_silu(c_ref[...]).astype(BF16), w_ref[0].astype(BF16), _NN) + b_ref[0]

    return pl.pallas_call(
        body, grid=(L, n // tn),
        in_specs=[pl.BlockSpec((16, D), lambda l, j: (0, 0)), pl.BlockSpec((1, D, tn), lambda l, j: (l, 0, j)), pl.BlockSpec((1, 1, tn), lambda l, j: (l, 0, j))],
        out_specs=pl.BlockSpec((1, 16, tn), lambda l, j: (l, 0, j)), out_shape=_sds((L, 16, n)), name="ada_fwd",
        compiler_params=pltpu.CompilerParams(dimension_semantics=("parallel", "parallel")))(c16, ada_w, ada_b_cols)


def _ada_bwd(c16, dmod, tn=512):
    L, _, n = dmod.shape

    def body(c_ref, d_ref, o_ref):
        o_ref[0] = _dot(_silu(c_ref[...]).astype(BF16), d_ref[0].astype(BF16), _TN)

    return pl.pallas_call(
        body, grid=(L, n // tn), in_specs=[pl.BlockSpec((16, D), lambda l, j: (0, 0)), pl.BlockSpec((1, 16, tn), lambda l, j: (l, 0, j))],
        out_specs=pl.BlockSpec((1, D, tn), lambda l, j: (l, 0, j)), out_shape=_sds((L, D, n)), name="ada_bwd",
        compiler_params=pltpu.CompilerParams(dimension_semantics=("parallel", "parallel")))(c16, dmod)


def _adam_math(w, g, m, v):
    mn = ADAM_B1 * m + (1.0 - ADAM_B1) * g
    vn = ADAM_B2 * v + (1.0 - ADAM_B2) * (g * g)
    m_hat = mn / (1.0 - ADAM_B1 ** ADAM_STEP)
    v_hat = vn / (1.0 - ADAM_B2 ** ADAM_STEP)
    return -ADAM_LR * (m_hat / (jnp.sqrt(v_hat) + ADAM_EPS) + ADAM_WD * w), mn, vn


def _adamw(w, g, m, v, name):
    R, n = w.shape
    tm = _row_tile(R)

    def body(w_ref, g_ref, m_ref, v_ref, d_ref, nm_ref, nv_ref):
        d_ref[...], nm_ref[...], nv_ref[...] = _adam_math(w_ref[...], g_ref[...], m_ref[...], v_ref[...])

    spec = pl.BlockSpec((tm, n), lambda i: (i, 0))
    return pl.pallas_call(body, grid=(R // tm,), in_specs=[spec] * 4, out_specs=(spec,) * 3, out_shape=(_sds((R, n)),) * 3,
                          name=name, compiler_params=_PAR)(w, g, m, v)


def _adamw_layers(w, g0, g1, m, v, name):
    _, r, n = w.shape
    tm = _row_tile(r)
    nb = r // tm

    def body(w_ref, g0_ref, g1_ref, m_ref, v_ref, g_ref, d_ref, nm_ref, nv_ref):
        gv = jnp.where(pl.program_id(0) == 0, g0_ref[...], g1_ref[...])
        g_ref[...] = gv
        d_ref[...], nm_ref[...], nv_ref[...] = _adam_math(w_ref[...], gv, m_ref[...], v_ref[...])

    spec = pl.BlockSpec((None, tm, n), lambda l, i: (l, i, 0))
    g0_spec = pl.BlockSpec((tm, n), lambda l, i: (i * (1 - l) + (nb - 1) * l, 0))
    g1_spec = pl.BlockSpec((tm, n), lambda l, i: (i * l, 0))
    return pl.pallas_call(body, grid=(2, nb), in_specs=[spec, g0_spec, g1_spec, spec, spec], out_specs=(spec,) * 4,
                          out_shape=(_sds(w.shape),) * 4, name=name,
                          compiler_params=pltpu.CompilerParams(dimension_semantics=("arbitrary", "arbitrary")))(w, g0, g1, m, v)


_W_NAMES = ["ada_w", "ada_b", "norm1_w", "w_in", "q_a_norm", "w_q_b", "kv_a_norm", "w_kv_b", "q_norm", "k_norm", "pool_w",
            "pool_scale", "ssd_conv_w", "ssd_conv_b", "ssd_dt_bias", "ssd_a_log", "ssd_d", "ssd_norm_w", "w_branch", "w_out",
            "norm2_w", "ffn_up", "ffn_conv_w", "ffn_conv_b", "ffn_down"]
_BIG = [("w_in", (D, IN_DIM // 4), 1), ("w_q_b", (Q_RANK, 192), 1), ("w_kv_b", (KV_RANK, 256), 1), ("w_branch", (512, D), 0),
        ("w_out", (256, D), 0), ("ffn_up", (D, 2 * FFN // 4), 1), ("ffn_down", (FFN // 4, D), 0)]

_SMALL = [("norm1_w", (D,)), ("q_a_norm", (Q_RANK,)), ("kv_a_norm", (KV_RANK,)), ("q_norm", (96,)), ("k_norm", (96,)),
          ("pool_w", (4, 128, 128)), ("pool_scale", (512,)), ("ssd_conv_w", (4, SSD_CONV_DIM)), ("ssd_conv_b", (SSD_CONV_DIM,)),
          ("ssd_dt_bias", (SSD_HEADS,)), ("ssd_a_log", (SSD_HEADS,)), ("ssd_d", (SSD_HEADS,)), ("ssd_norm_w", (D,)), ("norm2_w", (D,)),
          ("ffn_conv_w", (3, 2 * FFN)), ("ffn_conv_b", (2 * FFN,))]
_CONV_SHARDED = {"ssd_conv_w": SSD_CONV_DIM // 4, "ffn_conv_w": 2 * FFN // 4}


def _pack_flat(arrs, mult):
    flat = jnp.concatenate([a.astype(F32).reshape(-1) for a in arrs])
    rows = -(-flat.shape[0] // (128 * mult)) * mult
    return jnp.pad(flat, (0, rows * 128 - flat.shape[0])).reshape(rows, 128), [a.shape for a in arrs]


def _unpack_flat(packed, shapes):
    flat, out, off = packed.reshape(-1), [], 0
    for s in shapes:
        n = 1
        for d in s:
            n *= d
        out.append(flat[off:off + n].reshape(s))
        off += n
    return out


_EARLY = ["w_in", "w_q_b", "w_kv_b"]
_LATE = ["w_branch", "w_out", "ffn_up", "ffn_down"]


def _early_weights(a, l, stacks, conv_full):
    p = {n: a[n][l] for n in _W_NAMES if n not in ("ada_w", "ada_b")}
    p.update({n: conv_full[n][l] for n in conv_full})
    p.update(w_in4=stacks[0], w_q_b4=stacks[1], w_kv_b4=stacks[2])
    return p


def _late_weights(stacks):
    return _prep_late(dict(w_branch=stacks[0].reshape(2048, D), w_out=stacks[1].reshape(D, D), ffn_up4=stacks[2],
                           ffn_down=stacks[3].reshape(FFN, D)))


def _reduce_start(gs, ci, after, tag):
    recv = _send_halves(gs, f"rs_halves_{tag}")
    chip_sum = [_add_cast(g, r, ci, f"rs_add_{tag}") for g, r in zip(gs, recv)]
    return _ici_start(chip_sum, [_sds(v.shape, v.dtype) for v in chip_sum], _scatter_slices, after, f"rs_scatter_{tag}_start")


def _reduce_finish(started, after, tag):
    chip_sum, got = _ici_wait(started, _scatter_slices, after, f"rs_scatter_{tag}_wait")
    chip = 2 * lax.axis_index("x") + lax.axis_index("y")
    return _join_halves([_sum_chips(s, t, chip, f"rs_sum_{tag}") for s, t in zip(chip_sum, got)], f"rs_join_{tag}")


def kernel(x, c, positions, ada_w, ada_b, norm1_w, w_in, q_a_norm, w_q_b, kv_a_norm, w_kv_b, q_norm, k_norm, pool_w, pool_scale, ssd_conv_w, ssd_conv_b, ssd_dt_bias, ssd_a_log, ssd_d, ssd_norm_w, w_branch, w_out, norm2_w, ffn_up, ffn_conv_w, ffn_conv_b, ffn_down, loss_target, m_ada_w, m_ada_b, m_norm1_w, m_w_in, m_q_a_norm, m_w_q_b, m_kv_a_norm, m_w_kv_b, m_q_norm, m_k_norm, m_pool_w, m_pool_scale, m_ssd_conv_w, m_ssd_conv_b, m_ssd_dt_bias, m_ssd_a_log, m_ssd_d, m_ssd_norm_w, m_w_branch, m_w_out, m_norm2_w, m_ffn_up, m_ffn_conv_w, m_ffn_conv_b, m_ffn_down, v_ada_w, v_ada_b, v_norm1_w, v_w_in, v_q_a_norm, v_w_q_b, v_kv_a_norm, v_w_kv_b, v_q_norm, v_k_norm, v_pool_w, v_pool_scale, v_ssd_conv_w, v_ssd_conv_b, v_ssd_dt_bias, v_ssd_a_log, v_ssd_d, v_ssd_norm_w, v_w_branch, v_w_out, v_norm2_w, v_ffn_up, v_ffn_conv_w, v_ffn_conv_b, v_ffn_down):
    a = dict(locals())
    xi, yi, ci = lax.axis_index("x"), lax.axis_index("y"), lax.axis_index("c")
    chip = 2 * xi + yi
    dev = 2 * chip + ci
    ncol = 6 * D // 4

    c_all = _allgather8(c.reshape(8, 128), "gather_c").reshape(8, D)
    c16 = jnp.pad(c_all, ((0, 8), (0, 0)))
    ada_b_cols = lax.dynamic_slice_in_dim(ada_b, chip * ncol, ncol, axis=1).reshape(2, 1, ncol)
    mod_part = _ada_fwd(c16, ada_w, ada_b_cols)[:, :8]
    small1, shapes1 = _pack_flat([mod_part, ssd_conv_w, ffn_conv_w], 8)
    got1 = _allgather8(small1, "gather_mod").reshape(8, -1, 128)
    per_chip = [_unpack_flat(got1[2 * k], shapes1) for k in range(4)]
    mod_all = jnp.concatenate([per_chip[k][0] for k in range(4)], -1)
    conv_full = {"ssd_conv_w": jnp.concatenate([per_chip[k][1] for k in range(4)], -1),
                 "ffn_conv_w": jnp.concatenate([per_chip[k][2] for k in range(4)], -1)}
    mod_mine = lax.dynamic_index_in_dim(mod_all, dev, axis=1, keepdims=False).reshape(2, 6, D)
    mods = [jnp.pad(mod_mine[l], ((0, 2), (0, 0))) for l in range(2)]

    big = _EARLY + _LATE
    shard = lambda names, l: [a[n][l].astype(BF16) for n in names]
    g0a = _gather_start(shard(_EARLY, 0), mods[0], "gather_0a")
    g0b = _gather_start(shard(_LATE, 0), g0a[-1], "gather_0b")
    g1 = _gather_start(shard(big, 1), g0b[-1], "gather_1")
    mods[0] = mods[0] + g1[-1][0, 0]

    def get_layer(l, after):
        if l == 0:
            return (_early_weights(a, 0, _gather_finish(g0a, mods[0], "gather_0a"), conv_full),
                    lambda aft: _late_weights(_gather_finish(g0b, aft, "gather_0b")))
        stacks = _gather_finish(g1, after, "gather_1")
        return _early_weights(a, 1, stacks[:3], conv_full), lambda aft: _late_weights(stacks[3:])

    scatters = {}

    def bwd_mod(l):
        return mods[l] if l == 1 else mods[0] + scatters["1"][-1][0, 0]

    def emit(l, late):
        if l == 0:
            scatters["0b"] = _reduce_start([late[n] for n in _LATE], ci, scatters["1"][-1], "0b")

    def done(l, grads_l):
        if l == 1:
            scatters["1"] = _reduce_start([grads_l[n] for n in big], ci, grads_l["norm1_w"], "1")
        else:
            scatters["0a"] = _reduce_start([grads_l[n] for n in _EARLY], ci, scatters["0b"][-1], "0a")

    lpart, grad_x, grads, dmods = _local_step(x[0], loss_target[0], positions[0], mods, get_layer, bwd_mod, emit, done)
    loss = lax.psum(lpart, ("x", "y", "c"))
    red1 = _reduce_finish(scatters["1"], grad_x, "1")
    red0b = _reduce_finish(scatters["0b"], red1[0], "0b")

    small2, shapes2 = _pack_flat([jnp.stack(dmods)] + [grads[l][n] for l in range(2) for n, _ in _SMALL], 16)
    got2 = _allgather8(small2, "gather_small").reshape(8, -1, 128)
    tot = _unpack_flat(_sum_lead(got2, "sum_small"), shapes2)
    g = {"ada_b": tot[0].reshape(2, 6 * D)}
    for i, (n, _) in enumerate(_SMALL):
        g[n] = jnp.stack([tot[1 + i], tot[1 + len(_SMALL) + i]])
    for n, w in _CONV_SHARDED.items():
        g[n] = lax.dynamic_slice_in_dim(g[n], chip * w, w, axis=2)
    nd = 2 * 6 * D // 128
    dmod_all = jnp.transpose(got2[:, :nd].reshape(8, 2, 6 * D), (1, 0, 2))
    dmod_cols = lax.dynamic_slice_in_dim(jnp.pad(dmod_all, ((0, 0), (0, 8), (0, 0))), chip * ncol, ncol, axis=2)
    g["ada_w"] = _ada_bwd(c16, dmod_cols)

    delta, new_m, new_v = {}, {}, {}
    red1 = dict(zip(big, red1))
    for n, r0 in zip(_LATE, red0b):
        g[n], delta[n], new_m[n], new_v[n] = _adamw_layers(a[n], r0, red1[n], a["m_" + n], a["v_" + n], f"adamw_{n}")
    red0a = _reduce_finish(scatters["0a"], delta[_LATE[-1]], "0a")
    for n, r0 in zip(_EARLY, red0a):
        g[n], delta[n], new_m[n], new_v[n] = _adamw_layers(a[n], r0, red1[n], a["m_" + n], a["v_" + n], f"adamw_{n}")
    shp = ada_w.shape
    r2 = lambda t: t.reshape(-1, shp[-1])
    delta["ada_w"], new_m["ada_w"], new_v["ada_w"] = (
        o.reshape(shp) for o in _adamw(r2(ada_w), r2(g["ada_w"]), r2(m_ada_w), r2(v_ada_w), "adamw_ada_w"))
    rest = [n for n in _W_NAMES if n not in big and n != "ada_w"]
    packs = [_pack_flat([t[n] if pre is None else t[pre + n] for n in rest], 128)[0]
             for t, pre in ((a, None), (g, None), (a, "m_"), (a, "v_"))]
    rest_shapes = [a[n].shape for n in rest]
    outs = [_unpack_flat(o, rest_shapes) for o in _adamw(*packs, "adamw_rest")]
    for i, n in enumerate(rest):
        delta[n], new_m[n], new_v[n] = outs[0][i], outs[1][i], outs[2][i]

    return (loss, grad_x[None], *[g[n] for n in _W_NAMES], *[delta[n] for n in _W_NAMES],
            *[new_m[n] for n in _W_NAMES], *[new_v[n] for n in _W_NAMES])
```

```python
import functools

import jax
import jax.numpy as jnp
from jax import lax
from jax.experimental import pallas as pl
from jax.experimental.pallas import tpu as pltpu

F32 = jnp.float32
BF16 = jnp.bfloat16
MESH = pl.DeviceIdType.MESH
HI = lax.Precision.HIGHEST

D = 1024
N_HEADS = 8
NOPE, ROPE_DIM = 64, 32
Q_RANK, KV_RANK = 384, 256
POOL_WINDOWS = (2, 4, 8, 16)
SSD_HEADS, SSD_P, SSD_N, SSD_L = 16, 64, 128, 128
SSD_INNER = 1024
SSD_CONV_DIM = 1536
FFN = 2816
EPS = 1e-6
ROPE_THETA = 10000.0
OG, OZ, OX, OP, OQ, OKV, ODT, IN_PAD = 0, 3072, 4096, 5632, 6144, 6528, 6912, 7040
IN_DIM = 6832
ADAM_LR, ADAM_B1, ADAM_B2, ADAM_EPS, ADAM_WD, ADAM_STEP = 0.001, 0.9, 0.999, 1e-08, 0.01, 10

_ARB = pltpu.CompilerParams(dimension_semantics=("arbitrary",))
_PAR = pltpu.CompilerParams(dimension_semantics=("parallel",))


def _pick(n, pref):
    if n <= pref:
        return n
    best = None
    for t in range(128, pref + 1, 128):
        if n % t == 0:
            best = t
    assert best is not None, (n, pref)
    return best


def _row_tile(r, cap=256):
    best = None
    for t in range(16, min(r, cap) + 1, 16):
        if r % t == 0:
            best = t
    assert best is not None, r
    return best


def _sds(shape, dtype=F32):
    return jax.ShapeDtypeStruct(tuple(shape), dtype)


def _iota(shape, dim):
    return lax.broadcasted_iota(jnp.int32, shape, dim)


def _sigmoid(x):
    return 0.5 * jnp.tanh(0.5 * x) + 0.5


def _silu(x):
    return x * _sigmoid(x)


def _dsilu(x):
    s = _sigmoid(x)
    return s * (1.0 + x * (1.0 - s))


def _dot(a, b, dims):
    return lax.dot_general(a, b, (dims, ((), ())), preferred_element_type=F32)


_NN, _NT, _TN = ((1,), (0,)), ((1,), (1,)), ((0,), (0,))


def _dot_hi(a, b, dims=_NN):
    return lax.dot_general(a, b, (dims, ((), ())), preferred_element_type=F32, precision=HI)


def _shift_down(x, j):
    n = x.shape[0]
    return jnp.where(_iota(x.shape, 0) >= j, pltpu.roll(x, j, 0), 0.0)


def _shift_up(x, j):
    n = x.shape[0]
    return jnp.where(_iota(x.shape, 0) < n - j, pltpu.roll(x, n - j, 0), 0.0)


def _mm(a, b, mode="nn", out_dtype=F32, tm=512, tn=512, tk=4096, res=None, gate=None, name="mm"):
    if mode == "nn":
        (M, K), (K2, N) = a.shape, b.shape
    elif mode == "nt":
        (M, K), (N, K2) = a.shape, b.shape
    else:
        (K, M), (K2, N) = a.shape, b.shape
    assert K == K2, (a.shape, b.shape, mode)
    tm, tn, tk = _pick(M, tm), _pick(N, tn), _pick(K, tk)
    nk = K // tk
    dims = {"nn": _NN, "nt": _NT, "tn": _TN}[mode]
    fused = res is not None

    def body(*refs):
        a_ref, b_ref = refs[:2]

        def finish(acc):
            if fused:
                r_ref, g_ref, o_ref, raw_ref = refs[2:6]
                raw_ref[...] = acc
                o_ref[...] = r_ref[...] + g_ref[...] * acc
            else:
                refs[2][...] = acc.astype(out_dtype)

        _mm_steps(a_ref, b_ref, dims, nk, refs[-1] if nk > 1 else None, finish)

    if mode == "nn":
        a_spec = pl.BlockSpec((tm, tk), lambda i, j, k: (i, k))
        b_spec = pl.BlockSpec((tk, tn), lambda i, j, k: (k, j))
    elif mode == "nt":
        a_spec = pl.BlockSpec((tm, tk), lambda i, j, k: (i, k))
        b_spec = pl.BlockSpec((tn, tk), lambda i, j, k: (j, k))
    else:
        a_spec = pl.BlockSpec((tk, tm), lambda i, j, k: (k, i))
        b_spec = pl.BlockSpec((tk, tn), lambda i, j, k: (k, j))
    o_spec = pl.BlockSpec((tm, tn), lambda i, j, k: (i, j))
    in_specs, args = [a_spec, b_spec], [a, b]
    out_shape, out_specs = _sds((M, N), out_dtype), o_spec
    if fused:
        in_specs += [o_spec, pl.BlockSpec((1, tn), lambda i, j, k: (0, j))]
        args += [res, gate]
        out_shape, out_specs = (_sds((M, N)), _sds((M, N))), (o_spec, o_spec)
    return pl.pallas_call(
        body, grid=(M // tm, N // tn, nk), in_specs=in_specs, out_specs=out_specs, out_shape=out_shape,
        scratch_shapes=[pltpu.VMEM((tm, tn), F32)] if nk > 1 else [], name=name,
        compiler_params=pltpu.CompilerParams(dimension_semantics=("parallel", "parallel", "arbitrary")),
    )(*args)


def _mm_steps(a_ref, b_ref, dims, nk, acc_ref, finish):
    part = _dot(a_ref[...].astype(BF16), b_ref[...].astype(BF16), dims)
    if nk == 1:
        finish(part)
        return
    k = pl.program_id(2)

    @pl.when(k == 0)
    def _():
        acc_ref[...] = part

    @pl.when(k > 0)
    def _():
        acc_ref[...] += part

    @pl.when(k == nk - 1)
    def _():
        finish(acc_ref[...])


def _mm_blocks(a, b, dims, grid, a_spec, b_spec, o_spec, out_shape, acc_shape, name):
    nk = grid[2]

    def body(a_ref, b_ref, o_ref, *scratch):
        def finish(acc):
            o_ref[...] = acc.astype(o_ref.dtype)

        _mm_steps(a_ref, b_ref, dims, nk, scratch[0] if nk > 1 else None, finish)

    return pl.pallas_call(
        body, grid=grid, in_specs=[a_spec, b_spec], out_specs=o_spec, out_shape=out_shape,
        scratch_shapes=[pltpu.VMEM(acc_shape, F32)] if nk > 1 else [], name=name,
        compiler_params=pltpu.CompilerParams(dimension_semantics=("parallel", "parallel", "arbitrary")),
    )(a, b)


_UP_SHARD = 2 * FFN // 4


def _up_fwd(h2, wup4, name, tm=512):
    S = h2.shape[0]
    tm = min(tm, S)
    return _mm_blocks(h2, wup4, _NN, (S // tm, 4, 1), pl.BlockSpec((tm, D), lambda i, j, k: (i, 0)),
                      pl.BlockSpec((None, D, _UP_SHARD), lambda i, j, k: (j, 0, 0)), pl.BlockSpec((tm, _UP_SHARD), lambda i, j, k: (i, j)),
                      _sds((S, 2 * FFN)), (tm, _UP_SHARD), name)


def _up_dx(dup2, wup4, name, tm=512, tn=512):
    S = dup2.shape[1]
    tm = min(tm, S)
    return _mm_blocks(dup2, wup4, _NT, (S // tm, D // tn, 4), pl.BlockSpec((None, tm, _UP_SHARD), lambda i, j, k: (lax.div(k, 2), i, lax.rem(k, 2))),
                      pl.BlockSpec((None, tn, _UP_SHARD), lambda i, j, k: (k, j, 0)), pl.BlockSpec((tm, tn), lambda i, j, k: (i, j)),
                      _sds((S, D)), (tm, tn), name)


def _up_dw(h2, dup2, name, tm=512, tk=4096):
    S = h2.shape[0]
    tk = min(tk, S)
    return _mm_blocks(h2, dup2, _TN, (D // tm, 4, S // tk), pl.BlockSpec((tk, tm), lambda i, j, k: (k, i)),
                      pl.BlockSpec((None, tk, _UP_SHARD), lambda i, j, k: (lax.div(j, 2), k, lax.rem(j, 2))),
                      pl.BlockSpec((None, tm, _UP_SHARD), lambda i, j, k: (j, i, 0)), _sds((4, D, _UP_SHARD)), (tm, _UP_SHARD), name)


def _row_spec(tm, n):
    return pl.BlockSpec((tm, n), lambda i: (i, 0))


def _vec_spec(n, rows=1):
    return pl.BlockSpec((rows, n), lambda i: (0, 0))


def _ln_mod(x, nw, sc, sh, name, tm=256):
    S = x.shape[0]

    def body(x_ref, nw_ref, sc_ref, sh_ref, o_ref):
        xv = x_ref[...]
        r = lax.rsqrt(jnp.mean(xv * xv, -1, keepdims=True) + EPS)
        o_ref[...] = ((xv * r * nw_ref[...]) * (1.0 + sc_ref[...]) + sh_ref[...]).astype(BF16)

    return pl.pallas_call(
        body, grid=(S // tm,), in_specs=[_row_spec(tm, D)] + [_vec_spec(D)] * 3, out_specs=_row_spec(tm, D),
        out_shape=_sds((S, D), BF16), name=name, compiler_params=_PAR)(x, nw, sc, sh)


def _ln_mod_bwd(x, dh, dres, nw, sc, name, tm=256):
    S = x.shape[0]

    def body(x_ref, dh_ref, dres_ref, nw_ref, sc_ref, dx_ref, st_ref):
        @pl.when(pl.program_id(0) == 0)
        def _():
            st_ref[...] = jnp.zeros_like(st_ref)

        xv, dhv, nwv = x_ref[...], dh_ref[...], nw_ref[...]
        r = lax.rsqrt(jnp.mean(xv * xv, -1, keepdims=True) + EPS)
        xhat = xv * r
        dn = dhv * (1.0 + sc_ref[...])
        g = dn * nwv
        dx_ref[...] = dres_ref[...] + r * (g - xhat * jnp.mean(g * xhat, -1, keepdims=True))
        st_ref[0:1, :] += jnp.sum(dhv, 0, keepdims=True)
        st_ref[1:2, :] += jnp.sum(dhv * (xhat * nwv), 0, keepdims=True)
        st_ref[2:3, :] += jnp.sum(dn * xhat, 0, keepdims=True)

    return pl.pallas_call(
        body, grid=(S // tm,), in_specs=[_row_spec(tm, D)] * 3 + [_vec_spec(D)] * 2,
        out_specs=(_row_spec(tm, D), _vec_spec(D, 8)), out_shape=(_sds((S, D)), _sds((8, D))),
        name=name, compiler_params=_ARB)(x, dh, dres, nw, sc)


def _gate_bwd(dx, out, g, name, tm=256):
    S = dx.shape[0]

    def body(dx_ref, o_ref, g_ref, dz_ref, dg_ref):
        @pl.when(pl.program_id(0) == 0)
        def _():
            dg_ref[...] = jnp.zeros_like(dg_ref)

        dxv = dx_ref[...]
        dz_ref[...] = (dxv * g_ref[...]).astype(BF16)
        dg_ref[0:1, :] += jnp.sum(dxv * o_ref[...], 0, keepdims=True)

    return pl.pallas_call(
        body, grid=(S // tm,), in_specs=[_row_spec(tm, D)] * 2 + [_vec_spec(D)],
        out_specs=(_row_spec(tm, D), _vec_spec(D, 8)), out_shape=(_sds((S, D), BF16), _sds((8, D))),
        name=name, compiler_params=_ARB)(dx, out, g)


def _loss_grad(y, t, tm=256):
    S = y.shape[0]

    def body(y_ref, t_ref, dy_ref, l_ref):
        @pl.when(pl.program_id(0) == 0)
        def _():
            l_ref[...] = jnp.zeros_like(l_ref)

        e = y_ref[...] - t_ref[...]
        dy_ref[...] = e * (1.0 / D)
        l_ref[...] += 0.5 * jnp.sum(jnp.mean(e * e, -1, keepdims=True), 0, keepdims=True)

    return pl.pallas_call(
        body, grid=(S // tm,), in_specs=[_row_spec(tm, D)] * 2,
        out_specs=(_row_spec(tm, D), pl.BlockSpec((8, 128), lambda i: (0, 0))),
        out_shape=(_sds((S, D)), _sds((8, 128))), name="loss_grad", compiler_params=_ARB)(y, t)


def _conv(x, w, b):
    K = w.shape[0]
    acc = x * w[K - 1:K, :] + b
    for j in range(1, K):
        acc = acc + _shift_down(x, j) * w[K - 1 - j:K - j, :]
    return acc


def _conv_bwd(x, w, dc):
    K = w.shape[0]
    dx = dc * w[K - 1:K, :]
    dws = [jnp.sum(dc * x, 0, keepdims=True)]
    for j in range(1, K):
        dx = dx + _shift_up(dc, j) * w[K - 1 - j:K - j, :]
        dws.append(jnp.sum(dc * _shift_down(x, j), 0, keepdims=True))
    return dx, dws[::-1], jnp.sum(dc, 0, keepdims=True)


def _col_spec(S, tc, off=0):
    return pl.BlockSpec((S, tc), lambda j: (0, j + off))


def _ssd_pre(proj, cw, cb, tc=256):
    S, n = proj.shape[0], SSD_CONV_DIM

    def body(x_ref, w_ref, b_ref, o_ref):
        o_ref[...] = _silu(_conv(x_ref[...], w_ref[...], b_ref[...]))

    return pl.pallas_call(
        body, grid=(n // tc,),
        in_specs=[_col_spec(S, tc, OX // tc), pl.BlockSpec((4, tc), lambda j: (0, j)), pl.BlockSpec((1, tc), lambda j: (0, j))],
        out_specs=_col_spec(S, tc), out_shape=_sds((S, n)), name="ssd_pre", compiler_params=_PAR)(proj, cw, cb)


def _ssd_pre_bwd(proj, dxc, cw, cb, tc=256):
    S, n = proj.shape[0], SSD_CONV_DIM

    def body(x_ref, d_ref, w_ref, b_ref, dx_ref, dw_ref, db_ref):
        xv, wv = x_ref[...], w_ref[...]
        dc = d_ref[...] * _dsilu(_conv(xv, wv, b_ref[...]))
        dx, dw, db = _conv_bwd(xv, wv, dc)
        dx_ref[...] = dx.astype(BF16)
        for k, row in enumerate(dw):
            dw_ref[k:k + 1, :] = row
        db_ref[...] = db

    wspec, bspec = pl.BlockSpec((4, tc), lambda j: (0, j)), pl.BlockSpec((1, tc), lambda j: (0, j))
    return pl.pallas_call(
        body, grid=(n // tc,), in_specs=[_col_spec(S, tc, OX // tc), _col_spec(S, tc), wspec, bspec],
        out_specs=(_col_spec(S, tc), wspec, bspec), out_shape=(_sds((S, n), BF16), _sds((4, n)), _sds((1, n))),
        name="ssd_pre_bwd", compiler_params=_PAR)(proj, dxc, cw, cb)


def _ffn_act(up, cw, cb, tc=256):
    S, nb = up.shape[0], FFN // tc

    def body(g_ref, v_ref, wg_ref, wv_ref, bg_ref, bv_ref, o_ref):
        o_ref[...] = (_silu(_conv(g_ref[...], wg_ref[...], bg_ref[...])) * _conv(v_ref[...], wv_ref[...], bv_ref[...])).astype(BF16)

    def wspec(off):
        return pl.BlockSpec((3, tc), lambda j: (0, j + off))

    def bspec(off):
        return pl.BlockSpec((1, tc), lambda j: (0, j + off))

    return pl.pallas_call(
        body, grid=(nb,), in_specs=[_col_spec(S, tc), _col_spec(S, tc, nb), wspec(0), wspec(nb), bspec(0), bspec(nb)],
        out_specs=_col_spec(S, tc), out_shape=_sds((S, FFN), BF16), name="ffn_act", compiler_params=_PAR)(up, up, cw, cw, cb, cb)


def _ffn_act_bwd(up, dact, cw, cb, tc=256):
    S, nb = up.shape[0], FFN // tc

    def body(g_ref, v_ref, d_ref, wg_ref, wv_ref, bg_ref, bv_ref, dx_ref, dw_ref, db_ref):
        gv, vv, wg, wv, da = g_ref[...], v_ref[...], wg_ref[...], wv_ref[...], d_ref[...]
        cg, cv = _conv(gv, wg, bg_ref[...]), _conv(vv, wv, bv_ref[...])
        for half, (dx, dw, db) in enumerate((_conv_bwd(gv, wg, da * cv * _dsilu(cg)), _conv_bwd(vv, wv, da * _silu(cg)))):
            dx_ref[half] = dx.astype(BF16)
            for k in range(3):
                dw_ref[half, k:k + 1, :] = dw[k]
            db_ref[half] = db

    def wspec(off):
        return pl.BlockSpec((3, tc), lambda j: (0, j + off))

    def bspec(off):
        return pl.BlockSpec((1, tc), lambda j: (0, j + off))

    cs = _col_spec(S, tc)
    both = lambda r: pl.BlockSpec((2, r, tc), lambda j: (0, 0, j))
    return pl.pallas_call(
        body, grid=(nb,), in_specs=[cs, _col_spec(S, tc, nb), cs, wspec(0), wspec(nb), bspec(0), bspec(nb)],
        out_specs=(both(S), both(3), both(1)), out_shape=(_sds((2, S, FFN), BF16), _sds((2, 3, FFN)), _sds((2, 1, FFN))),
        name="ffn_act_bwd", compiler_params=_PAR)(up, up, dact, cw, cw, cb, cb)


def _window_sum(x, w, up=False):
    shift = _shift_up if up else _shift_down
    j = 1
    while j < w:
        x = x + shift(x, j)
        j *= 2
    return x


def _pool_fwd(proj, pool_w, pool_scale):
    S = proj.shape[0]

    def body(u_ref, w_ref, s_ref, o_ref):
        cnt_row = (_iota((S, 128), 0) + 1).astype(F32)
        for g, w in enumerate(POOL_WINDOWS):
            sl = slice(g * 128, (g + 1) * 128)
            u = u_ref[:, sl]
            pooled = _window_sum(u, w) / jnp.minimum(cnt_row, float(w)) - u
            mixed = _dot(pooled.astype(BF16), w_ref[g].astype(BF16), _NN)
            o_ref[:, sl] = (mixed * s_ref[:, sl]).astype(BF16)

    return pl.pallas_call(
        body, grid=(1,),
        in_specs=[pl.BlockSpec((S, 512), lambda i: (0, OP // 512)), pl.BlockSpec((4, 128, 128), lambda i: (0, 0, 0)), _vec_spec(512)],
        out_specs=pl.BlockSpec((S, 512), lambda i: (0, 0)), out_shape=_sds((S, 512), BF16), name="pool_fwd",
        compiler_params=_ARB)(proj, pool_w, pool_scale)


def _pool_bwd(proj, dob, pool_w, pool_scale):
    S = proj.shape[0]

    def body(u_ref, d_ref, w_ref, s_ref, du_ref, dw_ref, ds_ref):
        cnt_row = (_iota((S, 128), 0) + 1).astype(F32)
        for g, w in enumerate(POOL_WINDOWS):
            sl = slice(g * 128, (g + 1) * 128)
            u, dv, wv = u_ref[:, sl], d_ref[:, sl], w_ref[g].astype(BF16)
            cnt = jnp.minimum(cnt_row, float(w))
            pooled = (_window_sum(u, w) / cnt - u).astype(BF16)
            ds_ref[:, sl] = jnp.sum(dv * _dot(pooled, wv, _NN), 0, keepdims=True)
            dmix = (dv * s_ref[:, sl]).astype(BF16)
            dw_ref[g] = _dot(pooled, dmix, _TN)
            dp = _dot(dmix, wv, _NT)
            du_ref[:, sl] = (_window_sum(dp / cnt, w, up=True) - dp).astype(BF16)

    blk = pl.BlockSpec((S, 512), lambda i: (0, 0))
    wspec = pl.BlockSpec((4, 128, 128), lambda i: (0, 0, 0))
    return pl.pallas_call(
        body, grid=(1,), in_specs=[pl.BlockSpec((S, 512), lambda i: (0, OP // 512)), blk, wspec, _vec_spec(512)],
        out_specs=(blk, wspec, _vec_spec(512)), out_shape=(_sds((S, 512), BF16), _sds((4, 128, 128)), _sds((1, 512))),
        name="pool_bwd", compiler_params=_ARB)(proj, dob, pool_w, pool_scale)


def _branch_specs():
    return [pl.BlockSpec((512, D), lambda i: (0, 0)), pl.BlockSpec((512, D), lambda i: (1, 0)), pl.BlockSpec((1024, D), lambda i: (1, 0))]


def _merge_fwd(oa, ob, oc, proj, wbr, tm=256):
    S = oa.shape[0]

    def body(oa_ref, ob_ref, oc_ref, gl_ref, wa_ref, wb_ref, wc_ref, o_ref):
        acc = _sigmoid(gl_ref[:, 0:D]) * _dot(oa_ref[...], wa_ref[...], _NN)
        acc += _sigmoid(gl_ref[:, D:2 * D]) * _dot(ob_ref[...], wb_ref[...], _NN)
        acc += _sigmoid(gl_ref[:, 2 * D:3 * D]) * _dot(oc_ref[...], wc_ref[...], _NN)
        o_ref[...] = acc.astype(BF16)

    return pl.pallas_call(
        body, grid=(S // tm,),
        in_specs=[_row_spec(tm, 512), _row_spec(tm, 512), _row_spec(tm, D), _row_spec(tm, 3 * D)] + _branch_specs(),
        out_specs=_row_spec(tm, D), out_shape=_sds((S, D), BF16), name="merge_fwd", compiler_params=_PAR)(oa, ob, oc, proj, wbr, wbr, wbr)


def _merge_bwd(dm, oa, ob, oc, proj, wbr, tm=256):
    S = oa.shape[0]

    def body(dm_ref, oa_ref, ob_ref, oc_ref, gl_ref, wa_ref, wb_ref, wc_ref, dya_ref, dyb_ref, dyc_ref, dgl_ref, doa_ref, dob_ref, doc_ref):
        dmv = dm_ref[...]
        for i, (o_ref, w_ref, dy_ref, do_ref) in enumerate(
                ((oa_ref, wa_ref, dya_ref, doa_ref), (ob_ref, wb_ref, dyb_ref, dob_ref), (oc_ref, wc_ref, dyc_ref, doc_ref))):
            gt = _sigmoid(gl_ref[:, i * D:(i + 1) * D])
            wv = w_ref[...]
            yv = _dot(o_ref[...], wv, _NN)
            dy = (dmv * gt).astype(BF16)
            dy_ref[...] = dy
            dgl_ref[:, i * D:(i + 1) * D] = (dmv * yv * gt * (1.0 - gt)).astype(BF16)
            do_ref[...] = _dot(dy, wv, _NT)

    rs = _row_spec
    return pl.pallas_call(
        body, grid=(S // tm,),
        in_specs=[rs(tm, D), rs(tm, 512), rs(tm, 512), rs(tm, D), rs(tm, 3 * D)] + _branch_specs(),
        out_specs=(rs(tm, D), rs(tm, D), rs(tm, D), rs(tm, 3 * D), rs(tm, 512), rs(tm, 512), rs(tm, D)),
        out_shape=(_sds((S, D), BF16), _sds((S, D), BF16), _sds((S, D), BF16), _sds((S, 3 * D), BF16), _sds((S, 512)), _sds((S, 512)), _sds((S, D))),
        name="merge_bwd", compiler_params=_PAR)(dm, oa, ob, oc, proj, wbr, wbr, wbr)


def _rope_tab(posb, invf, tm=256):
    S = posb.shape[0]

    def body(p_ref, f_ref, c_ref, s1_ref, s2_ref):
        ang = p_ref[...] * f_ref[...]
        lane = _iota(ang.shape, 1)
        cs, sn = jnp.cos(ang), jnp.sin(ang)
        c_ref[...] = jnp.where(lane < NOPE, 1.0, cs)
        s1_ref[...] = jnp.where((lane >= 64) & (lane < 80), -sn, 0.0)
        s2_ref[...] = jnp.where((lane >= 80) & (lane < 96), sn, 0.0)

    rs = _row_spec(tm, 128)
    return pl.pallas_call(body, grid=(S // tm,), in_specs=[rs, _vec_spec(128)], out_specs=(rs, rs, rs),
                          out_shape=(_sds((S, 128)),) * 3, name="rope_tab", compiler_params=_PAR)(posb, invf)


def _rope(u, C, S1, S2):
    return u * C + pltpu.roll(u, 112, 1) * S1 + pltpu.roll(u, 16, 1) * S2


def _rope_t(dy, C, S1, S2):
    return dy * C + pltpu.roll(dy * S1, 16, 1) + pltpu.roll(dy * S2, 112, 1)


def _seg_sum(v, mask):
    return jnp.sum(jnp.where(mask, v, 0.0), -1, keepdims=True)


def _mla_latents(pq_ref, pkv_ref, wqb_ref, wkvb_ref, qan_ref, kvan_ref):
    ql, kvl = pq_ref[...], pkv_ref[...]
    ckv, kr = kvl[:, 0:KV_RANK], kvl[:, KV_RANK:KV_RANK + 128]
    rq = lax.rsqrt(jnp.mean(ql * ql, -1, keepdims=True) + EPS)
    rkv = lax.rsqrt(jnp.mean(ckv * ckv, -1, keepdims=True) + EPS)
    nq = (ql * rq * qan_ref[...]).astype(BF16)
    nkv = (ckv * rkv * kvan_ref[...]).astype(BF16)
    kv = jnp.concatenate([_dot(nkv, wkvb_ref[k], _NN) for k in range(4)], 1)
    return ql, ckv, kr, rq, rkv, nq, nkv, _dot(nq, wqb_ref[...], _NN), kv


def _mla_specs(tm):
    full = lambda r, n: pl.BlockSpec((r, n), lambda i: (0, 0))
    return ([pl.BlockSpec((tm, 384), lambda i: (i, OQ // 384)), pl.BlockSpec((tm, 384), lambda i: (i, OKV // 384))],
            [full(Q_RANK, D), pl.BlockSpec((4, KV_RANK, 256), lambda i: (0, 0, 0)), _vec_spec(Q_RANK), _vec_spec(KV_RANK), _vec_spec(128), _vec_spec(128)]
            + [_row_spec(tm, 128)] * 3)


def _mla_prep(proj, wqb, wkvb, qan, kvan, wq, wk, rope, tm=256):
    S = proj.shape[0]

    def body(pq_ref, pkv_ref, wqb_ref, wkvb_ref, qan_ref, kvan_ref, wq_ref, wk_ref, c_ref, s1_ref, s2_ref, qf_ref, kf_ref, ve_ref):
        _, _, kr, _, _, _, _, q, kv = _mla_latents(pq_ref, pkv_ref, wqb_ref, wkvb_ref, qan_ref, kvan_ref)
        C, S1, S2, wqv, wkv = c_ref[...], s1_ref[...], s2_ref[...], wq_ref[...], wk_ref[...]
        lane = _iota((tm, 128), 1)
        mn, mr = lane < 64, (lane >= 64) & (lane < 96)
        rrk = lax.rsqrt(_seg_sum(kr * kr, mr) / ROPE_DIM + EPS)
        ykr = _rope(jnp.where(mr, kr * rrk * wkv, 0.0), C, S1, S2)
        for h in range(N_HEADS):
            sl = slice(h * 128, (h + 1) * 128)
            t = q[:, sl]
            rn = lax.rsqrt(_seg_sum(t * t, mn) / NOPE + EPS)
            rr = lax.rsqrt(_seg_sum(t * t, mr) / ROPE_DIM + EPS)
            qf_ref[:, sl] = _rope(t * jnp.where(mn, rn, jnp.where(mr, rr, 0.0)) * wqv, C, S1, S2).astype(BF16)
            t = kv[:, sl]
            rn = lax.rsqrt(_seg_sum(t * t, mn) / NOPE + EPS)
            kf_ref[:, sl] = (jnp.where(mn, t * rn * wkv, 0.0) + ykr).astype(BF16)
            ve_ref[:, sl] = (jnp.where(mn, pltpu.roll(t, 64, 1), 0.0) if h % 2 == 0 else jnp.where(mn, 0.0, t)).astype(BF16)

    pspecs, wspecs = _mla_specs(tm)
    rs = _row_spec(tm, D)
    return pl.pallas_call(body, grid=(S // tm,), in_specs=pspecs + wspecs, out_specs=(rs, rs, rs),
                          out_shape=(_sds((S, D), BF16),) * 3, name="mla_prep", compiler_params=_PAR)(
        proj, proj, wqb, wkvb, qan, kvan, wq, wk, *rope)


def _mla_prep_bwd(proj, dqf, dkf, dve, wqb, wkvb, qan, kvan, wq, wk, rope, tm=256):
    S = proj.shape[0]

    def body(pq_ref, pkv_ref, wqb_ref, wkvb_ref, qan_ref, kvan_ref, wq_ref, wk_ref, c_ref, s1_ref, s2_ref,
             dqf_ref, dkf_ref, dve_ref, dqs_ref, dkvs_ref, dwqb_ref, dwkvb_ref, st_ref, dq_scr, dkv_scr):
        @pl.when(pl.program_id(0) == 0)
        def _():
            dwqb_ref[...] = jnp.zeros_like(dwqb_ref)
            dwkvb_ref[...] = jnp.zeros_like(dwkvb_ref)
            st_ref[...] = jnp.zeros_like(st_ref)

        ql, ckv, kr, rq, rkv, nq, nkv, q, kv = _mla_latents(pq_ref, pkv_ref, wqb_ref, wkvb_ref, qan_ref, kvan_ref)
        C, S1, S2, wqv, wkv = c_ref[...], s1_ref[...], s2_ref[...], wq_ref[...], wk_ref[...]
        lane = _iota((tm, 128), 1)
        mn, mr = lane < 64, (lane >= 64) & (lane < 96)
        dwq = jnp.zeros((1, 128), F32)
        dwk = jnp.zeros((1, 128), F32)
        dykr = jnp.zeros((tm, 128), F32)
        for h in range(N_HEADS):
            sl = slice(h * 128, (h + 1) * 128)
            t = q[:, sl]
            rn = lax.rsqrt(_seg_sum(t * t, mn) / NOPE + EPS)
            rr = lax.rsqrt(_seg_sum(t * t, mr) / ROPE_DIM + EPS)
            scale = jnp.where(mn, rn, jnp.where(mr, rr, 0.0))
            that = t * scale
            du = _rope_t(dqf_ref[:, sl], C, S1, S2)
            dwq += jnp.sum(du * that, 0, keepdims=True)
            g = du * wqv
            gt = g * that
            dq_scr[:, sl] = scale * (g - that * jnp.where(mn, _seg_sum(gt, mn) / NOPE, _seg_sum(gt, mr) / ROPE_DIM))
            t = kv[:, sl]
            rn = lax.rsqrt(_seg_sum(t * t, mn) / NOPE + EPS)
            that = jnp.where(mn, t * rn, 0.0)
            dkf = dkf_ref[:, sl]
            dykr += jnp.where(mr, dkf, 0.0)
            dkn = jnp.where(mn, dkf, 0.0)
            dwk += jnp.sum(dkn * that, 0, keepdims=True)
            g = dkn * wkv
            dve = dve_ref[:, sl]
            dkv_scr[:, sl] = jnp.where(mn, rn * (g - that * (jnp.sum(g * that, -1, keepdims=True) / NOPE)),
                                       pltpu.roll(dve, 64, 1) if h % 2 == 0 else dve)
        rrk = lax.rsqrt(_seg_sum(kr * kr, mr) / ROPE_DIM + EPS)
        that = jnp.where(mr, kr * rrk, 0.0)
        dukr = jnp.where(mr, _rope_t(dykr, C, S1, S2), 0.0)
        dwk += jnp.sum(dukr * that, 0, keepdims=True)
        g = dukr * wkv
        dkr = rrk * (g - that * (jnp.sum(g * that, -1, keepdims=True) / ROPE_DIM))
        dqv, dkvv = dq_scr[...].astype(BF16), dkv_scr[...].astype(BF16)
        dnq = _dot(dqv, wqb_ref[...], _NT)
        dwqb_ref[...] += _dot(nq, dqv, _TN)
        dnkv = jnp.zeros((tm, KV_RANK), F32)
        for k in range(4):
            dnkv += _dot(dkvv[:, k * 256:(k + 1) * 256], wkvb_ref[k], _NT)
            dwkvb_ref[k] += _dot(nkv, dkvv[:, k * 256:(k + 1) * 256], _TN)
        xhat = ql * rq
        st_ref[0:1, 0:Q_RANK] += jnp.sum(dnq * xhat, 0, keepdims=True)
        g = dnq * qan_ref[...]
        dqs_ref[...] = (rq * (g - xhat * jnp.mean(g * xhat, -1, keepdims=True))).astype(BF16)
        xhat = ckv * rkv
        st_ref[1:2, 0:KV_RANK] += jnp.sum(dnkv * xhat, 0, keepdims=True)
        g = dnkv * kvan_ref[...]
        dkvs_ref[:, 0:KV_RANK] = (rkv * (g - xhat * jnp.mean(g * xhat, -1, keepdims=True))).astype(BF16)
        dkvs_ref[:, KV_RANK:KV_RANK + 128] = dkr.astype(BF16)
        st_ref[2:3, 0:128] += dwq
        st_ref[3:4, 0:128] += dwk

    pspecs, wspecs = _mla_specs(tm)
    rs = _row_spec(tm, D)
    full = lambda r, n: pl.BlockSpec((r, n), lambda i: (0, 0))
    return pl.pallas_call(
        body, grid=(S // tm,), in_specs=pspecs + wspecs + [rs, rs, rs],
        out_specs=(_row_spec(tm, 384), _row_spec(tm, 384), full(Q_RANK, D), pl.BlockSpec((4, KV_RANK, 256), lambda i: (0, 0, 0)), full(8, D)),
        out_shape=(_sds((S, 384), BF16), _sds((S, 384), BF16), _sds((Q_RANK, D)), _sds((4, KV_RANK, 256)), _sds((8, D))),
        scratch_shapes=[pltpu.VMEM((tm, D), F32), pltpu.VMEM((tm, D), F32)], name="mla_prep_bwd", compiler_params=_ARB)(
        proj, proj, wqb, wkvb, qan, kvan, wq, wk, *rope, dqf, dkf, dve)


_ATT_SCALE = (NOPE + ROPE_DIM) ** -0.5


def _att_probs(q, k, i, tq):
    n = k.shape[0]
    s = _dot(q, k, _NT) * _ATT_SCALE
    s = jnp.where(_iota((tq, n), 1) <= i * tq + _iota((tq, n), 0), s, -1e30)
    p = jnp.exp(s - jnp.max(s, -1, keepdims=True))
    return p / jnp.sum(p, -1, keepdims=True)


def _attn_fwd(qf, kf, ve, tq=256):
    S = qf.shape[0]

    def body(q_ref, k_ref, v_ref, o_ref):
        for i in range(S // tq):
            n, rows = (i + 1) * tq, slice(i * tq, (i + 1) * tq)
            acc = jnp.zeros((tq, 128), F32)
            for hh in range(2):
                sl = slice(hh * 128, (hh + 1) * 128)
                p = _att_probs(q_ref[rows, sl], k_ref[0:n, sl], i, tq)
                acc += _dot(p.astype(BF16), v_ref[0:n, sl], _NN)
            o_ref[rows, :] = acc.astype(BF16)

    ps = pl.BlockSpec((S, 256), lambda h: (0, h))
    return pl.pallas_call(body, grid=(N_HEADS // 2,), in_specs=[ps, ps, ps], out_specs=pl.BlockSpec((S, 128), lambda h: (0, h)),
                          out_shape=_sds((S, 512), BF16), name="attn_fwd", compiler_params=_PAR)(qf, kf, ve)


def _attn_bwd(qf, kf, ve, do, tq=256):
    S = qf.shape[0]

    def body(q_ref, k_ref, v_ref, do_ref, dq_ref, dk_ref, dv_ref):
        dk_ref[...] = jnp.zeros_like(dk_ref)
        dv_ref[...] = jnp.zeros_like(dv_ref)
        for i in range(S // tq):
            n, rows = (i + 1) * tq, slice(i * tq, (i + 1) * tq)
            dob = do_ref[rows, :].astype(BF16)
            for hh in range(2):
                sl = slice(hh * 128, (hh + 1) * 128)
                q, k = q_ref[rows, sl], k_ref[0:n, sl]
                p = _att_probs(q, k, i, tq)
                dv_ref[0:n, sl] += _dot(p.astype(BF16), dob, _TN)
                dp = _dot(dob, v_ref[0:n, sl], _NT)
                ds = (p * (dp - jnp.sum(dp * p, -1, keepdims=True)) * _ATT_SCALE).astype(BF16)
                dq_ref[rows, sl] = _dot(ds, k, _NN)
                dk_ref[0:n, sl] += _dot(ds, q, _TN)

    ps = pl.BlockSpec((S, 256), lambda h: (0, h))
    return pl.pallas_call(body, grid=(N_HEADS // 2,), in_specs=[ps, ps, ps, pl.BlockSpec((S, 128), lambda h: (0, h))], out_specs=(ps, ps, ps),
                          out_shape=(_sds((S, D)),) * 3, name="attn_bwd", compiler_params=_PAR)(qf, kf, ve, do)


def _softplus(x):
    return jnp.maximum(x, 0.0) + jnp.log1p(jnp.exp(-jnp.abs(x)))


def _ssd_chunk(xc_ref, dtr_ref, dtb_ref, al_ref, e_ref):
    L = SSD_L
    a = -jnp.exp(al_ref[...])
    dtp = _softplus(dtr_ref[...] + dtb_ref[...])
    causal = _iota((L, L), 1) <= _iota((L, L), 0)
    cs = _dot_hi(causal.astype(F32), dtp * a)
    E = e_ref[...]
    dtx, csx = _dot_hi(dtp, E), _dot_hi(cs, E)
    X = xc_ref[:, 0:SSD_INNER]
    Xd = X * dtx
    dec_out = jnp.exp(csx)
    dec_st = jnp.exp(csx[L - 1:L, :] - csx)
    return a, dtp, causal, cs, cs.T, dtx, X, Xd, dec_out, dec_st


def _ssd_decay(causal, cs, cs_row, h):
    diff = cs[:, h:h + 1] - cs_row[h:h + 1, :]
    return jnp.where(causal, jnp.exp(jnp.where(causal, diff, 0.0)), 0.0)


def _ssd_groups(xc_ref, g):
    b0, c0 = SSD_INNER + g * SSD_N, SSD_INNER + 2 * SSD_N + g * SSD_N
    return xc_ref[:, b0:b0 + SSD_N].astype(BF16), xc_ref[:, c0:c0 + SSD_N].astype(BF16)


def _pair_decay(cs, pair):
    L = SSD_L
    return jnp.where(_iota((128, 128), 0) < 64, jnp.exp(cs[L - 1:L, 2 * pair:2 * pair + 1]), jnp.exp(cs[L - 1:L, 2 * pair + 1:2 * pair + 2]))


def _ssd_in_specs(nc, rev):
    idx = (lambda c: nc - 1 - c) if rev else (lambda c: c)
    return [pl.BlockSpec((SSD_L, SSD_CONV_DIM), lambda c: (idx(c), 0)), pl.BlockSpec((SSD_L, 128), lambda c: (idx(c), ODT // 128)),
            _vec_spec(128), _vec_spec(128), _vec_spec(SSD_INNER), pl.BlockSpec((128, SSD_INNER), lambda c: (0, 0))]


def _ssd_core(xc, proj, dtb, alog, dskip, E):
    S = xc.shape[0]
    nc = S // SSD_L

    def body(xc_ref, dtr_ref, dtb_ref, al_ref, dx_ref, e_ref, y_ref, hp_ref, h_scr):
        @pl.when(pl.program_id(0) == 0)
        def _():
            h_scr[...] = jnp.zeros_like(h_scr)

        hp_ref[0] = h_scr[...]
        _, _, causal, cs, cs_row, _, X, Xd, dec_out, dec_st = _ssd_chunk(xc_ref, dtr_ref, dtb_ref, al_ref, e_ref)
        Xs = Xd * dec_st
        lane = _iota((SSD_L, 128), 1)
        for g in range(2):
            Bg, Cg = _ssd_groups(xc_ref, g)
            CB = _dot(Cg, Bg, _NT)
            for pr in range(4):
                pair = g * 4 + pr
                psl = slice(pair * 128, (pair + 1) * 128)
                Xdp = Xd[:, psl].astype(BF16)
                r0 = _dot((CB * _ssd_decay(causal, cs, cs_row, 2 * pair)).astype(BF16), Xdp, _NN)
                r1 = _dot((CB * _ssd_decay(causal, cs, cs_row, 2 * pair + 1)).astype(BF16), Xdp, _NN)
                Hp = h_scr[psl, :]
                W = _dot(Cg, Hp.astype(BF16), _NT)
                y_ref[:, psl] = jnp.where(lane < 64, r0, r1) + W * dec_out[:, psl] + X[:, psl] * dx_ref[:, psl]
                h_scr[psl, :] = Hp * _pair_decay(cs, pair) + _dot(Xs[:, psl].astype(BF16), Bg, _TN)

    return pl.pallas_call(
        body, grid=(nc,), in_specs=_ssd_in_specs(nc, False),
        out_specs=(pl.BlockSpec((SSD_L, SSD_INNER), lambda c: (c, 0)), pl.BlockSpec((1, SSD_INNER, SSD_N), lambda c: (c, 0, 0))),
        out_shape=(_sds((S, SSD_INNER)), _sds((nc, SSD_INNER, SSD_N))), scratch_shapes=[pltpu.VMEM((SSD_INNER, SSD_N), F32)],
        name="ssd_core", compiler_params=_ARB)(xc, proj, dtb, alog, dskip, E)


def _ssd_core_bwd(xc, proj, hprev, dy, dtb, alog, dskip, E, ET):
    S = xc.shape[0]
    nc = S // SSD_L
    L = SSD_L

    def body(xc_ref, dtr_ref, dtb_ref, al_ref, dx_ref, e_ref, et_ref, hp_ref, dy_ref, dxc_ref, ddt_ref, st_ref, dh_scr, acc_scr):
        step = pl.program_id(0)

        @pl.when(step == 0)
        def _():
            dh_scr[...] = jnp.zeros_like(dh_scr)
            acc_scr[...] = jnp.zeros_like(acc_scr)
            st_ref[...] = jnp.zeros_like(st_ref)

        a, dtp, causal, cs, cs_row, dtx, X, Xd, dec_out, dec_st = _ssd_chunk(xc_ref, dtr_ref, dtb_ref, al_ref, e_ref)
        lane = _iota((L, 128), 1)
        sub = _iota((128, L), 0)
        dcs_col = jnp.zeros((L, 128), F32)
        dcs_row = jnp.zeros((128, L), F32)
        dcs_last = jnp.zeros((1, 128), F32)
        dcsx, ddtx, dlastx = [], [], []
        for g in range(2):
            Bg, Cg = _ssd_groups(xc_ref, g)
            CB = _dot(Cg, Bg, _NT)
            dCB = jnp.zeros((L, L), F32)
            dB = jnp.zeros((L, SSD_N), F32)
            dC = jnp.zeros((L, SSD_N), F32)
            for pr in range(4):
                pair = g * 4 + pr
                psl = slice(pair * 128, (pair + 1) * 128)
                dY, Xp, Xdp, dop, dsp = dy_ref[:, psl], X[:, psl], Xd[:, psl], dec_out[:, psl], dec_st[:, psl]
                Xdb = Xdp.astype(BF16)
                acc_scr[0:1, psl] += jnp.sum(dY * Xp, 0, keepdims=True)
                Hp = hp_ref[0, psl, :]
                Hb = Hp.astype(BF16)
                dW = (dY * dop).astype(BF16)
                dcx = dY * _dot(Cg, Hb, _NT) * dop
                dC += _dot(dW, Hb, _NN)
                dHp = _dot(dW, Cg, _TN)
                dHn = dh_scr[psl, :]
                cd = _pair_decay(cs, pair)
                dh_scr[psl, :] = dHp + dHn * cd
                rsum = jnp.sum(dHn * Hp * cd, -1, keepdims=True)
                half = _iota((128, 1), 0) < 64
                s0 = jnp.sum(jnp.where(half, rsum, 0.0), 0, keepdims=True)
                s1 = jnp.sum(jnp.where(half, 0.0, rsum), 0, keepdims=True)
                lane1 = _iota((1, 128), 1)
                dcs_last += jnp.where(lane1 == 2 * pair, s0, 0.0) + jnp.where(lane1 == 2 * pair + 1, s1, 0.0)
                dHb = dHn.astype(BF16)
                dXs = _dot(Bg, dHb, _NT)
                dB += _dot((Xdp * dsp).astype(BF16), dHb, _NN)
                dXd = dXs * dsp
                e_st = dXs * Xdp * dsp
                dcx -= e_st
                dlastx.append(jnp.sum(e_st, 0, keepdims=True))
                for i in range(2):
                    h = 2 * pair + i
                    Dm = _ssd_decay(causal, cs, cs_row, h)
                    M = CB * Dm
                    dYm = jnp.where((lane < 64) if i == 0 else (lane >= 64), dY, 0.0).astype(BF16)
                    dM = _dot(dYm, Xdb, _NT)
                    dXd += _dot(M.astype(BF16), dYm, _TN)
                    dCB += dM * Dm
                    Em = dM * M
                    dcs_col += jnp.where(lane == h, jnp.sum(Em, -1, keepdims=True), 0.0)
                    dcs_row += jnp.where(sub == h, jnp.sum(Em, 0, keepdims=True), 0.0)
                dxc_ref[:, psl] = dY * dx_ref[:, psl] + dXd * dtx[:, psl]
                ddtx.append(dXd * Xp)
                dcsx.append(dcx)
            dCBb = dCB.astype(BF16)
            b0, c0 = SSD_INNER + g * SSD_N, SSD_INNER + 2 * SSD_N + g * SSD_N
            dxc_ref[:, b0:b0 + SSD_N] = dB + _dot(dCBb, Cg, _TN)
            dxc_ref[:, c0:c0 + SSD_N] = dC + _dot(dCBb, Bg, _NN)
        ET = et_ref[...]
        dcs = dcs_col - dcs_row.T + _dot_hi(jnp.concatenate(dcsx, 1), ET)
        dlast = dcs_last + _dot_hi(jnp.broadcast_to(jnp.concatenate(dlastx, 1), (8, SSD_INNER)), ET)[0:1, :]
        dcs += jnp.where(_iota((L, 128), 0) == L - 1, dlast, 0.0)
        dda = _dot_hi((_iota((L, L), 1) >= _iota((L, L), 0)).astype(F32), dcs)
        ddtp = dda * a + _dot_hi(jnp.concatenate(ddtx, 1), ET)
        draw = ddtp * _sigmoid(dtr_ref[...] + dtb_ref[...])
        ddt_ref[...] = draw.astype(BF16)
        st_ref[0:1, :] += jnp.sum(draw, 0, keepdims=True)
        st_ref[1:2, :] += jnp.sum(dda * dtp, 0, keepdims=True) * a

        @pl.when(step == nc - 1)
        def _():
            st_ref[2:3, :] = _dot_hi(acc_scr[...], ET)[0:1, :]

    rev = lambda c: (nc - 1 - c, 0)
    return pl.pallas_call(
        body, grid=(nc,),
        in_specs=_ssd_in_specs(nc, True) + [pl.BlockSpec((SSD_INNER, 128), lambda c: (0, 0)),
                                            pl.BlockSpec((1, SSD_INNER, SSD_N), lambda c: (nc - 1 - c, 0, 0)),
                                            pl.BlockSpec((L, SSD_INNER), rev)],
        out_specs=(pl.BlockSpec((L, SSD_CONV_DIM), rev), pl.BlockSpec((L, 128), rev), pl.BlockSpec((8, 128), lambda c: (0, 0))),
        out_shape=(_sds((S, SSD_CONV_DIM)), _sds((S, 128), BF16), _sds((8, 128))),
        scratch_shapes=[pltpu.VMEM((SSD_INNER, SSD_N), F32), pltpu.VMEM((8, SSD_INNER), F32)],
        name="ssd_core_bwd", compiler_params=_ARB)(xc, proj, dtb, alog, dskip, E, ET, hprev, dy)


def _ssd_post(y, proj, nw, tm=256):
    S = y.shape[0]

    def body(y_ref, z_ref, nw_ref, o_ref):
        for g in range(2):
            sl = slice(g * 512, (g + 1) * 512)
            gated = y_ref[:, sl] * _silu(z_ref[:, sl])
            r = lax.rsqrt(jnp.mean(gated * gated, -1, keepdims=True) + EPS)
            o_ref[:, sl] = (gated * r * nw_ref[:, sl]).astype(BF16)

    return pl.pallas_call(
        body, grid=(S // tm,), in_specs=[_row_spec(tm, D), pl.BlockSpec((tm, D), lambda i: (i, OZ // D)), _vec_spec(D)],
        out_specs=_row_spec(tm, D), out_shape=_sds((S, D), BF16), name="ssd_post", compiler_params=_PAR)(y, proj, nw)


def _ssd_post_bwd(doc, y, proj, nw, tm=256):
    S = y.shape[0]

    def body(d_ref, y_ref, z_ref, nw_ref, dy_ref, dz_ref, st_ref):
        @pl.when(pl.program_id(0) == 0)
        def _():
            st_ref[...] = jnp.zeros_like(st_ref)

        for g in range(2):
            sl = slice(g * 512, (g + 1) * 512)
            yv, zv, dv = y_ref[:, sl], z_ref[:, sl], d_ref[:, sl]
            sz = _silu(zv)
            gated = yv * sz
            r = lax.rsqrt(jnp.mean(gated * gated, -1, keepdims=True) + EPS)
            ghat = gated * r
            st_ref[0:1, sl] += jnp.sum(dv * ghat, 0, keepdims=True)
            gg = dv * nw_ref[:, sl]
            dg = r * (gg - ghat * jnp.mean(gg * ghat, -1, keepdims=True))
            dy_ref[:, sl] = dg * sz
            dz_ref[:, sl] = (dg * yv * _dsilu(zv)).astype(BF16)

    zs = pl.BlockSpec((tm, D), lambda i: (i, OZ // D))
    return pl.pallas_call(
        body, grid=(S // tm,), in_specs=[_row_spec(tm, D), _row_spec(tm, D), zs, _vec_spec(D)],
        out_specs=(_row_spec(tm, D), _row_spec(tm, D), _vec_spec(D, 8)), out_shape=(_sds((S, D)), _sds((S, D), BF16), _sds((8, D))),
        name="ssd_post_bwd", compiler_params=_ARB)(doc, y, proj, nw)


def _row(v, n=None):
    v = v.astype(F32).reshape(1, -1)
    return v if n is None else jnp.pad(v, ((0, 0), (0, n - v.shape[1])))


_IN_SEGMENTS = [(0, 384, OQ), (384, 640, OKV), (640, 672, OKV + 320), (672, 1184, OP), (1184, 2208, OZ), (2208, 3744, OX),
                (3744, 3760, ODT), (3760, IN_DIM, OG)]
_IN_ZEROS = [(OKV + 256, OKV + 320), (OKV + 352, OKV + 384), (ODT + 16, ODT + 128)]


def _in_pieces():
    w, out = IN_DIM // 4, []
    for a, b, d in _IN_SEGMENTS:
        while a < b:
            k = a // w
            e = min(b, (k + 1) * w)
            out.append((k, a - k * w, e - k * w, d))
            d, a = d + e - a, e
    return out


def _win_layout(w_in4, tm=256):
    def body(w_ref, o_ref):
        for k, s0, s1, d in _in_pieces():
            o_ref[:, d:d + s1 - s0] = w_ref[k, :, s0:s1]
        for z0, z1 in _IN_ZEROS:
            o_ref[:, z0:z1] = jnp.zeros((tm, z1 - z0), o_ref.dtype)

    return pl.pallas_call(
        body, grid=(D // tm,), in_specs=[pl.BlockSpec((4, tm, IN_DIM // 4), lambda i: (0, i, 0))],
        out_specs=pl.BlockSpec((tm, IN_PAD), lambda i: (i, 0)), out_shape=_sds((D, IN_PAD), w_in4.dtype), name="win_layout",
        compiler_params=_PAR)(w_in4)


def _win_unlayout(dwin, tm=256):
    def body(d_ref, o_ref):
        for k, s0, s1, d in _in_pieces():
            o_ref[k, :, s0:s1] = d_ref[:, d:d + s1 - s0]

    return pl.pallas_call(
        body, grid=(D // tm,), in_specs=[pl.BlockSpec((tm, IN_PAD), lambda i: (i, 0))],
        out_specs=pl.BlockSpec((4, tm, IN_DIM // 4), lambda i: (0, i, 0)), out_shape=_sds((4, D, IN_DIM // 4), dwin.dtype),
        name="win_unlayout", compiler_params=_PAR)(dwin)


def _prep_late(p):
    return dict(wbr=p["w_branch"].astype(BF16), wo=p["w_out"].astype(BF16), wup4=p["ffn_up4"].astype(BF16), wdn=p["ffn_down"].astype(BF16))


def _prep_layer(p, late=None):
    win = _win_layout(p["w_in4"].astype(BF16))
    wqb = jnp.transpose(p["w_q_b4"].astype(BF16).reshape(4, Q_RANK, 2, 96), (1, 0, 2, 3))
    return dict(
        win=win, wqb=jnp.pad(wqb, ((0, 0), (0, 0), (0, 0), (0, 32))).reshape(Q_RANK, D), wkvb4=p["w_kv_b4"].astype(BF16),
        late=late if late is not None else (lambda after: _prep_late(p)),
        nw1=_row(p["norm1_w"]), nw2=_row(p["norm2_w"]), qan=_row(p["q_a_norm"]), kvan=_row(p["kv_a_norm"]),
        wq=_row(p["q_norm"], 128), wk=_row(p["k_norm"], 128), pool_w=p["pool_w"].astype(F32), pool_scale=_row(p["pool_scale"]),
        cw=p["ssd_conv_w"].astype(F32), cb=_row(p["ssd_conv_b"]), dtb=_row(p["ssd_dt_bias"], 128), alog=_row(p["ssd_a_log"], 128),
        dskip=_row(jnp.repeat(p["ssd_d"].astype(F32), SSD_P)), snw=_row(p["ssd_norm_w"]),
        fcw=p["ffn_conv_w"].astype(F32), fcb=_row(p["ffn_conv_b"]))


def _layer_fwd(x, mod8, W, rope, E, tag):
    sh1, sc1, g1, sh2, sc2, g2 = (mod8[i:i + 1] for i in range(6))
    h1 = _ln_mod(x, W["nw1"], sc1, sh1, name=f"ln1_{tag}")
    proj = _mm(h1, W["win"], tn=640, tk=1024, name=f"proj_{tag}")
    qf, kf, ve = _mla_prep(proj, W["wqb"], W["wkvb4"], W["qan"], W["kvan"], W["wq"], W["wk"], rope)
    oa = _attn_fwd(qf, kf, ve)
    ob = _pool_fwd(proj, W["pool_w"], W["pool_scale"])
    xc = _ssd_pre(proj, W["cw"], W["cb"])
    y, hprev = _ssd_core(xc, proj, W["dtb"], W["alog"], W["dskip"], E)
    oc = _ssd_post(y, proj, W["snw"])
    W.update(W["late"](oc))
    merged = _merge_fwd(oa, ob, oc, proj, W["wbr"])
    x1, out1 = _mm(merged, W["wo"], tk=1024, res=x, gate=g1, name=f"wout_{tag}")
    h2 = _ln_mod(x1, W["nw2"], sc2, sh2, name=f"ln2_{tag}")
    up = _up_fwd(h2, W["wup4"], name=f"up_{tag}")
    act = _ffn_act(up, W["fcw"], W["fcb"])
    x2, out2 = _mm(act, W["wdn"], res=x1, gate=g2, name=f"down_{tag}")
    saved = dict(x=x, h1=h1, proj=proj, qf=qf, kf=kf, ve=ve, oa=oa, ob=ob, oc=oc, xc=xc, hprev=hprev, y=y, merged=merged,
                 out1=out1, x1=x1, h2=h2, up=up, act=act, out2=out2)
    return x2, saved


def _layer_bwd(dx2, sv, mod8, W, rope, E, ET, tag, emit=None):
    sc1, g1, sc2, g2 = mod8[1:2], mod8[2:3], mod8[4:5], mod8[5:6]
    proj = sv["proj"]
    dz2, dg2 = _gate_bwd(dx2, sv["out2"], g2, name=f"gate2_bwd_{tag}")
    dact = _mm(dz2, W["wdn"], "nt", tn=1408, tk=1024, name=f"down_dx_{tag}")
    dwdn = _mm(sv["act"], dz2, "tn", tm=1408, name=f"down_dw_{tag}")
    dup2, dfcw, dfcb = _ffn_act_bwd(sv["up"], dact, W["fcw"], W["fcb"])
    dh2 = _up_dx(dup2, W["wup4"], name=f"up_dx_{tag}")
    dwup4 = _up_dw(sv["h2"], dup2, name=f"up_dw_{tag}")
    dx1, st2 = _ln_mod_bwd(sv["x1"], dh2, dx2, W["nw2"], sc2, name=f"ln2_bwd_{tag}")
    dz1, dg1 = _gate_bwd(dx1, sv["out1"], g1, name=f"gate1_bwd_{tag}")
    dmerged = _mm(dz1, W["wo"], "nt", tk=1024, name=f"wout_dx_{tag}")
    dwo = _mm(sv["merged"], dz1, "tn", name=f"wout_dw_{tag}")
    dya, dyb, dyc, dgl, doa, dob, doc = _merge_bwd(dmerged, sv["oa"], sv["ob"], sv["oc"], proj, W["wbr"])
    dwba = _mm(sv["oa"], dya, "tn", name=f"wba_dw_{tag}")
    dwbb = _mm(sv["ob"], dyb, "tn", name=f"wbb_dw_{tag}")
    dwbc = _mm(sv["oc"], dyc, "tn", name=f"wbc_dw_{tag}")
    late = dict(w_branch=jnp.concatenate([dwba, dwbb, dwbc], 0).reshape(4, 512, D), w_out=dwo.reshape(4, 256, D), ffn_up=dwup4,
                ffn_down=dwdn.reshape(4, FFN // 4, D))
    snw = W["snw"]
    if emit is not None:
        token = emit(late)
        if token is not None:
            snw, doa = snw + token, doa + token
    dy, dzs, st_post = _ssd_post_bwd(doc, sv["y"], proj, snw)
    dxc, ddt, st_ssd = _ssd_core_bwd(sv["xc"], proj, sv["hprev"], dy, W["dtb"], W["alog"], W["dskip"], E, ET)
    dxbc, dcw, dcb = _ssd_pre_bwd(proj, dxc, W["cw"], W["cb"])
    dpool, dpw, dps = _pool_bwd(proj, dob, W["pool_w"], W["pool_scale"])
    dqf, dkf, dve = _attn_bwd(sv["qf"], sv["kf"], sv["ve"], doa)
    dqs, dkvs, dwqb, dwkvb4, st_mla = _mla_prep_bwd(proj, dqf, dkf, dve, W["wqb"], W["wkvb4"], W["qan"], W["kvan"], W["wq"], W["wk"], rope)
    dproj = jnp.concatenate([dgl, dzs, dxbc, dpool, dqs, dkvs, ddt], 1)
    dh1 = _mm(dproj, W["win"], "nt", tk=IN_PAD, name=f"proj_dx_{tag}")
    dwin = _mm(sv["h1"], dproj, "tn", tn=640, name=f"proj_dw_{tag}")
    dx, st1 = _ln_mod_bwd(sv["x"], dh1, dx1, W["nw1"], sc1, name=f"ln1_bwd_{tag}")
    grads = dict(
        norm1_w=st1[2], norm2_w=st2[2], w_in=_win_unlayout(dwin),
        q_a_norm=st_mla[0, :Q_RANK], kv_a_norm=st_mla[1, :KV_RANK], q_norm=st_mla[2, :96], k_norm=st_mla[3, :96],
        w_q_b=jnp.transpose(dwqb.reshape(Q_RANK, 4, 2, 128)[:, :, :, :96], (1, 0, 2, 3)).reshape(4, Q_RANK, 192), w_kv_b=dwkvb4,
        pool_w=dpw, pool_scale=dps[0], ssd_conv_w=dcw, ssd_conv_b=dcb[0],
        ssd_dt_bias=st_ssd[0, :SSD_HEADS], ssd_a_log=st_ssd[1, :SSD_HEADS], ssd_d=st_ssd[2, :SSD_HEADS], ssd_norm_w=st_post[0],
        ffn_conv_w=jnp.transpose(dfcw, (1, 0, 2)).reshape(3, 2 * FFN), ffn_conv_b=dfcb.reshape(2 * FFN), **late)
    dmod = jnp.concatenate([st1[0:2], dg1[0:1], st2[0:2], dg2[0:1]], 0)
    return dx, grads, dmod


def _ssd_expand():
    E = (jnp.arange(SSD_INNER)[None, :] // SSD_P == jnp.arange(128)[:, None]).astype(F32)
    return E, E.T


def _rope_tables(positions):
    inv_freq = ROPE_THETA ** (-jnp.arange(0, ROPE_DIM, 2, dtype=F32) / ROPE_DIM)
    invf = jnp.concatenate([jnp.zeros((NOPE,), F32), inv_freq, inv_freq, jnp.zeros((32,), F32)]).reshape(1, 128)
    posb = jnp.broadcast_to(positions.astype(F32)[:, None], (positions.shape[0], 128))
    return _rope_tab(posb, invf)


def _local_step(x, target, positions, mods, get_layer, bwd_mod=None, emit=None, done=None):
    rope = _rope_tables(positions)
    E, ET = _ssd_expand()
    Ws, saved, h = [], [], x
    for l in range(2):
        Ws.append(_prep_layer(*get_layer(l, h)))
        h, sv = _layer_fwd(h, mods[l], Ws[l], rope, E, l)
        saved.append(sv)
    dy, lpart = _loss_grad(h, target)
    grads, dmods = [None, None], [None, None]
    for l in (1, 0):
        mod8 = mods[l] if bwd_mod is None else bwd_mod(l)
        dy, grads[l], dmods[l] = _layer_bwd(dy, saved[l], mod8, Ws[l], rope, E, ET, l, None if emit is None else functools.partial(emit, l))
        if done is not None:
            done(l, grads[l])
    return lpart[0, 0], dy, grads, dmods


_ANY = pl.BlockSpec(memory_space=pl.ANY)
_VMEM = pl.BlockSpec(memory_space=pltpu.VMEM)


def _place():
    x, y, c = lax.axis_index("x"), lax.axis_index("y"), lax.axis_index("c")
    return x, y, c, [(1 - x, y), (x, 1 - y), (1 - x, 1 - y)]


def _allgather8(v, name):
    m_per, n = v.shape

    def body(x_ref, out_ref, send_sems, recv_sems, local_sem):
        x, y, c, chips = _place()
        me, sibling = (x, y, c), (x, y, 1 - c)

        def rows(px, py, pc):
            return out_ref.at[pl.ds((4 * px + 2 * py + pc) * m_per, m_per), :]

        def copy(k, block, to, src=None):
            return pltpu.make_async_remote_copy(src_ref=rows(*block) if src is None else src, dst_ref=rows(*block),
                                                send_sem=send_sems.at[k], recv_sem=recv_sems.at[k], device_id=to, device_id_type=MESH)

        mine = pltpu.make_async_copy(x_ref, rows(*me), local_sem)
        mine.start()
        first = [copy(0, me, sibling, src=x_ref)] + [copy(1 + j, me, (*chip, c), src=x_ref) for j, chip in enumerate(chips)]
        for cp in first:
            cp.start()
        passed = [copy(4 + j, (*chip, c), sibling) for j, chip in enumerate(chips)]
        for j, chip in enumerate(chips):
            copy(1 + j, (*chip, c), me).wait_recv()
            passed[j].start()
        copy(0, sibling, me).wait_recv()
        for j, chip in enumerate(chips):
            copy(4 + j, (*chip, 1 - c), me).wait_recv()
        for cp in first + passed:
            cp.wait_send()
        mine.wait()

    return pl.pallas_call(
        body, out_shape=_sds((8 * m_per, n), v.dtype), in_specs=[_VMEM], out_specs=_VMEM,
        scratch_shapes=[pltpu.SemaphoreType.DMA((7,)), pltpu.SemaphoreType.DMA((7,)), pltpu.SemaphoreType.DMA], name=name)(v)


def _sems(n):
    return [pltpu.SemaphoreType.DMA((n,)), pltpu.SemaphoreType.DMA((n,))]


_HBM = pl.BlockSpec(memory_space=pltpu.HBM)
_SEM = pl.BlockSpec(memory_space=pltpu.SEMAPHORE)
_EFFECT = pltpu.CompilerParams(has_side_effects=pltpu.SideEffectType.DATAFLOW_SIDE_EFFECTING)


def _ici_copy(src_refs, land_refs, send_sems, recv_sems, a, j, slices, incoming):
    x, y, c, chips = _place()
    me, other = 2 * x + y, 2 * chips[j][0] + chips[j][1]
    src, dst = slices(src_refs[a], land_refs[a], other, me, c) if incoming else slices(src_refs[a], land_refs[a], me, other, c)
    return pltpu.make_async_remote_copy(src_ref=src, dst_ref=dst, send_sem=send_sems.at[3 * a + j], recv_sem=recv_sems.at[3 * a + j],
                                        device_id=(*chips[j], c), device_id_type=MESH)


def _ici_start(srcs, land_shapes, slices, after, name):
    na = len(srcs)

    def body(*refs):
        src_refs, land_refs, send_sems, recv_sems = refs[:na], refs[na:2 * na], refs[2 * na + 1], refs[2 * na + 2]
        for a in range(na):
            for j in range(3):
                _ici_copy(src_refs, land_refs, send_sems, recv_sems, a, j, slices, False).start()
        refs[-1][...] = jnp.zeros_like(refs[-1])

    hbm = lambda v: pltpu.with_memory_space_constraint(v, pltpu.HBM)
    lands = [hbm(lax.empty(s.shape, s.dtype)) for s in land_shapes]
    return pl.pallas_call(
        body, name=name,
        out_shape=(pltpu.SemaphoreType.DMA((3 * na,)), pltpu.SemaphoreType.DMA((3 * na,)), *[pltpu.HBM(v.shape, v.dtype) for v in srcs],
                   *[pltpu.HBM(s.shape, s.dtype) for s in land_shapes], _sds((8, 128))),
        in_specs=[_HBM] * (2 * na) + [_ANY], out_specs=(_SEM, _SEM, *[_HBM] * (2 * na), _VMEM),
        input_output_aliases={i: 2 + i for i in range(2 * na)}, compiler_params=_EFFECT)(*[hbm(v) for v in srcs], *lands, after)


def _ici_wait(handle, slices, after, name):
    na = (len(handle) - 3) // 2

    def body(*refs):
        src_refs, land_refs, send_sems, recv_sems = refs[:na], refs[na:2 * na], refs[2 * na], refs[2 * na + 1]
        for a in range(na):
            for j in range(3):
                _ici_copy(src_refs, land_refs, send_sems, recv_sems, a, j, slices, False).wait_send()
                _ici_copy(src_refs, land_refs, send_sems, recv_sems, a, j, slices, True).wait_recv()

    thru = handle[2:2 + 2 * na]
    outs = pl.pallas_call(
        body, name=name, out_shape=[pltpu.HBM(v.shape, v.dtype) for v in thru], in_specs=[_HBM] * (2 * na) + [_SEM, _SEM, _ANY],
        out_specs=[_HBM] * (2 * na), input_output_aliases={i: i for i in range(2 * na)}, compiler_params=_EFFECT)(
        *thru, handle[0], handle[1], after)
    return outs[:na], outs[na:]


def _gather_slices(p_ref, land_ref, sender, receiver, c):
    r2 = p_ref.shape[0] // 2
    return p_ref.at[pl.ds(c * r2, r2), :], land_ref.at[sender, pl.ds(c * r2, r2), :]


def _scatter_slices(a_ref, t_ref, sender, receiver, c):
    return a_ref.at[receiver], t_ref.at[sender]


def _gather_start(arrs, after, name):
    return _ici_start(arrs, [_sds((4,) + v.shape, v.dtype) for v in arrs], _gather_slices, after, name)


def _gather_finish(handle, after, name):
    arrs, stacks = _ici_wait(handle, _gather_slices, after, name + "_wait")
    na = len(stacks)

    def body(*refs):
        s_refs, o_refs, (send_sems, recv_sems) = refs[:na], refs[na:2 * na], refs[2 * na:]
        x, y, c, chips = _place()

        def copy(a, j, cc, to):
            r2 = s_refs[a].shape[1] // 2
            at = (2 * chips[j][0] + chips[j][1], pl.ds(cc * r2, r2), slice(None))
            return pltpu.make_async_remote_copy(src_ref=s_refs[a].at[at], dst_ref=o_refs[a].at[at], send_sem=send_sems.at[3 * a + j],
                                                recv_sem=recv_sems.at[3 * a + j], device_id=to, device_id_type=MESH)

        passed = [copy(a, j, c, (x, y, 1 - c)) for a in range(na) for j in range(3)]
        for cp in passed:
            cp.start()
        for a in range(na):
            for j in range(3):
                copy(a, j, 1 - c, (x, y, c)).wait_recv()
        for cp in passed:
            cp.wait_send()

    stacks = pl.pallas_call(
        body, out_shape=[_sds(v.shape, v.dtype) for v in stacks], in_specs=[_ANY] * na, out_specs=[_ANY] * na,
        input_output_aliases={i: i for i in range(na)}, scratch_shapes=_sems(3 * na), name=name + "_pass")(*stacks)
    chip = 2 * lax.axis_index("x") + lax.axis_index("y")
    return [lax.dynamic_update_slice(s, v[None], (chip, 0, 0)) for s, v in zip(stacks, arrs)]


def _send_halves(gs, name):
    na = len(gs)

    def body(*refs):
        g_refs, o_refs, (send_sems, recv_sems) = refs[:na], refs[na:2 * na], refs[2 * na:]
        x, y, c, _ = _place()
        cps = []
        for a in range(na):
            r2 = g_refs[a].shape[1] // 2
            cps.append(pltpu.make_async_remote_copy(src_ref=g_refs[a].at[:, pl.ds((1 - c) * r2, r2), :], dst_ref=o_refs[a],
                                                    send_sem=send_sems.at[a], recv_sem=recv_sems.at[a],
                                                    device_id=(x, y, 1 - c), device_id_type=MESH))
        for cp in cps:
            cp.start()
        for cp in cps:
            cp.wait()

    return pl.pallas_call(
        body, out_shape=[_sds((4, v.shape[1] // 2, v.shape[2]), v.dtype) for v in gs], in_specs=[_ANY] * na, out_specs=[_ANY] * na,
        scratch_shapes=_sems(na), name=name)(*gs)


def _join_halves(fs, name):
    na = len(fs)

    def body(*refs):
        f_refs, o_refs, (send_sems, recv_sems) = refs[:na], refs[na:2 * na], refs[2 * na:]
        x, y, c, _ = _place()

        def copy(a, cc, to):
            r2 = f_refs[a].shape[0]
            return pltpu.make_async_remote_copy(src_ref=f_refs[a], dst_ref=o_refs[a].at[pl.ds(cc * r2, r2), :], send_sem=send_sems.at[a],
                                                recv_sem=recv_sems.at[a], device_id=to, device_id_type=MESH)

        cps = [copy(a, c, (x, y, 1 - c)) for a in range(na)]
        for cp in cps:
            cp.start()
        for a in range(na):
            copy(a, 1 - c, (x, y, c)).wait_recv()
        for cp in cps:
            cp.wait_send()

    outs = pl.pallas_call(
        body, out_shape=[_sds((2 * v.shape[0], v.shape[1]), v.dtype) for v in fs], in_specs=[_ANY] * na, out_specs=[_ANY] * na,
        scratch_shapes=_sems(na), name=name)(*fs)
    ci = lax.axis_index("c")
    return [lax.dynamic_update_slice(o, f, (ci * f.shape[0], 0)) for o, f in zip(outs, fs)]


def _sum_chips(a, t, chip, name):
    _, r2, n = t.shape
    tm = _row_tile(r2)

    def body(k_ref, a_ref, t1_ref, t2_ref, t3_ref, o_ref):
        o_ref[...] = ((a_ref[...].astype(F32) + t1_ref[...].astype(F32)) + t2_ref[...].astype(F32)) + t3_ref[...].astype(F32)

    def slot(j):
        return pl.BlockSpec((None, tm, n), lambda i, k_ref: (lax.rem(k_ref[0] + j, 4), i, 0))

    return pl.pallas_call(
        body, grid_spec=pltpu.PrefetchScalarGridSpec(num_scalar_prefetch=1, grid=(r2 // tm,), in_specs=[slot(0), slot(1), slot(2), slot(3)],
                                                     out_specs=pl.BlockSpec((tm, n), lambda i, k_ref: (i, 0))),
        out_shape=_sds((r2, n)), name=name, compiler_params=_PAR)(chip.reshape(1).astype(jnp.int32), a, t, t, t)


def _add_cast(g, recv, c, name):
    _, r2, n = recv.shape

    def body(c_ref, a_ref, b_ref, o_ref):
        o_ref[...] = (a_ref[...] + b_ref[...]).astype(BF16)

    spec = pl.BlockSpec((None, r2, n), lambda k, c_ref: (k, 0, 0))
    return pl.pallas_call(
        body, grid_spec=pltpu.PrefetchScalarGridSpec(
            num_scalar_prefetch=1, grid=(4,), in_specs=[pl.BlockSpec((None, r2, n), lambda k, c_ref: (k, c_ref[0], 0)), spec], out_specs=spec),
        out_shape=_sds(recv.shape, BF16), name=name, compiler_params=_PAR)(c.reshape(1).astype(jnp.int32), g, recv)


def _sum_lead(t, name, tm=256):
    P, R, n = t.shape
    tm = _row_tile(R, tm)

    def body(t_ref, o_ref):
        acc = t_ref[0].astype(F32)
        for j in range(1, P):
            acc = acc + t_ref[j].astype(F32)
        o_ref[...] = acc

    return pl.pallas_call(body, grid=(R // tm,), in_specs=[pl.BlockSpec((P, tm, n), lambda i: (0, i, 0))],
                          out_specs=pl.BlockSpec((tm, n), lambda i: (i, 0)), out_shape=_sds((R, n)), name=name, compiler_params=_PAR)(t)


def _ada_fwd(c16, ada_w, ada_b_cols, tn=512):
    L, _, n = ada_w.shape

    def body(c_ref, w_ref, b_ref, o_ref):
        o_ref[0] = _dot(_silu(c_ref[...]).astype(BF16), w_ref[0].astype(BF16), _NN) + b_ref[0]

    return pl.pallas_call(
        body, grid=(L, n // tn),
        in_specs=[pl.BlockSpec((16, D), lambda l, j: (0, 0)), pl.BlockSpec((1, D, tn), lambda l, j: (l, 0, j)), pl.BlockSpec((1, 1, tn), lambda l, j: (l, 0, j))],
        out_specs=pl.BlockSpec((1, 16, tn), lambda l, j: (l, 0, j)), out_shape=_sds((L, 16, n)), name="ada_fwd",
        compiler_params=pltpu.CompilerParams(dimension_semantics=("parallel", "parallel")))(c16, ada_w, ada_b_cols)


def _ada_bwd(c16, dmod, tn=512):
    L, _, n = dmod.shape

    def body(c_ref, d_ref, o_ref):
        o_ref[0] = _dot(_silu(c_ref[...]).astype(BF16), d_ref[0].astype(BF16), _TN)

    return pl.pallas_call(
        body, grid=(L, n // tn), in_specs=[pl.BlockSpec((16, D), lambda l, j: (0, 0)), pl.BlockSpec((1, 16, tn), lambda l, j: (l, 0, j))],
        out_specs=pl.BlockSpec((1, D, tn), lambda l, j: (l, 0, j)), out_shape=_sds((L, D, n)), name="ada_bwd",
        compiler_params=pltpu.CompilerParams(dimension_semantics=("parallel", "parallel")))(c16, dmod)


def _adam_math(w, g, m, v):
    mn = ADAM_B1 * m + (1.0 - ADAM_B1) * g
    vn = ADAM_B2 * v + (1.0 - ADAM_B2) * (g * g)
    m_hat = mn / (1.0 - ADAM_B1 ** ADAM_STEP)
    v_hat = vn / (1.0 - ADAM_B2 ** ADAM_STEP)
    return -ADAM_LR * (m_hat / (jnp.sqrt(v_hat) + ADAM_EPS) + ADAM_WD * w), mn, vn


def _adamw(w, g, m, v, name):
    R, n = w.shape
    tm = _row_tile(R)

    def body(w_ref, g_ref, m_ref, v_ref, d_ref, nm_ref, nv_ref):
        d_ref[...], nm_ref[...], nv_ref[...] = _adam_math(w_ref[...], g_ref[...], m_ref[...], v_ref[...])

    spec = pl.BlockSpec((tm, n), lambda i: (i, 0))
    return pl.pallas_call(body, grid=(R // tm,), in_specs=[spec] * 4, out_specs=(spec,) * 3, out_shape=(_sds((R, n)),) * 3,
                          name=name, compiler_params=_PAR)(w, g, m, v)


def _adamw_layers(w, g0, g1, m, v, after, name):
    _, r, n = w.shape
    tm = _row_tile(r)
    nb = r // tm

    def body(w_ref, g0_ref, g1_ref, m_ref, v_ref, after_ref, g_ref, d_ref, nm_ref, nv_ref):
        gv = jnp.where(pl.program_id(0) == 0, g0_ref[...], g1_ref[...])
        g_ref[...] = gv
        d_ref[...], nm_ref[...], nv_ref[...] = _adam_math(w_ref[...], gv, m_ref[...], v_ref[...])

    spec = pl.BlockSpec((None, tm, n), lambda l, i: (l, i, 0))
    g0_spec = pl.BlockSpec((tm, n), lambda l, i: (i * (1 - l) + (nb - 1) * l, 0))
    g1_spec = pl.BlockSpec((tm, n), lambda l, i: (i * l, 0))
    return pl.pallas_call(body, grid=(2, nb), in_specs=[spec, g0_spec, g1_spec, spec, spec, _ANY], out_specs=(spec,) * 4,
                          out_shape=(_sds(w.shape),) * 4, name=name,
                          compiler_params=pltpu.CompilerParams(dimension_semantics=("arbitrary", "arbitrary")))(w, g0, g1, m, v, after)


_W_NAMES = ["ada_w", "ada_b", "norm1_w", "w_in", "q_a_norm", "w_q_b", "kv_a_norm", "w_kv_b", "q_norm", "k_norm", "pool_w",
            "pool_scale", "ssd_conv_w", "ssd_conv_b", "ssd_dt_bias", "ssd_a_log", "ssd_d", "ssd_norm_w", "w_branch", "w_out",
            "norm2_w", "ffn_up", "ffn_conv_w", "ffn_conv_b", "ffn_down"]
_BIG = [("w_in", (D, IN_DIM // 4), 1), ("w_q_b", (Q_RANK, 192), 1), ("w_kv_b", (KV_RANK, 256), 1), ("w_branch", (512, D), 0),
        ("w_out", (256, D), 0), ("ffn_up", (D, 2 * FFN // 4), 1), ("ffn_down", (FFN // 4, D), 0)]

_SMALL = [("norm1_w", (D,)), ("q_a_norm", (Q_RANK,)), ("kv_a_norm", (KV_RANK,)), ("q_norm", (96,)), ("k_norm", (96,)),
          ("pool_w", (4, 128, 128)), ("pool_scale", (512,)), ("ssd_conv_w", (4, SSD_CONV_DIM)), ("ssd_conv_b", (SSD_CONV_DIM,)),
          ("ssd_dt_bias", (SSD_HEADS,)), ("ssd_a_log", (SSD_HEADS,)), ("ssd_d", (SSD_HEADS,)), ("ssd_norm_w", (D,)), ("norm2_w", (D,)),
          ("ffn_conv_w", (3, 2 * FFN)), ("ffn_conv_b", (2 * FFN,))]
_CONV_SHARDED = {"ssd_conv_w": SSD_CONV_DIM // 4, "ffn_conv_w": 2 * FFN // 4}


def _pack_flat(arrs, mult):
    flat = jnp.concatenate([a.astype(F32).reshape(-1) for a in arrs])
    rows = -(-flat.shape[0] // (128 * mult)) * mult
    return jnp.pad(flat, (0, rows * 128 - flat.shape[0])).reshape(rows, 128), [a.shape for a in arrs]


def _unpack_flat(packed, shapes):
    flat, out, off = packed.reshape(-1), [], 0
    for s in shapes:
        n = 1
        for d in s:
            n *= d
        out.append(flat[off:off + n].reshape(s))
        off += n
    return out


_EARLY = ["w_in", "w_q_b", "w_kv_b"]
_LATE = ["w_branch", "w_out", "ffn_up", "ffn_down"]


def _early_weights(a, l, stacks, conv_full):
    p = {n: a[n][l] for n in _W_NAMES if n not in ("ada_w", "ada_b")}
    p.update({n: conv_full[n][l] for n in conv_full})
    p.update(w_in4=stacks[0], w_q_b4=stacks[1], w_kv_b4=stacks[2])
    return p


def _late_weights(stacks):
    return _prep_late(dict(w_branch=stacks[0].reshape(2048, D), w_out=stacks[1].reshape(D, D), ffn_up4=stacks[2],
                           ffn_down=stacks[3].reshape(FFN, D)))


def _reduce_start(gs, ci, after, tag):
    recv = _send_halves(gs, f"rs_halves_{tag}")
    chip_sum = [_add_cast(g, r, ci, f"rs_add_{tag}") for g, r in zip(gs, recv)]
    return _ici_start(chip_sum, [_sds(v.shape, v.dtype) for v in chip_sum], _scatter_slices, after, f"rs_scatter_{tag}_start")


def _reduce_finish(started, after, tag):
    chip_sum, got = _ici_wait(started, _scatter_slices, after, f"rs_scatter_{tag}_wait")
    chip = 2 * lax.axis_index("x") + lax.axis_index("y")
    return _join_halves([_sum_chips(s, t, chip, f"rs_sum_{tag}") for s, t in zip(chip_sum, got)], f"rs_join_{tag}")


def kernel(x, c, positions, ada_w, ada_b, norm1_w, w_in, q_a_norm, w_q_b, kv_a_norm, w_kv_b, q_norm, k_norm, pool_w, pool_scale, ssd_conv_w, ssd_conv_b, ssd_dt_bias, ssd_a_log, ssd_d, ssd_norm_w, w_branch, w_out, norm2_w, ffn_up, ffn_conv_w, ffn_conv_b, ffn_down, loss_target, m_ada_w, m_ada_b, m_norm1_w, m_w_in, m_q_a_norm, m_w_q_b, m_kv_a_norm, m_w_kv_b, m_q_norm, m_k_norm, m_pool_w, m_pool_scale, m_ssd_conv_w, m_ssd_conv_b, m_ssd_dt_bias, m_ssd_a_log, m_ssd_d, m_ssd_norm_w, m_w_branch, m_w_out, m_norm2_w, m_ffn_up, m_ffn_conv_w, m_ffn_conv_b, m_ffn_down, v_ada_w, v_ada_b, v_norm1_w, v_w_in, v_q_a_norm, v_w_q_b, v_kv_a_norm, v_w_kv_b, v_q_norm, v_k_norm, v_pool_w, v_pool_scale, v_ssd_conv_w, v_ssd_conv_b, v_ssd_dt_bias, v_ssd_a_log, v_ssd_d, v_ssd_norm_w, v_w_branch, v_w_out, v_norm2_w, v_ffn_up, v_ffn_conv_w, v_ffn_conv_b, v_ffn_down):
    a = dict(locals())
    xi, yi, ci = lax.axis_index("x"), lax.axis_index("y"), lax.axis_index("c")
    chip = 2 * xi + yi
    dev = 2 * chip + ci
    ncol = 6 * D // 4

    c_all = _allgather8(c.reshape(8, 128), "gather_c").reshape(8, D)
    c16 = jnp.pad(c_all, ((0, 8), (0, 0)))
    ada_b_cols = lax.dynamic_slice_in_dim(ada_b, chip * ncol, ncol, axis=1).reshape(2, 1, ncol)
    mod_part = _ada_fwd(c16, ada_w, ada_b_cols)[:, :8]
    small1, shapes1 = _pack_flat([mod_part, ssd_conv_w, ffn_conv_w], 8)
    got1 = _allgather8(small1, "gather_mod").reshape(8, -1, 128)
    per_chip = [_unpack_flat(got1[2 * k], shapes1) for k in range(4)]
    mod_all = jnp.concatenate([per_chip[k][0] for k in range(4)], -1)
    conv_full = {"ssd_conv_w": jnp.concatenate([per_chip[k][1] for k in range(4)], -1),
                 "ffn_conv_w": jnp.concatenate([per_chip[k][2] for k in range(4)], -1)}
    mod_mine = lax.dynamic_index_in_dim(mod_all, dev, axis=1, keepdims=False).reshape(2, 6, D)
    mods = [jnp.pad(mod_mine[l], ((0, 2), (0, 0))) for l in range(2)]

    big = _EARLY + _LATE
    shard = lambda names, l: [a[n][l].astype(BF16) for n in names]
    g0a = _gather_start(shard(_EARLY, 0), mods[0], "gather_0a")
    g0b = _gather_start(shard(_LATE, 0), g0a[-1], "gather_0b")
    g1 = _gather_start(shard(big, 1), g0b[-1], "gather_1")
    mods[0] = mods[0] + g1[-1][0, 0]

    def get_layer(l, after):
        if l == 0:
            return (_early_weights(a, 0, _gather_finish(g0a, mods[0], "gather_0a"), conv_full),
                    lambda aft: _late_weights(_gather_finish(g0b, aft, "gather_0b")))
        stacks = _gather_finish(g1, after, "gather_1")
        return _early_weights(a, 1, stacks[:3], conv_full), lambda aft: _late_weights(stacks[3:])

    scatters = {}

    def bwd_mod(l):
        return mods[l] if l == 1 else mods[0] + scatters["1"][-1][0, 0]

    def emit(l, late):
        if l == 0:
            scatters["0b"] = _reduce_start([late[n] for n in _LATE], ci, scatters["1"][-1], "0b")
            return scatters["0b"][-1][0, 0]

    def done(l, grads_l):
        if l == 1:
            scatters["1"] = _reduce_start([grads_l[n] for n in big], ci, grads_l["norm1_w"], "1")
        else:
            scatters["0a"] = _reduce_start([grads_l[n] for n in _EARLY], ci, scatters["0b"][-1], "0a")

    lpart, grad_x, grads, dmods = _local_step(x[0], loss_target[0], positions[0], mods, get_layer, bwd_mod, emit, done)
    loss = lax.psum(lpart, ("x", "y", "c"))
    red1 = _reduce_finish(scatters["1"], grad_x, "1")
    red0b = _reduce_finish(scatters["0b"], red1[0], "0b")

    small2, shapes2 = _pack_flat([jnp.stack(dmods)] + [grads[l][n] for l in range(2) for n, _ in _SMALL], 16)
    got2 = _allgather8(small2, "gather_small").reshape(8, -1, 128)
    tot = _unpack_flat(_sum_lead(got2, "sum_small"), shapes2)
    g = {"ada_b": tot[0].reshape(2, 6 * D)}
    for i, (n, _) in enumerate(_SMALL):
        g[n] = jnp.stack([tot[1 + i], tot[1 + len(_SMALL) + i]])
    for n, w in _CONV_SHARDED.items():
        g[n] = lax.dynamic_slice_in_dim(g[n], chip * w, w, axis=2)
    nd = 2 * 6 * D // 128
    dmod_all = jnp.transpose(got2[:, :nd].reshape(8, 2, 6 * D), (1, 0, 2))
    dmod_cols = lax.dynamic_slice_in_dim(jnp.pad(dmod_all, ((0, 0), (0, 8), (0, 0))), chip * ncol, ncol, axis=2)
    g["ada_w"] = _ada_bwd(c16, dmod_cols)

    delta, new_m, new_v = {}, {}, {}
    red1 = dict(zip(big, red1))
    token = scatters["0a"][-1]
    for n, r0 in zip(_LATE, red0b):
        g[n], delta[n], new_m[n], new_v[n] = _adamw_layers(a[n], r0, red1[n], a["m_" + n], a["v_" + n], token, f"adamw_{n}")
    shp = ada_w.shape
    r2 = lambda t: t.reshape(-1, shp[-1])
    delta["ada_w"], new_m["ada_w"], new_v["ada_w"] = (
        o.reshape(shp) for o in _adamw(r2(ada_w), r2(g["ada_w"]), r2(m_ada_w), r2(v_ada_w), "adamw_ada_w"))
    behind = (delta[_LATE[-1]][0, 0, :1] + delta["ada_w"][0, 0, :1]).reshape(1)
    red0a = _reduce_finish(scatters["0a"], behind, "0a")
    for n, r0 in zip(_EARLY, red0a):
        g[n], delta[n], new_m[n], new_v[n] = _adamw_layers(a[n], r0, red1[n], a["m_" + n], a["v_" + n], token, f"adamw_{n}")
    rest = [n for n in _W_NAMES if n not in big and n != "ada_w"]
    packs = [_pack_flat([t[n] if pre is None else t[pre + n] for n in rest], 128)[0]
             for t, pre in ((a, None), (g, None), (a, "m_"), (a, "v_"))]
    rest_shapes = [a[n].shape for n in rest]
    outs = [_unpack_flat(o, rest_shapes) for o in _adamw(*packs, "adamw_rest")]
    for i, n in enumerate(rest):
        delta[n], new_m[n], new_v[n] = outs[0][i], outs[1][i], outs[2][i]

    return (loss, grad_x[None], *[g[n] for n in _W_NAMES], *[delta[n] for n in _W_NAMES],
            *[new_m[n] for n in _W_NAMES], *[new_v[n] for n in _W_NAMES])
```

```python
import functools

import jax
import jax.numpy as jnp
from jax import lax
from jax.experimental import pallas as pl
from jax.experimental.pallas import tpu as pltpu

F32 = jnp.float32
BF16 = jnp.bfloat16
MESH = pl.DeviceIdType.MESH
HI = lax.Precision.HIGHEST

D = 1024
N_HEADS = 8
NOPE, ROPE_DIM = 64, 32
Q_RANK, KV_RANK = 384, 256
POOL_WINDOWS = (2, 4, 8, 16)
SSD_HEADS, SSD_P, SSD_N, SSD_L = 16, 64, 128, 128
SSD_INNER = 1024
SSD_CONV_DIM = 1536
FFN = 2816
EPS = 1e-6
ROPE_THETA = 10000.0
OG, OZ, OX, OP, OQ, OKV, ODT, IN_PAD = 0, 3072, 4096, 5632, 6144, 6528, 6912, 7040
IN_DIM = 6832
ADAM_LR, ADAM_B1, ADAM_B2, ADAM_EPS, ADAM_WD, ADAM_STEP = 0.001, 0.9, 0.999, 1e-08, 0.01, 10

_ARB = pltpu.CompilerParams(dimension_semantics=("arbitrary",))
_PAR = pltpu.CompilerParams(dimension_semantics=("parallel",))


def _pick(n, pref):
    if n <= pref:
        return n
    best = None
    for t in range(128, pref + 1, 128):
        if n % t == 0:
            best = t
    assert best is not None, (n, pref)
    return best


def _row_tile(r, cap=256):
    best = None
    for t in range(16, min(r, cap) + 1, 16):
        if r % t == 0:
            best = t
    assert best is not None, r
    return best


def _sds(shape, dtype=F32):
    return jax.ShapeDtypeStruct(tuple(shape), dtype)


def _iota(shape, dim):
    return lax.broadcasted_iota(jnp.int32, shape, dim)


def _sigmoid(x):
    return 0.5 * jnp.tanh(0.5 * x) + 0.5


def _silu(x):
    return x * _sigmoid(x)


def _dsilu(x):
    s = _sigmoid(x)
    return s * (1.0 + x * (1.0 - s))


def _dot(a, b, dims):
    return lax.dot_general(a, b, (dims, ((), ())), preferred_element_type=F32)


_NN, _NT, _TN = ((1,), (0,)), ((1,), (1,)), ((0,), (0,))


def _dot_hi(a, b, dims=_NN):
    return lax.dot_general(a, b, (dims, ((), ())), preferred_element_type=F32, precision=HI)


def _shift_down(x, j):
    n = x.shape[0]
    return jnp.where(_iota(x.shape, 0) >= j, pltpu.roll(x, j, 0), 0.0)


def _shift_up(x, j):
    n = x.shape[0]
    return jnp.where(_iota(x.shape, 0) < n - j, pltpu.roll(x, n - j, 0), 0.0)


def _mm(a, b, mode="nn", out_dtype=F32, tm=512, tn=512, tk=4096, res=None, gate=None, name="mm"):
    if mode == "nn":
        (M, K), (K2, N) = a.shape, b.shape
    elif mode == "nt":
        (M, K), (N, K2) = a.shape, b.shape
    else:
        (K, M), (K2, N) = a.shape, b.shape
    assert K == K2, (a.shape, b.shape, mode)
    tm, tn, tk = _pick(M, tm), _pick(N, tn), _pick(K, tk)
    nk = K // tk
    dims = {"nn": _NN, "nt": _NT, "tn": _TN}[mode]
    fused = res is not None

    def body(*refs):
        a_ref, b_ref = refs[:2]

        def finish(acc):
            if fused:
                r_ref, g_ref, o_ref, raw_ref = refs[2:6]
                raw_ref[...] = acc
                o_ref[...] = r_ref[...] + g_ref[...] * acc
            else:
                refs[2][...] = acc.astype(out_dtype)

        _mm_steps(a_ref, b_ref, dims, nk, refs[-1] if nk > 1 else None, finish)

    if mode == "nn":
        a_spec = pl.BlockSpec((tm, tk), lambda i, j, k: (i, k))
        b_spec = pl.BlockSpec((tk, tn), lambda i, j, k: (k, j))
    elif mode == "nt":
        a_spec = pl.BlockSpec((tm, tk), lambda i, j, k: (i, k))
        b_spec = pl.BlockSpec((tn, tk), lambda i, j, k: (j, k))
    else:
        a_spec = pl.BlockSpec((tk, tm), lambda i, j, k: (k, i))
        b_spec = pl.BlockSpec((tk, tn), lambda i, j, k: (k, j))
    o_spec = pl.BlockSpec((tm, tn), lambda i, j, k: (i, j))
    in_specs, args = [a_spec, b_spec], [a, b]
    out_shape, out_specs = _sds((M, N), out_dtype), o_spec
    if fused:
        in_specs += [o_spec, pl.BlockSpec((1, tn), lambda i, j, k: (0, j))]
        args += [res, gate]
        out_shape, out_specs = (_sds((M, N)), _sds((M, N))), (o_spec, o_spec)
    return pl.pallas_call(
        body, grid=(M // tm, N // tn, nk), in_specs=in_specs, out_specs=out_specs, out_shape=out_shape,
        scratch_shapes=[pltpu.VMEM((tm, tn), F32)] if nk > 1 else [], name=name,
        compiler_params=pltpu.CompilerParams(dimension_semantics=("parallel", "parallel", "arbitrary")),
    )(*args)


def _mm_steps(a_ref, b_ref, dims, nk, acc_ref, finish):
    part = _dot(a_ref[...].astype(BF16), b_ref[...].astype(BF16), dims)
    if nk == 1:
        finish(part)
        return
    k = pl.program_id(2)

    @pl.when(k == 0)
    def _():
        acc_ref[...] = part

    @pl.when(k > 0)
    def _():
        acc_ref[...] += part

    @pl.when(k == nk - 1)
    def _():
        finish(acc_ref[...])


def _mm_blocks(a, b, dims, grid, a_spec, b_spec, o_spec, out_shape, acc_shape, name):
    nk = grid[2]

    def body(a_ref, b_ref, o_ref, *scratch):
        def finish(acc):
            o_ref[...] = acc.astype(o_ref.dtype)

        _mm_steps(a_ref, b_ref, dims, nk, scratch[0] if nk > 1 else None, finish)

    return pl.pallas_call(
        body, grid=grid, in_specs=[a_spec, b_spec], out_specs=o_spec, out_shape=out_shape,
        scratch_shapes=[pltpu.VMEM(acc_shape, F32)] if nk > 1 else [], name=name,
        compiler_params=pltpu.CompilerParams(dimension_semantics=("parallel", "parallel", "arbitrary")),
    )(a, b)


_UP_SHARD = 2 * FFN // 4


def _up_fwd(h2, wup4, name, tm=512):
    S = h2.shape[0]
    tm = min(tm, S)
    return _mm_blocks(h2, wup4, _NN, (S // tm, 4, 1), pl.BlockSpec((tm, D), lambda i, j, k: (i, 0)),
                      pl.BlockSpec((None, D, _UP_SHARD), lambda i, j, k: (j, 0, 0)), pl.BlockSpec((tm, _UP_SHARD), lambda i, j, k: (i, j)),
                      _sds((S, 2 * FFN)), (tm, _UP_SHARD), name)


def _up_dx(dup2, wup4, name, tm=512, tn=512):
    S = dup2.shape[1]
    tm = min(tm, S)
    return _mm_blocks(dup2, wup4, _NT, (S // tm, D // tn, 4), pl.BlockSpec((None, tm, _UP_SHARD), lambda i, j, k: (lax.div(k, 2), i, lax.rem(k, 2))),
                      pl.BlockSpec((None, tn, _UP_SHARD), lambda i, j, k: (k, j, 0)), pl.BlockSpec((tm, tn), lambda i, j, k: (i, j)),
                      _sds((S, D)), (tm, tn), name)


def _up_dw(h2, dup2, name, tm=512, tk=4096):
    S = h2.shape[0]
    tk = min(tk, S)
    return _mm_blocks(h2, dup2, _TN, (D // tm, 4, S // tk), pl.BlockSpec((tk, tm), lambda i, j, k: (k, i)),
                      pl.BlockSpec((None, tk, _UP_SHARD), lambda i, j, k: (lax.div(j, 2), k, lax.rem(j, 2))),
                      pl.BlockSpec((None, tm, _UP_SHARD), lambda i, j, k: (j, i, 0)), _sds((4, D, _UP_SHARD)), (tm, _UP_SHARD), name)


def _row_spec(tm, n):
    return pl.BlockSpec((tm, n), lambda i: (i, 0))


def _vec_spec(n, rows=1):
    return pl.BlockSpec((rows, n), lambda i: (0, 0))


def _ln_mod(x, nw, sc, sh, name, tm=256):
    S = x.shape[0]

    def body(x_ref, nw_ref, sc_ref, sh_ref, o_ref):
        xv = x_ref[...]
        r = lax.rsqrt(jnp.mean(xv * xv, -1, keepdims=True) + EPS)
        o_ref[...] = ((xv * r * nw_ref[...]) * (1.0 + sc_ref[...]) + sh_ref[...]).astype(BF16)

    return pl.pallas_call(
        body, grid=(S // tm,), in_specs=[_row_spec(tm, D)] + [_vec_spec(D)] * 3, out_specs=_row_spec(tm, D),
        out_shape=_sds((S, D), BF16), name=name, compiler_params=_PAR)(x, nw, sc, sh)


def _ln_mod_bwd(x, dh, dres, nw, sc, name, tm=256):
    S = x.shape[0]

    def body(x_ref, dh_ref, dres_ref, nw_ref, sc_ref, dx_ref, st_ref):
        @pl.when(pl.program_id(0) == 0)
        def _():
            st_ref[...] = jnp.zeros_like(st_ref)

        xv, dhv, nwv = x_ref[...], dh_ref[...], nw_ref[...]
        r = lax.rsqrt(jnp.mean(xv * xv, -1, keepdims=True) + EPS)
        xhat = xv * r
        dn = dhv * (1.0 + sc_ref[...])
        g = dn * nwv
        dx_ref[...] = dres_ref[...] + r * (g - xhat * jnp.mean(g * xhat, -1, keepdims=True))
        st_ref[0:1, :] += jnp.sum(dhv, 0, keepdims=True)
        st_ref[1:2, :] += jnp.sum(dhv * (xhat * nwv), 0, keepdims=True)
        st_ref[2:3, :] += jnp.sum(dn * xhat, 0, keepdims=True)

    return pl.pallas_call(
        body, grid=(S // tm,), in_specs=[_row_spec(tm, D)] * 3 + [_vec_spec(D)] * 2,
        out_specs=(_row_spec(tm, D), _vec_spec(D, 8)), out_shape=(_sds((S, D)), _sds((8, D))),
        name=name, compiler_params=_ARB)(x, dh, dres, nw, sc)


def _gate_bwd(dx, out, g, name, tm=256):
    S = dx.shape[0]

    def body(dx_ref, o_ref, g_ref, dz_ref, dg_ref):
        @pl.when(pl.program_id(0) == 0)
        def _():
            dg_ref[...] = jnp.zeros_like(dg_ref)

        dxv = dx_ref[...]
        dz_ref[...] = (dxv * g_ref[...]).astype(BF16)
        dg_ref[0:1, :] += jnp.sum(dxv * o_ref[...], 0, keepdims=True)

    return pl.pallas_call(
        body, grid=(S // tm,), in_specs=[_row_spec(tm, D)] * 2 + [_vec_spec(D)],
        out_specs=(_row_spec(tm, D), _vec_spec(D, 8)), out_shape=(_sds((S, D), BF16), _sds((8, D))),
        name=name, compiler_params=_ARB)(dx, out, g)


def _loss_grad(y, t, tm=256):
    S = y.shape[0]

    def body(y_ref, t_ref, dy_ref, l_ref):
        @pl.when(pl.program_id(0) == 0)
        def _():
            l_ref[...] = jnp.zeros_like(l_ref)

        e = y_ref[...] - t_ref[...]
        dy_ref[...] = e * (1.0 / D)
        l_ref[...] += 0.5 * jnp.sum(jnp.mean(e * e, -1, keepdims=True), 0, keepdims=True)

    return pl.pallas_call(
        body, grid=(S // tm,), in_specs=[_row_spec(tm, D)] * 2,
        out_specs=(_row_spec(tm, D), pl.BlockSpec((8, 128), lambda i: (0, 0))),
        out_shape=(_sds((S, D)), _sds((8, 128))), name="loss_grad", compiler_params=_ARB)(y, t)


_CONV_ROWS = 256
_HALO = 8


def _rows(ref, lo, hi):
    S, c = ref.shape
    parts = [jnp.zeros((-lo, c), F32)] if lo < 0 else []
    parts.append(ref[max(lo, 0):min(hi, S), :])
    if hi > S:
        parts.append(jnp.zeros((hi - S, c), F32))
    return parts[0] if len(parts) == 1 else jnp.concatenate(parts, 0)


def _conv_rows(ext, w, b, n):
    K = w.shape[0]
    acc = ext[_HALO:_HALO + n] * w[K - 1:K, :] + b
    for j in range(1, K):
        acc = acc + pltpu.roll(ext, j, 0)[_HALO:_HALO + n] * w[K - 1 - j:K - j, :]
    return acc


def _conv_rows_bwd(ext, w, dc, n):
    K, m = w.shape[0], dc.shape[0]
    d0 = dc[0:n]
    dx = d0 * w[K - 1:K, :]
    dws = [jnp.sum(d0 * ext[_HALO:_HALO + n], 0, keepdims=True)]
    for j in range(1, K):
        dx = dx + pltpu.roll(dc, m - j, 0)[0:n] * w[K - 1 - j:K - j, :]
        dws.append(jnp.sum(d0 * pltpu.roll(ext, j, 0)[_HALO:_HALO + n], 0, keepdims=True))
    return dx, dws[::-1], jnp.sum(d0, 0, keepdims=True)


def _col_spec(S, tc, off=0):
    return pl.BlockSpec((S, tc), lambda j: (0, j + off))


def _ssd_pre(proj, cw, cb, tc=256):
    S, n = proj.shape[0], SSD_CONV_DIM
    R = min(_CONV_ROWS, S)

    def body(x_ref, w_ref, b_ref, o_ref):
        wv, bv = w_ref[...], b_ref[...]
        for r0 in range(0, S, R):
            o_ref[r0:r0 + R, :] = _silu(_conv_rows(_rows(x_ref, r0 - _HALO, r0 + R), wv, bv, R))

    return pl.pallas_call(
        body, grid=(n // tc,),
        in_specs=[_col_spec(S, tc, OX // tc), pl.BlockSpec((4, tc), lambda j: (0, j)), pl.BlockSpec((1, tc), lambda j: (0, j))],
        out_specs=_col_spec(S, tc), out_shape=_sds((S, n)), name="ssd_pre", compiler_params=_PAR)(proj, cw, cb)


def _ssd_pre_bwd(proj, dxc, cw, cb, tc=256):
    S, n = proj.shape[0], SSD_CONV_DIM
    R = min(_CONV_ROWS, S)

    def body(x_ref, d_ref, w_ref, b_ref, dx_ref, dw_ref, db_ref):
        wv, bv = w_ref[...], b_ref[...]
        acc = [jnp.zeros((1, tc), F32)] * 5
        for r0 in range(0, S, R):
            ext = _rows(x_ref, r0 - _HALO, r0 + R + _HALO)
            dc = _rows(d_ref, r0, r0 + R + _HALO) * _dsilu(_conv_rows(ext, wv, bv, R + _HALO))
            dx, dws, db = _conv_rows_bwd(ext, wv, dc, R)
            dx_ref[r0:r0 + R, :] = dx.astype(BF16)
            acc = [s + d for s, d in zip(acc, dws + [db])]
        for k in range(4):
            dw_ref[k:k + 1, :] = acc[k]
        db_ref[...] = acc[4]

    wspec, bspec = pl.BlockSpec((4, tc), lambda j: (0, j)), pl.BlockSpec((1, tc), lambda j: (0, j))
    return pl.pallas_call(
        body, grid=(n // tc,), in_specs=[_col_spec(S, tc, OX // tc), _col_spec(S, tc), wspec, bspec],
        out_specs=(_col_spec(S, tc), wspec, bspec), out_shape=(_sds((S, n), BF16), _sds((4, n)), _sds((1, n))),
        name="ssd_pre_bwd", compiler_params=_PAR)(proj, dxc, cw, cb)


def _ffn_act(up, cw, cb, tc=256):
    S, nb = up.shape[0], FFN // tc
    R = min(_CONV_ROWS, S)

    def body(g_ref, v_ref, wg_ref, wv_ref, bg_ref, bv_ref, o_ref):
        wg, wv, bg, bv = wg_ref[...], wv_ref[...], bg_ref[...], bv_ref[...]
        for r0 in range(0, S, R):
            cg = _conv_rows(_rows(g_ref, r0 - _HALO, r0 + R), wg, bg, R)
            cv = _conv_rows(_rows(v_ref, r0 - _HALO, r0 + R), wv, bv, R)
            o_ref[r0:r0 + R, :] = (_silu(cg) * cv).astype(BF16)

    def wspec(off):
        return pl.BlockSpec((3, tc), lambda j: (0, j + off))

    def bspec(off):
        return pl.BlockSpec((1, tc), lambda j: (0, j + off))

    return pl.pallas_call(
        body, grid=(nb,), in_specs=[_col_spec(S, tc), _col_spec(S, tc, nb), wspec(0), wspec(nb), bspec(0), bspec(nb)],
        out_specs=_col_spec(S, tc), out_shape=_sds((S, FFN), BF16), name="ffn_act", compiler_params=_PAR)(up, up, cw, cw, cb, cb)


def _ffn_act_bwd(up, dact, cw, cb, tc=256):
    S, nb = up.shape[0], FFN // tc
    R = min(_CONV_ROWS, S)

    def body(g_ref, v_ref, d_ref, wg_ref, wv_ref, bg_ref, bv_ref, dx_ref, dw_ref, db_ref):
        wg, wv, bg, bv = wg_ref[...], wv_ref[...], bg_ref[...], bv_ref[...]
        acc = [[jnp.zeros((1, tc), F32)] * 4, [jnp.zeros((1, tc), F32)] * 4]
        for r0 in range(0, S, R):
            eg, ev = _rows(g_ref, r0 - _HALO, r0 + R + _HALO), _rows(v_ref, r0 - _HALO, r0 + R + _HALO)
            da = _rows(d_ref, r0, r0 + R + _HALO)
            cg, cv = _conv_rows(eg, wg, bg, R + _HALO), _conv_rows(ev, wv, bv, R + _HALO)
            sg = _sigmoid(cg)
            for half, (ext, w, dc) in enumerate(((eg, wg, da * cv * (sg * (1.0 + cg * (1.0 - sg)))), (ev, wv, da * (cg * sg)))):
                dx, dws, db = _conv_rows_bwd(ext, w, dc, R)
                dx_ref[half, r0:r0 + R, :] = dx.astype(BF16)
                acc[half] = [s + d for s, d in zip(acc[half], dws + [db])]
        for half in range(2):
            for k in range(3):
                dw_ref[half, k:k + 1, :] = acc[half][k]
            db_ref[half] = acc[half][3]

    def wspec(off):
        return pl.BlockSpec((3, tc), lambda j: (0, j + off))

    def bspec(off):
        return pl.BlockSpec((1, tc), lambda j: (0, j + off))

    cs = _col_spec(S, tc)
    both = lambda r: pl.BlockSpec((2, r, tc), lambda j: (0, 0, j))
    return pl.pallas_call(
        body, grid=(nb,), in_specs=[cs, _col_spec(S, tc, nb), cs, wspec(0), wspec(nb), bspec(0), bspec(nb)],
        out_specs=(both(S), both(3), both(1)), out_shape=(_sds((2, S, FFN), BF16), _sds((2, 3, FFN)), _sds((2, 1, FFN))),
        name="ffn_act_bwd", compiler_params=_PAR)(up, up, dact, cw, cw, cb, cb)


def _window_sum(x, w, up=False):
    shift = _shift_up if up else _shift_down
    j = 1
    while j < w:
        x = x + shift(x, j)
        j *= 2
    return x


def _pool_fwd(proj, pool_w, pool_scale):
    S = proj.shape[0]

    def body(u_ref, w_ref, s_ref, o_ref):
        cnt_row = (_iota((S, 128), 0) + 1).astype(F32)
        for g, w in enumerate(POOL_WINDOWS):
            sl = slice(g * 128, (g + 1) * 128)
            u = u_ref[:, sl]
            pooled = _window_sum(u, w) / jnp.minimum(cnt_row, float(w)) - u
            mixed = _dot(pooled.astype(BF16), w_ref[g].astype(BF16), _NN)
            o_ref[:, sl] = (mixed * s_ref[:, sl]).astype(BF16)

    return pl.pallas_call(
        body, grid=(1,),
        in_specs=[pl.BlockSpec((S, 512), lambda i: (0, OP // 512)), pl.BlockSpec((4, 128, 128), lambda i: (0, 0, 0)), _vec_spec(512)],
        out_specs=pl.BlockSpec((S, 512), lambda i: (0, 0)), out_shape=_sds((S, 512), BF16), name="pool_fwd",
        compiler_params=_ARB)(proj, pool_w, pool_scale)


def _pool_bwd(proj, dob, pool_w, pool_scale):
    S = proj.shape[0]

    def body(u_ref, d_ref, w_ref, s_ref, du_ref, dw_ref, ds_ref):
        cnt_row = (_iota((S, 128), 0) + 1).astype(F32)
        for g, w in enumerate(POOL_WINDOWS):
            sl = slice(g * 128, (g + 1) * 128)
            u, dv, wv = u_ref[:, sl], d_ref[:, sl], w_ref[g].astype(BF16)
            cnt = jnp.minimum(cnt_row, float(w))
            pooled = (_window_sum(u, w) / cnt - u).astype(BF16)
            ds_ref[:, sl] = jnp.sum(dv * _dot(pooled, wv, _NN), 0, keepdims=True)
            dmix = (dv * s_ref[:, sl]).astype(BF16)
            dw_ref[g] = _dot(pooled, dmix, _TN)
            dp = _dot(dmix, wv, _NT)
            du_ref[:, sl] = (_window_sum(dp / cnt, w, up=True) - dp).astype(BF16)

    blk = pl.BlockSpec((S, 512), lambda i: (0, 0))
    wspec = pl.BlockSpec((4, 128, 128), lambda i: (0, 0, 0))
    return pl.pallas_call(
        body, grid=(1,), in_specs=[pl.BlockSpec((S, 512), lambda i: (0, OP // 512)), blk, wspec, _vec_spec(512)],
        out_specs=(blk, wspec, _vec_spec(512)), out_shape=(_sds((S, 512), BF16), _sds((4, 128, 128)), _sds((1, 512))),
        name="pool_bwd", compiler_params=_ARB)(proj, dob, pool_w, pool_scale)


def _branch_specs():
    return [pl.BlockSpec((512, D), lambda i: (0, 0)), pl.BlockSpec((512, D), lambda i: (1, 0)), pl.BlockSpec((1024, D), lambda i: (1, 0))]


def _merge_fwd(oa, ob, oc, proj, wbr, tm=256):
    S = oa.shape[0]

    def body(oa_ref, ob_ref, oc_ref, gl_ref, wa_ref, wb_ref, wc_ref, o_ref):
        acc = _sigmoid(gl_ref[:, 0:D]) * _dot(oa_ref[...], wa_ref[...], _NN)
        acc += _sigmoid(gl_ref[:, D:2 * D]) * _dot(ob_ref[...], wb_ref[...], _NN)
        acc += _sigmoid(gl_ref[:, 2 * D:3 * D]) * _dot(oc_ref[...], wc_ref[...], _NN)
        o_ref[...] = acc.astype(BF16)

    return pl.pallas_call(
        body, grid=(S // tm,),
        in_specs=[_row_spec(tm, 512), _row_spec(tm, 512), _row_spec(tm, D), _row_spec(tm, 3 * D)] + _branch_specs(),
        out_specs=_row_spec(tm, D), out_shape=_sds((S, D), BF16), name="merge_fwd", compiler_params=_PAR)(oa, ob, oc, proj, wbr, wbr, wbr)


def _merge_bwd(dm, oa, ob, oc, proj, wbr, tm=256):
    S = oa.shape[0]

    def body(dm_ref, oa_ref, ob_ref, oc_ref, gl_ref, wa_ref, wb_ref, wc_ref, dya_ref, dyb_ref, dyc_ref, dgl_ref, doa_ref, dob_ref, doc_ref):
        dmv = dm_ref[...]
        for i, (o_ref, w_ref, dy_ref, do_ref) in enumerate(
                ((oa_ref, wa_ref, dya_ref, doa_ref), (ob_ref, wb_ref, dyb_ref, dob_ref), (oc_ref, wc_ref, dyc_ref, doc_ref))):
            gt = _sigmoid(gl_ref[:, i * D:(i + 1) * D])
            wv = w_ref[...]
            yv = _dot(o_ref[...], wv, _NN)
            dy = (dmv * gt).astype(BF16)
            dy_ref[...] = dy
            dgl_ref[:, i * D:(i + 1) * D] = (dmv * yv * gt * (1.0 - gt)).astype(BF16)
            do_ref[...] = _dot(dy, wv, _NT)

    rs = _row_spec
    return pl.pallas_call(
        body, grid=(S // tm,),
        in_specs=[rs(tm, D), rs(tm, 512), rs(tm, 512), rs(tm, D), rs(tm, 3 * D)] + _branch_specs(),
        out_specs=(rs(tm, D), rs(tm, D), rs(tm, D), rs(tm, 3 * D), rs(tm, 512), rs(tm, 512), rs(tm, D)),
        out_shape=(_sds((S, D), BF16), _sds((S, D), BF16), _sds((S, D), BF16), _sds((S, 3 * D), BF16), _sds((S, 512)), _sds((S, 512)), _sds((S, D))),
        name="merge_bwd", compiler_params=_PAR)(dm, oa, ob, oc, proj, wbr, wbr, wbr)


def _rope_tab(posb, invf, tm=256):
    S = posb.shape[0]

    def body(p_ref, f_ref, c_ref, s1_ref, s2_ref):
        ang = p_ref[...] * f_ref[...]
        lane = _iota(ang.shape, 1)
        cs, sn = jnp.cos(ang), jnp.sin(ang)
        c_ref[...] = jnp.where(lane < NOPE, 1.0, cs)
        s1_ref[...] = jnp.where((lane >= 64) & (lane < 80), -sn, 0.0)
        s2_ref[...] = jnp.where((lane >= 80) & (lane < 96), sn, 0.0)

    rs = _row_spec(tm, 128)
    return pl.pallas_call(body, grid=(S // tm,), in_specs=[rs, _vec_spec(128)], out_specs=(rs, rs, rs),
                          out_shape=(_sds((S, 128)),) * 3, name="rope_tab", compiler_params=_PAR)(posb, invf)


def _rope(u, C, S1, S2):
    return u * C + pltpu.roll(u, 112, 1) * S1 + pltpu.roll(u, 16, 1) * S2


def _rope_t(dy, C, S1, S2):
    return dy * C + pltpu.roll(dy * S1, 16, 1) + pltpu.roll(dy * S2, 112, 1)


def _seg_sum(v, mask):
    return jnp.sum(jnp.where(mask, v, 0.0), -1, keepdims=True)


def _mla_latents(pq_ref, pkv_ref, wqb_ref, wkvb_ref, qan_ref, kvan_ref):
    ql, kvl = pq_ref[...], pkv_ref[...]
    ckv, kr = kvl[:, 0:KV_RANK], kvl[:, KV_RANK:KV_RANK + 128]
    rq = lax.rsqrt(jnp.mean(ql * ql, -1, keepdims=True) + EPS)
    rkv = lax.rsqrt(jnp.mean(ckv * ckv, -1, keepdims=True) + EPS)
    nq = (ql * rq * qan_ref[...]).astype(BF16)
    nkv = (ckv * rkv * kvan_ref[...]).astype(BF16)
    kv = jnp.concatenate([_dot(nkv, wkvb_ref[k], _NN) for k in range(4)], 1)
    return ql, ckv, kr, rq, rkv, nq, nkv, _dot(nq, wqb_ref[...], _NN), kv


def _mla_specs(tm):
    full = lambda r, n: pl.BlockSpec((r, n), lambda i: (0, 0))
    return ([pl.BlockSpec((tm, 384), lambda i: (i, OQ // 384)), pl.BlockSpec((tm, 384), lambda i: (i, OKV // 384))],
            [full(Q_RANK, D), pl.BlockSpec((4, KV_RANK, 256), lambda i: (0, 0, 0)), _vec_spec(Q_RANK), _vec_spec(KV_RANK), _vec_spec(128), _vec_spec(128)]
            + [_row_spec(tm, 128)] * 3)


def _mla_prep(proj, wqb, wkvb, qan, kvan, wq, wk, rope, tm=256):
    S = proj.shape[0]

    def body(pq_ref, pkv_ref, wqb_ref, wkvb_ref, qan_ref, kvan_ref, wq_ref, wk_ref, c_ref, s1_ref, s2_ref, qf_ref, kf_ref, ve_ref):
        _, _, kr, _, _, _, _, q, kv = _mla_latents(pq_ref, pkv_ref, wqb_ref, wkvb_ref, qan_ref, kvan_ref)
        C, S1, S2, wqv, wkv = c_ref[...], s1_ref[...], s2_ref[...], wq_ref[...], wk_ref[...]
        lane = _iota((tm, 128), 1)
        mn, mr = lane < 64, (lane >= 64) & (lane < 96)
        rrk = lax.rsqrt(_seg_sum(kr * kr, mr) / ROPE_DIM + EPS)
        ykr = _rope(jnp.where(mr, kr * rrk * wkv, 0.0), C, S1, S2)
        for h in range(N_HEADS):
            sl = slice(h * 128, (h + 1) * 128)
            t = q[:, sl]
            rn = lax.rsqrt(_seg_sum(t * t, mn) / NOPE + EPS)
            rr = lax.rsqrt(_seg_sum(t * t, mr) / ROPE_DIM + EPS)
            qf_ref[:, sl] = _rope(t * jnp.where(mn, rn, jnp.where(mr, rr, 0.0)) * wqv, C, S1, S2).astype(BF16)
            t = kv[:, sl]
            rn = lax.rsqrt(_seg_sum(t * t, mn) / NOPE + EPS)
            kf_ref[:, sl] = (jnp.where(mn, t * rn * wkv, 0.0) + ykr).astype(BF16)
            ve_ref[:, sl] = (jnp.where(mn, pltpu.roll(t, 64, 1), 0.0) if h % 2 == 0 else jnp.where(mn, 0.0, t)).astype(BF16)

    pspecs, wspecs = _mla_specs(tm)
    rs = _row_spec(tm, D)
    return pl.pallas_call(body, grid=(S // tm,), in_specs=pspecs + wspecs, out_specs=(rs, rs, rs),
                          out_shape=(_sds((S, D), BF16),) * 3, name="mla_prep", compiler_params=_PAR)(
        proj, proj, wqb, wkvb, qan, kvan, wq, wk, *rope)


def _mla_prep_bwd(proj, dqf, dkf, dve, wqb, wkvb, qan, kvan, wq, wk, rope, tm=256):
    S = proj.shape[0]

    def body(pq_ref, pkv_ref, wqb_ref, wkvb_ref, qan_ref, kvan_ref, wq_ref, wk_ref, c_ref, s1_ref, s2_ref,
             dqf_ref, dkf_ref, dve_ref, dqs_ref, dkvs_ref, dwqb_ref, dwkvb_ref, st_ref, dq_scr, dkv_scr):
        @pl.when(pl.program_id(0) == 0)
        def _():
            dwqb_ref[...] = jnp.zeros_like(dwqb_ref)
            dwkvb_ref[...] = jnp.zeros_like(dwkvb_ref)
            st_ref[...] = jnp.zeros_like(st_ref)

        ql, ckv, kr, rq, rkv, nq, nkv, q, kv = _mla_latents(pq_ref, pkv_ref, wqb_ref, wkvb_ref, qan_ref, kvan_ref)
        C, S1, S2, wqv, wkv = c_ref[...], s1_ref[...], s2_ref[...], wq_ref[...], wk_ref[...]
        lane = _iota((tm, 128), 1)
        mn, mr = lane < 64, (lane >= 64) & (lane < 96)
        dwq = jnp.zeros((1, 128), F32)
        dwk = jnp.zeros((1, 128), F32)
        dykr = jnp.zeros((tm, 128), F32)
        for h in range(N_HEADS):
            sl = slice(h * 128, (h + 1) * 128)
            t = q[:, sl]
            rn = lax.rsqrt(_seg_sum(t * t, mn) / NOPE + EPS)
            rr = lax.rsqrt(_seg_sum(t * t, mr) / ROPE_DIM + EPS)
            scale = jnp.where(mn, rn, jnp.where(mr, rr, 0.0))
            that = t * scale
            du = _rope_t(dqf_ref[:, sl], C, S1, S2)
            dwq += jnp.sum(du * that, 0, keepdims=True)
            g = du * wqv
            gt = g * that
            dq_scr[:, sl] = scale * (g - that * jnp.where(mn, _seg_sum(gt, mn) / NOPE, _seg_sum(gt, mr) / ROPE_DIM))
            t = kv[:, sl]
            rn = lax.rsqrt(_seg_sum(t * t, mn) / NOPE + EPS)
            that = jnp.where(mn, t * rn, 0.0)
            dkf = dkf_ref[:, sl]
            dykr += jnp.where(mr, dkf, 0.0)
            dkn = jnp.where(mn, dkf, 0.0)
            dwk += jnp.sum(dkn * that, 0, keepdims=True)
            g = dkn * wkv
            dve = dve_ref[:, sl]
            dkv_scr[:, sl] = jnp.where(mn, rn * (g - that * (jnp.sum(g * that, -1, keepdims=True) / NOPE)),
                                       pltpu.roll(dve, 64, 1) if h % 2 == 0 else dve)
        rrk = lax.rsqrt(_seg_sum(kr * kr, mr) / ROPE_DIM + EPS)
        that = jnp.where(mr, kr * rrk, 0.0)
        dukr = jnp.where(mr, _rope_t(dykr, C, S1, S2), 0.0)
        dwk += jnp.sum(dukr * that, 0, keepdims=True)
        g = dukr * wkv
        dkr = rrk * (g - that * (jnp.sum(g * that, -1, keepdims=True) / ROPE_DIM))
        dqv, dkvv = dq_scr[...].astype(BF16), dkv_scr[...].astype(BF16)
        dnq = _dot(dqv, wqb_ref[...], _NT)
        dwqb_ref[...] += _dot(nq, dqv, _TN)
        dnkv = jnp.zeros((tm, KV_RANK), F32)
        for k in range(4):
            dnkv += _dot(dkvv[:, k * 256:(k + 1) * 256], wkvb_ref[k], _NT)
            dwkvb_ref[k] += _dot(nkv, dkvv[:, k * 256:(k + 1) * 256], _TN)
        xhat = ql * rq
        st_ref[0:1, 0:Q_RANK] += jnp.sum(dnq * xhat, 0, keepdims=True)
        g = dnq * qan_ref[...]
        dqs_ref[...] = (rq * (g - xhat * jnp.mean(g * xhat, -1, keepdims=True))).astype(BF16)
        xhat = ckv * rkv
        st_ref[1:2, 0:KV_RANK] += jnp.sum(dnkv * xhat, 0, keepdims=True)
        g = dnkv * kvan_ref[...]
        dkvs_ref[:, 0:KV_RANK] = (rkv * (g - xhat * jnp.mean(g * xhat, -1, keepdims=True))).astype(BF16)
        dkvs_ref[:, KV_RANK:KV_RANK + 128] = dkr.astype(BF16)
        st_ref[2:3, 0:128] += dwq
        st_ref[3:4, 0:128] += dwk

    pspecs, wspecs = _mla_specs(tm)
    rs = _row_spec(tm, D)
    full = lambda r, n: pl.BlockSpec((r, n), lambda i: (0, 0))
    return pl.pallas_call(
        body, grid=(S // tm,), in_specs=pspecs + wspecs + [rs, rs, rs],
        out_specs=(_row_spec(tm, 384), _row_spec(tm, 384), full(Q_RANK, D), pl.BlockSpec((4, KV_RANK, 256), lambda i: (0, 0, 0)), full(8, D)),
        out_shape=(_sds((S, 384), BF16), _sds((S, 384), BF16), _sds((Q_RANK, D)), _sds((4, KV_RANK, 256)), _sds((8, D))),
        scratch_shapes=[pltpu.VMEM((tm, D), F32), pltpu.VMEM((tm, D), F32)], name="mla_prep_bwd", compiler_params=_ARB)(
        proj, proj, wqb, wkvb, qan, kvan, wq, wk, *rope, dqf, dkf, dve)


_ATT_SCALE = (NOPE + ROPE_DIM) ** -0.5


def _att_probs(q, k, i, tq):
    n = k.shape[0]
    s = _dot(q, k, _NT) * _ATT_SCALE
    tri = _iota((tq, tq), 1) <= _iota((tq, tq), 0)
    diag = jnp.where(tri, s[:, n - tq:], -1e30)
    s = diag if n == tq else jnp.concatenate([s[:, :n - tq], diag], 1)
    p = jnp.exp(s - jnp.max(s, -1, keepdims=True))
    return p * (1.0 / jnp.sum(p, -1, keepdims=True))


def _attn_fwd(qf, kf, ve, tq=256):
    S = qf.shape[0]

    def body(q_ref, k_ref, v_ref, o_ref):
        for i in range(S // tq):
            n, rows = (i + 1) * tq, slice(i * tq, (i + 1) * tq)
            acc = jnp.zeros((tq, 128), F32)
            for hh in range(2):
                sl = slice(hh * 128, (hh + 1) * 128)
                p = _att_probs(q_ref[rows, sl], k_ref[0:n, sl], i, tq)
                acc += _dot(p.astype(BF16), v_ref[0:n, sl], _NN)
            o_ref[rows, :] = acc.astype(BF16)

    ps = pl.BlockSpec((S, 256), lambda h: (0, h))
    return pl.pallas_call(body, grid=(N_HEADS // 2,), in_specs=[ps, ps, ps], out_specs=pl.BlockSpec((S, 128), lambda h: (0, h)),
                          out_shape=_sds((S, 512), BF16), name="attn_fwd", compiler_params=_PAR)(qf, kf, ve)


def _attn_bwd(qf, kf, ve, do, tq=256):
    S = qf.shape[0]

    def body(q_ref, k_ref, v_ref, do_ref, dq_ref, dk_ref, dv_ref):
        dk_ref[...] = jnp.zeros_like(dk_ref)
        dv_ref[...] = jnp.zeros_like(dv_ref)
        for i in range(S // tq):
            n, rows = (i + 1) * tq, slice(i * tq, (i + 1) * tq)
            dob = do_ref[rows, :].astype(BF16)
            for hh in range(2):
                sl = slice(hh * 128, (hh + 1) * 128)
                q, k = q_ref[rows, sl], k_ref[0:n, sl]
                p = _att_probs(q, k, i, tq)
                dv_ref[0:n, sl] += _dot(p.astype(BF16), dob, _TN)
                dp = _dot(dob, v_ref[0:n, sl], _NT)
                ds = (p * (dp - jnp.sum(dp * p, -1, keepdims=True)) * _ATT_SCALE).astype(BF16)
                dq_ref[rows, sl] = _dot(ds, k, _NN)
                dk_ref[0:n, sl] += _dot(ds, q, _TN)

    ps = pl.BlockSpec((S, 256), lambda h: (0, h))
    return pl.pallas_call(body, grid=(N_HEADS // 2,), in_specs=[ps, ps, ps, pl.BlockSpec((S, 128), lambda h: (0, h))], out_specs=(ps, ps, ps),
                          out_shape=(_sds((S, D)),) * 3, name="attn_bwd", compiler_params=_PAR)(qf, kf, ve, do)


def _softplus(x):
    return jnp.maximum(x, 0.0) + jnp.log1p(jnp.exp(-jnp.abs(x)))


def _ssd_chunk(xc_ref, dtr_ref, dtb_ref, al_ref, e_ref):
    L = SSD_L
    a = -jnp.exp(al_ref[...])
    dtp = _softplus(dtr_ref[...] + dtb_ref[...])
    causal = _iota((L, L), 1) <= _iota((L, L), 0)
    cs = _dot_hi(causal.astype(F32), dtp * a)
    E = e_ref[...]
    dtx, csx = _dot_hi(dtp, E), _dot_hi(cs, E)
    X = xc_ref[:, 0:SSD_INNER]
    Xd = X * dtx
    dec_out = jnp.exp(csx)
    dec_st = jnp.exp(csx[L - 1:L, :] - csx)
    return a, dtp, causal, cs, cs.T, dtx, X, Xd, dec_out, dec_st


def _ssd_decay(causal, cs, cs_row, h):
    diff = cs[:, h:h + 1] - cs_row[h:h + 1, :]
    return jnp.where(causal, jnp.exp(jnp.where(causal, diff, 0.0)), 0.0)


def _ssd_groups(xc_ref, g):
    b0, c0 = SSD_INNER + g * SSD_N, SSD_INNER + 2 * SSD_N + g * SSD_N
    return xc_ref[:, b0:b0 + SSD_N].astype(BF16), xc_ref[:, c0:c0 + SSD_N].astype(BF16)


def _pair_decay(cs, pair):
    L = SSD_L
    return jnp.where(_iota((128, 128), 0) < 64, jnp.exp(cs[L - 1:L, 2 * pair:2 * pair + 1]), jnp.exp(cs[L - 1:L, 2 * pair + 1:2 * pair + 2]))


def _ssd_in_specs(nc, rev):
    idx = (lambda c: nc - 1 - c) if rev else (lambda c: c)
    return [pl.BlockSpec((SSD_L, SSD_CONV_DIM), lambda c: (idx(c), 0)), pl.BlockSpec((SSD_L, 128), lambda c: (idx(c), ODT // 128)),
            _vec_spec(128), _vec_spec(128), _vec_spec(SSD_INNER), pl.BlockSpec((128, SSD_INNER), lambda c: (0, 0))]


def _ssd_core(xc, proj, dtb, alog, dskip, E):
    S = xc.shape[0]
    nc = S // SSD_L

    def body(xc_ref, dtr_ref, dtb_ref, al_ref, dx_ref, e_ref, y_ref, hp_ref, h_scr):
        @pl.when(pl.program_id(0) == 0)
        def _():
            h_scr[...] = jnp.zeros_like(h_scr)

        hp_ref[0] = h_scr[...]
        _, _, causal, cs, cs_row, _, X, Xd, dec_out, dec_st = _ssd_chunk(xc_ref, dtr_ref, dtb_ref, al_ref, e_ref)
        Xs = Xd * dec_st
        lane = _iota((SSD_L, 128), 1)
        for g in range(2):
            Bg, Cg = _ssd_groups(xc_ref, g)
            CB = _dot(Cg, Bg, _NT)
            for pr in range(4):
                pair = g * 4 + pr
                psl = slice(pair * 128, (pair + 1) * 128)
                Xdp = Xd[:, psl].astype(BF16)
                r0 = _dot((CB * _ssd_decay(causal, cs, cs_row, 2 * pair)).astype(BF16), Xdp, _NN)
                r1 = _dot((CB * _ssd_decay(causal, cs, cs_row, 2 * pair + 1)).astype(BF16), Xdp, _NN)
                Hp = h_scr[psl, :]
                W = _dot(Cg, Hp.astype(BF16), _NT)
                y_ref[:, psl] = jnp.where(lane < 64, r0, r1) + W * dec_out[:, psl] + X[:, psl] * dx_ref[:, psl]
                h_scr[psl, :] = Hp * _pair_decay(cs, pair) + _dot(Xs[:, psl].astype(BF16), Bg, _TN)

    return pl.pallas_call(
        body, grid=(nc,), in_specs=_ssd_in_specs(nc, False),
        out_specs=(pl.BlockSpec((SSD_L, SSD_INNER), lambda c: (c, 0)), pl.BlockSpec((1, SSD_INNER, SSD_N), lambda c: (c, 0, 0))),
        out_shape=(_sds((S, SSD_INNER)), _sds((nc, SSD_INNER, SSD_N))), scratch_shapes=[pltpu.VMEM((SSD_INNER, SSD_N), F32)],
        name="ssd_core", compiler_params=_ARB)(xc, proj, dtb, alog, dskip, E)


def _ssd_core_bwd(xc, proj, hprev, dy, dtb, alog, dskip, E, ET):
    S = xc.shape[0]
    nc = S // SSD_L
    L = SSD_L

    def body(xc_ref, dtr_ref, dtb_ref, al_ref, dx_ref, e_ref, et_ref, hp_ref, dy_ref, dxc_ref, ddt_ref, st_ref, dh_scr, acc_scr):
        step = pl.program_id(0)

        @pl.when(step == 0)
        def _():
            dh_scr[...] = jnp.zeros_like(dh_scr)
            acc_scr[...] = jnp.zeros_like(acc_scr)
            st_ref[...] = jnp.zeros_like(st_ref)

        a, dtp, causal, cs, cs_row, dtx, X, Xd, dec_out, dec_st = _ssd_chunk(xc_ref, dtr_ref, dtb_ref, al_ref, e_ref)
        lane = _iota((L, 128), 1)
        sub = _iota((128, L), 0)
        dcs_col = jnp.zeros((L, 128), F32)
        dcs_row = jnp.zeros((128, L), F32)
        dcs_last = jnp.zeros((1, 128), F32)
        dcsx, ddtx, dlastx = [], [], []
        for g in range(2):
            Bg, Cg = _ssd_groups(xc_ref, g)
            CB = _dot(Cg, Bg, _NT)
            dCB = jnp.zeros((L, L), F32)
            dB = jnp.zeros((L, SSD_N), F32)
            dC = jnp.zeros((L, SSD_N), F32)
            for pr in range(4):
                pair = g * 4 + pr
                psl = slice(pair * 128, (pair + 1) * 128)
                dY, Xp, Xdp, dop, dsp = dy_ref[:, psl], X[:, psl], Xd[:, psl], dec_out[:, psl], dec_st[:, psl]
                Xdb = Xdp.astype(BF16)
                acc_scr[0:1, psl] += jnp.sum(dY * Xp, 0, keepdims=True)
                Hp = hp_ref[0, psl, :]
                Hb = Hp.astype(BF16)
                dW = (dY * dop).astype(BF16)
                dcx = dY * _dot(Cg, Hb, _NT) * dop
                dC += _dot(dW, Hb, _NN)
                dHp = _dot(dW, Cg, _TN)
                dHn = dh_scr[psl, :]
                cd = _pair_decay(cs, pair)
                dh_scr[psl, :] = dHp + dHn * cd
                rsum = jnp.sum(dHn * Hp * cd, -1, keepdims=True)
                half = _iota((128, 1), 0) < 64
                s0 = jnp.sum(jnp.where(half, rsum, 0.0), 0, keepdims=True)
                s1 = jnp.sum(jnp.where(half, 0.0, rsum), 0, keepdims=True)
                lane1 = _iota((1, 128), 1)
                dcs_last += jnp.where(lane1 == 2 * pair, s0, 0.0) + jnp.where(lane1 == 2 * pair + 1, s1, 0.0)
                dHb = dHn.astype(BF16)
                dXs = _dot(Bg, dHb, _NT)
                dB += _dot((Xdp * dsp).astype(BF16), dHb, _NN)
                dXd = dXs * dsp
                e_st = dXs * Xdp * dsp
                dcx -= e_st
                dlastx.append(jnp.sum(e_st, 0, keepdims=True))
                for i in range(2):
                    h = 2 * pair + i
                    Dm = _ssd_decay(causal, cs, cs_row, h)
                    M = CB * Dm
                    dYm = jnp.where((lane < 64) if i == 0 else (lane >= 64), dY, 0.0).astype(BF16)
                    dM = _dot(dYm, Xdb, _NT)
                    dXd += _dot(M.astype(BF16), dYm, _TN)
                    dCB += dM * Dm
                    Em = dM * M
                    dcs_col += jnp.where(lane == h, jnp.sum(Em, -1, keepdims=True), 0.0)
                    dcs_row += jnp.where(sub == h, jnp.sum(Em, 0, keepdims=True), 0.0)
                dxc_ref[:, psl] = dY * dx_ref[:, psl] + dXd * dtx[:, psl]
                ddtx.append(dXd * Xp)
                dcsx.append(dcx)
            dCBb = dCB.astype(BF16)
            b0, c0 = SSD_INNER + g * SSD_N, SSD_INNER + 2 * SSD_N + g * SSD_N
            dxc_ref[:, b0:b0 + SSD_N] = dB + _dot(dCBb, Cg, _TN)
            dxc_ref[:, c0:c0 + SSD_N] = dC + _dot(dCBb, Bg, _NN)
        ET = et_ref[...]
        dcs = dcs_col - dcs_row.T + _dot_hi(jnp.concatenate(dcsx, 1), ET)
        dlast = dcs_last + _dot_hi(jnp.broadcast_to(jnp.concatenate(dlastx, 1), (8, SSD_INNER)), ET)[0:1, :]
        dcs += jnp.where(_iota((L, 128), 0) == L - 1, dlast, 0.0)
        dda = _dot_hi((_iota((L, L), 1) >= _iota((L, L), 0)).astype(F32), dcs)
        ddtp = dda * a + _dot_hi(jnp.concatenate(ddtx, 1), ET)
        draw = ddtp * _sigmoid(dtr_ref[...] + dtb_ref[...])
        ddt_ref[...] = draw.astype(BF16)
        st_ref[0:1, :] += jnp.sum(draw, 0, keepdims=True)
        st_ref[1:2, :] += jnp.sum(dda * dtp, 0, keepdims=True) * a

        @pl.when(step == nc - 1)
        def _():
            st_ref[2:3, :] = _dot_hi(acc_scr[...], ET)[0:1, :]

    rev = lambda c: (nc - 1 - c, 0)
    return pl.pallas_call(
        body, grid=(nc,),
        in_specs=_ssd_in_specs(nc, True) + [pl.BlockSpec((SSD_INNER, 128), lambda c: (0, 0)),
                                            pl.BlockSpec((1, SSD_INNER, SSD_N), lambda c: (nc - 1 - c, 0, 0)),
                                            pl.BlockSpec((L, SSD_INNER), rev)],
        out_specs=(pl.BlockSpec((L, SSD_CONV_DIM), rev), pl.BlockSpec((L, 128), rev), pl.BlockSpec((8, 128), lambda c: (0, 0))),
        out_shape=(_sds((S, SSD_CONV_DIM)), _sds((S, 128), BF16), _sds((8, 128))),
        scratch_shapes=[pltpu.VMEM((SSD_INNER, SSD_N), F32), pltpu.VMEM((8, SSD_INNER), F32)],
        name="ssd_core_bwd", compiler_params=_ARB)(xc, proj, dtb, alog, dskip, E, ET, hprev, dy)


def _ssd_post(y, proj, nw, tm=256):
    S = y.shape[0]

    def body(y_ref, z_ref, nw_ref, o_ref):
        for g in range(2):
            sl = slice(g * 512, (g + 1) * 512)
            gated = y_ref[:, sl] * _silu(z_ref[:, sl])
            r = lax.rsqrt(jnp.mean(gated * gated, -1, keepdims=True) + EPS)
            o_ref[:, sl] = (gated * r * nw_ref[:, sl]).astype(BF16)

    return pl.pallas_call(
        body, grid=(S // tm,), in_specs=[_row_spec(tm, D), pl.BlockSpec((tm, D), lambda i: (i, OZ // D)), _vec_spec(D)],
        out_specs=_row_spec(tm, D), out_shape=_sds((S, D), BF16), name="ssd_post", compiler_params=_PAR)(y, proj, nw)


def _ssd_post_bwd(doc, y, proj, nw, tm=256):
    S = y.shape[0]

    def body(d_ref, y_ref, z_ref, nw_ref, dy_ref, dz_ref, st_ref):
        @pl.when(pl.program_id(0) == 0)
        def _():
            st_ref[...] = jnp.zeros_like(st_ref)

        for g in range(2):
            sl = slice(g * 512, (g + 1) * 512)
            yv, zv, dv = y_ref[:, sl], z_ref[:, sl], d_ref[:, sl]
            sz = _silu(zv)
            gated = yv * sz
            r = lax.rsqrt(jnp.mean(gated * gated, -1, keepdims=True) + EPS)
            ghat = gated * r
            st_ref[0:1, sl] += jnp.sum(dv * ghat, 0, keepdims=True)
            gg = dv * nw_ref[:, sl]
            dg = r * (gg - ghat * jnp.mean(gg * ghat, -1, keepdims=True))
            dy_ref[:, sl] = dg * sz
            dz_ref[:, sl] = (dg * yv * _dsilu(zv)).astype(BF16)

    zs = pl.BlockSpec((tm, D), lambda i: (i, OZ // D))
    return pl.pallas_call(
        body, grid=(S // tm,), in_specs=[_row_spec(tm, D), _row_spec(tm, D), zs, _vec_spec(D)],
        out_specs=(_row_spec(tm, D), _row_spec(tm, D), _vec_spec(D, 8)), out_shape=(_sds((S, D)), _sds((S, D), BF16), _sds((8, D))),
        name="ssd_post_bwd", compiler_params=_ARB)(doc, y, proj, nw)


def _row(v, n=None):
    v = v.astype(F32).reshape(1, -1)
    return v if n is None else jnp.pad(v, ((0, 0), (0, n - v.shape[1])))


_IN_SEGMENTS = [(0, 384, OQ), (384, 640, OKV), (640, 672, OKV + 320), (672, 1184, OP), (1184, 2208, OZ), (2208, 3744, OX),
                (3744, 3760, ODT), (3760, IN_DIM, OG)]
_IN_ZEROS = [(OKV + 256, OKV + 320), (OKV + 352, OKV + 384), (ODT + 16, ODT + 128)]


def _in_pieces():
    w, out = IN_DIM // 4, []
    for a, b, d in _IN_SEGMENTS:
        while a < b:
            k = a // w
            e = min(b, (k + 1) * w)
            out.append((k, a - k * w, e - k * w, d))
            d, a = d + e - a, e
    return out


def _win_layout(w_in4, tm=256):
    def body(w_ref, o_ref):
        for k, s0, s1, d in _in_pieces():
            o_ref[:, d:d + s1 - s0] = w_ref[k, :, s0:s1]
        for z0, z1 in _IN_ZEROS:
            o_ref[:, z0:z1] = jnp.zeros((tm, z1 - z0), o_ref.dtype)

    return pl.pallas_call(
        body, grid=(D // tm,), in_specs=[pl.BlockSpec((4, tm, IN_DIM // 4), lambda i: (0, i, 0))],
        out_specs=pl.BlockSpec((tm, IN_PAD), lambda i: (i, 0)), out_shape=_sds((D, IN_PAD), w_in4.dtype), name="win_layout",
        compiler_params=_PAR)(w_in4)


def _win_unlayout(dwin, tm=256):
    def body(d_ref, o_ref):
        for k, s0, s1, d in _in_pieces():
            o_ref[k, :, s0:s1] = d_ref[:, d:d + s1 - s0]

    return pl.pallas_call(
        body, grid=(D // tm,), in_specs=[pl.BlockSpec((tm, IN_PAD), lambda i: (i, 0))],
        out_specs=pl.BlockSpec((4, tm, IN_DIM // 4), lambda i: (0, i, 0)), out_shape=_sds((4, D, IN_DIM // 4), dwin.dtype),
        name="win_unlayout", compiler_params=_PAR)(dwin)


def _prep_late(p):
    return dict(wbr=p["w_branch"].astype(BF16), wo=p["w_out"].astype(BF16), wup4=p["ffn_up4"].astype(BF16), wdn=p["ffn_down"].astype(BF16))


def _prep_layer(p, late=None):
    win = _win_layout(p["w_in4"].astype(BF16))
    wqb = jnp.transpose(p["w_q_b4"].astype(BF16).reshape(4, Q_RANK, 2, 96), (1, 0, 2, 3))
    return dict(
        win=win, wqb=jnp.pad(wqb, ((0, 0), (0, 0), (0, 0), (0, 32))).reshape(Q_RANK, D), wkvb4=p["w_kv_b4"].astype(BF16),
        late=late if late is not None else (lambda after: _prep_late(p)),
        nw1=_row(p["norm1_w"]), nw2=_row(p["norm2_w"]), qan=_row(p["q_a_norm"]), kvan=_row(p["kv_a_norm"]),
        wq=_row(p["q_norm"], 128), wk=_row(p["k_norm"], 128), pool_w=p["pool_w"].astype(F32), pool_scale=_row(p["pool_scale"]),
        cw=p["ssd_conv_w"].astype(F32), cb=_row(p["ssd_conv_b"]), dtb=_row(p["ssd_dt_bias"], 128), alog=_row(p["ssd_a_log"], 128),
        dskip=_row(jnp.repeat(p["ssd_d"].astype(F32), SSD_P)), snw=_row(p["ssd_norm_w"]),
        fcw=p["ffn_conv_w"].astype(F32), fcb=_row(p["ffn_conv_b"]))


def _layer_fwd(x, mod8, W, rope, E, tag):
    sh1, sc1, g1, sh2, sc2, g2 = (mod8[i:i + 1] for i in range(6))
    h1 = _ln_mod(x, W["nw1"], sc1, sh1, name=f"ln1_{tag}")
    proj = _mm(h1, W["win"], tn=640, tk=1024, name=f"proj_{tag}")
    qf, kf, ve = _mla_prep(proj, W["wqb"], W["wkvb4"], W["qan"], W["kvan"], W["wq"], W["wk"], rope)
    oa = _attn_fwd(qf, kf, ve)
    ob = _pool_fwd(proj, W["pool_w"], W["pool_scale"])
    xc = _ssd_pre(proj, W["cw"], W["cb"])
    y, hprev = _ssd_core(xc, proj, W["dtb"], W["alog"], W["dskip"], E)
    oc = _ssd_post(y, proj, W["snw"])
    W.update(W["late"](oc))
    merged = _merge_fwd(oa, ob, oc, proj, W["wbr"])
    x1, out1 = _mm(merged, W["wo"], tk=1024, res=x, gate=g1, name=f"wout_{tag}")
    h2 = _ln_mod(x1, W["nw2"], sc2, sh2, name=f"ln2_{tag}")
    up = _up_fwd(h2, W["wup4"], name=f"up_{tag}")
    act = _ffn_act(up, W["fcw"], W["fcb"])
    x2, out2 = _mm(act, W["wdn"], res=x1, gate=g2, name=f"down_{tag}")
    saved = dict(x=x, h1=h1, proj=proj, qf=qf, kf=kf, ve=ve, oa=oa, ob=ob, oc=oc, xc=xc, hprev=hprev, y=y, merged=merged,
                 out1=out1, x1=x1, h2=h2, up=up, act=act, out2=out2)
    return x2, saved


def _layer_bwd(dx2, sv, mod8, W, rope, E, ET, tag, emit=None):
    sc1, g1, sc2, g2 = mod8[1:2], mod8[2:3], mod8[4:5], mod8[5:6]
    proj = sv["proj"]
    dz2, dg2 = _gate_bwd(dx2, sv["out2"], g2, name=f"gate2_bwd_{tag}")
    dact = _mm(dz2, W["wdn"], "nt", tn=1408, tk=1024, name=f"down_dx_{tag}")
    dwdn = _mm(sv["act"], dz2, "tn", tm=1408, name=f"down_dw_{tag}")
    dup2, dfcw, dfcb = _ffn_act_bwd(sv["up"], dact, W["fcw"], W["fcb"])
    dh2 = _up_dx(dup2, W["wup4"], name=f"up_dx_{tag}")
    dwup4 = _up_dw(sv["h2"], dup2, name=f"up_dw_{tag}")
    dx1, st2 = _ln_mod_bwd(sv["x1"], dh2, dx2, W["nw2"], sc2, name=f"ln2_bwd_{tag}")
    dz1, dg1 = _gate_bwd(dx1, sv["out1"], g1, name=f"gate1_bwd_{tag}")
    dmerged = _mm(dz1, W["wo"], "nt", tk=1024, name=f"wout_dx_{tag}")
    dwo = _mm(sv["merged"], dz1, "tn", name=f"wout_dw_{tag}")
    dya, dyb, dyc, dgl, doa, dob, doc = _merge_bwd(dmerged, sv["oa"], sv["ob"], sv["oc"], proj, W["wbr"])
    dwba = _mm(sv["oa"], dya, "tn", name=f"wba_dw_{tag}")
    dwbb = _mm(sv["ob"], dyb, "tn", name=f"wbb_dw_{tag}")
    dwbc = _mm(sv["oc"], dyc, "tn", name=f"wbc_dw_{tag}")
    late = dict(w_branch=jnp.concatenate([dwba, dwbb, dwbc], 0).reshape(4, 512, D), w_out=dwo.reshape(4, 256, D), ffn_up=dwup4,
                ffn_down=dwdn.reshape(4, FFN // 4, D))
    snw = W["snw"]
    if emit is not None:
        token = emit(late)
        if token is not None:
            snw, doa = snw + token, doa + token
    dy, dzs, st_post = _ssd_post_bwd(doc, sv["y"], proj, snw)
    dxc, ddt, st_ssd = _ssd_core_bwd(sv["xc"], proj, sv["hprev"], dy, W["dtb"], W["alog"], W["dskip"], E, ET)
    dxbc, dcw, dcb = _ssd_pre_bwd(proj, dxc, W["cw"], W["cb"])
    dpool, dpw, dps = _pool_bwd(proj, dob, W["pool_w"], W["pool_scale"])
    dqf, dkf, dve = _attn_bwd(sv["qf"], sv["kf"], sv["ve"], doa)
    dqs, dkvs, dwqb, dwkvb4, st_mla = _mla_prep_bwd(proj, dqf, dkf, dve, W["wqb"], W["wkvb4"], W["qan"], W["kvan"], W["wq"], W["wk"], rope)
    dproj = jnp.concatenate([dgl, dzs, dxbc, dpool, dqs, dkvs, ddt], 1)
    dh1 = _mm(dproj, W["win"], "nt", tk=IN_PAD, name=f"proj_dx_{tag}")
    dwin = _mm(sv["h1"], dproj, "tn", tn=640, name=f"proj_dw_{tag}")
    dx, st1 = _ln_mod_bwd(sv["x"], dh1, dx1, W["nw1"], sc1, name=f"ln1_bwd_{tag}")
    grads = dict(
        norm1_w=st1[2], norm2_w=st2[2], w_in=_win_unlayout(dwin),
        q_a_norm=st_mla[0, :Q_RANK], kv_a_norm=st_mla[1, :KV_RANK], q_norm=st_mla[2, :96], k_norm=st_mla[3, :96],
        w_q_b=jnp.transpose(dwqb.reshape(Q_RANK, 4, 2, 128)[:, :, :, :96], (1, 0, 2, 3)).reshape(4, Q_RANK, 192), w_kv_b=dwkvb4,
        pool_w=dpw, pool_scale=dps[0], ssd_conv_w=dcw, ssd_conv_b=dcb[0],
        ssd_dt_bias=st_ssd[0, :SSD_HEADS], ssd_a_log=st_ssd[1, :SSD_HEADS], ssd_d=st_ssd[2, :SSD_HEADS], ssd_norm_w=st_post[0],
        ffn_conv_w=jnp.transpose(dfcw, (1, 0, 2)).reshape(3, 2 * FFN), ffn_conv_b=dfcb.reshape(2 * FFN), **late)
    dmod = jnp.concatenate([st1[0:2], dg1[0:1], st2[0:2], dg2[0:1]], 0)
    return dx, grads, dmod


def _ssd_expand():
    E = (jnp.arange(SSD_INNER)[None, :] // SSD_P == jnp.arange(128)[:, None]).astype(F32)
    return E, E.T


def _rope_tables(positions):
    inv_freq = ROPE_THETA ** (-jnp.arange(0, ROPE_DIM, 2, dtype=F32) / ROPE_DIM)
    invf = jnp.concatenate([jnp.zeros((NOPE,), F32), inv_freq, inv_freq, jnp.zeros((32,), F32)]).reshape(1, 128)
    posb = jnp.broadcast_to(positions.astype(F32)[:, None], (positions.shape[0], 128))
    return _rope_tab(posb, invf)


def _local_step(x, target, positions, mods, get_layer, bwd_mod=None, emit=None, done=None):
    rope = _rope_tables(positions)
    E, ET = _ssd_expand()
    Ws, saved, h = [], [], x
    for l in range(2):
        Ws.append(_prep_layer(*get_layer(l, h)))
        h, sv = _layer_fwd(h, mods[l], Ws[l], rope, E, l)
        saved.append(sv)
    dy, lpart = _loss_grad(h, target)
    grads, dmods = [None, None], [None, None]
    for l in (1, 0):
        mod8 = mods[l] if bwd_mod is None else bwd_mod(l)
        dy, grads[l], dmods[l] = _layer_bwd(dy, saved[l], mod8, Ws[l], rope, E, ET, l, None if emit is None else functools.partial(emit, l))
        if done is not None:
            done(l, grads[l])
    return lpart[0, 0], dy, grads, dmods


_ANY = pl.BlockSpec(memory_space=pl.ANY)
_VMEM = pl.BlockSpec(memory_space=pltpu.VMEM)


def _place():
    x, y, c = lax.axis_index("x"), lax.axis_index("y"), lax.axis_index("c")
    return x, y, c, [(1 - x, y), (x, 1 - y), (1 - x, 1 - y)]


def _allgather8(v, name):
    m_per, n = v.shape

    def body(x_ref, out_ref, send_sems, recv_sems, local_sem):
        x, y, c, chips = _place()
        me, sibling = (x, y, c), (x, y, 1 - c)

        def rows(px, py, pc):
            return out_ref.at[pl.ds((4 * px + 2 * py + pc) * m_per, m_per), :]

        def copy(k, block, to, src=None):
            return pltpu.make_async_remote_copy(src_ref=rows(*block) if src is None else src, dst_ref=rows(*block),
                                                send_sem=send_sems.at[k], recv_sem=recv_sems.at[k], device_id=to, device_id_type=MESH)

        mine = pltpu.make_async_copy(x_ref, rows(*me), local_sem)
        mine.start()
        first = [copy(0, me, sibling, src=x_ref)] + [copy(1 + j, me, (*chip, c), src=x_ref) for j, chip in enumerate(chips)]
        for cp in first:
            cp.start()
        passed = [copy(4 + j, (*chip, c), sibling) for j, chip in enumerate(chips)]
        for j, chip in enumerate(chips):
            copy(1 + j, (*chip, c), me).wait_recv()
            passed[j].start()
        copy(0, sibling, me).wait_recv()
        for j, chip in enumerate(chips):
            copy(4 + j, (*chip, 1 - c), me).wait_recv()
        for cp in first + passed:
            cp.wait_send()
        mine.wait()

    return pl.pallas_call(
        body, out_shape=_sds((8 * m_per, n), v.dtype), in_specs=[_VMEM], out_specs=_VMEM,
        scratch_shapes=[pltpu.SemaphoreType.DMA((7,)), pltpu.SemaphoreType.DMA((7,)), pltpu.SemaphoreType.DMA], name=name)(v)


def _sems(n):
    return [pltpu.SemaphoreType.DMA((n,)), pltpu.SemaphoreType.DMA((n,))]


_HBM = pl.BlockSpec(memory_space=pltpu.HBM)
_SEM = pl.BlockSpec(memory_space=pltpu.SEMAPHORE)
_EFFECT = pltpu.CompilerParams(has_side_effects=pltpu.SideEffectType.DATAFLOW_SIDE_EFFECTING)


def _ici_copy(src_refs, land_refs, send_sems, recv_sems, a, j, slices, incoming):
    x, y, c, chips = _place()
    me, other = 2 * x + y, 2 * chips[j][0] + chips[j][1]
    src, dst = slices(src_refs[a], land_refs[a], other, me, c) if incoming else slices(src_refs[a], land_refs[a], me, other, c)
    return pltpu.make_async_remote_copy(src_ref=src, dst_ref=dst, send_sem=send_sems.at[3 * a + j], recv_sem=recv_sems.at[3 * a + j],
                                        device_id=(*chips[j], c), device_id_type=MESH)


def _ici_start(srcs, land_shapes, slices, after, name):
    na = len(srcs)

    def body(*refs):
        src_refs, land_refs, send_sems, recv_sems = refs[:na], refs[na:2 * na], refs[2 * na + 1], refs[2 * na + 2]
        for a in range(na):
            for j in range(3):
                _ici_copy(src_refs, land_refs, send_sems, recv_sems, a, j, slices, False).start()
        refs[-1][...] = jnp.zeros_like(refs[-1])

    hbm = lambda v: pltpu.with_memory_space_constraint(v, pltpu.HBM)
    lands = [hbm(lax.empty(s.shape, s.dtype)) for s in land_shapes]
    return pl.pallas_call(
        body, name=name,
        out_shape=(pltpu.SemaphoreType.DMA((3 * na,)), pltpu.SemaphoreType.DMA((3 * na,)), *[pltpu.HBM(v.shape, v.dtype) for v in srcs],
                   *[pltpu.HBM(s.shape, s.dtype) for s in land_shapes], _sds((8, 128))),
        in_specs=[_HBM] * (2 * na) + [_ANY], out_specs=(_SEM, _SEM, *[_HBM] * (2 * na), _VMEM),
        input_output_aliases={i: 2 + i for i in range(2 * na)}, compiler_params=_EFFECT)(*[hbm(v) for v in srcs], *lands, after)


def _ici_wait(handle, slices, after, name):
    na = (len(handle) - 3) // 2

    def body(*refs):
        src_refs, land_refs, send_sems, recv_sems = refs[:na], refs[na:2 * na], refs[2 * na], refs[2 * na + 1]
        for a in range(na):
            for j in range(3):
                _ici_copy(src_refs, land_refs, send_sems, recv_sems, a, j, slices, False).wait_send()
                _ici_copy(src_refs, land_refs, send_sems, recv_sems, a, j, slices, True).wait_recv()

    thru = handle[2:2 + 2 * na]
    outs = pl.pallas_call(
        body, name=name, out_shape=[pltpu.HBM(v.shape, v.dtype) for v in thru], in_specs=[_HBM] * (2 * na) + [_SEM, _SEM, _ANY],
        out_specs=[_HBM] * (2 * na), input_output_aliases={i: i for i in range(2 * na)}, compiler_params=_EFFECT)(
        *thru, handle[0], handle[1], after)
    return outs[:na], outs[na:]


def _gather_slices(p_ref, land_ref, sender, receiver, c):
    r2 = p_ref.shape[0] // 2
    return p_ref.at[pl.ds(c * r2, r2), :], land_ref.at[sender, pl.ds(c * r2, r2), :]


def _scatter_slices(a_ref, t_ref, sender, receiver, c):
    return a_ref.at[receiver], t_ref.at[sender]


def _gather_start(arrs, after, name):
    return _ici_start(arrs, [_sds((4,) + v.shape, v.dtype) for v in arrs], _gather_slices, after, name)


def _gather_finish(handle, after, name):
    arrs, stacks = _ici_wait(handle, _gather_slices, after, name + "_wait")
    na = len(stacks)

    def body(*refs):
        s_refs, o_refs, (send_sems, recv_sems) = refs[:na], refs[na:2 * na], refs[2 * na:]
        x, y, c, chips = _place()

        def copy(a, j, cc, to):
            r2 = s_refs[a].shape[1] // 2
            at = (2 * chips[j][0] + chips[j][1], pl.ds(cc * r2, r2), slice(None))
            return pltpu.make_async_remote_copy(src_ref=s_refs[a].at[at], dst_ref=o_refs[a].at[at], send_sem=send_sems.at[3 * a + j],
                                                recv_sem=recv_sems.at[3 * a + j], device_id=to, device_id_type=MESH)

        passed = [copy(a, j, c, (x, y, 1 - c)) for a in range(na) for j in range(3)]
        for cp in passed:
            cp.start()
        for a in range(na):
            for j in range(3):
                copy(a, j, 1 - c, (x, y, c)).wait_recv()
        for cp in passed:
            cp.wait_send()

    stacks = pl.pallas_call(
        body, out_shape=[_sds(v.shape, v.dtype) for v in stacks], in_specs=[_ANY] * na, out_specs=[_ANY] * na,
        input_output_aliases={i: i for i in range(na)}, scratch_shapes=_sems(3 * na), name=name + "_pass")(*stacks)
    chip = 2 * lax.axis_index("x") + lax.axis_index("y")
    return [lax.dynamic_update_slice(s, v[None], (chip, 0, 0)) for s, v in zip(stacks, arrs)]


def _send_halves(gs, name):
    na = len(gs)

    def body(*refs):
        g_refs, o_refs, (send_sems, recv_sems) = refs[:na], refs[na:2 * na], refs[2 * na:]
        x, y, c, _ = _place()
        cps = []
        for a in range(na):
            r2 = g_refs[a].shape[1] // 2
            cps.append(pltpu.make_async_remote_copy(src_ref=g_refs[a].at[:, pl.ds((1 - c) * r2, r2), :], dst_ref=o_refs[a],
                                                    send_sem=send_sems.at[a], recv_sem=recv_sems.at[a],
                                                    device_id=(x, y, 1 - c), device_id_type=MESH))
        for cp in cps:
            cp.start()
        for cp in cps:
            cp.wait()

    return pl.pallas_call(
        body, out_shape=[_sds((4, v.shape[1] // 2, v.shape[2]), v.dtype) for v in gs], in_specs=[_ANY] * na, out_specs=[_ANY] * na,
        scratch_shapes=_sems(na), name=name)(*gs)


def _join_halves(fs, name):
    na = len(fs)

    def body(*refs):
        f_refs, o_refs, (send_sems, recv_sems) = refs[:na], refs[na:2 * na], refs[2 * na:]
        x, y, c, _ = _place()

        def copy(a, cc, to):
            r2 = f_refs[a].shape[0]
            return pltpu.make_async_remote_copy(src_ref=f_refs[a], dst_ref=o_refs[a].at[pl.ds(cc * r2, r2), :], send_sem=send_sems.at[a],
                                                recv_sem=recv_sems.at[a], device_id=to, device_id_type=MESH)

        cps = [copy(a, c, (x, y, 1 - c)) for a in range(na)]
        for cp in cps:
            cp.start()
        for a in range(na):
            copy(a, 1 - c, (x, y, c)).wait_recv()
        for cp in cps:
            cp.wait_send()

    outs = pl.pallas_call(
        body, out_shape=[_sds((2 * v.shape[0], v.shape[1]), v.dtype) for v in fs], in_specs=[_ANY] * na, out_specs=[_ANY] * na,
        scratch_shapes=_sems(na), name=name)(*fs)
    ci = lax.axis_index("c")
    return [lax.dynamic_update_slice(o, f, (ci * f.shape[0], 0)) for o, f in zip(outs, fs)]


def _sum_chips(a, t, chip, name):
    _, r2, n = t.shape
    tm = _row_tile(r2)

    def body(k_ref, a_ref, t1_ref, t2_ref, t3_ref, o_ref):
        o_ref[...] = ((a_ref[...].astype(F32) + t1_ref[...].astype(F32)) + t2_ref[...].astype(F32)) + t3_ref[...].astype(F32)

    def slot(j):
        return pl.BlockSpec((None, tm, n), lambda i, k_ref: (lax.rem(k_ref[0] + j, 4), i, 0))

    return pl.pallas_call(
        body, grid_spec=pltpu.PrefetchScalarGridSpec(num_scalar_prefetch=1, grid=(r2 // tm,), in_specs=[slot(0), slot(1), slot(2), slot(3)],
                                                     out_specs=pl.BlockSpec((tm, n), lambda i, k_ref: (i, 0))),
        out_shape=_sds((r2, n)), name=name, compiler_params=_PAR)(chip.reshape(1).astype(jnp.int32), a, t, t, t)


def _add_cast(g, recv, c, name):
    _, r2, n = recv.shape

    def body(c_ref, a_ref, b_ref, o_ref):
        o_ref[...] = (a_ref[...] + b_ref[...]).astype(BF16)

    spec = pl.BlockSpec((None, r2, n), lambda k, c_ref: (k, 0, 0))
    return pl.pallas_call(
        body, grid_spec=pltpu.PrefetchScalarGridSpec(
            num_scalar_prefetch=1, grid=(4,), in_specs=[pl.BlockSpec((None, r2, n), lambda k, c_ref: (k, c_ref[0], 0)), spec], out_specs=spec),
        out_shape=_sds(recv.shape, BF16), name=name, compiler_params=_PAR)(c.reshape(1).astype(jnp.int32), g, recv)


def _sum_lead(t, name, tm=256):
    P, R, n = t.shape
    tm = _row_tile(R, tm)

    def body(t_ref, o_ref):
        acc = t_ref[0].astype(F32)
        for j in range(1, P):
            acc = acc + t_ref[j].astype(F32)
        o_ref[...] = acc

    return pl.pallas_call(body, grid=(R // tm,), in_specs=[pl.BlockSpec((P, tm, n), lambda i: (0, i, 0))],
                          out_specs=pl.BlockSpec((tm, n), lambda i: (i, 0)), out_shape=_sds((R, n)), name=name, compiler_params=_PAR)(t)


def _ada_fwd(c16, ada_w, ada_b_cols, tn=512):
    L, _, n = ada_w.shape

    def body(c_ref, w_ref, b_ref, o_ref):
        o_ref[0] = _dot(_silu(c_ref[...]).astype(BF16), w_ref[0].astype(BF16), _NN) + b_ref[0]

    return pl.pallas_call(
        body, grid=(L, n // tn),
        in_specs=[pl.BlockSpec((16, D), lambda l, j: (0, 0)), pl.BlockSpec((1, D, tn), lambda l, j: (l, 0, j)), pl.BlockSpec((1, 1, tn), lambda l, j: (l, 0, j))],
        out_specs=pl.BlockSpec((1, 16, tn), lambda l, j: (l, 0, j)), out_shape=_sds((L, 16, n)), name="ada_fwd",
        compiler_params=pltpu.CompilerParams(dimension_semantics=("parallel", "parallel")))(c16, ada_w, ada_b_cols)


def _ada_bwd(c16, dmod, tn=512):
    L, _, n = dmod.shape

    def body(c_ref, d_ref, o_ref):
        o_ref[0] = _dot(_silu(c_ref[...]).astype(BF16), d_ref[0].astype(BF16), _TN)

    return pl.pallas_call(
        body, grid=(L, n // tn), in_specs=[pl.BlockSpec((16, D), lambda l, j: (0, 0)), pl.BlockSpec((1, 16, tn), lambda l, j: (l, 0, j))],
        out_specs=pl.BlockSpec((1, D, tn), lambda l, j: (l, 0, j)), out_shape=_sds((L, D, n)), name="ada_bwd",
        compiler_params=pltpu.CompilerParams(dimension_semantics=("parallel", "parallel")))(c16, dmod)


def _adam_math(w, g, m, v):
    mn = ADAM_B1 * m + (1.0 - ADAM_B1) * g
    vn = ADAM_B2 * v + (1.0 - ADAM_B2) * (g * g)
    m_hat = mn / (1.0 - ADAM_B1 ** ADAM_STEP)
    v_hat = vn / (1.0 - ADAM_B2 ** ADAM_STEP)
    return -ADAM_LR * (m_hat / (jnp.sqrt(v_hat) + ADAM_EPS) + ADAM_WD * w), mn, vn


def _adamw(w, g, m, v, name):
    R, n = w.shape
    tm = _row_tile(R)

    def body(w_ref, g_ref, m_ref, v_ref, d_ref, nm_ref, nv_ref):
        d_ref[...], nm_ref[...], nv_ref[...] = _adam_math(w_ref[...], g_ref[...], m_ref[...], v_ref[...])

    spec = pl.BlockSpec((tm, n), lambda i: (i, 0))
    return pl.pallas_call(body, grid=(R // tm,), in_specs=[spec] * 4, out_specs=(spec,) * 3, out_shape=(_sds((R, n)),) * 3,
                          name=name, compiler_params=_PAR)(w, g, m, v)


def _adamw_layers(w, g0, g1, m, v, after, name):
    _, r, n = w.shape
    tm = _row_tile(r)
    nb = r // tm

    def body(w_ref, g0_ref, g1_ref, m_ref, v_ref, after_ref, g_ref, d_ref, nm_ref, nv_ref):
        gv = jnp.where(pl.program_id(0) == 0, g0_ref[...], g1_ref[...])
        g_ref[...] = gv
        d_ref[...], nm_ref[...], nv_ref[...] = _adam_math(w_ref[...], gv, m_ref[...], v_ref[...])

    spec = pl.BlockSpec((None, tm, n), lambda l, i: (l, i, 0))
    g0_spec = pl.BlockSpec((tm, n), lambda l, i: (i * (1 - l) + (nb - 1) * l, 0))
    g1_spec = pl.BlockSpec((tm, n), lambda l, i: (i * l, 0))
    return pl.pallas_call(body, grid=(2, nb), in_specs=[spec, g0_spec, g1_spec, spec, spec, _ANY], out_specs=(spec,) * 4,
                          out_shape=(_sds(w.shape),) * 4, name=name,
                          compiler_params=pltpu.CompilerParams(dimension_semantics=("arbitrary", "arbitrary")))(w, g0, g1, m, v, after)


_W_NAMES = ["ada_w", "ada_b", "norm1_w", "w_in", "q_a_norm", "w_q_b", "kv_a_norm", "w_kv_b", "q_norm", "k_norm", "pool_w",
            "pool_scale", "ssd_conv_w", "ssd_conv_b", "ssd_dt_bias", "ssd_a_log", "ssd_d", "ssd_norm_w", "w_branch", "w_out",
            "norm2_w", "ffn_up", "ffn_conv_w", "ffn_conv_b", "ffn_down"]
_BIG = [("w_in", (D, IN_DIM // 4), 1), ("w_q_b", (Q_RANK, 192), 1), ("w_kv_b", (KV_RANK, 256), 1), ("w_branch", (512, D), 0),
        ("w_out", (256, D), 0), ("ffn_up", (D, 2 * FFN // 4), 1), ("ffn_down", (FFN // 4, D), 0)]

_SMALL = [("norm1_w", (D,)), ("q_a_norm", (Q_RANK,)), ("kv_a_norm", (KV_RANK,)), ("q_norm", (96,)), ("k_norm", (96,)),
          ("pool_w", (4, 128, 128)), ("pool_scale", (512,)), ("ssd_conv_w", (4, SSD_CONV_DIM)), ("ssd_conv_b", (SSD_CONV_DIM,)),
          ("ssd_dt_bias", (SSD_HEADS,)), ("ssd_a_log", (SSD_HEADS,)), ("ssd_d", (SSD_HEADS,)), ("ssd_norm_w", (D,)), ("norm2_w", (D,)),
          ("ffn_conv_w", (3, 2 * FFN)), ("ffn_conv_b", (2 * FFN,))]
_CONV_SHARDED = {"ssd_conv_w": SSD_CONV_DIM // 4, "ffn_conv_w": 2 * FFN // 4}


def _pack_flat(arrs, mult):
    flat = jnp.concatenate([a.astype(F32).reshape(-1) for a in arrs])
    rows = -(-flat.shape[0] // (128 * mult)) * mult
    return jnp.pad(flat, (0, rows * 128 - flat.shape[0])).reshape(rows, 128), [a.shape for a in arrs]


def _unpack_flat(packed, shapes):
    flat, out, off = packed.reshape(-1), [], 0
    for s in shapes:
        n = 1
        for d in s:
            n *= d
        out.append(flat[off:off + n].reshape(s))
        off += n
    return out


_EARLY = ["w_in", "w_q_b", "w_kv_b"]
_LATE = ["w_branch", "w_out", "ffn_up", "ffn_down"]


def _early_weights(a, l, stacks, conv_full):
    p = {n: a[n][l] for n in _W_NAMES if n not in ("ada_w", "ada_b")}
    p.update({n: conv_full[n][l] for n in conv_full})
    p.update(w_in4=stacks[0], w_q_b4=stacks[1], w_kv_b4=stacks[2])
    return p


def _late_weights(stacks):
    return _prep_late(dict(w_branch=stacks[0].reshape(2048, D), w_out=stacks[1].reshape(D, D), ffn_up4=stacks[2],
                           ffn_down=stacks[3].reshape(FFN, D)))


def _reduce_start(gs, ci, after, tag):
    recv = _send_halves(gs, f"rs_halves_{tag}")
    chip_sum = [_add_cast(g, r, ci, f"rs_add_{tag}") for g, r in zip(gs, recv)]
    return _ici_start(chip_sum, [_sds(v.shape, v.dtype) for v in chip_sum], _scatter_slices, after, f"rs_scatter_{tag}_start")


def _reduce_finish(started, after, tag):
    chip_sum, got = _ici_wait(started, _scatter_slices, after, f"rs_scatter_{tag}_wait")
    chip = 2 * lax.axis_index("x") + lax.axis_index("y")
    return _join_halves([_sum_chips(s, t, chip, f"rs_sum_{tag}") for s, t in zip(chip_sum, got)], f"rs_join_{tag}")


def kernel(x, c, positions, ada_w, ada_b, norm1_w, w_in, q_a_norm, w_q_b, kv_a_norm, w_kv_b, q_norm, k_norm, pool_w, pool_scale, ssd_conv_w, ssd_conv_b, ssd_dt_bias, ssd_a_log, ssd_d, ssd_norm_w, w_branch, w_out, norm2_w, ffn_up, ffn_conv_w, ffn_conv_b, ffn_down, loss_target, m_ada_w, m_ada_b, m_norm1_w, m_w_in, m_q_a_norm, m_w_q_b, m_kv_a_norm, m_w_kv_b, m_q_norm, m_k_norm, m_pool_w, m_pool_scale, m_ssd_conv_w, m_ssd_conv_b, m_ssd_dt_bias, m_ssd_a_log, m_ssd_d, m_ssd_norm_w, m_w_branch, m_w_out, m_norm2_w, m_ffn_up, m_ffn_conv_w, m_ffn_conv_b, m_ffn_down, v_ada_w, v_ada_b, v_norm1_w, v_w_in, v_q_a_norm, v_w_q_b, v_kv_a_norm, v_w_kv_b, v_q_norm, v_k_norm, v_pool_w, v_pool_scale, v_ssd_conv_w, v_ssd_conv_b, v_ssd_dt_bias, v_ssd_a_log, v_ssd_d, v_ssd_norm_w, v_w_branch, v_w_out, v_norm2_w, v_ffn_up, v_ffn_conv_w, v_ffn_conv_b, v_ffn_down):
    a = dict(locals())
    xi, yi, ci = lax.axis_index("x"), lax.axis_index("y"), lax.axis_index("c")
    chip = 2 * xi + yi
    dev = 2 * chip + ci
    ncol = 6 * D // 4

    c_all = _allgather8(c.reshape(8, 128), "gather_c").reshape(8, D)
    c16 = jnp.pad(c_all, ((0, 8), (0, 0)))
    ada_b_cols = lax.dynamic_slice_in_dim(ada_b, chip * ncol, ncol, axis=1).reshape(2, 1, ncol)
    mod_part = _ada_fwd(c16, ada_w, ada_b_cols)[:, :8]
    small1, shapes1 = _pack_flat([mod_part, ssd_conv_w, ffn_conv_w], 8)
    got1 = _allgather8(small1, "gather_mod").reshape(8, -1, 128)
    per_chip = [_unpack_flat(got1[2 * k], shapes1) for k in range(4)]
    mod_all = jnp.concatenate([per_chip[k][0] for k in range(4)], -1)
    conv_full = {"ssd_conv_w": jnp.concatenate([per_chip[k][1] for k in range(4)], -1),
                 "ffn_conv_w": jnp.concatenate([per_chip[k][2] for k in range(4)], -1)}
    mod_mine = lax.dynamic_index_in_dim(mod_all, dev, axis=1, keepdims=False).reshape(2, 6, D)
    mods = [jnp.pad(mod_mine[l], ((0, 2), (0, 0))) for l in range(2)]

    big = _EARLY + _LATE
    shard = lambda names, l: [a[n][l].astype(BF16) for n in names]
    g0a = _gather_start(shard(_EARLY, 0), mods[0], "gather_0a")
    g0b = _gather_start(shard(_LATE, 0), g0a[-1], "gather_0b")
    g1 = _gather_start(shard(big, 1), g0b[-1], "gather_1")
    mods[0] = mods[0] + g1[-1][0, 0]

    def get_layer(l, after):
        if l == 0:
            return (_early_weights(a, 0, _gather_finish(g0a, mods[0], "gather_0a"), conv_full),
                    lambda aft: _late_weights(_gather_finish(g0b, aft, "gather_0b")))
        stacks = _gather_finish(g1, after, "gather_1")
        return _early_weights(a, 1, stacks[:3], conv_full), lambda aft: _late_weights(stacks[3:])

    scatters = {}

    def bwd_mod(l):
        return mods[l] if l == 1 else mods[0] + scatters["1"][-1][0, 0]

    def emit(l, late):
        if l == 0:
            scatters["0b"] = _reduce_start([late[n] for n in _LATE], ci, scatters["1"][-1], "0b")
            return scatters["0b"][-1][0, 0]

    def done(l, grads_l):
        if l == 1:
            scatters["1"] = _reduce_start([grads_l[n] for n in big], ci, grads_l["norm1_w"], "1")
        else:
            scatters["0a"] = _reduce_start([grads_l[n] for n in _EARLY], ci, scatters["0b"][-1], "0a")

    lpart, grad_x, grads, dmods = _local_step(x[0], loss_target[0], positions[0], mods, get_layer, bwd_mod, emit, done)
    loss = lax.psum(lpart, ("x", "y", "c"))
    red1 = _reduce_finish(scatters["1"], grad_x, "1")
    red0b = _reduce_finish(scatters["0b"], red1[0], "0b")

    small2, shapes2 = _pack_flat([jnp.stack(dmods)] + [grads[l][n] for l in range(2) for n, _ in _SMALL], 16)
    got2 = _allgather8(small2, "gather_small").reshape(8, -1, 128)
    tot = _unpack_flat(_sum_lead(got2, "sum_small"), shapes2)
    g = {"ada_b": tot[0].reshape(2, 6 * D)}
    for i, (n, _) in enumerate(_SMALL):
        g[n] = jnp.stack([tot[1 + i], tot[1 + len(_SMALL) + i]])
    for n, w in _CONV_SHARDED.items():
        g[n] = lax.dynamic_slice_in_dim(g[n], chip * w, w, axis=2)
    nd = 2 * 6 * D // 128
    dmod_all = jnp.transpose(got2[:, :nd].reshape(8, 2, 6 * D), (1, 0, 2))
    dmod_cols = lax.dynamic_slice_in_dim(jnp.pad(dmod_all, ((0, 0), (0, 8), (0, 0))), chip * ncol, ncol, axis=2)
    g["ada_w"] = _ada_bwd(c16, dmod_cols)

    delta, new_m, new_v = {}, {}, {}
    red1 = dict(zip(big, red1))
    token = scatters["0a"][-1]
    for n, r0 in zip(_LATE, red0b):
        g[n], delta[n], new_m[n], new_v[n] = _adamw_layers(a[n], r0, red1[n], a["m_" + n], a["v_" + n], token, f"adamw_{n}")
    shp = ada_w.shape
    r2 = lambda t: t.reshape(-1, shp[-1])
    delta["ada_w"], new_m["ada_w"], new_v["ada_w"] = (
        o.reshape(shp) for o in _adamw(r2(ada_w), r2(g["ada_w"]), r2(m_ada_w), r2(v_ada_w), "adamw_ada_w"))
    behind = (delta[_LATE[-1]][0, 0, :1] + delta["ada_w"][0, 0, :1]).reshape(1)
    red0a = _reduce_finish(scatters["0a"], behind, "0a")
    for n, r0 in zip(_EARLY, red0a):
        g[n], delta[n], new_m[n], new_v[n] = _adamw_layers(a[n], r0, red1[n], a["m_" + n], a["v_" + n], token, f"adamw_{n}")
    rest = [n for n in _W_NAMES if n not in big and n != "ada_w"]
    packs = [_pack_flat([t[n] if pre is None else t[pre + n] for n in rest], 128)[0]
             for t, pre in ((a, None), (g, None), (a, "m_"), (a, "v_"))]
    rest_shapes = [a[n].shape for n in rest]
    outs = [_unpack_flat(o, rest_shapes) for o in _adamw(*packs, "adamw_rest")]
    for i, n in enumerate(rest):
        delta[n], new_m[n], new_v[n] = outs[0][i], outs[1][i], outs[2][i]

    return (loss, grad_x[None], *[g[n] for n in _W_NAMES], *[delta[n] for n in _W_NAMES],
            *[new_m[n] for n in _W_NAMES], *[new_v[n] for n in _W_NAMES])
```

```python
import functools

import jax
import jax.numpy as jnp
from jax import lax
from jax.experimental import pallas as pl
from jax.experimental.pallas import tpu as pltpu

F32 = jnp.float32
BF16 = jnp.bfloat16
MESH = pl.DeviceIdType.MESH
HI = lax.Precision.HIGHEST

D = 1024
N_HEADS = 8
NOPE, ROPE_DIM = 64, 32
Q_RANK, KV_RANK = 384, 256
POOL_WINDOWS = (2, 4, 8, 16)
SSD_HEADS, SSD_P, SSD_N, SSD_L = 16, 64, 128, 128
SSD_INNER = 1024
SSD_CONV_DIM = 1536
FFN = 2816
EPS = 1e-6
ROPE_THETA = 10000.0
OG, OZ, OX, OP, OQ, OKV, ODT, IN_PAD = 0, 3072, 4096, 5632, 6144, 6528, 6912, 7040
IN_DIM = 6832
ADAM_LR, ADAM_B1, ADAM_B2, ADAM_EPS, ADAM_WD, ADAM_STEP = 0.001, 0.9, 0.999, 1e-08, 0.01, 10

_ARB = pltpu.CompilerParams(dimension_semantics=("arbitrary",))
_PAR = pltpu.CompilerParams(dimension_semantics=("parallel",))


def _pick(n, pref):
    if n <= pref:
        return n
    best = None
    for t in range(128, pref + 1, 128):
        if n % t == 0:
            best = t
    assert best is not None, (n, pref)
    return best


def _row_tile(r, cap=256):
    best = None
    for t in range(16, min(r, cap) + 1, 16):
        if r % t == 0:
            best = t
    assert best is not None, r
    return best


def _sds(shape, dtype=F32):
    return jax.ShapeDtypeStruct(tuple(shape), dtype)


def _iota(shape, dim):
    return lax.broadcasted_iota(jnp.int32, shape, dim)


def _sigmoid(x):
    return 0.5 * jnp.tanh(0.5 * x) + 0.5


def _silu(x):
    return x * _sigmoid(x)


def _dsilu(x):
    s = _sigmoid(x)
    return s * (1.0 + x * (1.0 - s))


def _dot(a, b, dims):
    return lax.dot_general(a, b, (dims, ((), ())), preferred_element_type=F32)


_NN, _NT, _TN = ((1,), (0,)), ((1,), (1,)), ((0,), (0,))


def _dot_hi(a, b, dims=_NN):
    return lax.dot_general(a, b, (dims, ((), ())), preferred_element_type=F32, precision=HI)


def _shift_down(x, j):
    n = x.shape[0]
    return jnp.where(_iota(x.shape, 0) >= j, pltpu.roll(x, j, 0), 0.0)


def _shift_up(x, j):
    n = x.shape[0]
    return jnp.where(_iota(x.shape, 0) < n - j, pltpu.roll(x, n - j, 0), 0.0)


def _mm(a, b, mode="nn", out_dtype=F32, tm=2048, tn=512, tk=4096, res=None, gate=None, name="mm"):
    if mode == "nn":
        (M, K), (K2, N) = a.shape, b.shape
    elif mode == "nt":
        (M, K), (N, K2) = a.shape, b.shape
    else:
        (K, M), (K2, N) = a.shape, b.shape
    assert K == K2, (a.shape, b.shape, mode)
    tm, tn, tk = _pick(M, tm), _pick(N, tn), _pick(K, tk)
    nk = K // tk
    dims = {"nn": _NN, "nt": _NT, "tn": _TN}[mode]
    fused = res is not None

    def body(*refs):
        a_ref, b_ref = refs[:2]

        def finish(acc):
            if fused:
                r_ref, g_ref, o_ref, raw_ref = refs[2:6]
                raw_ref[...] = acc
                o_ref[...] = r_ref[...] + g_ref[...] * acc
            else:
                refs[2][...] = acc.astype(out_dtype)

        _mm_steps(a_ref, b_ref, dims, nk, refs[-1] if nk > 1 else None, finish)

    if mode == "nn":
        a_spec = pl.BlockSpec((tm, tk), lambda i, j, k: (i, k))
        b_spec = pl.BlockSpec((tk, tn), lambda i, j, k: (k, j))
    elif mode == "nt":
        a_spec = pl.BlockSpec((tm, tk), lambda i, j, k: (i, k))
        b_spec = pl.BlockSpec((tn, tk), lambda i, j, k: (j, k))
    else:
        a_spec = pl.BlockSpec((tk, tm), lambda i, j, k: (k, i))
        b_spec = pl.BlockSpec((tk, tn), lambda i, j, k: (k, j))
    o_spec = pl.BlockSpec((tm, tn), lambda i, j, k: (i, j))
    in_specs, args = [a_spec, b_spec], [a, b]
    out_shape, out_specs = _sds((M, N), out_dtype), o_spec
    if fused:
        in_specs += [o_spec, pl.BlockSpec((1, tn), lambda i, j, k: (0, j))]
        args += [res, gate]
        out_shape, out_specs = (_sds((M, N)), _sds((M, N))), (o_spec, o_spec)
    return pl.pallas_call(
        body, grid=(M // tm, N // tn, nk), in_specs=in_specs, out_specs=out_specs, out_shape=out_shape,
        scratch_shapes=[pltpu.VMEM((tm, tn), F32)] if nk > 1 else [], name=name,
        compiler_params=pltpu.CompilerParams(dimension_semantics=("parallel", "parallel", "arbitrary")),
    )(*args)


def _mm_steps(a_ref, b_ref, dims, nk, acc_ref, finish):
    part = _dot(a_ref[...].astype(BF16), b_ref[...].astype(BF16), dims)
    if nk == 1:
        finish(part)
        return
    k = pl.program_id(2)

    @pl.when(k == 0)
    def _():
        acc_ref[...] = part

    @pl.when(k > 0)
    def _():
        acc_ref[...] += part

    @pl.when(k == nk - 1)
    def _():
        finish(acc_ref[...])


def _mm_blocks(a, b, dims, grid, a_spec, b_spec, o_spec, out_shape, acc_shape, name):
    nk = grid[2]

    def body(a_ref, b_ref, o_ref, *scratch):
        def finish(acc):
            o_ref[...] = acc.astype(o_ref.dtype)

        _mm_steps(a_ref, b_ref, dims, nk, scratch[0] if nk > 1 else None, finish)

    return pl.pallas_call(
        body, grid=grid, in_specs=[a_spec, b_spec], out_specs=o_spec, out_shape=out_shape,
        scratch_shapes=[pltpu.VMEM(acc_shape, F32)] if nk > 1 else [], name=name,
        compiler_params=pltpu.CompilerParams(dimension_semantics=("parallel", "parallel", "arbitrary")),
    )(a, b)


_UP_SHARD = 2 * FFN // 4


def _up_fwd(h2, wup4, name, tm=2048):
    S = h2.shape[0]
    tm = min(tm, S)
    return _mm_blocks(h2, wup4, _NN, (S // tm, 4, 1), pl.BlockSpec((tm, D), lambda i, j, k: (i, 0)),
                      pl.BlockSpec((None, D, _UP_SHARD), lambda i, j, k: (j, 0, 0)), pl.BlockSpec((tm, _UP_SHARD), lambda i, j, k: (i, j)),
                      _sds((S, 2 * FFN)), (tm, _UP_SHARD), name)


def _up_dx(dup2, wup4, name, tm=2048, tn=512):
    S = dup2.shape[1]
    tm = min(tm, S)
    return _mm_blocks(dup2, wup4, _NT, (S // tm, D // tn, 4), pl.BlockSpec((None, tm, _UP_SHARD), lambda i, j, k: (lax.div(k, 2), i, lax.rem(k, 2))),
                      pl.BlockSpec((None, tn, _UP_SHARD), lambda i, j, k: (k, j, 0)), pl.BlockSpec((tm, tn), lambda i, j, k: (i, j)),
                      _sds((S, D)), (tm, tn), name)


def _up_dw(h2, dup2, name, tm=1024, tk=4096):
    S = h2.shape[0]
    tk = min(tk, S)
    return _mm_blocks(h2, dup2, _TN, (D // tm, 4, S // tk), pl.BlockSpec((tk, tm), lambda i, j, k: (k, i)),
                      pl.BlockSpec((None, tk, _UP_SHARD), lambda i, j, k: (lax.div(j, 2), k, lax.rem(j, 2))),
                      pl.BlockSpec((None, tm, _UP_SHARD), lambda i, j, k: (j, i, 0)), _sds((4, D, _UP_SHARD)), (tm, _UP_SHARD), name)


def _row_spec(tm, n):
    return pl.BlockSpec((tm, n), lambda i: (i, 0))


def _vec_spec(n, rows=1):
    return pl.BlockSpec((rows, n), lambda i: (0, 0))


def _ln_mod(x, nw, sc, sh, name, tm=256):
    S = x.shape[0]

    def body(x_ref, nw_ref, sc_ref, sh_ref, o_ref):
        xv = x_ref[...]
        r = lax.rsqrt(jnp.mean(xv * xv, -1, keepdims=True) + EPS)
        o_ref[...] = ((xv * r * nw_ref[...]) * (1.0 + sc_ref[...]) + sh_ref[...]).astype(BF16)

    return pl.pallas_call(
        body, grid=(S // tm,), in_specs=[_row_spec(tm, D)] + [_vec_spec(D)] * 3, out_specs=_row_spec(tm, D),
        out_shape=_sds((S, D), BF16), name=name, compiler_params=_PAR)(x, nw, sc, sh)


def _ln_mod_bwd(x, dh, dres, nw, sc, name, tm=256):
    S = x.shape[0]

    def body(x_ref, dh_ref, dres_ref, nw_ref, sc_ref, dx_ref, st_ref):
        @pl.when(pl.program_id(0) == 0)
        def _():
            st_ref[...] = jnp.zeros_like(st_ref)

        xv, dhv, nwv = x_ref[...], dh_ref[...], nw_ref[...]
        r = lax.rsqrt(jnp.mean(xv * xv, -1, keepdims=True) + EPS)
        xhat = xv * r
        dn = dhv * (1.0 + sc_ref[...])
        g = dn * nwv
        dx_ref[...] = dres_ref[...] + r * (g - xhat * jnp.mean(g * xhat, -1, keepdims=True))
        st_ref[0:1, :] += jnp.sum(dhv, 0, keepdims=True)
        st_ref[1:2, :] += jnp.sum(dhv * (xhat * nwv), 0, keepdims=True)
        st_ref[2:3, :] += jnp.sum(dn * xhat, 0, keepdims=True)

    return pl.pallas_call(
        body, grid=(S // tm,), in_specs=[_row_spec(tm, D)] * 3 + [_vec_spec(D)] * 2,
        out_specs=(_row_spec(tm, D), _vec_spec(D, 8)), out_shape=(_sds((S, D)), _sds((8, D))),
        name=name, compiler_params=_ARB)(x, dh, dres, nw, sc)


def _gate_bwd(dx, out, g, name, tm=256):
    S = dx.shape[0]

    def body(dx_ref, o_ref, g_ref, dz_ref, dg_ref):
        @pl.when(pl.program_id(0) == 0)
        def _():
            dg_ref[...] = jnp.zeros_like(dg_ref)

        dxv = dx_ref[...]
        dz_ref[...] = (dxv * g_ref[...]).astype(BF16)
        dg_ref[0:1, :] += jnp.sum(dxv * o_ref[...], 0, keepdims=True)

    return pl.pallas_call(
        body, grid=(S // tm,), in_specs=[_row_spec(tm, D)] * 2 + [_vec_spec(D)],
        out_specs=(_row_spec(tm, D), _vec_spec(D, 8)), out_shape=(_sds((S, D), BF16), _sds((8, D))),
        name=name, compiler_params=_ARB)(dx, out, g)


def _loss_grad(y, t, tm=256):
    S = y.shape[0]

    def body(y_ref, t_ref, dy_ref, l_ref):
        @pl.when(pl.program_id(0) == 0)
        def _():
            l_ref[...] = jnp.zeros_like(l_ref)

        e = y_ref[...] - t_ref[...]
        dy_ref[...] = e * (1.0 / D)
        l_ref[...] += 0.5 * jnp.sum(jnp.mean(e * e, -1, keepdims=True), 0, keepdims=True)

    return pl.pallas_call(
        body, grid=(S // tm,), in_specs=[_row_spec(tm, D)] * 2,
        out_specs=(_row_spec(tm, D), pl.BlockSpec((8, 128), lambda i: (0, 0))),
        out_shape=(_sds((S, D)), _sds((8, 128))), name="loss_grad", compiler_params=_ARB)(y, t)


_CONV_ROWS = 256
_HALO = 8


def _rows(ref, lo, hi):
    S, c = ref.shape
    parts = [jnp.zeros((-lo, c), F32)] if lo < 0 else []
    parts.append(ref[max(lo, 0):min(hi, S), :])
    if hi > S:
        parts.append(jnp.zeros((hi - S, c), F32))
    return parts[0] if len(parts) == 1 else jnp.concatenate(parts, 0)


def _conv_rows(ext, w, b, n):
    K = w.shape[0]
    acc = ext[_HALO:_HALO + n] * w[K - 1:K, :] + b
    for j in range(1, K):
        acc = acc + pltpu.roll(ext, j, 0)[_HALO:_HALO + n] * w[K - 1 - j:K - j, :]
    return acc


def _conv_rows_bwd(ext, w, dc, n):
    K, m = w.shape[0], dc.shape[0]
    d0 = dc[0:n]
    dx = d0 * w[K - 1:K, :]
    dws = [jnp.sum(d0 * ext[_HALO:_HALO + n], 0, keepdims=True)]
    for j in range(1, K):
        dx = dx + pltpu.roll(dc, m - j, 0)[0:n] * w[K - 1 - j:K - j, :]
        dws.append(jnp.sum(d0 * pltpu.roll(ext, j, 0)[_HALO:_HALO + n], 0, keepdims=True))
    return dx, dws[::-1], jnp.sum(d0, 0, keepdims=True)


def _col_spec(S, tc, off=0):
    return pl.BlockSpec((S, tc), lambda j: (0, j + off))


def _ssd_pre(proj, cw, cb, tc=256):
    S, n = proj.shape[0], SSD_CONV_DIM
    R = min(_CONV_ROWS, S)

    def body(x_ref, w_ref, b_ref, o_ref):
        wv, bv = w_ref[...], b_ref[...]
        for r0 in range(0, S, R):
            o_ref[r0:r0 + R, :] = _silu(_conv_rows(_rows(x_ref, r0 - _HALO, r0 + R), wv, bv, R))

    return pl.pallas_call(
        body, grid=(n // tc,),
        in_specs=[_col_spec(S, tc, OX // tc), pl.BlockSpec((4, tc), lambda j: (0, j)), pl.BlockSpec((1, tc), lambda j: (0, j))],
        out_specs=_col_spec(S, tc), out_shape=_sds((S, n)), name="ssd_pre", compiler_params=_PAR)(proj, cw, cb)


def _ssd_pre_bwd(proj, dxc, cw, cb, tc=256):
    S, n = proj.shape[0], SSD_CONV_DIM
    R = min(_CONV_ROWS, S)

    def body(x_ref, d_ref, w_ref, b_ref, dx_ref, dw_ref, db_ref):
        wv, bv = w_ref[...], b_ref[...]
        acc = [jnp.zeros((1, tc), F32)] * 5
        for r0 in range(0, S, R):
            ext = _rows(x_ref, r0 - _HALO, r0 + R + _HALO)
            dc = _rows(d_ref, r0, r0 + R + _HALO) * _dsilu(_conv_rows(ext, wv, bv, R + _HALO))
            dx, dws, db = _conv_rows_bwd(ext, wv, dc, R)
            dx_ref[r0:r0 + R, :] = dx.astype(BF16)
            acc = [s + d for s, d in zip(acc, dws + [db])]
        for k in range(4):
            dw_ref[k:k + 1, :] = acc[k]
        db_ref[...] = acc[4]

    wspec, bspec = pl.BlockSpec((4, tc), lambda j: (0, j)), pl.BlockSpec((1, tc), lambda j: (0, j))
    return pl.pallas_call(
        body, grid=(n // tc,), in_specs=[_col_spec(S, tc, OX // tc), _col_spec(S, tc), wspec, bspec],
        out_specs=(_col_spec(S, tc), wspec, bspec), out_shape=(_sds((S, n), BF16), _sds((4, n)), _sds((1, n))),
        name="ssd_pre_bwd", compiler_params=_PAR)(proj, dxc, cw, cb)


def _ffn_act(up, cw, cb, tc=256):
    S, nb = up.shape[0], FFN // tc
    R = min(_CONV_ROWS, S)

    def body(g_ref, v_ref, wg_ref, wv_ref, bg_ref, bv_ref, o_ref):
        wg, wv, bg, bv = wg_ref[...], wv_ref[...], bg_ref[...], bv_ref[...]
        for r0 in range(0, S, R):
            cg = _conv_rows(_rows(g_ref, r0 - _HALO, r0 + R), wg, bg, R)
            cv = _conv_rows(_rows(v_ref, r0 - _HALO, r0 + R), wv, bv, R)
            o_ref[r0:r0 + R, :] = (_silu(cg) * cv).astype(BF16)

    def wspec(off):
        return pl.BlockSpec((3, tc), lambda j: (0, j + off))

    def bspec(off):
        return pl.BlockSpec((1, tc), lambda j: (0, j + off))

    return pl.pallas_call(
        body, grid=(nb,), in_specs=[_col_spec(S, tc), _col_spec(S, tc, nb), wspec(0), wspec(nb), bspec(0), bspec(nb)],
        out_specs=_col_spec(S, tc), out_shape=_sds((S, FFN), BF16), name="ffn_act", compiler_params=_PAR)(up, up, cw, cw, cb, cb)


def _ffn_act_bwd(up, dact, cw, cb, tc=256):
    S, nb = up.shape[0], FFN // tc
    R = min(_CONV_ROWS, S)

    def body(g_ref, v_ref, d_ref, wg_ref, wv_ref, bg_ref, bv_ref, dx_ref, dw_ref, db_ref):
        wg, wv, bg, bv = wg_ref[...], wv_ref[...], bg_ref[...], bv_ref[...]
        acc = [[jnp.zeros((1, tc), F32)] * 4, [jnp.zeros((1, tc), F32)] * 4]
        for r0 in range(0, S, R):
            eg, ev = _rows(g_ref, r0 - _HALO, r0 + R + _HALO), _rows(v_ref, r0 - _HALO, r0 + R + _HALO)
            da = _rows(d_ref, r0, r0 + R + _HALO)
            cg, cv = _conv_rows(eg, wg, bg, R + _HALO), _conv_rows(ev, wv, bv, R + _HALO)
            sg = _sigmoid(cg)
            for half, (ext, w, dc) in enumerate(((eg, wg, da * cv * (sg * (1.0 + cg * (1.0 - sg)))), (ev, wv, da * (cg * sg)))):
                dx, dws, db = _conv_rows_bwd(ext, w, dc, R)
                dx_ref[half, r0:r0 + R, :] = dx.astype(BF16)
                acc[half] = [s + d for s, d in zip(acc[half], dws + [db])]
        for half in range(2):
            for k in range(3):
                dw_ref[half, k:k + 1, :] = acc[half][k]
            db_ref[half] = acc[half][3]

    def wspec(off):
        return pl.BlockSpec((3, tc), lambda j: (0, j + off))

    def bspec(off):
        return pl.BlockSpec((1, tc), lambda j: (0, j + off))

    cs = _col_spec(S, tc)
    both = lambda r: pl.BlockSpec((2, r, tc), lambda j: (0, 0, j))
    return pl.pallas_call(
        body, grid=(nb,), in_specs=[cs, _col_spec(S, tc, nb), cs, wspec(0), wspec(nb), bspec(0), bspec(nb)],
        out_specs=(both(S), both(3), both(1)), out_shape=(_sds((2, S, FFN), BF16), _sds((2, 3, FFN)), _sds((2, 1, FFN))),
        name="ffn_act_bwd", compiler_params=_PAR)(up, up, dact, cw, cw, cb, cb)


def _window_sum(x, w, up=False):
    shift = _shift_up if up else _shift_down
    j = 1
    while j < w:
        x = x + shift(x, j)
        j *= 2
    return x


def _pool_fwd(proj, pool_w, pool_scale):
    S = proj.shape[0]

    def body(u_ref, w_ref, s_ref, o_ref):
        cnt_row = (_iota((S, 128), 0) + 1).astype(F32)
        for g, w in enumerate(POOL_WINDOWS):
            sl = slice(g * 128, (g + 1) * 128)
            u = u_ref[:, sl]
            pooled = _window_sum(u, w) / jnp.minimum(cnt_row, float(w)) - u
            mixed = _dot(pooled.astype(BF16), w_ref[g].astype(BF16), _NN)
            o_ref[:, sl] = (mixed * s_ref[:, sl]).astype(BF16)

    return pl.pallas_call(
        body, grid=(1,),
        in_specs=[pl.BlockSpec((S, 512), lambda i: (0, OP // 512)), pl.BlockSpec((4, 128, 128), lambda i: (0, 0, 0)), _vec_spec(512)],
        out_specs=pl.BlockSpec((S, 512), lambda i: (0, 0)), out_shape=_sds((S, 512), BF16), name="pool_fwd",
        compiler_params=_ARB)(proj, pool_w, pool_scale)


def _pool_bwd(proj, dob, pool_w, pool_scale):
    S = proj.shape[0]

    def body(u_ref, d_ref, w_ref, s_ref, du_ref, dw_ref, ds_ref):
        cnt_row = (_iota((S, 128), 0) + 1).astype(F32)
        for g, w in enumerate(POOL_WINDOWS):
            sl = slice(g * 128, (g + 1) * 128)
            u, dv, wv = u_ref[:, sl], d_ref[:, sl], w_ref[g].astype(BF16)
            cnt = jnp.minimum(cnt_row, float(w))
            pooled = (_window_sum(u, w) / cnt - u).astype(BF16)
            ds_ref[:, sl] = jnp.sum(dv * _dot(pooled, wv, _NN), 0, keepdims=True)
            dmix = (dv * s_ref[:, sl]).astype(BF16)
            dw_ref[g] = _dot(pooled, dmix, _TN)
            dp = _dot(dmix, wv, _NT)
            du_ref[:, sl] = (_window_sum(dp / cnt, w, up=True) - dp).astype(BF16)

    blk = pl.BlockSpec((S, 512), lambda i: (0, 0))
    wspec = pl.BlockSpec((4, 128, 128), lambda i: (0, 0, 0))
    return pl.pallas_call(
        body, grid=(1,), in_specs=[pl.BlockSpec((S, 512), lambda i: (0, OP // 512)), blk, wspec, _vec_spec(512)],
        out_specs=(blk, wspec, _vec_spec(512)), out_shape=(_sds((S, 512), BF16), _sds((4, 128, 128)), _sds((1, 512))),
        name="pool_bwd", compiler_params=_ARB)(proj, dob, pool_w, pool_scale)


def _branch_specs():
    return [pl.BlockSpec((512, D), lambda i: (0, 0)), pl.BlockSpec((512, D), lambda i: (1, 0)), pl.BlockSpec((1024, D), lambda i: (1, 0))]


def _merge_fwd(oa, ob, oc, proj, wbr, tm=256):
    S = oa.shape[0]

    def body(oa_ref, ob_ref, oc_ref, gl_ref, wa_ref, wb_ref, wc_ref, o_ref):
        acc = _sigmoid(gl_ref[:, 0:D]) * _dot(oa_ref[...], wa_ref[...], _NN)
        acc += _sigmoid(gl_ref[:, D:2 * D]) * _dot(ob_ref[...], wb_ref[...], _NN)
        acc += _sigmoid(gl_ref[:, 2 * D:3 * D]) * _dot(oc_ref[...], wc_ref[...], _NN)
        o_ref[...] = acc.astype(BF16)

    return pl.pallas_call(
        body, grid=(S // tm,),
        in_specs=[_row_spec(tm, 512), _row_spec(tm, 512), _row_spec(tm, D), _row_spec(tm, 3 * D)] + _branch_specs(),
        out_specs=_row_spec(tm, D), out_shape=_sds((S, D), BF16), name="merge_fwd", compiler_params=_PAR)(oa, ob, oc, proj, wbr, wbr, wbr)


def _merge_bwd(dm, oa, ob, oc, proj, wbr, tm=256):
    S = oa.shape[0]

    def body(dm_ref, oa_ref, ob_ref, oc_ref, gl_ref, wa_ref, wb_ref, wc_ref, dya_ref, dyb_ref, dyc_ref, dgl_ref, doa_ref, dob_ref, doc_ref):
        dmv = dm_ref[...]
        for i, (o_ref, w_ref, dy_ref, do_ref) in enumerate(
                ((oa_ref, wa_ref, dya_ref, doa_ref), (ob_ref, wb_ref, dyb_ref, dob_ref), (oc_ref, wc_ref, dyc_ref, doc_ref))):
            gt = _sigmoid(gl_ref[:, i * D:(i + 1) * D])
            wv = w_ref[...]
            yv = _dot(o_ref[...], wv, _NN)
            dy = (dmv * gt).astype(BF16)
            dy_ref[...] = dy
            dgl_ref[:, i * D:(i + 1) * D] = (dmv * yv * gt * (1.0 - gt)).astype(BF16)
            do_ref[...] = _dot(dy, wv, _NT)

    rs = _row_spec
    return pl.pallas_call(
        body, grid=(S // tm,),
        in_specs=[rs(tm, D), rs(tm, 512), rs(tm, 512), rs(tm, D), rs(tm, 3 * D)] + _branch_specs(),
        out_specs=(rs(tm, D), rs(tm, D), rs(tm, D), rs(tm, 3 * D), rs(tm, 512), rs(tm, 512), rs(tm, D)),
        out_shape=(_sds((S, D), BF16), _sds((S, D), BF16), _sds((S, D), BF16), _sds((S, 3 * D), BF16), _sds((S, 512)), _sds((S, 512)), _sds((S, D))),
        name="merge_bwd", compiler_params=_PAR)(dm, oa, ob, oc, proj, wbr, wbr, wbr)


def _rope_tab(posb, invf, tm=256):
    S = posb.shape[0]

    def body(p_ref, f_ref, c_ref, s1_ref, s2_ref):
        ang = p_ref[...] * f_ref[...]
        lane = _iota(ang.shape, 1)
        cs, sn = jnp.cos(ang), jnp.sin(ang)
        c_ref[...] = jnp.where(lane < NOPE, 1.0, cs)
        s1_ref[...] = jnp.where((lane >= 64) & (lane < 80), -sn, 0.0)
        s2_ref[...] = jnp.where((lane >= 80) & (lane < 96), sn, 0.0)

    rs = _row_spec(tm, 128)
    return pl.pallas_call(body, grid=(S // tm,), in_specs=[rs, _vec_spec(128)], out_specs=(rs, rs, rs),
                          out_shape=(_sds((S, 128)),) * 3, name="rope_tab", compiler_params=_PAR)(posb, invf)


def _rope(u, C, S1, S2):
    return u * C + pltpu.roll(u, 112, 1) * S1 + pltpu.roll(u, 16, 1) * S2


def _rope_t(dy, C, S1, S2):
    return dy * C + pltpu.roll(dy * S1, 16, 1) + pltpu.roll(dy * S2, 112, 1)


def _seg_sum(v, mask):
    return jnp.sum(jnp.where(mask, v, 0.0), -1, keepdims=True)


def _mla_latents(pq_ref, pkv_ref, wqb_ref, wkvb_ref, qan_ref, kvan_ref):
    ql, kvl = pq_ref[...], pkv_ref[...]
    ckv, kr = kvl[:, 0:KV_RANK], kvl[:, KV_RANK:KV_RANK + 128]
    rq = lax.rsqrt(jnp.mean(ql * ql, -1, keepdims=True) + EPS)
    rkv = lax.rsqrt(jnp.mean(ckv * ckv, -1, keepdims=True) + EPS)
    nq = (ql * rq * qan_ref[...]).astype(BF16)
    nkv = (ckv * rkv * kvan_ref[...]).astype(BF16)
    kv = jnp.concatenate([_dot(nkv, wkvb_ref[k], _NN) for k in range(4)], 1)
    return ql, ckv, kr, rq, rkv, nq, nkv, _dot(nq, wqb_ref[...], _NN), kv


def _mla_specs(tm):
    full = lambda r, n: pl.BlockSpec((r, n), lambda i: (0, 0))
    return ([pl.BlockSpec((tm, 384), lambda i: (i, OQ // 384)), pl.BlockSpec((tm, 384), lambda i: (i, OKV // 384))],
            [full(Q_RANK, D), pl.BlockSpec((4, KV_RANK, 256), lambda i: (0, 0, 0)), _vec_spec(Q_RANK), _vec_spec(KV_RANK), _vec_spec(128), _vec_spec(128)]
            + [_row_spec(tm, 128)] * 3)


def _mla_prep(proj, wqb, wkvb, qan, kvan, wq, wk, rope, tm=256):
    S = proj.shape[0]

    def body(pq_ref, pkv_ref, wqb_ref, wkvb_ref, qan_ref, kvan_ref, wq_ref, wk_ref, c_ref, s1_ref, s2_ref, qf_ref, kf_ref, ve_ref):
        _, _, kr, _, _, _, _, q, kv = _mla_latents(pq_ref, pkv_ref, wqb_ref, wkvb_ref, qan_ref, kvan_ref)
        C, S1, S2, wqv, wkv = c_ref[...], s1_ref[...], s2_ref[...], wq_ref[...], wk_ref[...]
        lane = _iota((tm, 128), 1)
        mn, mr = lane < 64, (lane >= 64) & (lane < 96)
        rrk = lax.rsqrt(_seg_sum(kr * kr, mr) / ROPE_DIM + EPS)
        ykr = _rope(jnp.where(mr, kr * rrk * wkv, 0.0), C, S1, S2)
        for h in range(N_HEADS):
            sl = slice(h * 128, (h + 1) * 128)
            t = q[:, sl]
            rn = lax.rsqrt(_seg_sum(t * t, mn) / NOPE + EPS)
            rr = lax.rsqrt(_seg_sum(t * t, mr) / ROPE_DIM + EPS)
            qf_ref[:, sl] = _rope(t * jnp.where(mn, rn, jnp.where(mr, rr, 0.0)) * wqv, C, S1, S2).astype(BF16)
            t = kv[:, sl]
            rn = lax.rsqrt(_seg_sum(t * t, mn) / NOPE + EPS)
            kf_ref[:, sl] = (jnp.where(mn, t * rn * wkv, 0.0) + ykr).astype(BF16)
            ve_ref[:, sl] = (jnp.where(mn, pltpu.roll(t, 64, 1), 0.0) if h % 2 == 0 else jnp.where(mn, 0.0, t)).astype(BF16)

    pspecs, wspecs = _mla_specs(tm)
    rs = _row_spec(tm, D)
    return pl.pallas_call(body, grid=(S // tm,), in_specs=pspecs + wspecs, out_specs=(rs, rs, rs),
                          out_shape=(_sds((S, D), BF16),) * 3, name="mla_prep", compiler_params=_PAR)(
        proj, proj, wqb, wkvb, qan, kvan, wq, wk, *rope)


def _mla_prep_bwd(proj, dqf, dkf, dve, wqb, wkvb, qan, kvan, wq, wk, rope, tm=256):
    S = proj.shape[0]

    def body(pq_ref, pkv_ref, wqb_ref, wkvb_ref, qan_ref, kvan_ref, wq_ref, wk_ref, c_ref, s1_ref, s2_ref,
             dqf_ref, dkf_ref, dve_ref, dqs_ref, dkvs_ref, dwqb_ref, dwkvb_ref, st_ref, dq_scr, dkv_scr):
        @pl.when(pl.program_id(0) == 0)
        def _():
            dwqb_ref[...] = jnp.zeros_like(dwqb_ref)
            dwkvb_ref[...] = jnp.zeros_like(dwkvb_ref)
            st_ref[...] = jnp.zeros_like(st_ref)

        ql, ckv, kr, rq, rkv, nq, nkv, q, kv = _mla_latents(pq_ref, pkv_ref, wqb_ref, wkvb_ref, qan_ref, kvan_ref)
        C, S1, S2, wqv, wkv = c_ref[...], s1_ref[...], s2_ref[...], wq_ref[...], wk_ref[...]
        lane = _iota((tm, 128), 1)
        mn, mr = lane < 64, (lane >= 64) & (lane < 96)
        dwq = jnp.zeros((1, 128), F32)
        dwk = jnp.zeros((1, 128), F32)
        dykr = jnp.zeros((tm, 128), F32)
        for h in range(N_HEADS):
            sl = slice(h * 128, (h + 1) * 128)
            t = q[:, sl]
            rn = lax.rsqrt(_seg_sum(t * t, mn) / NOPE + EPS)
            rr = lax.rsqrt(_seg_sum(t * t, mr) / ROPE_DIM + EPS)
            scale = jnp.where(mn, rn, jnp.where(mr, rr, 0.0))
            that = t * scale
            du = _rope_t(dqf_ref[:, sl], C, S1, S2)
            dwq += jnp.sum(du * that, 0, keepdims=True)
            g = du * wqv
            gt = g * that
            dq_scr[:, sl] = scale * (g - that * jnp.where(mn, _seg_sum(gt, mn) / NOPE, _seg_sum(gt, mr) / ROPE_DIM))
            t = kv[:, sl]
            rn = lax.rsqrt(_seg_sum(t * t, mn) / NOPE + EPS)
            that = jnp.where(mn, t * rn, 0.0)
            dkf = dkf_ref[:, sl]
            dykr += jnp.where(mr, dkf, 0.0)
            dkn = jnp.where(mn, dkf, 0.0)
            dwk += jnp.sum(dkn * that, 0, keepdims=True)
            g = dkn * wkv
            dve = dve_ref[:, sl]
            dkv_scr[:, sl] = jnp.where(mn, rn * (g - that * (jnp.sum(g * that, -1, keepdims=True) / NOPE)),
                                       pltpu.roll(dve, 64, 1) if h % 2 == 0 else dve)
        rrk = lax.rsqrt(_seg_sum(kr * kr, mr) / ROPE_DIM + EPS)
        that = jnp.where(mr, kr * rrk, 0.0)
        dukr = jnp.where(mr, _rope_t(dykr, C, S1, S2), 0.0)
        dwk += jnp.sum(dukr * that, 0, keepdims=True)
        g = dukr * wkv
        dkr = rrk * (g - that * (jnp.sum(g * that, -1, keepdims=True) / ROPE_DIM))
        dqv, dkvv = dq_scr[...].astype(BF16), dkv_scr[...].astype(BF16)
        dnq = _dot(dqv, wqb_ref[...], _NT)
        dwqb_ref[...] += _dot(nq, dqv, _TN)
        dnkv = jnp.zeros((tm, KV_RANK), F32)
        for k in range(4):
            dnkv += _dot(dkvv[:, k * 256:(k + 1) * 256], wkvb_ref[k], _NT)
            dwkvb_ref[k] += _dot(nkv, dkvv[:, k * 256:(k + 1) * 256], _TN)
        xhat = ql * rq
        st_ref[0:1, 0:Q_RANK] += jnp.sum(dnq * xhat, 0, keepdims=True)
        g = dnq * qan_ref[...]
        dqs_ref[...] = (rq * (g - xhat * jnp.mean(g * xhat, -1, keepdims=True))).astype(BF16)
        xhat = ckv * rkv
        st_ref[1:2, 0:KV_RANK] += jnp.sum(dnkv * xhat, 0, keepdims=True)
        g = dnkv * kvan_ref[...]
        dkvs_ref[:, 0:KV_RANK] = (rkv * (g - xhat * jnp.mean(g * xhat, -1, keepdims=True))).astype(BF16)
        dkvs_ref[:, KV_RANK:KV_RANK + 128] = dkr.astype(BF16)
        st_ref[2:3, 0:128] += dwq
        st_ref[3:4, 0:128] += dwk

    pspecs, wspecs = _mla_specs(tm)
    rs = _row_spec(tm, D)
    full = lambda r, n: pl.BlockSpec((r, n), lambda i: (0, 0))
    return pl.pallas_call(
        body, grid=(S // tm,), in_specs=pspecs + wspecs + [rs, rs, rs],
        out_specs=(_row_spec(tm, 384), _row_spec(tm, 384), full(Q_RANK, D), pl.BlockSpec((4, KV_RANK, 256), lambda i: (0, 0, 0)), full(8, D)),
        out_shape=(_sds((S, 384), BF16), _sds((S, 384), BF16), _sds((Q_RANK, D)), _sds((4, KV_RANK, 256)), _sds((8, D))),
        scratch_shapes=[pltpu.VMEM((tm, D), F32), pltpu.VMEM((tm, D), F32)], name="mla_prep_bwd", compiler_params=_ARB)(
        proj, proj, wqb, wkvb, qan, kvan, wq, wk, *rope, dqf, dkf, dve)


_ATT_SCALE = (NOPE + ROPE_DIM) ** -0.5


def _att_probs(q, k, i, tq):
    n = k.shape[0]
    s = _dot(q, k, _NT) * _ATT_SCALE
    tri = _iota((tq, tq), 1) <= _iota((tq, tq), 0)
    diag = jnp.where(tri, s[:, n - tq:], -1e30)
    s = diag if n == tq else jnp.concatenate([s[:, :n - tq], diag], 1)
    p = jnp.exp(s - jnp.max(s, -1, keepdims=True))
    return p * (1.0 / jnp.sum(p, -1, keepdims=True))


def _attn_fwd(qf, kf, ve, tq=256):
    S = qf.shape[0]

    def body(q_ref, k_ref, v_ref, o_ref):
        for i in range(S // tq):
            n, rows = (i + 1) * tq, slice(i * tq, (i + 1) * tq)
            acc = jnp.zeros((tq, 128), F32)
            for hh in range(2):
                sl = slice(hh * 128, (hh + 1) * 128)
                p = _att_probs(q_ref[rows, sl], k_ref[0:n, sl], i, tq)
                acc += _dot(p.astype(BF16), v_ref[0:n, sl], _NN)
            o_ref[rows, :] = acc.astype(BF16)

    ps = pl.BlockSpec((S, 256), lambda h: (0, h))
    return pl.pallas_call(body, grid=(N_HEADS // 2,), in_specs=[ps, ps, ps], out_specs=pl.BlockSpec((S, 128), lambda h: (0, h)),
                          out_shape=_sds((S, 512), BF16), name="attn_fwd", compiler_params=_PAR)(qf, kf, ve)


def _attn_bwd(qf, kf, ve, do, tq=256):
    S = qf.shape[0]

    def body(q_ref, k_ref, v_ref, do_ref, dq_ref, dk_ref, dv_ref):
        dk_ref[...] = jnp.zeros_like(dk_ref)
        dv_ref[...] = jnp.zeros_like(dv_ref)
        for i in range(S // tq):
            n, rows = (i + 1) * tq, slice(i * tq, (i + 1) * tq)
            dob = do_ref[rows, :].astype(BF16)
            for hh in range(2):
                sl = slice(hh * 128, (hh + 1) * 128)
                q, k = q_ref[rows, sl], k_ref[0:n, sl]
                p = _att_probs(q, k, i, tq)
                dv_ref[0:n, sl] += _dot(p.astype(BF16), dob, _TN)
                dp = _dot(dob, v_ref[0:n, sl], _NT)
                ds = (p * (dp - jnp.sum(dp * p, -1, keepdims=True)) * _ATT_SCALE).astype(BF16)
                dq_ref[rows, sl] = _dot(ds, k, _NN)
                dk_ref[0:n, sl] += _dot(ds, q, _TN)

    ps = pl.BlockSpec((S, 256), lambda h: (0, h))
    return pl.pallas_call(body, grid=(N_HEADS // 2,), in_specs=[ps, ps, ps, pl.BlockSpec((S, 128), lambda h: (0, h))], out_specs=(ps, ps, ps),
                          out_shape=(_sds((S, D)),) * 3, name="attn_bwd", compiler_params=_PAR)(qf, kf, ve, do)


def _softplus(x):
    return jnp.maximum(x, 0.0) + jnp.log1p(jnp.exp(-jnp.abs(x)))


def _ssd_chunk(xc_ref, dtr_ref, dtb_ref, al_ref, e_ref):
    L = SSD_L
    a = -jnp.exp(al_ref[...])
    dtp = _softplus(dtr_ref[...] + dtb_ref[...])
    causal = _iota((L, L), 1) <= _iota((L, L), 0)
    cs = _dot_hi(causal.astype(F32), dtp * a)
    E = e_ref[...]
    dtx, csx = _dot_hi(dtp, E), _dot_hi(cs, E)
    X = xc_ref[:, 0:SSD_INNER]
    Xd = X * dtx
    dec_out = jnp.exp(csx)
    dec_st = jnp.exp(csx[L - 1:L, :] - csx)
    return a, dtp, causal, cs, cs.T, dtx, X, Xd, dec_out, dec_st


def _ssd_decay(causal, cs, cs_row, h):
    diff = cs[:, h:h + 1] - cs_row[h:h + 1, :]
    return jnp.where(causal, jnp.exp(jnp.where(causal, diff, 0.0)), 0.0)


def _ssd_groups(xc_ref, g):
    b0, c0 = SSD_INNER + g * SSD_N, SSD_INNER + 2 * SSD_N + g * SSD_N
    return xc_ref[:, b0:b0 + SSD_N].astype(BF16), xc_ref[:, c0:c0 + SSD_N].astype(BF16)


def _pair_decay(cs, pair):
    L = SSD_L
    return jnp.where(_iota((128, 128), 0) < 64, jnp.exp(cs[L - 1:L, 2 * pair:2 * pair + 1]), jnp.exp(cs[L - 1:L, 2 * pair + 1:2 * pair + 2]))


def _ssd_in_specs(nc, rev):
    idx = (lambda c: nc - 1 - c) if rev else (lambda c: c)
    return [pl.BlockSpec((SSD_L, SSD_CONV_DIM), lambda c: (idx(c), 0)), pl.BlockSpec((SSD_L, 128), lambda c: (idx(c), ODT // 128)),
            _vec_spec(128), _vec_spec(128), _vec_spec(SSD_INNER), pl.BlockSpec((128, SSD_INNER), lambda c: (0, 0))]


def _ssd_core(xc, proj, dtb, alog, dskip, E):
    S = xc.shape[0]
    nc = S // SSD_L

    def body(xc_ref, dtr_ref, dtb_ref, al_ref, dx_ref, e_ref, y_ref, hp_ref, h_scr):
        @pl.when(pl.program_id(0) == 0)
        def _():
            h_scr[...] = jnp.zeros_like(h_scr)

        hp_ref[0] = h_scr[...]
        _, _, causal, cs, cs_row, _, X, Xd, dec_out, dec_st = _ssd_chunk(xc_ref, dtr_ref, dtb_ref, al_ref, e_ref)
        Xs = Xd * dec_st
        lane = _iota((SSD_L, 128), 1)
        for g in range(2):
            Bg, Cg = _ssd_groups(xc_ref, g)
            CB = _dot(Cg, Bg, _NT)
            for pr in range(4):
                pair = g * 4 + pr
                psl = slice(pair * 128, (pair + 1) * 128)
                Xdp = Xd[:, psl].astype(BF16)
                r0 = _dot((CB * _ssd_decay(causal, cs, cs_row, 2 * pair)).astype(BF16), Xdp, _NN)
                r1 = _dot((CB * _ssd_decay(causal, cs, cs_row, 2 * pair + 1)).astype(BF16), Xdp, _NN)
                Hp = h_scr[psl, :]
                W = _dot(Cg, Hp.astype(BF16), _NT)
                y_ref[:, psl] = jnp.where(lane < 64, r0, r1) + W * dec_out[:, psl] + X[:, psl] * dx_ref[:, psl]
                h_scr[psl, :] = Hp * _pair_decay(cs, pair) + _dot(Xs[:, psl].astype(BF16), Bg, _TN)

    return pl.pallas_call(
        body, grid=(nc,), in_specs=_ssd_in_specs(nc, False),
        out_specs=(pl.BlockSpec((SSD_L, SSD_INNER), lambda c: (c, 0)), pl.BlockSpec((1, SSD_INNER, SSD_N), lambda c: (c, 0, 0))),
        out_shape=(_sds((S, SSD_INNER)), _sds((nc, SSD_INNER, SSD_N))), scratch_shapes=[pltpu.VMEM((SSD_INNER, SSD_N), F32)],
        name="ssd_core", compiler_params=_ARB)(xc, proj, dtb, alog, dskip, E)


def _ssd_core_bwd(xc, proj, hprev, dy, dtb, alog, dskip, E, ET):
    S = xc.shape[0]
    nc = S // SSD_L
    L = SSD_L

    def body(xc_ref, dtr_ref, dtb_ref, al_ref, dx_ref, e_ref, et_ref, hp_ref, dy_ref, dxc_ref, ddt_ref, st_ref, dh_scr, acc_scr):
        step = pl.program_id(0)

        @pl.when(step == 0)
        def _():
            dh_scr[...] = jnp.zeros_like(dh_scr)
            acc_scr[...] = jnp.zeros_like(acc_scr)
            st_ref[...] = jnp.zeros_like(st_ref)

        a, dtp, causal, cs, cs_row, dtx, X, Xd, dec_out, dec_st = _ssd_chunk(xc_ref, dtr_ref, dtb_ref, al_ref, e_ref)
        lane = _iota((L, 128), 1)
        sub = _iota((128, L), 0)
        dcs_col = jnp.zeros((L, 128), F32)
        dcs_row = jnp.zeros((128, L), F32)
        dcs_last = jnp.zeros((1, 128), F32)
        dcsx, ddtx, dlastx = [], [], []
        for g in range(2):
            Bg, Cg = _ssd_groups(xc_ref, g)
            CB = _dot(Cg, Bg, _NT)
            dCB = jnp.zeros((L, L), F32)
            dB = jnp.zeros((L, SSD_N), F32)
            dC = jnp.zeros((L, SSD_N), F32)
            for pr in range(4):
                pair = g * 4 + pr
                psl = slice(pair * 128, (pair + 1) * 128)
                dY, Xp, Xdp, dop, dsp = dy_ref[:, psl], X[:, psl], Xd[:, psl], dec_out[:, psl], dec_st[:, psl]
                Xdb = Xdp.astype(BF16)
                acc_scr[0:1, psl] += jnp.sum(dY * Xp, 0, keepdims=True)
                Hp = hp_ref[0, psl, :]
                Hb = Hp.astype(BF16)
                dW = (dY * dop).astype(BF16)
                dcx = dY * _dot(Cg, Hb, _NT) * dop
                dC += _dot(dW, Hb, _NN)
                dHp = _dot(dW, Cg, _TN)
                dHn = dh_scr[psl, :]
                cd = _pair_decay(cs, pair)
                dh_scr[psl, :] = dHp + dHn * cd
                rsum = jnp.sum(dHn * Hp * cd, -1, keepdims=True)
                half = _iota((128, 1), 0) < 64
                s0 = jnp.sum(jnp.where(half, rsum, 0.0), 0, keepdims=True)
                s1 = jnp.sum(jnp.where(half, 0.0, rsum), 0, keepdims=True)
                lane1 = _iota((1, 128), 1)
                dcs_last += jnp.where(lane1 == 2 * pair, s0, 0.0) + jnp.where(lane1 == 2 * pair + 1, s1, 0.0)
                dHb = dHn.astype(BF16)
                dXs = _dot(Bg, dHb, _NT)
                dB += _dot((Xdp * dsp).astype(BF16), dHb, _NN)
                dXd = dXs * dsp
                e_st = dXs * Xdp * dsp
                dcx -= e_st
                dlastx.append(jnp.sum(e_st, 0, keepdims=True))
                for i in range(2):
                    h = 2 * pair + i
                    Dm = _ssd_decay(causal, cs, cs_row, h)
                    M = CB * Dm
                    dYm = jnp.where((lane < 64) if i == 0 else (lane >= 64), dY, 0.0).astype(BF16)
                    dM = _dot(dYm, Xdb, _NT)
                    dXd += _dot(M.astype(BF16), dYm, _TN)
                    dCB += dM * Dm
                    Em = dM * M
                    dcs_col += jnp.where(lane == h, jnp.sum(Em, -1, keepdims=True), 0.0)
                    dcs_row += jnp.where(sub == h, jnp.sum(Em, 0, keepdims=True), 0.0)
                dxc_ref[:, psl] = dY * dx_ref[:, psl] + dXd * dtx[:, psl]
                ddtx.append(dXd * Xp)
                dcsx.append(dcx)
            dCBb = dCB.astype(BF16)
            b0, c0 = SSD_INNER + g * SSD_N, SSD_INNER + 2 * SSD_N + g * SSD_N
            dxc_ref[:, b0:b0 + SSD_N] = dB + _dot(dCBb, Cg, _TN)
            dxc_ref[:, c0:c0 + SSD_N] = dC + _dot(dCBb, Bg, _NN)
        ET = et_ref[...]
        dcs = dcs_col - dcs_row.T + _dot_hi(jnp.concatenate(dcsx, 1), ET)
        dlast = dcs_last + _dot_hi(jnp.broadcast_to(jnp.concatenate(dlastx, 1), (8, SSD_INNER)), ET)[0:1, :]
        dcs += jnp.where(_iota((L, 128), 0) == L - 1, dlast, 0.0)
        dda = _dot_hi((_iota((L, L), 1) >= _iota((L, L), 0)).astype(F32), dcs)
        ddtp = dda * a + _dot_hi(jnp.concatenate(ddtx, 1), ET)
        draw = ddtp * _sigmoid(dtr_ref[...] + dtb_ref[...])
        ddt_ref[...] = draw.astype(BF16)
        st_ref[0:1, :] += jnp.sum(draw, 0, keepdims=True)
        st_ref[1:2, :] += jnp.sum(dda * dtp, 0, keepdims=True) * a

        @pl.when(step == nc - 1)
        def _():
            st_ref[2:3, :] = _dot_hi(acc_scr[...], ET)[0:1, :]

    rev = lambda c: (nc - 1 - c, 0)
    return pl.pallas_call(
        body, grid=(nc,),
        in_specs=_ssd_in_specs(nc, True) + [pl.BlockSpec((SSD_INNER, 128), lambda c: (0, 0)),
                                            pl.BlockSpec((1, SSD_INNER, SSD_N), lambda c: (nc - 1 - c, 0, 0)),
                                            pl.BlockSpec((L, SSD_INNER), rev)],
        out_specs=(pl.BlockSpec((L, SSD_CONV_DIM), rev), pl.BlockSpec((L, 128), rev), pl.BlockSpec((8, 128), lambda c: (0, 0))),
        out_shape=(_sds((S, SSD_CONV_DIM)), _sds((S, 128), BF16), _sds((8, 128))),
        scratch_shapes=[pltpu.VMEM((SSD_INNER, SSD_N), F32), pltpu.VMEM((8, SSD_INNER), F32)],
        name="ssd_core_bwd", compiler_params=_ARB)(xc, proj, dtb, alog, dskip, E, ET, hprev, dy)


def _ssd_post(y, proj, nw, tm=256):
    S = y.shape[0]

    def body(y_ref, z_ref, nw_ref, o_ref):
        for g in range(2):
            sl = slice(g * 512, (g + 1) * 512)
            gated = y_ref[:, sl] * _silu(z_ref[:, sl])
            r = lax.rsqrt(jnp.mean(gated * gated, -1, keepdims=True) + EPS)
            o_ref[:, sl] = (gated * r * nw_ref[:, sl]).astype(BF16)

    return pl.pallas_call(
        body, grid=(S // tm,), in_specs=[_row_spec(tm, D), pl.BlockSpec((tm, D), lambda i: (i, OZ // D)), _vec_spec(D)],
        out_specs=_row_spec(tm, D), out_shape=_sds((S, D), BF16), name="ssd_post", compiler_params=_PAR)(y, proj, nw)


def _ssd_post_bwd(doc, y, proj, nw, tm=256):
    S = y.shape[0]

    def body(d_ref, y_ref, z_ref, nw_ref, dy_ref, dz_ref, st_ref):
        @pl.when(pl.program_id(0) == 0)
        def _():
            st_ref[...] = jnp.zeros_like(st_ref)

        for g in range(2):
            sl = slice(g * 512, (g + 1) * 512)
            yv, zv, dv = y_ref[:, sl], z_ref[:, sl], d_ref[:, sl]
            sz = _silu(zv)
            gated = yv * sz
            r = lax.rsqrt(jnp.mean(gated * gated, -1, keepdims=True) + EPS)
            ghat = gated * r
            st_ref[0:1, sl] += jnp.sum(dv * ghat, 0, keepdims=True)
            gg = dv * nw_ref[:, sl]
            dg = r * (gg - ghat * jnp.mean(gg * ghat, -1, keepdims=True))
            dy_ref[:, sl] = dg * sz
            dz_ref[:, sl] = (dg * yv * _dsilu(zv)).astype(BF16)

    zs = pl.BlockSpec((tm, D), lambda i: (i, OZ // D))
    return pl.pallas_call(
        body, grid=(S // tm,), in_specs=[_row_spec(tm, D), _row_spec(tm, D), zs, _vec_spec(D)],
        out_specs=(_row_spec(tm, D), _row_spec(tm, D), _vec_spec(D, 8)), out_shape=(_sds((S, D)), _sds((S, D), BF16), _sds((8, D))),
        name="ssd_post_bwd", compiler_params=_ARB)(doc, y, proj, nw)


def _row(v, n=None):
    v = v.astype(F32).reshape(1, -1)
    return v if n is None else jnp.pad(v, ((0, 0), (0, n - v.shape[1])))


_IN_SEGMENTS = [(0, 384, OQ), (384, 640, OKV), (640, 672, OKV + 320), (672, 1184, OP), (1184, 2208, OZ), (2208, 3744, OX),
                (3744, 3760, ODT), (3760, IN_DIM, OG)]
_IN_ZEROS = [(OKV + 256, OKV + 320), (OKV + 352, OKV + 384), (ODT + 16, ODT + 128)]


def _in_pieces():
    w, out = IN_DIM // 4, []
    for a, b, d in _IN_SEGMENTS:
        while a < b:
            k = a // w
            e = min(b, (k + 1) * w)
            out.append((k, a - k * w, e - k * w, d))
            d, a = d + e - a, e
    return out


def _win_layout(w_in4, tm=256):
    def body(w_ref, o_ref):
        for k, s0, s1, d in _in_pieces():
            o_ref[:, d:d + s1 - s0] = w_ref[k, :, s0:s1]
        for z0, z1 in _IN_ZEROS:
            o_ref[:, z0:z1] = jnp.zeros((tm, z1 - z0), o_ref.dtype)

    return pl.pallas_call(
        body, grid=(D // tm,), in_specs=[pl.BlockSpec((4, tm, IN_DIM // 4), lambda i: (0, i, 0))],
        out_specs=pl.BlockSpec((tm, IN_PAD), lambda i: (i, 0)), out_shape=_sds((D, IN_PAD), w_in4.dtype), name="win_layout",
        compiler_params=_PAR)(w_in4)


def _win_unlayout(dwin, tm=256):
    def body(d_ref, o_ref):
        for k, s0, s1, d in _in_pieces():
            o_ref[k, :, s0:s1] = d_ref[:, d:d + s1 - s0]

    return pl.pallas_call(
        body, grid=(D // tm,), in_specs=[pl.BlockSpec((tm, IN_PAD), lambda i: (i, 0))],
        out_specs=pl.BlockSpec((4, tm, IN_DIM // 4), lambda i: (0, i, 0)), out_shape=_sds((4, D, IN_DIM // 4), dwin.dtype),
        name="win_unlayout", compiler_params=_PAR)(dwin)


def _prep_late(p):
    return dict(wbr=p["w_branch"].astype(BF16), wo=p["w_out"].astype(BF16), wup4=p["ffn_up4"].astype(BF16), wdn=p["ffn_down"].astype(BF16))


def _prep_layer(p, late=None):
    win = _win_layout(p["w_in4"].astype(BF16))
    wqb = jnp.transpose(p["w_q_b4"].astype(BF16).reshape(4, Q_RANK, 2, 96), (1, 0, 2, 3))
    return dict(
        win=win, wqb=jnp.pad(wqb, ((0, 0), (0, 0), (0, 0), (0, 32))).reshape(Q_RANK, D), wkvb4=p["w_kv_b4"].astype(BF16),
        late=late if late is not None else (lambda after: _prep_late(p)),
        nw1=_row(p["norm1_w"]), nw2=_row(p["norm2_w"]), qan=_row(p["q_a_norm"]), kvan=_row(p["kv_a_norm"]),
        wq=_row(p["q_norm"], 128), wk=_row(p["k_norm"], 128), pool_w=p["pool_w"].astype(F32), pool_scale=_row(p["pool_scale"]),
        cw=p["ssd_conv_w"].astype(F32), cb=_row(p["ssd_conv_b"]), dtb=_row(p["ssd_dt_bias"], 128), alog=_row(p["ssd_a_log"], 128),
        dskip=_row(jnp.repeat(p["ssd_d"].astype(F32), SSD_P)), snw=_row(p["ssd_norm_w"]),
        fcw=p["ffn_conv_w"].astype(F32), fcb=_row(p["ffn_conv_b"]))


def _layer_fwd(x, mod8, W, rope, E, tag):
    sh1, sc1, g1, sh2, sc2, g2 = (mod8[i:i + 1] for i in range(6))
    h1 = _ln_mod(x, W["nw1"], sc1, sh1, name=f"ln1_{tag}")
    proj = _mm(h1, W["win"], tn=640, tk=1024, name=f"proj_{tag}")
    qf, kf, ve = _mla_prep(proj, W["wqb"], W["wkvb4"], W["qan"], W["kvan"], W["wq"], W["wk"], rope)
    oa = _attn_fwd(qf, kf, ve)
    ob = _pool_fwd(proj, W["pool_w"], W["pool_scale"])
    xc = _ssd_pre(proj, W["cw"], W["cb"])
    y, hprev = _ssd_core(xc, proj, W["dtb"], W["alog"], W["dskip"], E)
    oc = _ssd_post(y, proj, W["snw"])
    W.update(W["late"](oc))
    merged = _merge_fwd(oa, ob, oc, proj, W["wbr"])
    x1, out1 = _mm(merged, W["wo"], tk=1024, res=x, gate=g1, name=f"wout_{tag}")
    h2 = _ln_mod(x1, W["nw2"], sc2, sh2, name=f"ln2_{tag}")
    up = _up_fwd(h2, W["wup4"], name=f"up_{tag}")
    act = _ffn_act(up, W["fcw"], W["fcb"])
    x2, out2 = _mm(act, W["wdn"], tm=1024, res=x1, gate=g2, name=f"down_{tag}")
    saved = dict(x=x, h1=h1, proj=proj, qf=qf, kf=kf, ve=ve, oa=oa, ob=ob, oc=oc, xc=xc, hprev=hprev, y=y, merged=merged,
                 out1=out1, x1=x1, h2=h2, up=up, act=act, out2=out2)
    return x2, saved


def _layer_bwd(dx2, sv, mod8, W, rope, E, ET, tag, emit=None):
    sc1, g1, sc2, g2 = mod8[1:2], mod8[2:3], mod8[4:5], mod8[5:6]
    proj = sv["proj"]
    dz2, dg2 = _gate_bwd(dx2, sv["out2"], g2, name=f"gate2_bwd_{tag}")
    dact = _mm(dz2, W["wdn"], "nt", tn=1408, tk=1024, name=f"down_dx_{tag}")
    dwdn = _mm(sv["act"], dz2, "tn", tm=1408, name=f"down_dw_{tag}")
    dup2, dfcw, dfcb = _ffn_act_bwd(sv["up"], dact, W["fcw"], W["fcb"])
    dh2 = _up_dx(dup2, W["wup4"], name=f"up_dx_{tag}")
    dwup4 = _up_dw(sv["h2"], dup2, name=f"up_dw_{tag}")
    dx1, st2 = _ln_mod_bwd(sv["x1"], dh2, dx2, W["nw2"], sc2, name=f"ln2_bwd_{tag}")
    dz1, dg1 = _gate_bwd(dx1, sv["out1"], g1, name=f"gate1_bwd_{tag}")
    dmerged = _mm(dz1, W["wo"], "nt", tk=1024, name=f"wout_dx_{tag}")
    dwo = _mm(sv["merged"], dz1, "tn", name=f"wout_dw_{tag}")
    dya, dyb, dyc, dgl, doa, dob, doc = _merge_bwd(dmerged, sv["oa"], sv["ob"], sv["oc"], proj, W["wbr"])
    dwba = _mm(sv["oa"], dya, "tn", name=f"wba_dw_{tag}")
    dwbb = _mm(sv["ob"], dyb, "tn", name=f"wbb_dw_{tag}")
    dwbc = _mm(sv["oc"], dyc, "tn", name=f"wbc_dw_{tag}")
    late = dict(w_branch=jnp.concatenate([dwba, dwbb, dwbc], 0).reshape(4, 512, D), w_out=dwo.reshape(4, 256, D), ffn_up=dwup4,
                ffn_down=dwdn.reshape(4, FFN // 4, D))
    snw = W["snw"]
    if emit is not None:
        token = emit(late)
        if token is not None:
            snw, doa = snw + token, doa + token
    dy, dzs, st_post = _ssd_post_bwd(doc, sv["y"], proj, snw)
    dxc, ddt, st_ssd = _ssd_core_bwd(sv["xc"], proj, sv["hprev"], dy, W["dtb"], W["alog"], W["dskip"], E, ET)
    dxbc, dcw, dcb = _ssd_pre_bwd(proj, dxc, W["cw"], W["cb"])
    dpool, dpw, dps = _pool_bwd(proj, dob, W["pool_w"], W["pool_scale"])
    dqf, dkf, dve = _attn_bwd(sv["qf"], sv["kf"], sv["ve"], doa)
    dqs, dkvs, dwqb, dwkvb4, st_mla = _mla_prep_bwd(proj, dqf, dkf, dve, W["wqb"], W["wkvb4"], W["qan"], W["kvan"], W["wq"], W["wk"], rope)
    dproj = jnp.concatenate([dgl, dzs, dxbc, dpool, dqs, dkvs, ddt], 1)
    dh1 = _mm(dproj, W["win"], "nt", tm=512, tk=IN_PAD, name=f"proj_dx_{tag}")
    dwin = _mm(sv["h1"], dproj, "tn", tn=640, name=f"proj_dw_{tag}")
    dx, st1 = _ln_mod_bwd(sv["x"], dh1, dx1, W["nw1"], sc1, name=f"ln1_bwd_{tag}")
    grads = dict(
        norm1_w=st1[2], norm2_w=st2[2], w_in=_win_unlayout(dwin),
        q_a_norm=st_mla[0, :Q_RANK], kv_a_norm=st_mla[1, :KV_RANK], q_norm=st_mla[2, :96], k_norm=st_mla[3, :96],
        w_q_b=jnp.transpose(dwqb.reshape(Q_RANK, 4, 2, 128)[:, :, :, :96], (1, 0, 2, 3)).reshape(4, Q_RANK, 192), w_kv_b=dwkvb4,
        pool_w=dpw, pool_scale=dps[0], ssd_conv_w=dcw, ssd_conv_b=dcb[0],
        ssd_dt_bias=st_ssd[0, :SSD_HEADS], ssd_a_log=st_ssd[1, :SSD_HEADS], ssd_d=st_ssd[2, :SSD_HEADS], ssd_norm_w=st_post[0],
        ffn_conv_w=jnp.transpose(dfcw, (1, 0, 2)).reshape(3, 2 * FFN), ffn_conv_b=dfcb.reshape(2 * FFN), **late)
    dmod = jnp.concatenate([st1[0:2], dg1[0:1], st2[0:2], dg2[0:1]], 0)
    return dx, grads, dmod


def _ssd_expand():
    E = (jnp.arange(SSD_INNER)[None, :] // SSD_P == jnp.arange(128)[:, None]).astype(F32)
    return E, E.T


def _rope_tables(positions):
    inv_freq = ROPE_THETA ** (-jnp.arange(0, ROPE_DIM, 2, dtype=F32) / ROPE_DIM)
    invf = jnp.concatenate([jnp.zeros((NOPE,), F32), inv_freq, inv_freq, jnp.zeros((32,), F32)]).reshape(1, 128)
    posb = jnp.broadcast_to(positions.astype(F32)[:, None], (positions.shape[0], 128))
    return _rope_tab(posb, invf)


def _local_step(x, target, positions, mods, get_layer, bwd_mod=None, emit=None, done=None):
    rope = _rope_tables(positions)
    E, ET = _ssd_expand()
    Ws, saved, h = [], [], x
    for l in range(2):
        Ws.append(_prep_layer(*get_layer(l, h)))
        h, sv = _layer_fwd(h, mods[l], Ws[l], rope, E, l)
        saved.append(sv)
    dy, lpart = _loss_grad(h, target)
    grads, dmods = [None, None], [None, None]
    for l in (1, 0):
        mod8 = mods[l] if bwd_mod is None else bwd_mod(l)
        dy, grads[l], dmods[l] = _layer_bwd(dy, saved[l], mod8, Ws[l], rope, E, ET, l, None if emit is None else functools.partial(emit, l))
        if done is not None:
            done(l, grads[l])
    return lpart[0, 0], dy, grads, dmods


_ANY = pl.BlockSpec(memory_space=pl.ANY)
_VMEM = pl.BlockSpec(memory_space=pltpu.VMEM)


def _place():
    x, y, c = lax.axis_index("x"), lax.axis_index("y"), lax.axis_index("c")
    return x, y, c, [(1 - x, y), (x, 1 - y), (1 - x, 1 - y)]


def _allgather8(v, name):
    m_per, n = v.shape

    def body(x_ref, out_ref, send_sems, recv_sems, local_sem):
        x, y, c, chips = _place()
        me, sibling = (x, y, c), (x, y, 1 - c)

        def rows(px, py, pc):
            return out_ref.at[pl.ds((4 * px + 2 * py + pc) * m_per, m_per), :]

        def copy(k, block, to, src=None):
            return pltpu.make_async_remote_copy(src_ref=rows(*block) if src is None else src, dst_ref=rows(*block),
                                                send_sem=send_sems.at[k], recv_sem=recv_sems.at[k], device_id=to, device_id_type=MESH)

        mine = pltpu.make_async_copy(x_ref, rows(*me), local_sem)
        mine.start()
        first = [copy(0, me, sibling, src=x_ref)] + [copy(1 + j, me, (*chip, c), src=x_ref) for j, chip in enumerate(chips)]
        for cp in first:
            cp.start()
        passed = [copy(4 + j, (*chip, c), sibling) for j, chip in enumerate(chips)]
        for j, chip in enumerate(chips):
            copy(1 + j, (*chip, c), me).wait_recv()
            passed[j].start()
        copy(0, sibling, me).wait_recv()
        for j, chip in enumerate(chips):
            copy(4 + j, (*chip, 1 - c), me).wait_recv()
        for cp in first + passed:
            cp.wait_send()
        mine.wait()

    return pl.pallas_call(
        body, out_shape=_sds((8 * m_per, n), v.dtype), in_specs=[_VMEM], out_specs=_VMEM,
        scratch_shapes=[pltpu.SemaphoreType.DMA((7,)), pltpu.SemaphoreType.DMA((7,)), pltpu.SemaphoreType.DMA], name=name)(v)


def _sems(n):
    return [pltpu.SemaphoreType.DMA((n,)), pltpu.SemaphoreType.DMA((n,))]


_HBM = pl.BlockSpec(memory_space=pltpu.HBM)
_SEM = pl.BlockSpec(memory_space=pltpu.SEMAPHORE)
_EFFECT = pltpu.CompilerParams(has_side_effects=pltpu.SideEffectType.DATAFLOW_SIDE_EFFECTING)


def _ici_copy(src_refs, land_refs, send_sems, recv_sems, a, j, slices, incoming):
    x, y, c, chips = _place()
    me, other = 2 * x + y, 2 * chips[j][0] + chips[j][1]
    src, dst = slices(src_refs[a], land_refs[a], other, me, c) if incoming else slices(src_refs[a], land_refs[a], me, other, c)
    return pltpu.make_async_remote_copy(src_ref=src, dst_ref=dst, send_sem=send_sems.at[3 * a + j], recv_sem=recv_sems.at[3 * a + j],
                                        device_id=(*chips[j], c), device_id_type=MESH)


def _ici_start(srcs, land_shapes, slices, after, name):
    na = len(srcs)

    def body(*refs):
        src_refs, land_refs, send_sems, recv_sems = refs[:na], refs[na:2 * na], refs[2 * na + 1], refs[2 * na + 2]
        for a in range(na):
            for j in range(3):
                _ici_copy(src_refs, land_refs, send_sems, recv_sems, a, j, slices, False).start()
        refs[-1][...] = jnp.zeros_like(refs[-1])

    hbm = lambda v: pltpu.with_memory_space_constraint(v, pltpu.HBM)
    lands = [hbm(lax.empty(s.shape, s.dtype)) for s in land_shapes]
    return pl.pallas_call(
        body, name=name,
        out_shape=(pltpu.SemaphoreType.DMA((3 * na,)), pltpu.SemaphoreType.DMA((3 * na,)), *[pltpu.HBM(v.shape, v.dtype) for v in srcs],
                   *[pltpu.HBM(s.shape, s.dtype) for s in land_shapes], _sds((8, 128))),
        in_specs=[_HBM] * (2 * na) + [_ANY], out_specs=(_SEM, _SEM, *[_HBM] * (2 * na), _VMEM),
        input_output_aliases={i: 2 + i for i in range(2 * na)}, compiler_params=_EFFECT)(*[hbm(v) for v in srcs], *lands, after)


def _ici_wait(handle, slices, after, name):
    na = (len(handle) - 3) // 2

    def body(*refs):
        src_refs, land_refs, send_sems, recv_sems = refs[:na], refs[na:2 * na], refs[2 * na], refs[2 * na + 1]
        for a in range(na):
            for j in range(3):
                _ici_copy(src_refs, land_refs, send_sems, recv_sems, a, j, slices, False).wait_send()
                _ici_copy(src_refs, land_refs, send_sems, recv_sems, a, j, slices, True).wait_recv()

    thru = handle[2:2 + 2 * na]
    outs = pl.pallas_call(
        body, name=name, out_shape=[pltpu.HBM(v.shape, v.dtype) for v in thru], in_specs=[_HBM] * (2 * na) + [_SEM, _SEM, _ANY],
        out_specs=[_HBM] * (2 * na), input_output_aliases={i: i for i in range(2 * na)}, compiler_params=_EFFECT)(
        *thru, handle[0], handle[1], after)
    return outs[:na], outs[na:]


def _gather_slices(p_ref, land_ref, sender, receiver, c):
    r2 = p_ref.shape[0] // 2
    return p_ref.at[pl.ds(c * r2, r2), :], land_ref.at[sender, pl.ds(c * r2, r2), :]


def _scatter_slices(a_ref, t_ref, sender, receiver, c):
    return a_ref.at[receiver], t_ref.at[sender]


def _gather_start(arrs, after, name):
    return _ici_start(arrs, [_sds((4,) + v.shape, v.dtype) for v in arrs], _gather_slices, after, name)


def _gather_finish(handle, after, name):
    arrs, stacks = _ici_wait(handle, _gather_slices, after, name + "_wait")
    na = len(stacks)

    def body(*refs):
        s_refs, o_refs, (send_sems, recv_sems) = refs[:na], refs[na:2 * na], refs[2 * na:]
        x, y, c, chips = _place()

        def copy(a, j, cc, to):
            r2 = s_refs[a].shape[1] // 2
            at = (2 * chips[j][0] + chips[j][1], pl.ds(cc * r2, r2), slice(None))
            return pltpu.make_async_remote_copy(src_ref=s_refs[a].at[at], dst_ref=o_refs[a].at[at], send_sem=send_sems.at[3 * a + j],
                                                recv_sem=recv_sems.at[3 * a + j], device_id=to, device_id_type=MESH)

        passed = [copy(a, j, c, (x, y, 1 - c)) for a in range(na) for j in range(3)]
        for cp in passed:
            cp.start()
        for a in range(na):
            for j in range(3):
                copy(a, j, 1 - c, (x, y, c)).wait_recv()
        for cp in passed:
            cp.wait_send()

    stacks = pl.pallas_call(
        body, out_shape=[_sds(v.shape, v.dtype) for v in stacks], in_specs=[_ANY] * na, out_specs=[_ANY] * na,
        input_output_aliases={i: i for i in range(na)}, scratch_shapes=_sems(3 * na), name=name + "_pass")(*stacks)
    chip = 2 * lax.axis_index("x") + lax.axis_index("y")
    return [lax.dynamic_update_slice(s, v[None], (chip, 0, 0)) for s, v in zip(stacks, arrs)]


def _send_halves(gs, name):
    na = len(gs)

    def body(*refs):
        g_refs, o_refs, (send_sems, recv_sems) = refs[:na], refs[na:2 * na], refs[2 * na:]
        x, y, c, _ = _place()
        cps = []
        for a in range(na):
            r2 = g_refs[a].shape[1] // 2
            cps.append(pltpu.make_async_remote_copy(src_ref=g_refs[a].at[:, pl.ds((1 - c) * r2, r2), :], dst_ref=o_refs[a],
                                                    send_sem=send_sems.at[a], recv_sem=recv_sems.at[a],
                                                    device_id=(x, y, 1 - c), device_id_type=MESH))
        for cp in cps:
            cp.start()
        for cp in cps:
            cp.wait()

    return pl.pallas_call(
        body, out_shape=[_sds((4, v.shape[1] // 2, v.shape[2]), v.dtype) for v in gs], in_specs=[_ANY] * na, out_specs=[_ANY] * na,
        scratch_shapes=_sems(na), name=name)(*gs)


def _join_halves(fs, name):
    na = len(fs)

    def body(*refs):
        f_refs, o_refs, (send_sems, recv_sems) = refs[:na], refs[na:2 * na], refs[2 * na:]
        x, y, c, _ = _place()

        def copy(a, cc, to):
            r2 = f_refs[a].shape[0]
            return pltpu.make_async_remote_copy(src_ref=f_refs[a], dst_ref=o_refs[a].at[pl.ds(cc * r2, r2), :], send_sem=send_sems.at[a],
                                                recv_sem=recv_sems.at[a], device_id=to, device_id_type=MESH)

        cps = [copy(a, c, (x, y, 1 - c)) for a in range(na)]
        for cp in cps:
            cp.start()
        for a in range(na):
            copy(a, 1 - c, (x, y, c)).wait_recv()
        for cp in cps:
            cp.wait_send()

    outs = pl.pallas_call(
        body, out_shape=[_sds((2 * v.shape[0], v.shape[1]), v.dtype) for v in fs], in_specs=[_ANY] * na, out_specs=[_ANY] * na,
        scratch_shapes=_sems(na), name=name)(*fs)
    ci = lax.axis_index("c")
    return [lax.dynamic_update_slice(o, f, (ci * f.shape[0], 0)) for o, f in zip(outs, fs)]


def _sum_chips(a, t, chip, name):
    _, r2, n = t.shape
    tm = _row_tile(r2)

    def body(k_ref, a_ref, t1_ref, t2_ref, t3_ref, o_ref):
        o_ref[...] = ((a_ref[...].astype(F32) + t1_ref[...].astype(F32)) + t2_ref[...].astype(F32)) + t3_ref[...].astype(F32)

    def slot(j):
        return pl.BlockSpec((None, tm, n), lambda i, k_ref: (lax.rem(k_ref[0] + j, 4), i, 0))

    return pl.pallas_call(
        body, grid_spec=pltpu.PrefetchScalarGridSpec(num_scalar_prefetch=1, grid=(r2 // tm,), in_specs=[slot(0), slot(1), slot(2), slot(3)],
                                                     out_specs=pl.BlockSpec((tm, n), lambda i, k_ref: (i, 0))),
        out_shape=_sds((r2, n)), name=name, compiler_params=_PAR)(chip.reshape(1).astype(jnp.int32), a, t, t, t)


def _add_cast(g, recv, c, name):
    _, r2, n = recv.shape

    def body(c_ref, a_ref, b_ref, o_ref):
        o_ref[...] = (a_ref[...] + b_ref[...]).astype(BF16)

    spec = pl.BlockSpec((None, r2, n), lambda k, c_ref: (k, 0, 0))
    return pl.pallas_call(
        body, grid_spec=pltpu.PrefetchScalarGridSpec(
            num_scalar_prefetch=1, grid=(4,), in_specs=[pl.BlockSpec((None, r2, n), lambda k, c_ref: (k, c_ref[0], 0)), spec], out_specs=spec),
        out_shape=_sds(recv.shape, BF16), name=name, compiler_params=_PAR)(c.reshape(1).astype(jnp.int32), g, recv)


def _sum_lead(t, name, tm=256):
    P, R, n = t.shape
    tm = _row_tile(R, tm)

    def body(t_ref, o_ref):
        acc = t_ref[0].astype(F32)
        for j in range(1, P):
            acc = acc + t_ref[j].astype(F32)
        o_ref[...] = acc

    return pl.pallas_call(body, grid=(R // tm,), in_specs=[pl.BlockSpec((P, tm, n), lambda i: (0, i, 0))],
                          out_specs=pl.BlockSpec((tm, n), lambda i: (i, 0)), out_shape=_sds((R, n)), name=name, compiler_params=_PAR)(t)


def _ada_fwd(c16, ada_w, ada_b_cols, tn=512):
    L, _, n = ada_w.shape

    def body(c_ref, w_ref, b_ref, o_ref):
        o_ref[0] = _dot(_silu(c_ref[...]).astype(BF16), w_ref[0].astype(BF16), _NN) + b_ref[0]

    return pl.pallas_call(
        body, grid=(L, n // tn),
        in_specs=[pl.BlockSpec((16, D), lambda l, j: (0, 0)), pl.BlockSpec((1, D, tn), lambda l, j: (l, 0, j)), pl.BlockSpec((1, 1, tn), lambda l, j: (l, 0, j))],
        out_specs=pl.BlockSpec((1, 16, tn), lambda l, j: (l, 0, j)), out_shape=_sds((L, 16, n)), name="ada_fwd",
        compiler_params=pltpu.CompilerParams(dimension_semantics=("parallel", "parallel")))(c16, ada_w, ada_b_cols)


def _ada_bwd(c16, dmod, tn=512):
    L, _, n = dmod.shape

    def body(c_ref, d_ref, o_ref):
        o_ref[0] = _dot(_silu(c_ref[...]).astype(BF16), d_ref[0].astype(BF16), _TN)

    return pl.pallas_call(
        body, grid=(L, n // tn), in_specs=[pl.BlockSpec((16, D), lambda l, j: (0, 0)), pl.BlockSpec((1, 16, tn), lambda l, j: (l, 0, j))],
        out_specs=pl.BlockSpec((1, D, tn), lambda l, j: (l, 0, j)), out_shape=_sds((L, D, n)), name="ada_bwd",
        compiler_params=pltpu.CompilerParams(dimension_semantics=("parallel", "parallel")))(c16, dmod)


def _adam_math(w, g, m, v):
    mn = ADAM_B1 * m + (1.0 - ADAM_B1) * g
    vn = ADAM_B2 * v + (1.0 - ADAM_B2) * (g * g)
    m_hat = mn / (1.0 - ADAM_B1 ** ADAM_STEP)
    v_hat = vn / (1.0 - ADAM_B2 ** ADAM_STEP)
    return -ADAM_LR * (m_hat / (jnp.sqrt(v_hat) + ADAM_EPS) + ADAM_WD * w), mn, vn


def _adamw(w, g, m, v, name):
    R, n = w.shape
    tm = _row_tile(R)

    def body(w_ref, g_ref, m_ref, v_ref, d_ref, nm_ref, nv_ref):
        d_ref[...], nm_ref[...], nv_ref[...] = _adam_math(w_ref[...], g_ref[...], m_ref[...], v_ref[...])

    spec = pl.BlockSpec((tm, n), lambda i: (i, 0))
    return pl.pallas_call(body, grid=(R // tm,), in_specs=[spec] * 4, out_specs=(spec,) * 3, out_shape=(_sds((R, n)),) * 3,
                          name=name, compiler_params=_PAR)(w, g, m, v)


def _adamw_layers(w, g0, g1, m, v, after, name):
    _, r, n = w.shape
    tm = _row_tile(r)
    nb = r // tm

    def body(w_ref, g0_ref, g1_ref, m_ref, v_ref, after_ref, g_ref, d_ref, nm_ref, nv_ref):
        gv = jnp.where(pl.program_id(0) == 0, g0_ref[...], g1_ref[...])
        g_ref[...] = gv
        d_ref[...], nm_ref[...], nv_ref[...] = _adam_math(w_ref[...], gv, m_ref[...], v_ref[...])

    spec = pl.BlockSpec((None, tm, n), lambda l, i: (l, i, 0))
    g0_spec = pl.BlockSpec((tm, n), lambda l, i: (i * (1 - l) + (nb - 1) * l, 0))
    g1_spec = pl.BlockSpec((tm, n), lambda l, i: (i * l, 0))
    return pl.pallas_call(body, grid=(2, nb), in_specs=[spec, g0_spec, g1_spec, spec, spec, _ANY], out_specs=(spec,) * 4,
                          out_shape=(_sds(w.shape),) * 4, name=name,
                          compiler_params=pltpu.CompilerParams(dimension_semantics=("arbitrary", "arbitrary")))(w, g0, g1, m, v, after)


_W_NAMES = ["ada_w", "ada_b", "norm1_w", "w_in", "q_a_norm", "w_q_b", "kv_a_norm", "w_kv_b", "q_norm", "k_norm", "pool_w",
            "pool_scale", "ssd_conv_w", "ssd_conv_b", "ssd_dt_bias", "ssd_a_log", "ssd_d", "ssd_norm_w", "w_branch", "w_out",
            "norm2_w", "ffn_up", "ffn_conv_w", "ffn_conv_b", "ffn_down"]
_BIG = [("w_in", (D, IN_DIM // 4), 1), ("w_q_b", (Q_RANK, 192), 1), ("w_kv_b", (KV_RANK, 256), 1), ("w_branch", (512, D), 0),
        ("w_out", (256, D), 0), ("ffn_up", (D, 2 * FFN // 4), 1), ("ffn_down", (FFN // 4, D), 0)]

_SMALL = [("norm1_w", (D,)), ("q_a_norm", (Q_RANK,)), ("kv_a_norm", (KV_RANK,)), ("q_norm", (96,)), ("k_norm", (96,)),
          ("pool_w", (4, 128, 128)), ("pool_scale", (512,)), ("ssd_conv_w", (4, SSD_CONV_DIM)), ("ssd_conv_b", (SSD_CONV_DIM,)),
          ("ssd_dt_bias", (SSD_HEADS,)), ("ssd_a_log", (SSD_HEADS,)), ("ssd_d", (SSD_HEADS,)), ("ssd_norm_w", (D,)), ("norm2_w", (D,)),
          ("ffn_conv_w", (3, 2 * FFN)), ("ffn_conv_b", (2 * FFN,))]
_CONV_SHARDED = {"ssd_conv_w": SSD_CONV_DIM // 4, "ffn_conv_w": 2 * FFN // 4}


def _pack_flat(arrs, mult):
    flat = jnp.concatenate([a.astype(F32).reshape(-1) for a in arrs])
    rows = -(-flat.shape[0] // (128 * mult)) * mult
    return jnp.pad(flat, (0, rows * 128 - flat.shape[0])).reshape(rows, 128), [a.shape for a in arrs]


def _unpack_flat(packed, shapes):
    flat, out, off = packed.reshape(-1), [], 0
    for s in shapes:
        n = 1
        for d in s:
            n *= d
        out.append(flat[off:off + n].reshape(s))
        off += n
    return out


_EARLY = ["w_in", "w_q_b", "w_kv_b"]
_LATE = ["w_branch", "w_out", "ffn_up", "ffn_down"]


def _early_weights(a, l, stacks, conv_full):
    p = {n: a[n][l] for n in _W_NAMES if n not in ("ada_w", "ada_b")}
    p.update({n: conv_full[n][l] for n in conv_full})
    p.update(w_in4=stacks[0], w_q_b4=stacks[1], w_kv_b4=stacks[2])
    return p


def _late_weights(stacks):
    return _prep_late(dict(w_branch=stacks[0].reshape(2048, D), w_out=stacks[1].reshape(D, D), ffn_up4=stacks[2],
                           ffn_down=stacks[3].reshape(FFN, D)))


def _reduce_start(gs, ci, after, tag):
    recv = _send_halves(gs, f"rs_halves_{tag}")
    chip_sum = [_add_cast(g, r, ci, f"rs_add_{tag}") for g, r in zip(gs, recv)]
    return _ici_start(chip_sum, [_sds(v.shape, v.dtype) for v in chip_sum], _scatter_slices, after, f"rs_scatter_{tag}_start")


def _reduce_finish(started, after, tag):
    chip_sum, got = _ici_wait(started, _scatter_slices, after, f"rs_scatter_{tag}_wait")
    chip = 2 * lax.axis_index("x") + lax.axis_index("y")
    return _join_halves([_sum_chips(s, t, chip, f"rs_sum_{tag}") for s, t in zip(chip_sum, got)], f"rs_join_{tag}")


def kernel(x, c, positions, ada_w, ada_b, norm1_w, w_in, q_a_norm, w_q_b, kv_a_norm, w_kv_b, q_norm, k_norm, pool_w, pool_scale, ssd_conv_w, ssd_conv_b, ssd_dt_bias, ssd_a_log, ssd_d, ssd_norm_w, w_branch, w_out, norm2_w, ffn_up, ffn_conv_w, ffn_conv_b, ffn_down, loss_target, m_ada_w, m_ada_b, m_norm1_w, m_w_in, m_q_a_norm, m_w_q_b, m_kv_a_norm, m_w_kv_b, m_q_norm, m_k_norm, m_pool_w, m_pool_scale, m_ssd_conv_w, m_ssd_conv_b, m_ssd_dt_bias, m_ssd_a_log, m_ssd_d, m_ssd_norm_w, m_w_branch, m_w_out, m_norm2_w, m_ffn_up, m_ffn_conv_w, m_ffn_conv_b, m_ffn_down, v_ada_w, v_ada_b, v_norm1_w, v_w_in, v_q_a_norm, v_w_q_b, v_kv_a_norm, v_w_kv_b, v_q_norm, v_k_norm, v_pool_w, v_pool_scale, v_ssd_conv_w, v_ssd_conv_b, v_ssd_dt_bias, v_ssd_a_log, v_ssd_d, v_ssd_norm_w, v_w_branch, v_w_out, v_norm2_w, v_ffn_up, v_ffn_conv_w, v_ffn_conv_b, v_ffn_down):
    a = dict(locals())
    xi, yi, ci = lax.axis_index("x"), lax.axis_index("y"), lax.axis_index("c")
    chip = 2 * xi + yi
    dev = 2 * chip + ci
    ncol = 6 * D // 4

    c_all = _allgather8(c.reshape(8, 128), "gather_c").reshape(8, D)
    c16 = jnp.pad(c_all, ((0, 8), (0, 0)))
    ada_b_cols = lax.dynamic_slice_in_dim(ada_b, chip * ncol, ncol, axis=1).reshape(2, 1, ncol)
    mod_part = _ada_fwd(c16, ada_w, ada_b_cols)[:, :8]
    small1, shapes1 = _pack_flat([mod_part, ssd_conv_w, ffn_conv_w], 8)
    got1 = _allgather8(small1, "gather_mod").reshape(8, -1, 128)
    per_chip = [_unpack_flat(got1[2 * k], shapes1) for k in range(4)]
    mod_all = jnp.concatenate([per_chip[k][0] for k in range(4)], -1)
    conv_full = {"ssd_conv_w": jnp.concatenate([per_chip[k][1] for k in range(4)], -1),
                 "ffn_conv_w": jnp.concatenate([per_chip[k][2] for k in range(4)], -1)}
    mod_mine = lax.dynamic_index_in_dim(mod_all, dev, axis=1, keepdims=False).reshape(2, 6, D)
    mods = [jnp.pad(mod_mine[l], ((0, 2), (0, 0))) for l in range(2)]

    big = _EARLY + _LATE
    shard = lambda names, l: [a[n][l].astype(BF16) for n in names]
    g0a = _gather_start(shard(_EARLY, 0), mods[0], "gather_0a")
    g0b = _gather_start(shard(_LATE, 0), g0a[-1], "gather_0b")
    g1 = _gather_start(shard(big, 1), g0b[-1], "gather_1")
    mods[0] = mods[0] + g1[-1][0, 0]

    def get_layer(l, after):
        if l == 0:
            return (_early_weights(a, 0, _gather_finish(g0a, mods[0], "gather_0a"), conv_full),
                    lambda aft: _late_weights(_gather_finish(g0b, aft, "gather_0b")))
        stacks = _gather_finish(g1, after, "gather_1")
        return _early_weights(a, 1, stacks[:3], conv_full), lambda aft: _late_weights(stacks[3:])

    scatters = {}

    def bwd_mod(l):
        return mods[l] if l == 1 else mods[0] + scatters["1"][-1][0, 0]

    def emit(l, late):
        if l == 0:
            scatters["0b"] = _reduce_start([late[n] for n in _LATE], ci, scatters["1"][-1], "0b")
            return scatters["0b"][-1][0, 0]

    def done(l, grads_l):
        if l == 1:
            scatters["1"] = _reduce_start([grads_l[n] for n in big], ci, grads_l["norm1_w"], "1")
        else:
            scatters["0a"] = _reduce_start([grads_l[n] for n in _EARLY], ci, scatters["0b"][-1], "0a")

    lpart, grad_x, grads, dmods = _local_step(x[0], loss_target[0], positions[0], mods, get_layer, bwd_mod, emit, done)
    loss = lax.psum(lpart, ("x", "y", "c"))
    red1 = _reduce_finish(scatters["1"], grad_x, "1")
    red0b = _reduce_finish(scatters["0b"], red1[0], "0b")

    small2, shapes2 = _pack_flat([jnp.stack(dmods)] + [grads[l][n] for l in range(2) for n, _ in _SMALL], 16)
    got2 = _allgather8(small2, "gather_small").reshape(8, -1, 128)
    tot = _unpack_flat(_sum_lead(got2, "sum_small"), shapes2)
    g = {"ada_b": tot[0].reshape(2, 6 * D)}
    for i, (n, _) in enumerate(_SMALL):
        g[n] = jnp.stack([tot[1 + i], tot[1 + len(_SMALL) + i]])
    for n, w in _CONV_SHARDED.items():
        g[n] = lax.dynamic_slice_in_dim(g[n], chip * w, w, axis=2)
    nd = 2 * 6 * D // 128
    dmod_all = jnp.transpose(got2[:, :nd].reshape(8, 2, 6 * D), (1, 0, 2))
    dmod_cols = lax.dynamic_slice_in_dim(jnp.pad(dmod_all, ((0, 0), (0, 8), (0, 0))), chip * ncol, ncol, axis=2)
    g["ada_w"] = _ada_bwd(c16, dmod_cols)

    delta, new_m, new_v = {}, {}, {}
    red1 = dict(zip(big, red1))
    token = scatters["0a"][-1]
    for n, r0 in zip(_LATE, red0b):
        g[n], delta[n], new_m[n], new_v[n] = _adamw_layers(a[n], r0, red1[n], a["m_" + n], a["v_" + n], token, f"adamw_{n}")
    shp = ada_w.shape
    r2 = lambda t: t.reshape(-1, shp[-1])
    delta["ada_w"], new_m["ada_w"], new_v["ada_w"] = (
        o.reshape(shp) for o in _adamw(r2(ada_w), r2(g["ada_w"]), r2(m_ada_w), r2(v_ada_w), "adamw_ada_w"))
    behind = (delta[_LATE[-1]][0, 0, :1] + delta["ada_w"][0, 0, :1]).reshape(1)
    red0a = _reduce_finish(scatters["0a"], behind, "0a")
    for n, r0 in zip(_EARLY, red0a):
        g[n], delta[n], new_m[n], new_v[n] = _adamw_layers(a[n], r0, red1[n], a["m_" + n], a["v_" + n], token, f"adamw_{n}")
    rest = [n for n in _W_NAMES if n not in big and n != "ada_w"]
    packs = [_pack_flat([t[n] if pre is None else t[pre + n] for n in rest], 128)[0]
             for t, pre in ((a, None), (g, None), (a, "m_"), (a, "v_"))]
    rest_shapes = [a[n].shape for n in rest]
    outs = [_unpack_flat(o, rest_shapes) for o in _adamw(*packs, "adamw_rest")]
    for i, n in enumerate(rest):
        delta[n], new_m[n], new_v[n] = outs[0][i], outs[1][i], outs[2][i]

    return (loss, grad_x[None], *[g[n] for n in _W_NAMES], *[delta[n] for n in _W_NAMES],
            *[new_m[n] for n in _W_NAMES], *[new_v[n] for n in _W_NAMES])
```

```python
import functools

import jax
import jax.numpy as jnp
from jax import lax
from jax.experimental import pallas as pl
from jax.experimental.pallas import tpu as pltpu

F32 = jnp.float32
BF16 = jnp.bfloat16
MESH = pl.DeviceIdType.MESH
HI = lax.Precision.HIGHEST

D = 1024
N_HEADS = 8
NOPE, ROPE_DIM = 64, 32
Q_RANK, KV_RANK = 384, 256
POOL_WINDOWS = (2, 4, 8, 16)
SSD_HEADS, SSD_P, SSD_N, SSD_L = 16, 64, 128, 128
SSD_INNER = 1024
SSD_CONV_DIM = 1536
FFN = 2816
EPS = 1e-6
ROPE_THETA = 10000.0
OG, OZ, OX, OP, OQ, OKV, ODT, IN_PAD = 0, 3072, 4096, 5632, 6144, 6528, 6912, 7040
IN_DIM = 6832
ADAM_LR, ADAM_B1, ADAM_B2, ADAM_EPS, ADAM_WD, ADAM_STEP = 0.001, 0.9, 0.999, 1e-08, 0.01, 10

_ARB = pltpu.CompilerParams(dimension_semantics=("arbitrary",))
_PAR = pltpu.CompilerParams(dimension_semantics=("parallel",))


def _pick(n, pref):
    if n <= pref:
        return n
    best = None
    for t in range(128, pref + 1, 128):
        if n % t == 0:
            best = t
    assert best is not None, (n, pref)
    return best


def _row_tile(r, cap=256):
    best = None
    for t in range(16, min(r, cap) + 1, 16):
        if r % t == 0:
            best = t
    assert best is not None, r
    return best


def _sds(shape, dtype=F32):
    return jax.ShapeDtypeStruct(tuple(shape), dtype)


def _iota(shape, dim):
    return lax.broadcasted_iota(jnp.int32, shape, dim)


def _sigmoid(x):
    return 0.5 * jnp.tanh(0.5 * x) + 0.5


def _silu(x):
    return x * _sigmoid(x)


def _dsilu(x):
    s = _sigmoid(x)
    return s * (1.0 + x * (1.0 - s))


def _dot(a, b, dims):
    return lax.dot_general(a, b, (dims, ((), ())), preferred_element_type=F32)


_NN, _NT, _TN = ((1,), (0,)), ((1,), (1,)), ((0,), (0,))


def _dot_hi(a, b, dims=_NN):
    return lax.dot_general(a, b, (dims, ((), ())), preferred_element_type=F32, precision=HI)


def _shift_down(x, j):
    n = x.shape[0]
    return jnp.where(_iota(x.shape, 0) >= j, pltpu.roll(x, j, 0), 0.0)


def _shift_up(x, j):
    n = x.shape[0]
    return jnp.where(_iota(x.shape, 0) < n - j, pltpu.roll(x, n - j, 0), 0.0)


def _mm(a, b, mode="nn", out_dtype=F32, tm=2048, tn=512, tk=4096, res=None, gate=None, name="mm"):
    if mode == "nn":
        (M, K), (K2, N) = a.shape, b.shape
    elif mode == "nt":
        (M, K), (N, K2) = a.shape, b.shape
    else:
        (K, M), (K2, N) = a.shape, b.shape
    assert K == K2, (a.shape, b.shape, mode)
    tm, tn, tk = _pick(M, tm), _pick(N, tn), _pick(K, tk)
    nk = K // tk
    dims = {"nn": _NN, "nt": _NT, "tn": _TN}[mode]
    fused = res is not None

    def body(*refs):
        a_ref, b_ref = refs[:2]

        def finish(acc):
            if fused:
                r_ref, g_ref, o_ref, raw_ref = refs[2:6]
                raw_ref[...] = acc
                o_ref[...] = r_ref[...] + g_ref[...] * acc
            else:
                refs[2][...] = acc.astype(out_dtype)

        _mm_steps(a_ref, b_ref, dims, nk, refs[-1] if nk > 1 else None, finish)

    if mode == "nn":
        a_spec = pl.BlockSpec((tm, tk), lambda i, j, k: (i, k))
        b_spec = pl.BlockSpec((tk, tn), lambda i, j, k: (k, j))
    elif mode == "nt":
        a_spec = pl.BlockSpec((tm, tk), lambda i, j, k: (i, k))
        b_spec = pl.BlockSpec((tn, tk), lambda i, j, k: (j, k))
    else:
        a_spec = pl.BlockSpec((tk, tm), lambda i, j, k: (k, i))
        b_spec = pl.BlockSpec((tk, tn), lambda i, j, k: (k, j))
    o_spec = pl.BlockSpec((tm, tn), lambda i, j, k: (i, j))
    in_specs, args = [a_spec, b_spec], [a, b]
    out_shape, out_specs = _sds((M, N), out_dtype), o_spec
    if fused:
        in_specs += [o_spec, pl.BlockSpec((1, tn), lambda i, j, k: (0, j))]
        args += [res, gate]
        out_shape, out_specs = (_sds((M, N)), _sds((M, N))), (o_spec, o_spec)
    return pl.pallas_call(
        body, grid=(M // tm, N // tn, nk), in_specs=in_specs, out_specs=out_specs, out_shape=out_shape,
        scratch_shapes=[pltpu.VMEM((tm, tn), F32)] if nk > 1 else [], name=name,
        compiler_params=pltpu.CompilerParams(dimension_semantics=("parallel", "parallel", "arbitrary")),
    )(*args)


def _mm_steps(a_ref, b_ref, dims, nk, acc_ref, finish):
    part = _dot(a_ref[...].astype(BF16), b_ref[...].astype(BF16), dims)
    if nk == 1:
        finish(part)
        return
    k = pl.program_id(2)

    @pl.when(k == 0)
    def _():
        acc_ref[...] = part

    @pl.when(k > 0)
    def _():
        acc_ref[...] += part

    @pl.when(k == nk - 1)
    def _():
        finish(acc_ref[...])


def _mm_blocks(a, b, dims, grid, a_spec, b_spec, o_spec, out_shape, acc_shape, name):
    nk = grid[2]

    def body(a_ref, b_ref, o_ref, *scratch):
        def finish(acc):
            o_ref[...] = acc.astype(o_ref.dtype)

        _mm_steps(a_ref, b_ref, dims, nk, scratch[0] if nk > 1 else None, finish)

    return pl.pallas_call(
        body, grid=grid, in_specs=[a_spec, b_spec], out_specs=o_spec, out_shape=out_shape,
        scratch_shapes=[pltpu.VMEM(acc_shape, F32)] if nk > 1 else [], name=name,
        compiler_params=pltpu.CompilerParams(dimension_semantics=("parallel", "parallel", "arbitrary")),
    )(a, b)


_UP_SHARD = 2 * FFN // 4


def _up_fwd(h2, wup4, name, tm=2048):
    S = h2.shape[0]
    tm = min(tm, S)
    return _mm_blocks(h2, wup4, _NN, (S // tm, 4, 1), pl.BlockSpec((tm, D), lambda i, j, k: (i, 0)),
                      pl.BlockSpec((None, D, _UP_SHARD), lambda i, j, k: (j, 0, 0)), pl.BlockSpec((tm, _UP_SHARD), lambda i, j, k: (i, j)),
                      _sds((S, 2 * FFN)), (tm, _UP_SHARD), name)


def _up_dx(dup2, wup4, name, tm=2048, tn=512):
    S = dup2.shape[1]
    tm = min(tm, S)
    return _mm_blocks(dup2, wup4, _NT, (S // tm, D // tn, 4), pl.BlockSpec((None, tm, _UP_SHARD), lambda i, j, k: (lax.div(k, 2), i, lax.rem(k, 2))),
                      pl.BlockSpec((None, tn, _UP_SHARD), lambda i, j, k: (k, j, 0)), pl.BlockSpec((tm, tn), lambda i, j, k: (i, j)),
                      _sds((S, D)), (tm, tn), name)


def _up_dw(h2, dup2, name, tm=1024, tk=4096):
    S = h2.shape[0]
    tk = min(tk, S)
    return _mm_blocks(h2, dup2, _TN, (D // tm, 4, S // tk), pl.BlockSpec((tk, tm), lambda i, j, k: (k, i)),
                      pl.BlockSpec((None, tk, _UP_SHARD), lambda i, j, k: (lax.div(j, 2), k, lax.rem(j, 2))),
                      pl.BlockSpec((None, tm, _UP_SHARD), lambda i, j, k: (j, i, 0)), _sds((4, D, _UP_SHARD)), (tm, _UP_SHARD), name)


def _row_spec(tm, n):
    return pl.BlockSpec((tm, n), lambda i: (i, 0))


def _vec_spec(n, rows=1):
    return pl.BlockSpec((rows, n), lambda i: (0, 0))


def _ln_mod(x, nw, sc, sh, name, tm=256):
    S = x.shape[0]

    def body(x_ref, nw_ref, sc_ref, sh_ref, o_ref):
        xv = x_ref[...]
        r = lax.rsqrt(jnp.mean(xv * xv, -1, keepdims=True) + EPS)
        o_ref[...] = ((xv * r * nw_ref[...]) * (1.0 + sc_ref[...]) + sh_ref[...]).astype(BF16)

    return pl.pallas_call(
        body, grid=(S // tm,), in_specs=[_row_spec(tm, D)] + [_vec_spec(D)] * 3, out_specs=_row_spec(tm, D),
        out_shape=_sds((S, D), BF16), name=name, compiler_params=_PAR)(x, nw, sc, sh)


def _ln_mod_bwd(x, dh, dres, nw, sc, name, tm=256):
    S = x.shape[0]

    def body(x_ref, dh_ref, dres_ref, nw_ref, sc_ref, dx_ref, st_ref):
        @pl.when(pl.program_id(0) == 0)
        def _():
            st_ref[...] = jnp.zeros_like(st_ref)

        xv, dhv, nwv = x_ref[...], dh_ref[...], nw_ref[...]
        r = lax.rsqrt(jnp.mean(xv * xv, -1, keepdims=True) + EPS)
        xhat = xv * r
        dn = dhv * (1.0 + sc_ref[...])
        g = dn * nwv
        dx_ref[...] = dres_ref[...] + r * (g - xhat * jnp.mean(g * xhat, -1, keepdims=True))
        st_ref[0:1, :] += jnp.sum(dhv, 0, keepdims=True)
        st_ref[1:2, :] += jnp.sum(dhv * (xhat * nwv), 0, keepdims=True)
        st_ref[2:3, :] += jnp.sum(dn * xhat, 0, keepdims=True)

    return pl.pallas_call(
        body, grid=(S // tm,), in_specs=[_row_spec(tm, D)] * 3 + [_vec_spec(D)] * 2,
        out_specs=(_row_spec(tm, D), _vec_spec(D, 8)), out_shape=(_sds((S, D)), _sds((8, D))),
        name=name, compiler_params=_ARB)(x, dh, dres, nw, sc)


def _gate_bwd(dx, out, g, name, tm=256):
    S = dx.shape[0]

    def body(dx_ref, o_ref, g_ref, dz_ref, dg_ref):
        @pl.when(pl.program_id(0) == 0)
        def _():
            dg_ref[...] = jnp.zeros_like(dg_ref)

        dxv = dx_ref[...]
        dz_ref[...] = (dxv * g_ref[...]).astype(BF16)
        dg_ref[0:1, :] += jnp.sum(dxv * o_ref[...], 0, keepdims=True)

    return pl.pallas_call(
        body, grid=(S // tm,), in_specs=[_row_spec(tm, D)] * 2 + [_vec_spec(D)],
        out_specs=(_row_spec(tm, D), _vec_spec(D, 8)), out_shape=(_sds((S, D), BF16), _sds((8, D))),
        name=name, compiler_params=_ARB)(dx, out, g)


def _loss_grad(y, t, tm=256):
    S = y.shape[0]

    def body(y_ref, t_ref, dy_ref, l_ref):
        @pl.when(pl.program_id(0) == 0)
        def _():
            l_ref[...] = jnp.zeros_like(l_ref)

        e = y_ref[...] - t_ref[...]
        dy_ref[...] = e * (1.0 / D)
        l_ref[...] += 0.5 * jnp.sum(jnp.mean(e * e, -1, keepdims=True), 0, keepdims=True)

    return pl.pallas_call(
        body, grid=(S // tm,), in_specs=[_row_spec(tm, D)] * 2,
        out_specs=(_row_spec(tm, D), pl.BlockSpec((8, 128), lambda i: (0, 0))),
        out_shape=(_sds((S, D)), _sds((8, 128))), name="loss_grad", compiler_params=_ARB)(y, t)


_CONV_ROWS = 256
_HALO = 8


def _rows(ref, lo, hi):
    S, c = ref.shape
    parts = [jnp.zeros((-lo, c), F32)] if lo < 0 else []
    parts.append(ref[max(lo, 0):min(hi, S), :])
    if hi > S:
        parts.append(jnp.zeros((hi - S, c), F32))
    return parts[0] if len(parts) == 1 else jnp.concatenate(parts, 0)


def _conv_rows(ext, w, b, n):
    K = w.shape[0]
    acc = ext[_HALO:_HALO + n] * w[K - 1:K, :] + b
    for j in range(1, K):
        acc = acc + pltpu.roll(ext, j, 0)[_HALO:_HALO + n] * w[K - 1 - j:K - j, :]
    return acc


def _conv_rows_bwd(ext, w, dc, n):
    K, m = w.shape[0], dc.shape[0]
    d0 = dc[0:n]
    dx = d0 * w[K - 1:K, :]
    dws = [jnp.sum(d0 * ext[_HALO:_HALO + n], 0, keepdims=True)]
    for j in range(1, K):
        dx = dx + pltpu.roll(dc, m - j, 0)[0:n] * w[K - 1 - j:K - j, :]
        dws.append(jnp.sum(d0 * pltpu.roll(ext, j, 0)[_HALO:_HALO + n], 0, keepdims=True))
    return dx, dws[::-1], jnp.sum(d0, 0, keepdims=True)


def _col_spec(S, tc, off=0):
    return pl.BlockSpec((S, tc), lambda j: (0, j + off))


def _ssd_pre(proj, cw, cb, tc=256):
    S, n = proj.shape[0], SSD_CONV_DIM
    R = min(_CONV_ROWS, S)

    def body(x_ref, w_ref, b_ref, o_ref):
        wv, bv = w_ref[...], b_ref[...]
        for r0 in range(0, S, R):
            o_ref[r0:r0 + R, :] = _silu(_conv_rows(_rows(x_ref, r0 - _HALO, r0 + R), wv, bv, R))

    return pl.pallas_call(
        body, grid=(n // tc,),
        in_specs=[_col_spec(S, tc, OX // tc), pl.BlockSpec((4, tc), lambda j: (0, j)), pl.BlockSpec((1, tc), lambda j: (0, j))],
        out_specs=_col_spec(S, tc), out_shape=_sds((S, n)), name="ssd_pre", compiler_params=_PAR)(proj, cw, cb)


def _ssd_pre_bwd(proj, dxc, cw, cb, tc=256):
    S, n = proj.shape[0], SSD_CONV_DIM
    R = min(_CONV_ROWS, S)

    def body(x_ref, d_ref, w_ref, b_ref, dx_ref, dw_ref, db_ref):
        wv, bv = w_ref[...], b_ref[...]
        acc = [jnp.zeros((1, tc), F32)] * 5
        for r0 in range(0, S, R):
            ext = _rows(x_ref, r0 - _HALO, r0 + R + _HALO)
            dc = _rows(d_ref, r0, r0 + R + _HALO) * _dsilu(_conv_rows(ext, wv, bv, R + _HALO))
            dx, dws, db = _conv_rows_bwd(ext, wv, dc, R)
            dx_ref[r0:r0 + R, :] = dx.astype(BF16)
            acc = [s + d for s, d in zip(acc, dws + [db])]
        for k in range(4):
            dw_ref[k:k + 1, :] = acc[k]
        db_ref[...] = acc[4]

    wspec, bspec = pl.BlockSpec((4, tc), lambda j: (0, j)), pl.BlockSpec((1, tc), lambda j: (0, j))
    return pl.pallas_call(
        body, grid=(n // tc,), in_specs=[_col_spec(S, tc, OX // tc), _col_spec(S, tc), wspec, bspec],
        out_specs=(_col_spec(S, tc), wspec, bspec), out_shape=(_sds((S, n), BF16), _sds((4, n)), _sds((1, n))),
        name="ssd_pre_bwd", compiler_params=_PAR)(proj, dxc, cw, cb)


def _ffn_act(up, cw, cb, tc=256):
    S, nb = up.shape[0], FFN // tc
    R = min(_CONV_ROWS, S)

    def body(g_ref, v_ref, wg_ref, wv_ref, bg_ref, bv_ref, o_ref):
        wg, wv, bg, bv = wg_ref[...], wv_ref[...], bg_ref[...], bv_ref[...]
        for r0 in range(0, S, R):
            cg = _conv_rows(_rows(g_ref, r0 - _HALO, r0 + R), wg, bg, R)
            cv = _conv_rows(_rows(v_ref, r0 - _HALO, r0 + R), wv, bv, R)
            o_ref[r0:r0 + R, :] = (_silu(cg) * cv).astype(BF16)

    def wspec(off):
        return pl.BlockSpec((3, tc), lambda j: (0, j + off))

    def bspec(off):
        return pl.BlockSpec((1, tc), lambda j: (0, j + off))

    return pl.pallas_call(
        body, grid=(nb,), in_specs=[_col_spec(S, tc), _col_spec(S, tc, nb), wspec(0), wspec(nb), bspec(0), bspec(nb)],
        out_specs=_col_spec(S, tc), out_shape=_sds((S, FFN), BF16), name="ffn_act", compiler_params=_PAR)(up, up, cw, cw, cb, cb)


def _ffn_act_bwd(up, dact, cw, cb, tc=256):
    S, nb = up.shape[0], FFN // tc
    R = min(_CONV_ROWS, S)

    def body(g_ref, v_ref, d_ref, wg_ref, wv_ref, bg_ref, bv_ref, dx_ref, dw_ref, db_ref):
        wg, wv, bg, bv = wg_ref[...], wv_ref[...], bg_ref[...], bv_ref[...]
        acc = [[jnp.zeros((1, tc), F32)] * 4, [jnp.zeros((1, tc), F32)] * 4]
        for r0 in range(0, S, R):
            eg, ev = _rows(g_ref, r0 - _HALO, r0 + R + _HALO), _rows(v_ref, r0 - _HALO, r0 + R + _HALO)
            da = _rows(d_ref, r0, r0 + R + _HALO)
            cg, cv = _conv_rows(eg, wg, bg, R + _HALO), _conv_rows(ev, wv, bv, R + _HALO)
            sg = _sigmoid(cg)
            for half, (ext, w, dc) in enumerate(((eg, wg, da * cv * (sg * (1.0 + cg * (1.0 - sg)))), (ev, wv, da * (cg * sg)))):
                dx, dws, db = _conv_rows_bwd(ext, w, dc, R)
                dx_ref[half, r0:r0 + R, :] = dx.astype(BF16)
                acc[half] = [s + d for s, d in zip(acc[half], dws + [db])]
        for half in range(2):
            for k in range(3):
                dw_ref[half, k:k + 1, :] = acc[half][k]
            db_ref[half] = acc[half][3]

    def wspec(off):
        return pl.BlockSpec((3, tc), lambda j: (0, j + off))

    def bspec(off):
        return pl.BlockSpec((1, tc), lambda j: (0, j + off))

    cs = _col_spec(S, tc)
    both = lambda r: pl.BlockSpec((2, r, tc), lambda j: (0, 0, j))
    return pl.pallas_call(
        body, grid=(nb,), in_specs=[cs, _col_spec(S, tc, nb), cs, wspec(0), wspec(nb), bspec(0), bspec(nb)],
        out_specs=(both(S), both(3), both(1)), out_shape=(_sds((2, S, FFN), BF16), _sds((2, 3, FFN)), _sds((2, 1, FFN))),
        name="ffn_act_bwd", compiler_params=_PAR)(up, up, dact, cw, cw, cb, cb)


def _window_sum(x, w, up=False):
    shift = _shift_up if up else _shift_down
    j = 1
    while j < w:
        x = x + shift(x, j)
        j *= 2
    return x


def _pool_fwd(proj, pool_w, pool_scale):
    S = proj.shape[0]

    def body(u_ref, w_ref, s_ref, o_ref):
        cnt_row = (_iota((S, 128), 0) + 1).astype(F32)
        for g, w in enumerate(POOL_WINDOWS):
            sl = slice(g * 128, (g + 1) * 128)
            u = u_ref[:, sl]
            pooled = _window_sum(u, w) / jnp.minimum(cnt_row, float(w)) - u
            mixed = _dot(pooled.astype(BF16), w_ref[g].astype(BF16), _NN)
            o_ref[:, sl] = (mixed * s_ref[:, sl]).astype(BF16)

    return pl.pallas_call(
        body, grid=(1,),
        in_specs=[pl.BlockSpec((S, 512), lambda i: (0, OP // 512)), pl.BlockSpec((4, 128, 128), lambda i: (0, 0, 0)), _vec_spec(512)],
        out_specs=pl.BlockSpec((S, 512), lambda i: (0, 0)), out_shape=_sds((S, 512), BF16), name="pool_fwd",
        compiler_params=_ARB)(proj, pool_w, pool_scale)


def _pool_bwd(proj, dob, pool_w, pool_scale):
    S = proj.shape[0]

    def body(u_ref, d_ref, w_ref, s_ref, du_ref, dw_ref, ds_ref):
        cnt_row = (_iota((S, 128), 0) + 1).astype(F32)
        for g, w in enumerate(POOL_WINDOWS):
            sl = slice(g * 128, (g + 1) * 128)
            u, dv, wv = u_ref[:, sl], d_ref[:, sl], w_ref[g].astype(BF16)
            cnt = jnp.minimum(cnt_row, float(w))
            pooled = (_window_sum(u, w) / cnt - u).astype(BF16)
            ds_ref[:, sl] = jnp.sum(dv * _dot(pooled, wv, _NN), 0, keepdims=True)
            dmix = (dv * s_ref[:, sl]).astype(BF16)
            dw_ref[g] = _dot(pooled, dmix, _TN)
            dp = _dot(dmix, wv, _NT)
            du_ref[:, sl] = (_window_sum(dp / cnt, w, up=True) - dp).astype(BF16)

    blk = pl.BlockSpec((S, 512), lambda i: (0, 0))
    wspec = pl.BlockSpec((4, 128, 128), lambda i: (0, 0, 0))
    return pl.pallas_call(
        body, grid=(1,), in_specs=[pl.BlockSpec((S, 512), lambda i: (0, OP // 512)), blk, wspec, _vec_spec(512)],
        out_specs=(blk, wspec, _vec_spec(512)), out_shape=(_sds((S, 512), BF16), _sds((4, 128, 128)), _sds((1, 512))),
        name="pool_bwd", compiler_params=_ARB)(proj, dob, pool_w, pool_scale)


def _branch_specs():
    return [pl.BlockSpec((512, D), lambda i: (0, 0)), pl.BlockSpec((512, D), lambda i: (1, 0)), pl.BlockSpec((1024, D), lambda i: (1, 0))]


def _merge_fwd(oa, ob, oc, proj, wbr, tm=256):
    S = oa.shape[0]

    def body(oa_ref, ob_ref, oc_ref, gl_ref, wa_ref, wb_ref, wc_ref, o_ref):
        acc = _sigmoid(gl_ref[:, 0:D]) * _dot(oa_ref[...], wa_ref[...], _NN)
        acc += _sigmoid(gl_ref[:, D:2 * D]) * _dot(ob_ref[...], wb_ref[...], _NN)
        acc += _sigmoid(gl_ref[:, 2 * D:3 * D]) * _dot(oc_ref[...], wc_ref[...], _NN)
        o_ref[...] = acc.astype(BF16)

    return pl.pallas_call(
        body, grid=(S // tm,),
        in_specs=[_row_spec(tm, 512), _row_spec(tm, 512), _row_spec(tm, D), _row_spec(tm, 3 * D)] + _branch_specs(),
        out_specs=_row_spec(tm, D), out_shape=_sds((S, D), BF16), name="merge_fwd", compiler_params=_PAR)(oa, ob, oc, proj, wbr, wbr, wbr)


def _merge_bwd(dm, oa, ob, oc, proj, wbr, tm=256):
    S = oa.shape[0]

    def body(dm_ref, oa_ref, ob_ref, oc_ref, gl_ref, wa_ref, wb_ref, wc_ref, dya_ref, dyb_ref, dyc_ref, dgl_ref, doa_ref, dob_ref, doc_ref):
        dmv = dm_ref[...]
        for i, (o_ref, w_ref, dy_ref, do_ref) in enumerate(
                ((oa_ref, wa_ref, dya_ref, doa_ref), (ob_ref, wb_ref, dyb_ref, dob_ref), (oc_ref, wc_ref, dyc_ref, doc_ref))):
            gt = _sigmoid(gl_ref[:, i * D:(i + 1) * D])
            wv = w_ref[...]
            yv = _dot(o_ref[...], wv, _NN)
            dy = (dmv * gt).astype(BF16)
            dy_ref[...] = dy
            dgl_ref[:, i * D:(i + 1) * D] = (dmv * yv * gt * (1.0 - gt)).astype(BF16)
            do_ref[...] = _dot(dy, wv, _NT)

    rs = _row_spec
    return pl.pallas_call(
        body, grid=(S // tm,),
        in_specs=[rs(tm, D), rs(tm, 512), rs(tm, 512), rs(tm, D), rs(tm, 3 * D)] + _branch_specs(),
        out_specs=(rs(tm, D), rs(tm, D), rs(tm, D), rs(tm, 3 * D), rs(tm, 512), rs(tm, 512), rs(tm, D)),
        out_shape=(_sds((S, D), BF16), _sds((S, D), BF16), _sds((S, D), BF16), _sds((S, 3 * D), BF16), _sds((S, 512)), _sds((S, 512)), _sds((S, D))),
        name="merge_bwd", compiler_params=_PAR)(dm, oa, ob, oc, proj, wbr, wbr, wbr)


def _rope_tab(posb, invf, tm=256):
    S = posb.shape[0]

    def body(p_ref, f_ref, c_ref, s1_ref, s2_ref):
        ang = p_ref[...] * f_ref[...]
        lane = _iota(ang.shape, 1)
        cs, sn = jnp.cos(ang), jnp.sin(ang)
        c_ref[...] = jnp.where(lane < NOPE, 1.0, cs)
        s1_ref[...] = jnp.where((lane >= 64) & (lane < 80), -sn, 0.0)
        s2_ref[...] = jnp.where((lane >= 80) & (lane < 96), sn, 0.0)

    rs = _row_spec(tm, 128)
    return pl.pallas_call(body, grid=(S // tm,), in_specs=[rs, _vec_spec(128)], out_specs=(rs, rs, rs),
                          out_shape=(_sds((S, 128)),) * 3, name="rope_tab", compiler_params=_PAR)(posb, invf)


def _rope(u, C, S1, S2):
    return u * C + pltpu.roll(u, 112, 1) * S1 + pltpu.roll(u, 16, 1) * S2


def _rope_t(dy, C, S1, S2):
    return dy * C + pltpu.roll(dy * S1, 16, 1) + pltpu.roll(dy * S2, 112, 1)


def _seg_sum(v, mask):
    return jnp.sum(jnp.where(mask, v, 0.0), -1, keepdims=True)


def _mla_latents(pq_ref, pkv_ref, wqb_ref, wkvb_ref, qan_ref, kvan_ref):
    ql, kvl = pq_ref[...], pkv_ref[...]
    ckv, kr = kvl[:, 0:KV_RANK], kvl[:, KV_RANK:KV_RANK + 128]
    rq = lax.rsqrt(jnp.mean(ql * ql, -1, keepdims=True) + EPS)
    rkv = lax.rsqrt(jnp.mean(ckv * ckv, -1, keepdims=True) + EPS)
    nq = (ql * rq * qan_ref[...]).astype(BF16)
    nkv = (ckv * rkv * kvan_ref[...]).astype(BF16)
    kv = jnp.concatenate([_dot(nkv, wkvb_ref[k], _NN) for k in range(4)], 1)
    return ql, ckv, kr, rq, rkv, nq, nkv, _dot(nq, wqb_ref[...], _NN), kv


def _mla_specs(tm):
    full = lambda r, n: pl.BlockSpec((r, n), lambda i: (0, 0))
    return ([pl.BlockSpec((tm, 384), lambda i: (i, OQ // 384)), pl.BlockSpec((tm, 384), lambda i: (i, OKV // 384))],
            [full(Q_RANK, D), pl.BlockSpec((4, KV_RANK, 256), lambda i: (0, 0, 0)), _vec_spec(Q_RANK), _vec_spec(KV_RANK), _vec_spec(128), _vec_spec(128)]
            + [_row_spec(tm, 128)] * 3)


def _mla_prep(proj, wqb, wkvb, qan, kvan, wq, wk, rope, tm=256):
    S = proj.shape[0]

    def body(pq_ref, pkv_ref, wqb_ref, wkvb_ref, qan_ref, kvan_ref, wq_ref, wk_ref, c_ref, s1_ref, s2_ref, qf_ref, kf_ref, ve_ref):
        _, _, kr, _, _, _, _, q, kv = _mla_latents(pq_ref, pkv_ref, wqb_ref, wkvb_ref, qan_ref, kvan_ref)
        C, S1, S2, wqv, wkv = c_ref[...], s1_ref[...], s2_ref[...], wq_ref[...], wk_ref[...]
        lane = _iota((tm, 128), 1)
        mn, mr = lane < 64, (lane >= 64) & (lane < 96)
        rrk = lax.rsqrt(_seg_sum(kr * kr, mr) / ROPE_DIM + EPS)
        ykr = _rope(jnp.where(mr, kr * rrk * wkv, 0.0), C, S1, S2)
        for h in range(N_HEADS):
            sl = slice(h * 128, (h + 1) * 128)
            t = q[:, sl]
            rn = lax.rsqrt(_seg_sum(t * t, mn) / NOPE + EPS)
            rr = lax.rsqrt(_seg_sum(t * t, mr) / ROPE_DIM + EPS)
            qf_ref[:, sl] = _rope(t * jnp.where(mn, rn, jnp.where(mr, rr, 0.0)) * wqv, C, S1, S2).astype(BF16)
            t = kv[:, sl]
            rn = lax.rsqrt(_seg_sum(t * t, mn) / NOPE + EPS)
            kf_ref[:, sl] = (jnp.where(mn, t * rn * wkv, 0.0) + ykr).astype(BF16)
            ve_ref[:, sl] = (jnp.where(mn, pltpu.roll(t, 64, 1), 0.0) if h % 2 == 0 else jnp.where(mn, 0.0, t)).astype(BF16)

    pspecs, wspecs = _mla_specs(tm)
    rs = _row_spec(tm, D)
    return pl.pallas_call(body, grid=(S // tm,), in_specs=pspecs + wspecs, out_specs=(rs, rs, rs),
                          out_shape=(_sds((S, D), BF16),) * 3, name="mla_prep", compiler_params=_PAR)(
        proj, proj, wqb, wkvb, qan, kvan, wq, wk, *rope)


def _mla_prep_bwd(proj, dqf, dkf, dve, wqb, wkvb, qan, kvan, wq, wk, rope, tm=256):
    S = proj.shape[0]

    def body(pq_ref, pkv_ref, wqb_ref, wkvb_ref, qan_ref, kvan_ref, wq_ref, wk_ref, c_ref, s1_ref, s2_ref,
             dqf_ref, dkf_ref, dve_ref, dqs_ref, dkvs_ref, dwqb_ref, dwkvb_ref, st_ref, dq_scr, dkv_scr):
        @pl.when(pl.program_id(0) == 0)
        def _():
            dwqb_ref[...] = jnp.zeros_like(dwqb_ref)
            dwkvb_ref[...] = jnp.zeros_like(dwkvb_ref)
            st_ref[...] = jnp.zeros_like(st_ref)

        ql, ckv, kr, rq, rkv, nq, nkv, q, kv = _mla_latents(pq_ref, pkv_ref, wqb_ref, wkvb_ref, qan_ref, kvan_ref)
        C, S1, S2, wqv, wkv = c_ref[...], s1_ref[...], s2_ref[...], wq_ref[...], wk_ref[...]
        lane = _iota((tm, 128), 1)
        mn, mr = lane < 64, (lane >= 64) & (lane < 96)
        dwq = jnp.zeros((1, 128), F32)
        dwk = jnp.zeros((1, 128), F32)
        dykr = jnp.zeros((tm, 128), F32)
        for h in range(N_HEADS):
            sl = slice(h * 128, (h + 1) * 128)
            t = q[:, sl]
            rn = lax.rsqrt(_seg_sum(t * t, mn) / NOPE + EPS)
            rr = lax.rsqrt(_seg_sum(t * t, mr) / ROPE_DIM + EPS)
            scale = jnp.where(mn, rn, jnp.where(mr, rr, 0.0))
            that = t * scale
            du = _rope_t(dqf_ref[:, sl], C, S1, S2)
            dwq += jnp.sum(du * that, 0, keepdims=True)
            g = du * wqv
            gt = g * that
            dq_scr[:, sl] = scale * (g - that * jnp.where(mn, _seg_sum(gt, mn) / NOPE, _seg_sum(gt, mr) / ROPE_DIM))
            t = kv[:, sl]
            rn = lax.rsqrt(_seg_sum(t * t, mn) / NOPE + EPS)
            that = jnp.where(mn, t * rn, 0.0)
            dkf = dkf_ref[:, sl]
            dykr += jnp.where(mr, dkf, 0.0)
            dkn = jnp.where(mn, dkf, 0.0)
            dwk += jnp.sum(dkn * that, 0, keepdims=True)
            g = dkn * wkv
            dve = dve_ref[:, sl]
            dkv_scr[:, sl] = jnp.where(mn, rn * (g - that * (jnp.sum(g * that, -1, keepdims=True) / NOPE)),
                                       pltpu.roll(dve, 64, 1) if h % 2 == 0 else dve)
        rrk = lax.rsqrt(_seg_sum(kr * kr, mr) / ROPE_DIM + EPS)
        that = jnp.where(mr, kr * rrk, 0.0)
        dukr = jnp.where(mr, _rope_t(dykr, C, S1, S2), 0.0)
        dwk += jnp.sum(dukr * that, 0, keepdims=True)
        g = dukr * wkv
        dkr = rrk * (g - that * (jnp.sum(g * that, -1, keepdims=True) / ROPE_DIM))
        dqv, dkvv = dq_scr[...].astype(BF16), dkv_scr[...].astype(BF16)
        dnq = _dot(dqv, wqb_ref[...], _NT)
        dwqb_ref[...] += _dot(nq, dqv, _TN)
        dnkv = jnp.zeros((tm, KV_RANK), F32)
        for k in range(4):
            dnkv += _dot(dkvv[:, k * 256:(k + 1) * 256], wkvb_ref[k], _NT)
            dwkvb_ref[k] += _dot(nkv, dkvv[:, k * 256:(k + 1) * 256], _TN)
        xhat = ql * rq
        st_ref[0:1, 0:Q_RANK] += jnp.sum(dnq * xhat, 0, keepdims=True)
        g = dnq * qan_ref[...]
        dqs_ref[...] = (rq * (g - xhat * jnp.mean(g * xhat, -1, keepdims=True))).astype(BF16)
        xhat = ckv * rkv
        st_ref[1:2, 0:KV_RANK] += jnp.sum(dnkv * xhat, 0, keepdims=True)
        g = dnkv * kvan_ref[...]
        dkvs_ref[:, 0:KV_RANK] = (rkv * (g - xhat * jnp.mean(g * xhat, -1, keepdims=True))).astype(BF16)
        dkvs_ref[:, KV_RANK:KV_RANK + 128] = dkr.astype(BF16)
        st_ref[2:3, 0:128] += dwq
        st_ref[3:4, 0:128] += dwk

    pspecs, wspecs = _mla_specs(tm)
    rs = _row_spec(tm, D)
    full = lambda r, n: pl.BlockSpec((r, n), lambda i: (0, 0))
    return pl.pallas_call(
        body, grid=(S // tm,), in_specs=pspecs + wspecs + [rs, rs, rs],
        out_specs=(_row_spec(tm, 384), _row_spec(tm, 384), full(Q_RANK, D), pl.BlockSpec((4, KV_RANK, 256), lambda i: (0, 0, 0)), full(8, D)),
        out_shape=(_sds((S, 384), BF16), _sds((S, 384), BF16), _sds((Q_RANK, D)), _sds((4, KV_RANK, 256)), _sds((8, D))),
        scratch_shapes=[pltpu.VMEM((tm, D), F32), pltpu.VMEM((tm, D), F32)], name="mla_prep_bwd", compiler_params=_ARB)(
        proj, proj, wqb, wkvb, qan, kvan, wq, wk, *rope, dqf, dkf, dve)


_ATT_SCALE = (NOPE + ROPE_DIM) ** -0.5


def _att_probs(q, k, i, tq):
    n = k.shape[0]
    s = _dot(q, k, _NT) * _ATT_SCALE
    tri = _iota((tq, tq), 1) <= _iota((tq, tq), 0)
    diag = jnp.where(tri, s[:, n - tq:], -1e30)
    s = diag if n == tq else jnp.concatenate([s[:, :n - tq], diag], 1)
    p = jnp.exp(s - jnp.max(s, -1, keepdims=True))
    return p * (1.0 / jnp.sum(p, -1, keepdims=True))


def _attn_fwd(qf, kf, ve, tq=256):
    S = qf.shape[0]

    def body(q_ref, k_ref, v_ref, o_ref):
        for i in range(S // tq):
            n, rows = (i + 1) * tq, slice(i * tq, (i + 1) * tq)
            acc = jnp.zeros((tq, 128), F32)
            for hh in range(2):
                sl = slice(hh * 128, (hh + 1) * 128)
                p = _att_probs(q_ref[rows, sl], k_ref[0:n, sl], i, tq)
                acc += _dot(p.astype(BF16), v_ref[0:n, sl], _NN)
            o_ref[rows, :] = acc.astype(BF16)

    ps = pl.BlockSpec((S, 256), lambda h: (0, h))
    return pl.pallas_call(body, grid=(N_HEADS // 2,), in_specs=[ps, ps, ps], out_specs=pl.BlockSpec((S, 128), lambda h: (0, h)),
                          out_shape=_sds((S, 512), BF16), name="attn_fwd", compiler_params=_PAR)(qf, kf, ve)


def _attn_bwd(qf, kf, ve, do, tq=256):
    S = qf.shape[0]

    def body(q_ref, k_ref, v_ref, do_ref, dq_ref, dk_ref, dv_ref):
        dk_ref[...] = jnp.zeros_like(dk_ref)
        dv_ref[...] = jnp.zeros_like(dv_ref)
        for i in range(S // tq):
            n, rows = (i + 1) * tq, slice(i * tq, (i + 1) * tq)
            dob = do_ref[rows, :].astype(BF16)
            for hh in range(2):
                sl = slice(hh * 128, (hh + 1) * 128)
                q, k = q_ref[rows, sl], k_ref[0:n, sl]
                p = _att_probs(q, k, i, tq)
                dv_ref[0:n, sl] += _dot(p.astype(BF16), dob, _TN)
                dp = _dot(dob, v_ref[0:n, sl], _NT)
                ds = (p * (dp - jnp.sum(dp * p, -1, keepdims=True)) * _ATT_SCALE).astype(BF16)
                dq_ref[rows, sl] = _dot(ds, k, _NN)
                dk_ref[0:n, sl] += _dot(ds, q, _TN)

    ps = pl.BlockSpec((S, 256), lambda h: (0, h))
    return pl.pallas_call(body, grid=(N_HEADS // 2,), in_specs=[ps, ps, ps, pl.BlockSpec((S, 128), lambda h: (0, h))], out_specs=(ps, ps, ps),
                          out_shape=(_sds((S, D)),) * 3, name="attn_bwd", compiler_params=_PAR)(qf, kf, ve, do)


def _softplus(x):
    return jnp.maximum(x, 0.0) + jnp.log1p(jnp.exp(-jnp.abs(x)))


def _ssd_chunk(xc_ref, dtr_ref, dtb_ref, al_ref, e_ref):
    L = SSD_L
    a = -jnp.exp(al_ref[...])
    dtp = _softplus(dtr_ref[...] + dtb_ref[...])
    causal = _iota((L, L), 1) <= _iota((L, L), 0)
    cs = _dot_hi(causal.astype(F32), dtp * a)
    E = e_ref[...]
    dtx, csx = _dot_hi(dtp, E), _dot_hi(cs, E)
    X = xc_ref[:, 0:SSD_INNER]
    Xd = X * dtx
    dec_out = jnp.exp(csx)
    dec_st = jnp.exp(csx[L - 1:L, :] - csx)
    return a, dtp, causal, cs, cs.T, dtx, X, Xd, dec_out, dec_st


def _ssd_decay(causal, cs, cs_row, h):
    diff = cs[:, h:h + 1] - cs_row[h:h + 1, :]
    return jnp.where(causal, jnp.exp(jnp.where(causal, diff, 0.0)), 0.0)


def _ssd_groups(xc_ref, g):
    b0, c0 = SSD_INNER + g * SSD_N, SSD_INNER + 2 * SSD_N + g * SSD_N
    return xc_ref[:, b0:b0 + SSD_N].astype(BF16), xc_ref[:, c0:c0 + SSD_N].astype(BF16)


def _pair_decay(cs, pair):
    L = SSD_L
    return jnp.where(_iota((128, 128), 0) < 64, jnp.exp(cs[L - 1:L, 2 * pair:2 * pair + 1]), jnp.exp(cs[L - 1:L, 2 * pair + 1:2 * pair + 2]))


def _ssd_in_specs(nc, rev):
    idx = (lambda c: nc - 1 - c) if rev else (lambda c: c)
    return [pl.BlockSpec((SSD_L, SSD_CONV_DIM), lambda c: (idx(c), 0)), pl.BlockSpec((SSD_L, 128), lambda c: (idx(c), ODT // 128)),
            _vec_spec(128), _vec_spec(128), _vec_spec(SSD_INNER), pl.BlockSpec((128, SSD_INNER), lambda c: (0, 0))]


def _ssd_core(xc, proj, dtb, alog, dskip, E):
    S = xc.shape[0]
    nc = S // SSD_L

    def body(xc_ref, dtr_ref, dtb_ref, al_ref, dx_ref, e_ref, y_ref, hp_ref, h_scr):
        @pl.when(pl.program_id(0) == 0)
        def _():
            h_scr[...] = jnp.zeros_like(h_scr)

        hp_ref[0] = h_scr[...]
        _, _, causal, cs, cs_row, _, X, Xd, dec_out, dec_st = _ssd_chunk(xc_ref, dtr_ref, dtb_ref, al_ref, e_ref)
        Xs = Xd * dec_st
        lane = _iota((SSD_L, 128), 1)
        for g in range(2):
            Bg, Cg = _ssd_groups(xc_ref, g)
            CB = _dot(Cg, Bg, _NT)
            for pr in range(4):
                pair = g * 4 + pr
                psl = slice(pair * 128, (pair + 1) * 128)
                Xdp = Xd[:, psl].astype(BF16)
                r0 = _dot((CB * _ssd_decay(causal, cs, cs_row, 2 * pair)).astype(BF16), Xdp, _NN)
                r1 = _dot((CB * _ssd_decay(causal, cs, cs_row, 2 * pair + 1)).astype(BF16), Xdp, _NN)
                Hp = h_scr[psl, :]
                W = _dot(Cg, Hp.astype(BF16), _NT)
                y_ref[:, psl] = jnp.where(lane < 64, r0, r1) + W * dec_out[:, psl] + X[:, psl] * dx_ref[:, psl]
                h_scr[psl, :] = Hp * _pair_decay(cs, pair) + _dot(Xs[:, psl].astype(BF16), Bg, _TN)

    return pl.pallas_call(
        body, grid=(nc,), in_specs=_ssd_in_specs(nc, False),
        out_specs=(pl.BlockSpec((SSD_L, SSD_INNER), lambda c: (c, 0)), pl.BlockSpec((1, SSD_INNER, SSD_N), lambda c: (c, 0, 0))),
        out_shape=(_sds((S, SSD_INNER)), _sds((nc, SSD_INNER, SSD_N))), scratch_shapes=[pltpu.VMEM((SSD_INNER, SSD_N), F32)],
        name="ssd_core", compiler_params=_ARB)(xc, proj, dtb, alog, dskip, E)


def _ssd_core_bwd(xc, proj, hprev, dy, dtb, alog, dskip, E, ET):
    S = xc.shape[0]
    nc = S // SSD_L
    L = SSD_L

    def body(xc_ref, dtr_ref, dtb_ref, al_ref, dx_ref, e_ref, et_ref, hp_ref, dy_ref, dxc_ref, ddt_ref, st_ref, dh_scr, acc_scr):
        step = pl.program_id(0)

        @pl.when(step == 0)
        def _():
            dh_scr[...] = jnp.zeros_like(dh_scr)
            acc_scr[...] = jnp.zeros_like(acc_scr)
            st_ref[...] = jnp.zeros_like(st_ref)

        a, dtp, causal, cs, cs_row, dtx, X, Xd, dec_out, dec_st = _ssd_chunk(xc_ref, dtr_ref, dtb_ref, al_ref, e_ref)
        lane = _iota((L, 128), 1)
        sub = _iota((128, L), 0)
        dcs_col = jnp.zeros((L, 128), F32)
        dcs_row = jnp.zeros((128, L), F32)
        dcs_last = jnp.zeros((1, 128), F32)
        dcsx, ddtx, dlastx = [], [], []
        for g in range(2):
            Bg, Cg = _ssd_groups(xc_ref, g)
            CB = _dot(Cg, Bg, _NT)
            dCB = jnp.zeros((L, L), F32)
            dB = jnp.zeros((L, SSD_N), F32)
            dC = jnp.zeros((L, SSD_N), F32)
            for pr in range(4):
                pair = g * 4 + pr
                psl = slice(pair * 128, (pair + 1) * 128)
                dY, Xp, Xdp, dop, dsp = dy_ref[:, psl], X[:, psl], Xd[:, psl], dec_out[:, psl], dec_st[:, psl]
                Xdb = Xdp.astype(BF16)
                acc_scr[0:1, psl] += jnp.sum(dY * Xp, 0, keepdims=True)
                Hp = hp_ref[0, psl, :]
                Hb = Hp.astype(BF16)
                dW = (dY * dop).astype(BF16)
                dcx = dY * _dot(Cg, Hb, _NT) * dop
                dC += _dot(dW, Hb, _NN)
                dHp = _dot(dW, Cg, _TN)
                dHn = dh_scr[psl, :]
                cd = _pair_decay(cs, pair)
                dh_scr[psl, :] = dHp + dHn * cd
                rsum = jnp.sum(dHn * Hp * cd, -1, keepdims=True)
                half = _iota((128, 1), 0) < 64
                s0 = jnp.sum(jnp.where(half, rsum, 0.0), 0, keepdims=True)
                s1 = jnp.sum(jnp.where(half, 0.0, rsum), 0, keepdims=True)
                lane1 = _iota((1, 128), 1)
                dcs_last += jnp.where(lane1 == 2 * pair, s0, 0.0) + jnp.where(lane1 == 2 * pair + 1, s1, 0.0)
                dHb = dHn.astype(BF16)
                dXs = _dot(Bg, dHb, _NT)
                dB += _dot((Xdp * dsp).astype(BF16), dHb, _NN)
                dXd = dXs * dsp
                e_st = dXs * Xdp * dsp
                dcx -= e_st
                dlastx.append(jnp.sum(e_st, 0, keepdims=True))
                for i in range(2):
                    h = 2 * pair + i
                    Dm = _ssd_decay(causal, cs, cs_row, h)
                    M = CB * Dm
                    dYm = jnp.where((lane < 64) if i == 0 else (lane >= 64), dY, 0.0).astype(BF16)
                    dM = _dot(dYm, Xdb, _NT)
                    dXd += _dot(M.astype(BF16), dYm, _TN)
                    dCB += dM * Dm
                    Em = dM * M
                    dcs_col += jnp.where(lane == h, jnp.sum(Em, -1, keepdims=True), 0.0)
                    dcs_row += jnp.where(sub == h, jnp.sum(Em, 0, keepdims=True), 0.0)
                dxc_ref[:, psl] = dY * dx_ref[:, psl] + dXd * dtx[:, psl]
                ddtx.append(dXd * Xp)
                dcsx.append(dcx)
            dCBb = dCB.astype(BF16)
            b0, c0 = SSD_INNER + g * SSD_N, SSD_INNER + 2 * SSD_N + g * SSD_N
            dxc_ref[:, b0:b0 + SSD_N] = dB + _dot(dCBb, Cg, _TN)
            dxc_ref[:, c0:c0 + SSD_N] = dC + _dot(dCBb, Bg, _NN)
        ET = et_ref[...]
        dcs = dcs_col - dcs_row.T + _dot_hi(jnp.concatenate(dcsx, 1), ET)
        dlast = dcs_last + _dot_hi(jnp.broadcast_to(jnp.concatenate(dlastx, 1), (8, SSD_INNER)), ET)[0:1, :]
        dcs += jnp.where(_iota((L, 128), 0) == L - 1, dlast, 0.0)
        dda = _dot_hi((_iota((L, L), 1) >= _iota((L, L), 0)).astype(F32), dcs)
        ddtp = dda * a + _dot_hi(jnp.concatenate(ddtx, 1), ET)
        draw = ddtp * _sigmoid(dtr_ref[...] + dtb_ref[...])
        ddt_ref[...] = draw.astype(BF16)
        st_ref[0:1, :] += jnp.sum(draw, 0, keepdims=True)
        st_ref[1:2, :] += jnp.sum(dda * dtp, 0, keepdims=True) * a

        @pl.when(step == nc - 1)
        def _():
            st_ref[2:3, :] = _dot_hi(acc_scr[...], ET)[0:1, :]

    rev = lambda c: (nc - 1 - c, 0)
    return pl.pallas_call(
        body, grid=(nc,),
        in_specs=_ssd_in_specs(nc, True) + [pl.BlockSpec((SSD_INNER, 128), lambda c: (0, 0)),
                                            pl.BlockSpec((1, SSD_INNER, SSD_N), lambda c: (nc - 1 - c, 0, 0)),
                                            pl.BlockSpec((L, SSD_INNER), rev)],
        out_specs=(pl.BlockSpec((L, SSD_CONV_DIM), rev), pl.BlockSpec((L, 128), rev), pl.BlockSpec((8, 128), lambda c: (0, 0))),
        out_shape=(_sds((S, SSD_CONV_DIM)), _sds((S, 128), BF16), _sds((8, 128))),
        scratch_shapes=[pltpu.VMEM((SSD_INNER, SSD_N), F32), pltpu.VMEM((8, SSD_INNER), F32)],
        name="ssd_core_bwd", compiler_params=_ARB)(xc, proj, dtb, alog, dskip, E, ET, hprev, dy)


def _ssd_post(y, proj, nw, tm=256):
    S = y.shape[0]

    def body(y_ref, z_ref, nw_ref, o_ref):
        for g in range(2):
            sl = slice(g * 512, (g + 1) * 512)
            gated = y_ref[:, sl] * _silu(z_ref[:, sl])
            r = lax.rsqrt(jnp.mean(gated * gated, -1, keepdims=True) + EPS)
            o_ref[:, sl] = (gated * r * nw_ref[:, sl]).astype(BF16)

    return pl.pallas_call(
        body, grid=(S // tm,), in_specs=[_row_spec(tm, D), pl.BlockSpec((tm, D), lambda i: (i, OZ // D)), _vec_spec(D)],
        out_specs=_row_spec(tm, D), out_shape=_sds((S, D), BF16), name="ssd_post", compiler_params=_PAR)(y, proj, nw)


def _ssd_post_bwd(doc, y, proj, nw, tm=256):
    S = y.shape[0]

    def body(d_ref, y_ref, z_ref, nw_ref, dy_ref, dz_ref, st_ref):
        @pl.when(pl.program_id(0) == 0)
        def _():
            st_ref[...] = jnp.zeros_like(st_ref)

        for g in range(2):
            sl = slice(g * 512, (g + 1) * 512)
            yv, zv, dv = y_ref[:, sl], z_ref[:, sl], d_ref[:, sl]
            sz = _silu(zv)
            gated = yv * sz
            r = lax.rsqrt(jnp.mean(gated * gated, -1, keepdims=True) + EPS)
            ghat = gated * r
            st_ref[0:1, sl] += jnp.sum(dv * ghat, 0, keepdims=True)
            gg = dv * nw_ref[:, sl]
            dg = r * (gg - ghat * jnp.mean(gg * ghat, -1, keepdims=True))
            dy_ref[:, sl] = dg * sz
            dz_ref[:, sl] = (dg * yv * _dsilu(zv)).astype(BF16)

    zs = pl.BlockSpec((tm, D), lambda i: (i, OZ // D))
    return pl.pallas_call(
        body, grid=(S // tm,), in_specs=[_row_spec(tm, D), _row_spec(tm, D), zs, _vec_spec(D)],
        out_specs=(_row_spec(tm, D), _row_spec(tm, D), _vec_spec(D, 8)), out_shape=(_sds((S, D)), _sds((S, D), BF16), _sds((8, D))),
        name="ssd_post_bwd", compiler_params=_ARB)(doc, y, proj, nw)


def _row(v, n=None):
    v = v.astype(F32).reshape(1, -1)
    return v if n is None else jnp.pad(v, ((0, 0), (0, n - v.shape[1])))


_IN_SEGMENTS = [(0, 384, OQ), (384, 640, OKV), (640, 672, OKV + 320), (672, 1184, OP), (1184, 2208, OZ), (2208, 3744, OX),
                (3744, 3760, ODT), (3760, IN_DIM, OG)]
_IN_ZEROS = [(OKV + 256, OKV + 320), (OKV + 352, OKV + 384), (ODT + 16, ODT + 128)]


def _in_pieces():
    w, out = IN_DIM // 4, []
    for a, b, d in _IN_SEGMENTS:
        while a < b:
            k = a // w
            e = min(b, (k + 1) * w)
            out.append((k, a - k * w, e - k * w, d))
            d, a = d + e - a, e
    return out


def _win_layout(w_in4, tm=256):
    def body(w_ref, o_ref):
        for k, s0, s1, d in _in_pieces():
            o_ref[:, d:d + s1 - s0] = w_ref[k, :, s0:s1]
        for z0, z1 in _IN_ZEROS:
            o_ref[:, z0:z1] = jnp.zeros((tm, z1 - z0), o_ref.dtype)

    return pl.pallas_call(
        body, grid=(D // tm,), in_specs=[pl.BlockSpec((4, tm, IN_DIM // 4), lambda i: (0, i, 0))],
        out_specs=pl.BlockSpec((tm, IN_PAD), lambda i: (i, 0)), out_shape=_sds((D, IN_PAD), w_in4.dtype), name="win_layout",
        compiler_params=_PAR)(w_in4)


def _win_unlayout(dwin, tm=256):
    def body(d_ref, o_ref):
        for k, s0, s1, d in _in_pieces():
            o_ref[k, :, s0:s1] = d_ref[:, d:d + s1 - s0]

    return pl.pallas_call(
        body, grid=(D // tm,), in_specs=[pl.BlockSpec((tm, IN_PAD), lambda i: (i, 0))],
        out_specs=pl.BlockSpec((4, tm, IN_DIM // 4), lambda i: (0, i, 0)), out_shape=_sds((4, D, IN_DIM // 4), dwin.dtype),
        name="win_unlayout", compiler_params=_PAR)(dwin)


def _prep_late(p):
    return dict(wbr=p["w_branch"].astype(BF16), wo=p["w_out"].astype(BF16), wup4=p["ffn_up4"].astype(BF16), wdn=p["ffn_down"].astype(BF16))


def _prep_layer(p, late=None):
    win = _win_layout(p["w_in4"].astype(BF16))
    wqb = jnp.transpose(p["w_q_b4"].astype(BF16).reshape(4, Q_RANK, 2, 96), (1, 0, 2, 3))
    return dict(
        win=win, wqb=jnp.pad(wqb, ((0, 0), (0, 0), (0, 0), (0, 32))).reshape(Q_RANK, D), wkvb4=p["w_kv_b4"].astype(BF16),
        late=late if late is not None else (lambda after: _prep_late(p)),
        nw1=_row(p["norm1_w"]), nw2=_row(p["norm2_w"]), qan=_row(p["q_a_norm"]), kvan=_row(p["kv_a_norm"]),
        wq=_row(p["q_norm"], 128), wk=_row(p["k_norm"], 128), pool_w=p["pool_w"].astype(F32), pool_scale=_row(p["pool_scale"]),
        cw=p["ssd_conv_w"].astype(F32), cb=_row(p["ssd_conv_b"]), dtb=_row(p["ssd_dt_bias"], 128), alog=_row(p["ssd_a_log"], 128),
        dskip=_row(jnp.repeat(p["ssd_d"].astype(F32), SSD_P)), snw=_row(p["ssd_norm_w"]),
        fcw=p["ffn_conv_w"].astype(F32), fcb=_row(p["ffn_conv_b"]))


def _layer_fwd(x, mod8, W, rope, E, tag):
    sh1, sc1, g1, sh2, sc2, g2 = (mod8[i:i + 1] for i in range(6))
    h1 = _ln_mod(x, W["nw1"], sc1, sh1, name=f"ln1_{tag}")
    proj = _mm(h1, W["win"], tn=640, tk=1024, name=f"proj_{tag}")
    qf, kf, ve = _mla_prep(proj, W["wqb"], W["wkvb4"], W["qan"], W["kvan"], W["wq"], W["wk"], rope)
    oa = _attn_fwd(qf, kf, ve)
    ob = _pool_fwd(proj, W["pool_w"], W["pool_scale"])
    xc = _ssd_pre(proj, W["cw"], W["cb"])
    y, hprev = _ssd_core(xc, proj, W["dtb"], W["alog"], W["dskip"], E)
    oc = _ssd_post(y, proj, W["snw"])
    W.update(W["late"](oc))
    merged = _merge_fwd(oa, ob, oc, proj, W["wbr"])
    x1, out1 = _mm(merged, W["wo"], tk=1024, res=x, gate=g1, name=f"wout_{tag}")
    h2 = _ln_mod(x1, W["nw2"], sc2, sh2, name=f"ln2_{tag}")
    up = _up_fwd(h2, W["wup4"], name=f"up_{tag}")
    act = _ffn_act(up, W["fcw"], W["fcb"])
    x2, out2 = _mm(act, W["wdn"], tm=1024, res=x1, gate=g2, name=f"down_{tag}")
    saved = dict(x=x, h1=h1, proj=proj, qf=qf, kf=kf, ve=ve, oa=oa, ob=ob, oc=oc, xc=xc, hprev=hprev, y=y, merged=merged,
                 out1=out1, x1=x1, h2=h2, up=up, act=act, out2=out2)
    return x2, saved


def _layer_bwd(dx2, sv, mod8, W, rope, E, ET, tag, emit=None):
    sc1, g1, sc2, g2 = mod8[1:2], mod8[2:3], mod8[4:5], mod8[5:6]
    proj = sv["proj"]
    dz2, dg2 = _gate_bwd(dx2, sv["out2"], g2, name=f"gate2_bwd_{tag}")
    dact = _mm(dz2, W["wdn"], "nt", tn=1408, tk=1024, name=f"down_dx_{tag}")
    dwdn = _mm(sv["act"], dz2, "tn", tm=1408, name=f"down_dw_{tag}")
    dup2, dfcw, dfcb = _ffn_act_bwd(sv["up"], dact, W["fcw"], W["fcb"])
    dh2 = _up_dx(dup2, W["wup4"], name=f"up_dx_{tag}")
    dwup4 = _up_dw(sv["h2"], dup2, name=f"up_dw_{tag}")
    dx1, st2 = _ln_mod_bwd(sv["x1"], dh2, dx2, W["nw2"], sc2, name=f"ln2_bwd_{tag}")
    dz1, dg1 = _gate_bwd(dx1, sv["out1"], g1, name=f"gate1_bwd_{tag}")
    dmerged = _mm(dz1, W["wo"], "nt", tk=1024, name=f"wout_dx_{tag}")
    dwo = _mm(sv["merged"], dz1, "tn", name=f"wout_dw_{tag}")
    dya, dyb, dyc, dgl, doa, dob, doc = _merge_bwd(dmerged, sv["oa"], sv["ob"], sv["oc"], proj, W["wbr"])
    dwba = _mm(sv["oa"], dya, "tn", name=f"wba_dw_{tag}")
    dwbb = _mm(sv["ob"], dyb, "tn", name=f"wbb_dw_{tag}")
    dwbc = _mm(sv["oc"], dyc, "tn", name=f"wbc_dw_{tag}")
    late = dict(w_branch=jnp.concatenate([dwba, dwbb, dwbc], 0).reshape(4, 512, D), w_out=dwo.reshape(4, 256, D), ffn_up=dwup4,
                ffn_down=dwdn.reshape(4, FFN // 4, D))
    snw = W["snw"]
    if emit is not None:
        token = emit(late)
        if token is not None:
            snw, doa = snw + token, doa + token
    dy, dzs, st_post = _ssd_post_bwd(doc, sv["y"], proj, snw)
    dxc, ddt, st_ssd = _ssd_core_bwd(sv["xc"], proj, sv["hprev"], dy, W["dtb"], W["alog"], W["dskip"], E, ET)
    dxbc, dcw, dcb = _ssd_pre_bwd(proj, dxc, W["cw"], W["cb"])
    dpool, dpw, dps = _pool_bwd(proj, dob, W["pool_w"], W["pool_scale"])
    dqf, dkf, dve = _attn_bwd(sv["qf"], sv["kf"], sv["ve"], doa)
    dqs, dkvs, dwqb, dwkvb4, st_mla = _mla_prep_bwd(proj, dqf, dkf, dve, W["wqb"], W["wkvb4"], W["qan"], W["kvan"], W["wq"], W["wk"], rope)
    dproj = jnp.concatenate([dgl, dzs, dxbc, dpool, dqs, dkvs, ddt], 1)
    dh1 = _mm(dproj, W["win"], "nt", tk=1408, name=f"proj_dx_{tag}")
    dwin = _mm(sv["h1"], dproj, "tn", tn=640, name=f"proj_dw_{tag}")
    dx, st1 = _ln_mod_bwd(sv["x"], dh1, dx1, W["nw1"], sc1, name=f"ln1_bwd_{tag}")
    grads = dict(
        norm1_w=st1[2], norm2_w=st2[2], w_in=_win_unlayout(dwin),
        q_a_norm=st_mla[0, :Q_RANK], kv_a_norm=st_mla[1, :KV_RANK], q_norm=st_mla[2, :96], k_norm=st_mla[3, :96],
        w_q_b=jnp.transpose(dwqb.reshape(Q_RANK, 4, 2, 128)[:, :, :, :96], (1, 0, 2, 3)).reshape(4, Q_RANK, 192), w_kv_b=dwkvb4,
        pool_w=dpw, pool_scale=dps[0], ssd_conv_w=dcw, ssd_conv_b=dcb[0],
        ssd_dt_bias=st_ssd[0, :SSD_HEADS], ssd_a_log=st_ssd[1, :SSD_HEADS], ssd_d=st_ssd[2, :SSD_HEADS], ssd_norm_w=st_post[0],
        ffn_conv_w=jnp.transpose(dfcw, (1, 0, 2)).reshape(3, 2 * FFN), ffn_conv_b=dfcb.reshape(2 * FFN), **late)
    dmod = jnp.concatenate([st1[0:2], dg1[0:1], st2[0:2], dg2[0:1]], 0)
    return dx, grads, dmod


def _ssd_expand():
    E = (jnp.arange(SSD_INNER)[None, :] // SSD_P == jnp.arange(128)[:, None]).astype(F32)
    return E, E.T


def _rope_tables(positions):
    inv_freq = ROPE_THETA ** (-jnp.arange(0, ROPE_DIM, 2, dtype=F32) / ROPE_DIM)
    invf = jnp.concatenate([jnp.zeros((NOPE,), F32), inv_freq, inv_freq, jnp.zeros((32,), F32)]).reshape(1, 128)
    posb = jnp.broadcast_to(positions.astype(F32)[:, None], (positions.shape[0], 128))
    return _rope_tab(posb, invf)


def _local_step(x, target, positions, mods, get_layer, bwd_mod=None, emit=None, done=None):
    rope = _rope_tables(positions)
    E, ET = _ssd_expand()
    Ws, saved, h = [], [], x
    for l in range(2):
        Ws.append(_prep_layer(*get_layer(l, h)))
        h, sv = _layer_fwd(h, mods[l], Ws[l], rope, E, l)
        saved.append(sv)
    dy, lpart = _loss_grad(h, target)
    grads, dmods = [None, None], [None, None]
    for l in (1, 0):
        mod8 = mods[l] if bwd_mod is None else bwd_mod(l)
        dy, grads[l], dmods[l] = _layer_bwd(dy, saved[l], mod8, Ws[l], rope, E, ET, l, None if emit is None else functools.partial(emit, l))
        if done is not None:
            done(l, grads[l])
    return lpart[0, 0], dy, grads, dmods


_ANY = pl.BlockSpec(memory_space=pl.ANY)
_VMEM = pl.BlockSpec(memory_space=pltpu.VMEM)


def _place():
    x, y, c = lax.axis_index("x"), lax.axis_index("y"), lax.axis_index("c")
    return x, y, c, [(1 - x, y), (x, 1 - y), (1 - x, 1 - y)]


def _allgather8(v, name):
    m_per, n = v.shape

    def body(x_ref, out_ref, send_sems, recv_sems, local_sem):
        x, y, c, chips = _place()
        me, sibling = (x, y, c), (x, y, 1 - c)

        def rows(px, py, pc):
            return out_ref.at[pl.ds((4 * px + 2 * py + pc) * m_per, m_per), :]

        def copy(k, block, to, src=None):
            return pltpu.make_async_remote_copy(src_ref=rows(*block) if src is None else src, dst_ref=rows(*block),
                                                send_sem=send_sems.at[k], recv_sem=recv_sems.at[k], device_id=to, device_id_type=MESH)

        mine = pltpu.make_async_copy(x_ref, rows(*me), local_sem)
        mine.start()
        first = [copy(0, me, sibling, src=x_ref)] + [copy(1 + j, me, (*chip, c), src=x_ref) for j, chip in enumerate(chips)]
        for cp in first:
            cp.start()
        passed = [copy(4 + j, (*chip, c), sibling) for j, chip in enumerate(chips)]
        for j, chip in enumerate(chips):
            copy(1 + j, (*chip, c), me).wait_recv()
            passed[j].start()
        copy(0, sibling, me).wait_recv()
        for j, chip in enumerate(chips):
            copy(4 + j, (*chip, 1 - c), me).wait_recv()
        for cp in first + passed:
            cp.wait_send()
        mine.wait()

    return pl.pallas_call(
        body, out_shape=_sds((8 * m_per, n), v.dtype), in_specs=[_VMEM], out_specs=_VMEM,
        scratch_shapes=[pltpu.SemaphoreType.DMA((7,)), pltpu.SemaphoreType.DMA((7,)), pltpu.SemaphoreType.DMA], name=name)(v)


def _sems(n):
    return [pltpu.SemaphoreType.DMA((n,)), pltpu.SemaphoreType.DMA((n,))]


_HBM = pl.BlockSpec(memory_space=pltpu.HBM)
_SEM = pl.BlockSpec(memory_space=pltpu.SEMAPHORE)
_EFFECT = pltpu.CompilerParams(has_side_effects=pltpu.SideEffectType.DATAFLOW_SIDE_EFFECTING)


def _ici_copy(src_refs, land_refs, send_sems, recv_sems, a, j, slices, incoming):
    x, y, c, chips = _place()
    me, other = 2 * x + y, 2 * chips[j][0] + chips[j][1]
    src, dst = slices(src_refs[a], land_refs[a], other, me, c) if incoming else slices(src_refs[a], land_refs[a], me, other, c)
    return pltpu.make_async_remote_copy(src_ref=src, dst_ref=dst, send_sem=send_sems.at[3 * a + j], recv_sem=recv_sems.at[3 * a + j],
                                        device_id=(*chips[j], c), device_id_type=MESH)


def _ici_start(srcs, land_shapes, slices, after, name):
    na = len(srcs)

    def body(*refs):
        src_refs, land_refs, send_sems, recv_sems = refs[:na], refs[na:2 * na], refs[2 * na + 1], refs[2 * na + 2]
        for a in range(na):
            for j in range(3):
                _ici_copy(src_refs, land_refs, send_sems, recv_sems, a, j, slices, False).start()
        refs[-1][...] = jnp.zeros_like(refs[-1])

    hbm = lambda v: pltpu.with_memory_space_constraint(v, pltpu.HBM)
    lands = [hbm(lax.empty(s.shape, s.dtype)) for s in land_shapes]
    return pl.pallas_call(
        body, name=name,
        out_shape=(pltpu.SemaphoreType.DMA((3 * na,)), pltpu.SemaphoreType.DMA((3 * na,)), *[pltpu.HBM(v.shape, v.dtype) for v in srcs],
                   *[pltpu.HBM(s.shape, s.dtype) for s in land_shapes], _sds((8, 128))),
        in_specs=[_HBM] * (2 * na) + [_ANY], out_specs=(_SEM, _SEM, *[_HBM] * (2 * na), _VMEM),
        input_output_aliases={i: 2 + i for i in range(2 * na)}, compiler_params=_EFFECT)(*[hbm(v) for v in srcs], *lands, after)


def _ici_wait(handle, slices, after, name):
    na = (len(handle) - 3) // 2

    def body(*refs):
        src_refs, land_refs, send_sems, recv_sems = refs[:na], refs[na:2 * na], refs[2 * na], refs[2 * na + 1]
        for a in range(na):
            for j in range(3):
                _ici_copy(src_refs, land_refs, send_sems, recv_sems, a, j, slices, False).wait_send()
                _ici_copy(src_refs, land_refs, send_sems, recv_sems, a, j, slices, True).wait_recv()

    thru = handle[2:2 + 2 * na]
    outs = pl.pallas_call(
        body, name=name, out_shape=[pltpu.HBM(v.shape, v.dtype) for v in thru], in_specs=[_HBM] * (2 * na) + [_SEM, _SEM, _ANY],
        out_specs=[_HBM] * (2 * na), input_output_aliases={i: i for i in range(2 * na)}, compiler_params=_EFFECT)(
        *thru, handle[0], handle[1], after)
    return outs[:na], outs[na:]


def _gather_slices(p_ref, land_ref, sender, receiver, c):
    r2 = p_ref.shape[0] // 2
    return p_ref.at[pl.ds(c * r2, r2), :], land_ref.at[sender, pl.ds(c * r2, r2), :]


def _scatter_slices(a_ref, t_ref, sender, receiver, c):
    return a_ref.at[receiver], t_ref.at[sender]


def _gather_start(arrs, after, name):
    return _ici_start(arrs, [_sds((4,) + v.shape, v.dtype) for v in arrs], _gather_slices, after, name)


def _gather_finish(handle, after, name):
    arrs, stacks = _ici_wait(handle, _gather_slices, after, name + "_wait")
    na = len(stacks)

    def body(*refs):
        s_refs, o_refs, (send_sems, recv_sems) = refs[:na], refs[na:2 * na], refs[2 * na:]
        x, y, c, chips = _place()

        def copy(a, j, cc, to):
            r2 = s_refs[a].shape[1] // 2
            at = (2 * chips[j][0] + chips[j][1], pl.ds(cc * r2, r2), slice(None))
            return pltpu.make_async_remote_copy(src_ref=s_refs[a].at[at], dst_ref=o_refs[a].at[at], send_sem=send_sems.at[3 * a + j],
                                                recv_sem=recv_sems.at[3 * a + j], device_id=to, device_id_type=MESH)

        passed = [copy(a, j, c, (x, y, 1 - c)) for a in range(na) for j in range(3)]
        for cp in passed:
            cp.start()
        for a in range(na):
            for j in range(3):
                copy(a, j, 1 - c, (x, y, c)).wait_recv()
        for cp in passed:
            cp.wait_send()

    stacks = pl.pallas_call(
        body, out_shape=[_sds(v.shape, v.dtype) for v in stacks], in_specs=[_ANY] * na, out_specs=[_ANY] * na,
        input_output_aliases={i: i for i in range(na)}, scratch_shapes=_sems(3 * na), name=name + "_pass")(*stacks)
    chip = 2 * lax.axis_index("x") + lax.axis_index("y")
    return [lax.dynamic_update_slice(s, v[None], (chip, 0, 0)) for s, v in zip(stacks, arrs)]


def _send_halves(gs, name):
    na = len(gs)

    def body(*refs):
        g_refs, o_refs, (send_sems, recv_sems) = refs[:na], refs[na:2 * na], refs[2 * na:]
        x, y, c, _ = _place()
        cps = []
        for a in range(na):
            r2 = g_refs[a].shape[1] // 2
            cps.append(pltpu.make_async_remote_copy(src_ref=g_refs[a].at[:, pl.ds((1 - c) * r2, r2), :], dst_ref=o_refs[a],
                                                    send_sem=send_sems.at[a], recv_sem=recv_sems.at[a],
                                                    device_id=(x, y, 1 - c), device_id_type=MESH))
        for cp in cps:
            cp.start()
        for cp in cps:
            cp.wait()

    return pl.pallas_call(
        body, out_shape=[_sds((4, v.shape[1] // 2, v.shape[2]), v.dtype) for v in gs], in_specs=[_ANY] * na, out_specs=[_ANY] * na,
        scratch_shapes=_sems(na), name=name)(*gs)


def _join_halves(fs, name):
    na = len(fs)

    def body(*refs):
        f_refs, o_refs, (send_sems, recv_sems) = refs[:na], refs[na:2 * na], refs[2 * na:]
        x, y, c, _ = _place()

        def copy(a, cc, to):
            r2 = f_refs[a].shape[0]
            return pltpu.make_async_remote_copy(src_ref=f_refs[a], dst_ref=o_refs[a].at[pl.ds(cc * r2, r2), :], send_sem=send_sems.at[a],
                                                recv_sem=recv_sems.at[a], device_id=to, device_id_type=MESH)

        cps = [copy(a, c, (x, y, 1 - c)) for a in range(na)]
        for cp in cps:
            cp.start()
        for a in range(na):
            copy(a, 1 - c, (x, y, c)).wait_recv()
        for cp in cps:
            cp.wait_send()

    outs = pl.pallas_call(
        body, out_shape=[_sds((2 * v.shape[0], v.shape[1]), v.dtype) for v in fs], in_specs=[_ANY] * na, out_specs=[_ANY] * na,
        scratch_shapes=_sems(na), name=name)(*fs)
    ci = lax.axis_index("c")
    return [lax.dynamic_update_slice(o, f, (ci * f.shape[0], 0)) for o, f in zip(outs, fs)]


def _sum_chips(a, t, chip, name):
    _, r2, n = t.shape
    tm = _row_tile(r2)

    def body(k_ref, a_ref, t1_ref, t2_ref, t3_ref, o_ref):
        o_ref[...] = ((a_ref[...].astype(F32) + t1_ref[...].astype(F32)) + t2_ref[...].astype(F32)) + t3_ref[...].astype(F32)

    def slot(j):
        return pl.BlockSpec((None, tm, n), lambda i, k_ref: (lax.rem(k_ref[0] + j, 4), i, 0))

    return pl.pallas_call(
        body, grid_spec=pltpu.PrefetchScalarGridSpec(num_scalar_prefetch=1, grid=(r2 // tm,), in_specs=[slot(0), slot(1), slot(2), slot(3)],
                                                     out_specs=pl.BlockSpec((tm, n), lambda i, k_ref: (i, 0))),
        out_shape=_sds((r2, n)), name=name, compiler_params=_PAR)(chip.reshape(1).astype(jnp.int32), a, t, t, t)


def _add_cast(g, recv, c, name):
    _, r2, n = recv.shape

    def body(c_ref, a_ref, b_ref, o_ref):
        o_ref[...] = (a_ref[...] + b_ref[...]).astype(BF16)

    spec = pl.BlockSpec((None, r2, n), lambda k, c_ref: (k, 0, 0))
    return pl.pallas_call(
        body, grid_spec=pltpu.PrefetchScalarGridSpec(
            num_scalar_prefetch=1, grid=(4,), in_specs=[pl.BlockSpec((None, r2, n), lambda k, c_ref: (k, c_ref[0], 0)), spec], out_specs=spec),
        out_shape=_sds(recv.shape, BF16), name=name, compiler_params=_PAR)(c.reshape(1).astype(jnp.int32), g, recv)


def _sum_lead(t, name, tm=256):
    P, R, n = t.shape
    tm = _row_tile(R, tm)

    def body(t_ref, o_ref):
        acc = t_ref[0].astype(F32)
        for j in range(1, P):
            acc = acc + t_ref[j].astype(F32)
        o_ref[...] = acc

    return pl.pallas_call(body, grid=(R // tm,), in_specs=[pl.BlockSpec((P, tm, n), lambda i: (0, i, 0))],
                          out_specs=pl.BlockSpec((tm, n), lambda i: (i, 0)), out_shape=_sds((R, n)), name=name, compiler_params=_PAR)(t)


def _ada_fwd(c16, ada_w, ada_b_cols, tn=512):
    L, _, n = ada_w.shape

    def body(c_ref, w_ref, b_ref, o_ref):
        o_ref[0] = _dot(_silu(c_ref[...]).astype(BF16), w_ref[0].astype(BF16), _NN) + b_ref[0]

    return pl.pallas_call(
        body, grid=(L, n // tn),
        in_specs=[pl.BlockSpec((16, D), lambda l, j: (0, 0)), pl.BlockSpec((1, D, tn), lambda l, j: (l, 0, j)), pl.BlockSpec((1, 1, tn), lambda l, j: (l, 0, j))],
        out_specs=pl.BlockSpec((1, 16, tn), lambda l, j: (l, 0, j)), out_shape=_sds((L, 16, n)), name="ada_fwd",
        compiler_params=pltpu.CompilerParams(dimension_semantics=("parallel", "parallel")))(c16, ada_w, ada_b_cols)


def _ada_bwd(c16, dmod, tn=512):
    L, _, n = dmod.shape

    def body(c_ref, d_ref, o_ref):
        o_ref[0] = _dot(_silu(c_ref[...]).astype(BF16), d_ref[0].astype(BF16), _TN)

    return pl.pallas_call(
        body, grid=(L, n // tn), in_specs=[pl.BlockSpec((16, D), lambda l, j: (0, 0)), pl.BlockSpec((1, 16, tn), lambda l, j: (l, 0, j))],
        out_specs=pl.BlockSpec((1, D, tn), lambda l, j: (l, 0, j)), out_shape=_sds((L, D, n)), name="ada_bwd",
        compiler_params=pltpu.CompilerParams(dimension_semantics=("parallel", "parallel")))(c16, dmod)


def _adam_math(w, g, m, v):
    mn = ADAM_B1 * m + (1.0 - ADAM_B1) * g
    vn = ADAM_B2 * v + (1.0 - ADAM_B2) * (g * g)
    m_hat = mn / (1.0 - ADAM_B1 ** ADAM_STEP)
    v_hat = vn / (1.0 - ADAM_B2 ** ADAM_STEP)
    return -ADAM_LR * (m_hat / (jnp.sqrt(v_hat) + ADAM_EPS) + ADAM_WD * w), mn, vn


def _adamw(w, g, m, v, name):
    R, n = w.shape
    tm = _row_tile(R)

    def body(w_ref, g_ref, m_ref, v_ref, d_ref, nm_ref, nv_ref):
        d_ref[...], nm_ref[...], nv_ref[...] = _adam_math(w_ref[...], g_ref[...], m_ref[...], v_ref[...])

    spec = pl.BlockSpec((tm, n), lambda i: (i, 0))
    return pl.pallas_call(body, grid=(R // tm,), in_specs=[spec] * 4, out_specs=(spec,) * 3, out_shape=(_sds((R, n)),) * 3,
                          name=name, compiler_params=_PAR)(w, g, m, v)


def _adamw_cols(w, g0, g1, m, v, name):
    fwd, back = (lambda t: jnp.transpose(t, (2, 0, 1))), (lambda t: jnp.transpose(t, (1, 2, 0)))
    gt = jnp.stack([g0.T, g1.T], 1)
    n, _, r = gt.shape
    tr = max(t for t in range(1, 257) if n % t == 0)

    def body(w_ref, g_ref, m_ref, v_ref, d_ref, nm_ref, nv_ref):
        d_ref[...], nm_ref[...], nv_ref[...] = _adam_math(w_ref[...], g_ref[...], m_ref[...], v_ref[...])

    spec = pl.BlockSpec((tr, 2, r), lambda i: (i, 0, 0))
    outs = pl.pallas_call(body, grid=(n // tr,), in_specs=[spec] * 4, out_specs=(spec,) * 3, out_shape=(_sds(gt.shape),) * 3,
                          name=name, compiler_params=_PAR)(fwd(w), gt, fwd(m), fwd(v))
    return (back(gt), *[back(o) for o in outs])


def _adamw_layers(w, g0, g1, m, v, after, name):
    _, r, n = w.shape
    tm = _row_tile(r)
    nb = r // tm

    def body(w_ref, g0_ref, g1_ref, m_ref, v_ref, after_ref, g_ref, d_ref, nm_ref, nv_ref):
        gv = jnp.where(pl.program_id(0) == 0, g0_ref[...], g1_ref[...])
        g_ref[...] = gv
        d_ref[...], nm_ref[...], nv_ref[...] = _adam_math(w_ref[...], gv, m_ref[...], v_ref[...])

    spec = pl.BlockSpec((None, tm, n), lambda l, i: (l, i, 0))
    g0_spec = pl.BlockSpec((tm, n), lambda l, i: (i * (1 - l) + (nb - 1) * l, 0))
    g1_spec = pl.BlockSpec((tm, n), lambda l, i: (i * l, 0))
    return pl.pallas_call(body, grid=(2, nb), in_specs=[spec, g0_spec, g1_spec, spec, spec, _ANY], out_specs=(spec,) * 4,
                          out_shape=(_sds(w.shape),) * 4, name=name,
                          compiler_params=pltpu.CompilerParams(dimension_semantics=("arbitrary", "arbitrary")))(w, g0, g1, m, v, after)


_W_NAMES = ["ada_w", "ada_b", "norm1_w", "w_in", "q_a_norm", "w_q_b", "kv_a_norm", "w_kv_b", "q_norm", "k_norm", "pool_w",
            "pool_scale", "ssd_conv_w", "ssd_conv_b", "ssd_dt_bias", "ssd_a_log", "ssd_d", "ssd_norm_w", "w_branch", "w_out",
            "norm2_w", "ffn_up", "ffn_conv_w", "ffn_conv_b", "ffn_down"]
_BIG = [("w_in", (D, IN_DIM // 4), 1), ("w_q_b", (Q_RANK, 192), 1), ("w_kv_b", (KV_RANK, 256), 1), ("w_branch", (512, D), 0),
        ("w_out", (256, D), 0), ("ffn_up", (D, 2 * FFN // 4), 1), ("ffn_down", (FFN // 4, D), 0)]

_SMALL = [("norm1_w", (D,)), ("q_a_norm", (Q_RANK,)), ("kv_a_norm", (KV_RANK,)), ("q_norm", (96,)), ("k_norm", (96,)),
          ("pool_w", (4, 128, 128)), ("pool_scale", (512,)), ("ssd_conv_w", (4, SSD_CONV_DIM)), ("ssd_conv_b", (SSD_CONV_DIM,)),
          ("ssd_dt_bias", (SSD_HEADS,)), ("ssd_a_log", (SSD_HEADS,)), ("ssd_d", (SSD_HEADS,)), ("ssd_norm_w", (D,)), ("norm2_w", (D,)),
          ("ffn_conv_w", (3, 2 * FFN)), ("ffn_conv_b", (2 * FFN,))]
_CONV_SHARDED = {"ssd_conv_w": SSD_CONV_DIM // 4, "ffn_conv_w": 2 * FFN // 4}


def _pack_flat(arrs, mult):
    flat = jnp.concatenate([a.astype(F32).reshape(-1) for a in arrs])
    rows = -(-flat.shape[0] // (128 * mult)) * mult
    return jnp.pad(flat, (0, rows * 128 - flat.shape[0])).reshape(rows, 128), [a.shape for a in arrs]


def _unpack_flat(packed, shapes):
    flat, out, off = packed.reshape(-1), [], 0
    for s in shapes:
        n = 1
        for d in s:
            n *= d
        out.append(flat[off:off + n].reshape(s))
        off += n
    return out


_EARLY = ["w_in", "w_q_b", "w_kv_b"]
_LATE = ["w_branch", "w_out", "ffn_up", "ffn_down"]


def _early_weights(a, l, stacks, conv_full):
    p = {n: a[n][l] for n in _W_NAMES if n not in ("ada_w", "ada_b")}
    p.update({n: conv_full[n][l] for n in conv_full})
    p.update(w_in4=stacks[0], w_q_b4=stacks[1], w_kv_b4=stacks[2])
    return p


def _late_weights(stacks):
    return _prep_late(dict(w_branch=stacks[0].reshape(2048, D), w_out=stacks[1].reshape(D, D), ffn_up4=stacks[2],
                           ffn_down=stacks[3].reshape(FFN, D)))


def _reduce_start(gs, ci, after, tag):
    recv = _send_halves(gs, f"rs_halves_{tag}")
    chip_sum = [_add_cast(g, r, ci, f"rs_add_{tag}") for g, r in zip(gs, recv)]
    return _ici_start(chip_sum, [_sds(v.shape, v.dtype) for v in chip_sum], _scatter_slices, after, f"rs_scatter_{tag}_start")


def _reduce_finish(started, after, tag):
    chip_sum, got = _ici_wait(started, _scatter_slices, after, f"rs_scatter_{tag}_wait")
    chip = 2 * lax.axis_index("x") + lax.axis_index("y")
    return _join_halves([_sum_chips(s, t, chip, f"rs_sum_{tag}") for s, t in zip(chip_sum, got)], f"rs_join_{tag}")


def kernel(x, c, positions, ada_w, ada_b, norm1_w, w_in, q_a_norm, w_q_b, kv_a_norm, w_kv_b, q_norm, k_norm, pool_w, pool_scale, ssd_conv_w, ssd_conv_b, ssd_dt_bias, ssd_a_log, ssd_d, ssd_norm_w, w_branch, w_out, norm2_w, ffn_up, ffn_conv_w, ffn_conv_b, ffn_down, loss_target, m_ada_w, m_ada_b, m_norm1_w, m_w_in, m_q_a_norm, m_w_q_b, m_kv_a_norm, m_w_kv_b, m_q_norm, m_k_norm, m_pool_w, m_pool_scale, m_ssd_conv_w, m_ssd_conv_b, m_ssd_dt_bias, m_ssd_a_log, m_ssd_d, m_ssd_norm_w, m_w_branch, m_w_out, m_norm2_w, m_ffn_up, m_ffn_conv_w, m_ffn_conv_b, m_ffn_down, v_ada_w, v_ada_b, v_norm1_w, v_w_in, v_q_a_norm, v_w_q_b, v_kv_a_norm, v_w_kv_b, v_q_norm, v_k_norm, v_pool_w, v_pool_scale, v_ssd_conv_w, v_ssd_conv_b, v_ssd_dt_bias, v_ssd_a_log, v_ssd_d, v_ssd_norm_w, v_w_branch, v_w_out, v_norm2_w, v_ffn_up, v_ffn_conv_w, v_ffn_conv_b, v_ffn_down):
    a = dict(locals())
    xi, yi, ci = lax.axis_index("x"), lax.axis_index("y"), lax.axis_index("c")
    chip = 2 * xi + yi
    dev = 2 * chip + ci
    ncol = 6 * D // 4

    c_all = _allgather8(c.reshape(8, 128), "gather_c").reshape(8, D)
    c16 = jnp.pad(c_all, ((0, 8), (0, 0)))
    ada_b_cols = lax.dynamic_slice_in_dim(ada_b, chip * ncol, ncol, axis=1).reshape(2, 1, ncol)
    mod_part = _ada_fwd(c16, ada_w, ada_b_cols)[:, :8]
    small1, shapes1 = _pack_flat([mod_part, ssd_conv_w, ffn_conv_w], 8)
    got1 = _allgather8(small1, "gather_mod").reshape(8, -1, 128)
    per_chip = [_unpack_flat(got1[2 * k], shapes1) for k in range(4)]
    mod_all = jnp.concatenate([per_chip[k][0] for k in range(4)], -1)
    conv_full = {"ssd_conv_w": jnp.concatenate([per_chip[k][1] for k in range(4)], -1),
                 "ffn_conv_w": jnp.concatenate([per_chip[k][2] for k in range(4)], -1)}
    mod_mine = lax.dynamic_index_in_dim(mod_all, dev, axis=1, keepdims=False).reshape(2, 6, D)
    mods = [jnp.pad(mod_mine[l], ((0, 2), (0, 0))) for l in range(2)]

    big = _EARLY + _LATE
    shard = lambda names, l: [a[n][l].astype(BF16) for n in names]
    g0a = _gather_start(shard(_EARLY, 0), mods[0], "gather_0a")
    g0b = _gather_start(shard(_LATE, 0), g0a[-1], "gather_0b")
    g1 = _gather_start(shard(big, 1), g0b[-1], "gather_1")
    mods[0] = mods[0] + g1[-1][0, 0]

    def get_layer(l, after):
        if l == 0:
            return (_early_weights(a, 0, _gather_finish(g0a, mods[0], "gather_0a"), conv_full),
                    lambda aft: _late_weights(_gather_finish(g0b, aft, "gather_0b")))
        stacks = _gather_finish(g1, after, "gather_1")
        return _early_weights(a, 1, stacks[:3], conv_full), lambda aft: _late_weights(stacks[3:])

    scatters = {}

    def bwd_mod(l):
        return mods[l] if l == 1 else mods[0] + scatters["1"][-1][0, 0]

    def emit(l, late):
        if l == 0:
            scatters["0b"] = _reduce_start([late[n] for n in _LATE], ci, scatters["1"][-1], "0b")
            return scatters["0b"][-1][0, 0]

    def done(l, grads_l):
        if l == 1:
            scatters["1"] = _reduce_start([grads_l[n] for n in big], ci, grads_l["norm1_w"], "1")
        else:
            scatters["0a"] = _reduce_start([grads_l[n] for n in _EARLY], ci, scatters["0b"][-1], "0a")

    lpart, grad_x, grads, dmods = _local_step(x[0], loss_target[0], positions[0], mods, get_layer, bwd_mod, emit, done)
    loss = lax.psum(lpart, ("x", "y", "c"))
    red1 = _reduce_finish(scatters["1"], grad_x, "1")
    red0b = _reduce_finish(scatters["0b"], red1[0], "0b")

    small2, shapes2 = _pack_flat([jnp.stack(dmods)] + [grads[l][n] for l in range(2) for n, _ in _SMALL], 16)
    got2 = _allgather8(small2, "gather_small").reshape(8, -1, 128)
    tot = _unpack_flat(_sum_lead(got2, "sum_small"), shapes2)
    g = {"ada_b": tot[0].reshape(2, 6 * D)}
    for i, (n, _) in enumerate(_SMALL):
        g[n] = jnp.stack([tot[1 + i], tot[1 + len(_SMALL) + i]])
    for n, w in _CONV_SHARDED.items():
        g[n] = lax.dynamic_slice_in_dim(g[n], chip * w, w, axis=2)
    nd = 2 * 6 * D // 128
    dmod_all = jnp.transpose(got2[:, :nd].reshape(8, 2, 6 * D), (1, 0, 2))
    dmod_cols = lax.dynamic_slice_in_dim(jnp.pad(dmod_all, ((0, 0), (0, 8), (0, 0))), chip * ncol, ncol, axis=2)
    g["ada_w"] = _ada_bwd(c16, dmod_cols)

    delta, new_m, new_v = {}, {}, {}
    red1 = dict(zip(big, red1))
    token = scatters["0a"][-1]
    for n, r0 in zip(_LATE, red0b):
        g[n], delta[n], new_m[n], new_v[n] = _adamw_layers(a[n], r0, red1[n], a["m_" + n], a["v_" + n], token, f"adamw_{n}")
    shp = ada_w.shape
    r2 = lambda t: t.reshape(-1, shp[-1])
    delta["ada_w"], new_m["ada_w"], new_v["ada_w"] = (
        o.reshape(shp) for o in _adamw(r2(ada_w), r2(g["ada_w"]), r2(m_ada_w), r2(v_ada_w), "adamw_ada_w"))
    behind = (delta[_LATE[-1]][0, 0, :1] + delta["ada_w"][0, 0, :1]).reshape(1)
    red0a = _reduce_finish(scatters["0a"], behind, "0a")
    for n, r0 in zip(_EARLY, red0a):
        if n == "w_in":
            g[n], delta[n], new_m[n], new_v[n] = _adamw_cols(a[n], r0, red1[n], a["m_" + n], a["v_" + n], f"adamw_{n}")
        else:
            g[n], delta[n], new_m[n], new_v[n] = _adamw_layers(a[n], r0, red1[n], a["m_" + n], a["v_" + n], token, f"adamw_{n}")
    rest = [n for n in _W_NAMES if n not in big and n != "ada_w"]
    packs = [_pack_flat([t[n] if pre is None else t[pre + n] for n in rest], 128)[0]
             for t, pre in ((a, None), (g, None), (a, "m_"), (a, "v_"))]
    rest_shapes = [a[n].shape for n in rest]
    outs = [_unpack_flat(o, rest_shapes) for o in _adamw(*packs, "adamw_rest")]
    for i, n in enumerate(rest):
        delta[n], new_m[n], new_v[n] = outs[0][i], outs[1][i], outs[2][i]

    return (loss, grad_x[None], *[g[n] for n in _W_NAMES], *[delta[n] for n in _W_NAMES],
            *[new_m[n] for n in _W_NAMES], *[new_v[n] for n in _W_NAMES])
```

```python
import functools

import jax
import jax.numpy as jnp
from jax import lax
from jax.experimental import pallas as pl
from jax.experimental.pallas import tpu as pltpu

F32 = jnp.float32
BF16 = jnp.bfloat16
MESH = pl.DeviceIdType.MESH
HI = lax.Precision.HIGHEST

D = 1024
N_HEADS = 8
NOPE, ROPE_DIM = 64, 32
Q_RANK, KV_RANK = 384, 256
POOL_WINDOWS = (2, 4, 8, 16)
SSD_HEADS, SSD_P, SSD_N, SSD_L = 16, 64, 128, 128
SSD_INNER = 1024
SSD_CONV_DIM = 1536
FFN = 2816
EPS = 1e-6
ROPE_THETA = 10000.0
OG, OZ, OX, OP, OQ, OKV, ODT, IN_PAD = 0, 3072, 4096, 5632, 6144, 6528, 6912, 7040
IN_DIM = 6832
ADAM_LR, ADAM_B1, ADAM_B2, ADAM_EPS, ADAM_WD, ADAM_STEP = 0.001, 0.9, 0.999, 1e-08, 0.01, 10

_ARB = pltpu.CompilerParams(dimension_semantics=("arbitrary",))
_PAR = pltpu.CompilerParams(dimension_semantics=("parallel",))


def _pick(n, pref):
    if n <= pref:
        return n
    best = None
    for t in range(128, pref + 1, 128):
        if n % t == 0:
            best = t
    assert best is not None, (n, pref)
    return best


def _row_tile(r, cap=256):
    best = None
    for t in range(16, min(r, cap) + 1, 16):
        if r % t == 0:
            best = t
    assert best is not None, r
    return best


def _sds(shape, dtype=F32):
    return jax.ShapeDtypeStruct(tuple(shape), dtype)


def _iota(shape, dim):
    return lax.broadcasted_iota(jnp.int32, shape, dim)


def _sigmoid(x):
    return 0.5 * jnp.tanh(0.5 * x) + 0.5


def _silu(x):
    return x * _sigmoid(x)


def _dsilu(x):
    s = _sigmoid(x)
    return s * (1.0 + x * (1.0 - s))


def _dot(a, b, dims):
    return lax.dot_general(a, b, (dims, ((), ())), preferred_element_type=F32)


_NN, _NT, _TN = ((1,), (0,)), ((1,), (1,)), ((0,), (0,))


def _dot_hi(a, b, dims=_NN):
    return lax.dot_general(a, b, (dims, ((), ())), preferred_element_type=F32, precision=HI)


def _shift_down(x, j):
    n = x.shape[0]
    return jnp.where(_iota(x.shape, 0) >= j, pltpu.roll(x, j, 0), 0.0)


def _shift_up(x, j):
    n = x.shape[0]
    return jnp.where(_iota(x.shape, 0) < n - j, pltpu.roll(x, n - j, 0), 0.0)


def _mm(a, b, mode="nn", out_dtype=F32, tm=2048, tn=512, tk=4096, res=None, gate=None, name="mm"):
    if mode == "nn":
        (M, K), (K2, N) = a.shape, b.shape
    elif mode == "nt":
        (M, K), (N, K2) = a.shape, b.shape
    else:
        (K, M), (K2, N) = a.shape, b.shape
    assert K == K2, (a.shape, b.shape, mode)
    tm, tn, tk = _pick(M, tm), _pick(N, tn), _pick(K, tk)
    nk = K // tk
    dims = {"nn": _NN, "nt": _NT, "tn": _TN}[mode]
    fused = res is not None

    def body(*refs):
        a_ref, b_ref = refs[:2]

        def finish(acc):
            if fused:
                r_ref, g_ref, o_ref, raw_ref = refs[2:6]
                raw_ref[...] = acc
                o_ref[...] = r_ref[...] + g_ref[...] * acc
            else:
                refs[2][...] = acc.astype(out_dtype)

        _mm_steps(a_ref, b_ref, dims, nk, refs[-1] if nk > 1 else None, finish)

    if mode == "nn":
        a_spec = pl.BlockSpec((tm, tk), lambda i, j, k: (i, k))
        b_spec = pl.BlockSpec((tk, tn), lambda i, j, k: (k, j))
    elif mode == "nt":
        a_spec = pl.BlockSpec((tm, tk), lambda i, j, k: (i, k))
        b_spec = pl.BlockSpec((tn, tk), lambda i, j, k: (j, k))
    else:
        a_spec = pl.BlockSpec((tk, tm), lambda i, j, k: (k, i))
        b_spec = pl.BlockSpec((tk, tn), lambda i, j, k: (k, j))
    o_spec = pl.BlockSpec((tm, tn), lambda i, j, k: (i, j))
    in_specs, args = [a_spec, b_spec], [a, b]
    out_shape, out_specs = _sds((M, N), out_dtype), o_spec
    if fused:
        in_specs += [o_spec, pl.BlockSpec((1, tn), lambda i, j, k: (0, j))]
        args += [res, gate]
        out_shape, out_specs = (_sds((M, N)), _sds((M, N))), (o_spec, o_spec)
    return pl.pallas_call(
        body, grid=(M // tm, N // tn, nk), in_specs=in_specs, out_specs=out_specs, out_shape=out_shape,
        scratch_shapes=[pltpu.VMEM((tm, tn), F32)] if nk > 1 else [], name=name,
        compiler_params=pltpu.CompilerParams(dimension_semantics=("parallel", "parallel", "arbitrary")),
    )(*args)


def _mm_steps(a_ref, b_ref, dims, nk, acc_ref, finish):
    part = _dot(a_ref[...].astype(BF16), b_ref[...].astype(BF16), dims)
    if nk == 1:
        finish(part)
        return
    k = pl.program_id(2)

    @pl.when(k == 0)
    def _():
        acc_ref[...] = part

    @pl.when(k > 0)
    def _():
        acc_ref[...] += part

    @pl.when(k == nk - 1)
    def _():
        finish(acc_ref[...])


def _mm_blocks(a, b, dims, grid, a_spec, b_spec, o_spec, out_shape, acc_shape, name):
    nk = grid[2]

    def body(a_ref, b_ref, o_ref, *scratch):
        def finish(acc):
            o_ref[...] = acc.astype(o_ref.dtype)

        _mm_steps(a_ref, b_ref, dims, nk, scratch[0] if nk > 1 else None, finish)

    return pl.pallas_call(
        body, grid=grid, in_specs=[a_spec, b_spec], out_specs=o_spec, out_shape=out_shape,
        scratch_shapes=[pltpu.VMEM(acc_shape, F32)] if nk > 1 else [], name=name,
        compiler_params=pltpu.CompilerParams(dimension_semantics=("parallel", "parallel", "arbitrary")),
    )(a, b)


_UP_SHARD = 2 * FFN // 4


def _up_fwd(h2, wup4, name, tm=2048):
    S = h2.shape[0]
    tm = min(tm, S)
    return _mm_blocks(h2, wup4, _NN, (S // tm, 4, 1), pl.BlockSpec((tm, D), lambda i, j, k: (i, 0)),
                      pl.BlockSpec((None, D, _UP_SHARD), lambda i, j, k: (j, 0, 0)), pl.BlockSpec((tm, _UP_SHARD), lambda i, j, k: (i, j)),
                      _sds((S, 2 * FFN)), (tm, _UP_SHARD), name)


def _up_dx(dup2, wup4, name, tm=2048, tn=512):
    S = dup2.shape[1]
    tm = min(tm, S)
    return _mm_blocks(dup2, wup4, _NT, (S // tm, D // tn, 4), pl.BlockSpec((None, tm, _UP_SHARD), lambda i, j, k: (lax.div(k, 2), i, lax.rem(k, 2))),
                      pl.BlockSpec((None, tn, _UP_SHARD), lambda i, j, k: (k, j, 0)), pl.BlockSpec((tm, tn), lambda i, j, k: (i, j)),
                      _sds((S, D)), (tm, tn), name)


def _up_dw(h2, dup2, name, tm=1024, tk=4096):
    S = h2.shape[0]
    tk = min(tk, S)
    return _mm_blocks(h2, dup2, _TN, (D // tm, 4, S // tk), pl.BlockSpec((tk, tm), lambda i, j, k: (k, i)),
                      pl.BlockSpec((None, tk, _UP_SHARD), lambda i, j, k: (lax.div(j, 2), k, lax.rem(j, 2))),
                      pl.BlockSpec((None, tm, _UP_SHARD), lambda i, j, k: (j, i, 0)), _sds((4, D, _UP_SHARD)), (tm, _UP_SHARD), name)


def _row_spec(tm, n):
    return pl.BlockSpec((tm, n), lambda i: (i, 0))


def _vec_spec(n, rows=1):
    return pl.BlockSpec((rows, n), lambda i: (0, 0))


def _ln_mod(x, nw, sc, sh, name, tm=256):
    S = x.shape[0]

    def body(x_ref, nw_ref, sc_ref, sh_ref, o_ref):
        xv = x_ref[...]
        r = lax.rsqrt(jnp.mean(xv * xv, -1, keepdims=True) + EPS)
        o_ref[...] = ((xv * r * nw_ref[...]) * (1.0 + sc_ref[...]) + sh_ref[...]).astype(BF16)

    return pl.pallas_call(
        body, grid=(S // tm,), in_specs=[_row_spec(tm, D)] + [_vec_spec(D)] * 3, out_specs=_row_spec(tm, D),
        out_shape=_sds((S, D), BF16), name=name, compiler_params=_PAR)(x, nw, sc, sh)


def _ln_mod_bwd(x, dh, dres, nw, sc, name, tm=256):
    S = x.shape[0]

    def body(x_ref, dh_ref, dres_ref, nw_ref, sc_ref, dx_ref, st_ref):
        @pl.when(pl.program_id(0) == 0)
        def _():
            st_ref[...] = jnp.zeros_like(st_ref)

        xv, dhv, nwv = x_ref[...], dh_ref[...], nw_ref[...]
        r = lax.rsqrt(jnp.mean(xv * xv, -1, keepdims=True) + EPS)
        xhat = xv * r
        dn = dhv * (1.0 + sc_ref[...])
        g = dn * nwv
        dx_ref[...] = dres_ref[...] + r * (g - xhat * jnp.mean(g * xhat, -1, keepdims=True))
        st_ref[0:1, :] += jnp.sum(dhv, 0, keepdims=True)
        st_ref[1:2, :] += jnp.sum(dhv * (xhat * nwv), 0, keepdims=True)
        st_ref[2:3, :] += jnp.sum(dn * xhat, 0, keepdims=True)

    return pl.pallas_call(
        body, grid=(S // tm,), in_specs=[_row_spec(tm, D)] * 3 + [_vec_spec(D)] * 2,
        out_specs=(_row_spec(tm, D), _vec_spec(D, 8)), out_shape=(_sds((S, D)), _sds((8, D))),
        name=name, compiler_params=_ARB)(x, dh, dres, nw, sc)


def _gate_bwd(dx, out, g, name, tm=256):
    S = dx.shape[0]

    def body(dx_ref, o_ref, g_ref, dz_ref, dg_ref):
        @pl.when(pl.program_id(0) == 0)
        def _():
            dg_ref[...] = jnp.zeros_like(dg_ref)

        dxv = dx_ref[...]
        dz_ref[...] = (dxv * g_ref[...]).astype(BF16)
        dg_ref[0:1, :] += jnp.sum(dxv * o_ref[...], 0, keepdims=True)

    return pl.pallas_call(
        body, grid=(S // tm,), in_specs=[_row_spec(tm, D)] * 2 + [_vec_spec(D)],
        out_specs=(_row_spec(tm, D), _vec_spec(D, 8)), out_shape=(_sds((S, D), BF16), _sds((8, D))),
        name=name, compiler_params=_ARB)(dx, out, g)


def _loss_grad(y, t, tm=256):
    S = y.shape[0]

    def body(y_ref, t_ref, dy_ref, l_ref):
        @pl.when(pl.program_id(0) == 0)
        def _():
            l_ref[...] = jnp.zeros_like(l_ref)

        e = y_ref[...] - t_ref[...]
        dy_ref[...] = e * (1.0 / D)
        l_ref[...] += 0.5 * jnp.sum(jnp.mean(e * e, -1, keepdims=True), 0, keepdims=True)

    return pl.pallas_call(
        body, grid=(S // tm,), in_specs=[_row_spec(tm, D)] * 2,
        out_specs=(_row_spec(tm, D), pl.BlockSpec((8, 128), lambda i: (0, 0))),
        out_shape=(_sds((S, D)), _sds((8, 128))), name="loss_grad", compiler_params=_ARB)(y, t)


_CONV_ROWS = 256
_HALO = 8


def _rows(ref, lo, hi):
    S, c = ref.shape
    parts = [jnp.zeros((-lo, c), F32)] if lo < 0 else []
    parts.append(ref[max(lo, 0):min(hi, S), :])
    if hi > S:
        parts.append(jnp.zeros((hi - S, c), F32))
    return parts[0] if len(parts) == 1 else jnp.concatenate(parts, 0)


def _conv_rows(ext, w, b, n):
    K = w.shape[0]
    acc = ext[_HALO:_HALO + n] * w[K - 1:K, :] + b
    for j in range(1, K):
        acc = acc + pltpu.roll(ext, j, 0)[_HALO:_HALO + n] * w[K - 1 - j:K - j, :]
    return acc


def _conv_rows_bwd(ext, w, dc, n):
    K, m = w.shape[0], dc.shape[0]
    d0 = dc[0:n]
    dx = d0 * w[K - 1:K, :]
    dws = [jnp.sum(d0 * ext[_HALO:_HALO + n], 0, keepdims=True)]
    for j in range(1, K):
        dx = dx + pltpu.roll(dc, m - j, 0)[0:n] * w[K - 1 - j:K - j, :]
        dws.append(jnp.sum(d0 * pltpu.roll(ext, j, 0)[_HALO:_HALO + n], 0, keepdims=True))
    return dx, dws[::-1], jnp.sum(d0, 0, keepdims=True)


def _col_spec(S, tc, off=0):
    return pl.BlockSpec((S, tc), lambda j: (0, j + off))


def _ssd_pre(proj, cw, cb, tc=256):
    S, n = proj.shape[0], SSD_CONV_DIM
    R = min(_CONV_ROWS, S)

    def body(x_ref, w_ref, b_ref, o_ref):
        wv, bv = w_ref[...], b_ref[...]
        for r0 in range(0, S, R):
            o_ref[r0:r0 + R, :] = _silu(_conv_rows(_rows(x_ref, r0 - _HALO, r0 + R), wv, bv, R))

    return pl.pallas_call(
        body, grid=(n // tc,),
        in_specs=[_col_spec(S, tc, OX // tc), pl.BlockSpec((4, tc), lambda j: (0, j)), pl.BlockSpec((1, tc), lambda j: (0, j))],
        out_specs=_col_spec(S, tc), out_shape=_sds((S, n)), name="ssd_pre", compiler_params=_PAR)(proj, cw, cb)


def _ssd_pre_bwd(proj, dxc, cw, cb, tc=256):
    S, n = proj.shape[0], SSD_CONV_DIM
    R = min(_CONV_ROWS, S)

    def body(x_ref, d_ref, w_ref, b_ref, dx_ref, dw_ref, db_ref):
        wv, bv = w_ref[...], b_ref[...]
        acc = [jnp.zeros((1, tc), F32)] * 5
        for r0 in range(0, S, R):
            ext = _rows(x_ref, r0 - _HALO, r0 + R + _HALO)
            dc = _rows(d_ref, r0, r0 + R + _HALO) * _dsilu(_conv_rows(ext, wv, bv, R + _HALO))
            dx, dws, db = _conv_rows_bwd(ext, wv, dc, R)
            dx_ref[r0:r0 + R, :] = dx.astype(BF16)
            acc = [s + d for s, d in zip(acc, dws + [db])]
        for k in range(4):
            dw_ref[k:k + 1, :] = acc[k]
        db_ref[...] = acc[4]

    wspec, bspec = pl.BlockSpec((4, tc), lambda j: (0, j)), pl.BlockSpec((1, tc), lambda j: (0, j))
    return pl.pallas_call(
        body, grid=(n // tc,), in_specs=[_col_spec(S, tc, OX // tc), _col_spec(S, tc), wspec, bspec],
        out_specs=(_col_spec(S, tc), wspec, bspec), out_shape=(_sds((S, n), BF16), _sds((4, n)), _sds((1, n))),
        name="ssd_pre_bwd", compiler_params=_PAR)(proj, dxc, cw, cb)


def _ffn_act(up, cw, cb, tc=256):
    S, nb = up.shape[0], FFN // tc
    R = min(_CONV_ROWS, S)

    def body(g_ref, v_ref, wg_ref, wv_ref, bg_ref, bv_ref, o_ref):
        wg, wv, bg, bv = wg_ref[...], wv_ref[...], bg_ref[...], bv_ref[...]
        for r0 in range(0, S, R):
            cg = _conv_rows(_rows(g_ref, r0 - _HALO, r0 + R), wg, bg, R)
            cv = _conv_rows(_rows(v_ref, r0 - _HALO, r0 + R), wv, bv, R)
            o_ref[r0:r0 + R, :] = (_silu(cg) * cv).astype(BF16)

    def wspec(off):
        return pl.BlockSpec((3, tc), lambda j: (0, j + off))

    def bspec(off):
        return pl.BlockSpec((1, tc), lambda j: (0, j + off))

    return pl.pallas_call(
        body, grid=(nb,), in_specs=[_col_spec(S, tc), _col_spec(S, tc, nb), wspec(0), wspec(nb), bspec(0), bspec(nb)],
        out_specs=_col_spec(S, tc), out_shape=_sds((S, FFN), BF16), name="ffn_act", compiler_params=_PAR)(up, up, cw, cw, cb, cb)


def _ffn_act_bwd(up, dact, cw, cb, tc=256):
    S, nb = up.shape[0], FFN // tc
    R = min(_CONV_ROWS, S)

    def body(g_ref, v_ref, d_ref, wg_ref, wv_ref, bg_ref, bv_ref, dx_ref, dw_ref, db_ref):
        wg, wv, bg, bv = wg_ref[...], wv_ref[...], bg_ref[...], bv_ref[...]
        acc = [[jnp.zeros((1, tc), F32)] * 4, [jnp.zeros((1, tc), F32)] * 4]
        for r0 in range(0, S, R):
            eg, ev = _rows(g_ref, r0 - _HALO, r0 + R + _HALO), _rows(v_ref, r0 - _HALO, r0 + R + _HALO)
            da = _rows(d_ref, r0, r0 + R + _HALO)
            cg, cv = _conv_rows(eg, wg, bg, R + _HALO), _conv_rows(ev, wv, bv, R + _HALO)
            sg = _sigmoid(cg)
            for half, (ext, w, dc) in enumerate(((eg, wg, da * cv * (sg * (1.0 + cg * (1.0 - sg)))), (ev, wv, da * (cg * sg)))):
                dx, dws, db = _conv_rows_bwd(ext, w, dc, R)
                dx_ref[half, r0:r0 + R, :] = dx.astype(BF16)
                acc[half] = [s + d for s, d in zip(acc[half], dws + [db])]
        for half in range(2):
            for k in range(3):
                dw_ref[half, k:k + 1, :] = acc[half][k]
            db_ref[half] = acc[half][3]

    def wspec(off):
        return pl.BlockSpec((3, tc), lambda j: (0, j + off))

    def bspec(off):
        return pl.BlockSpec((1, tc), lambda j: (0, j + off))

    cs = _col_spec(S, tc)
    both = lambda r: pl.BlockSpec((2, r, tc), lambda j: (0, 0, j))
    return pl.pallas_call(
        body, grid=(nb,), in_specs=[cs, _col_spec(S, tc, nb), cs, wspec(0), wspec(nb), bspec(0), bspec(nb)],
        out_specs=(both(S), both(3), both(1)), out_shape=(_sds((2, S, FFN), BF16), _sds((2, 3, FFN)), _sds((2, 1, FFN))),
        name="ffn_act_bwd", compiler_params=_PAR)(up, up, dact, cw, cw, cb, cb)


def _window_sum(x, w, up=False):
    shift = _shift_up if up else _shift_down
    j = 1
    while j < w:
        x = x + shift(x, j)
        j *= 2
    return x


def _pool_fwd(proj, pool_w, pool_scale):
    S = proj.shape[0]

    def body(u_ref, w_ref, s_ref, o_ref):
        cnt_row = (_iota((S, 128), 0) + 1).astype(F32)
        for g, w in enumerate(POOL_WINDOWS):
            sl = slice(g * 128, (g + 1) * 128)
            u = u_ref[:, sl]
            pooled = _window_sum(u, w) / jnp.minimum(cnt_row, float(w)) - u
            mixed = _dot(pooled.astype(BF16), w_ref[g].astype(BF16), _NN)
            o_ref[:, sl] = (mixed * s_ref[:, sl]).astype(BF16)

    return pl.pallas_call(
        body, grid=(1,),
        in_specs=[pl.BlockSpec((S, 512), lambda i: (0, OP // 512)), pl.BlockSpec((4, 128, 128), lambda i: (0, 0, 0)), _vec_spec(512)],
        out_specs=pl.BlockSpec((S, 512), lambda i: (0, 0)), out_shape=_sds((S, 512), BF16), name="pool_fwd",
        compiler_params=_ARB)(proj, pool_w, pool_scale)


def _pool_bwd(proj, dob, pool_w, pool_scale):
    S = proj.shape[0]

    def body(u_ref, d_ref, w_ref, s_ref, du_ref, dw_ref, ds_ref):
        cnt_row = (_iota((S, 128), 0) + 1).astype(F32)
        for g, w in enumerate(POOL_WINDOWS):
            sl = slice(g * 128, (g + 1) * 128)
            u, dv, wv = u_ref[:, sl], d_ref[:, sl], w_ref[g].astype(BF16)
            cnt = jnp.minimum(cnt_row, float(w))
            pooled = (_window_sum(u, w) / cnt - u).astype(BF16)
            ds_ref[:, sl] = jnp.sum(dv * _dot(pooled, wv, _NN), 0, keepdims=True)
            dmix = (dv * s_ref[:, sl]).astype(BF16)
            dw_ref[g] = _dot(pooled, dmix, _TN)
            dp = _dot(dmix, wv, _NT)
            du_ref[:, sl] = (_window_sum(dp / cnt, w, up=True) - dp).astype(BF16)

    blk = pl.BlockSpec((S, 512), lambda i: (0, 0))
    wspec = pl.BlockSpec((4, 128, 128), lambda i: (0, 0, 0))
    return pl.pallas_call(
        body, grid=(1,), in_specs=[pl.BlockSpec((S, 512), lambda i: (0, OP // 512)), blk, wspec, _vec_spec(512)],
        out_specs=(blk, wspec, _vec_spec(512)), out_shape=(_sds((S, 512), BF16), _sds((4, 128, 128)), _sds((1, 512))),
        name="pool_bwd", compiler_params=_ARB)(proj, dob, pool_w, pool_scale)


def _branch_specs():
    return [pl.BlockSpec((512, D), lambda i: (0, 0)), pl.BlockSpec((512, D), lambda i: (1, 0)), pl.BlockSpec((1024, D), lambda i: (1, 0))]


def _merge_fwd(oa, ob, oc, proj, wbr, tm=256):
    S = oa.shape[0]

    def body(oa_ref, ob_ref, oc_ref, gl_ref, wa_ref, wb_ref, wc_ref, o_ref):
        acc = _sigmoid(gl_ref[:, 0:D]) * _dot(oa_ref[...], wa_ref[...], _NN)
        acc += _sigmoid(gl_ref[:, D:2 * D]) * _dot(ob_ref[...], wb_ref[...], _NN)
        acc += _sigmoid(gl_ref[:, 2 * D:3 * D]) * _dot(oc_ref[...], wc_ref[...], _NN)
        o_ref[...] = acc.astype(BF16)

    return pl.pallas_call(
        body, grid=(S // tm,),
        in_specs=[_row_spec(tm, 512), _row_spec(tm, 512), _row_spec(tm, D), _row_spec(tm, 3 * D)] + _branch_specs(),
        out_specs=_row_spec(tm, D), out_shape=_sds((S, D), BF16), name="merge_fwd", compiler_params=_PAR)(oa, ob, oc, proj, wbr, wbr, wbr)


def _merge_bwd(dm, oa, ob, oc, proj, wbr, tm=256):
    S = oa.shape[0]

    def body(dm_ref, oa_ref, ob_ref, oc_ref, gl_ref, wa_ref, wb_ref, wc_ref, dya_ref, dyb_ref, dyc_ref, dgl_ref, doa_ref, dob_ref, doc_ref):
        dmv = dm_ref[...]
        for i, (o_ref, w_ref, dy_ref, do_ref) in enumerate(
                ((oa_ref, wa_ref, dya_ref, doa_ref), (ob_ref, wb_ref, dyb_ref, dob_ref), (oc_ref, wc_ref, dyc_ref, doc_ref))):
            gt = _sigmoid(gl_ref[:, i * D:(i + 1) * D])
            wv = w_ref[...]
            yv = _dot(o_ref[...], wv, _NN)
            dy = (dmv * gt).astype(BF16)
            dy_ref[...] = dy
            dgl_ref[:, i * D:(i + 1) * D] = (dmv * yv * gt * (1.0 - gt)).astype(BF16)
            do_ref[...] = _dot(dy, wv, _NT)

    rs = _row_spec
    return pl.pallas_call(
        body, grid=(S // tm,),
        in_specs=[rs(tm, D), rs(tm, 512), rs(tm, 512), rs(tm, D), rs(tm, 3 * D)] + _branch_specs(),
        out_specs=(rs(tm, D), rs(tm, D), rs(tm, D), rs(tm, 3 * D), rs(tm, 512), rs(tm, 512), rs(tm, D)),
        out_shape=(_sds((S, D), BF16), _sds((S, D), BF16), _sds((S, D), BF16), _sds((S, 3 * D), BF16), _sds((S, 512)), _sds((S, 512)), _sds((S, D))),
        name="merge_bwd", compiler_params=_PAR)(dm, oa, ob, oc, proj, wbr, wbr, wbr)


def _rope_tab(posb, invf, tm=256):
    S = posb.shape[0]

    def body(p_ref, f_ref, c_ref, s1_ref, s2_ref):
        ang = p_ref[...] * f_ref[...]
        lane = _iota(ang.shape, 1)
        cs, sn = jnp.cos(ang), jnp.sin(ang)
        c_ref[...] = jnp.where(lane < NOPE, 1.0, cs)
        s1_ref[...] = jnp.where((lane >= 64) & (lane < 80), -sn, 0.0)
        s2_ref[...] = jnp.where((lane >= 80) & (lane < 96), sn, 0.0)

    rs = _row_spec(tm, 128)
    return pl.pallas_call(body, grid=(S // tm,), in_specs=[rs, _vec_spec(128)], out_specs=(rs, rs, rs),
                          out_shape=(_sds((S, 128)),) * 3, name="rope_tab", compiler_params=_PAR)(posb, invf)


def _rope(u, C, S1, S2):
    return u * C + pltpu.roll(u, 112, 1) * S1 + pltpu.roll(u, 16, 1) * S2


def _rope_t(dy, C, S1, S2):
    return dy * C + pltpu.roll(dy * S1, 16, 1) + pltpu.roll(dy * S2, 112, 1)


def _seg_sum(v, mask):
    return jnp.sum(jnp.where(mask, v, 0.0), -1, keepdims=True)


def _mla_latents(pq_ref, pkv_ref, wqb_ref, wkvb_ref, qan_ref, kvan_ref):
    ql, kvl = pq_ref[...], pkv_ref[...]
    ckv, kr = kvl[:, 0:KV_RANK], kvl[:, KV_RANK:KV_RANK + 128]
    rq = lax.rsqrt(jnp.mean(ql * ql, -1, keepdims=True) + EPS)
    rkv = lax.rsqrt(jnp.mean(ckv * ckv, -1, keepdims=True) + EPS)
    nq = (ql * rq * qan_ref[...]).astype(BF16)
    nkv = (ckv * rkv * kvan_ref[...]).astype(BF16)
    kv = jnp.concatenate([_dot(nkv, wkvb_ref[k], _NN) for k in range(4)], 1)
    return ql, ckv, kr, rq, rkv, nq, nkv, _dot(nq, wqb_ref[...], _NN), kv


def _mla_specs(tm):
    full = lambda r, n: pl.BlockSpec((r, n), lambda i: (0, 0))
    return ([pl.BlockSpec((tm, 384), lambda i: (i, OQ // 384)), pl.BlockSpec((tm, 384), lambda i: (i, OKV // 384))],
            [full(Q_RANK, D), pl.BlockSpec((4, KV_RANK, 256), lambda i: (0, 0, 0)), _vec_spec(Q_RANK), _vec_spec(KV_RANK), _vec_spec(128), _vec_spec(128)]
            + [_row_spec(tm, 128)] * 3)


def _mla_prep(proj, wqb, wkvb, qan, kvan, wq, wk, rope, tm=256):
    S = proj.shape[0]

    def body(pq_ref, pkv_ref, wqb_ref, wkvb_ref, qan_ref, kvan_ref, wq_ref, wk_ref, c_ref, s1_ref, s2_ref, qf_ref, kf_ref, ve_ref):
        _, _, kr, _, _, _, _, q, kv = _mla_latents(pq_ref, pkv_ref, wqb_ref, wkvb_ref, qan_ref, kvan_ref)
        C, S1, S2, wqv, wkv = c_ref[...], s1_ref[...], s2_ref[...], wq_ref[...], wk_ref[...]
        lane = _iota((tm, 128), 1)
        mn, mr = lane < 64, (lane >= 64) & (lane < 96)
        rrk = lax.rsqrt(_seg_sum(kr * kr, mr) / ROPE_DIM + EPS)
        ykr = _rope(jnp.where(mr, kr * rrk * wkv, 0.0), C, S1, S2)
        for h in range(N_HEADS):
            sl = slice(h * 128, (h + 1) * 128)
            t = q[:, sl]
            rn = lax.rsqrt(_seg_sum(t * t, mn) / NOPE + EPS)
            rr = lax.rsqrt(_seg_sum(t * t, mr) / ROPE_DIM + EPS)
            qf_ref[:, sl] = _rope(t * jnp.where(mn, rn, jnp.where(mr, rr, 0.0)) * wqv, C, S1, S2).astype(BF16)
            t = kv[:, sl]
            rn = lax.rsqrt(_seg_sum(t * t, mn) / NOPE + EPS)
            kf_ref[:, sl] = (jnp.where(mn, t * rn * wkv, 0.0) + ykr).astype(BF16)
            ve_ref[:, sl] = (jnp.where(mn, pltpu.roll(t, 64, 1), 0.0) if h % 2 == 0 else jnp.where(mn, 0.0, t)).astype(BF16)

    pspecs, wspecs = _mla_specs(tm)
    rs = _row_spec(tm, D)
    return pl.pallas_call(body, grid=(S // tm,), in_specs=pspecs + wspecs, out_specs=(rs, rs, rs),
                          out_shape=(_sds((S, D), BF16),) * 3, name="mla_prep", compiler_params=_PAR)(
        proj, proj, wqb, wkvb, qan, kvan, wq, wk, *rope)


def _mla_prep_bwd(proj, dqf, dkf, dve, wqb, wkvb, qan, kvan, wq, wk, rope, tm=256):
    S = proj.shape[0]

    def body(pq_ref, pkv_ref, wqb_ref, wkvb_ref, qan_ref, kvan_ref, wq_ref, wk_ref, c_ref, s1_ref, s2_ref,
             dqf_ref, dkf_ref, dve_ref, dqs_ref, dkvs_ref, dwqb_ref, dwkvb_ref, st_ref, dq_scr, dkv_scr):
        @pl.when(pl.program_id(0) == 0)
        def _():
            dwqb_ref[...] = jnp.zeros_like(dwqb_ref)
            dwkvb_ref[...] = jnp.zeros_like(dwkvb_ref)
            st_ref[...] = jnp.zeros_like(st_ref)

        ql, ckv, kr, rq, rkv, nq, nkv, q, kv = _mla_latents(pq_ref, pkv_ref, wqb_ref, wkvb_ref, qan_ref, kvan_ref)
        C, S1, S2, wqv, wkv = c_ref[...], s1_ref[...], s2_ref[...], wq_ref[...], wk_ref[...]
        lane = _iota((tm, 128), 1)
        mn, mr = lane < 64, (lane >= 64) & (lane < 96)
        dwq = jnp.zeros((1, 128), F32)
        dwk = jnp.zeros((1, 128), F32)
        dykr = jnp.zeros((tm, 128), F32)
        for h in range(N_HEADS):
            sl = slice(h * 128, (h + 1) * 128)
            t = q[:, sl]
            rn = lax.rsqrt(_seg_sum(t * t, mn) / NOPE + EPS)
            rr = lax.rsqrt(_seg_sum(t * t, mr) / ROPE_DIM + EPS)
            scale = jnp.where(mn, rn, jnp.where(mr, rr, 0.0))
            that = t * scale
            du = _rope_t(dqf_ref[:, sl], C, S1, S2)
            dwq += jnp.sum(du * that, 0, keepdims=True)
            g = du * wqv
            gt = g * that
            dq_scr[:, sl] = scale * (g - that * jnp.where(mn, _seg_sum(gt, mn) / NOPE, _seg_sum(gt, mr) / ROPE_DIM))
            t = kv[:, sl]
            rn = lax.rsqrt(_seg_sum(t * t, mn) / NOPE + EPS)
            that = jnp.where(mn, t * rn, 0.0)
            dkf = dkf_ref[:, sl]
            dykr += jnp.where(mr, dkf, 0.0)
            dkn = jnp.where(mn, dkf, 0.0)
            dwk += jnp.sum(dkn * that, 0, keepdims=True)
            g = dkn * wkv
            dve = dve_ref[:, sl]
            dkv_scr[:, sl] = jnp.where(mn, rn * (g - that * (jnp.sum(g * that, -1, keepdims=True) / NOPE)),
                                       pltpu.roll(dve, 64, 1) if h % 2 == 0 else dve)
        rrk = lax.rsqrt(_seg_sum(kr * kr, mr) / ROPE_DIM + EPS)
        that = jnp.where(mr, kr * rrk, 0.0)
        dukr = jnp.where(mr, _rope_t(dykr, C, S1, S2), 0.0)
        dwk += jnp.sum(dukr * that, 0, keepdims=True)
        g = dukr * wkv
        dkr = rrk * (g - that * (jnp.sum(g * that, -1, keepdims=True) / ROPE_DIM))
        dqv, dkvv = dq_scr[...].astype(BF16), dkv_scr[...].astype(BF16)
        dnq = _dot(dqv, wqb_ref[...], _NT)
        dwqb_ref[...] += _dot(nq, dqv, _TN)
        dnkv = jnp.zeros((tm, KV_RANK), F32)
        for k in range(4):
            dnkv += _dot(dkvv[:, k * 256:(k + 1) * 256], wkvb_ref[k], _NT)
            dwkvb_ref[k] += _dot(nkv, dkvv[:, k * 256:(k + 1) * 256], _TN)
        xhat = ql * rq
        st_ref[0:1, 0:Q_RANK] += jnp.sum(dnq * xhat, 0, keepdims=True)
        g = dnq * qan_ref[...]
        dqs_ref[...] = (rq * (g - xhat * jnp.mean(g * xhat, -1, keepdims=True))).astype(BF16)
        xhat = ckv * rkv
        st_ref[1:2, 0:KV_RANK] += jnp.sum(dnkv * xhat, 0, keepdims=True)
        g = dnkv * kvan_ref[...]
        dkvs_ref[:, 0:KV_RANK] = (rkv * (g - xhat * jnp.mean(g * xhat, -1, keepdims=True))).astype(BF16)
        dkvs_ref[:, KV_RANK:KV_RANK + 128] = dkr.astype(BF16)
        st_ref[2:3, 0:128] += dwq
        st_ref[3:4, 0:128] += dwk

    pspecs, wspecs = _mla_specs(tm)
    rs = _row_spec(tm, D)
    full = lambda r, n: pl.BlockSpec((r, n), lambda i: (0, 0))
    return pl.pallas_call(
        body, grid=(S // tm,), in_specs=pspecs + wspecs + [rs, rs, rs],
        out_specs=(_row_spec(tm, 384), _row_spec(tm, 384), full(Q_RANK, D), pl.BlockSpec((4, KV_RANK, 256), lambda i: (0, 0, 0)), full(8, D)),
        out_shape=(_sds((S, 384), BF16), _sds((S, 384), BF16), _sds((Q_RANK, D)), _sds((4, KV_RANK, 256)), _sds((8, D))),
        scratch_shapes=[pltpu.VMEM((tm, D), F32), pltpu.VMEM((tm, D), F32)], name="mla_prep_bwd", compiler_params=_ARB)(
        proj, proj, wqb, wkvb, qan, kvan, wq, wk, *rope, dqf, dkf, dve)


_ATT_SCALE = (NOPE + ROPE_DIM) ** -0.5


def _att_probs(q, k, i, tq):
    n = k.shape[0]
    s = _dot(q, k, _NT) * _ATT_SCALE
    tri = _iota((tq, tq), 1) <= _iota((tq, tq), 0)
    diag = jnp.where(tri, s[:, n - tq:], -1e30)
    s = diag if n == tq else jnp.concatenate([s[:, :n - tq], diag], 1)
    p = jnp.exp(s - jnp.max(s, -1, keepdims=True))
    return p * (1.0 / jnp.sum(p, -1, keepdims=True))


def _attn_fwd(qf, kf, ve, tq=256):
    S = qf.shape[0]

    def body(q_ref, k_ref, v_ref, o_ref):
        for i in range(S // tq):
            n, rows = (i + 1) * tq, slice(i * tq, (i + 1) * tq)
            acc = jnp.zeros((tq, 128), F32)
            for hh in range(2):
                sl = slice(hh * 128, (hh + 1) * 128)
                p = _att_probs(q_ref[rows, sl], k_ref[0:n, sl], i, tq)
                acc += _dot(p.astype(BF16), v_ref[0:n, sl], _NN)
            o_ref[rows, :] = acc.astype(BF16)

    ps = pl.BlockSpec((S, 256), lambda h: (0, h))
    return pl.pallas_call(body, grid=(N_HEADS // 2,), in_specs=[ps, ps, ps], out_specs=pl.BlockSpec((S, 128), lambda h: (0, h)),
                          out_shape=_sds((S, 512), BF16), name="attn_fwd", compiler_params=_PAR)(qf, kf, ve)


def _attn_bwd(qf, kf, ve, do, tq=256):
    S = qf.shape[0]

    def body(q_ref, k_ref, v_ref, do_ref, dq_ref, dk_ref, dv_ref):
        dk_ref[...] = jnp.zeros_like(dk_ref)
        dv_ref[...] = jnp.zeros_like(dv_ref)
        for i in range(S // tq):
            n, rows = (i + 1) * tq, slice(i * tq, (i + 1) * tq)
            dob = do_ref[rows, :].astype(BF16)
            for hh in range(2):
                sl = slice(hh * 128, (hh + 1) * 128)
                q, k = q_ref[rows, sl], k_ref[0:n, sl]
                p = _att_probs(q, k, i, tq)
                dv_ref[0:n, sl] += _dot(p.astype(BF16), dob, _TN)
                dp = _dot(dob, v_ref[0:n, sl], _NT)
                ds = (p * (dp - jnp.sum(dp * p, -1, keepdims=True)) * _ATT_SCALE).astype(BF16)
                dq_ref[rows, sl] = _dot(ds, k, _NN)
                dk_ref[0:n, sl] += _dot(ds, q, _TN)

    ps = pl.BlockSpec((S, 256), lambda h: (0, h))
    return pl.pallas_call(body, grid=(N_HEADS // 2,), in_specs=[ps, ps, ps, pl.BlockSpec((S, 128), lambda h: (0, h))], out_specs=(ps, ps, ps),
                          out_shape=(_sds((S, D)),) * 3, name="attn_bwd", compiler_params=_PAR)(qf, kf, ve, do)


def _softplus(x):
    return jnp.maximum(x, 0.0) + jnp.log1p(jnp.exp(-jnp.abs(x)))


def _ssd_chunk(xc_ref, dtr_ref, dtb_ref, al_ref, e_ref):
    L = SSD_L
    a = -jnp.exp(al_ref[...])
    dtp = _softplus(dtr_ref[...] + dtb_ref[...])
    causal = _iota((L, L), 1) <= _iota((L, L), 0)
    cs = _dot_hi(causal.astype(F32), dtp * a)
    E = e_ref[...]
    dtx, csx = _dot_hi(dtp, E), _dot_hi(cs, E)
    X = xc_ref[:, 0:SSD_INNER]
    Xd = X * dtx
    dec_out = jnp.exp(csx)
    dec_st = jnp.exp(csx[L - 1:L, :] - csx)
    return a, dtp, causal, cs, cs.T, dtx, X, Xd, dec_out, dec_st


def _ssd_decay(causal, cs, cs_row, h):
    diff = cs[:, h:h + 1] - cs_row[h:h + 1, :]
    return jnp.where(causal, jnp.exp(jnp.where(causal, diff, 0.0)), 0.0)


def _ssd_groups(xc_ref, g):
    b0, c0 = SSD_INNER + g * SSD_N, SSD_INNER + 2 * SSD_N + g * SSD_N
    return xc_ref[:, b0:b0 + SSD_N].astype(BF16), xc_ref[:, c0:c0 + SSD_N].astype(BF16)


def _pair_decay(cs, pair):
    L = SSD_L
    return jnp.where(_iota((128, 128), 0) < 64, jnp.exp(cs[L - 1:L, 2 * pair:2 * pair + 1]), jnp.exp(cs[L - 1:L, 2 * pair + 1:2 * pair + 2]))


def _ssd_in_specs(nc, rev):
    idx = (lambda c: nc - 1 - c) if rev else (lambda c: c)
    return [pl.BlockSpec((SSD_L, SSD_CONV_DIM), lambda c: (idx(c), 0)), pl.BlockSpec((SSD_L, 128), lambda c: (idx(c), ODT // 128)),
            _vec_spec(128), _vec_spec(128), _vec_spec(SSD_INNER), pl.BlockSpec((128, SSD_INNER), lambda c: (0, 0))]


def _ssd_core(xc, proj, dtb, alog, dskip, E):
    S = xc.shape[0]
    nc = S // SSD_L

    def body(xc_ref, dtr_ref, dtb_ref, al_ref, dx_ref, e_ref, y_ref, hp_ref, h_scr):
        @pl.when(pl.program_id(0) == 0)
        def _():
            h_scr[...] = jnp.zeros_like(h_scr)

        hp_ref[0] = h_scr[...]
        _, _, causal, cs, cs_row, _, X, Xd, dec_out, dec_st = _ssd_chunk(xc_ref, dtr_ref, dtb_ref, al_ref, e_ref)
        Xs = Xd * dec_st
        lane = _iota((SSD_L, 128), 1)
        for g in range(2):
            Bg, Cg = _ssd_groups(xc_ref, g)
            CB = _dot(Cg, Bg, _NT)
            for pr in range(4):
                pair = g * 4 + pr
                psl = slice(pair * 128, (pair + 1) * 128)
                Xdp = Xd[:, psl].astype(BF16)
                r0 = _dot((CB * _ssd_decay(causal, cs, cs_row, 2 * pair)).astype(BF16), Xdp, _NN)
                r1 = _dot((CB * _ssd_decay(causal, cs, cs_row, 2 * pair + 1)).astype(BF16), Xdp, _NN)
                Hp = h_scr[psl, :]
                W = _dot(Cg, Hp.astype(BF16), _NT)
                y_ref[:, psl] = jnp.where(lane < 64, r0, r1) + W * dec_out[:, psl] + X[:, psl] * dx_ref[:, psl]
                h_scr[psl, :] = Hp * _pair_decay(cs, pair) + _dot(Xs[:, psl].astype(BF16), Bg, _TN)

    return pl.pallas_call(
        body, grid=(nc,), in_specs=_ssd_in_specs(nc, False),
        out_specs=(pl.BlockSpec((SSD_L, SSD_INNER), lambda c: (c, 0)), pl.BlockSpec((1, SSD_INNER, SSD_N), lambda c: (c, 0, 0))),
        out_shape=(_sds((S, SSD_INNER)), _sds((nc, SSD_INNER, SSD_N))), scratch_shapes=[pltpu.VMEM((SSD_INNER, SSD_N), F32)],
        name="ssd_core", compiler_params=_ARB)(xc, proj, dtb, alog, dskip, E)


def _ssd_core_bwd(xc, proj, hprev, dy, dtb, alog, dskip, E, ET):
    S = xc.shape[0]
    nc = S // SSD_L
    L = SSD_L

    def body(xc_ref, dtr_ref, dtb_ref, al_ref, dx_ref, e_ref, et_ref, hp_ref, dy_ref, dxc_ref, ddt_ref, st_ref, dh_scr, acc_scr):
        step = pl.program_id(0)

        @pl.when(step == 0)
        def _():
            dh_scr[...] = jnp.zeros_like(dh_scr)
            acc_scr[...] = jnp.zeros_like(acc_scr)
            st_ref[...] = jnp.zeros_like(st_ref)

        a, dtp, causal, cs, cs_row, dtx, X, Xd, dec_out, dec_st = _ssd_chunk(xc_ref, dtr_ref, dtb_ref, al_ref, e_ref)
        lane = _iota((L, 128), 1)
        sub = _iota((128, L), 0)
        dcs_col = jnp.zeros((L, 128), F32)
        dcs_row = jnp.zeros((128, L), F32)
        dcs_last = jnp.zeros((1, 128), F32)
        dcsx, ddtx, dlastx = [], [], []
        for g in range(2):
            Bg, Cg = _ssd_groups(xc_ref, g)
            CB = _dot(Cg, Bg, _NT)
            dCB = jnp.zeros((L, L), F32)
            dB = jnp.zeros((L, SSD_N), F32)
            dC = jnp.zeros((L, SSD_N), F32)
            for pr in range(4):
                pair = g * 4 + pr
                psl = slice(pair * 128, (pair + 1) * 128)
                dY, Xp, Xdp, dop, dsp = dy_ref[:, psl], X[:, psl], Xd[:, psl], dec_out[:, psl], dec_st[:, psl]
                Xdb = Xdp.astype(BF16)
                acc_scr[0:1, psl] += jnp.sum(dY * Xp, 0, keepdims=True)
                Hp = hp_ref[0, psl, :]
                Hb = Hp.astype(BF16)
                dW = (dY * dop).astype(BF16)
                dcx = dY * _dot(Cg, Hb, _NT) * dop
                dC += _dot(dW, Hb, _NN)
                dHp = _dot(dW, Cg, _TN)
                dHn = dh_scr[psl, :]
                cd = _pair_decay(cs, pair)
                dh_scr[psl, :] = dHp + dHn * cd
                rsum = jnp.sum(dHn * Hp * cd, -1, keepdims=True)
                half = _iota((128, 1), 0) < 64
                s0 = jnp.sum(jnp.where(half, rsum, 0.0), 0, keepdims=True)
                s1 = jnp.sum(jnp.where(half, 0.0, rsum), 0, keepdims=True)
                lane1 = _iota((1, 128), 1)
                dcs_last += jnp.where(lane1 == 2 * pair, s0, 0.0) + jnp.where(lane1 == 2 * pair + 1, s1, 0.0)
                dHb = dHn.astype(BF16)
                dXs = _dot(Bg, dHb, _NT)
                dB += _dot((Xdp * dsp).astype(BF16), dHb, _NN)
                dXd = dXs * dsp
                e_st = dXs * Xdp * dsp
                dcx -= e_st
                dlastx.append(jnp.sum(e_st, 0, keepdims=True))
                for i in range(2):
                    h = 2 * pair + i
                    Dm = _ssd_decay(causal, cs, cs_row, h)
                    M = CB * Dm
                    dYm = jnp.where((lane < 64) if i == 0 else (lane >= 64), dY, 0.0).astype(BF16)
                    dM = _dot(dYm, Xdb, _NT)
                    dXd += _dot(M.astype(BF16), dYm, _TN)
                    dCB += dM * Dm
                    Em = dM * M
                    dcs_col += jnp.where(lane == h, jnp.sum(Em, -1, keepdims=True), 0.0)
                    dcs_row += jnp.where(sub == h, jnp.sum(Em, 0, keepdims=True), 0.0)
                dxc_ref[:, psl] = dY * dx_ref[:, psl] + dXd * dtx[:, psl]
                ddtx.append(dXd * Xp)
                dcsx.append(dcx)
            dCBb = dCB.astype(BF16)
            b0, c0 = SSD_INNER + g * SSD_N, SSD_INNER + 2 * SSD_N + g * SSD_N
            dxc_ref[:, b0:b0 + SSD_N] = dB + _dot(dCBb, Cg, _TN)
            dxc_ref[:, c0:c0 + SSD_N] = dC + _dot(dCBb, Bg, _NN)
        ET = et_ref[...]
        dcs = dcs_col - dcs_row.T + _dot_hi(jnp.concatenate(dcsx, 1), ET)
        dlast = dcs_last + _dot_hi(jnp.broadcast_to(jnp.concatenate(dlastx, 1), (8, SSD_INNER)), ET)[0:1, :]
        dcs += jnp.where(_iota((L, 128), 0) == L - 1, dlast, 0.0)
        dda = _dot_hi((_iota((L, L), 1) >= _iota((L, L), 0)).astype(F32), dcs)
        ddtp = dda * a + _dot_hi(jnp.concatenate(ddtx, 1), ET)
        draw = ddtp * _sigmoid(dtr_ref[...] + dtb_ref[...])
        ddt_ref[...] = draw.astype(BF16)
        st_ref[0:1, :] += jnp.sum(draw, 0, keepdims=True)
        st_ref[1:2, :] += jnp.sum(dda * dtp, 0, keepdims=True) * a

        @pl.when(step == nc - 1)
        def _():
            st_ref[2:3, :] = _dot_hi(acc_scr[...], ET)[0:1, :]

    rev = lambda c: (nc - 1 - c, 0)
    return pl.pallas_call(
        body, grid=(nc,),
        in_specs=_ssd_in_specs(nc, True) + [pl.BlockSpec((SSD_INNER, 128), lambda c: (0, 0)),
                                            pl.BlockSpec((1, SSD_INNER, SSD_N), lambda c: (nc - 1 - c, 0, 0)),
                                            pl.BlockSpec((L, SSD_INNER), rev)],
        out_specs=(pl.BlockSpec((L, SSD_CONV_DIM), rev), pl.BlockSpec((L, 128), rev), pl.BlockSpec((8, 128), lambda c: (0, 0))),
        out_shape=(_sds((S, SSD_CONV_DIM)), _sds((S, 128), BF16), _sds((8, 128))),
        scratch_shapes=[pltpu.VMEM((SSD_INNER, SSD_N), F32), pltpu.VMEM((8, SSD_INNER), F32)],
        name="ssd_core_bwd", compiler_params=_ARB)(xc, proj, dtb, alog, dskip, E, ET, hprev, dy)


def _ssd_post(y, proj, nw, tm=256):
    S = y.shape[0]

    def body(y_ref, z_ref, nw_ref, o_ref):
        for g in range(2):
            sl = slice(g * 512, (g + 1) * 512)
            gated = y_ref[:, sl] * _silu(z_ref[:, sl])
            r = lax.rsqrt(jnp.mean(gated * gated, -1, keepdims=True) + EPS)
            o_ref[:, sl] = (gated * r * nw_ref[:, sl]).astype(BF16)

    return pl.pallas_call(
        body, grid=(S // tm,), in_specs=[_row_spec(tm, D), pl.BlockSpec((tm, D), lambda i: (i, OZ // D)), _vec_spec(D)],
        out_specs=_row_spec(tm, D), out_shape=_sds((S, D), BF16), name="ssd_post", compiler_params=_PAR)(y, proj, nw)


def _ssd_post_bwd(doc, y, proj, nw, tm=256):
    S = y.shape[0]

    def body(d_ref, y_ref, z_ref, nw_ref, dy_ref, dz_ref, st_ref):
        @pl.when(pl.program_id(0) == 0)
        def _():
            st_ref[...] = jnp.zeros_like(st_ref)

        for g in range(2):
            sl = slice(g * 512, (g + 1) * 512)
            yv, zv, dv = y_ref[:, sl], z_ref[:, sl], d_ref[:, sl]
            sz = _silu(zv)
            gated = yv * sz
            r = lax.rsqrt(jnp.mean(gated * gated, -1, keepdims=True) + EPS)
            ghat = gated * r
            st_ref[0:1, sl] += jnp.sum(dv * ghat, 0, keepdims=True)
            gg = dv * nw_ref[:, sl]
            dg = r * (gg - ghat * jnp.mean(gg * ghat, -1, keepdims=True))
            dy_ref[:, sl] = dg * sz
            dz_ref[:, sl] = (dg * yv * _dsilu(zv)).astype(BF16)

    zs = pl.BlockSpec((tm, D), lambda i: (i, OZ // D))
    return pl.pallas_call(
        body, grid=(S // tm,), in_specs=[_row_spec(tm, D), _row_spec(tm, D), zs, _vec_spec(D)],
        out_specs=(_row_spec(tm, D), _row_spec(tm, D), _vec_spec(D, 8)), out_shape=(_sds((S, D)), _sds((S, D), BF16), _sds((8, D))),
        name="ssd_post_bwd", compiler_params=_ARB)(doc, y, proj, nw)


def _row(v, n=None):
    v = v.astype(F32).reshape(1, -1)
    return v if n is None else jnp.pad(v, ((0, 0), (0, n - v.shape[1])))


_IN_SEGMENTS = [(0, 384, OQ), (384, 640, OKV), (640, 672, OKV + 320), (672, 1184, OP), (1184, 2208, OZ), (2208, 3744, OX),
                (3744, 3760, ODT), (3760, IN_DIM, OG)]
_IN_ZEROS = [(OKV + 256, OKV + 320), (OKV + 352, OKV + 384), (ODT + 16, ODT + 128)]


def _in_pieces():
    w, out = IN_DIM // 4, []
    for a, b, d in _IN_SEGMENTS:
        while a < b:
            k = a // w
            e = min(b, (k + 1) * w)
            out.append((k, a - k * w, e - k * w, d))
            d, a = d + e - a, e
    return out


def _win_layout(w_in4, tm=256):
    def body(w_ref, o_ref):
        for k, s0, s1, d in _in_pieces():
            o_ref[:, d:d + s1 - s0] = w_ref[k, :, s0:s1]
        for z0, z1 in _IN_ZEROS:
            o_ref[:, z0:z1] = jnp.zeros((tm, z1 - z0), o_ref.dtype)

    return pl.pallas_call(
        body, grid=(D // tm,), in_specs=[pl.BlockSpec((4, tm, IN_DIM // 4), lambda i: (0, i, 0))],
        out_specs=pl.BlockSpec((tm, IN_PAD), lambda i: (i, 0)), out_shape=_sds((D, IN_PAD), w_in4.dtype), name="win_layout",
        compiler_params=_PAR)(w_in4)


def _win_unlayout(dwin, tm=256):
    def body(d_ref, o_ref):
        for k, s0, s1, d in _in_pieces():
            o_ref[k, :, s0:s1] = d_ref[:, d:d + s1 - s0]

    return pl.pallas_call(
        body, grid=(D // tm,), in_specs=[pl.BlockSpec((tm, IN_PAD), lambda i: (i, 0))],
        out_specs=pl.BlockSpec((4, tm, IN_DIM // 4), lambda i: (0, i, 0)), out_shape=_sds((4, D, IN_DIM // 4), dwin.dtype),
        name="win_unlayout", compiler_params=_PAR)(dwin)


def _prep_late(p):
    return dict(wbr=p["w_branch"].astype(BF16), wo=p["w_out"].astype(BF16), wup4=p["ffn_up4"].astype(BF16), wdn=p["ffn_down"].astype(BF16))


def _prep_layer(p, late=None):
    win = _win_layout(p["w_in4"].astype(BF16))
    wqb = jnp.transpose(p["w_q_b4"].astype(BF16).reshape(4, Q_RANK, 2, 96), (1, 0, 2, 3))
    return dict(
        win=win, wqb=jnp.pad(wqb, ((0, 0), (0, 0), (0, 0), (0, 32))).reshape(Q_RANK, D), wkvb4=p["w_kv_b4"].astype(BF16),
        late=late if late is not None else (lambda after: _prep_late(p)),
        nw1=_row(p["norm1_w"]), nw2=_row(p["norm2_w"]), qan=_row(p["q_a_norm"]), kvan=_row(p["kv_a_norm"]),
        wq=_row(p["q_norm"], 128), wk=_row(p["k_norm"], 128), pool_w=p["pool_w"].astype(F32), pool_scale=_row(p["pool_scale"]),
        cw=p["ssd_conv_w"].astype(F32), cb=_row(p["ssd_conv_b"]), dtb=_row(p["ssd_dt_bias"], 128), alog=_row(p["ssd_a_log"], 128),
        dskip=_row(jnp.repeat(p["ssd_d"].astype(F32), SSD_P)), snw=_row(p["ssd_norm_w"]),
        fcw=p["ffn_conv_w"].astype(F32), fcb=_row(p["ffn_conv_b"]))


def _layer_fwd(x, mod8, W, rope, E, tag):
    sh1, sc1, g1, sh2, sc2, g2 = (mod8[i:i + 1] for i in range(6))
    h1 = _ln_mod(x, W["nw1"], sc1, sh1, name=f"ln1_{tag}")
    proj = _mm(h1, W["win"], tn=640, tk=1024, name=f"proj_{tag}")
    qf, kf, ve = _mla_prep(proj, W["wqb"], W["wkvb4"], W["qan"], W["kvan"], W["wq"], W["wk"], rope)
    oa = _attn_fwd(qf, kf, ve)
    ob = _pool_fwd(proj, W["pool_w"], W["pool_scale"])
    xc = _ssd_pre(proj, W["cw"], W["cb"])
    y, hprev = _ssd_core(xc, proj, W["dtb"], W["alog"], W["dskip"], E)
    oc = _ssd_post(y, proj, W["snw"])
    W.update(W["late"](oc))
    merged = _merge_fwd(oa, ob, oc, proj, W["wbr"])
    x1, out1 = _mm(merged, W["wo"], tk=1024, res=x, gate=g1, name=f"wout_{tag}")
    h2 = _ln_mod(x1, W["nw2"], sc2, sh2, name=f"ln2_{tag}")
    up = _up_fwd(h2, W["wup4"], name=f"up_{tag}")
    act = _ffn_act(up, W["fcw"], W["fcb"])
    x2, out2 = _mm(act, W["wdn"], tm=1024, res=x1, gate=g2, name=f"down_{tag}")
    saved = dict(x=x, h1=h1, proj=proj, qf=qf, kf=kf, ve=ve, oa=oa, ob=ob, oc=oc, xc=xc, hprev=hprev, y=y, merged=merged,
                 out1=out1, x1=x1, h2=h2, up=up, act=act, out2=out2)
    return x2, saved


def _layer_bwd(dx2, sv, mod8, W, rope, E, ET, tag, emit=None, mid=None):
    sc1, g1, sc2, g2 = mod8[1:2], mod8[2:3], mod8[4:5], mod8[5:6]
    proj = sv["proj"]
    dz2, dg2 = _gate_bwd(dx2, sv["out2"], g2, name=f"gate2_bwd_{tag}")
    dact = _mm(dz2, W["wdn"], "nt", tn=1408, tk=1024, name=f"down_dx_{tag}")
    dwdn = _mm(sv["act"], dz2, "tn", tm=1408, name=f"down_dw_{tag}")
    dup2, dfcw, dfcb = _ffn_act_bwd(sv["up"], dact, W["fcw"], W["fcb"])
    dh2 = _up_dx(dup2, W["wup4"], name=f"up_dx_{tag}")
    dwup4 = _up_dw(sv["h2"], dup2, name=f"up_dw_{tag}")
    dx1, st2 = _ln_mod_bwd(sv["x1"], dh2, dx2, W["nw2"], sc2, name=f"ln2_bwd_{tag}")
    dz1, dg1 = _gate_bwd(dx1, sv["out1"], g1, name=f"gate1_bwd_{tag}")
    dmerged = _mm(dz1, W["wo"], "nt", tk=1024, name=f"wout_dx_{tag}")
    dwo = _mm(sv["merged"], dz1, "tn", name=f"wout_dw_{tag}")
    dya, dyb, dyc, dgl, doa, dob, doc = _merge_bwd(dmerged, sv["oa"], sv["ob"], sv["oc"], proj, W["wbr"])
    dwba = _mm(sv["oa"], dya, "tn", name=f"wba_dw_{tag}")
    dwbb = _mm(sv["ob"], dyb, "tn", name=f"wbb_dw_{tag}")
    dwbc = _mm(sv["oc"], dyc, "tn", name=f"wbc_dw_{tag}")
    late = dict(w_branch=jnp.concatenate([dwba, dwbb, dwbc], 0).reshape(4, 512, D), w_out=dwo.reshape(4, 256, D), ffn_up=dwup4,
                ffn_down=dwdn.reshape(4, FFN // 4, D))
    snw = W["snw"]
    if emit is not None:
        token = emit(late)
        if token is not None:
            snw, doa = snw + token, doa + token
    dy, dzs, st_post = _ssd_post_bwd(doc, sv["y"], proj, snw)
    dxc, ddt, st_ssd = _ssd_core_bwd(sv["xc"], proj, sv["hprev"], dy, W["dtb"], W["alog"], W["dskip"], E, ET)
    dxbc, dcw, dcb = _ssd_pre_bwd(proj, dxc, W["cw"], W["cb"])
    if mid is not None:
        token = mid()
        if token is not None:
            doa, dob = doa + token, dob + token
    dpool, dpw, dps = _pool_bwd(proj, dob, W["pool_w"], W["pool_scale"])
    dqf, dkf, dve = _attn_bwd(sv["qf"], sv["kf"], sv["ve"], doa)
    dqs, dkvs, dwqb, dwkvb4, st_mla = _mla_prep_bwd(proj, dqf, dkf, dve, W["wqb"], W["wkvb4"], W["qan"], W["kvan"], W["wq"], W["wk"], rope)
    dproj = jnp.concatenate([dgl, dzs, dxbc, dpool, dqs, dkvs, ddt], 1)
    dh1 = _mm(dproj, W["win"], "nt", tk=1408, name=f"proj_dx_{tag}")
    dwin = _mm(sv["h1"], dproj, "tn", tn=640, name=f"proj_dw_{tag}")
    dx, st1 = _ln_mod_bwd(sv["x"], dh1, dx1, W["nw1"], sc1, name=f"ln1_bwd_{tag}")
    grads = dict(
        norm1_w=st1[2], norm2_w=st2[2], w_in=_win_unlayout(dwin),
        q_a_norm=st_mla[0, :Q_RANK], kv_a_norm=st_mla[1, :KV_RANK], q_norm=st_mla[2, :96], k_norm=st_mla[3, :96],
        w_q_b=jnp.transpose(dwqb.reshape(Q_RANK, 4, 2, 128)[:, :, :, :96], (1, 0, 2, 3)).reshape(4, Q_RANK, 192), w_kv_b=dwkvb4,
        pool_w=dpw, pool_scale=dps[0], ssd_conv_w=dcw, ssd_conv_b=dcb[0],
        ssd_dt_bias=st_ssd[0, :SSD_HEADS], ssd_a_log=st_ssd[1, :SSD_HEADS], ssd_d=st_ssd[2, :SSD_HEADS], ssd_norm_w=st_post[0],
        ffn_conv_w=jnp.transpose(dfcw, (1, 0, 2)).reshape(3, 2 * FFN), ffn_conv_b=dfcb.reshape(2 * FFN), **late)
    dmod = jnp.concatenate([st1[0:2], dg1[0:1], st2[0:2], dg2[0:1]], 0)
    return dx, grads, dmod


def _ssd_expand():
    E = (jnp.arange(SSD_INNER)[None, :] // SSD_P == jnp.arange(128)[:, None]).astype(F32)
    return E, E.T


def _rope_tables(positions):
    inv_freq = ROPE_THETA ** (-jnp.arange(0, ROPE_DIM, 2, dtype=F32) / ROPE_DIM)
    invf = jnp.concatenate([jnp.zeros((NOPE,), F32), inv_freq, inv_freq, jnp.zeros((32,), F32)]).reshape(1, 128)
    posb = jnp.broadcast_to(positions.astype(F32)[:, None], (positions.shape[0], 128))
    return _rope_tab(posb, invf)


def _local_step(x, target, positions, mods, get_layer, bwd_mod=None, emit=None, mid=None, done=None):
    rope = _rope_tables(positions)
    E, ET = _ssd_expand()
    Ws, saved, h = [], [], x
    for l in range(2):
        Ws.append(_prep_layer(*get_layer(l, h)))
        h, sv = _layer_fwd(h, mods[l], Ws[l], rope, E, l)
        saved.append(sv)
    dy, lpart = _loss_grad(h, target)
    grads, dmods = [None, None], [None, None]
    for l in (1, 0):
        mod8 = mods[l] if bwd_mod is None else bwd_mod(l)
        hook = lambda f: None if f is None else functools.partial(f, l)
        dy, grads[l], dmods[l] = _layer_bwd(dy, saved[l], mod8, Ws[l], rope, E, ET, l, hook(emit), hook(mid))
        if done is not None:
            done(l, grads[l])
    return lpart[0, 0], dy, grads, dmods


_ANY = pl.BlockSpec(memory_space=pl.ANY)
_VMEM = pl.BlockSpec(memory_space=pltpu.VMEM)


def _place():
    x, y, c = lax.axis_index("x"), lax.axis_index("y"), lax.axis_index("c")
    return x, y, c, [(1 - x, y), (x, 1 - y), (1 - x, 1 - y)]


def _allgather8(v, name):
    m_per, n = v.shape

    def body(x_ref, out_ref, send_sems, recv_sems, local_sem):
        x, y, c, chips = _place()
        me, sibling = (x, y, c), (x, y, 1 - c)

        def rows(px, py, pc):
            return out_ref.at[pl.ds((4 * px + 2 * py + pc) * m_per, m_per), :]

        def copy(k, block, to, src=None):
            return pltpu.make_async_remote_copy(src_ref=rows(*block) if src is None else src, dst_ref=rows(*block),
                                                send_sem=send_sems.at[k], recv_sem=recv_sems.at[k], device_id=to, device_id_type=MESH)

        mine = pltpu.make_async_copy(x_ref, rows(*me), local_sem)
        mine.start()
        first = [copy(0, me, sibling, src=x_ref)] + [copy(1 + j, me, (*chip, c), src=x_ref) for j, chip in enumerate(chips)]
        for cp in first:
            cp.start()
        passed = [copy(4 + j, (*chip, c), sibling) for j, chip in enumerate(chips)]
        for j, chip in enumerate(chips):
            copy(1 + j, (*chip, c), me).wait_recv()
            passed[j].start()
        copy(0, sibling, me).wait_recv()
        for j, chip in enumerate(chips):
            copy(4 + j, (*chip, 1 - c), me).wait_recv()
        for cp in first + passed:
            cp.wait_send()
        mine.wait()

    return pl.pallas_call(
        body, out_shape=_sds((8 * m_per, n), v.dtype), in_specs=[_VMEM], out_specs=_VMEM,
        scratch_shapes=[pltpu.SemaphoreType.DMA((7,)), pltpu.SemaphoreType.DMA((7,)), pltpu.SemaphoreType.DMA], name=name)(v)


def _sems(n):
    return [pltpu.SemaphoreType.DMA((n,)), pltpu.SemaphoreType.DMA((n,))]


_HBM = pl.BlockSpec(memory_space=pltpu.HBM)
_SEM = pl.BlockSpec(memory_space=pltpu.SEMAPHORE)
_EFFECT = pltpu.CompilerParams(has_side_effects=pltpu.SideEffectType.DATAFLOW_SIDE_EFFECTING)


def _ici_copy(src_refs, land_refs, send_sems, recv_sems, a, j, slices, incoming):
    x, y, c, chips = _place()
    if _peers(slices) == 1:
        src, dst = slices(src_refs[a], land_refs[a], 1 - c if incoming else c)
        return pltpu.make_async_remote_copy(src_ref=src, dst_ref=dst, send_sem=send_sems.at[a], recv_sem=recv_sems.at[a],
                                            device_id=(x, y, 1 - c), device_id_type=MESH)
    me, other = 2 * x + y, 2 * chips[j][0] + chips[j][1]
    src, dst = slices(src_refs[a], land_refs[a], other, me, c) if incoming else slices(src_refs[a], land_refs[a], me, other, c)
    return pltpu.make_async_remote_copy(src_ref=src, dst_ref=dst, send_sem=send_sems.at[3 * a + j], recv_sem=recv_sems.at[3 * a + j],
                                        device_id=(*chips[j], c), device_id_type=MESH)


def _peers(slices):
    return 1 if slices is _halves_slices else 3


def _ici_start(srcs, land_shapes, slices, after, name):
    na = len(srcs)

    def body(*refs):
        src_refs, land_refs, send_sems, recv_sems = refs[:na], refs[na:2 * na], refs[2 * na + 1], refs[2 * na + 2]
        for a in range(na):
            for j in range(_peers(slices)):
                _ici_copy(src_refs, land_refs, send_sems, recv_sems, a, j, slices, False).start()
        refs[-1][...] = jnp.zeros_like(refs[-1])

    hbm = lambda v: pltpu.with_memory_space_constraint(v, pltpu.HBM)
    lands = [hbm(lax.empty(s.shape, s.dtype)) for s in land_shapes]
    return pl.pallas_call(
        body, name=name,
        out_shape=(pltpu.SemaphoreType.DMA((_peers(slices) * na,)), pltpu.SemaphoreType.DMA((_peers(slices) * na,)),
                   *[pltpu.HBM(v.shape, v.dtype) for v in srcs],
                   *[pltpu.HBM(s.shape, s.dtype) for s in land_shapes], _sds((8, 128))),
        in_specs=[_HBM] * (2 * na) + [_ANY], out_specs=(_SEM, _SEM, *[_HBM] * (2 * na), _VMEM),
        input_output_aliases={i: 2 + i for i in range(2 * na)}, compiler_params=_EFFECT)(*[hbm(v) for v in srcs], *lands, after)


def _ici_wait(handle, slices, after, name):
    na = (len(handle) - 3) // 2

    def body(*refs):
        src_refs, land_refs, send_sems, recv_sems = refs[:na], refs[na:2 * na], refs[2 * na], refs[2 * na + 1]
        for a in range(na):
            for j in range(_peers(slices)):
                _ici_copy(src_refs, land_refs, send_sems, recv_sems, a, j, slices, False).wait_send()
                _ici_copy(src_refs, land_refs, send_sems, recv_sems, a, j, slices, True).wait_recv()

    thru = handle[2:2 + 2 * na]
    outs = pl.pallas_call(
        body, name=name, out_shape=[pltpu.HBM(v.shape, v.dtype) for v in thru], in_specs=[_HBM] * (2 * na) + [_SEM, _SEM, _ANY],
        out_specs=[_HBM] * (2 * na), input_output_aliases={i: i for i in range(2 * na)}, compiler_params=_EFFECT)(
        *thru, handle[0], handle[1], after)
    return outs[:na], outs[na:]


def _gather_slices(p_ref, land_ref, sender, receiver, c):
    r2 = p_ref.shape[0] // 2
    return p_ref.at[pl.ds(c * r2, r2), :], land_ref.at[sender, pl.ds(c * r2, r2), :]


def _scatter_slices(a_ref, t_ref, sender, receiver, c):
    return a_ref.at[receiver], t_ref.at[sender]


def _halves_slices(g_ref, land_ref, sender_c):
    r2 = g_ref.shape[1] // 2
    return g_ref.at[:, pl.ds((1 - sender_c) * r2, r2), :], land_ref


def _gather_start(arrs, after, name):
    return _ici_start(arrs, [_sds((4,) + v.shape, v.dtype) for v in arrs], _gather_slices, after, name)


def _gather_finish(handle, after, name):
    arrs, stacks = _ici_wait(handle, _gather_slices, after, name + "_wait")
    na = len(stacks)

    def body(*refs):
        s_refs, o_refs, (send_sems, recv_sems) = refs[:na], refs[na:2 * na], refs[2 * na:]
        x, y, c, chips = _place()

        def copy(a, j, cc, to):
            r2 = s_refs[a].shape[1] // 2
            at = (2 * chips[j][0] + chips[j][1], pl.ds(cc * r2, r2), slice(None))
            return pltpu.make_async_remote_copy(src_ref=s_refs[a].at[at], dst_ref=o_refs[a].at[at], send_sem=send_sems.at[3 * a + j],
                                                recv_sem=recv_sems.at[3 * a + j], device_id=to, device_id_type=MESH)

        passed = [copy(a, j, c, (x, y, 1 - c)) for a in range(na) for j in range(3)]
        for cp in passed:
            cp.start()
        for a in range(na):
            for j in range(3):
                copy(a, j, 1 - c, (x, y, c)).wait_recv()
        for cp in passed:
            cp.wait_send()

    stacks = pl.pallas_call(
        body, out_shape=[_sds(v.shape, v.dtype) for v in stacks], in_specs=[_ANY] * na, out_specs=[_ANY] * na,
        input_output_aliases={i: i for i in range(na)}, scratch_shapes=_sems(3 * na), name=name + "_pass")(*stacks)
    chip = 2 * lax.axis_index("x") + lax.axis_index("y")
    return [lax.dynamic_update_slice(s, v[None], (chip, 0, 0)) for s, v in zip(stacks, arrs)]


def _halves_start(gs, after, tag):
    return _ici_start(gs, [_sds((4, v.shape[1] // 2, v.shape[2]), v.dtype) for v in gs], _halves_slices, after, f"rs_halves_{tag}_start")


def _join_halves(fs, name):
    na = len(fs)

    def body(*refs):
        f_refs, o_refs, (send_sems, recv_sems) = refs[:na], refs[na:2 * na], refs[2 * na:]
        x, y, c, _ = _place()

        def copy(a, cc, to):
            r2 = f_refs[a].shape[0]
            return pltpu.make_async_remote_copy(src_ref=f_refs[a], dst_ref=o_refs[a].at[pl.ds(cc * r2, r2), :], send_sem=send_sems.at[a],
                                                recv_sem=recv_sems.at[a], device_id=to, device_id_type=MESH)

        cps = [copy(a, c, (x, y, 1 - c)) for a in range(na)]
        for cp in cps:
            cp.start()
        for a in range(na):
            copy(a, 1 - c, (x, y, c)).wait_recv()
        for cp in cps:
            cp.wait_send()

    outs = pl.pallas_call(
        body, out_shape=[_sds((2 * v.shape[0], v.shape[1]), v.dtype) for v in fs], in_specs=[_ANY] * na, out_specs=[_ANY] * na,
        scratch_shapes=_sems(na), name=name)(*fs)
    ci = lax.axis_index("c")
    return [lax.dynamic_update_slice(o, f, (ci * f.shape[0], 0)) for o, f in zip(outs, fs)]


def _sum_chips(a, t, chip, name):
    _, r2, n = t.shape
    tm = _row_tile(r2)

    def body(k_ref, a_ref, t1_ref, t2_ref, t3_ref, o_ref):
        o_ref[...] = ((a_ref[...].astype(F32) + t1_ref[...].astype(F32)) + t2_ref[...].astype(F32)) + t3_ref[...].astype(F32)

    def slot(j):
        return pl.BlockSpec((None, tm, n), lambda i, k_ref: (lax.rem(k_ref[0] + j, 4), i, 0))

    return pl.pallas_call(
        body, grid_spec=pltpu.PrefetchScalarGridSpec(num_scalar_prefetch=1, grid=(r2 // tm,), in_specs=[slot(0), slot(1), slot(2), slot(3)],
                                                     out_specs=pl.BlockSpec((tm, n), lambda i, k_ref: (i, 0))),
        out_shape=_sds((r2, n)), name=name, compiler_params=_PAR)(chip.reshape(1).astype(jnp.int32), a, t, t, t)


def _add_cast(g, recv, c, name):
    _, r2, n = recv.shape

    def body(c_ref, a_ref, b_ref, o_ref):
        o_ref[...] = (a_ref[...] + b_ref[...]).astype(BF16)

    spec = pl.BlockSpec((None, r2, n), lambda k, c_ref: (k, 0, 0))
    return pl.pallas_call(
        body, grid_spec=pltpu.PrefetchScalarGridSpec(
            num_scalar_prefetch=1, grid=(4,), in_specs=[pl.BlockSpec((None, r2, n), lambda k, c_ref: (k, c_ref[0], 0)), spec], out_specs=spec),
        out_shape=_sds(recv.shape, BF16), name=name, compiler_params=_PAR)(c.reshape(1).astype(jnp.int32), g, recv)


def _sum_lead(t, name, tm=256):
    P, R, n = t.shape
    tm = _row_tile(R, tm)

    def body(t_ref, o_ref):
        acc = t_ref[0].astype(F32)
        for j in range(1, P):
            acc = acc + t_ref[j].astype(F32)
        o_ref[...] = acc

    return pl.pallas_call(body, grid=(R // tm,), in_specs=[pl.BlockSpec((P, tm, n), lambda i: (0, i, 0))],
                          out_specs=pl.BlockSpec((tm, n), lambda i: (i, 0)), out_shape=_sds((R, n)), name=name, compiler_params=_PAR)(t)


def _ada_fwd(c16, ada_w, ada_b_cols, tn=512):
    L, _, n = ada_w.shape

    def body(c_ref, w_ref, b_ref, o_ref):
        o_ref[0] = _dot(_silu(c_ref[...]).astype(BF16), w_ref[0].astype(BF16), _NN) + b_ref[0]

    return pl.pallas_call(
        body, grid=(L, n // tn),
        in_specs=[pl.BlockSpec((16, D), lambda l, j: (0, 0)), pl.BlockSpec((1, D, tn), lambda l, j: (l, 0, j)), pl.BlockSpec((1, 1, tn), lambda l, j: (l, 0, j))],
        out_specs=pl.BlockSpec((1, 16, tn), lambda l, j: (l, 0, j)), out_shape=_sds((L, 16, n)), name="ada_fwd",
        compiler_params=pltpu.CompilerParams(dimension_semantics=("parallel", "parallel")))(c16, ada_w, ada_b_cols)


def _ada_bwd(c16, dmod, tn=512):
    L, _, n = dmod.shape

    def body(c_ref, d_ref, o_ref):
        o_ref[0] = _dot(_silu(c_ref[...]).astype(BF16), d_ref[0].astype(BF16), _TN)

    return pl.pallas_call(
        body, grid=(L, n // tn), in_specs=[pl.BlockSpec((16, D), lambda l, j: (0, 0)), pl.BlockSpec((1, 16, tn), lambda l, j: (l, 0, j))],
        out_specs=pl.BlockSpec((1, D, tn), lambda l, j: (l, 0, j)), out_shape=_sds((L, D, n)), name="ada_bwd",
        compiler_params=pltpu.CompilerParams(dimension_semantics=("parallel", "parallel")))(c16, dmod)


def _adam_math(w, g, m, v):
    mn = ADAM_B1 * m + (1.0 - ADAM_B1) * g
    vn = ADAM_B2 * v + (1.0 - ADAM_B2) * (g * g)
    m_hat = mn / (1.0 - ADAM_B1 ** ADAM_STEP)
    v_hat = vn / (1.0 - ADAM_B2 ** ADAM_STEP)
    return -ADAM_LR * (m_hat / (jnp.sqrt(v_hat) + ADAM_EPS) + ADAM_WD * w), mn, vn


def _adamw(w, g, m, v, name):
    R, n = w.shape
    tm = _row_tile(R)

    def body(w_ref, g_ref, m_ref, v_ref, d_ref, nm_ref, nv_ref):
        d_ref[...], nm_ref[...], nv_ref[...] = _adam_math(w_ref[...], g_ref[...], m_ref[...], v_ref[...])

    spec = pl.BlockSpec((tm, n), lambda i: (i, 0))
    return pl.pallas_call(body, grid=(R // tm,), in_specs=[spec] * 4, out_specs=(spec,) * 3, out_shape=(_sds((R, n)),) * 3,
                          name=name, compiler_params=_PAR)(w, g, m, v)


def _adamw_cols(w, g0, g1, m, v, name):
    fwd, back = (lambda t: jnp.transpose(t, (2, 0, 1))), (lambda t: jnp.transpose(t, (1, 2, 0)))
    gt = jnp.stack([g0.T, g1.T], 1)
    n, _, r = gt.shape
    tr = max(t for t in range(1, 257) if n % t == 0)

    def body(w_ref, g_ref, m_ref, v_ref, d_ref, nm_ref, nv_ref):
        d_ref[...], nm_ref[...], nv_ref[...] = _adam_math(w_ref[...], g_ref[...], m_ref[...], v_ref[...])

    spec = pl.BlockSpec((tr, 2, r), lambda i: (i, 0, 0))
    outs = pl.pallas_call(body, grid=(n // tr,), in_specs=[spec] * 4, out_specs=(spec,) * 3, out_shape=(_sds(gt.shape),) * 3,
                          name=name, compiler_params=_PAR)(fwd(w), gt, fwd(m), fwd(v))
    return (back(gt), *[back(o) for o in outs])


def _adamw_layers(w, g0, g1, m, v, after, name):
    _, r, n = w.shape
    tm = _row_tile(r)
    nb = r // tm

    def body(w_ref, g0_ref, g1_ref, m_ref, v_ref, after_ref, g_ref, d_ref, nm_ref, nv_ref):
        gv = jnp.where(pl.program_id(0) == 0, g0_ref[...], g1_ref[...])
        g_ref[...] = gv
        d_ref[...], nm_ref[...], nv_ref[...] = _adam_math(w_ref[...], gv, m_ref[...], v_ref[...])

    spec = pl.BlockSpec((None, tm, n), lambda l, i: (l, i, 0))
    g0_spec = pl.BlockSpec((tm, n), lambda l, i: (i * (1 - l) + (nb - 1) * l, 0))
    g1_spec = pl.BlockSpec((tm, n), lambda l, i: (i * l, 0))
    return pl.pallas_call(body, grid=(2, nb), in_specs=[spec, g0_spec, g1_spec, spec, spec, _ANY], out_specs=(spec,) * 4,
                          out_shape=(_sds(w.shape),) * 4, name=name,
                          compiler_params=pltpu.CompilerParams(dimension_semantics=("arbitrary", "arbitrary")))(w, g0, g1, m, v, after)


_W_NAMES = ["ada_w", "ada_b", "norm1_w", "w_in", "q_a_norm", "w_q_b", "kv_a_norm", "w_kv_b", "q_norm", "k_norm", "pool_w",
            "pool_scale", "ssd_conv_w", "ssd_conv_b", "ssd_dt_bias", "ssd_a_log", "ssd_d", "ssd_norm_w", "w_branch", "w_out",
            "norm2_w", "ffn_up", "ffn_conv_w", "ffn_conv_b", "ffn_down"]
_BIG = [("w_in", (D, IN_DIM // 4), 1), ("w_q_b", (Q_RANK, 192), 1), ("w_kv_b", (KV_RANK, 256), 1), ("w_branch", (512, D), 0),
        ("w_out", (256, D), 0), ("ffn_up", (D, 2 * FFN // 4), 1), ("ffn_down", (FFN // 4, D), 0)]

_SMALL = [("norm1_w", (D,)), ("q_a_norm", (Q_RANK,)), ("kv_a_norm", (KV_RANK,)), ("q_norm", (96,)), ("k_norm", (96,)),
          ("pool_w", (4, 128, 128)), ("pool_scale", (512,)), ("ssd_conv_w", (4, SSD_CONV_DIM)), ("ssd_conv_b", (SSD_CONV_DIM,)),
          ("ssd_dt_bias", (SSD_HEADS,)), ("ssd_a_log", (SSD_HEADS,)), ("ssd_d", (SSD_HEADS,)), ("ssd_norm_w", (D,)), ("norm2_w", (D,)),
          ("ffn_conv_w", (3, 2 * FFN)), ("ffn_conv_b", (2 * FFN,))]
_CONV_SHARDED = {"ssd_conv_w": SSD_CONV_DIM // 4, "ffn_conv_w": 2 * FFN // 4}


def _pack_flat(arrs, mult):
    flat = jnp.concatenate([a.astype(F32).reshape(-1) for a in arrs])
    rows = -(-flat.shape[0] // (128 * mult)) * mult
    return jnp.pad(flat, (0, rows * 128 - flat.shape[0])).reshape(rows, 128), [a.shape for a in arrs]


def _unpack_flat(packed, shapes):
    flat, out, off = packed.reshape(-1), [], 0
    for s in shapes:
        n = 1
        for d in s:
            n *= d
        out.append(flat[off:off + n].reshape(s))
        off += n
    return out


_EARLY = ["w_in", "w_q_b", "w_kv_b"]
_LATE = ["w_branch", "w_out", "ffn_up", "ffn_down"]


def _early_weights(a, l, stacks, conv_full):
    p = {n: a[n][l] for n in _W_NAMES if n not in ("ada_w", "ada_b")}
    p.update({n: conv_full[n][l] for n in conv_full})
    p.update(w_in4=stacks[0], w_q_b4=stacks[1], w_kv_b4=stacks[2])
    return p


def _late_weights(stacks):
    return _prep_late(dict(w_branch=stacks[0].reshape(2048, D), w_out=stacks[1].reshape(D, D), ffn_up4=stacks[2],
                           ffn_down=stacks[3].reshape(FFN, D)))


def _reduce_start(halves, ci, after, tag):
    chip_sum = []
    for t, h in halves.items():
        gs, recv = _ici_wait(h, _halves_slices, after, f"rs_halves_{t}_wait")
        chip_sum += [_add_cast(g, r, ci, f"rs_add_{t}") for g, r in zip(gs, recv)]
    return _ici_start(chip_sum, [_sds(v.shape, v.dtype) for v in chip_sum], _scatter_slices, after, f"rs_scatter_{tag}_start")


def _reduce_finish(started, after, tag):
    chip_sum, got = _ici_wait(started, _scatter_slices, after, f"rs_scatter_{tag}_wait")
    chip = 2 * lax.axis_index("x") + lax.axis_index("y")
    return _join_halves([_sum_chips(s, t, chip, f"rs_sum_{tag}") for s, t in zip(chip_sum, got)], f"rs_join_{tag}")


def kernel(x, c, positions, ada_w, ada_b, norm1_w, w_in, q_a_norm, w_q_b, kv_a_norm, w_kv_b, q_norm, k_norm, pool_w, pool_scale, ssd_conv_w, ssd_conv_b, ssd_dt_bias, ssd_a_log, ssd_d, ssd_norm_w, w_branch, w_out, norm2_w, ffn_up, ffn_conv_w, ffn_conv_b, ffn_down, loss_target, m_ada_w, m_ada_b, m_norm1_w, m_w_in, m_q_a_norm, m_w_q_b, m_kv_a_norm, m_w_kv_b, m_q_norm, m_k_norm, m_pool_w, m_pool_scale, m_ssd_conv_w, m_ssd_conv_b, m_ssd_dt_bias, m_ssd_a_log, m_ssd_d, m_ssd_norm_w, m_w_branch, m_w_out, m_norm2_w, m_ffn_up, m_ffn_conv_w, m_ffn_conv_b, m_ffn_down, v_ada_w, v_ada_b, v_norm1_w, v_w_in, v_q_a_norm, v_w_q_b, v_kv_a_norm, v_w_kv_b, v_q_norm, v_k_norm, v_pool_w, v_pool_scale, v_ssd_conv_w, v_ssd_conv_b, v_ssd_dt_bias, v_ssd_a_log, v_ssd_d, v_ssd_norm_w, v_w_branch, v_w_out, v_norm2_w, v_ffn_up, v_ffn_conv_w, v_ffn_conv_b, v_ffn_down):
    a = dict(locals())
    xi, yi, ci = lax.axis_index("x"), lax.axis_index("y"), lax.axis_index("c")
    chip = 2 * xi + yi
    dev = 2 * chip + ci
    ncol = 6 * D // 4

    big = _EARLY + _LATE
    shard = lambda names, l: [a[n][l].astype(BF16) for n in names]
    g0a = _gather_start(shard(_EARLY, 0), c, "gather_0a")
    g0b = _gather_start(shard(_LATE, 0), g0a[-1], "gather_0b")
    g1 = _gather_start(shard(big, 1), g0b[-1], "gather_1")

    c_all = _allgather8(c.reshape(8, 128) + g1[-1][0, 0], "gather_c").reshape(8, D)
    c16 = jnp.pad(c_all, ((0, 8), (0, 0)))
    ada_b_cols = lax.dynamic_slice_in_dim(ada_b, chip * ncol, ncol, axis=1).reshape(2, 1, ncol)
    mod_part = _ada_fwd(c16, ada_w, ada_b_cols)[:, :8]
    small1, shapes1 = _pack_flat([mod_part, ssd_conv_w, ffn_conv_w], 8)
    got1 = _allgather8(small1, "gather_mod").reshape(8, -1, 128)
    per_chip = [_unpack_flat(got1[2 * k], shapes1) for k in range(4)]
    mod_all = jnp.concatenate([per_chip[k][0] for k in range(4)], -1)
    conv_full = {"ssd_conv_w": jnp.concatenate([per_chip[k][1] for k in range(4)], -1),
                 "ffn_conv_w": jnp.concatenate([per_chip[k][2] for k in range(4)], -1)}
    mod_mine = lax.dynamic_index_in_dim(mod_all, dev, axis=1, keepdims=False).reshape(2, 6, D)
    mods = [jnp.pad(mod_mine[l], ((0, 2), (0, 0))) for l in range(2)]

    def get_layer(l, after):
        if l == 0:
            return (_early_weights(a, 0, _gather_finish(g0a, mods[0], "gather_0a"), conv_full),
                    lambda aft: _late_weights(_gather_finish(g0b, aft, "gather_0b")))
        stacks = _gather_finish(g1, after, "gather_1")
        return _early_weights(a, 1, stacks[:3], conv_full), lambda aft: _late_weights(stacks[3:])

    halves, scatters = {}, {}

    def bwd_mod(l):
        return mods[l] if l == 1 else mods[0] + scatters["1"][-1][0, 0]

    def emit(l, late):
        behind = late["ffn_down"] if l == 1 else scatters["1"][-1]
        halves[f"{l}b"] = _halves_start([late[n] for n in _LATE], behind, f"{l}b")
        return halves[f"{l}b"][-1][0, 0]

    def mid(l):
        if l == 0:
            scatters["0b"] = _reduce_start({"0b": halves["0b"]}, ci, halves["0b"][-1], "0b")
            return scatters["0b"][-1][0, 0]

    def done(l, grads_l):
        halves[f"{l}a"] = _halves_start([grads_l[n] for n in _EARLY], halves[f"{l}b"][-1], f"{l}a")
        if l == 1:
            scatters["1"] = _reduce_start({"1b": halves["1b"], "1a": halves["1a"]}, ci, halves["1a"][-1], "1")

    lpart, grad_x, grads, dmods = _local_step(x[0], loss_target[0], positions[0], mods, get_layer, bwd_mod, emit, mid, done)
    loss = lax.psum(lpart, ("x", "y", "c"))
    red1 = dict(zip(_LATE + _EARLY, _reduce_finish(scatters["1"], grad_x, "1")))
    red0b = _reduce_finish(scatters["0b"], red1["w_in"], "0b")
    scatters["0a"] = _reduce_start({"0a": halves["0a"]}, ci, red0b[0], "0a")

    small2, shapes2 = _pack_flat([jnp.stack(dmods)] + [grads[l][n] for l in range(2) for n, _ in _SMALL], 16)
    got2 = _allgather8(small2, "gather_small").reshape(8, -1, 128)
    tot = _unpack_flat(_sum_lead(got2, "sum_small"), shapes2)
    g = {"ada_b": tot[0].reshape(2, 6 * D)}
    for i, (n, _) in enumerate(_SMALL):
        g[n] = jnp.stack([tot[1 + i], tot[1 + len(_SMALL) + i]])
    for n, w in _CONV_SHARDED.items():
        g[n] = lax.dynamic_slice_in_dim(g[n], chip * w, w, axis=2)
    nd = 2 * 6 * D // 128
    dmod_all = jnp.transpose(got2[:, :nd].reshape(8, 2, 6 * D), (1, 0, 2))
    dmod_cols = lax.dynamic_slice_in_dim(jnp.pad(dmod_all, ((0, 0), (0, 8), (0, 0))), chip * ncol, ncol, axis=2)
    g["ada_w"] = _ada_bwd(c16, dmod_cols)

    delta, new_m, new_v = {}, {}, {}
    token = scatters["0a"][-1]
    for n, r0 in zip(_LATE, red0b):
        g[n], delta[n], new_m[n], new_v[n] = _adamw_layers(a[n], r0, red1[n], a["m_" + n], a["v_" + n], token, f"adamw_{n}")
    shp = ada_w.shape
    r2 = lambda t: t.reshape(-1, shp[-1])
    delta["ada_w"], new_m["ada_w"], new_v["ada_w"] = (
        o.reshape(shp) for o in _adamw(r2(ada_w), r2(g["ada_w"]), r2(m_ada_w), r2(v_ada_w), "adamw_ada_w"))
    behind = (delta[_LATE[-1]][0, 0, :1] + delta["ada_w"][0, 0, :1]).reshape(1)
    red0a = _reduce_finish(scatters["0a"], behind, "0a")
    for n, r0 in zip(_EARLY, red0a):
        if n == "w_in":
            g[n], delta[n], new_m[n], new_v[n] = _adamw_cols(a[n], r0, red1[n], a["m_" + n], a["v_" + n], f"adamw_{n}")
        else:
            g[n], delta[n], new_m[n], new_v[n] = _adamw_layers(a[n], r0, red1[n], a["m_" + n], a["v_" + n], token, f"adamw_{n}")
    rest = [n for n in _W_NAMES if n not in big and n != "ada_w"]
    packs = [_pack_flat([t[n] if pre is None else t[pre + n] for n in rest], 128)[0]
             for t, pre in ((a, None), (g, None), (a, "m_"), (a, "v_"))]
    rest_shapes = [a[n].shape for n in rest]
    outs = [_unpack_flat(o, rest_shapes) for o in _adamw(*packs, "adamw_rest")]
    for i, n in enumerate(rest):
        delta[n], new_m[n], new_v[n] = outs[0][i], outs[1][i], outs[2][i]

    return (loss, grad_x[None], *[g[n] for n in _W_NAMES], *[delta[n] for n in _W_NAMES],
            *[new_m[n] for n in _W_NAMES], *[new_v[n] for n in _W_NAMES])
```

```python
import functools

import jax
import jax.numpy as jnp
from jax import lax
from jax.experimental import pallas as pl
from jax.experimental.pallas import tpu as pltpu

F32 = jnp.float32
BF16 = jnp.bfloat16
MESH = pl.DeviceIdType.MESH
HI = lax.Precision.HIGHEST

D = 1024
N_HEADS = 8
NOPE, ROPE_DIM = 64, 32
Q_RANK, KV_RANK = 384, 256
POOL_WINDOWS = (2, 4, 8, 16)
SSD_HEADS, SSD_P, SSD_N, SSD_L = 16, 64, 128, 128
SSD_INNER = 1024
SSD_CONV_DIM = 1536
FFN = 2816
EPS = 1e-6
ROPE_THETA = 10000.0
OG, OZ, OX, OP, OQ, OKV, ODT, IN_PAD = 0, 3072, 4096, 5632, 6144, 6528, 6912, 7040
IN_DIM = 6832
ADAM_LR, ADAM_B1, ADAM_B2, ADAM_EPS, ADAM_WD, ADAM_STEP = 0.001, 0.9, 0.999, 1e-08, 0.01, 10

_ARB = pltpu.CompilerParams(dimension_semantics=("arbitrary",))
_PAR = pltpu.CompilerParams(dimension_semantics=("parallel",))


def _pick(n, pref):
    if n <= pref:
        return n
    best = None
    for t in range(128, pref + 1, 128):
        if n % t == 0:
            best = t
    assert best is not None, (n, pref)
    return best


def _row_tile(r, cap=256):
    best = None
    for t in range(16, min(r, cap) + 1, 16):
        if r % t == 0:
            best = t
    assert best is not None, r
    return best


def _sds(shape, dtype=F32):
    return jax.ShapeDtypeStruct(tuple(shape), dtype)


def _iota(shape, dim):
    return lax.broadcasted_iota(jnp.int32, shape, dim)


def _sigmoid(x):
    return 0.5 * jnp.tanh(0.5 * x) + 0.5


def _silu(x):
    return x * _sigmoid(x)


def _dsilu(x):
    s = _sigmoid(x)
    return s * (1.0 + x * (1.0 - s))


def _dot(a, b, dims):
    return lax.dot_general(a, b, (dims, ((), ())), preferred_element_type=F32)


_NN, _NT, _TN = ((1,), (0,)), ((1,), (1,)), ((0,), (0,))


def _dot_hi(a, b, dims=_NN):
    return lax.dot_general(a, b, (dims, ((), ())), preferred_element_type=F32, precision=HI)


def _shift_down(x, j):
    n = x.shape[0]
    return jnp.where(_iota(x.shape, 0) >= j, pltpu.roll(x, j, 0), 0.0)


def _shift_up(x, j):
    n = x.shape[0]
    return jnp.where(_iota(x.shape, 0) < n - j, pltpu.roll(x, n - j, 0), 0.0)


def _mm(a, b, mode="nn", out_dtype=F32, tm=2048, tn=512, tk=4096, res=None, gate=None, name="mm"):
    if mode == "nn":
        (M, K), (K2, N) = a.shape, b.shape
    elif mode == "nt":
        (M, K), (N, K2) = a.shape, b.shape
    else:
        (K, M), (K2, N) = a.shape, b.shape
    assert K == K2, (a.shape, b.shape, mode)
    tm, tn, tk = _pick(M, tm), _pick(N, tn), _pick(K, tk)
    nk = K // tk
    dims = {"nn": _NN, "nt": _NT, "tn": _TN}[mode]
    fused = res is not None

    def body(*refs):
        a_ref, b_ref = refs[:2]

        def finish(acc):
            if fused:
                r_ref, g_ref, o_ref, raw_ref = refs[2:6]
                raw_ref[...] = acc
                o_ref[...] = r_ref[...] + g_ref[...] * acc
            else:
                refs[2][...] = acc.astype(out_dtype)

        _mm_steps(a_ref, b_ref, dims, nk, refs[-1] if nk > 1 else None, finish)

    if mode == "nn":
        a_spec = pl.BlockSpec((tm, tk), lambda i, j, k: (i, k))
        b_spec = pl.BlockSpec((tk, tn), lambda i, j, k: (k, j))
    elif mode == "nt":
        a_spec = pl.BlockSpec((tm, tk), lambda i, j, k: (i, k))
        b_spec = pl.BlockSpec((tn, tk), lambda i, j, k: (j, k))
    else:
        a_spec = pl.BlockSpec((tk, tm), lambda i, j, k: (k, i))
        b_spec = pl.BlockSpec((tk, tn), lambda i, j, k: (k, j))
    o_spec = pl.BlockSpec((tm, tn), lambda i, j, k: (i, j))
    in_specs, args = [a_spec, b_spec], [a, b]
    out_shape, out_specs = _sds((M, N), out_dtype), o_spec
    if fused:
        in_specs += [o_spec, pl.BlockSpec((1, tn), lambda i, j, k: (0, j))]
        args += [res, gate]
        out_shape, out_specs = (_sds((M, N)), _sds((M, N))), (o_spec, o_spec)
    return pl.pallas_call(
        body, grid=(M // tm, N // tn, nk), in_specs=in_specs, out_specs=out_specs, out_shape=out_shape,
        scratch_shapes=[pltpu.VMEM((tm, tn), F32)] if nk > 1 else [], name=name,
        compiler_params=pltpu.CompilerParams(dimension_semantics=("parallel", "parallel", "arbitrary")),
    )(*args)


def _mm_steps(a_ref, b_ref, dims, nk, acc_ref, finish):
    part = _dot(a_ref[...].astype(BF16), b_ref[...].astype(BF16), dims)
    if nk == 1:
        finish(part)
        return
    k = pl.program_id(2)

    @pl.when(k == 0)
    def _():
        acc_ref[...] = part

    @pl.when(k > 0)
    def _():
        acc_ref[...] += part

    @pl.when(k == nk - 1)
    def _():
        finish(acc_ref[...])


def _mm_blocks(a, b, dims, grid, a_spec, b_spec, o_spec, out_shape, acc_shape, name):
    nk = grid[2]

    def body(a_ref, b_ref, o_ref, *scratch):
        def finish(acc):
            o_ref[...] = acc.astype(o_ref.dtype)

        _mm_steps(a_ref, b_ref, dims, nk, scratch[0] if nk > 1 else None, finish)

    return pl.pallas_call(
        body, grid=grid, in_specs=[a_spec, b_spec], out_specs=o_spec, out_shape=out_shape,
        scratch_shapes=[pltpu.VMEM(acc_shape, F32)] if nk > 1 else [], name=name,
        compiler_params=pltpu.CompilerParams(dimension_semantics=("parallel", "parallel", "arbitrary")),
    )(a, b)


_UP_SHARD = 2 * FFN // 4


def _up_fwd(h2, wup4, name, tm=2048):
    S = h2.shape[0]
    tm = min(tm, S)
    return _mm_blocks(h2, wup4, _NN, (S // tm, 4, 1), pl.BlockSpec((tm, D), lambda i, j, k: (i, 0)),
                      pl.BlockSpec((None, D, _UP_SHARD), lambda i, j, k: (j, 0, 0)), pl.BlockSpec((tm, _UP_SHARD), lambda i, j, k: (i, j)),
                      _sds((S, 2 * FFN)), (tm, _UP_SHARD), name)


def _up_dx(dup2, wup4, name, tm=2048, tn=512):
    S = dup2.shape[1]
    tm = min(tm, S)
    return _mm_blocks(dup2, wup4, _NT, (S // tm, D // tn, 4), pl.BlockSpec((None, tm, _UP_SHARD), lambda i, j, k: (lax.div(k, 2), i, lax.rem(k, 2))),
                      pl.BlockSpec((None, tn, _UP_SHARD), lambda i, j, k: (k, j, 0)), pl.BlockSpec((tm, tn), lambda i, j, k: (i, j)),
                      _sds((S, D)), (tm, tn), name)


def _up_dw(h2, dup2, name, tm=1024, tk=4096):
    S = h2.shape[0]
    tk = min(tk, S)
    return _mm_blocks(h2, dup2, _TN, (D // tm, 4, S // tk), pl.BlockSpec((tk, tm), lambda i, j, k: (k, i)),
                      pl.BlockSpec((None, tk, _UP_SHARD), lambda i, j, k: (lax.div(j, 2), k, lax.rem(j, 2))),
                      pl.BlockSpec((None, tm, _UP_SHARD), lambda i, j, k: (j, i, 0)), _sds((4, D, _UP_SHARD)), (tm, _UP_SHARD), name)


def _row_spec(tm, n):
    return pl.BlockSpec((tm, n), lambda i: (i, 0))


def _vec_spec(n, rows=1):
    return pl.BlockSpec((rows, n), lambda i: (0, 0))


def _ln_mod(x, nw, sc, sh, name, tm=256):
    S = x.shape[0]

    def body(x_ref, nw_ref, sc_ref, sh_ref, o_ref):
        xv = x_ref[...]
        r = lax.rsqrt(jnp.mean(xv * xv, -1, keepdims=True) + EPS)
        o_ref[...] = ((xv * r * nw_ref[...]) * (1.0 + sc_ref[...]) + sh_ref[...]).astype(BF16)

    return pl.pallas_call(
        body, grid=(S // tm,), in_specs=[_row_spec(tm, D)] + [_vec_spec(D)] * 3, out_specs=_row_spec(tm, D),
        out_shape=_sds((S, D), BF16), name=name, compiler_params=_PAR)(x, nw, sc, sh)


def _ln_mod_bwd(x, dh, dres, nw, sc, name, tm=256):
    S = x.shape[0]

    def body(x_ref, dh_ref, dres_ref, nw_ref, sc_ref, dx_ref, st_ref):
        @pl.when(pl.program_id(0) == 0)
        def _():
            st_ref[...] = jnp.zeros_like(st_ref)

        xv, dhv, nwv = x_ref[...], dh_ref[...], nw_ref[...]
        r = lax.rsqrt(jnp.mean(xv * xv, -1, keepdims=True) + EPS)
        xhat = xv * r
        dn = dhv * (1.0 + sc_ref[...])
        g = dn * nwv
        dx_ref[...] = dres_ref[...] + r * (g - xhat * jnp.mean(g * xhat, -1, keepdims=True))
        st_ref[0:1, :] += jnp.sum(dhv, 0, keepdims=True)
        st_ref[1:2, :] += jnp.sum(dhv * (xhat * nwv), 0, keepdims=True)
        st_ref[2:3, :] += jnp.sum(dn * xhat, 0, keepdims=True)

    return pl.pallas_call(
        body, grid=(S // tm,), in_specs=[_row_spec(tm, D)] * 3 + [_vec_spec(D)] * 2,
        out_specs=(_row_spec(tm, D), _vec_spec(D, 8)), out_shape=(_sds((S, D)), _sds((8, D))),
        name=name, compiler_params=_ARB)(x, dh, dres, nw, sc)


def _gate_bwd(dx, out, g, name, tm=256):
    S = dx.shape[0]

    def body(dx_ref, o_ref, g_ref, dz_ref, dg_ref):
        @pl.when(pl.program_id(0) == 0)
        def _():
            dg_ref[...] = jnp.zeros_like(dg_ref)

        dxv = dx_ref[...]
        dz_ref[...] = (dxv * g_ref[...]).astype(BF16)
        dg_ref[0:1, :] += jnp.sum(dxv * o_ref[...], 0, keepdims=True)

    return pl.pallas_call(
        body, grid=(S // tm,), in_specs=[_row_spec(tm, D)] * 2 + [_vec_spec(D)],
        out_specs=(_row_spec(tm, D), _vec_spec(D, 8)), out_shape=(_sds((S, D), BF16), _sds((8, D))),
        name=name, compiler_params=_ARB)(dx, out, g)


def _loss_grad(y, t, tm=256):
    S = y.shape[0]

    def body(y_ref, t_ref, dy_ref, l_ref):
        @pl.when(pl.program_id(0) == 0)
        def _():
            l_ref[...] = jnp.zeros_like(l_ref)

        e = y_ref[...] - t_ref[...]
        dy_ref[...] = e * (1.0 / D)
        l_ref[...] += 0.5 * jnp.sum(jnp.mean(e * e, -1, keepdims=True), 0, keepdims=True)

    return pl.pallas_call(
        body, grid=(S // tm,), in_specs=[_row_spec(tm, D)] * 2,
        out_specs=(_row_spec(tm, D), pl.BlockSpec((8, 128), lambda i: (0, 0))),
        out_shape=(_sds((S, D)), _sds((8, 128))), name="loss_grad", compiler_params=_ARB)(y, t)


_CONV_ROWS = 256
_HALO = 8


def _rows(ref, lo, hi):
    S, c = ref.shape
    parts = [jnp.zeros((-lo, c), F32)] if lo < 0 else []
    parts.append(ref[max(lo, 0):min(hi, S), :])
    if hi > S:
        parts.append(jnp.zeros((hi - S, c), F32))
    return parts[0] if len(parts) == 1 else jnp.concatenate(parts, 0)


def _conv_rows(ext, w, b, n):
    K = w.shape[0]
    acc = ext[_HALO:_HALO + n] * w[K - 1:K, :] + b
    for j in range(1, K):
        acc = acc + pltpu.roll(ext, j, 0)[_HALO:_HALO + n] * w[K - 1 - j:K - j, :]
    return acc


def _conv_rows_bwd(ext, w, dc, n):
    K, m = w.shape[0], dc.shape[0]
    d0 = dc[0:n]
    dx = d0 * w[K - 1:K, :]
    dws = [jnp.sum(d0 * ext[_HALO:_HALO + n], 0, keepdims=True)]
    for j in range(1, K):
        dx = dx + pltpu.roll(dc, m - j, 0)[0:n] * w[K - 1 - j:K - j, :]
        dws.append(jnp.sum(d0 * pltpu.roll(ext, j, 0)[_HALO:_HALO + n], 0, keepdims=True))
    return dx, dws[::-1], jnp.sum(d0, 0, keepdims=True)


def _col_spec(S, tc, off=0):
    return pl.BlockSpec((S, tc), lambda j: (0, j + off))


def _ssd_pre(proj, cw, cb, tc=256):
    S, n = proj.shape[0], SSD_CONV_DIM
    R = min(_CONV_ROWS, S)

    def body(x_ref, w_ref, b_ref, o_ref):
        wv, bv = w_ref[...], b_ref[...]
        for r0 in range(0, S, R):
            o_ref[r0:r0 + R, :] = _silu(_conv_rows(_rows(x_ref, r0 - _HALO, r0 + R), wv, bv, R))

    return pl.pallas_call(
        body, grid=(n // tc,),
        in_specs=[_col_spec(S, tc, OX // tc), pl.BlockSpec((4, tc), lambda j: (0, j)), pl.BlockSpec((1, tc), lambda j: (0, j))],
        out_specs=_col_spec(S, tc), out_shape=_sds((S, n)), name="ssd_pre", compiler_params=_PAR)(proj, cw, cb)


def _ssd_pre_bwd(proj, dxc, cw, cb, tc=256):
    S, n = proj.shape[0], SSD_CONV_DIM
    R = min(_CONV_ROWS, S)

    def body(x_ref, d_ref, w_ref, b_ref, dx_ref, dw_ref, db_ref):
        wv, bv = w_ref[...], b_ref[...]
        acc = [jnp.zeros((1, tc), F32)] * 5
        for r0 in range(0, S, R):
            ext = _rows(x_ref, r0 - _HALO, r0 + R + _HALO)
            dc = _rows(d_ref, r0, r0 + R + _HALO) * _dsilu(_conv_rows(ext, wv, bv, R + _HALO))
            dx, dws, db = _conv_rows_bwd(ext, wv, dc, R)
            dx_ref[r0:r0 + R, :] = dx.astype(BF16)
            acc = [s + d for s, d in zip(acc, dws + [db])]
        for k in range(4):
            dw_ref[k:k + 1, :] = acc[k]
        db_ref[...] = acc[4]

    wspec, bspec = pl.BlockSpec((4, tc), lambda j: (0, j)), pl.BlockSpec((1, tc), lambda j: (0, j))
    return pl.pallas_call(
        body, grid=(n // tc,), in_specs=[_col_spec(S, tc, OX // tc), _col_spec(S, tc), wspec, bspec],
        out_specs=(_col_spec(S, tc), wspec, bspec), out_shape=(_sds((S, n), BF16), _sds((4, n)), _sds((1, n))),
        name="ssd_pre_bwd", compiler_params=_PAR)(proj, dxc, cw, cb)


def _ffn_act(up, cw, cb, tc=256):
    S, nb = up.shape[0], FFN // tc
    R = min(_CONV_ROWS, S)

    def body(g_ref, v_ref, wg_ref, wv_ref, bg_ref, bv_ref, o_ref):
        wg, wv, bg, bv = wg_ref[...], wv_ref[...], bg_ref[...], bv_ref[...]
        for r0 in range(0, S, R):
            cg = _conv_rows(_rows(g_ref, r0 - _HALO, r0 + R), wg, bg, R)
            cv = _conv_rows(_rows(v_ref, r0 - _HALO, r0 + R), wv, bv, R)
            o_ref[r0:r0 + R, :] = (_silu(cg) * cv).astype(BF16)

    def wspec(off):
        return pl.BlockSpec((3, tc), lambda j: (0, j + off))

    def bspec(off):
        return pl.BlockSpec((1, tc), lambda j: (0, j + off))

    return pl.pallas_call(
        body, grid=(nb,), in_specs=[_col_spec(S, tc), _col_spec(S, tc, nb), wspec(0), wspec(nb), bspec(0), bspec(nb)],
        out_specs=_col_spec(S, tc), out_shape=_sds((S, FFN), BF16), name="ffn_act", compiler_params=_PAR)(up, up, cw, cw, cb, cb)


def _ffn_act_bwd(up, dact, cw, cb, tc=256):
    S, nb = up.shape[0], FFN // tc
    R = min(_CONV_ROWS, S)

    def body(g_ref, v_ref, d_ref, wg_ref, wv_ref, bg_ref, bv_ref, dx_ref, dw_ref, db_ref):
        wg, wv, bg, bv = wg_ref[...], wv_ref[...], bg_ref[...], bv_ref[...]
        acc = [[jnp.zeros((1, tc), F32)] * 4, [jnp.zeros((1, tc), F32)] * 4]
        for r0 in range(0, S, R):
            eg, ev = _rows(g_ref, r0 - _HALO, r0 + R + _HALO), _rows(v_ref, r0 - _HALO, r0 + R + _HALO)
            da = _rows(d_ref, r0, r0 + R + _HALO)
            cg, cv = _conv_rows(eg, wg, bg, R + _HALO), _conv_rows(ev, wv, bv, R + _HALO)
            sg = _sigmoid(cg)
            for half, (ext, w, dc) in enumerate(((eg, wg, da * cv * (sg * (1.0 + cg * (1.0 - sg)))), (ev, wv, da * (cg * sg)))):
                dx, dws, db = _conv_rows_bwd(ext, w, dc, R)
                dx_ref[half, r0:r0 + R, :] = dx.astype(BF16)
                acc[half] = [s + d for s, d in zip(acc[half], dws + [db])]
        for half in range(2):
            for k in range(3):
                dw_ref[half, k:k + 1, :] = acc[half][k]
            db_ref[half] = acc[half][3]

    def wspec(off):
        return pl.BlockSpec((3, tc), lambda j: (0, j + off))

    def bspec(off):
        return pl.BlockSpec((1, tc), lambda j: (0, j + off))

    cs = _col_spec(S, tc)
    both = lambda r: pl.BlockSpec((2, r, tc), lambda j: (0, 0, j))
    return pl.pallas_call(
        body, grid=(nb,), in_specs=[cs, _col_spec(S, tc, nb), cs, wspec(0), wspec(nb), bspec(0), bspec(nb)],
        out_specs=(both(S), both(3), both(1)), out_shape=(_sds((2, S, FFN), BF16), _sds((2, 3, FFN)), _sds((2, 1, FFN))),
        name="ffn_act_bwd", compiler_params=_PAR)(up, up, dact, cw, cw, cb, cb)


def _window_sum(x, w, up=False):
    shift = _shift_up if up else _shift_down
    j = 1
    while j < w:
        x = x + shift(x, j)
        j *= 2
    return x


def _pool_fwd(proj, pool_w, pool_scale):
    S = proj.shape[0]

    def body(u_ref, w_ref, s_ref, o_ref):
        cnt_row = (_iota((S, 128), 0) + 1).astype(F32)
        for g, w in enumerate(POOL_WINDOWS):
            sl = slice(g * 128, (g + 1) * 128)
            u = u_ref[:, sl]
            pooled = _window_sum(u, w) / jnp.minimum(cnt_row, float(w)) - u
            mixed = _dot(pooled.astype(BF16), w_ref[g].astype(BF16), _NN)
            o_ref[:, sl] = (mixed * s_ref[:, sl]).astype(BF16)

    return pl.pallas_call(
        body, grid=(1,),
        in_specs=[pl.BlockSpec((S, 512), lambda i: (0, OP // 512)), pl.BlockSpec((4, 128, 128), lambda i: (0, 0, 0)), _vec_spec(512)],
        out_specs=pl.BlockSpec((S, 512), lambda i: (0, 0)), out_shape=_sds((S, 512), BF16), name="pool_fwd",
        compiler_params=_ARB)(proj, pool_w, pool_scale)


def _pool_bwd(proj, dob, pool_w, pool_scale):
    S = proj.shape[0]

    def body(u_ref, d_ref, w_ref, s_ref, du_ref, dw_ref, ds_ref):
        cnt_row = (_iota((S, 128), 0) + 1).astype(F32)
        for g, w in enumerate(POOL_WINDOWS):
            sl = slice(g * 128, (g + 1) * 128)
            u, dv, wv = u_ref[:, sl], d_ref[:, sl], w_ref[g].astype(BF16)
            cnt = jnp.minimum(cnt_row, float(w))
            pooled = (_window_sum(u, w) / cnt - u).astype(BF16)
            ds_ref[:, sl] = jnp.sum(dv * _dot(pooled, wv, _NN), 0, keepdims=True)
            dmix = (dv * s_ref[:, sl]).astype(BF16)
            dw_ref[g] = _dot(pooled, dmix, _TN)
            dp = _dot(dmix, wv, _NT)
            du_ref[:, sl] = (_window_sum(dp / cnt, w, up=True) - dp).astype(BF16)

    blk = pl.BlockSpec((S, 512), lambda i: (0, 0))
    wspec = pl.BlockSpec((4, 128, 128), lambda i: (0, 0, 0))
    return pl.pallas_call(
        body, grid=(1,), in_specs=[pl.BlockSpec((S, 512), lambda i: (0, OP // 512)), blk, wspec, _vec_spec(512)],
        out_specs=(blk, wspec, _vec_spec(512)), out_shape=(_sds((S, 512), BF16), _sds((4, 128, 128)), _sds((1, 512))),
        name="pool_bwd", compiler_params=_ARB)(proj, dob, pool_w, pool_scale)


def _branch_specs():
    return [pl.BlockSpec((512, D), lambda i: (0, 0)), pl.BlockSpec((512, D), lambda i: (1, 0)), pl.BlockSpec((1024, D), lambda i: (1, 0))]


def _merge_fwd(oa, ob, oc, proj, wbr, tm=256):
    S = oa.shape[0]

    def body(oa_ref, ob_ref, oc_ref, gl_ref, wa_ref, wb_ref, wc_ref, o_ref):
        acc = _sigmoid(gl_ref[:, 0:D]) * _dot(oa_ref[...], wa_ref[...], _NN)
        acc += _sigmoid(gl_ref[:, D:2 * D]) * _dot(ob_ref[...], wb_ref[...], _NN)
        acc += _sigmoid(gl_ref[:, 2 * D:3 * D]) * _dot(oc_ref[...], wc_ref[...], _NN)
        o_ref[...] = acc.astype(BF16)

    return pl.pallas_call(
        body, grid=(S // tm,),
        in_specs=[_row_spec(tm, 512), _row_spec(tm, 512), _row_spec(tm, D), _row_spec(tm, 3 * D)] + _branch_specs(),
        out_specs=_row_spec(tm, D), out_shape=_sds((S, D), BF16), name="merge_fwd", compiler_params=_PAR)(oa, ob, oc, proj, wbr, wbr, wbr)


def _merge_bwd(dm, oa, ob, oc, proj, wbr, tm=256):
    S = oa.shape[0]

    def body(dm_ref, oa_ref, ob_ref, oc_ref, gl_ref, wa_ref, wb_ref, wc_ref, dya_ref, dyb_ref, dyc_ref, dgl_ref, doa_ref, dob_ref, doc_ref):
        dmv = dm_ref[...]
        for i, (o_ref, w_ref, dy_ref, do_ref) in enumerate(
                ((oa_ref, wa_ref, dya_ref, doa_ref), (ob_ref, wb_ref, dyb_ref, dob_ref), (oc_ref, wc_ref, dyc_ref, doc_ref))):
            gt = _sigmoid(gl_ref[:, i * D:(i + 1) * D])
            wv = w_ref[...]
            yv = _dot(o_ref[...], wv, _NN)
            dy = (dmv * gt).astype(BF16)
            dy_ref[...] = dy
            dgl_ref[:, i * D:(i + 1) * D] = (dmv * yv * gt * (1.0 - gt)).astype(BF16)
            do_ref[...] = _dot(dy, wv, _NT)

    rs = _row_spec
    return pl.pallas_call(
        body, grid=(S // tm,),
        in_specs=[rs(tm, D), rs(tm, 512), rs(tm, 512), rs(tm, D), rs(tm, 3 * D)] + _branch_specs(),
        out_specs=(rs(tm, D), rs(tm, D), rs(tm, D), rs(tm, 3 * D), rs(tm, 512), rs(tm, 512), rs(tm, D)),
        out_shape=(_sds((S, D), BF16), _sds((S, D), BF16), _sds((S, D), BF16), _sds((S, 3 * D), BF16), _sds((S, 512)), _sds((S, 512)), _sds((S, D))),
        name="merge_bwd", compiler_params=_PAR)(dm, oa, ob, oc, proj, wbr, wbr, wbr)


def _rope_tab(posb, invf, tm=256):
    S = posb.shape[0]

    def body(p_ref, f_ref, c_ref, s1_ref, s2_ref):
        ang = p_ref[...] * f_ref[...]
        lane = _iota(ang.shape, 1)
        cs, sn = jnp.cos(ang), jnp.sin(ang)
        c_ref[...] = jnp.where(lane < NOPE, 1.0, cs)
        s1_ref[...] = jnp.where((lane >= 64) & (lane < 80), -sn, 0.0)
        s2_ref[...] = jnp.where((lane >= 80) & (lane < 96), sn, 0.0)

    rs = _row_spec(tm, 128)
    return pl.pallas_call(body, grid=(S // tm,), in_specs=[rs, _vec_spec(128)], out_specs=(rs, rs, rs),
                          out_shape=(_sds((S, 128)),) * 3, name="rope_tab", compiler_params=_PAR)(posb, invf)


def _rope(u, C, S1, S2):
    return u * C + pltpu.roll(u, 112, 1) * S1 + pltpu.roll(u, 16, 1) * S2


def _rope_t(dy, C, S1, S2):
    return dy * C + pltpu.roll(dy * S1, 16, 1) + pltpu.roll(dy * S2, 112, 1)


def _seg_sum(v, mask):
    return jnp.sum(jnp.where(mask, v, 0.0), -1, keepdims=True)


def _mla_latents(pq_ref, pkv_ref, wqb_ref, wkvb_ref, qan_ref, kvan_ref):
    ql, kvl = pq_ref[...], pkv_ref[...]
    ckv, kr = kvl[:, 0:KV_RANK], kvl[:, KV_RANK:KV_RANK + 128]
    rq = lax.rsqrt(jnp.mean(ql * ql, -1, keepdims=True) + EPS)
    rkv = lax.rsqrt(jnp.mean(ckv * ckv, -1, keepdims=True) + EPS)
    nq = (ql * rq * qan_ref[...]).astype(BF16)
    nkv = (ckv * rkv * kvan_ref[...]).astype(BF16)
    kv = jnp.concatenate([_dot(nkv, wkvb_ref[k], _NN) for k in range(4)], 1)
    return ql, ckv, kr, rq, rkv, nq, nkv, _dot(nq, wqb_ref[...], _NN), kv


def _mla_specs(tm):
    full = lambda r, n: pl.BlockSpec((r, n), lambda i: (0, 0))
    return ([pl.BlockSpec((tm, 384), lambda i: (i, OQ // 384)), pl.BlockSpec((tm, 384), lambda i: (i, OKV // 384))],
            [full(Q_RANK, D), pl.BlockSpec((4, KV_RANK, 256), lambda i: (0, 0, 0)), _vec_spec(Q_RANK), _vec_spec(KV_RANK), _vec_spec(128), _vec_spec(128)]
            + [_row_spec(tm, 128)] * 3)


def _mla_prep(proj, wqb, wkvb, qan, kvan, wq, wk, rope, tm=256):
    S = proj.shape[0]

    def body(pq_ref, pkv_ref, wqb_ref, wkvb_ref, qan_ref, kvan_ref, wq_ref, wk_ref, c_ref, s1_ref, s2_ref, qf_ref, kf_ref, ve_ref):
        _, _, kr, _, _, _, _, q, kv = _mla_latents(pq_ref, pkv_ref, wqb_ref, wkvb_ref, qan_ref, kvan_ref)
        C, S1, S2, wqv, wkv = c_ref[...], s1_ref[...], s2_ref[...], wq_ref[...], wk_ref[...]
        lane = _iota((tm, 128), 1)
        mn, mr = lane < 64, (lane >= 64) & (lane < 96)
        rrk = lax.rsqrt(_seg_sum(kr * kr, mr) / ROPE_DIM + EPS)
        ykr = _rope(jnp.where(mr, kr * rrk * wkv, 0.0), C, S1, S2)
        for h in range(N_HEADS):
            sl = slice(h * 128, (h + 1) * 128)
            t = q[:, sl]
            rn = lax.rsqrt(_seg_sum(t * t, mn) / NOPE + EPS)
            rr = lax.rsqrt(_seg_sum(t * t, mr) / ROPE_DIM + EPS)
            qf_ref[:, sl] = _rope(t * jnp.where(mn, rn, jnp.where(mr, rr, 0.0)) * wqv, C, S1, S2).astype(BF16)
            t = kv[:, sl]
            rn = lax.rsqrt(_seg_sum(t * t, mn) / NOPE + EPS)
            kf_ref[:, sl] = (jnp.where(mn, t * rn * wkv, 0.0) + ykr).astype(BF16)
            ve_ref[:, sl] = (jnp.where(mn, pltpu.roll(t, 64, 1), 0.0) if h % 2 == 0 else jnp.where(mn, 0.0, t)).astype(BF16)

    pspecs, wspecs = _mla_specs(tm)
    rs = _row_spec(tm, D)
    return pl.pallas_call(body, grid=(S // tm,), in_specs=pspecs + wspecs, out_specs=(rs, rs, rs),
                          out_shape=(_sds((S, D), BF16),) * 3, name="mla_prep", compiler_params=_PAR)(
        proj, proj, wqb, wkvb, qan, kvan, wq, wk, *rope)


def _mla_prep_bwd(proj, dqf, dkf, dve, wqb, wkvb, qan, kvan, wq, wk, rope, tm=256):
    S = proj.shape[0]

    def body(pq_ref, pkv_ref, wqb_ref, wkvb_ref, qan_ref, kvan_ref, wq_ref, wk_ref, c_ref, s1_ref, s2_ref,
             dqf_ref, dkf_ref, dve_ref, dqs_ref, dkvs_ref, dwqb_ref, dwkvb_ref, st_ref, dq_scr, dkv_scr):
        @pl.when(pl.program_id(0) == 0)
        def _():
            dwqb_ref[...] = jnp.zeros_like(dwqb_ref)
            dwkvb_ref[...] = jnp.zeros_like(dwkvb_ref)
            st_ref[...] = jnp.zeros_like(st_ref)

        ql, ckv, kr, rq, rkv, nq, nkv, q, kv = _mla_latents(pq_ref, pkv_ref, wqb_ref, wkvb_ref, qan_ref, kvan_ref)
        C, S1, S2, wqv, wkv = c_ref[...], s1_ref[...], s2_ref[...], wq_ref[...], wk_ref[...]
        lane = _iota((tm, 128), 1)
        mn, mr = lane < 64, (lane >= 64) & (lane < 96)
        dwq = jnp.zeros((1, 128), F32)
        dwk = jnp.zeros((1, 128), F32)
        dykr = jnp.zeros((tm, 128), F32)
        for h in range(N_HEADS):
            sl = slice(h * 128, (h + 1) * 128)
            t = q[:, sl]
            rn = lax.rsqrt(_seg_sum(t * t, mn) / NOPE + EPS)
            rr = lax.rsqrt(_seg_sum(t * t, mr) / ROPE_DIM + EPS)
            scale = jnp.where(mn, rn, jnp.where(mr, rr, 0.0))
            that = t * scale
            du = _rope_t(dqf_ref[:, sl], C, S1, S2)
            dwq += jnp.sum(du * that, 0, keepdims=True)
            g = du * wqv
            gt = g * that
            dq_scr[:, sl] = scale * (g - that * jnp.where(mn, _seg_sum(gt, mn) / NOPE, _seg_sum(gt, mr) / ROPE_DIM))
            t = kv[:, sl]
            rn = lax.rsqrt(_seg_sum(t * t, mn) / NOPE + EPS)
            that = jnp.where(mn, t * rn, 0.0)
            dkf = dkf_ref[:, sl]
            dykr += jnp.where(mr, dkf, 0.0)
            dkn = jnp.where(mn, dkf, 0.0)
            dwk += jnp.sum(dkn * that, 0, keepdims=True)
            g = dkn * wkv
            dve = dve_ref[:, sl]
            dkv_scr[:, sl] = jnp.where(mn, rn * (g - that * (jnp.sum(g * that, -1, keepdims=True) / NOPE)),
                                       pltpu.roll(dve, 64, 1) if h % 2 == 0 else dve)
        rrk = lax.rsqrt(_seg_sum(kr * kr, mr) / ROPE_DIM + EPS)
        that = jnp.where(mr, kr * rrk, 0.0)
        dukr = jnp.where(mr, _rope_t(dykr, C, S1, S2), 0.0)
        dwk += jnp.sum(dukr * that, 0, keepdims=True)
        g = dukr * wkv
        dkr = rrk * (g - that * (jnp.sum(g * that, -1, keepdims=True) / ROPE_DIM))
        dqv, dkvv = dq_scr[...].astype(BF16), dkv_scr[...].astype(BF16)
        dnq = _dot(dqv, wqb_ref[...], _NT)
        dwqb_ref[...] += _dot(nq, dqv, _TN)
        dnkv = jnp.zeros((tm, KV_RANK), F32)
        for k in range(4):
            dnkv += _dot(dkvv[:, k * 256:(k + 1) * 256], wkvb_ref[k], _NT)
            dwkvb_ref[k] += _dot(nkv, dkvv[:, k * 256:(k + 1) * 256], _TN)
        xhat = ql * rq
        st_ref[0:1, 0:Q_RANK] += jnp.sum(dnq * xhat, 0, keepdims=True)
        g = dnq * qan_ref[...]
        dqs_ref[...] = (rq * (g - xhat * jnp.mean(g * xhat, -1, keepdims=True))).astype(BF16)
        xhat = ckv * rkv
        st_ref[1:2, 0:KV_RANK] += jnp.sum(dnkv * xhat, 0, keepdims=True)
        g = dnkv * kvan_ref[...]
        dkvs_ref[:, 0:KV_RANK] = (rkv * (g - xhat * jnp.mean(g * xhat, -1, keepdims=True))).astype(BF16)
        dkvs_ref[:, KV_RANK:KV_RANK + 128] = dkr.astype(BF16)
        st_ref[2:3, 0:128] += dwq
        st_ref[3:4, 0:128] += dwk

    pspecs, wspecs = _mla_specs(tm)
    rs = _row_spec(tm, D)
    full = lambda r, n: pl.BlockSpec((r, n), lambda i: (0, 0))
    return pl.pallas_call(
        body, grid=(S // tm,), in_specs=pspecs + wspecs + [rs, rs, rs],
        out_specs=(_row_spec(tm, 384), _row_spec(tm, 384), full(Q_RANK, D), pl.BlockSpec((4, KV_RANK, 256), lambda i: (0, 0, 0)), full(8, D)),
        out_shape=(_sds((S, 384), BF16), _sds((S, 384), BF16), _sds((Q_RANK, D)), _sds((4, KV_RANK, 256)), _sds((8, D))),
        scratch_shapes=[pltpu.VMEM((tm, D), F32), pltpu.VMEM((tm, D), F32)], name="mla_prep_bwd", compiler_params=_ARB)(
        proj, proj, wqb, wkvb, qan, kvan, wq, wk, *rope, dqf, dkf, dve)


_ATT_SCALE = (NOPE + ROPE_DIM) ** -0.5


def _att_probs(q, k, i, tq):
    n = k.shape[0]
    s = _dot(q, k, _NT) * _ATT_SCALE
    tri = _iota((tq, tq), 1) <= _iota((tq, tq), 0)
    diag = jnp.where(tri, s[:, n - tq:], -1e30)
    s = diag if n == tq else jnp.concatenate([s[:, :n - tq], diag], 1)
    p = jnp.exp(s - jnp.max(s, -1, keepdims=True))
    return p * (1.0 / jnp.sum(p, -1, keepdims=True))


def _attn_fwd(qf, kf, ve, tq=256):
    S = qf.shape[0]

    def body(q_ref, k_ref, v_ref, o_ref):
        for i in range(S // tq):
            n, rows = (i + 1) * tq, slice(i * tq, (i + 1) * tq)
            acc = jnp.zeros((tq, 128), F32)
            for hh in range(2):
                sl = slice(hh * 128, (hh + 1) * 128)
                p = _att_probs(q_ref[rows, sl], k_ref[0:n, sl], i, tq)
                acc += _dot(p.astype(BF16), v_ref[0:n, sl], _NN)
            o_ref[rows, :] = acc.astype(BF16)

    ps = pl.BlockSpec((S, 256), lambda h: (0, h))
    return pl.pallas_call(body, grid=(N_HEADS // 2,), in_specs=[ps, ps, ps], out_specs=pl.BlockSpec((S, 128), lambda h: (0, h)),
                          out_shape=_sds((S, 512), BF16), name="attn_fwd", compiler_params=_PAR)(qf, kf, ve)


def _attn_bwd(qf, kf, ve, do, tq=256):
    S = qf.shape[0]

    def body(q_ref, k_ref, v_ref, do_ref, dq_ref, dk_ref, dv_ref):
        dk_ref[...] = jnp.zeros_like(dk_ref)
        dv_ref[...] = jnp.zeros_like(dv_ref)
        for i in range(S // tq):
            n, rows = (i + 1) * tq, slice(i * tq, (i + 1) * tq)
            dob = do_ref[rows, :].astype(BF16)
            for hh in range(2):
                sl = slice(hh * 128, (hh + 1) * 128)
                q, k = q_ref[rows, sl], k_ref[0:n, sl]
                p = _att_probs(q, k, i, tq)
                dv_ref[0:n, sl] += _dot(p.astype(BF16), dob, _TN)
                dp = _dot(dob, v_ref[0:n, sl], _NT)
                ds = (p * (dp - jnp.sum(dp * p, -1, keepdims=True)) * _ATT_SCALE).astype(BF16)
                dq_ref[rows, sl] = _dot(ds, k, _NN)
                dk_ref[0:n, sl] += _dot(ds, q, _TN)

    ps = pl.BlockSpec((S, 256), lambda h: (0, h))
    return pl.pallas_call(body, grid=(N_HEADS // 2,), in_specs=[ps, ps, ps, pl.BlockSpec((S, 128), lambda h: (0, h))], out_specs=(ps, ps, ps),
                          out_shape=(_sds((S, D)),) * 3, name="attn_bwd", compiler_params=_PAR)(qf, kf, ve, do)


def _softplus(x):
    return jnp.maximum(x, 0.0) + jnp.log1p(jnp.exp(-jnp.abs(x)))


def _ssd_chunk(xc_ref, dtr_ref, dtb_ref, al_ref, e_ref):
    L = SSD_L
    a = -jnp.exp(al_ref[...])
    dtp = _softplus(dtr_ref[...] + dtb_ref[...])
    causal = _iota((L, L), 1) <= _iota((L, L), 0)
    cs = _dot_hi(causal.astype(F32), dtp * a)
    E = e_ref[...]
    dtx, csx = _dot_hi(dtp, E), _dot_hi(cs, E)
    X = xc_ref[:, 0:SSD_INNER]
    Xd = X * dtx
    dec_out = jnp.exp(csx)
    dec_st = jnp.exp(csx[L - 1:L, :] - csx)
    return a, dtp, causal, cs, cs.T, dtx, X, Xd, dec_out, dec_st


def _ssd_decay(causal, cs, cs_row, h):
    diff = cs[:, h:h + 1] - cs_row[h:h + 1, :]
    return jnp.where(causal, jnp.exp(jnp.where(causal, diff, 0.0)), 0.0)


def _ssd_groups(xc_ref, g):
    b0, c0 = SSD_INNER + g * SSD_N, SSD_INNER + 2 * SSD_N + g * SSD_N
    return xc_ref[:, b0:b0 + SSD_N].astype(BF16), xc_ref[:, c0:c0 + SSD_N].astype(BF16)


def _pair_decay(cs, pair):
    L = SSD_L
    return jnp.where(_iota((128, 128), 0) < 64, jnp.exp(cs[L - 1:L, 2 * pair:2 * pair + 1]), jnp.exp(cs[L - 1:L, 2 * pair + 1:2 * pair + 2]))


def _ssd_in_specs(nc, rev):
    idx = (lambda c: nc - 1 - c) if rev else (lambda c: c)
    return [pl.BlockSpec((SSD_L, SSD_CONV_DIM), lambda c: (idx(c), 0)), pl.BlockSpec((SSD_L, 128), lambda c: (idx(c), ODT // 128)),
            _vec_spec(128), _vec_spec(128), _vec_spec(SSD_INNER), pl.BlockSpec((128, SSD_INNER), lambda c: (0, 0))]


def _ssd_core(xc, proj, dtb, alog, dskip, E):
    S = xc.shape[0]
    nc = S // SSD_L

    def body(xc_ref, dtr_ref, dtb_ref, al_ref, dx_ref, e_ref, y_ref, hp_ref, h_scr):
        @pl.when(pl.program_id(0) == 0)
        def _():
            h_scr[...] = jnp.zeros_like(h_scr)

        hp_ref[0] = h_scr[...]
        _, _, causal, cs, cs_row, _, X, Xd, dec_out, dec_st = _ssd_chunk(xc_ref, dtr_ref, dtb_ref, al_ref, e_ref)
        Xs = Xd * dec_st
        lane = _iota((SSD_L, 128), 1)
        for g in range(2):
            Bg, Cg = _ssd_groups(xc_ref, g)
            CB = _dot(Cg, Bg, _NT)
            for pr in range(4):
                pair = g * 4 + pr
                psl = slice(pair * 128, (pair + 1) * 128)
                Xdp = Xd[:, psl].astype(BF16)
                r0 = _dot((CB * _ssd_decay(causal, cs, cs_row, 2 * pair)).astype(BF16), Xdp, _NN)
                r1 = _dot((CB * _ssd_decay(causal, cs, cs_row, 2 * pair + 1)).astype(BF16), Xdp, _NN)
                Hp = h_scr[psl, :]
                W = _dot(Cg, Hp.astype(BF16), _NT)
                y_ref[:, psl] = jnp.where(lane < 64, r0, r1) + W * dec_out[:, psl] + X[:, psl] * dx_ref[:, psl]
                h_scr[psl, :] = Hp * _pair_decay(cs, pair) + _dot(Xs[:, psl].astype(BF16), Bg, _TN)

    return pl.pallas_call(
        body, grid=(nc,), in_specs=_ssd_in_specs(nc, False),
        out_specs=(pl.BlockSpec((SSD_L, SSD_INNER), lambda c: (c, 0)), pl.BlockSpec((1, SSD_INNER, SSD_N), lambda c: (c, 0, 0))),
        out_shape=(_sds((S, SSD_INNER)), _sds((nc, SSD_INNER, SSD_N))), scratch_shapes=[pltpu.VMEM((SSD_INNER, SSD_N), F32)],
        name="ssd_core", compiler_params=_ARB)(xc, proj, dtb, alog, dskip, E)


def _ssd_core_bwd(xc, proj, hprev, dy, dtb, alog, dskip, E, ET):
    S = xc.shape[0]
    nc = S // SSD_L
    L = SSD_L

    def body(xc_ref, dtr_ref, dtb_ref, al_ref, dx_ref, e_ref, et_ref, hp_ref, dy_ref, dxc_ref, ddt_ref, st_ref, dh_scr, acc_scr):
        step = pl.program_id(0)

        @pl.when(step == 0)
        def _():
            dh_scr[...] = jnp.zeros_like(dh_scr)
            acc_scr[...] = jnp.zeros_like(acc_scr)
            st_ref[...] = jnp.zeros_like(st_ref)

        a, dtp, causal, cs, cs_row, dtx, X, Xd, dec_out, dec_st = _ssd_chunk(xc_ref, dtr_ref, dtb_ref, al_ref, e_ref)
        lane = _iota((L, 128), 1)
        sub = _iota((128, L), 0)
        dcs_col = jnp.zeros((L, 128), F32)
        dcs_row = jnp.zeros((128, L), F32)
        dcs_last = jnp.zeros((1, 128), F32)
        dcsx, ddtx, dlastx = [], [], []
        for g in range(2):
            Bg, Cg = _ssd_groups(xc_ref, g)
            CB = _dot(Cg, Bg, _NT)
            dCB = jnp.zeros((L, L), F32)
            dB = jnp.zeros((L, SSD_N), F32)
            dC = jnp.zeros((L, SSD_N), F32)
            for pr in range(4):
                pair = g * 4 + pr
                psl = slice(pair * 128, (pair + 1) * 128)
                dY, Xp, Xdp, dop, dsp = dy_ref[:, psl], X[:, psl], Xd[:, psl], dec_out[:, psl], dec_st[:, psl]
                Xdb = Xdp.astype(BF16)
                acc_scr[0:1, psl] += jnp.sum(dY * Xp, 0, keepdims=True)
                Hp = hp_ref[0, psl, :]
                Hb = Hp.astype(BF16)
                dW = (dY * dop).astype(BF16)
                dcx = dY * _dot(Cg, Hb, _NT) * dop
                dC += _dot(dW, Hb, _NN)
                dHp = _dot(dW, Cg, _TN)
                dHn = dh_scr[psl, :]
                cd = _pair_decay(cs, pair)
                dh_scr[psl, :] = dHp + dHn * cd
                rsum = jnp.sum(dHn * Hp * cd, -1, keepdims=True)
                half = _iota((128, 1), 0) < 64
                s0 = jnp.sum(jnp.where(half, rsum, 0.0), 0, keepdims=True)
                s1 = jnp.sum(jnp.where(half, 0.0, rsum), 0, keepdims=True)
                lane1 = _iota((1, 128), 1)
                dcs_last += jnp.where(lane1 == 2 * pair, s0, 0.0) + jnp.where(lane1 == 2 * pair + 1, s1, 0.0)
                dHb = dHn.astype(BF16)
                dXs = _dot(Bg, dHb, _NT)
                dB += _dot((Xdp * dsp).astype(BF16), dHb, _NN)
                dXd = dXs * dsp
                e_st = dXs * Xdp * dsp
                dcx -= e_st
                dlastx.append(jnp.sum(e_st, 0, keepdims=True))
                for i in range(2):
                    h = 2 * pair + i
                    Dm = _ssd_decay(causal, cs, cs_row, h)
                    M = CB * Dm
                    dYm = jnp.where((lane < 64) if i == 0 else (lane >= 64), dY, 0.0).astype(BF16)
                    dM = _dot(dYm, Xdb, _NT)
                    dXd += _dot(M.astype(BF16), dYm, _TN)
                    dCB += dM * Dm
                    Em = dM * M
                    dcs_col += jnp.where(lane == h, jnp.sum(Em, -1, keepdims=True), 0.0)
                    dcs_row += jnp.where(sub == h, jnp.sum(Em, 0, keepdims=True), 0.0)
                dxc_ref[:, psl] = dY * dx_ref[:, psl] + dXd * dtx[:, psl]
                ddtx.append(dXd * Xp)
                dcsx.append(dcx)
            dCBb = dCB.astype(BF16)
            b0, c0 = SSD_INNER + g * SSD_N, SSD_INNER + 2 * SSD_N + g * SSD_N
            dxc_ref[:, b0:b0 + SSD_N] = dB + _dot(dCBb, Cg, _TN)
            dxc_ref[:, c0:c0 + SSD_N] = dC + _dot(dCBb, Bg, _NN)
        ET = et_ref[...]
        dcs = dcs_col - dcs_row.T + _dot_hi(jnp.concatenate(dcsx, 1), ET)
        dlast = dcs_last + _dot_hi(jnp.broadcast_to(jnp.concatenate(dlastx, 1), (8, SSD_INNER)), ET)[0:1, :]
        dcs += jnp.where(_iota((L, 128), 0) == L - 1, dlast, 0.0)
        dda = _dot_hi((_iota((L, L), 1) >= _iota((L, L), 0)).astype(F32), dcs)
        ddtp = dda * a + _dot_hi(jnp.concatenate(ddtx, 1), ET)
        draw = ddtp * _sigmoid(dtr_ref[...] + dtb_ref[...])
        ddt_ref[...] = draw.astype(BF16)
        st_ref[0:1, :] += jnp.sum(draw, 0, keepdims=True)
        st_ref[1:2, :] += jnp.sum(dda * dtp, 0, keepdims=True) * a

        @pl.when(step == nc - 1)
        def _():
            st_ref[2:3, :] = _dot_hi(acc_scr[...], ET)[0:1, :]

    rev = lambda c: (nc - 1 - c, 0)
    return pl.pallas_call(
        body, grid=(nc,),
        in_specs=_ssd_in_specs(nc, True) + [pl.BlockSpec((SSD_INNER, 128), lambda c: (0, 0)),
                                            pl.BlockSpec((1, SSD_INNER, SSD_N), lambda c: (nc - 1 - c, 0, 0)),
                                            pl.BlockSpec((L, SSD_INNER), rev)],
        out_specs=(pl.BlockSpec((L, SSD_CONV_DIM), rev), pl.BlockSpec((L, 128), rev), pl.BlockSpec((8, 128), lambda c: (0, 0))),
        out_shape=(_sds((S, SSD_CONV_DIM)), _sds((S, 128), BF16), _sds((8, 128))),
        scratch_shapes=[pltpu.VMEM((SSD_INNER, SSD_N), F32), pltpu.VMEM((8, SSD_INNER), F32)],
        name="ssd_core_bwd", compiler_params=_ARB)(xc, proj, dtb, alog, dskip, E, ET, hprev, dy)


def _ssd_post(y, proj, nw, tm=256):
    S = y.shape[0]

    def body(y_ref, z_ref, nw_ref, o_ref):
        for g in range(2):
            sl = slice(g * 512, (g + 1) * 512)
            gated = y_ref[:, sl] * _silu(z_ref[:, sl])
            r = lax.rsqrt(jnp.mean(gated * gated, -1, keepdims=True) + EPS)
            o_ref[:, sl] = (gated * r * nw_ref[:, sl]).astype(BF16)

    return pl.pallas_call(
        body, grid=(S // tm,), in_specs=[_row_spec(tm, D), pl.BlockSpec((tm, D), lambda i: (i, OZ // D)), _vec_spec(D)],
        out_specs=_row_spec(tm, D), out_shape=_sds((S, D), BF16), name="ssd_post", compiler_params=_PAR)(y, proj, nw)


def _ssd_post_bwd(doc, y, proj, nw, tm=256):
    S = y.shape[0]

    def body(d_ref, y_ref, z_ref, nw_ref, dy_ref, dz_ref, st_ref):
        @pl.when(pl.program_id(0) == 0)
        def _():
            st_ref[...] = jnp.zeros_like(st_ref)

        for g in range(2):
            sl = slice(g * 512, (g + 1) * 512)
            yv, zv, dv = y_ref[:, sl], z_ref[:, sl], d_ref[:, sl]
            sz = _silu(zv)
            gated = yv * sz
            r = lax.rsqrt(jnp.mean(gated * gated, -1, keepdims=True) + EPS)
            ghat = gated * r
            st_ref[0:1, sl] += jnp.sum(dv * ghat, 0, keepdims=True)
            gg = dv * nw_ref[:, sl]
            dg = r * (gg - ghat * jnp.mean(gg * ghat, -1, keepdims=True))
            dy_ref[:, sl] = dg * sz
            dz_ref[:, sl] = (dg * yv * _dsilu(zv)).astype(BF16)

    zs = pl.BlockSpec((tm, D), lambda i: (i, OZ // D))
    return pl.pallas_call(
        body, grid=(S // tm,), in_specs=[_row_spec(tm, D), _row_spec(tm, D), zs, _vec_spec(D)],
        out_specs=(_row_spec(tm, D), _row_spec(tm, D), _vec_spec(D, 8)), out_shape=(_sds((S, D)), _sds((S, D), BF16), _sds((8, D))),
        name="ssd_post_bwd", compiler_params=_ARB)(doc, y, proj, nw)


def _row(v, n=None):
    v = v.astype(F32).reshape(1, -1)
    return v if n is None else jnp.pad(v, ((0, 0), (0, n - v.shape[1])))


_IN_SEGMENTS = [(0, 384, OQ), (384, 640, OKV), (640, 672, OKV + 320), (672, 1184, OP), (1184, 2208, OZ), (2208, 3744, OX),
                (3744, 3760, ODT), (3760, IN_DIM, OG)]
_IN_ZEROS = [(OKV + 256, OKV + 320), (OKV + 352, OKV + 384), (ODT + 16, ODT + 128)]


def _in_pieces():
    w, out = IN_DIM // 4, []
    for a, b, d in _IN_SEGMENTS:
        while a < b:
            k = a // w
            e = min(b, (k + 1) * w)
            out.append((k, a - k * w, e - k * w, d))
            d, a = d + e - a, e
    return out


def _win_layout(w_in4, tm=256):
    def body(w_ref, o_ref):
        for k, s0, s1, d in _in_pieces():
            o_ref[:, d:d + s1 - s0] = w_ref[k, :, s0:s1]
        for z0, z1 in _IN_ZEROS:
            o_ref[:, z0:z1] = jnp.zeros((tm, z1 - z0), o_ref.dtype)

    return pl.pallas_call(
        body, grid=(D // tm,), in_specs=[pl.BlockSpec((4, tm, IN_DIM // 4), lambda i: (0, i, 0))],
        out_specs=pl.BlockSpec((tm, IN_PAD), lambda i: (i, 0)), out_shape=_sds((D, IN_PAD), w_in4.dtype), name="win_layout",
        compiler_params=_PAR)(w_in4)


def _win_unlayout(dwin, tm=256):
    def body(d_ref, o_ref):
        for k, s0, s1, d in _in_pieces():
            o_ref[k, :, s0:s1] = d_ref[:, d:d + s1 - s0]

    return pl.pallas_call(
        body, grid=(D // tm,), in_specs=[pl.BlockSpec((tm, IN_PAD), lambda i: (i, 0))],
        out_specs=pl.BlockSpec((4, tm, IN_DIM // 4), lambda i: (0, i, 0)), out_shape=_sds((4, D, IN_DIM // 4), dwin.dtype),
        name="win_unlayout", compiler_params=_PAR)(dwin)


def _prep_late(p):
    return dict(wbr=p["w_branch"].astype(BF16), wo=p["w_out"].astype(BF16), wup4=p["ffn_up4"].astype(BF16), wdn=p["ffn_down"].astype(BF16))


def _prep_layer(p, late=None):
    win = _win_layout(p["w_in4"].astype(BF16))
    wqb = jnp.transpose(p["w_q_b4"].astype(BF16).reshape(4, Q_RANK, 2, 96), (1, 0, 2, 3))
    return dict(
        win=win, wqb=jnp.pad(wqb, ((0, 0), (0, 0), (0, 0), (0, 32))).reshape(Q_RANK, D), wkvb4=p["w_kv_b4"].astype(BF16),
        late=late if late is not None else (lambda after: _prep_late(p)),
        nw1=_row(p["norm1_w"]), nw2=_row(p["norm2_w"]), qan=_row(p["q_a_norm"]), kvan=_row(p["kv_a_norm"]),
        wq=_row(p["q_norm"], 128), wk=_row(p["k_norm"], 128), pool_w=p["pool_w"].astype(F32), pool_scale=_row(p["pool_scale"]),
        cw=p["ssd_conv_w"].astype(F32), cb=_row(p["ssd_conv_b"]), dtb=_row(p["ssd_dt_bias"], 128), alog=_row(p["ssd_a_log"], 128),
        dskip=_row(jnp.repeat(p["ssd_d"].astype(F32), SSD_P)), snw=_row(p["ssd_norm_w"]),
        fcw=p["ffn_conv_w"].astype(F32), fcb=_row(p["ffn_conv_b"]))


def _layer_fwd(x, mod8, W, rope, E, tag):
    sh1, sc1, g1, sh2, sc2, g2 = (mod8[i:i + 1] for i in range(6))
    h1 = _ln_mod(x, W["nw1"], sc1, sh1, name=f"ln1_{tag}")
    proj = _mm(h1, W["win"], tn=640, tk=1024, name=f"proj_{tag}")
    qf, kf, ve = _mla_prep(proj, W["wqb"], W["wkvb4"], W["qan"], W["kvan"], W["wq"], W["wk"], rope)
    oa = _attn_fwd(qf, kf, ve)
    ob = _pool_fwd(proj, W["pool_w"], W["pool_scale"])
    xc = _ssd_pre(proj, W["cw"], W["cb"])
    y, hprev = _ssd_core(xc, proj, W["dtb"], W["alog"], W["dskip"], E)
    oc = _ssd_post(y, proj, W["snw"])
    W.update(W["late"](oc))
    merged = _merge_fwd(oa, ob, oc, proj, W["wbr"])
    x1, out1 = _mm(merged, W["wo"], tk=1024, res=x, gate=g1, name=f"wout_{tag}")
    h2 = _ln_mod(x1, W["nw2"], sc2, sh2, name=f"ln2_{tag}")
    up = _up_fwd(h2, W["wup4"], name=f"up_{tag}")
    act = _ffn_act(up, W["fcw"], W["fcb"])
    x2, out2 = _mm(act, W["wdn"], tm=1024, res=x1, gate=g2, name=f"down_{tag}")
    saved = dict(x=x, h1=h1, proj=proj, qf=qf, kf=kf, ve=ve, oa=oa, ob=ob, oc=oc, xc=xc, hprev=hprev, y=y, merged=merged,
                 out1=out1, x1=x1, h2=h2, up=up, act=act, out2=out2)
    return x2, saved


def _layer_bwd(dx2, sv, mod8, W, rope, E, ET, tag, emit=None, mid=None):
    sc1, g1, sc2, g2 = mod8[1:2], mod8[2:3], mod8[4:5], mod8[5:6]
    proj = sv["proj"]
    dz2, dg2 = _gate_bwd(dx2, sv["out2"], g2, name=f"gate2_bwd_{tag}")
    dact = _mm(dz2, W["wdn"], "nt", tn=1408, tk=1024, name=f"down_dx_{tag}")
    dwdn = _mm(sv["act"], dz2, "tn", tm=1408, name=f"down_dw_{tag}")
    dup2, dfcw, dfcb = _ffn_act_bwd(sv["up"], dact, W["fcw"], W["fcb"])
    dh2 = _up_dx(dup2, W["wup4"], name=f"up_dx_{tag}")
    dwup4 = _up_dw(sv["h2"], dup2, name=f"up_dw_{tag}")
    dx1, st2 = _ln_mod_bwd(sv["x1"], dh2, dx2, W["nw2"], sc2, name=f"ln2_bwd_{tag}")
    dz1, dg1 = _gate_bwd(dx1, sv["out1"], g1, name=f"gate1_bwd_{tag}")
    dmerged = _mm(dz1, W["wo"], "nt", tk=1024, name=f"wout_dx_{tag}")
    dwo = _mm(sv["merged"], dz1, "tn", name=f"wout_dw_{tag}")
    dya, dyb, dyc, dgl, doa, dob, doc = _merge_bwd(dmerged, sv["oa"], sv["ob"], sv["oc"], proj, W["wbr"])
    dwba = _mm(sv["oa"], dya, "tn", name=f"wba_dw_{tag}")
    dwbb = _mm(sv["ob"], dyb, "tn", name=f"wbb_dw_{tag}")
    dwbc = _mm(sv["oc"], dyc, "tn", name=f"wbc_dw_{tag}")
    late = dict(w_branch=jnp.concatenate([dwba, dwbb, dwbc], 0).reshape(4, 512, D), w_out=dwo.reshape(4, 256, D), ffn_up=dwup4,
                ffn_down=dwdn.reshape(4, FFN // 4, D))
    snw = W["snw"]
    if emit is not None:
        token = emit(late)
        if token is not None:
            snw, doa = snw + token, doa + token
    dy, dzs, st_post = _ssd_post_bwd(doc, sv["y"], proj, snw)
    dxc, ddt, st_ssd = _ssd_core_bwd(sv["xc"], proj, sv["hprev"], dy, W["dtb"], W["alog"], W["dskip"], E, ET)
    dxbc, dcw, dcb = _ssd_pre_bwd(proj, dxc, W["cw"], W["cb"])
    if mid is not None:
        token = mid(dxbc)
        if token is not None:
            doa, dob = doa + token, dob + token
    dpool, dpw, dps = _pool_bwd(proj, dob, W["pool_w"], W["pool_scale"])
    dqf, dkf, dve = _attn_bwd(sv["qf"], sv["kf"], sv["ve"], doa)
    dqs, dkvs, dwqb, dwkvb4, st_mla = _mla_prep_bwd(proj, dqf, dkf, dve, W["wqb"], W["wkvb4"], W["qan"], W["kvan"], W["wq"], W["wk"], rope)
    dproj = jnp.concatenate([dgl, dzs, dxbc, dpool, dqs, dkvs, ddt], 1)
    dh1 = _mm(dproj, W["win"], "nt", tk=1408, name=f"proj_dx_{tag}")
    dwin = _mm(sv["h1"], dproj, "tn", tn=640, name=f"proj_dw_{tag}")
    dx, st1 = _ln_mod_bwd(sv["x"], dh1, dx1, W["nw1"], sc1, name=f"ln1_bwd_{tag}")
    grads = dict(
        norm1_w=st1[2], norm2_w=st2[2], w_in=_win_unlayout(dwin),
        q_a_norm=st_mla[0, :Q_RANK], kv_a_norm=st_mla[1, :KV_RANK], q_norm=st_mla[2, :96], k_norm=st_mla[3, :96],
        w_q_b=jnp.transpose(dwqb.reshape(Q_RANK, 4, 2, 128)[:, :, :, :96], (1, 0, 2, 3)).reshape(4, Q_RANK, 192), w_kv_b=dwkvb4,
        pool_w=dpw, pool_scale=dps[0], ssd_conv_w=dcw, ssd_conv_b=dcb[0],
        ssd_dt_bias=st_ssd[0, :SSD_HEADS], ssd_a_log=st_ssd[1, :SSD_HEADS], ssd_d=st_ssd[2, :SSD_HEADS], ssd_norm_w=st_post[0],
        ffn_conv_w=jnp.transpose(dfcw, (1, 0, 2)).reshape(3, 2 * FFN), ffn_conv_b=dfcb.reshape(2 * FFN), **late)
    dmod = jnp.concatenate([st1[0:2], dg1[0:1], st2[0:2], dg2[0:1]], 0)
    return dx, grads, dmod


def _ssd_expand():
    E = (jnp.arange(SSD_INNER)[None, :] // SSD_P == jnp.arange(128)[:, None]).astype(F32)
    return E, E.T


def _rope_tables(positions):
    inv_freq = ROPE_THETA ** (-jnp.arange(0, ROPE_DIM, 2, dtype=F32) / ROPE_DIM)
    invf = jnp.concatenate([jnp.zeros((NOPE,), F32), inv_freq, inv_freq, jnp.zeros((32,), F32)]).reshape(1, 128)
    posb = jnp.broadcast_to(positions.astype(F32)[:, None], (positions.shape[0], 128))
    return _rope_tab(posb, invf)


def _local_step(x, target, positions, mods, get_layer, bwd_mod=None, emit=None, mid=None, done=None):
    rope = _rope_tables(positions)
    E, ET = _ssd_expand()
    Ws, saved, h = [], [], x
    for l in range(2):
        Ws.append(_prep_layer(*get_layer(l, h)))
        h, sv = _layer_fwd(h, mods[l], Ws[l], rope, E, l)
        saved.append(sv)
    dy, lpart = _loss_grad(h, target)
    grads, dmods = [None, None], [None, None]
    for l in (1, 0):
        mod8 = mods[l] if bwd_mod is None else bwd_mod(l)
        hook = lambda f: None if f is None else functools.partial(f, l)
        dy, grads[l], dmods[l] = _layer_bwd(dy, saved[l], mod8, Ws[l], rope, E, ET, l, hook(emit), hook(mid))
        if done is not None:
            done(l, grads[l])
    return lpart[0, 0], dy, grads, dmods


_ANY = pl.BlockSpec(memory_space=pl.ANY)
_VMEM = pl.BlockSpec(memory_space=pltpu.VMEM)


def _place():
    x, y, c = lax.axis_index("x"), lax.axis_index("y"), lax.axis_index("c")
    return x, y, c, [(1 - x, y), (x, 1 - y), (1 - x, 1 - y)]


def _allgather8(v, name):
    m_per, n = v.shape

    def body(x_ref, out_ref, send_sems, recv_sems, local_sem):
        x, y, c, chips = _place()
        me, sibling = (x, y, c), (x, y, 1 - c)

        def rows(px, py, pc):
            return out_ref.at[pl.ds((4 * px + 2 * py + pc) * m_per, m_per), :]

        def copy(k, block, to, src=None):
            return pltpu.make_async_remote_copy(src_ref=rows(*block) if src is None else src, dst_ref=rows(*block),
                                                send_sem=send_sems.at[k], recv_sem=recv_sems.at[k], device_id=to, device_id_type=MESH)

        mine = pltpu.make_async_copy(x_ref, rows(*me), local_sem)
        mine.start()
        first = [copy(0, me, sibling, src=x_ref)] + [copy(1 + j, me, (*chip, c), src=x_ref) for j, chip in enumerate(chips)]
        for cp in first:
            cp.start()
        passed = [copy(4 + j, (*chip, c), sibling) for j, chip in enumerate(chips)]
        for j, chip in enumerate(chips):
            copy(1 + j, (*chip, c), me).wait_recv()
            passed[j].start()
        copy(0, sibling, me).wait_recv()
        for j, chip in enumerate(chips):
            copy(4 + j, (*chip, 1 - c), me).wait_recv()
        for cp in first + passed:
            cp.wait_send()
        mine.wait()

    return pl.pallas_call(
        body, out_shape=_sds((8 * m_per, n), v.dtype), in_specs=[_VMEM], out_specs=_VMEM,
        scratch_shapes=[pltpu.SemaphoreType.DMA((7,)), pltpu.SemaphoreType.DMA((7,)), pltpu.SemaphoreType.DMA], name=name)(v)


def _sems(n):
    return [pltpu.SemaphoreType.DMA((n,)), pltpu.SemaphoreType.DMA((n,))]


_HBM = pl.BlockSpec(memory_space=pltpu.HBM)
_SEM = pl.BlockSpec(memory_space=pltpu.SEMAPHORE)
_EFFECT = pltpu.CompilerParams(has_side_effects=pltpu.SideEffectType.DATAFLOW_SIDE_EFFECTING)


def _ici_copy(src_refs, land_refs, send_sems, recv_sems, a, j, slices, incoming):
    x, y, c, chips = _place()
    if _peers(slices) == 1:
        src, dst = slices(src_refs[a], land_refs[a], 1 - c if incoming else c)
        return pltpu.make_async_remote_copy(src_ref=src, dst_ref=dst, send_sem=send_sems.at[a], recv_sem=recv_sems.at[a],
                                            device_id=(x, y, 1 - c), device_id_type=MESH)
    me, other = 2 * x + y, 2 * chips[j][0] + chips[j][1]
    src, dst = slices(src_refs[a], land_refs[a], other, me, c) if incoming else slices(src_refs[a], land_refs[a], me, other, c)
    return pltpu.make_async_remote_copy(src_ref=src, dst_ref=dst, send_sem=send_sems.at[3 * a + j], recv_sem=recv_sems.at[3 * a + j],
                                        device_id=(*chips[j], c), device_id_type=MESH)


def _peers(slices):
    return 1 if slices is _halves_slices else 3


def _ici_start(srcs, land_shapes, slices, after, name):
    na = len(srcs)

    def body(*refs):
        src_refs, land_refs, send_sems, recv_sems = refs[:na], refs[na:2 * na], refs[2 * na + 1], refs[2 * na + 2]
        for a in range(na):
            for j in range(_peers(slices)):
                _ici_copy(src_refs, land_refs, send_sems, recv_sems, a, j, slices, False).start()
        refs[-1][...] = jnp.zeros_like(refs[-1])

    hbm = lambda v: pltpu.with_memory_space_constraint(v, pltpu.HBM)
    lands = [hbm(lax.empty(s.shape, s.dtype)) for s in land_shapes]
    return pl.pallas_call(
        body, name=name,
        out_shape=(pltpu.SemaphoreType.DMA((_peers(slices) * na,)), pltpu.SemaphoreType.DMA((_peers(slices) * na,)),
                   *[pltpu.HBM(v.shape, v.dtype) for v in srcs],
                   *[pltpu.HBM(s.shape, s.dtype) for s in land_shapes], _sds((8, 128))),
        in_specs=[_HBM] * (2 * na) + [_ANY], out_specs=(_SEM, _SEM, *[_HBM] * (2 * na), _VMEM),
        input_output_aliases={i: 2 + i for i in range(2 * na)}, compiler_params=_EFFECT)(*[hbm(v) for v in srcs], *lands, after)


def _ici_wait(handle, slices, after, name):
    na = (len(handle) - 3) // 2

    def body(*refs):
        src_refs, land_refs, send_sems, recv_sems = refs[:na], refs[na:2 * na], refs[2 * na], refs[2 * na + 1]
        for a in range(na):
            for j in range(_peers(slices)):
                _ici_copy(src_refs, land_refs, send_sems, recv_sems, a, j, slices, False).wait_send()
                _ici_copy(src_refs, land_refs, send_sems, recv_sems, a, j, slices, True).wait_recv()

    thru = handle[2:2 + 2 * na]
    outs = pl.pallas_call(
        body, name=name, out_shape=[pltpu.HBM(v.shape, v.dtype) for v in thru], in_specs=[_HBM] * (2 * na) + [_SEM, _SEM, _ANY],
        out_specs=[_HBM] * (2 * na), input_output_aliases={i: i for i in range(2 * na)}, compiler_params=_EFFECT)(
        *thru, handle[0], handle[1], after)
    return outs[:na], outs[na:]


def _gather_slices(p_ref, land_ref, sender, receiver, c):
    r2 = p_ref.shape[0] // 2
    return p_ref.at[pl.ds(c * r2, r2), :], land_ref.at[sender, pl.ds(c * r2, r2), :]


def _scatter_slices(a_ref, t_ref, sender, receiver, c):
    return a_ref.at[receiver], t_ref.at[sender]


def _halves_slices(g_ref, land_ref, sender_c):
    r2 = g_ref.shape[1] // 2
    return g_ref.at[:, pl.ds((1 - sender_c) * r2, r2), :], land_ref


def _gather_start(arrs, after, name):
    return _ici_start(arrs, [_sds((4,) + v.shape, v.dtype) for v in arrs], _gather_slices, after, name)


def _gather_finish(handle, after, name):
    arrs, stacks = _ici_wait(handle, _gather_slices, after, name + "_wait")
    na = len(stacks)

    def body(*refs):
        s_refs, o_refs, (send_sems, recv_sems) = refs[:na], refs[na:2 * na], refs[2 * na:]
        x, y, c, chips = _place()

        def copy(a, j, cc, to):
            r2 = s_refs[a].shape[1] // 2
            at = (2 * chips[j][0] + chips[j][1], pl.ds(cc * r2, r2), slice(None))
            return pltpu.make_async_remote_copy(src_ref=s_refs[a].at[at], dst_ref=o_refs[a].at[at], send_sem=send_sems.at[3 * a + j],
                                                recv_sem=recv_sems.at[3 * a + j], device_id=to, device_id_type=MESH)

        passed = [copy(a, j, c, (x, y, 1 - c)) for a in range(na) for j in range(3)]
        for cp in passed:
            cp.start()
        for a in range(na):
            for j in range(3):
                copy(a, j, 1 - c, (x, y, c)).wait_recv()
        for cp in passed:
            cp.wait_send()

    stacks = pl.pallas_call(
        body, out_shape=[_sds(v.shape, v.dtype) for v in stacks], in_specs=[_ANY] * na, out_specs=[_ANY] * na,
        input_output_aliases={i: i for i in range(na)}, scratch_shapes=_sems(3 * na), name=name + "_pass")(*stacks)
    chip = 2 * lax.axis_index("x") + lax.axis_index("y")
    return [lax.dynamic_update_slice(s, v[None], (chip, 0, 0)) for s, v in zip(stacks, arrs)]


def _halves_start(gs, after, tag):
    return _ici_start(gs, [_sds((4, v.shape[1] // 2, v.shape[2]), v.dtype) for v in gs], _halves_slices, after, f"rs_halves_{tag}_start")


def _join_halves(fs, name):
    na = len(fs)

    def body(*refs):
        f_refs, o_refs, (send_sems, recv_sems) = refs[:na], refs[na:2 * na], refs[2 * na:]
        x, y, c, _ = _place()

        def copy(a, cc, to):
            r2 = f_refs[a].shape[0]
            return pltpu.make_async_remote_copy(src_ref=f_refs[a], dst_ref=o_refs[a].at[pl.ds(cc * r2, r2), :], send_sem=send_sems.at[a],
                                                recv_sem=recv_sems.at[a], device_id=to, device_id_type=MESH)

        cps = [copy(a, c, (x, y, 1 - c)) for a in range(na)]
        for cp in cps:
            cp.start()
        for a in range(na):
            copy(a, 1 - c, (x, y, c)).wait_recv()
        for cp in cps:
            cp.wait_send()

    outs = pl.pallas_call(
        body, out_shape=[_sds((2 * v.shape[0], v.shape[1]), v.dtype) for v in fs], in_specs=[_ANY] * na, out_specs=[_ANY] * na,
        scratch_shapes=_sems(na), name=name)(*fs)
    ci = lax.axis_index("c")
    return [lax.dynamic_update_slice(o, f, (ci * f.shape[0], 0)) for o, f in zip(outs, fs)]


def _sum_chips(a, t, chip, name):
    _, r2, n = t.shape
    tm = _row_tile(r2)

    def body(k_ref, a_ref, t1_ref, t2_ref, t3_ref, o_ref):
        o_ref[...] = ((a_ref[...].astype(F32) + t1_ref[...].astype(F32)) + t2_ref[...].astype(F32)) + t3_ref[...].astype(F32)

    def slot(j):
        return pl.BlockSpec((None, tm, n), lambda i, k_ref: (lax.rem(k_ref[0] + j, 4), i, 0))

    return pl.pallas_call(
        body, grid_spec=pltpu.PrefetchScalarGridSpec(num_scalar_prefetch=1, grid=(r2 // tm,), in_specs=[slot(0), slot(1), slot(2), slot(3)],
                                                     out_specs=pl.BlockSpec((tm, n), lambda i, k_ref: (i, 0))),
        out_shape=_sds((r2, n)), name=name, compiler_params=_PAR)(chip.reshape(1).astype(jnp.int32), a, t, t, t)


def _add_cast(g, recv, c, name):
    _, r2, n = recv.shape

    def body(c_ref, a_ref, b_ref, o_ref):
        o_ref[...] = (a_ref[...] + b_ref[...]).astype(BF16)

    spec = pl.BlockSpec((None, r2, n), lambda k, c_ref: (k, 0, 0))
    return pl.pallas_call(
        body, grid_spec=pltpu.PrefetchScalarGridSpec(
            num_scalar_prefetch=1, grid=(4,), in_specs=[pl.BlockSpec((None, r2, n), lambda k, c_ref: (k, c_ref[0], 0)), spec], out_specs=spec),
        out_shape=_sds(recv.shape, BF16), name=name, compiler_params=_PAR)(c.reshape(1).astype(jnp.int32), g, recv)


def _sum_lead(t, name, tm=256):
    P, R, n = t.shape
    tm = _row_tile(R, tm)

    def body(t_ref, o_ref):
        acc = t_ref[0].astype(F32)
        for j in range(1, P):
            acc = acc + t_ref[j].astype(F32)
        o_ref[...] = acc

    return pl.pallas_call(body, grid=(R // tm,), in_specs=[pl.BlockSpec((P, tm, n), lambda i: (0, i, 0))],
                          out_specs=pl.BlockSpec((tm, n), lambda i: (i, 0)), out_shape=_sds((R, n)), name=name, compiler_params=_PAR)(t)


def _ada_fwd(c16, ada_w, ada_b_cols, tn=512):
    L, _, n = ada_w.shape

    def body(c_ref, w_ref, b_ref, o_ref):
        o_ref[0] = _dot(_silu(c_ref[...]).astype(BF16), w_ref[0].astype(BF16), _NN) + b_ref[0]

    return pl.pallas_call(
        body, grid=(L, n // tn),
        in_specs=[pl.BlockSpec((16, D), lambda l, j: (0, 0)), pl.BlockSpec((1, D, tn), lambda l, j: (l, 0, j)), pl.BlockSpec((1, 1, tn), lambda l, j: (l, 0, j))],
        out_specs=pl.BlockSpec((1, 16, tn), lambda l, j: (l, 0, j)), out_shape=_sds((L, 16, n)), name="ada_fwd",
        compiler_params=pltpu.CompilerParams(dimension_semantics=("parallel", "parallel")))(c16, ada_w, ada_b_cols)


def _ada_bwd(c16, dmod, tn=512):
    L, _, n = dmod.shape

    def body(c_ref, d_ref, o_ref):
        o_ref[0] = _dot(_silu(c_ref[...]).astype(BF16), d_ref[0].astype(BF16), _TN)

    return pl.pallas_call(
        body, grid=(L, n // tn), in_specs=[pl.BlockSpec((16, D), lambda l, j: (0, 0)), pl.BlockSpec((1, 16, tn), lambda l, j: (l, 0, j))],
        out_specs=pl.BlockSpec((1, D, tn), lambda l, j: (l, 0, j)), out_shape=_sds((L, D, n)), name="ada_bwd",
        compiler_params=pltpu.CompilerParams(dimension_semantics=("parallel", "parallel")))(c16, dmod)


def _adam_math(w, g, m, v):
    mn = ADAM_B1 * m + (1.0 - ADAM_B1) * g
    vn = ADAM_B2 * v + (1.0 - ADAM_B2) * (g * g)
    m_hat = mn / (1.0 - ADAM_B1 ** ADAM_STEP)
    v_hat = vn / (1.0 - ADAM_B2 ** ADAM_STEP)
    return -ADAM_LR * (m_hat / (jnp.sqrt(v_hat) + ADAM_EPS) + ADAM_WD * w), mn, vn


def _adamw(w, g, m, v, name):
    R, n = w.shape
    tm = _row_tile(R)

    def body(w_ref, g_ref, m_ref, v_ref, d_ref, nm_ref, nv_ref):
        d_ref[...], nm_ref[...], nv_ref[...] = _adam_math(w_ref[...], g_ref[...], m_ref[...], v_ref[...])

    spec = pl.BlockSpec((tm, n), lambda i: (i, 0))
    return pl.pallas_call(body, grid=(R // tm,), in_specs=[spec] * 4, out_specs=(spec,) * 3, out_shape=(_sds((R, n)),) * 3,
                          name=name, compiler_params=_PAR)(w, g, m, v)


def _adamw_cols(w, g0, g1, m, v, name):
    fwd, back = (lambda t: jnp.transpose(t, (2, 0, 1))), (lambda t: jnp.transpose(t, (1, 2, 0)))
    gt = jnp.stack([g0.T, g1.T], 1)
    n, _, r = gt.shape
    tr = max(t for t in range(1, 257) if n % t == 0)

    def body(w_ref, g_ref, m_ref, v_ref, d_ref, nm_ref, nv_ref):
        d_ref[...], nm_ref[...], nv_ref[...] = _adam_math(w_ref[...], g_ref[...], m_ref[...], v_ref[...])

    spec = pl.BlockSpec((tr, 2, r), lambda i: (i, 0, 0))
    outs = pl.pallas_call(body, grid=(n // tr,), in_specs=[spec] * 4, out_specs=(spec,) * 3, out_shape=(_sds(gt.shape),) * 3,
                          name=name, compiler_params=_PAR)(fwd(w), gt, fwd(m), fwd(v))
    return (back(gt), *[back(o) for o in outs])


def _adamw_layers(w, g0, g1, m, v, after, name):
    _, r, n = w.shape
    tm = _row_tile(r)
    nb = r // tm

    def body(w_ref, g0_ref, g1_ref, m_ref, v_ref, after_ref, g_ref, d_ref, nm_ref, nv_ref):
        gv = jnp.where(pl.program_id(0) == 0, g0_ref[...], g1_ref[...])
        g_ref[...] = gv
        d_ref[...], nm_ref[...], nv_ref[...] = _adam_math(w_ref[...], gv, m_ref[...], v_ref[...])

    spec = pl.BlockSpec((None, tm, n), lambda l, i: (l, i, 0))
    g0_spec = pl.BlockSpec((tm, n), lambda l, i: (i * (1 - l) + (nb - 1) * l, 0))
    g1_spec = pl.BlockSpec((tm, n), lambda l, i: (i * l, 0))
    return pl.pallas_call(body, grid=(2, nb), in_specs=[spec, g0_spec, g1_spec, spec, spec, _ANY], out_specs=(spec,) * 4,
                          out_shape=(_sds(w.shape),) * 4, name=name,
                          compiler_params=pltpu.CompilerParams(dimension_semantics=("arbitrary", "arbitrary")))(w, g0, g1, m, v, after)


_W_NAMES = ["ada_w", "ada_b", "norm1_w", "w_in", "q_a_norm", "w_q_b", "kv_a_norm", "w_kv_b", "q_norm", "k_norm", "pool_w",
            "pool_scale", "ssd_conv_w", "ssd_conv_b", "ssd_dt_bias", "ssd_a_log", "ssd_d", "ssd_norm_w", "w_branch", "w_out",
            "norm2_w", "ffn_up", "ffn_conv_w", "ffn_conv_b", "ffn_down"]
_BIG = [("w_in", (D, IN_DIM // 4), 1), ("w_q_b", (Q_RANK, 192), 1), ("w_kv_b", (KV_RANK, 256), 1), ("w_branch", (512, D), 0),
        ("w_out", (256, D), 0), ("ffn_up", (D, 2 * FFN // 4), 1), ("ffn_down", (FFN // 4, D), 0)]

_SMALL = [("norm1_w", (D,)), ("q_a_norm", (Q_RANK,)), ("kv_a_norm", (KV_RANK,)), ("q_norm", (96,)), ("k_norm", (96,)),
          ("pool_w", (4, 128, 128)), ("pool_scale", (512,)), ("ssd_conv_w", (4, SSD_CONV_DIM)), ("ssd_conv_b", (SSD_CONV_DIM,)),
          ("ssd_dt_bias", (SSD_HEADS,)), ("ssd_a_log", (SSD_HEADS,)), ("ssd_d", (SSD_HEADS,)), ("ssd_norm_w", (D,)), ("norm2_w", (D,)),
          ("ffn_conv_w", (3, 2 * FFN)), ("ffn_conv_b", (2 * FFN,))]
_CONV_SHARDED = {"ssd_conv_w": SSD_CONV_DIM // 4, "ffn_conv_w": 2 * FFN // 4}


def _pack_flat(arrs, mult):
    flat = jnp.concatenate([a.astype(F32).reshape(-1) for a in arrs])
    rows = -(-flat.shape[0] // (128 * mult)) * mult
    return jnp.pad(flat, (0, rows * 128 - flat.shape[0])).reshape(rows, 128), [a.shape for a in arrs]


def _unpack_flat(packed, shapes):
    flat, out, off = packed.reshape(-1), [], 0
    for s in shapes:
        n = 1
        for d in s:
            n *= d
        out.append(flat[off:off + n].reshape(s))
        off += n
    return out


_EARLY = ["w_in", "w_q_b", "w_kv_b"]
_LATE = ["w_branch", "w_out", "ffn_up", "ffn_down"]


def _early_weights(a, l, stacks, conv_full):
    p = {n: a[n][l] for n in _W_NAMES if n not in ("ada_w", "ada_b")}
    p.update({n: conv_full[n][l] for n in conv_full})
    p.update(w_in4=stacks[0], w_q_b4=stacks[1], w_kv_b4=stacks[2])
    return p


def _late_weights(stacks):
    return _prep_late(dict(w_branch=stacks[0].reshape(2048, D), w_out=stacks[1].reshape(D, D), ffn_up4=stacks[2],
                           ffn_down=stacks[3].reshape(FFN, D)))


def _reduce_start(halves, ci, after, tag):
    chip_sum = []
    for t, h in halves.items():
        gs, recv = _ici_wait(h, _halves_slices, after, f"rs_halves_{t}_wait")
        chip_sum += [_add_cast(g, r, ci, f"rs_add_{t}") for g, r in zip(gs, recv)]
    return _ici_start(chip_sum, [_sds(v.shape, v.dtype) for v in chip_sum], _scatter_slices, after, f"rs_scatter_{tag}_start")


def _reduce_finish(started, after, tag):
    chip_sum, got = _ici_wait(started, _scatter_slices, after, f"rs_scatter_{tag}_wait")
    chip = 2 * lax.axis_index("x") + lax.axis_index("y")
    return _join_halves([_sum_chips(s, t, chip, f"rs_sum_{tag}") for s, t in zip(chip_sum, got)], f"rs_join_{tag}")


def kernel(x, c, positions, ada_w, ada_b, norm1_w, w_in, q_a_norm, w_q_b, kv_a_norm, w_kv_b, q_norm, k_norm, pool_w, pool_scale, ssd_conv_w, ssd_conv_b, ssd_dt_bias, ssd_a_log, ssd_d, ssd_norm_w, w_branch, w_out, norm2_w, ffn_up, ffn_conv_w, ffn_conv_b, ffn_down, loss_target, m_ada_w, m_ada_b, m_norm1_w, m_w_in, m_q_a_norm, m_w_q_b, m_kv_a_norm, m_w_kv_b, m_q_norm, m_k_norm, m_pool_w, m_pool_scale, m_ssd_conv_w, m_ssd_conv_b, m_ssd_dt_bias, m_ssd_a_log, m_ssd_d, m_ssd_norm_w, m_w_branch, m_w_out, m_norm2_w, m_ffn_up, m_ffn_conv_w, m_ffn_conv_b, m_ffn_down, v_ada_w, v_ada_b, v_norm1_w, v_w_in, v_q_a_norm, v_w_q_b, v_kv_a_norm, v_w_kv_b, v_q_norm, v_k_norm, v_pool_w, v_pool_scale, v_ssd_conv_w, v_ssd_conv_b, v_ssd_dt_bias, v_ssd_a_log, v_ssd_d, v_ssd_norm_w, v_w_branch, v_w_out, v_norm2_w, v_ffn_up, v_ffn_conv_w, v_ffn_conv_b, v_ffn_down):
    a = dict(locals())
    xi, yi, ci = lax.axis_index("x"), lax.axis_index("y"), lax.axis_index("c")
    chip = 2 * xi + yi
    dev = 2 * chip + ci
    ncol = 6 * D // 4

    c_all = _allgather8(c.reshape(8, 128), "gather_c").reshape(8, D)
    c16 = jnp.pad(c_all, ((0, 8), (0, 0)))
    ada_b_cols = lax.dynamic_slice_in_dim(ada_b, chip * ncol, ncol, axis=1).reshape(2, 1, ncol)
    mod_part = _ada_fwd(c16, ada_w, ada_b_cols)[:, :8]
    small1, shapes1 = _pack_flat([mod_part, ssd_conv_w, ffn_conv_w], 8)
    got1 = _allgather8(small1, "gather_mod").reshape(8, -1, 128)
    per_chip = [_unpack_flat(got1[2 * k], shapes1) for k in range(4)]
    mod_all = jnp.concatenate([per_chip[k][0] for k in range(4)], -1)
    conv_full = {"ssd_conv_w": jnp.concatenate([per_chip[k][1] for k in range(4)], -1),
                 "ffn_conv_w": jnp.concatenate([per_chip[k][2] for k in range(4)], -1)}
    mod_mine = lax.dynamic_index_in_dim(mod_all, dev, axis=1, keepdims=False).reshape(2, 6, D)
    mods = [jnp.pad(mod_mine[l], ((0, 2), (0, 0))) for l in range(2)]

    big = _EARLY + _LATE
    shard = lambda names, l: [a[n][l].astype(BF16) for n in names]
    g0a = _gather_start(shard(_EARLY, 0), mods[0], "gather_0a")
    g0b = _gather_start(shard(_LATE, 0), g0a[-1], "gather_0b")
    g1 = _gather_start(shard(big, 1), g0b[-1], "gather_1")
    mods[0] = mods[0] + g1[-1][0, 0]

    def get_layer(l, after):
        if l == 0:
            return (_early_weights(a, 0, _gather_finish(g0a, mods[0], "gather_0a"), conv_full),
                    lambda aft: _late_weights(_gather_finish(g0b, aft, "gather_0b")))
        stacks = _gather_finish(g1, after, "gather_1")
        return _early_weights(a, 1, stacks[:3], conv_full), lambda aft: _late_weights(stacks[3:])

    halves, scatters = {}, {}

    def bwd_mod(l):
        return mods[l] if l == 1 else mods[0] + halves["1a"][-1][0, 0]

    def emit(l, late):
        if l == 1:
            halves["1b"] = _halves_start([late[n] for n in _LATE], late["ffn_down"], "1b")
            return halves["1b"][-1][0, 0]
        scatters["1"] = _reduce_start({"1b": halves["1b"], "1a": halves["1a"]}, ci, late["ffn_down"], "1")
        halves["0b"] = _halves_start([late[n] for n in _LATE], scatters["1"][-1], "0b")
        return halves["0b"][-1][0, 0]

    def mid(l, after):
        if l == 0:
            scatters["0b"] = _reduce_start({"0b": halves["0b"]}, ci, after, "0b")
            return scatters["0b"][-1][0, 0]

    def done(l, grads_l):
        halves[f"{l}a"] = _halves_start([grads_l[n] for n in _EARLY], halves[f"{l}b"][-1], f"{l}a")

    lpart, grad_x, grads, dmods = _local_step(x[0], loss_target[0], positions[0], mods, get_layer, bwd_mod, emit, mid, done)
    loss = lax.psum(lpart, ("x", "y", "c"))
    red1 = dict(zip(_LATE + _EARLY, _reduce_finish(scatters["1"], grad_x, "1")))
    red0b = _reduce_finish(scatters["0b"], red1["w_in"], "0b")
    scatters["0a"] = _reduce_start({"0a": halves["0a"]}, ci, red0b[0], "0a")

    small2, shapes2 = _pack_flat([jnp.stack(dmods)] + [grads[l][n] for l in range(2) for n, _ in _SMALL], 16)
    got2 = _allgather8(small2, "gather_small").reshape(8, -1, 128)
    tot = _unpack_flat(_sum_lead(got2, "sum_small"), shapes2)
    g = {"ada_b": tot[0].reshape(2, 6 * D)}
    for i, (n, _) in enumerate(_SMALL):
        g[n] = jnp.stack([tot[1 + i], tot[1 + len(_SMALL) + i]])
    for n, w in _CONV_SHARDED.items():
        g[n] = lax.dynamic_slice_in_dim(g[n], chip * w, w, axis=2)
    nd = 2 * 6 * D // 128
    dmod_all = jnp.transpose(got2[:, :nd].reshape(8, 2, 6 * D), (1, 0, 2))
    dmod_cols = lax.dynamic_slice_in_dim(jnp.pad(dmod_all, ((0, 0), (0, 8), (0, 0))), chip * ncol, ncol, axis=2)
    g["ada_w"] = _ada_bwd(c16, dmod_cols)

    delta, new_m, new_v = {}, {}, {}
    token = scatters["0a"][-1]
    for n, r0 in zip(_LATE, red0b):
        g[n], delta[n], new_m[n], new_v[n] = _adamw_layers(a[n], r0, red1[n], a["m_" + n], a["v_" + n], token, f"adamw_{n}")
    shp = ada_w.shape
    r2 = lambda t: t.reshape(-1, shp[-1])
    delta["ada_w"], new_m["ada_w"], new_v["ada_w"] = (
        o.reshape(shp) for o in _adamw(r2(ada_w), r2(g["ada_w"]), r2(m_ada_w), r2(v_ada_w), "adamw_ada_w"))
    behind = (delta[_LATE[-1]][0, 0, :1] + delta["ada_w"][0, 0, :1]).reshape(1)
    red0a = _reduce_finish(scatters["0a"], behind, "0a")
    for n, r0 in zip(_EARLY, red0a):
        if n == "w_in":
            g[n], delta[n], new_m[n], new_v[n] = _adamw_cols(a[n], r0, red1[n], a["m_" + n], a["v_" + n], f"adamw_{n}")
        else:
            g[n], delta[n], new_m[n], new_v[n] = _adamw_layers(a[n], r0, red1[n], a["m_" + n], a["v_" + n], token, f"adamw_{n}")
    rest = [n for n in _W_NAMES if n not in big and n != "ada_w"]
    packs = [_pack_flat([t[n] if pre is None else t[pre + n] for n in rest], 128)[0]
             for t, pre in ((a, None), (g, None), (a, "m_"), (a, "v_"))]
    rest_shapes = [a[n].shape for n in rest]
    outs = [_unpack_flat(o, rest_shapes) for o in _adamw(*packs, "adamw_rest")]
    for i, n in enumerate(rest):
        delta[n], new_m[n], new_v[n] = outs[0][i], outs[1][i], outs[2][i]

    return (loss, grad_x[None], *[g[n] for n in _W_NAMES], *[delta[n] for n in _W_NAMES],
            *[new_m[n] for n in _W_NAMES], *[new_v[n] for n in _W_NAMES])
```

```python
import functools

import jax
import jax.numpy as jnp
from jax import lax
from jax.experimental import pallas as pl
from jax.experimental.pallas import tpu as pltpu

F32 = jnp.float32
BF16 = jnp.bfloat16
MESH = pl.DeviceIdType.MESH
HI = lax.Precision.HIGHEST

D = 1024
N_HEADS = 8
NOPE, ROPE_DIM = 64, 32
Q_RANK, KV_RANK = 384, 256
POOL_WINDOWS = (2, 4, 8, 16)
SSD_HEADS, SSD_P, SSD_N, SSD_L = 16, 64, 128, 128
SSD_INNER = 1024
SSD_CONV_DIM = 1536
FFN = 2816
EPS = 1e-6
ROPE_THETA = 10000.0
OG, OZ, OX, OP, OQ, OKV, ODT, IN_PAD = 0, 3072, 4096, 5632, 6144, 6528, 6912, 7040
IN_DIM = 6832
ADAM_LR, ADAM_B1, ADAM_B2, ADAM_EPS, ADAM_WD, ADAM_STEP = 0.001, 0.9, 0.999, 1e-08, 0.01, 10

_ARB = pltpu.CompilerParams(dimension_semantics=("arbitrary",))
_PAR = pltpu.CompilerParams(dimension_semantics=("parallel",))


def _pick(n, pref):
    if n <= pref:
        return n
    best = None
    for t in range(128, pref + 1, 128):
        if n % t == 0:
            best = t
    assert best is not None, (n, pref)
    return best


def _row_tile(r, cap=256):
    best = None
    for t in range(16, min(r, cap) + 1, 16):
        if r % t == 0:
            best = t
    assert best is not None, r
    return best


def _sds(shape, dtype=F32):
    return jax.ShapeDtypeStruct(tuple(shape), dtype)


def _iota(shape, dim):
    return lax.broadcasted_iota(jnp.int32, shape, dim)


def _sigmoid(x):
    return 0.5 * jnp.tanh(0.5 * x) + 0.5


def _silu(x):
    return x * _sigmoid(x)


def _dsilu(x):
    s = _sigmoid(x)
    return s * (1.0 + x * (1.0 - s))


def _dot(a, b, dims):
    return lax.dot_general(a, b, (dims, ((), ())), preferred_element_type=F32)


_NN, _NT, _TN = ((1,), (0,)), ((1,), (1,)), ((0,), (0,))


def _dot_hi(a, b, dims=_NN):
    return lax.dot_general(a, b, (dims, ((), ())), preferred_element_type=F32, precision=HI)


def _shift_down(x, j):
    n = x.shape[0]
    return jnp.where(_iota(x.shape, 0) >= j, pltpu.roll(x, j, 0), 0.0)


def _shift_up(x, j):
    n = x.shape[0]
    return jnp.where(_iota(x.shape, 0) < n - j, pltpu.roll(x, n - j, 0), 0.0)


def _mm(a, b, mode="nn", out_dtype=F32, tm=2048, tn=512, tk=4096, res=None, gate=None, name="mm"):
    if mode == "nn":
        (M, K), (K2, N) = a.shape, b.shape
    elif mode == "nt":
        (M, K), (N, K2) = a.shape, b.shape
    else:
        (K, M), (K2, N) = a.shape, b.shape
    assert K == K2, (a.shape, b.shape, mode)
    tm, tn, tk = _pick(M, tm), _pick(N, tn), _pick(K, tk)
    nk = K // tk
    dims = {"nn": _NN, "nt": _NT, "tn": _TN}[mode]
    fused = res is not None

    def body(*refs):
        a_ref, b_ref = refs[:2]

        def finish(acc):
            if fused:
                r_ref, g_ref, o_ref, raw_ref = refs[2:6]
                raw_ref[...] = acc
                o_ref[...] = r_ref[...] + g_ref[...] * acc
            else:
                refs[2][...] = acc.astype(out_dtype)

        _mm_steps(a_ref, b_ref, dims, nk, refs[-1] if nk > 1 else None, finish)

    if mode == "nn":
        a_spec = pl.BlockSpec((tm, tk), lambda i, j, k: (i, k))
        b_spec = pl.BlockSpec((tk, tn), lambda i, j, k: (k, j))
    elif mode == "nt":
        a_spec = pl.BlockSpec((tm, tk), lambda i, j, k: (i, k))
        b_spec = pl.BlockSpec((tn, tk), lambda i, j, k: (j, k))
    else:
        a_spec = pl.BlockSpec((tk, tm), lambda i, j, k: (k, i))
        b_spec = pl.BlockSpec((tk, tn), lambda i, j, k: (k, j))
    o_spec = pl.BlockSpec((tm, tn), lambda i, j, k: (i, j))
    in_specs, args = [a_spec, b_spec], [a, b]
    out_shape, out_specs = _sds((M, N), out_dtype), o_spec
    if fused:
        in_specs += [o_spec, pl.BlockSpec((1, tn), lambda i, j, k: (0, j))]
        args += [res, gate]
        out_shape, out_specs = (_sds((M, N)), _sds((M, N))), (o_spec, o_spec)
    return pl.pallas_call(
        body, grid=(M // tm, N // tn, nk), in_specs=in_specs, out_specs=out_specs, out_shape=out_shape,
        scratch_shapes=[pltpu.VMEM((tm, tn), F32)] if nk > 1 else [], name=name,
        compiler_params=pltpu.CompilerParams(dimension_semantics=("parallel", "parallel", "arbitrary")),
    )(*args)


def _mm_steps(a_ref, b_ref, dims, nk, acc_ref, finish):
    part = _dot(a_ref[...].astype(BF16), b_ref[...].astype(BF16), dims)
    if nk == 1:
        finish(part)
        return
    k = pl.program_id(2)

    @pl.when(k == 0)
    def _():
        acc_ref[...] = part

    @pl.when(k > 0)
    def _():
        acc_ref[...] += part

    @pl.when(k == nk - 1)
    def _():
        finish(acc_ref[...])


def _mm_blocks(a, b, dims, grid, a_spec, b_spec, o_spec, out_shape, acc_shape, name):
    nk = grid[2]

    def body(a_ref, b_ref, o_ref, *scratch):
        def finish(acc):
            o_ref[...] = acc.astype(o_ref.dtype)

        _mm_steps(a_ref, b_ref, dims, nk, scratch[0] if nk > 1 else None, finish)

    return pl.pallas_call(
        body, grid=grid, in_specs=[a_spec, b_spec], out_specs=o_spec, out_shape=out_shape,
        scratch_shapes=[pltpu.VMEM(acc_shape, F32)] if nk > 1 else [], name=name,
        compiler_params=pltpu.CompilerParams(dimension_semantics=("parallel", "parallel", "arbitrary")),
    )(a, b)


_UP_SHARD = 2 * FFN // 4


def _up_fwd(h2, wup4, name, tm=2048):
    S = h2.shape[0]
    tm = min(tm, S)
    return _mm_blocks(h2, wup4, _NN, (S // tm, 4, 1), pl.BlockSpec((tm, D), lambda i, j, k: (i, 0)),
                      pl.BlockSpec((None, D, _UP_SHARD), lambda i, j, k: (j, 0, 0)), pl.BlockSpec((tm, _UP_SHARD), lambda i, j, k: (i, j)),
                      _sds((S, 2 * FFN)), (tm, _UP_SHARD), name)


def _up_dx(dup2, wup4, name, tm=2048, tn=512):
    S = dup2.shape[1]
    tm = min(tm, S)
    return _mm_blocks(dup2, wup4, _NT, (S // tm, D // tn, 4), pl.BlockSpec((None, tm, _UP_SHARD), lambda i, j, k: (lax.div(k, 2), i, lax.rem(k, 2))),
                      pl.BlockSpec((None, tn, _UP_SHARD), lambda i, j, k: (k, j, 0)), pl.BlockSpec((tm, tn), lambda i, j, k: (i, j)),
                      _sds((S, D)), (tm, tn), name)


def _up_dw(h2, dup2, name, tm=1024, tk=4096):
    S = h2.shape[0]
    tk = min(tk, S)
    return _mm_blocks(h2, dup2, _TN, (D // tm, 4, S // tk), pl.BlockSpec((tk, tm), lambda i, j, k: (k, i)),
                      pl.BlockSpec((None, tk, _UP_SHARD), lambda i, j, k: (lax.div(j, 2), k, lax.rem(j, 2))),
                      pl.BlockSpec((None, tm, _UP_SHARD), lambda i, j, k: (j, i, 0)), _sds((4, D, _UP_SHARD)), (tm, _UP_SHARD), name)


def _row_spec(tm, n):
    return pl.BlockSpec((tm, n), lambda i: (i, 0))


def _vec_spec(n, rows=1):
    return pl.BlockSpec((rows, n), lambda i: (0, 0))


def _ln_mod(x, nw, sc, sh, name, tm=256):
    S = x.shape[0]

    def body(x_ref, nw_ref, sc_ref, sh_ref, o_ref):
        xv = x_ref[...]
        r = lax.rsqrt(jnp.mean(xv * xv, -1, keepdims=True) + EPS)
        o_ref[...] = ((xv * r * nw_ref[...]) * (1.0 + sc_ref[...]) + sh_ref[...]).astype(BF16)

    return pl.pallas_call(
        body, grid=(S // tm,), in_specs=[_row_spec(tm, D)] + [_vec_spec(D)] * 3, out_specs=_row_spec(tm, D),
        out_shape=_sds((S, D), BF16), name=name, compiler_params=_PAR)(x, nw, sc, sh)


def _ln_mod_bwd(x, dh, dres, nw, sc, name, tm=256):
    S = x.shape[0]

    def body(x_ref, dh_ref, dres_ref, nw_ref, sc_ref, dx_ref, st_ref):
        @pl.when(pl.program_id(0) == 0)
        def _():
            st_ref[...] = jnp.zeros_like(st_ref)

        xv, dhv, nwv = x_ref[...], dh_ref[...], nw_ref[...]
        r = lax.rsqrt(jnp.mean(xv * xv, -1, keepdims=True) + EPS)
        xhat = xv * r
        dn = dhv * (1.0 + sc_ref[...])
        g = dn * nwv
        dx_ref[...] = dres_ref[...] + r * (g - xhat * jnp.mean(g * xhat, -1, keepdims=True))
        st_ref[0:1, :] += jnp.sum(dhv, 0, keepdims=True)
        st_ref[1:2, :] += jnp.sum(dhv * (xhat * nwv), 0, keepdims=True)
        st_ref[2:3, :] += jnp.sum(dn * xhat, 0, keepdims=True)

    return pl.pallas_call(
        body, grid=(S // tm,), in_specs=[_row_spec(tm, D)] * 3 + [_vec_spec(D)] * 2,
        out_specs=(_row_spec(tm, D), _vec_spec(D, 8)), out_shape=(_sds((S, D)), _sds((8, D))),
        name=name, compiler_params=_ARB)(x, dh, dres, nw, sc)


def _gate_bwd(dx, out, g, name, tm=256):
    S = dx.shape[0]

    def body(dx_ref, o_ref, g_ref, dz_ref, dg_ref):
        @pl.when(pl.program_id(0) == 0)
        def _():
            dg_ref[...] = jnp.zeros_like(dg_ref)

        dxv = dx_ref[...]
        dz_ref[...] = (dxv * g_ref[...]).astype(BF16)
        dg_ref[0:1, :] += jnp.sum(dxv * o_ref[...], 0, keepdims=True)

    return pl.pallas_call(
        body, grid=(S // tm,), in_specs=[_row_spec(tm, D)] * 2 + [_vec_spec(D)],
        out_specs=(_row_spec(tm, D), _vec_spec(D, 8)), out_shape=(_sds((S, D), BF16), _sds((8, D))),
        name=name, compiler_params=_ARB)(dx, out, g)


def _loss_grad(y, t, tm=256):
    S = y.shape[0]

    def body(y_ref, t_ref, dy_ref, l_ref):
        @pl.when(pl.program_id(0) == 0)
        def _():
            l_ref[...] = jnp.zeros_like(l_ref)

        e = y_ref[...] - t_ref[...]
        dy_ref[...] = e * (1.0 / D)
        l_ref[...] += 0.5 * jnp.sum(jnp.mean(e * e, -1, keepdims=True), 0, keepdims=True)

    return pl.pallas_call(
        body, grid=(S // tm,), in_specs=[_row_spec(tm, D)] * 2,
        out_specs=(_row_spec(tm, D), pl.BlockSpec((8, 128), lambda i: (0, 0))),
        out_shape=(_sds((S, D)), _sds((8, 128))), name="loss_grad", compiler_params=_ARB)(y, t)


_CONV_ROWS = 256
_HALO = 8


def _rows(ref, lo, hi):
    S, c = ref.shape
    parts = [jnp.zeros((-lo, c), F32)] if lo < 0 else []
    parts.append(ref[max(lo, 0):min(hi, S), :])
    if hi > S:
        parts.append(jnp.zeros((hi - S, c), F32))
    return parts[0] if len(parts) == 1 else jnp.concatenate(parts, 0)


def _conv_rows(ext, w, b, n):
    K = w.shape[0]
    acc = ext[_HALO:_HALO + n] * w[K - 1:K, :] + b
    for j in range(1, K):
        acc = acc + pltpu.roll(ext, j, 0)[_HALO:_HALO + n] * w[K - 1 - j:K - j, :]
    return acc


def _conv_rows_bwd(ext, w, dc, n):
    K, m = w.shape[0], dc.shape[0]
    d0 = dc[0:n]
    dx = d0 * w[K - 1:K, :]
    dws = [jnp.sum(d0 * ext[_HALO:_HALO + n], 0, keepdims=True)]
    for j in range(1, K):
        dx = dx + pltpu.roll(dc, m - j, 0)[0:n] * w[K - 1 - j:K - j, :]
        dws.append(jnp.sum(d0 * pltpu.roll(ext, j, 0)[_HALO:_HALO + n], 0, keepdims=True))
    return dx, dws[::-1], jnp.sum(d0, 0, keepdims=True)


def _col_spec(S, tc, off=0):
    return pl.BlockSpec((S, tc), lambda j: (0, j + off))


def _ssd_pre(proj, cw, cb, tc=256):
    S, n = proj.shape[0], SSD_CONV_DIM
    R = min(_CONV_ROWS, S)

    def body(x_ref, w_ref, b_ref, o_ref):
        wv, bv = w_ref[...], b_ref[...]
        for r0 in range(0, S, R):
            o_ref[r0:r0 + R, :] = _silu(_conv_rows(_rows(x_ref, r0 - _HALO, r0 + R), wv, bv, R))

    return pl.pallas_call(
        body, grid=(n // tc,),
        in_specs=[_col_spec(S, tc, OX // tc), pl.BlockSpec((4, tc), lambda j: (0, j)), pl.BlockSpec((1, tc), lambda j: (0, j))],
        out_specs=_col_spec(S, tc), out_shape=_sds((S, n)), name="ssd_pre", compiler_params=_PAR)(proj, cw, cb)


def _ssd_pre_bwd(proj, dxc, cw, cb, tc=256):
    S, n = proj.shape[0], SSD_CONV_DIM
    R = min(_CONV_ROWS, S)

    def body(x_ref, d_ref, w_ref, b_ref, dx_ref, dw_ref, db_ref):
        wv, bv = w_ref[...], b_ref[...]
        acc = [jnp.zeros((1, tc), F32)] * 5
        for r0 in range(0, S, R):
            ext = _rows(x_ref, r0 - _HALO, r0 + R + _HALO)
            dc = _rows(d_ref, r0, r0 + R + _HALO) * _dsilu(_conv_rows(ext, wv, bv, R + _HALO))
            dx, dws, db = _conv_rows_bwd(ext, wv, dc, R)
            dx_ref[r0:r0 + R, :] = dx.astype(BF16)
            acc = [s + d for s, d in zip(acc, dws + [db])]
        for k in range(4):
            dw_ref[k:k + 1, :] = acc[k]
        db_ref[...] = acc[4]

    wspec, bspec = pl.BlockSpec((4, tc), lambda j: (0, j)), pl.BlockSpec((1, tc), lambda j: (0, j))
    return pl.pallas_call(
        body, grid=(n // tc,), in_specs=[_col_spec(S, tc, OX // tc), _col_spec(S, tc), wspec, bspec],
        out_specs=(_col_spec(S, tc), wspec, bspec), out_shape=(_sds((S, n), BF16), _sds((4, n)), _sds((1, n))),
        name="ssd_pre_bwd", compiler_params=_PAR)(proj, dxc, cw, cb)


def _ffn_act(up, cw, cb, tc=256):
    S, nb = up.shape[0], FFN // tc
    R = min(_CONV_ROWS, S)

    def body(g_ref, v_ref, wg_ref, wv_ref, bg_ref, bv_ref, o_ref):
        wg, wv, bg, bv = wg_ref[...], wv_ref[...], bg_ref[...], bv_ref[...]
        for r0 in range(0, S, R):
            cg = _conv_rows(_rows(g_ref, r0 - _HALO, r0 + R), wg, bg, R)
            cv = _conv_rows(_rows(v_ref, r0 - _HALO, r0 + R), wv, bv, R)
            o_ref[r0:r0 + R, :] = (_silu(cg) * cv).astype(BF16)

    def wspec(off):
        return pl.BlockSpec((3, tc), lambda j: (0, j + off))

    def bspec(off):
        return pl.BlockSpec((1, tc), lambda j: (0, j + off))

    return pl.pallas_call(
        body, grid=(nb,), in_specs=[_col_spec(S, tc), _col_spec(S, tc, nb), wspec(0), wspec(nb), bspec(0), bspec(nb)],
        out_specs=_col_spec(S, tc), out_shape=_sds((S, FFN), BF16), name="ffn_act", compiler_params=_PAR)(up, up, cw, cw, cb, cb)


def _ffn_act_bwd(up, dact, cw, cb, tc=256):
    S, nb = up.shape[0], FFN // tc
    R = min(_CONV_ROWS, S)

    def body(g_ref, v_ref, d_ref, wg_ref, wv_ref, bg_ref, bv_ref, dx_ref, dw_ref, db_ref):
        wg, wv, bg, bv = wg_ref[...], wv_ref[...], bg_ref[...], bv_ref[...]
        acc = [[jnp.zeros((1, tc), F32)] * 4, [jnp.zeros((1, tc), F32)] * 4]
        for r0 in range(0, S, R):
            eg, ev = _rows(g_ref, r0 - _HALO, r0 + R + _HALO), _rows(v_ref, r0 - _HALO, r0 + R + _HALO)
            da = _rows(d_ref, r0, r0 + R + _HALO)
            cg, cv = _conv_rows(eg, wg, bg, R + _HALO), _conv_rows(ev, wv, bv, R + _HALO)
            sg = _sigmoid(cg)
            for half, (ext, w, dc) in enumerate(((eg, wg, da * cv * (sg * (1.0 + cg * (1.0 - sg)))), (ev, wv, da * (cg * sg)))):
                dx, dws, db = _conv_rows_bwd(ext, w, dc, R)
                dx_ref[half, r0:r0 + R, :] = dx.astype(BF16)
                acc[half] = [s + d for s, d in zip(acc[half], dws + [db])]
        for half in range(2):
            for k in range(3):
                dw_ref[half, k:k + 1, :] = acc[half][k]
            db_ref[half] = acc[half][3]

    def wspec(off):
        return pl.BlockSpec((3, tc), lambda j: (0, j + off))

    def bspec(off):
        return pl.BlockSpec((1, tc), lambda j: (0, j + off))

    cs = _col_spec(S, tc)
    both = lambda r: pl.BlockSpec((2, r, tc), lambda j: (0, 0, j))
    return pl.pallas_call(
        body, grid=(nb,), in_specs=[cs, _col_spec(S, tc, nb), cs, wspec(0), wspec(nb), bspec(0), bspec(nb)],
        out_specs=(both(S), both(3), both(1)), out_shape=(_sds((2, S, FFN), BF16), _sds((2, 3, FFN)), _sds((2, 1, FFN))),
        name="ffn_act_bwd", compiler_params=_PAR)(up, up, dact, cw, cw, cb, cb)


def _window_sum(x, w, up=False):
    shift = _shift_up if up else _shift_down
    j = 1
    while j < w:
        x = x + shift(x, j)
        j *= 2
    return x


def _pool_fwd(proj, pool_w, pool_scale):
    S = proj.shape[0]

    def body(u_ref, w_ref, s_ref, o_ref):
        cnt_row = (_iota((S, 128), 0) + 1).astype(F32)
        for g, w in enumerate(POOL_WINDOWS):
            sl = slice(g * 128, (g + 1) * 128)
            u = u_ref[:, sl]
            pooled = _window_sum(u, w) / jnp.minimum(cnt_row, float(w)) - u
            mixed = _dot(pooled.astype(BF16), w_ref[g].astype(BF16), _NN)
            o_ref[:, sl] = (mixed * s_ref[:, sl]).astype(BF16)

    return pl.pallas_call(
        body, grid=(1,),
        in_specs=[pl.BlockSpec((S, 512), lambda i: (0, OP // 512)), pl.BlockSpec((4, 128, 128), lambda i: (0, 0, 0)), _vec_spec(512)],
        out_specs=pl.BlockSpec((S, 512), lambda i: (0, 0)), out_shape=_sds((S, 512), BF16), name="pool_fwd",
        compiler_params=_ARB)(proj, pool_w, pool_scale)


def _pool_bwd(proj, dob, pool_w, pool_scale):
    S = proj.shape[0]

    def body(u_ref, d_ref, w_ref, s_ref, du_ref, dw_ref, ds_ref):
        cnt_row = (_iota((S, 128), 0) + 1).astype(F32)
        for g, w in enumerate(POOL_WINDOWS):
            sl = slice(g * 128, (g + 1) * 128)
            u, dv, wv = u_ref[:, sl], d_ref[:, sl], w_ref[g].astype(BF16)
            cnt = jnp.minimum(cnt_row, float(w))
            pooled = (_window_sum(u, w) / cnt - u).astype(BF16)
            ds_ref[:, sl] = jnp.sum(dv * _dot(pooled, wv, _NN), 0, keepdims=True)
            dmix = (dv * s_ref[:, sl]).astype(BF16)
            dw_ref[g] = _dot(pooled, dmix, _TN)
            dp = _dot(dmix, wv, _NT)
            du_ref[:, sl] = (_window_sum(dp / cnt, w, up=True) - dp).astype(BF16)

    blk = pl.BlockSpec((S, 512), lambda i: (0, 0))
    wspec = pl.BlockSpec((4, 128, 128), lambda i: (0, 0, 0))
    return pl.pallas_call(
        body, grid=(1,), in_specs=[pl.BlockSpec((S, 512), lambda i: (0, OP // 512)), blk, wspec, _vec_spec(512)],
        out_specs=(blk, wspec, _vec_spec(512)), out_shape=(_sds((S, 512), BF16), _sds((4, 128, 128)), _sds((1, 512))),
        name="pool_bwd", compiler_params=_ARB)(proj, dob, pool_w, pool_scale)


def _branch_specs():
    return [pl.BlockSpec((512, D), lambda i: (0, 0)), pl.BlockSpec((512, D), lambda i: (1, 0)), pl.BlockSpec((1024, D), lambda i: (1, 0))]


def _merge_fwd(oa, ob, oc, proj, wbr, tm=256):
    S = oa.shape[0]

    def body(oa_ref, ob_ref, oc_ref, gl_ref, wa_ref, wb_ref, wc_ref, o_ref):
        acc = _sigmoid(gl_ref[:, 0:D]) * _dot(oa_ref[...], wa_ref[...], _NN)
        acc += _sigmoid(gl_ref[:, D:2 * D]) * _dot(ob_ref[...], wb_ref[...], _NN)
        acc += _sigmoid(gl_ref[:, 2 * D:3 * D]) * _dot(oc_ref[...], wc_ref[...], _NN)
        o_ref[...] = acc.astype(BF16)

    return pl.pallas_call(
        body, grid=(S // tm,),
        in_specs=[_row_spec(tm, 512), _row_spec(tm, 512), _row_spec(tm, D), _row_spec(tm, 3 * D)] + _branch_specs(),
        out_specs=_row_spec(tm, D), out_shape=_sds((S, D), BF16), name="merge_fwd", compiler_params=_PAR)(oa, ob, oc, proj, wbr, wbr, wbr)


def _merge_bwd(dm, oa, ob, oc, proj, wbr, tm=256):
    S = oa.shape[0]

    def body(dm_ref, oa_ref, ob_ref, oc_ref, gl_ref, wa_ref, wb_ref, wc_ref, dya_ref, dyb_ref, dyc_ref, dgl_ref, doa_ref, dob_ref, doc_ref):
        dmv = dm_ref[...]
        for i, (o_ref, w_ref, dy_ref, do_ref) in enumerate(
                ((oa_ref, wa_ref, dya_ref, doa_ref), (ob_ref, wb_ref, dyb_ref, dob_ref), (oc_ref, wc_ref, dyc_ref, doc_ref))):
            gt = _sigmoid(gl_ref[:, i * D:(i + 1) * D])
            wv = w_ref[...]
            yv = _dot(o_ref[...], wv, _NN)
            dy = (dmv * gt).astype(BF16)
            dy_ref[...] = dy
            dgl_ref[:, i * D:(i + 1) * D] = (dmv * yv * gt * (1.0 - gt)).astype(BF16)
            do_ref[...] = _dot(dy, wv, _NT)

    rs = _row_spec
    return pl.pallas_call(
        body, grid=(S // tm,),
        in_specs=[rs(tm, D), rs(tm, 512), rs(tm, 512), rs(tm, D), rs(tm, 3 * D)] + _branch_specs(),
        out_specs=(rs(tm, D), rs(tm, D), rs(tm, D), rs(tm, 3 * D), rs(tm, 512), rs(tm, 512), rs(tm, D)),
        out_shape=(_sds((S, D), BF16), _sds((S, D), BF16), _sds((S, D), BF16), _sds((S, 3 * D), BF16), _sds((S, 512)), _sds((S, 512)), _sds((S, D))),
        name="merge_bwd", compiler_params=_PAR)(dm, oa, ob, oc, proj, wbr, wbr, wbr)


def _rope_tab(posb, invf, tm=256):
    S = posb.shape[0]

    def body(p_ref, f_ref, c_ref, s1_ref, s2_ref):
        ang = p_ref[...] * f_ref[...]
        lane = _iota(ang.shape, 1)
        cs, sn = jnp.cos(ang), jnp.sin(ang)
        c_ref[...] = jnp.where(lane < NOPE, 1.0, cs)
        s1_ref[...] = jnp.where((lane >= 64) & (lane < 80), -sn, 0.0)
        s2_ref[...] = jnp.where((lane >= 80) & (lane < 96), sn, 0.0)

    rs = _row_spec(tm, 128)
    return pl.pallas_call(body, grid=(S // tm,), in_specs=[rs, _vec_spec(128)], out_specs=(rs, rs, rs),
                          out_shape=(_sds((S, 128)),) * 3, name="rope_tab", compiler_params=_PAR)(posb, invf)


def _rope(u, C, S1, S2):
    return u * C + pltpu.roll(u, 112, 1) * S1 + pltpu.roll(u, 16, 1) * S2


def _rope_t(dy, C, S1, S2):
    return dy * C + pltpu.roll(dy * S1, 16, 1) + pltpu.roll(dy * S2, 112, 1)


def _seg_sum(v, mask):
    return jnp.sum(jnp.where(mask, v, 0.0), -1, keepdims=True)


def _mla_latents(pq_ref, pkv_ref, wqb_ref, wkvb_ref, qan_ref, kvan_ref):
    ql, kvl = pq_ref[...], pkv_ref[...]
    ckv, kr = kvl[:, 0:KV_RANK], kvl[:, KV_RANK:KV_RANK + 128]
    rq = lax.rsqrt(jnp.mean(ql * ql, -1, keepdims=True) + EPS)
    rkv = lax.rsqrt(jnp.mean(ckv * ckv, -1, keepdims=True) + EPS)
    nq = (ql * rq * qan_ref[...]).astype(BF16)
    nkv = (ckv * rkv * kvan_ref[...]).astype(BF16)
    kv = jnp.concatenate([_dot(nkv, wkvb_ref[k], _NN) for k in range(4)], 1)
    return ql, ckv, kr, rq, rkv, nq, nkv, _dot(nq, wqb_ref[...], _NN), kv


def _mla_specs(tm):
    full = lambda r, n: pl.BlockSpec((r, n), lambda i: (0, 0))
    return ([pl.BlockSpec((tm, 384), lambda i: (i, OQ // 384)), pl.BlockSpec((tm, 384), lambda i: (i, OKV // 384))],
            [full(Q_RANK, D), pl.BlockSpec((4, KV_RANK, 256), lambda i: (0, 0, 0)), _vec_spec(Q_RANK), _vec_spec(KV_RANK), _vec_spec(128), _vec_spec(128)]
            + [_row_spec(tm, 128)] * 3)


def _mla_prep(proj, wqb, wkvb, qan, kvan, wq, wk, rope, tm=256):
    S = proj.shape[0]

    def body(pq_ref, pkv_ref, wqb_ref, wkvb_ref, qan_ref, kvan_ref, wq_ref, wk_ref, c_ref, s1_ref, s2_ref, qf_ref, kf_ref, ve_ref):
        _, _, kr, _, _, _, _, q, kv = _mla_latents(pq_ref, pkv_ref, wqb_ref, wkvb_ref, qan_ref, kvan_ref)
        C, S1, S2, wqv, wkv = c_ref[...], s1_ref[...], s2_ref[...], wq_ref[...], wk_ref[...]
        lane = _iota((tm, 128), 1)
        mn, mr = lane < 64, (lane >= 64) & (lane < 96)
        rrk = lax.rsqrt(_seg_sum(kr * kr, mr) / ROPE_DIM + EPS)
        ykr = _rope(jnp.where(mr, kr * rrk * wkv, 0.0), C, S1, S2)
        for h in range(N_HEADS):
            sl = slice(h * 128, (h + 1) * 128)
            t = q[:, sl]
            rn = lax.rsqrt(_seg_sum(t * t, mn) / NOPE + EPS)
            rr = lax.rsqrt(_seg_sum(t * t, mr) / ROPE_DIM + EPS)
            qf_ref[:, sl] = _rope(t * jnp.where(mn, rn, jnp.where(mr, rr, 0.0)) * wqv, C, S1, S2).astype(BF16)
            t = kv[:, sl]
            rn = lax.rsqrt(_seg_sum(t * t, mn) / NOPE + EPS)
            kf_ref[:, sl] = (jnp.where(mn, t * rn * wkv, 0.0) + ykr).astype(BF16)
            ve_ref[:, sl] = (jnp.where(mn, pltpu.roll(t, 64, 1), 0.0) if h % 2 == 0 else jnp.where(mn, 0.0, t)).astype(BF16)

    pspecs, wspecs = _mla_specs(tm)
    rs = _row_spec(tm, D)
    return pl.pallas_call(body, grid=(S // tm,), in_specs=pspecs + wspecs, out_specs=(rs, rs, rs),
                          out_shape=(_sds((S, D), BF16),) * 3, name="mla_prep", compiler_params=_PAR)(
        proj, proj, wqb, wkvb, qan, kvan, wq, wk, *rope)


def _mla_prep_bwd(proj, dqf, dkf, dve, wqb, wkvb, qan, kvan, wq, wk, rope, tm=256):
    S = proj.shape[0]

    def body(pq_ref, pkv_ref, wqb_ref, wkvb_ref, qan_ref, kvan_ref, wq_ref, wk_ref, c_ref, s1_ref, s2_ref,
             dqf_ref, dkf_ref, dve_ref, dqs_ref, dkvs_ref, dwqb_ref, dwkvb_ref, st_ref, dq_scr, dkv_scr):
        @pl.when(pl.program_id(0) == 0)
        def _():
            dwqb_ref[...] = jnp.zeros_like(dwqb_ref)
            dwkvb_ref[...] = jnp.zeros_like(dwkvb_ref)
            st_ref[...] = jnp.zeros_like(st_ref)

        ql, ckv, kr, rq, rkv, nq, nkv, q, kv = _mla_latents(pq_ref, pkv_ref, wqb_ref, wkvb_ref, qan_ref, kvan_ref)
        C, S1, S2, wqv, wkv = c_ref[...], s1_ref[...], s2_ref[...], wq_ref[...], wk_ref[...]
        lane = _iota((tm, 128), 1)
        mn, mr = lane < 64, (lane >= 64) & (lane < 96)
        dwq = jnp.zeros((1, 128), F32)
        dwk = jnp.zeros((1, 128), F32)
        dykr = jnp.zeros((tm, 128), F32)
        for h in range(N_HEADS):
            sl = slice(h * 128, (h + 1) * 128)
            t = q[:, sl]
            rn = lax.rsqrt(_seg_sum(t * t, mn) / NOPE + EPS)
            rr = lax.rsqrt(_seg_sum(t * t, mr) / ROPE_DIM + EPS)
            scale = jnp.where(mn, rn, jnp.where(mr, rr, 0.0))
            that = t * scale
            du = _rope_t(dqf_ref[:, sl], C, S1, S2)
            dwq += jnp.sum(du * that, 0, keepdims=True)
            g = du * wqv
            gt = g * that
            dq_scr[:, sl] = scale * (g - that * jnp.where(mn, _seg_sum(gt, mn) / NOPE, _seg_sum(gt, mr) / ROPE_DIM))
            t = kv[:, sl]
            rn = lax.rsqrt(_seg_sum(t * t, mn) / NOPE + EPS)
            that = jnp.where(mn, t * rn, 0.0)
            dkf = dkf_ref[:, sl]
            dykr += jnp.where(mr, dkf, 0.0)
            dkn = jnp.where(mn, dkf, 0.0)
            dwk += jnp.sum(dkn * that, 0, keepdims=True)
            g = dkn * wkv
            dve = dve_ref[:, sl]
            dkv_scr[:, sl] = jnp.where(mn, rn * (g - that * (jnp.sum(g * that, -1, keepdims=True) / NOPE)),
                                       pltpu.roll(dve, 64, 1) if h % 2 == 0 else dve)
        rrk = lax.rsqrt(_seg_sum(kr * kr, mr) / ROPE_DIM + EPS)
        that = jnp.where(mr, kr * rrk, 0.0)
        dukr = jnp.where(mr, _rope_t(dykr, C, S1, S2), 0.0)
        dwk += jnp.sum(dukr * that, 0, keepdims=True)
        g = dukr * wkv
        dkr = rrk * (g - that * (jnp.sum(g * that, -1, keepdims=True) / ROPE_DIM))
        dqv, dkvv = dq_scr[...].astype(BF16), dkv_scr[...].astype(BF16)
        dnq = _dot(dqv, wqb_ref[...], _NT)
        dwqb_ref[...] += _dot(nq, dqv, _TN)
        dnkv = jnp.zeros((tm, KV_RANK), F32)
        for k in range(4):
            dnkv += _dot(dkvv[:, k * 256:(k + 1) * 256], wkvb_ref[k], _NT)
            dwkvb_ref[k] += _dot(nkv, dkvv[:, k * 256:(k + 1) * 256], _TN)
        xhat = ql * rq
        st_ref[0:1, 0:Q_RANK] += jnp.sum(dnq * xhat, 0, keepdims=True)
        g = dnq * qan_ref[...]
        dqs_ref[...] = (rq * (g - xhat * jnp.mean(g * xhat, -1, keepdims=True))).astype(BF16)
        xhat = ckv * rkv
        st_ref[1:2, 0:KV_RANK] += jnp.sum(dnkv * xhat, 0, keepdims=True)
        g = dnkv * kvan_ref[...]
        dkvs_ref[:, 0:KV_RANK] = (rkv * (g - xhat * jnp.mean(g * xhat, -1, keepdims=True))).astype(BF16)
        dkvs_ref[:, KV_RANK:KV_RANK + 128] = dkr.astype(BF16)
        st_ref[2:3, 0:128] += dwq
        st_ref[3:4, 0:128] += dwk

    pspecs, wspecs = _mla_specs(tm)
    rs = _row_spec(tm, D)
    full = lambda r, n: pl.BlockSpec((r, n), lambda i: (0, 0))
    return pl.pallas_call(
        body, grid=(S // tm,), in_specs=pspecs + wspecs + [rs, rs, rs],
        out_specs=(_row_spec(tm, 384), _row_spec(tm, 384), full(Q_RANK, D), pl.BlockSpec((4, KV_RANK, 256), lambda i: (0, 0, 0)), full(8, D)),
        out_shape=(_sds((S, 384), BF16), _sds((S, 384), BF16), _sds((Q_RANK, D)), _sds((4, KV_RANK, 256)), _sds((8, D))),
        scratch_shapes=[pltpu.VMEM((tm, D), F32), pltpu.VMEM((tm, D), F32)], name="mla_prep_bwd", compiler_params=_ARB)(
        proj, proj, wqb, wkvb, qan, kvan, wq, wk, *rope, dqf, dkf, dve)


_ATT_SCALE = (NOPE + ROPE_DIM) ** -0.5


def _att_probs(q, k, i, tq):
    n = k.shape[0]
    s = _dot(q, k, _NT) * _ATT_SCALE
    tri = _iota((tq, tq), 1) <= _iota((tq, tq), 0)
    diag = jnp.where(tri, s[:, n - tq:], -1e30)
    s = diag if n == tq else jnp.concatenate([s[:, :n - tq], diag], 1)
    p = jnp.exp(s - jnp.max(s, -1, keepdims=True))
    return p * (1.0 / jnp.sum(p, -1, keepdims=True))


def _attn_fwd(qf, kf, ve, tq=256):
    S = qf.shape[0]

    def body(q_ref, k_ref, v_ref, o_ref):
        for i in range(S // tq):
            n, rows = (i + 1) * tq, slice(i * tq, (i + 1) * tq)
            acc = jnp.zeros((tq, 128), F32)
            for hh in range(2):
                sl = slice(hh * 128, (hh + 1) * 128)
                p = _att_probs(q_ref[rows, sl], k_ref[0:n, sl], i, tq)
                acc += _dot(p.astype(BF16), v_ref[0:n, sl], _NN)
            o_ref[rows, :] = acc.astype(BF16)

    ps = pl.BlockSpec((S, 256), lambda h: (0, h))
    return pl.pallas_call(body, grid=(N_HEADS // 2,), in_specs=[ps, ps, ps], out_specs=pl.BlockSpec((S, 128), lambda h: (0, h)),
                          out_shape=_sds((S, 512), BF16), name="attn_fwd", compiler_params=_PAR)(qf, kf, ve)


def _attn_bwd(qf, kf, ve, do, tq=256):
    S = qf.shape[0]

    def body(q_ref, k_ref, v_ref, do_ref, dq_ref, dk_ref, dv_ref):
        dk_ref[...] = jnp.zeros_like(dk_ref)
        dv_ref[...] = jnp.zeros_like(dv_ref)
        for i in range(S // tq):
            n, rows = (i + 1) * tq, slice(i * tq, (i + 1) * tq)
            dob = do_ref[rows, :].astype(BF16)
            for hh in range(2):
                sl = slice(hh * 128, (hh + 1) * 128)
                q, k = q_ref[rows, sl], k_ref[0:n, sl]
                p = _att_probs(q, k, i, tq)
                dv_ref[0:n, sl] += _dot(p.astype(BF16), dob, _TN)
                dp = _dot(dob, v_ref[0:n, sl], _NT)
                ds = (p * (dp - jnp.sum(dp * p, -1, keepdims=True)) * _ATT_SCALE).astype(BF16)
                dq_ref[rows, sl] = _dot(ds, k, _NN)
                dk_ref[0:n, sl] += _dot(ds, q, _TN)

    ps = pl.BlockSpec((S, 256), lambda h: (0, h))
    return pl.pallas_call(body, grid=(N_HEADS // 2,), in_specs=[ps, ps, ps, pl.BlockSpec((S, 128), lambda h: (0, h))], out_specs=(ps, ps, ps),
                          out_shape=(_sds((S, D)),) * 3, name="attn_bwd", compiler_params=_PAR)(qf, kf, ve, do)


def _softplus(x):
    return jnp.maximum(x, 0.0) + jnp.log1p(jnp.exp(-jnp.abs(x)))


def _ssd_chunk(xc_ref, dtr_ref, dtb_ref, al_ref, e_ref):
    L = SSD_L
    a = -jnp.exp(al_ref[...])
    dtp = _softplus(dtr_ref[...] + dtb_ref[...])
    causal = _iota((L, L), 1) <= _iota((L, L), 0)
    cs = _dot_hi(causal.astype(F32), dtp * a)
    E = e_ref[...]
    dtx, csx = _dot_hi(dtp, E), _dot_hi(cs, E)
    X = xc_ref[:, 0:SSD_INNER]
    Xd = X * dtx
    dec_out = jnp.exp(csx)
    dec_st = jnp.exp(csx[L - 1:L, :] - csx)
    return a, dtp, causal, cs, cs.T, dtx, X, Xd, dec_out, dec_st


def _ssd_decay(causal, cs, cs_row, h):
    diff = cs[:, h:h + 1] - cs_row[h:h + 1, :]
    return jnp.where(causal, jnp.exp(jnp.where(causal, diff, 0.0)), 0.0)


def _ssd_groups(xc_ref, g):
    b0, c0 = SSD_INNER + g * SSD_N, SSD_INNER + 2 * SSD_N + g * SSD_N
    return xc_ref[:, b0:b0 + SSD_N].astype(BF16), xc_ref[:, c0:c0 + SSD_N].astype(BF16)


def _pair_decay(cs, pair):
    L = SSD_L
    return jnp.where(_iota((128, 128), 0) < 64, jnp.exp(cs[L - 1:L, 2 * pair:2 * pair + 1]), jnp.exp(cs[L - 1:L, 2 * pair + 1:2 * pair + 2]))


def _ssd_in_specs(nc, rev):
    idx = (lambda c: nc - 1 - c) if rev else (lambda c: c)
    return [pl.BlockSpec((SSD_L, SSD_CONV_DIM), lambda c: (idx(c), 0)), pl.BlockSpec((SSD_L, 128), lambda c: (idx(c), ODT // 128)),
            _vec_spec(128), _vec_spec(128), _vec_spec(SSD_INNER), pl.BlockSpec((128, SSD_INNER), lambda c: (0, 0))]


def _ssd_core(xc, proj, dtb, alog, dskip, E):
    S = xc.shape[0]
    nc = S // SSD_L

    def body(xc_ref, dtr_ref, dtb_ref, al_ref, dx_ref, e_ref, y_ref, hp_ref, h_scr):
        @pl.when(pl.program_id(0) == 0)
        def _():
            h_scr[...] = jnp.zeros_like(h_scr)

        hp_ref[0] = h_scr[...]
        _, _, causal, cs, cs_row, _, X, Xd, dec_out, dec_st = _ssd_chunk(xc_ref, dtr_ref, dtb_ref, al_ref, e_ref)
        Xs = Xd * dec_st
        lane = _iota((SSD_L, 128), 1)
        for g in range(2):
            Bg, Cg = _ssd_groups(xc_ref, g)
            CB = _dot(Cg, Bg, _NT)
            for pr in range(4):
                pair = g * 4 + pr
                psl = slice(pair * 128, (pair + 1) * 128)
                Xdp = Xd[:, psl].astype(BF16)
                r0 = _dot((CB * _ssd_decay(causal, cs, cs_row, 2 * pair)).astype(BF16), Xdp, _NN)
                r1 = _dot((CB * _ssd_decay(causal, cs, cs_row, 2 * pair + 1)).astype(BF16), Xdp, _NN)
                Hp = h_scr[psl, :]
                W = _dot(Cg, Hp.astype(BF16), _NT)
                y_ref[:, psl] = jnp.where(lane < 64, r0, r1) + W * dec_out[:, psl] + X[:, psl] * dx_ref[:, psl]
                h_scr[psl, :] = Hp * _pair_decay(cs, pair) + _dot(Xs[:, psl].astype(BF16), Bg, _TN)

    return pl.pallas_call(
        body, grid=(nc,), in_specs=_ssd_in_specs(nc, False),
        out_specs=(pl.BlockSpec((SSD_L, SSD_INNER), lambda c: (c, 0)), pl.BlockSpec((1, SSD_INNER, SSD_N), lambda c: (c, 0, 0))),
        out_shape=(_sds((S, SSD_INNER)), _sds((nc, SSD_INNER, SSD_N))), scratch_shapes=[pltpu.VMEM((SSD_INNER, SSD_N), F32)],
        name="ssd_core", compiler_params=_ARB)(xc, proj, dtb, alog, dskip, E)


def _ssd_core_bwd(xc, proj, hprev, dy, dtb, alog, dskip, E, ET):
    S = xc.shape[0]
    nc = S // SSD_L
    L = SSD_L

    def body(xc_ref, dtr_ref, dtb_ref, al_ref, dx_ref, e_ref, et_ref, hp_ref, dy_ref, dxc_ref, ddt_ref, st_ref, dh_scr, acc_scr):
        step = pl.program_id(0)

        @pl.when(step == 0)
        def _():
            dh_scr[...] = jnp.zeros_like(dh_scr)
            acc_scr[...] = jnp.zeros_like(acc_scr)
            st_ref[...] = jnp.zeros_like(st_ref)

        a, dtp, causal, cs, cs_row, dtx, X, Xd, dec_out, dec_st = _ssd_chunk(xc_ref, dtr_ref, dtb_ref, al_ref, e_ref)
        lane = _iota((L, 128), 1)
        sub = _iota((128, L), 0)
        dcs_col = jnp.zeros((L, 128), F32)
        dcs_row = jnp.zeros((128, L), F32)
        dcs_last = jnp.zeros((1, 128), F32)
        dcsx, ddtx, dlastx = [], [], []
        for g in range(2):
            Bg, Cg = _ssd_groups(xc_ref, g)
            CB = _dot(Cg, Bg, _NT)
            dCB = jnp.zeros((L, L), F32)
            dB = jnp.zeros((L, SSD_N), F32)
            dC = jnp.zeros((L, SSD_N), F32)
            for pr in range(4):
                pair = g * 4 + pr
                psl = slice(pair * 128, (pair + 1) * 128)
                dY, Xp, Xdp, dop, dsp = dy_ref[:, psl], X[:, psl], Xd[:, psl], dec_out[:, psl], dec_st[:, psl]
                Xdb = Xdp.astype(BF16)
                acc_scr[0:1, psl] += jnp.sum(dY * Xp, 0, keepdims=True)
                Hp = hp_ref[0, psl, :]
                Hb = Hp.astype(BF16)
                dW = (dY * dop).astype(BF16)
                dcx = dY * _dot(Cg, Hb, _NT) * dop
                dC += _dot(dW, Hb, _NN)
                dHp = _dot(dW, Cg, _TN)
                dHn = dh_scr[psl, :]
                cd = _pair_decay(cs, pair)
                dh_scr[psl, :] = dHp + dHn * cd
                rsum = jnp.sum(dHn * Hp * cd, -1, keepdims=True)
                half = _iota((128, 1), 0) < 64
                s0 = jnp.sum(jnp.where(half, rsum, 0.0), 0, keepdims=True)
                s1 = jnp.sum(jnp.where(half, 0.0, rsum), 0, keepdims=True)
                lane1 = _iota((1, 128), 1)
                dcs_last += jnp.where(lane1 == 2 * pair, s0, 0.0) + jnp.where(lane1 == 2 * pair + 1, s1, 0.0)
                dHb = dHn.astype(BF16)
                dXs = _dot(Bg, dHb, _NT)
                dB += _dot((Xdp * dsp).astype(BF16), dHb, _NN)
                dXd = dXs * dsp
                e_st = dXs * Xdp * dsp
                dcx -= e_st
                dlastx.append(jnp.sum(e_st, 0, keepdims=True))
                for i in range(2):
                    h = 2 * pair + i
                    Dm = _ssd_decay(causal, cs, cs_row, h)
                    M = CB * Dm
                    dYm = jnp.where((lane < 64) if i == 0 else (lane >= 64), dY, 0.0).astype(BF16)
                    dM = _dot(dYm, Xdb, _NT)
                    dXd += _dot(M.astype(BF16), dYm, _TN)
                    dCB += dM * Dm
                    Em = dM * M
                    dcs_col += jnp.where(lane == h, jnp.sum(Em, -1, keepdims=True), 0.0)
                    dcs_row += jnp.where(sub == h, jnp.sum(Em, 0, keepdims=True), 0.0)
                dxc_ref[:, psl] = dY * dx_ref[:, psl] + dXd * dtx[:, psl]
                ddtx.append(dXd * Xp)
                dcsx.append(dcx)
            dCBb = dCB.astype(BF16)
            b0, c0 = SSD_INNER + g * SSD_N, SSD_INNER + 2 * SSD_N + g * SSD_N
            dxc_ref[:, b0:b0 + SSD_N] = dB + _dot(dCBb, Cg, _TN)
            dxc_ref[:, c0:c0 + SSD_N] = dC + _dot(dCBb, Bg, _NN)
        ET = et_ref[...]
        dcs = dcs_col - dcs_row.T + _dot_hi(jnp.concatenate(dcsx, 1), ET)
        dlast = dcs_last + _dot_hi(jnp.broadcast_to(jnp.concatenate(dlastx, 1), (8, SSD_INNER)), ET)[0:1, :]
        dcs += jnp.where(_iota((L, 128), 0) == L - 1, dlast, 0.0)
        dda = _dot_hi((_iota((L, L), 1) >= _iota((L, L), 0)).astype(F32), dcs)
        ddtp = dda * a + _dot_hi(jnp.concatenate(ddtx, 1), ET)
        draw = ddtp * _sigmoid(dtr_ref[...] + dtb_ref[...])
        ddt_ref[...] = draw.astype(BF16)
        st_ref[0:1, :] += jnp.sum(draw, 0, keepdims=True)
        st_ref[1:2, :] += jnp.sum(dda * dtp, 0, keepdims=True) * a

        @pl.when(step == nc - 1)
        def _():
            st_ref[2:3, :] = _dot_hi(acc_scr[...], ET)[0:1, :]

    rev = lambda c: (nc - 1 - c, 0)
    return pl.pallas_call(
        body, grid=(nc,),
        in_specs=_ssd_in_specs(nc, True) + [pl.BlockSpec((SSD_INNER, 128), lambda c: (0, 0)),
                                            pl.BlockSpec((1, SSD_INNER, SSD_N), lambda c: (nc - 1 - c, 0, 0)),
                                            pl.BlockSpec((L, SSD_INNER), rev)],
        out_specs=(pl.BlockSpec((L, SSD_CONV_DIM), rev), pl.BlockSpec((L, 128), rev), pl.BlockSpec((8, 128), lambda c: (0, 0))),
        out_shape=(_sds((S, SSD_CONV_DIM)), _sds((S, 128), BF16), _sds((8, 128))),
        scratch_shapes=[pltpu.VMEM((SSD_INNER, SSD_N), F32), pltpu.VMEM((8, SSD_INNER), F32)],
        name="ssd_core_bwd", compiler_params=_ARB)(xc, proj, dtb, alog, dskip, E, ET, hprev, dy)


def _ssd_post(y, proj, nw, tm=256):
    S = y.shape[0]

    def body(y_ref, z_ref, nw_ref, o_ref):
        for g in range(2):
            sl = slice(g * 512, (g + 1) * 512)
            gated = y_ref[:, sl] * _silu(z_ref[:, sl])
            r = lax.rsqrt(jnp.mean(gated * gated, -1, keepdims=True) + EPS)
            o_ref[:, sl] = (gated * r * nw_ref[:, sl]).astype(BF16)

    return pl.pallas_call(
        body, grid=(S // tm,), in_specs=[_row_spec(tm, D), pl.BlockSpec((tm, D), lambda i: (i, OZ // D)), _vec_spec(D)],
        out_specs=_row_spec(tm, D), out_shape=_sds((S, D), BF16), name="ssd_post", compiler_params=_PAR)(y, proj, nw)


def _ssd_post_bwd(doc, y, proj, nw, tm=256):
    S = y.shape[0]

    def body(d_ref, y_ref, z_ref, nw_ref, dy_ref, dz_ref, st_ref):
        @pl.when(pl.program_id(0) == 0)
        def _():
            st_ref[...] = jnp.zeros_like(st_ref)

        for g in range(2):
            sl = slice(g * 512, (g + 1) * 512)
            yv, zv, dv = y_ref[:, sl], z_ref[:, sl], d_ref[:, sl]
            sz = _silu(zv)
            gated = yv * sz
            r = lax.rsqrt(jnp.mean(gated * gated, -1, keepdims=True) + EPS)
            ghat = gated * r
            st_ref[0:1, sl] += jnp.sum(dv * ghat, 0, keepdims=True)
            gg = dv * nw_ref[:, sl]
            dg = r * (gg - ghat * jnp.mean(gg * ghat, -1, keepdims=True))
            dy_ref[:, sl] = dg * sz
            dz_ref[:, sl] = (dg * yv * _dsilu(zv)).astype(BF16)

    zs = pl.BlockSpec((tm, D), lambda i: (i, OZ // D))
    return pl.pallas_call(
        body, grid=(S // tm,), in_specs=[_row_spec(tm, D), _row_spec(tm, D), zs, _vec_spec(D)],
        out_specs=(_row_spec(tm, D), _row_spec(tm, D), _vec_spec(D, 8)), out_shape=(_sds((S, D)), _sds((S, D), BF16), _sds((8, D))),
        name="ssd_post_bwd", compiler_params=_ARB)(doc, y, proj, nw)


def _row(v, n=None):
    v = v.astype(F32).reshape(1, -1)
    return v if n is None else jnp.pad(v, ((0, 0), (0, n - v.shape[1])))


_IN_SEGMENTS = [(0, 384, OQ), (384, 640, OKV), (640, 672, OKV + 320), (672, 1184, OP), (1184, 2208, OZ), (2208, 3744, OX),
                (3744, 3760, ODT), (3760, IN_DIM, OG)]
_IN_ZEROS = [(OKV + 256, OKV + 320), (OKV + 352, OKV + 384), (ODT + 16, ODT + 128)]


def _in_pieces():
    w, out = IN_DIM // 4, []
    for a, b, d in _IN_SEGMENTS:
        while a < b:
            k = a // w
            e = min(b, (k + 1) * w)
            out.append((k, a - k * w, e - k * w, d))
            d, a = d + e - a, e
    return out


def _win_layout(w_in4, tm=256):
    def body(w_ref, o_ref):
        for k, s0, s1, d in _in_pieces():
            o_ref[:, d:d + s1 - s0] = w_ref[k, :, s0:s1]
        for z0, z1 in _IN_ZEROS:
            o_ref[:, z0:z1] = jnp.zeros((tm, z1 - z0), o_ref.dtype)

    return pl.pallas_call(
        body, grid=(D // tm,), in_specs=[pl.BlockSpec((4, tm, IN_DIM // 4), lambda i: (0, i, 0))],
        out_specs=pl.BlockSpec((tm, IN_PAD), lambda i: (i, 0)), out_shape=_sds((D, IN_PAD), w_in4.dtype), name="win_layout",
        compiler_params=_PAR)(w_in4)


def _win_unlayout(dwin, tm=256):
    def body(d_ref, o_ref):
        for k, s0, s1, d in _in_pieces():
            o_ref[k, :, s0:s1] = d_ref[:, d:d + s1 - s0]

    return pl.pallas_call(
        body, grid=(D // tm,), in_specs=[pl.BlockSpec((tm, IN_PAD), lambda i: (i, 0))],
        out_specs=pl.BlockSpec((4, tm, IN_DIM // 4), lambda i: (0, i, 0)), out_shape=_sds((4, D, IN_DIM // 4), dwin.dtype),
        name="win_unlayout", compiler_params=_PAR)(dwin)


def _prep_late(p):
    return dict(wbr=p["w_branch"].astype(BF16), wo=p["w_out"].astype(BF16), wup4=p["ffn_up4"].astype(BF16), wdn=p["ffn_down"].astype(BF16))


def _prep_layer(p, late=None):
    win = _win_layout(p["w_in4"].astype(BF16))
    wqb = jnp.transpose(p["w_q_b4"].astype(BF16).reshape(4, Q_RANK, 2, 96), (1, 0, 2, 3))
    return dict(
        win=win, wqb=jnp.pad(wqb, ((0, 0), (0, 0), (0, 0), (0, 32))).reshape(Q_RANK, D), wkvb4=p["w_kv_b4"].astype(BF16),
        late=late if late is not None else (lambda after: _prep_late(p)),
        nw1=_row(p["norm1_w"]), nw2=_row(p["norm2_w"]), qan=_row(p["q_a_norm"]), kvan=_row(p["kv_a_norm"]),
        wq=_row(p["q_norm"], 128), wk=_row(p["k_norm"], 128), pool_w=p["pool_w"].astype(F32), pool_scale=_row(p["pool_scale"]),
        cw=p["ssd_conv_w"].astype(F32), cb=_row(p["ssd_conv_b"]), dtb=_row(p["ssd_dt_bias"], 128), alog=_row(p["ssd_a_log"], 128),
        dskip=_row(jnp.repeat(p["ssd_d"].astype(F32), SSD_P)), snw=_row(p["ssd_norm_w"]),
        fcw=p["ffn_conv_w"].astype(F32), fcb=_row(p["ffn_conv_b"]))


def _layer_fwd(x, mod8, W, rope, E, tag):
    sh1, sc1, g1, sh2, sc2, g2 = (mod8[i:i + 1] for i in range(6))
    h1 = _ln_mod(x, W["nw1"], sc1, sh1, name=f"ln1_{tag}")
    proj = _mm(h1, W["win"], tn=640, tk=1024, name=f"proj_{tag}")
    qf, kf, ve = _mla_prep(proj, W["wqb"], W["wkvb4"], W["qan"], W["kvan"], W["wq"], W["wk"], rope)
    oa = _attn_fwd(qf, kf, ve)
    ob = _pool_fwd(proj, W["pool_w"], W["pool_scale"])
    xc = _ssd_pre(proj, W["cw"], W["cb"])
    y, hprev = _ssd_core(xc, proj, W["dtb"], W["alog"], W["dskip"], E)
    oc = _ssd_post(y, proj, W["snw"])
    W.update(W["late"](oc))
    merged = _merge_fwd(oa, ob, oc, proj, W["wbr"])
    x1, out1 = _mm(merged, W["wo"], tk=1024, res=x, gate=g1, name=f"wout_{tag}")
    h2 = _ln_mod(x1, W["nw2"], sc2, sh2, name=f"ln2_{tag}")
    up = _up_fwd(h2, W["wup4"], name=f"up_{tag}")
    act = _ffn_act(up, W["fcw"], W["fcb"])
    x2, out2 = _mm(act, W["wdn"], tm=1024, res=x1, gate=g2, name=f"down_{tag}")
    saved = dict(x=x, h1=h1, proj=proj, qf=qf, kf=kf, ve=ve, oa=oa, ob=ob, oc=oc, xc=xc, hprev=hprev, y=y, merged=merged,
                 out1=out1, x1=x1, h2=h2, up=up, act=act, out2=out2)
    return x2, saved


def _layer_bwd(dx2, sv, mod8, W, rope, E, ET, tag, emit=None, mid=None):
    sc1, g1, sc2, g2 = mod8[1:2], mod8[2:3], mod8[4:5], mod8[5:6]
    proj = sv["proj"]
    dz2, dg2 = _gate_bwd(dx2, sv["out2"], g2, name=f"gate2_bwd_{tag}")
    dact = _mm(dz2, W["wdn"], "nt", tn=1408, tk=1024, name=f"down_dx_{tag}")
    dwdn = _mm(sv["act"], dz2, "tn", tm=1408, name=f"down_dw_{tag}")
    dup2, dfcw, dfcb = _ffn_act_bwd(sv["up"], dact, W["fcw"], W["fcb"])
    dh2 = _up_dx(dup2, W["wup4"], name=f"up_dx_{tag}")
    dwup4 = _up_dw(sv["h2"], dup2, name=f"up_dw_{tag}")
    dx1, st2 = _ln_mod_bwd(sv["x1"], dh2, dx2, W["nw2"], sc2, name=f"ln2_bwd_{tag}")
    dz1, dg1 = _gate_bwd(dx1, sv["out1"], g1, name=f"gate1_bwd_{tag}")
    dmerged = _mm(dz1, W["wo"], "nt", tk=1024, name=f"wout_dx_{tag}")
    dwo = _mm(sv["merged"], dz1, "tn", name=f"wout_dw_{tag}")
    dya, dyb, dyc, dgl, doa, dob, doc = _merge_bwd(dmerged, sv["oa"], sv["ob"], sv["oc"], proj, W["wbr"])
    dwba = _mm(sv["oa"], dya, "tn", name=f"wba_dw_{tag}")
    dwbb = _mm(sv["ob"], dyb, "tn", name=f"wbb_dw_{tag}")
    dwbc = _mm(sv["oc"], dyc, "tn", name=f"wbc_dw_{tag}")
    late = dict(w_branch=jnp.concatenate([dwba, dwbb, dwbc], 0).reshape(4, 512, D), w_out=dwo.reshape(4, 256, D), ffn_up=dwup4,
                ffn_down=dwdn.reshape(4, FFN // 4, D))
    snw = W["snw"]
    if emit is not None:
        token = emit(late)
        if token is not None:
            snw, doa = snw + token, doa + token
    dy, dzs, st_post = _ssd_post_bwd(doc, sv["y"], proj, snw)
    dxc, ddt, st_ssd = _ssd_core_bwd(sv["xc"], proj, sv["hprev"], dy, W["dtb"], W["alog"], W["dskip"], E, ET)
    dxbc, dcw, dcb = _ssd_pre_bwd(proj, dxc, W["cw"], W["cb"])
    if mid is not None:
        token = mid(dxbc)
        if token is not None:
            doa, dob = doa + token, dob + token
    dpool, dpw, dps = _pool_bwd(proj, dob, W["pool_w"], W["pool_scale"])
    dqf, dkf, dve = _attn_bwd(sv["qf"], sv["kf"], sv["ve"], doa)
    dqs, dkvs, dwqb, dwkvb4, st_mla = _mla_prep_bwd(proj, dqf, dkf, dve, W["wqb"], W["wkvb4"], W["qan"], W["kvan"], W["wq"], W["wk"], rope)
    dproj = jnp.concatenate([dgl, dzs, dxbc, dpool, dqs, dkvs, ddt], 1)
    dh1 = _mm(dproj, W["win"], "nt", tk=1408, name=f"proj_dx_{tag}")
    dwin = _mm(sv["h1"], dproj, "tn", tn=640, name=f"proj_dw_{tag}")
    dx, st1 = _ln_mod_bwd(sv["x"], dh1, dx1, W["nw1"], sc1, name=f"ln1_bwd_{tag}")
    grads = dict(
        norm1_w=st1[2], norm2_w=st2[2], w_in=_win_unlayout(dwin),
        q_a_norm=st_mla[0, :Q_RANK], kv_a_norm=st_mla[1, :KV_RANK], q_norm=st_mla[2, :96], k_norm=st_mla[3, :96],
        w_q_b=jnp.transpose(dwqb.reshape(Q_RANK, 4, 2, 128)[:, :, :, :96], (1, 0, 2, 3)).reshape(4, Q_RANK, 192), w_kv_b=dwkvb4,
        pool_w=dpw, pool_scale=dps[0], ssd_conv_w=dcw, ssd_conv_b=dcb[0],
        ssd_dt_bias=st_ssd[0, :SSD_HEADS], ssd_a_log=st_ssd[1, :SSD_HEADS], ssd_d=st_ssd[2, :SSD_HEADS], ssd_norm_w=st_post[0],
        ffn_conv_w=jnp.transpose(dfcw, (1, 0, 2)).reshape(3, 2 * FFN), ffn_conv_b=dfcb.reshape(2 * FFN), **late)
    dmod = jnp.concatenate([st1[0:2], dg1[0:1], st2[0:2], dg2[0:1]], 0)
    return dx, grads, dmod


def _ssd_expand():
    E = (jnp.arange(SSD_INNER)[None, :] // SSD_P == jnp.arange(128)[:, None]).astype(F32)
    return E, E.T


def _rope_tables(positions):
    inv_freq = ROPE_THETA ** (-jnp.arange(0, ROPE_DIM, 2, dtype=F32) / ROPE_DIM)
    invf = jnp.concatenate([jnp.zeros((NOPE,), F32), inv_freq, inv_freq, jnp.zeros((32,), F32)]).reshape(1, 128)
    posb = jnp.broadcast_to(positions.astype(F32)[:, None], (positions.shape[0], 128))
    return _rope_tab(posb, invf)


def _local_step(x, target, positions, mods, get_layer, bwd_mod=None, emit=None, mid=None, done=None):
    rope = _rope_tables(positions)
    E, ET = _ssd_expand()
    Ws, saved, h = [], [], x
    for l in range(2):
        Ws.append(_prep_layer(*get_layer(l, h)))
        h, sv = _layer_fwd(h, mods[l], Ws[l], rope, E, l)
        saved.append(sv)
    dy, lpart = _loss_grad(h, target)
    grads, dmods = [None, None], [None, None]
    for l in (1, 0):
        mod8 = mods[l] if bwd_mod is None else bwd_mod(l)
        hook = lambda f: None if f is None else functools.partial(f, l)
        dy, grads[l], dmods[l] = _layer_bwd(dy, saved[l], mod8, Ws[l], rope, E, ET, l, hook(emit), hook(mid))
        if done is not None:
            token = done(l, grads[l])
            if token is not None:
                dy = dy + token
    return lpart[0, 0], dy, grads, dmods


_ANY = pl.BlockSpec(memory_space=pl.ANY)
_VMEM = pl.BlockSpec(memory_space=pltpu.VMEM)


def _place():
    x, y, c = lax.axis_index("x"), lax.axis_index("y"), lax.axis_index("c")
    return x, y, c, [(1 - x, y), (x, 1 - y), (1 - x, 1 - y)]


def _allgather8(v, name):
    m_per, n = v.shape

    def body(x_ref, out_ref, send_sems, recv_sems, local_sem):
        x, y, c, chips = _place()
        me, sibling = (x, y, c), (x, y, 1 - c)

        def rows(px, py, pc):
            return out_ref.at[pl.ds((4 * px + 2 * py + pc) * m_per, m_per), :]

        def copy(k, block, to, src=None):
            return pltpu.make_async_remote_copy(src_ref=rows(*block) if src is None else src, dst_ref=rows(*block),
                                                send_sem=send_sems.at[k], recv_sem=recv_sems.at[k], device_id=to, device_id_type=MESH)

        mine = pltpu.make_async_copy(x_ref, rows(*me), local_sem)
        mine.start()
        first = [copy(0, me, sibling, src=x_ref)] + [copy(1 + j, me, (*chip, c), src=x_ref) for j, chip in enumerate(chips)]
        for cp in first:
            cp.start()
        passed = [copy(4 + j, (*chip, c), sibling) for j, chip in enumerate(chips)]
        for j, chip in enumerate(chips):
            copy(1 + j, (*chip, c), me).wait_recv()
            passed[j].start()
        copy(0, sibling, me).wait_recv()
        for j, chip in enumerate(chips):
            copy(4 + j, (*chip, 1 - c), me).wait_recv()
        for cp in first + passed:
            cp.wait_send()
        mine.wait()

    return pl.pallas_call(
        body, out_shape=_sds((8 * m_per, n), v.dtype), in_specs=[_VMEM], out_specs=_VMEM,
        scratch_shapes=[pltpu.SemaphoreType.DMA((7,)), pltpu.SemaphoreType.DMA((7,)), pltpu.SemaphoreType.DMA], name=name)(v)


def _sems(n):
    return [pltpu.SemaphoreType.DMA((n,)), pltpu.SemaphoreType.DMA((n,))]


_HBM = pl.BlockSpec(memory_space=pltpu.HBM)
_SEM = pl.BlockSpec(memory_space=pltpu.SEMAPHORE)
_EFFECT = pltpu.CompilerParams(has_side_effects=pltpu.SideEffectType.DATAFLOW_SIDE_EFFECTING)


def _ici_copy(src_refs, land_refs, send_sems, recv_sems, a, j, slices, incoming):
    x, y, c, chips = _place()
    if _peers(slices) == 1:
        src, dst = slices(src_refs[a], land_refs[a], 1 - c if incoming else c)
        return pltpu.make_async_remote_copy(src_ref=src, dst_ref=dst, send_sem=send_sems.at[a], recv_sem=recv_sems.at[a],
                                            device_id=(x, y, 1 - c), device_id_type=MESH)
    me, other = 2 * x + y, 2 * chips[j][0] + chips[j][1]
    src, dst = slices(src_refs[a], land_refs[a], other, me, c) if incoming else slices(src_refs[a], land_refs[a], me, other, c)
    return pltpu.make_async_remote_copy(src_ref=src, dst_ref=dst, send_sem=send_sems.at[3 * a + j], recv_sem=recv_sems.at[3 * a + j],
                                        device_id=(*chips[j], c), device_id_type=MESH)


def _peers(slices):
    return 1 if slices is _halves_slices else 3


def _ici_start(srcs, land_shapes, slices, after, name):
    na = len(srcs)

    def body(*refs):
        src_refs, land_refs, send_sems, recv_sems = refs[:na], refs[na:2 * na], refs[2 * na + 1], refs[2 * na + 2]
        for a in range(na):
            for j in range(_peers(slices)):
                _ici_copy(src_refs, land_refs, send_sems, recv_sems, a, j, slices, False).start()
        refs[-1][...] = jnp.zeros_like(refs[-1])

    hbm = lambda v: pltpu.with_memory_space_constraint(v, pltpu.HBM)
    lands = [hbm(lax.empty(s.shape, s.dtype)) for s in land_shapes]
    return pl.pallas_call(
        body, name=name,
        out_shape=(pltpu.SemaphoreType.DMA((_peers(slices) * na,)), pltpu.SemaphoreType.DMA((_peers(slices) * na,)),
                   *[pltpu.HBM(v.shape, v.dtype) for v in srcs],
                   *[pltpu.HBM(s.shape, s.dtype) for s in land_shapes], _sds((8, 128))),
        in_specs=[_HBM] * (2 * na) + [_ANY], out_specs=(_SEM, _SEM, *[_HBM] * (2 * na), _VMEM),
        input_output_aliases={i: 2 + i for i in range(2 * na)}, compiler_params=_EFFECT)(*[hbm(v) for v in srcs], *lands, after)


def _ici_wait(handle, slices, after, name):
    na = (len(handle) - 3) // 2

    def body(*refs):
        src_refs, land_refs, send_sems, recv_sems = refs[:na], refs[na:2 * na], refs[2 * na], refs[2 * na + 1]
        for a in range(na):
            for j in range(_peers(slices)):
                _ici_copy(src_refs, land_refs, send_sems, recv_sems, a, j, slices, False).wait_send()
                _ici_copy(src_refs, land_refs, send_sems, recv_sems, a, j, slices, True).wait_recv()

    thru = handle[2:2 + 2 * na]
    outs = pl.pallas_call(
        body, name=name, out_shape=[pltpu.HBM(v.shape, v.dtype) for v in thru], in_specs=[_HBM] * (2 * na) + [_SEM, _SEM, _ANY],
        out_specs=[_HBM] * (2 * na), input_output_aliases={i: i for i in range(2 * na)}, compiler_params=_EFFECT)(
        *thru, handle[0], handle[1], after)
    return outs[:na], outs[na:]


def _gather_slices(p_ref, land_ref, sender, receiver, c):
    r2 = p_ref.shape[0] // 2
    return p_ref.at[pl.ds(c * r2, r2), :], land_ref.at[sender, pl.ds(c * r2, r2), :]


def _scatter_slices(a_ref, t_ref, sender, receiver, c):
    return a_ref.at[receiver], t_ref.at[sender]


def _halves_slices(g_ref, land_ref, sender_c):
    r2 = g_ref.shape[1] // 2
    return g_ref.at[:, pl.ds((1 - sender_c) * r2, r2), :], land_ref


def _gather_start(arrs, after, name):
    return _ici_start(arrs, [_sds((4,) + v.shape, v.dtype) for v in arrs], _gather_slices, after, name)


def _gather_finish(handle, after, name):
    arrs, stacks = _ici_wait(handle, _gather_slices, after, name + "_wait")
    na = len(stacks)

    def body(*refs):
        s_refs, o_refs, (send_sems, recv_sems) = refs[:na], refs[na:2 * na], refs[2 * na:]
        x, y, c, chips = _place()

        def copy(a, j, cc, to):
            r2 = s_refs[a].shape[1] // 2
            at = (2 * chips[j][0] + chips[j][1], pl.ds(cc * r2, r2), slice(None))
            return pltpu.make_async_remote_copy(src_ref=s_refs[a].at[at], dst_ref=o_refs[a].at[at], send_sem=send_sems.at[3 * a + j],
                                                recv_sem=recv_sems.at[3 * a + j], device_id=to, device_id_type=MESH)

        passed = [copy(a, j, c, (x, y, 1 - c)) for a in range(na) for j in range(3)]
        for cp in passed:
            cp.start()
        for a in range(na):
            for j in range(3):
                copy(a, j, 1 - c, (x, y, c)).wait_recv()
        for cp in passed:
            cp.wait_send()

    stacks = pl.pallas_call(
        body, out_shape=[_sds(v.shape, v.dtype) for v in stacks], in_specs=[_ANY] * na, out_specs=[_ANY] * na,
        input_output_aliases={i: i for i in range(na)}, scratch_shapes=_sems(3 * na), name=name + "_pass")(*stacks)
    chip = 2 * lax.axis_index("x") + lax.axis_index("y")
    return [lax.dynamic_update_slice(s, v[None], (chip, 0, 0)) for s, v in zip(stacks, arrs)]


def _halves_start(gs, after, tag):
    return _ici_start(gs, [_sds((4, v.shape[1] // 2, v.shape[2]), v.dtype) for v in gs], _halves_slices, after, f"rs_halves_{tag}_start")


def _join_halves(fs, name):
    na = len(fs)

    def body(*refs):
        f_refs, o_refs, (send_sems, recv_sems) = refs[:na], refs[na:2 * na], refs[2 * na:]
        x, y, c, _ = _place()

        def copy(a, cc, to):
            r2 = f_refs[a].shape[0] // 2
            return pltpu.make_async_remote_copy(src_ref=f_refs[a].at[pl.ds(cc * r2, r2), :], dst_ref=o_refs[a].at[pl.ds(cc * r2, r2), :],
                                                send_sem=send_sems.at[a], recv_sem=recv_sems.at[a], device_id=to, device_id_type=MESH)

        cps = [copy(a, c, (x, y, 1 - c)) for a in range(na)]
        for cp in cps:
            cp.start()
        for a in range(na):
            copy(a, 1 - c, (x, y, c)).wait_recv()
        for cp in cps:
            cp.wait_send()

    return pl.pallas_call(
        body, out_shape=[_sds(v.shape, v.dtype) for v in fs], in_specs=[_ANY] * na, out_specs=[_ANY] * na,
        input_output_aliases={i: i for i in range(na)}, scratch_shapes=_sems(na), name=name)(*fs)


def _sum_chips(a, t, chip, ci, name):
    _, r2, n = t.shape
    tm = _row_tile(r2)
    nb = r2 // tm

    def body(k_ref, a_ref, t1_ref, t2_ref, t3_ref, o_ref):
        o_ref[...] = ((a_ref[...].astype(F32) + t1_ref[...].astype(F32)) + t2_ref[...].astype(F32)) + t3_ref[...].astype(F32)

    def slot(j):
        return pl.BlockSpec((None, tm, n), lambda i, k_ref: (lax.rem(k_ref[0] + j, 4), i, 0))

    return pl.pallas_call(
        body, grid_spec=pltpu.PrefetchScalarGridSpec(num_scalar_prefetch=1, grid=(nb,), in_specs=[slot(0), slot(1), slot(2), slot(3)],
                                                     out_specs=pl.BlockSpec((tm, n), lambda i, k_ref: (k_ref[1] * nb + i, 0))),
        out_shape=_sds((2 * r2, n)), name=name, compiler_params=_PAR)(jnp.stack([chip, ci]).astype(jnp.int32), a, t, t, t)


def _add_cast(g, recv, c, name):
    _, r2, n = recv.shape

    def body(c_ref, a_ref, b_ref, o_ref):
        o_ref[...] = (a_ref[...] + b_ref[...]).astype(BF16)

    spec = pl.BlockSpec((None, r2, n), lambda k, c_ref: (k, 0, 0))
    return pl.pallas_call(
        body, grid_spec=pltpu.PrefetchScalarGridSpec(
            num_scalar_prefetch=1, grid=(4,), in_specs=[pl.BlockSpec((None, r2, n), lambda k, c_ref: (k, c_ref[0], 0)), spec], out_specs=spec),
        out_shape=_sds(recv.shape, BF16), name=name, compiler_params=_PAR)(c.reshape(1).astype(jnp.int32), g, recv)


def _sum_lead(t, name, tm=256):
    P, R, n = t.shape
    tm = _row_tile(R, tm)

    def body(t_ref, o_ref):
        acc = t_ref[0].astype(F32)
        for j in range(1, P):
            acc = acc + t_ref[j].astype(F32)
        o_ref[...] = acc

    return pl.pallas_call(body, grid=(R // tm,), in_specs=[pl.BlockSpec((P, tm, n), lambda i: (0, i, 0))],
                          out_specs=pl.BlockSpec((tm, n), lambda i: (i, 0)), out_shape=_sds((R, n)), name=name, compiler_params=_PAR)(t)


def _ada_fwd(c16, ada_w, ada_b_cols, tn=512):
    L, _, n = ada_w.shape

    def body(c_ref, w_ref, b_ref, o_ref):
        o_ref[0] = _dot(_silu(c_ref[...]).astype(BF16), w_ref[0].astype(BF16), _NN) + b_ref[0]

    return pl.pallas_call(
        body, grid=(L, n // tn),
        in_specs=[pl.BlockSpec((16, D), lambda l, j: (0, 0)), pl.BlockSpec((1, D, tn), lambda l, j: (l, 0, j)), pl.BlockSpec((1, 1, tn), lambda l, j: (l, 0, j))],
        out_specs=pl.BlockSpec((1, 16, tn), lambda l, j: (l, 0, j)), out_shape=_sds((L, 16, n)), name="ada_fwd",
        compiler_params=pltpu.CompilerParams(dimension_semantics=("parallel", "parallel")))(c16, ada_w, ada_b_cols)


def _ada_bwd(c16, dmod, tn=512):
    L, _, n = dmod.shape

    def body(c_ref, d_ref, o_ref):
        o_ref[0] = _dot(_silu(c_ref[...]).astype(BF16), d_ref[0].astype(BF16), _TN)

    return pl.pallas_call(
        body, grid=(L, n // tn), in_specs=[pl.BlockSpec((16, D), lambda l, j: (0, 0)), pl.BlockSpec((1, 16, tn), lambda l, j: (l, 0, j))],
        out_specs=pl.BlockSpec((1, D, tn), lambda l, j: (l, 0, j)), out_shape=_sds((L, D, n)), name="ada_bwd",
        compiler_params=pltpu.CompilerParams(dimension_semantics=("parallel", "parallel")))(c16, dmod)


def _adam_math(w, g, m, v):
    mn = ADAM_B1 * m + (1.0 - ADAM_B1) * g
    vn = ADAM_B2 * v + (1.0 - ADAM_B2) * (g * g)
    m_hat = mn / (1.0 - ADAM_B1 ** ADAM_STEP)
    v_hat = vn / (1.0 - ADAM_B2 ** ADAM_STEP)
    return -ADAM_LR * (m_hat / (jnp.sqrt(v_hat) + ADAM_EPS) + ADAM_WD * w), mn, vn


def _adamw(w, g, m, v, name):
    R, n = w.shape
    tm = _row_tile(R)

    def body(w_ref, g_ref, m_ref, v_ref, d_ref, nm_ref, nv_ref):
        d_ref[...], nm_ref[...], nv_ref[...] = _adam_math(w_ref[...], g_ref[...], m_ref[...], v_ref[...])

    spec = pl.BlockSpec((tm, n), lambda i: (i, 0))
    return pl.pallas_call(body, grid=(R // tm,), in_specs=[spec] * 4, out_specs=(spec,) * 3, out_shape=(_sds((R, n)),) * 3,
                          name=name, compiler_params=_PAR)(w, g, m, v)


def _adamw_cols(w, g0, g1, m, v, name):
    fwd, back = (lambda t: jnp.transpose(t, (2, 0, 1))), (lambda t: jnp.transpose(t, (1, 2, 0)))
    gt = jnp.stack([g0.T, g1.T], 1)
    n, _, r = gt.shape
    tr = max(t for t in range(1, 257) if n % t == 0)

    def body(w_ref, g_ref, m_ref, v_ref, d_ref, nm_ref, nv_ref):
        d_ref[...], nm_ref[...], nv_ref[...] = _adam_math(w_ref[...], g_ref[...], m_ref[...], v_ref[...])

    spec = pl.BlockSpec((tr, 2, r), lambda i: (i, 0, 0))
    outs = pl.pallas_call(body, grid=(n // tr,), in_specs=[spec] * 4, out_specs=(spec,) * 3, out_shape=(_sds(gt.shape),) * 3,
                          name=name, compiler_params=_PAR)(fwd(w), gt, fwd(m), fwd(v))
    return (back(gt), *[back(o) for o in outs])


def _adamw_layers(w, g0, g1, m, v, after, name):
    _, r, n = w.shape
    tm = _row_tile(r)
    nb = r // tm

    def body(w_ref, g0_ref, g1_ref, m_ref, v_ref, after_ref, g_ref, d_ref, nm_ref, nv_ref):
        gv = jnp.where(pl.program_id(0) == 0, g0_ref[...], g1_ref[...])
        g_ref[...] = gv
        d_ref[...], nm_ref[...], nv_ref[...] = _adam_math(w_ref[...], gv, m_ref[...], v_ref[...])

    spec = pl.BlockSpec((None, tm, n), lambda l, i: (l, i, 0))
    g0_spec = pl.BlockSpec((tm, n), lambda l, i: (i * (1 - l) + (nb - 1) * l, 0))
    g1_spec = pl.BlockSpec((tm, n), lambda l, i: (i * l, 0))
    return pl.pallas_call(body, grid=(2, nb), in_specs=[spec, g0_spec, g1_spec, spec, spec, _ANY], out_specs=(spec,) * 4,
                          out_shape=(_sds(w.shape),) * 4, name=name,
                          compiler_params=pltpu.CompilerParams(dimension_semantics=("arbitrary", "arbitrary")))(w, g0, g1, m, v, after)


_W_NAMES = ["ada_w", "ada_b", "norm1_w", "w_in", "q_a_norm", "w_q_b", "kv_a_norm", "w_kv_b", "q_norm", "k_norm", "pool_w",
            "pool_scale", "ssd_conv_w", "ssd_conv_b", "ssd_dt_bias", "ssd_a_log", "ssd_d", "ssd_norm_w", "w_branch", "w_out",
            "norm2_w", "ffn_up", "ffn_conv_w", "ffn_conv_b", "ffn_down"]
_BIG = [("w_in", (D, IN_DIM // 4), 1), ("w_q_b", (Q_RANK, 192), 1), ("w_kv_b", (KV_RANK, 256), 1), ("w_branch", (512, D), 0),
        ("w_out", (256, D), 0), ("ffn_up", (D, 2 * FFN // 4), 1), ("ffn_down", (FFN // 4, D), 0)]

_SMALL = [("norm1_w", (D,)), ("q_a_norm", (Q_RANK,)), ("kv_a_norm", (KV_RANK,)), ("q_norm", (96,)), ("k_norm", (96,)),
          ("pool_w", (4, 128, 128)), ("pool_scale", (512,)), ("ssd_conv_w", (4, SSD_CONV_DIM)), ("ssd_conv_b", (SSD_CONV_DIM,)),
          ("ssd_dt_bias", (SSD_HEADS,)), ("ssd_a_log", (SSD_HEADS,)), ("ssd_d", (SSD_HEADS,)), ("ssd_norm_w", (D,)), ("norm2_w", (D,)),
          ("ffn_conv_w", (3, 2 * FFN)), ("ffn_conv_b", (2 * FFN,))]
_CONV_SHARDED = {"ssd_conv_w": SSD_CONV_DIM // 4, "ffn_conv_w": 2 * FFN // 4}


def _pack_flat(arrs, mult):
    flat = jnp.concatenate([a.astype(F32).reshape(-1) for a in arrs])
    rows = -(-flat.shape[0] // (128 * mult)) * mult
    return jnp.pad(flat, (0, rows * 128 - flat.shape[0])).reshape(rows, 128), [a.shape for a in arrs]


def _unpack_flat(packed, shapes):
    flat, out, off = packed.reshape(-1), [], 0
    for s in shapes:
        n = 1
        for d in s:
            n *= d
        out.append(flat[off:off + n].reshape(s))
        off += n
    return out


_EARLY = ["w_in", "w_q_b", "w_kv_b"]
_LATE = ["w_branch", "w_out", "ffn_up", "ffn_down"]


def _early_weights(a, l, stacks, conv_full):
    p = {n: a[n][l] for n in _W_NAMES if n not in ("ada_w", "ada_b")}
    p.update({n: conv_full[n][l] for n in conv_full})
    p.update(w_in4=stacks[0], w_q_b4=stacks[1], w_kv_b4=stacks[2])
    return p


def _late_weights(stacks):
    return _prep_late(dict(w_branch=stacks[0].reshape(2048, D), w_out=stacks[1].reshape(D, D), ffn_up4=stacks[2],
                           ffn_down=stacks[3].reshape(FFN, D)))


def _reduce_start(halves, ci, after, tag):
    chip_sum = []
    for t, h in halves.items():
        gs, recv = _ici_wait(h, _halves_slices, after, f"rs_halves_{t}_wait")
        chip_sum += [_add_cast(g, r, ci, f"rs_add_{t}") for g, r in zip(gs, recv)]
    return _ici_start(chip_sum, [_sds(v.shape, v.dtype) for v in chip_sum], _scatter_slices, after, f"rs_scatter_{tag}_start")


def _reduce_finish(started, after, tag):
    chip_sum, got = _ici_wait(started, _scatter_slices, after, f"rs_scatter_{tag}_wait")
    chip, ci = 2 * lax.axis_index("x") + lax.axis_index("y"), lax.axis_index("c")
    return _join_halves([_sum_chips(s, t, chip, ci, f"rs_sum_{tag}") for s, t in zip(chip_sum, got)], f"rs_join_{tag}")


def kernel(x, c, positions, ada_w, ada_b, norm1_w, w_in, q_a_norm, w_q_b, kv_a_norm, w_kv_b, q_norm, k_norm, pool_w, pool_scale, ssd_conv_w, ssd_conv_b, ssd_dt_bias, ssd_a_log, ssd_d, ssd_norm_w, w_branch, w_out, norm2_w, ffn_up, ffn_conv_w, ffn_conv_b, ffn_down, loss_target, m_ada_w, m_ada_b, m_norm1_w, m_w_in, m_q_a_norm, m_w_q_b, m_kv_a_norm, m_w_kv_b, m_q_norm, m_k_norm, m_pool_w, m_pool_scale, m_ssd_conv_w, m_ssd_conv_b, m_ssd_dt_bias, m_ssd_a_log, m_ssd_d, m_ssd_norm_w, m_w_branch, m_w_out, m_norm2_w, m_ffn_up, m_ffn_conv_w, m_ffn_conv_b, m_ffn_down, v_ada_w, v_ada_b, v_norm1_w, v_w_in, v_q_a_norm, v_w_q_b, v_kv_a_norm, v_w_kv_b, v_q_norm, v_k_norm, v_pool_w, v_pool_scale, v_ssd_conv_w, v_ssd_conv_b, v_ssd_dt_bias, v_ssd_a_log, v_ssd_d, v_ssd_norm_w, v_w_branch, v_w_out, v_norm2_w, v_ffn_up, v_ffn_conv_w, v_ffn_conv_b, v_ffn_down):
    a = dict(locals())
    xi, yi, ci = lax.axis_index("x"), lax.axis_index("y"), lax.axis_index("c")
    chip = 2 * xi + yi
    dev = 2 * chip + ci
    ncol = 6 * D // 4

    c_all = _allgather8(c.reshape(8, 128), "gather_c").reshape(8, D)
    c16 = jnp.pad(c_all, ((0, 8), (0, 0)))
    ada_b_cols = lax.dynamic_slice_in_dim(ada_b, chip * ncol, ncol, axis=1).reshape(2, 1, ncol)
    mod_part = _ada_fwd(c16, ada_w, ada_b_cols)[:, :8]
    small1, shapes1 = _pack_flat([mod_part, ssd_conv_w, ffn_conv_w], 8)
    got1 = _allgather8(small1, "gather_mod").reshape(8, -1, 128)
    per_chip = [_unpack_flat(got1[2 * k], shapes1) for k in range(4)]
    mod_all = jnp.concatenate([per_chip[k][0] for k in range(4)], -1)
    conv_full = {"ssd_conv_w": jnp.concatenate([per_chip[k][1] for k in range(4)], -1),
                 "ffn_conv_w": jnp.concatenate([per_chip[k][2] for k in range(4)], -1)}
    mod_mine = lax.dynamic_index_in_dim(mod_all, dev, axis=1, keepdims=False).reshape(2, 6, D)
    mods = [jnp.pad(mod_mine[l], ((0, 2), (0, 0))) for l in range(2)]

    big = _EARLY + _LATE
    shard = lambda names, l: [a[n][l].astype(BF16) for n in names]
    g0a = _gather_start(shard(_EARLY, 0), mods[0], "gather_0a")
    g0b = _gather_start(shard(_LATE, 0), g0a[-1], "gather_0b")
    g1 = _gather_start(shard(big, 1), g0b[-1], "gather_1")
    mods[0] = mods[0] + g1[-1][0, 0]

    def get_layer(l, after):
        if l == 0:
            return (_early_weights(a, 0, _gather_finish(g0a, mods[0], "gather_0a"), conv_full),
                    lambda aft: _late_weights(_gather_finish(g0b, aft, "gather_0b")))
        stacks = _gather_finish(g1, after, "gather_1")
        return _early_weights(a, 1, stacks[:3], conv_full), lambda aft: _late_weights(stacks[3:])

    halves, scatters = {}, {}

    def bwd_mod(l):
        return mods[l] if l == 1 else mods[0] + halves["1a"][-1][0, 0]

    def emit(l, late):
        if l == 1:
            halves["1b"] = _halves_start([late[n] for n in _LATE], late["ffn_down"], "1b")
            return halves["1b"][-1][0, 0]
        scatters["1"] = _reduce_start({"1b": halves["1b"], "1a": halves["1a"]}, ci, late["ffn_down"], "1")
        halves["0b"] = _halves_start([late[n] for n in _LATE], scatters["1"][-1], "0b")
        return halves["0b"][-1][0, 0]

    def mid(l, after):
        if l == 0:
            scatters["0b"] = _reduce_start({"0b": halves["0b"]}, ci, after, "0b")
            return scatters["0b"][-1][0, 0]

    def done(l, grads_l):
        halves[f"{l}a"] = _halves_start([grads_l[n] for n in _EARLY], halves[f"{l}b"][-1], f"{l}a")
        return halves[f"{l}a"][-1][0, 0]

    lpart, grad_x, grads, dmods = _local_step(x[0], loss_target[0], positions[0], mods, get_layer, bwd_mod, emit, mid, done)
    loss = lax.psum(lpart, ("x", "y", "c"))
    red1 = dict(zip(_LATE + _EARLY, _reduce_finish(scatters["1"], grad_x, "1")))
    red0b = _reduce_finish(scatters["0b"], red1["w_in"], "0b")
    scatters["0a"] = _reduce_start({"0a": halves["0a"]}, ci, red0b[0], "0a")

    small2, shapes2 = _pack_flat([jnp.stack(dmods)] + [grads[l][n] for l in range(2) for n, _ in _SMALL], 16)
    got2 = _allgather8(small2, "gather_small").reshape(8, -1, 128)
    tot = _unpack_flat(_sum_lead(got2, "sum_small"), shapes2)
    g = {"ada_b": tot[0].reshape(2, 6 * D)}
    for i, (n, _) in enumerate(_SMALL):
        g[n] = jnp.stack([tot[1 + i], tot[1 + len(_SMALL) + i]])
    for n, w in _CONV_SHARDED.items():
        g[n] = lax.dynamic_slice_in_dim(g[n], chip * w, w, axis=2)
    nd = 2 * 6 * D // 128
    dmod_all = jnp.transpose(got2[:, :nd].reshape(8, 2, 6 * D), (1, 0, 2))
    dmod_cols = lax.dynamic_slice_in_dim(jnp.pad(dmod_all, ((0, 0), (0, 8), (0, 0))), chip * ncol, ncol, axis=2)
    g["ada_w"] = _ada_bwd(c16, dmod_cols)

    delta, new_m, new_v = {}, {}, {}
    token = scatters["0a"][-1]
    for n, r0 in zip(_LATE, red0b):
        g[n], delta[n], new_m[n], new_v[n] = _adamw_layers(a[n], r0, red1[n], a["m_" + n], a["v_" + n], token, f"adamw_{n}")
    shp = ada_w.shape
    r2 = lambda t: t.reshape(-1, shp[-1])
    delta["ada_w"], new_m["ada_w"], new_v["ada_w"] = (
        o.reshape(shp) for o in _adamw(r2(ada_w), r2(g["ada_w"]), r2(m_ada_w), r2(v_ada_w), "adamw_ada_w"))
    behind = (delta[_LATE[-1]][0, 0, :1] + delta["ada_w"][0, 0, :1]).reshape(1)
    red0a = _reduce_finish(scatters["0a"], behind, "0a")
    for n, r0 in zip(_EARLY, red0a):
        if n == "w_in":
            g[n], delta[n], new_m[n], new_v[n] = _adamw_cols(a[n], r0, red1[n], a["m_" + n], a["v_" + n], f"adamw_{n}")
        else:
            g[n], delta[n], new_m[n], new_v[n] = _adamw_layers(a[n], r0, red1[n], a["m_" + n], a["v_" + n], token, f"adamw_{n}")
    rest = [n for n in _W_NAMES if n not in big and n != "ada_w"]
    packs = [_pack_flat([t[n] if pre is None else t[pre + n] for n in rest], 128)[0]
             for t, pre in ((a, None), (g, None), (a, "m_"), (a, "v_"))]
    rest_shapes = [a[n].shape for n in rest]
    outs = [_unpack_flat(o, rest_shapes) for o in _adamw(*packs, "adamw_rest")]
    for i, n in enumerate(rest):
        delta[n], new_m[n], new_v[n] = outs[0][i], outs[1][i], outs[2][i]

    return (loss, grad_x[None], *[g[n] for n in _W_NAMES], *[delta[n] for n in _W_NAMES],
            *[new_m[n] for n in _W_NAMES], *[new_v[n] for n in _W_NAMES])
```

```python
import functools

import jax
import jax.numpy as jnp
from jax import lax
from jax.experimental import pallas as pl
from jax.experimental.pallas import tpu as pltpu

F32 = jnp.float32
BF16 = jnp.bfloat16
MESH = pl.DeviceIdType.MESH
HI = lax.Precision.HIGHEST

D = 1024
N_HEADS = 8
NOPE, ROPE_DIM = 64, 32
Q_RANK, KV_RANK = 384, 256
POOL_WINDOWS = (2, 4, 8, 16)
SSD_HEADS, SSD_P, SSD_N, SSD_L = 16, 64, 128, 128
SSD_INNER = 1024
SSD_CONV_DIM = 1536
FFN = 2816
EPS = 1e-6
ROPE_THETA = 10000.0
OG, OZ, OX, OP, OQ, OKV, ODT, IN_PAD = 0, 3072, 4096, 5632, 6144, 6528, 6912, 7040
IN_DIM = 6832
ADAM_LR, ADAM_B1, ADAM_B2, ADAM_EPS, ADAM_WD, ADAM_STEP = 0.001, 0.9, 0.999, 1e-08, 0.01, 10

_ARB = pltpu.CompilerParams(dimension_semantics=("arbitrary",))
_PAR = pltpu.CompilerParams(dimension_semantics=("parallel",))


def _pick(n, pref):
    if n <= pref:
        return n
    best = None
    for t in range(128, pref + 1, 128):
        if n % t == 0:
            best = t
    assert best is not None, (n, pref)
    return best


def _row_tile(r, cap=256):
    best = None
    for t in range(16, min(r, cap) + 1, 16):
        if r % t == 0:
            best = t
    assert best is not None, r
    return best


def _sds(shape, dtype=F32):
    return jax.ShapeDtypeStruct(tuple(shape), dtype)


def _iota(shape, dim):
    return lax.broadcasted_iota(jnp.int32, shape, dim)


def _sigmoid(x):
    return 0.5 * jnp.tanh(0.5 * x) + 0.5


def _silu(x):
    return x * _sigmoid(x)


def _dsilu(x):
    s = _sigmoid(x)
    return s * (1.0 + x * (1.0 - s))


def _dot(a, b, dims):
    return lax.dot_general(a, b, (dims, ((), ())), preferred_element_type=F32)


_NN, _NT, _TN = ((1,), (0,)), ((1,), (1,)), ((0,), (0,))


def _dot_hi(a, b, dims=_NN):
    return lax.dot_general(a, b, (dims, ((), ())), preferred_element_type=F32, precision=HI)


def _shift_down(x, j):
    n = x.shape[0]
    return jnp.where(_iota(x.shape, 0) >= j, pltpu.roll(x, j, 0), 0.0)


def _shift_up(x, j):
    n = x.shape[0]
    return jnp.where(_iota(x.shape, 0) < n - j, pltpu.roll(x, n - j, 0), 0.0)


def _mm(a, b, mode="nn", out_dtype=F32, tm=2048, tn=512, tk=4096, res=None, gate=None, name="mm"):
    if mode == "nn":
        (M, K), (K2, N) = a.shape, b.shape
    elif mode == "nt":
        (M, K), (N, K2) = a.shape, b.shape
    else:
        (K, M), (K2, N) = a.shape, b.shape
    assert K == K2, (a.shape, b.shape, mode)
    tm, tn, tk = _pick(M, tm), _pick(N, tn), _pick(K, tk)
    nk = K // tk
    dims = {"nn": _NN, "nt": _NT, "tn": _TN}[mode]
    fused = res is not None

    def body(*refs):
        a_ref, b_ref = refs[:2]

        def finish(acc):
            if fused:
                r_ref, g_ref, o_ref, raw_ref = refs[2:6]
                raw_ref[...] = acc
                o_ref[...] = r_ref[...] + g_ref[...] * acc
            else:
                refs[2][...] = acc.astype(out_dtype)

        _mm_steps(a_ref, b_ref, dims, nk, refs[-1] if nk > 1 else None, finish)

    if mode == "nn":
        a_spec = pl.BlockSpec((tm, tk), lambda i, j, k: (i, k))
        b_spec = pl.BlockSpec((tk, tn), lambda i, j, k: (k, j))
    elif mode == "nt":
        a_spec = pl.BlockSpec((tm, tk), lambda i, j, k: (i, k))
        b_spec = pl.BlockSpec((tn, tk), lambda i, j, k: (j, k))
    else:
        a_spec = pl.BlockSpec((tk, tm), lambda i, j, k: (k, i))
        b_spec = pl.BlockSpec((tk, tn), lambda i, j, k: (k, j))
    o_spec = pl.BlockSpec((tm, tn), lambda i, j, k: (i, j))
    in_specs, args = [a_spec, b_spec], [a, b]
    out_shape, out_specs = _sds((M, N), out_dtype), o_spec
    if fused:
        in_specs += [o_spec, pl.BlockSpec((1, tn), lambda i, j, k: (0, j))]
        args += [res, gate]
        out_shape, out_specs = (_sds((M, N)), _sds((M, N))), (o_spec, o_spec)
    return pl.pallas_call(
        body, grid=(M // tm, N // tn, nk), in_specs=in_specs, out_specs=out_specs, out_shape=out_shape,
        scratch_shapes=[pltpu.VMEM((tm, tn), F32)] if nk > 1 else [], name=name,
        compiler_params=pltpu.CompilerParams(dimension_semantics=("parallel", "parallel", "arbitrary")),
    )(*args)


def _mm_steps(a_ref, b_ref, dims, nk, acc_ref, finish):
    part = _dot(a_ref[...].astype(BF16), b_ref[...].astype(BF16), dims)
    if nk == 1:
        finish(part)
        return
    k = pl.program_id(2)

    @pl.when(k == 0)
    def _():
        acc_ref[...] = part

    @pl.when(k > 0)
    def _():
        acc_ref[...] += part

    @pl.when(k == nk - 1)
    def _():
        finish(acc_ref[...])


def _mm_blocks(a, b, dims, grid, a_spec, b_spec, o_spec, out_shape, acc_shape, name):
    nk = grid[2]

    def body(a_ref, b_ref, o_ref, *scratch):
        def finish(acc):
            o_ref[...] = acc.astype(o_ref.dtype)

        _mm_steps(a_ref, b_ref, dims, nk, scratch[0] if nk > 1 else None, finish)

    return pl.pallas_call(
        body, grid=grid, in_specs=[a_spec, b_spec], out_specs=o_spec, out_shape=out_shape,
        scratch_shapes=[pltpu.VMEM(acc_shape, F32)] if nk > 1 else [], name=name,
        compiler_params=pltpu.CompilerParams(dimension_semantics=("parallel", "parallel", "arbitrary")),
    )(a, b)


_UP_SHARD = 2 * FFN // 4


def _up_fwd(h2, wup4, name, tm=2048):
    S = h2.shape[0]
    tm = min(tm, S)
    return _mm_blocks(h2, wup4, _NN, (S // tm, 4, 1), pl.BlockSpec((tm, D), lambda i, j, k: (i, 0)),
                      pl.BlockSpec((None, D, _UP_SHARD), lambda i, j, k: (j, 0, 0)), pl.BlockSpec((tm, _UP_SHARD), lambda i, j, k: (i, j)),
                      _sds((S, 2 * FFN)), (tm, _UP_SHARD), name)


def _up_dx(dup2, wup4, name, tm=2048, tn=512):
    S = dup2.shape[1]
    tm = min(tm, S)
    return _mm_blocks(dup2, wup4, _NT, (S // tm, D // tn, 4), pl.BlockSpec((None, tm, _UP_SHARD), lambda i, j, k: (lax.div(k, 2), i, lax.rem(k, 2))),
                      pl.BlockSpec((None, tn, _UP_SHARD), lambda i, j, k: (k, j, 0)), pl.BlockSpec((tm, tn), lambda i, j, k: (i, j)),
                      _sds((S, D)), (tm, tn), name)


def _up_dw(h2, dup2, name, tm=1024, tk=4096):
    S = h2.shape[0]
    tk = min(tk, S)
    return _mm_blocks(h2, dup2, _TN, (D // tm, 4, S // tk), pl.BlockSpec((tk, tm), lambda i, j, k: (k, i)),
                      pl.BlockSpec((None, tk, _UP_SHARD), lambda i, j, k: (lax.div(j, 2), k, lax.rem(j, 2))),
                      pl.BlockSpec((None, tm, _UP_SHARD), lambda i, j, k: (j, i, 0)), _sds((4, D, _UP_SHARD)), (tm, _UP_SHARD), name)


def _row_spec(tm, n):
    return pl.BlockSpec((tm, n), lambda i: (i, 0))


def _vec_spec(n, rows=1):
    return pl.BlockSpec((rows, n), lambda i: (0, 0))


def _ln_mod(x, nw, sc, sh, name, tm=256):
    S = x.shape[0]

    def body(x_ref, nw_ref, sc_ref, sh_ref, o_ref):
        xv = x_ref[...]
        r = lax.rsqrt(jnp.mean(xv * xv, -1, keepdims=True) + EPS)
        o_ref[...] = ((xv * r * nw_ref[...]) * (1.0 + sc_ref[...]) + sh_ref[...]).astype(BF16)

    return pl.pallas_call(
        body, grid=(S // tm,), in_specs=[_row_spec(tm, D)] + [_vec_spec(D)] * 3, out_specs=_row_spec(tm, D),
        out_shape=_sds((S, D), BF16), name=name, compiler_params=_PAR)(x, nw, sc, sh)


def _ln_mod_bwd(x, dh, dres, nw, sc, name, tm=256):
    S = x.shape[0]

    def body(x_ref, dh_ref, dres_ref, nw_ref, sc_ref, dx_ref, st_ref):
        @pl.when(pl.program_id(0) == 0)
        def _():
            st_ref[...] = jnp.zeros_like(st_ref)

        xv, dhv, nwv = x_ref[...], dh_ref[...], nw_ref[...]
        r = lax.rsqrt(jnp.mean(xv * xv, -1, keepdims=True) + EPS)
        xhat = xv * r
        dn = dhv * (1.0 + sc_ref[...])
        g = dn * nwv
        dx_ref[...] = dres_ref[...] + r * (g - xhat * jnp.mean(g * xhat, -1, keepdims=True))
        st_ref[0:1, :] += jnp.sum(dhv, 0, keepdims=True)
        st_ref[1:2, :] += jnp.sum(dhv * (xhat * nwv), 0, keepdims=True)
        st_ref[2:3, :] += jnp.sum(dn * xhat, 0, keepdims=True)

    return pl.pallas_call(
        body, grid=(S // tm,), in_specs=[_row_spec(tm, D)] * 3 + [_vec_spec(D)] * 2,
        out_specs=(_row_spec(tm, D), _vec_spec(D, 8)), out_shape=(_sds((S, D)), _sds((8, D))),
        name=name, compiler_params=_ARB)(x, dh, dres, nw, sc)


def _gate_bwd(dx, out, g, name, tm=256):
    S = dx.shape[0]

    def body(dx_ref, o_ref, g_ref, dz_ref, dg_ref):
        @pl.when(pl.program_id(0) == 0)
        def _():
            dg_ref[...] = jnp.zeros_like(dg_ref)

        dxv = dx_ref[...]
        dz_ref[...] = (dxv * g_ref[...]).astype(BF16)
        dg_ref[0:1, :] += jnp.sum(dxv * o_ref[...], 0, keepdims=True)

    return pl.pallas_call(
        body, grid=(S // tm,), in_specs=[_row_spec(tm, D)] * 2 + [_vec_spec(D)],
        out_specs=(_row_spec(tm, D), _vec_spec(D, 8)), out_shape=(_sds((S, D), BF16), _sds((8, D))),
        name=name, compiler_params=_ARB)(dx, out, g)


def _loss_grad(y, t, tm=256):
    S = y.shape[0]

    def body(y_ref, t_ref, dy_ref, l_ref):
        @pl.when(pl.program_id(0) == 0)
        def _():
            l_ref[...] = jnp.zeros_like(l_ref)

        e = y_ref[...] - t_ref[...]
        dy_ref[...] = e * (1.0 / D)
        l_ref[...] += 0.5 * jnp.sum(jnp.mean(e * e, -1, keepdims=True), 0, keepdims=True)

    return pl.pallas_call(
        body, grid=(S // tm,), in_specs=[_row_spec(tm, D)] * 2,
        out_specs=(_row_spec(tm, D), pl.BlockSpec((8, 128), lambda i: (0, 0))),
        out_shape=(_sds((S, D)), _sds((8, 128))), name="loss_grad", compiler_params=_ARB)(y, t)


_CONV_ROWS = 256
_HALO = 8


def _rows(ref, lo, hi):
    S, c = ref.shape
    parts = [jnp.zeros((-lo, c), F32)] if lo < 0 else []
    parts.append(ref[max(lo, 0):min(hi, S), :])
    if hi > S:
        parts.append(jnp.zeros((hi - S, c), F32))
    return parts[0] if len(parts) == 1 else jnp.concatenate(parts, 0)


def _conv_rows(ext, w, b, n):
    K = w.shape[0]
    acc = ext[_HALO:_HALO + n] * w[K - 1:K, :] + b
    for j in range(1, K):
        acc = acc + pltpu.roll(ext, j, 0)[_HALO:_HALO + n] * w[K - 1 - j:K - j, :]
    return acc


def _conv_rows_bwd(ext, w, dc, n):
    K, m = w.shape[0], dc.shape[0]
    d0 = dc[0:n]
    dx = d0 * w[K - 1:K, :]
    dws = [jnp.sum(d0 * ext[_HALO:_HALO + n], 0, keepdims=True)]
    for j in range(1, K):
        dx = dx + pltpu.roll(dc, m - j, 0)[0:n] * w[K - 1 - j:K - j, :]
        dws.append(jnp.sum(d0 * pltpu.roll(ext, j, 0)[_HALO:_HALO + n], 0, keepdims=True))
    return dx, dws[::-1], jnp.sum(d0, 0, keepdims=True)


def _col_spec(S, tc, off=0):
    return pl.BlockSpec((S, tc), lambda j: (0, j + off))


def _ssd_pre(proj, cw, cb, tc=256):
    S, n = proj.shape[0], SSD_CONV_DIM
    R = min(_CONV_ROWS, S)

    def body(x_ref, w_ref, b_ref, o_ref):
        wv, bv = w_ref[...], b_ref[...]
        for r0 in range(0, S, R):
            o_ref[r0:r0 + R, :] = _silu(_conv_rows(_rows(x_ref, r0 - _HALO, r0 + R), wv, bv, R))

    return pl.pallas_call(
        body, grid=(n // tc,),
        in_specs=[_col_spec(S, tc, OX // tc), pl.BlockSpec((4, tc), lambda j: (0, j)), pl.BlockSpec((1, tc), lambda j: (0, j))],
        out_specs=_col_spec(S, tc), out_shape=_sds((S, n)), name="ssd_pre", compiler_params=_PAR)(proj, cw, cb)


def _ssd_pre_bwd(proj, dxc, cw, cb, dproj, tc=256):
    S, n = proj.shape[0], SSD_CONV_DIM
    R = min(_CONV_ROWS, S)

    def body(x_ref, d_ref, w_ref, b_ref, dproj_in, dx_ref, dw_ref, db_ref):
        wv, bv = w_ref[...], b_ref[...]
        acc = [jnp.zeros((1, tc), F32)] * 5
        for r0 in range(0, S, R):
            ext = _rows(x_ref, r0 - _HALO, r0 + R + _HALO)
            dc = _rows(d_ref, r0, r0 + R + _HALO) * _dsilu(_conv_rows(ext, wv, bv, R + _HALO))
            dx, dws, db = _conv_rows_bwd(ext, wv, dc, R)
            dx_ref[r0:r0 + R, :] = dx.astype(BF16)
            acc = [s + d for s, d in zip(acc, dws + [db])]
        for k in range(4):
            dw_ref[k:k + 1, :] = acc[k]
        db_ref[...] = acc[4]

    wspec, bspec = pl.BlockSpec((4, tc), lambda j: (0, j)), pl.BlockSpec((1, tc), lambda j: (0, j))
    return pl.pallas_call(
        body, grid=(n // tc,), in_specs=[_col_spec(S, tc, OX // tc), _col_spec(S, tc), wspec, bspec, _ANY],
        out_specs=(_col_spec(S, tc, OX // tc), wspec, bspec), out_shape=(_sds(dproj.shape, BF16), _sds((4, n)), _sds((1, n))),
        input_output_aliases={4: 0}, name="ssd_pre_bwd", compiler_params=_PAR)(proj, dxc, cw, cb, dproj)


def _ffn_act(up, cw, cb, tc=256):
    S, nb = up.shape[0], FFN // tc
    R = min(_CONV_ROWS, S)

    def body(g_ref, v_ref, wg_ref, wv_ref, bg_ref, bv_ref, o_ref):
        wg, wv, bg, bv = wg_ref[...], wv_ref[...], bg_ref[...], bv_ref[...]
        for r0 in range(0, S, R):
            cg = _conv_rows(_rows(g_ref, r0 - _HALO, r0 + R), wg, bg, R)
            cv = _conv_rows(_rows(v_ref, r0 - _HALO, r0 + R), wv, bv, R)
            o_ref[r0:r0 + R, :] = (_silu(cg) * cv).astype(BF16)

    def wspec(off):
        return pl.BlockSpec((3, tc), lambda j: (0, j + off))

    def bspec(off):
        return pl.BlockSpec((1, tc), lambda j: (0, j + off))

    return pl.pallas_call(
        body, grid=(nb,), in_specs=[_col_spec(S, tc), _col_spec(S, tc, nb), wspec(0), wspec(nb), bspec(0), bspec(nb)],
        out_specs=_col_spec(S, tc), out_shape=_sds((S, FFN), BF16), name="ffn_act", compiler_params=_PAR)(up, up, cw, cw, cb, cb)


def _ffn_act_bwd(up, dact, cw, cb, tc=256):
    S, nb = up.shape[0], FFN // tc
    R = min(_CONV_ROWS, S)

    def body(g_ref, v_ref, d_ref, wg_ref, wv_ref, bg_ref, bv_ref, dx_ref, dw_ref, db_ref):
        wg, wv, bg, bv = wg_ref[...], wv_ref[...], bg_ref[...], bv_ref[...]
        acc = [[jnp.zeros((1, tc), F32)] * 4, [jnp.zeros((1, tc), F32)] * 4]
        for r0 in range(0, S, R):
            eg, ev = _rows(g_ref, r0 - _HALO, r0 + R + _HALO), _rows(v_ref, r0 - _HALO, r0 + R + _HALO)
            da = _rows(d_ref, r0, r0 + R + _HALO)
            cg, cv = _conv_rows(eg, wg, bg, R + _HALO), _conv_rows(ev, wv, bv, R + _HALO)
            sg = _sigmoid(cg)
            for half, (ext, w, dc) in enumerate(((eg, wg, da * cv * (sg * (1.0 + cg * (1.0 - sg)))), (ev, wv, da * (cg * sg)))):
                dx, dws, db = _conv_rows_bwd(ext, w, dc, R)
                dx_ref[half, r0:r0 + R, :] = dx.astype(BF16)
                acc[half] = [s + d for s, d in zip(acc[half], dws + [db])]
        for half in range(2):
            for k in range(3):
                dw_ref[half, k:k + 1, :] = acc[half][k]
            db_ref[half] = acc[half][3]

    def wspec(off):
        return pl.BlockSpec((3, tc), lambda j: (0, j + off))

    def bspec(off):
        return pl.BlockSpec((1, tc), lambda j: (0, j + off))

    cs = _col_spec(S, tc)
    both = lambda r: pl.BlockSpec((2, r, tc), lambda j: (0, 0, j))
    return pl.pallas_call(
        body, grid=(nb,), in_specs=[cs, _col_spec(S, tc, nb), cs, wspec(0), wspec(nb), bspec(0), bspec(nb)],
        out_specs=(both(S), both(3), both(1)), out_shape=(_sds((2, S, FFN), BF16), _sds((2, 3, FFN)), _sds((2, 1, FFN))),
        name="ffn_act_bwd", compiler_params=_PAR)(up, up, dact, cw, cw, cb, cb)


def _window_sum(x, w, up=False):
    shift = _shift_up if up else _shift_down
    j = 1
    while j < w:
        x = x + shift(x, j)
        j *= 2
    return x


def _pool_fwd(proj, pool_w, pool_scale):
    S = proj.shape[0]

    def body(u_ref, w_ref, s_ref, o_ref):
        cnt_row = (_iota((S, 128), 0) + 1).astype(F32)
        for g, w in enumerate(POOL_WINDOWS):
            sl = slice(g * 128, (g + 1) * 128)
            u = u_ref[:, sl]
            pooled = _window_sum(u, w) / jnp.minimum(cnt_row, float(w)) - u
            mixed = _dot(pooled.astype(BF16), w_ref[g].astype(BF16), _NN)
            o_ref[:, sl] = (mixed * s_ref[:, sl]).astype(BF16)

    return pl.pallas_call(
        body, grid=(1,),
        in_specs=[pl.BlockSpec((S, 512), lambda i: (0, OP // 512)), pl.BlockSpec((4, 128, 128), lambda i: (0, 0, 0)), _vec_spec(512)],
        out_specs=pl.BlockSpec((S, 512), lambda i: (0, 0)), out_shape=_sds((S, 512), BF16), name="pool_fwd",
        compiler_params=_ARB)(proj, pool_w, pool_scale)


def _pool_bwd(proj, dob, pool_w, pool_scale, dproj):
    S = proj.shape[0]

    def body(u_ref, d_ref, w_ref, s_ref, dproj_in, du_ref, dw_ref, ds_ref):
        cnt_row = (_iota((S, 128), 0) + 1).astype(F32)
        for g, w in enumerate(POOL_WINDOWS):
            sl = slice(g * 128, (g + 1) * 128)
            u, dv, wv = u_ref[:, sl], d_ref[:, sl], w_ref[g].astype(BF16)
            cnt = jnp.minimum(cnt_row, float(w))
            pooled = (_window_sum(u, w) / cnt - u).astype(BF16)
            ds_ref[:, sl] = jnp.sum(dv * _dot(pooled, wv, _NN), 0, keepdims=True)
            dmix = (dv * s_ref[:, sl]).astype(BF16)
            dw_ref[g] = _dot(pooled, dmix, _TN)
            dp = _dot(dmix, wv, _NT)
            du_ref[:, sl] = (_window_sum(dp / cnt, w, up=True) - dp).astype(BF16)

    blk = pl.BlockSpec((S, 512), lambda i: (0, 0))
    wspec = pl.BlockSpec((4, 128, 128), lambda i: (0, 0, 0))
    return pl.pallas_call(
        body, grid=(1,), in_specs=[pl.BlockSpec((S, 512), lambda i: (0, OP // 512)), blk, wspec, _vec_spec(512), _ANY],
        out_specs=(pl.BlockSpec((S, 512), lambda i: (0, OP // 512)), wspec, _vec_spec(512)),
        out_shape=(_sds(dproj.shape, BF16), _sds((4, 128, 128)), _sds((1, 512))), input_output_aliases={4: 0},
        name="pool_bwd", compiler_params=_ARB)(proj, dob, pool_w, pool_scale, dproj)


def _branch_specs():
    return [pl.BlockSpec((512, D), lambda i: (0, 0)), pl.BlockSpec((512, D), lambda i: (1, 0)), pl.BlockSpec((1024, D), lambda i: (1, 0))]


def _merge_fwd(oa, ob, oc, proj, wbr, tm=256):
    S = oa.shape[0]

    def body(oa_ref, ob_ref, oc_ref, gl_ref, wa_ref, wb_ref, wc_ref, o_ref):
        acc = _sigmoid(gl_ref[:, 0:D]) * _dot(oa_ref[...], wa_ref[...], _NN)
        acc += _sigmoid(gl_ref[:, D:2 * D]) * _dot(ob_ref[...], wb_ref[...], _NN)
        acc += _sigmoid(gl_ref[:, 2 * D:3 * D]) * _dot(oc_ref[...], wc_ref[...], _NN)
        o_ref[...] = acc.astype(BF16)

    return pl.pallas_call(
        body, grid=(S // tm,),
        in_specs=[_row_spec(tm, 512), _row_spec(tm, 512), _row_spec(tm, D), _row_spec(tm, 3 * D)] + _branch_specs(),
        out_specs=_row_spec(tm, D), out_shape=_sds((S, D), BF16), name="merge_fwd", compiler_params=_PAR)(oa, ob, oc, proj, wbr, wbr, wbr)


def _merge_bwd(dm, oa, ob, oc, proj, wbr, tm=256):
    S = oa.shape[0]

    def body(dm_ref, oa_ref, ob_ref, oc_ref, gl_ref, wa_ref, wb_ref, wc_ref, dya_ref, dyb_ref, dyc_ref, dgl_ref, doa_ref, dob_ref, doc_ref):
        dmv = dm_ref[...]
        for i, (o_ref, w_ref, dy_ref, do_ref) in enumerate(
                ((oa_ref, wa_ref, dya_ref, doa_ref), (ob_ref, wb_ref, dyb_ref, dob_ref), (oc_ref, wc_ref, dyc_ref, doc_ref))):
            gt = _sigmoid(gl_ref[:, i * D:(i + 1) * D])
            wv = w_ref[...]
            yv = _dot(o_ref[...], wv, _NN)
            dy = (dmv * gt).astype(BF16)
            dy_ref[...] = dy
            dgl_ref[:, i * D:(i + 1) * D] = (dmv * yv * gt * (1.0 - gt)).astype(BF16)
            do_ref[...] = _dot(dy, wv, _NT)

    rs = _row_spec
    return pl.pallas_call(
        body, grid=(S // tm,),
        in_specs=[rs(tm, D), rs(tm, 512), rs(tm, 512), rs(tm, D), rs(tm, 3 * D)] + _branch_specs(),
        out_specs=(rs(tm, D), rs(tm, D), rs(tm, D), rs(tm, 3 * D), rs(tm, 512), rs(tm, 512), rs(tm, D)),
        out_shape=(_sds((S, D), BF16), _sds((S, D), BF16), _sds((S, D), BF16), _sds((S, IN_PAD), BF16), _sds((S, 512)), _sds((S, 512)), _sds((S, D))),
        name="merge_bwd", compiler_params=_PAR)(dm, oa, ob, oc, proj, wbr, wbr, wbr)


def _rope_tab(posb, invf, tm=256):
    S = posb.shape[0]

    def body(p_ref, f_ref, c_ref, s1_ref, s2_ref):
        ang = p_ref[...] * f_ref[...]
        lane = _iota(ang.shape, 1)
        cs, sn = jnp.cos(ang), jnp.sin(ang)
        c_ref[...] = jnp.where(lane < NOPE, 1.0, cs)
        s1_ref[...] = jnp.where((lane >= 64) & (lane < 80), -sn, 0.0)
        s2_ref[...] = jnp.where((lane >= 80) & (lane < 96), sn, 0.0)

    rs = _row_spec(tm, 128)
    return pl.pallas_call(body, grid=(S // tm,), in_specs=[rs, _vec_spec(128)], out_specs=(rs, rs, rs),
                          out_shape=(_sds((S, 128)),) * 3, name="rope_tab", compiler_params=_PAR)(posb, invf)


def _rope(u, C, S1, S2):
    return u * C + pltpu.roll(u, 112, 1) * S1 + pltpu.roll(u, 16, 1) * S2


def _rope_t(dy, C, S1, S2):
    return dy * C + pltpu.roll(dy * S1, 16, 1) + pltpu.roll(dy * S2, 112, 1)


def _seg_sum(v, mask):
    return jnp.sum(jnp.where(mask, v, 0.0), -1, keepdims=True)


def _mla_latents(pq_ref, pkv_ref, wqb_ref, wkvb_ref, qan_ref, kvan_ref):
    ql, kvl = pq_ref[...], pkv_ref[...]
    ckv, kr = kvl[:, 0:KV_RANK], kvl[:, KV_RANK:KV_RANK + 128]
    rq = lax.rsqrt(jnp.mean(ql * ql, -1, keepdims=True) + EPS)
    rkv = lax.rsqrt(jnp.mean(ckv * ckv, -1, keepdims=True) + EPS)
    nq = (ql * rq * qan_ref[...]).astype(BF16)
    nkv = (ckv * rkv * kvan_ref[...]).astype(BF16)
    kv = jnp.concatenate([_dot(nkv, wkvb_ref[k], _NN) for k in range(4)], 1)
    return ql, ckv, kr, rq, rkv, nq, nkv, _dot(nq, wqb_ref[...], _NN), kv


def _mla_specs(tm):
    full = lambda r, n: pl.BlockSpec((r, n), lambda i: (0, 0))
    return ([pl.BlockSpec((tm, 384), lambda i: (i, OQ // 384)), pl.BlockSpec((tm, 384), lambda i: (i, OKV // 384))],
            [full(Q_RANK, D), pl.BlockSpec((4, KV_RANK, 256), lambda i: (0, 0, 0)), _vec_spec(Q_RANK), _vec_spec(KV_RANK), _vec_spec(128), _vec_spec(128)]
            + [_row_spec(tm, 128)] * 3)


def _mla_prep(proj, wqb, wkvb, qan, kvan, wq, wk, rope, tm=256):
    S = proj.shape[0]

    def body(pq_ref, pkv_ref, wqb_ref, wkvb_ref, qan_ref, kvan_ref, wq_ref, wk_ref, c_ref, s1_ref, s2_ref, qf_ref, kf_ref, ve_ref):
        _, _, kr, _, _, _, _, q, kv = _mla_latents(pq_ref, pkv_ref, wqb_ref, wkvb_ref, qan_ref, kvan_ref)
        C, S1, S2, wqv, wkv = c_ref[...], s1_ref[...], s2_ref[...], wq_ref[...], wk_ref[...]
        lane = _iota((tm, 128), 1)
        mn, mr = lane < 64, (lane >= 64) & (lane < 96)
        rrk = lax.rsqrt(_seg_sum(kr * kr, mr) / ROPE_DIM + EPS)
        ykr = _rope(jnp.where(mr, kr * rrk * wkv, 0.0), C, S1, S2)
        for h in range(N_HEADS):
            sl = slice(h * 128, (h + 1) * 128)
            t = q[:, sl]
            rn = lax.rsqrt(_seg_sum(t * t, mn) / NOPE + EPS)
            rr = lax.rsqrt(_seg_sum(t * t, mr) / ROPE_DIM + EPS)
            qf_ref[:, sl] = _rope(t * jnp.where(mn, rn, jnp.where(mr, rr, 0.0)) * wqv, C, S1, S2).astype(BF16)
            t = kv[:, sl]
            rn = lax.rsqrt(_seg_sum(t * t, mn) / NOPE + EPS)
            kf_ref[:, sl] = (jnp.where(mn, t * rn * wkv, 0.0) + ykr).astype(BF16)
            ve_ref[:, sl] = (jnp.where(mn, pltpu.roll(t, 64, 1), 0.0) if h % 2 == 0 else jnp.where(mn, 0.0, t)).astype(BF16)

    pspecs, wspecs = _mla_specs(tm)
    rs = _row_spec(tm, D)
    return pl.pallas_call(body, grid=(S // tm,), in_specs=pspecs + wspecs, out_specs=(rs, rs, rs),
                          out_shape=(_sds((S, D), BF16),) * 3, name="mla_prep", compiler_params=_PAR)(
        proj, proj, wqb, wkvb, qan, kvan, wq, wk, *rope)


def _mla_prep_bwd(proj, dqf, dkf, dve, wqb, wkvb, qan, kvan, wq, wk, rope, dproj, tm=256):
    S = proj.shape[0]

    def body(pq_ref, pkv_ref, wqb_ref, wkvb_ref, qan_ref, kvan_ref, wq_ref, wk_ref, c_ref, s1_ref, s2_ref,
             dqf_ref, dkf_ref, dve_ref, dproj_in, dlat_ref, dwqb_ref, dwkvb_ref, st_ref, dq_scr, dkv_scr):
        dqs_ref, dkvs_ref = dlat_ref.at[:, 0:384], dlat_ref.at[:, 384:768]
        @pl.when(pl.program_id(0) == 0)
        def _():
            dwqb_ref[...] = jnp.zeros_like(dwqb_ref)
            dwkvb_ref[...] = jnp.zeros_like(dwkvb_ref)
            st_ref[...] = jnp.zeros_like(st_ref)

        ql, ckv, kr, rq, rkv, nq, nkv, q, kv = _mla_latents(pq_ref, pkv_ref, wqb_ref, wkvb_ref, qan_ref, kvan_ref)
        C, S1, S2, wqv, wkv = c_ref[...], s1_ref[...], s2_ref[...], wq_ref[...], wk_ref[...]
        lane = _iota((tm, 128), 1)
        mn, mr = lane < 64, (lane >= 64) & (lane < 96)
        dwq = jnp.zeros((1, 128), F32)
        dwk = jnp.zeros((1, 128), F32)
        dykr = jnp.zeros((tm, 128), F32)
        for h in range(N_HEADS):
            sl = slice(h * 128, (h + 1) * 128)
            t = q[:, sl]
            rn = lax.rsqrt(_seg_sum(t * t, mn) / NOPE + EPS)
            rr = lax.rsqrt(_seg_sum(t * t, mr) / ROPE_DIM + EPS)
            scale = jnp.where(mn, rn, jnp.where(mr, rr, 0.0))
            that = t * scale
            du = _rope_t(dqf_ref[:, sl], C, S1, S2)
            dwq += jnp.sum(du * that, 0, keepdims=True)
            g = du * wqv
            gt = g * that
            dq_scr[:, sl] = scale * (g - that * jnp.where(mn, _seg_sum(gt, mn) / NOPE, _seg_sum(gt, mr) / ROPE_DIM))
            t = kv[:, sl]
            rn = lax.rsqrt(_seg_sum(t * t, mn) / NOPE + EPS)
            that = jnp.where(mn, t * rn, 0.0)
            dkf = dkf_ref[:, sl]
            dykr += jnp.where(mr, dkf, 0.0)
            dkn = jnp.where(mn, dkf, 0.0)
            dwk += jnp.sum(dkn * that, 0, keepdims=True)
            g = dkn * wkv
            dve = dve_ref[:, sl]
            dkv_scr[:, sl] = jnp.where(mn, rn * (g - that * (jnp.sum(g * that, -1, keepdims=True) / NOPE)),
                                       pltpu.roll(dve, 64, 1) if h % 2 == 0 else dve)
        rrk = lax.rsqrt(_seg_sum(kr * kr, mr) / ROPE_DIM + EPS)
        that = jnp.where(mr, kr * rrk, 0.0)
        dukr = jnp.where(mr, _rope_t(dykr, C, S1, S2), 0.0)
        dwk += jnp.sum(dukr * that, 0, keepdims=True)
        g = dukr * wkv
        dkr = rrk * (g - that * (jnp.sum(g * that, -1, keepdims=True) / ROPE_DIM))
        dqv, dkvv = dq_scr[...].astype(BF16), dkv_scr[...].astype(BF16)
        dnq = _dot(dqv, wqb_ref[...], _NT)
        dwqb_ref[...] += _dot(nq, dqv, _TN)
        dnkv = jnp.zeros((tm, KV_RANK), F32)
        for k in range(4):
            dnkv += _dot(dkvv[:, k * 256:(k + 1) * 256], wkvb_ref[k], _NT)
            dwkvb_ref[k] += _dot(nkv, dkvv[:, k * 256:(k + 1) * 256], _TN)
        xhat = ql * rq
        st_ref[0:1, 0:Q_RANK] += jnp.sum(dnq * xhat, 0, keepdims=True)
        g = dnq * qan_ref[...]
        dqs_ref[...] = (rq * (g - xhat * jnp.mean(g * xhat, -1, keepdims=True))).astype(BF16)
        xhat = ckv * rkv
        st_ref[1:2, 0:KV_RANK] += jnp.sum(dnkv * xhat, 0, keepdims=True)
        g = dnkv * kvan_ref[...]
        dkvs_ref[:, 0:KV_RANK] = (rkv * (g - xhat * jnp.mean(g * xhat, -1, keepdims=True))).astype(BF16)
        dkvs_ref[:, KV_RANK:KV_RANK + 128] = dkr.astype(BF16)
        st_ref[2:3, 0:128] += dwq
        st_ref[3:4, 0:128] += dwk

    pspecs, wspecs = _mla_specs(tm)
    rs = _row_spec(tm, D)
    full = lambda r, n: pl.BlockSpec((r, n), lambda i: (0, 0))
    return pl.pallas_call(
        body, grid=(S // tm,), in_specs=pspecs + wspecs + [rs, rs, rs, _ANY],
        out_specs=(pl.BlockSpec((tm, 768), lambda i: (i, OQ // 768)), full(Q_RANK, D), pl.BlockSpec((4, KV_RANK, 256), lambda i: (0, 0, 0)), full(8, D)),
        out_shape=(_sds(dproj.shape, BF16), _sds((Q_RANK, D)), _sds((4, KV_RANK, 256)), _sds((8, D))), input_output_aliases={14: 0},
        scratch_shapes=[pltpu.VMEM((tm, D), F32), pltpu.VMEM((tm, D), F32)], name="mla_prep_bwd", compiler_params=_ARB)(
        proj, proj, wqb, wkvb, qan, kvan, wq, wk, *rope, dqf, dkf, dve, dproj)


_ATT_SCALE = (NOPE + ROPE_DIM) ** -0.5


def _att_probs(q, k, i, tq):
    n = k.shape[0]
    s = _dot(q, k, _NT) * _ATT_SCALE
    tri = _iota((tq, tq), 1) <= _iota((tq, tq), 0)
    diag = jnp.where(tri, s[:, n - tq:], -1e30)
    s = diag if n == tq else jnp.concatenate([s[:, :n - tq], diag], 1)
    p = jnp.exp(s - jnp.max(s, -1, keepdims=True))
    return p * (1.0 / jnp.sum(p, -1, keepdims=True))


def _attn_fwd(qf, kf, ve, tq=256):
    S = qf.shape[0]

    def body(q_ref, k_ref, v_ref, o_ref):
        for i in range(S // tq):
            n, rows = (i + 1) * tq, slice(i * tq, (i + 1) * tq)
            acc = jnp.zeros((tq, 128), F32)
            for hh in range(2):
                sl = slice(hh * 128, (hh + 1) * 128)
                p = _att_probs(q_ref[rows, sl], k_ref[0:n, sl], i, tq)
                acc += _dot(p.astype(BF16), v_ref[0:n, sl], _NN)
            o_ref[rows, :] = acc.astype(BF16)

    ps = pl.BlockSpec((S, 256), lambda h: (0, h))
    return pl.pallas_call(body, grid=(N_HEADS // 2,), in_specs=[ps, ps, ps], out_specs=pl.BlockSpec((S, 128), lambda h: (0, h)),
                          out_shape=_sds((S, 512), BF16), name="attn_fwd", compiler_params=_PAR)(qf, kf, ve)


def _attn_bwd(qf, kf, ve, do, tq=256):
    S = qf.shape[0]

    def body(q_ref, k_ref, v_ref, do_ref, dq_ref, dk_ref, dv_ref):
        dk_ref[...] = jnp.zeros_like(dk_ref)
        dv_ref[...] = jnp.zeros_like(dv_ref)
        for i in range(S // tq):
            n, rows = (i + 1) * tq, slice(i * tq, (i + 1) * tq)
            dob = do_ref[rows, :].astype(BF16)
            for hh in range(2):
                sl = slice(hh * 128, (hh + 1) * 128)
                q, k = q_ref[rows, sl], k_ref[0:n, sl]
                p = _att_probs(q, k, i, tq)
                dv_ref[0:n, sl] += _dot(p.astype(BF16), dob, _TN)
                dp = _dot(dob, v_ref[0:n, sl], _NT)
                ds = (p * (dp - jnp.sum(dp * p, -1, keepdims=True)) * _ATT_SCALE).astype(BF16)
                dq_ref[rows, sl] = _dot(ds, k, _NN)
                dk_ref[0:n, sl] += _dot(ds, q, _TN)

    ps = pl.BlockSpec((S, 256), lambda h: (0, h))
    return pl.pallas_call(body, grid=(N_HEADS // 2,), in_specs=[ps, ps, ps, pl.BlockSpec((S, 128), lambda h: (0, h))], out_specs=(ps, ps, ps),
                          out_shape=(_sds((S, D)),) * 3, name="attn_bwd", compiler_params=_PAR)(qf, kf, ve, do)


def _softplus(x):
    return jnp.maximum(x, 0.0) + jnp.log1p(jnp.exp(-jnp.abs(x)))


def _ssd_chunk(xc_ref, dtr_ref, dtb_ref, al_ref, e_ref):
    L = SSD_L
    a = -jnp.exp(al_ref[...])
    dtp = _softplus(dtr_ref[...] + dtb_ref[...])
    causal = _iota((L, L), 1) <= _iota((L, L), 0)
    cs = _dot_hi(causal.astype(F32), dtp * a)
    E = e_ref[...]
    dtx, csx = _dot_hi(dtp, E), _dot_hi(cs, E)
    X = xc_ref[:, 0:SSD_INNER]
    Xd = X * dtx
    dec_out = jnp.exp(csx)
    dec_st = jnp.exp(csx[L - 1:L, :] - csx)
    return a, dtp, causal, cs, cs.T, dtx, X, Xd, dec_out, dec_st


def _ssd_decay(causal, cs, cs_row, h):
    diff = cs[:, h:h + 1] - cs_row[h:h + 1, :]
    return jnp.where(causal, jnp.exp(jnp.where(causal, diff, 0.0)), 0.0)


def _ssd_groups(xc_ref, g):
    b0, c0 = SSD_INNER + g * SSD_N, SSD_INNER + 2 * SSD_N + g * SSD_N
    return xc_ref[:, b0:b0 + SSD_N].astype(BF16), xc_ref[:, c0:c0 + SSD_N].astype(BF16)


def _pair_decay(cs, pair):
    L = SSD_L
    return jnp.where(_iota((128, 128), 0) < 64, jnp.exp(cs[L - 1:L, 2 * pair:2 * pair + 1]), jnp.exp(cs[L - 1:L, 2 * pair + 1:2 * pair + 2]))


def _ssd_in_specs(nc, rev):
    idx = (lambda c: nc - 1 - c) if rev else (lambda c: c)
    return [pl.BlockSpec((SSD_L, SSD_CONV_DIM), lambda c: (idx(c), 0)), pl.BlockSpec((SSD_L, 128), lambda c: (idx(c), ODT // 128)),
            _vec_spec(128), _vec_spec(128), _vec_spec(SSD_INNER), pl.BlockSpec((128, SSD_INNER), lambda c: (0, 0))]


def _ssd_core(xc, proj, dtb, alog, dskip, E):
    S = xc.shape[0]
    nc = S // SSD_L

    def body(xc_ref, dtr_ref, dtb_ref, al_ref, dx_ref, e_ref, y_ref, hp_ref, h_scr):
        @pl.when(pl.program_id(0) == 0)
        def _():
            h_scr[...] = jnp.zeros_like(h_scr)

        hp_ref[0] = h_scr[...]
        _, _, causal, cs, cs_row, _, X, Xd, dec_out, dec_st = _ssd_chunk(xc_ref, dtr_ref, dtb_ref, al_ref, e_ref)
        Xs = Xd * dec_st
        lane = _iota((SSD_L, 128), 1)
        for g in range(2):
            Bg, Cg = _ssd_groups(xc_ref, g)
            CB = _dot(Cg, Bg, _NT)
            for pr in range(4):
                pair = g * 4 + pr
                psl = slice(pair * 128, (pair + 1) * 128)
                Xdp = Xd[:, psl].astype(BF16)
                r0 = _dot((CB * _ssd_decay(causal, cs, cs_row, 2 * pair)).astype(BF16), Xdp, _NN)
                r1 = _dot((CB * _ssd_decay(causal, cs, cs_row, 2 * pair + 1)).astype(BF16), Xdp, _NN)
                Hp = h_scr[psl, :]
                W = _dot(Cg, Hp.astype(BF16), _NT)
                y_ref[:, psl] = jnp.where(lane < 64, r0, r1) + W * dec_out[:, psl] + X[:, psl] * dx_ref[:, psl]
                h_scr[psl, :] = Hp * _pair_decay(cs, pair) + _dot(Xs[:, psl].astype(BF16), Bg, _TN)

    return pl.pallas_call(
        body, grid=(nc,), in_specs=_ssd_in_specs(nc, False),
        out_specs=(pl.BlockSpec((SSD_L, SSD_INNER), lambda c: (c, 0)), pl.BlockSpec((1, SSD_INNER, SSD_N), lambda c: (c, 0, 0))),
        out_shape=(_sds((S, SSD_INNER)), _sds((nc, SSD_INNER, SSD_N))), scratch_shapes=[pltpu.VMEM((SSD_INNER, SSD_N), F32)],
        name="ssd_core", compiler_params=_ARB)(xc, proj, dtb, alog, dskip, E)


def _ssd_core_bwd(xc, proj, hprev, dy, dtb, alog, dskip, E, ET, dproj):
    S = xc.shape[0]
    nc = S // SSD_L
    L = SSD_L

    def body(xc_ref, dtr_ref, dtb_ref, al_ref, dx_ref, e_ref, et_ref, hp_ref, dy_ref, dproj_in, dxc_ref, ddt_ref, st_ref, dh_scr, acc_scr):
        step = pl.program_id(0)

        @pl.when(step == 0)
        def _():
            dh_scr[...] = jnp.zeros_like(dh_scr)
            acc_scr[...] = jnp.zeros_like(acc_scr)
            st_ref[...] = jnp.zeros_like(st_ref)

        a, dtp, causal, cs, cs_row, dtx, X, Xd, dec_out, dec_st = _ssd_chunk(xc_ref, dtr_ref, dtb_ref, al_ref, e_ref)
        lane = _iota((L, 128), 1)
        sub = _iota((128, L), 0)
        dcs_col = jnp.zeros((L, 128), F32)
        dcs_row = jnp.zeros((128, L), F32)
        dcs_last = jnp.zeros((1, 128), F32)
        dcsx, ddtx, dlastx = [], [], []
        for g in range(2):
            Bg, Cg = _ssd_groups(xc_ref, g)
            CB = _dot(Cg, Bg, _NT)
            dCB = jnp.zeros((L, L), F32)
            dB = jnp.zeros((L, SSD_N), F32)
            dC = jnp.zeros((L, SSD_N), F32)
            for pr in range(4):
                pair = g * 4 + pr
                psl = slice(pair * 128, (pair + 1) * 128)
                dY, Xp, Xdp, dop, dsp = dy_ref[:, psl], X[:, psl], Xd[:, psl], dec_out[:, psl], dec_st[:, psl]
                Xdb = Xdp.astype(BF16)
                acc_scr[0:1, psl] += jnp.sum(dY * Xp, 0, keepdims=True)
                Hp = hp_ref[0, psl, :]
                Hb = Hp.astype(BF16)
                dW = (dY * dop).astype(BF16)
                dcx = dY * _dot(Cg, Hb, _NT) * dop
                dC += _dot(dW, Hb, _NN)
                dHp = _dot(dW, Cg, _TN)
                dHn = dh_scr[psl, :]
                cd = _pair_decay(cs, pair)
                dh_scr[psl, :] = dHp + dHn * cd
                rsum = jnp.sum(dHn * Hp * cd, -1, keepdims=True)
                half = _iota((128, 1), 0) < 64
                s0 = jnp.sum(jnp.where(half, rsum, 0.0), 0, keepdims=True)
                s1 = jnp.sum(jnp.where(half, 0.0, rsum), 0, keepdims=True)
                lane1 = _iota((1, 128), 1)
                dcs_last += jnp.where(lane1 == 2 * pair, s0, 0.0) + jnp.where(lane1 == 2 * pair + 1, s1, 0.0)
                dHb = dHn.astype(BF16)
                dXs = _dot(Bg, dHb, _NT)
                dB += _dot((Xdp * dsp).astype(BF16), dHb, _NN)
                dXd = dXs * dsp
                e_st = dXs * Xdp * dsp
                dcx -= e_st
                dlastx.append(jnp.sum(e_st, 0, keepdims=True))
                for i in range(2):
                    h = 2 * pair + i
                    Dm = _ssd_decay(causal, cs, cs_row, h)
                    M = CB * Dm
                    dYm = jnp.where((lane < 64) if i == 0 else (lane >= 64), dY, 0.0).astype(BF16)
                    dM = _dot(dYm, Xdb, _NT)
                    dXd += _dot(M.astype(BF16), dYm, _TN)
                    dCB += dM * Dm
                    Em = dM * M
                    dcs_col += jnp.where(lane == h, jnp.sum(Em, -1, keepdims=True), 0.0)
                    dcs_row += jnp.where(sub == h, jnp.sum(Em, 0, keepdims=True), 0.0)
                dxc_ref[:, psl] = dY * dx_ref[:, psl] + dXd * dtx[:, psl]
                ddtx.append(dXd * Xp)
                dcsx.append(dcx)
            dCBb = dCB.astype(BF16)
            b0, c0 = SSD_INNER + g * SSD_N, SSD_INNER + 2 * SSD_N + g * SSD_N
            dxc_ref[:, b0:b0 + SSD_N] = dB + _dot(dCBb, Cg, _TN)
            dxc_ref[:, c0:c0 + SSD_N] = dC + _dot(dCBb, Bg, _NN)
        ET = et_ref[...]
        dcs = dcs_col - dcs_row.T + _dot_hi(jnp.concatenate(dcsx, 1), ET)
        dlast = dcs_last + _dot_hi(jnp.broadcast_to(jnp.concatenate(dlastx, 1), (8, SSD_INNER)), ET)[0:1, :]
        dcs += jnp.where(_iota((L, 128), 0) == L - 1, dlast, 0.0)
        dda = _dot_hi((_iota((L, L), 1) >= _iota((L, L), 0)).astype(F32), dcs)
        ddtp = dda * a + _dot_hi(jnp.concatenate(ddtx, 1), ET)
        draw = ddtp * _sigmoid(dtr_ref[...] + dtb_ref[...])
        ddt_ref[...] = draw.astype(BF16)
        st_ref[0:1, :] += jnp.sum(draw, 0, keepdims=True)
        st_ref[1:2, :] += jnp.sum(dda * dtp, 0, keepdims=True) * a

        @pl.when(step == nc - 1)
        def _():
            st_ref[2:3, :] = _dot_hi(acc_scr[...], ET)[0:1, :]

    rev = lambda c: (nc - 1 - c, 0)
    return pl.pallas_call(
        body, grid=(nc,),
        in_specs=_ssd_in_specs(nc, True) + [pl.BlockSpec((SSD_INNER, 128), lambda c: (0, 0)),
                                            pl.BlockSpec((1, SSD_INNER, SSD_N), lambda c: (nc - 1 - c, 0, 0)),
                                            pl.BlockSpec((L, SSD_INNER), rev), _ANY],
        out_specs=(pl.BlockSpec((L, SSD_CONV_DIM), rev), pl.BlockSpec((L, 128), lambda c: (nc - 1 - c, ODT // 128)),
                   pl.BlockSpec((8, 128), lambda c: (0, 0))),
        out_shape=(_sds((S, SSD_CONV_DIM)), _sds(dproj.shape, BF16), _sds((8, 128))), input_output_aliases={9: 1},
        scratch_shapes=[pltpu.VMEM((SSD_INNER, SSD_N), F32), pltpu.VMEM((8, SSD_INNER), F32)],
        name="ssd_core_bwd", compiler_params=_ARB)(xc, proj, dtb, alog, dskip, E, ET, hprev, dy, dproj)


def _ssd_post(y, proj, nw, tm=256):
    S = y.shape[0]

    def body(y_ref, z_ref, nw_ref, o_ref):
        for g in range(2):
            sl = slice(g * 512, (g + 1) * 512)
            gated = y_ref[:, sl] * _silu(z_ref[:, sl])
            r = lax.rsqrt(jnp.mean(gated * gated, -1, keepdims=True) + EPS)
            o_ref[:, sl] = (gated * r * nw_ref[:, sl]).astype(BF16)

    return pl.pallas_call(
        body, grid=(S // tm,), in_specs=[_row_spec(tm, D), pl.BlockSpec((tm, D), lambda i: (i, OZ // D)), _vec_spec(D)],
        out_specs=_row_spec(tm, D), out_shape=_sds((S, D), BF16), name="ssd_post", compiler_params=_PAR)(y, proj, nw)


def _ssd_post_bwd(doc, y, proj, nw, dproj, tm=256):
    S = y.shape[0]

    def body(d_ref, y_ref, z_ref, nw_ref, dproj_in, dy_ref, dz_ref, st_ref):
        @pl.when(pl.program_id(0) == 0)
        def _():
            st_ref[...] = jnp.zeros_like(st_ref)

        for g in range(2):
            sl = slice(g * 512, (g + 1) * 512)
            yv, zv, dv = y_ref[:, sl], z_ref[:, sl], d_ref[:, sl]
            sz = _silu(zv)
            gated = yv * sz
            r = lax.rsqrt(jnp.mean(gated * gated, -1, keepdims=True) + EPS)
            ghat = gated * r
            st_ref[0:1, sl] += jnp.sum(dv * ghat, 0, keepdims=True)
            gg = dv * nw_ref[:, sl]
            dg = r * (gg - ghat * jnp.mean(gg * ghat, -1, keepdims=True))
            dy_ref[:, sl] = dg * sz
            dz_ref[:, sl] = (dg * yv * _dsilu(zv)).astype(BF16)

    zs = pl.BlockSpec((tm, D), lambda i: (i, OZ // D))
    return pl.pallas_call(
        body, grid=(S // tm,), in_specs=[_row_spec(tm, D), _row_spec(tm, D), zs, _vec_spec(D), _ANY],
        out_specs=(_row_spec(tm, D), zs, _vec_spec(D, 8)), out_shape=(_sds((S, D)), _sds(dproj.shape, BF16), _sds((8, D))),
        input_output_aliases={4: 1}, name="ssd_post_bwd", compiler_params=_ARB)(doc, y, proj, nw, dproj)


def _row(v, n=None):
    v = v.astype(F32).reshape(1, -1)
    return v if n is None else jnp.pad(v, ((0, 0), (0, n - v.shape[1])))


_IN_SEGMENTS = [(0, 384, OQ), (384, 640, OKV), (640, 672, OKV + 320), (672, 1184, OP), (1184, 2208, OZ), (2208, 3744, OX),
                (3744, 3760, ODT), (3760, IN_DIM, OG)]
_IN_ZEROS = [(OKV + 256, OKV + 320), (OKV + 352, OKV + 384), (ODT + 16, ODT + 128)]


def _in_pieces():
    w, out = IN_DIM // 4, []
    for a, b, d in _IN_SEGMENTS:
        while a < b:
            k = a // w
            e = min(b, (k + 1) * w)
            out.append((k, a - k * w, e - k * w, d))
            d, a = d + e - a, e
    return out


def _win_layout(w_in4, tm=256):
    def body(w_ref, o_ref):
        for k, s0, s1, d in _in_pieces():
            o_ref[:, d:d + s1 - s0] = w_ref[k, :, s0:s1]
        for z0, z1 in _IN_ZEROS:
            o_ref[:, z0:z1] = jnp.zeros((tm, z1 - z0), o_ref.dtype)

    return pl.pallas_call(
        body, grid=(D // tm,), in_specs=[pl.BlockSpec((4, tm, IN_DIM // 4), lambda i: (0, i, 0))],
        out_specs=pl.BlockSpec((tm, IN_PAD), lambda i: (i, 0)), out_shape=_sds((D, IN_PAD), w_in4.dtype), name="win_layout",
        compiler_params=_PAR)(w_in4)


def _win_unlayout(dwin, tm=256):
    def body(d_ref, o_ref):
        for k, s0, s1, d in _in_pieces():
            o_ref[k, :, s0:s1] = d_ref[:, d:d + s1 - s0]

    return pl.pallas_call(
        body, grid=(D // tm,), in_specs=[pl.BlockSpec((tm, IN_PAD), lambda i: (i, 0))],
        out_specs=pl.BlockSpec((4, tm, IN_DIM // 4), lambda i: (0, i, 0)), out_shape=_sds((4, D, IN_DIM // 4), dwin.dtype),
        name="win_unlayout", compiler_params=_PAR)(dwin)


def _prep_late(p):
    return dict(wbr=p["w_branch"].astype(BF16), wo=p["w_out"].astype(BF16), wup4=p["ffn_up4"].astype(BF16), wdn=p["ffn_down"].astype(BF16))


def _prep_layer(p, late=None):
    win = _win_layout(p["w_in4"].astype(BF16))
    wqb = jnp.transpose(p["w_q_b4"].astype(BF16).reshape(4, Q_RANK, 2, 96), (1, 0, 2, 3))
    return dict(
        win=win, wqb=jnp.pad(wqb, ((0, 0), (0, 0), (0, 0), (0, 32))).reshape(Q_RANK, D), wkvb4=p["w_kv_b4"].astype(BF16),
        late=late if late is not None else (lambda after: _prep_late(p)),
        nw1=_row(p["norm1_w"]), nw2=_row(p["norm2_w"]), qan=_row(p["q_a_norm"]), kvan=_row(p["kv_a_norm"]),
        wq=_row(p["q_norm"], 128), wk=_row(p["k_norm"], 128), pool_w=p["pool_w"].astype(F32), pool_scale=_row(p["pool_scale"]),
        cw=p["ssd_conv_w"].astype(F32), cb=_row(p["ssd_conv_b"]), dtb=_row(p["ssd_dt_bias"], 128), alog=_row(p["ssd_a_log"], 128),
        dskip=_row(jnp.repeat(p["ssd_d"].astype(F32), SSD_P)), snw=_row(p["ssd_norm_w"]),
        fcw=p["ffn_conv_w"].astype(F32), fcb=_row(p["ffn_conv_b"]))


def _layer_fwd(x, mod8, W, rope, E, tag):
    sh1, sc1, g1, sh2, sc2, g2 = (mod8[i:i + 1] for i in range(6))
    h1 = _ln_mod(x, W["nw1"], sc1, sh1, name=f"ln1_{tag}")
    proj = _mm(h1, W["win"], tn=640, tk=1024, name=f"proj_{tag}")
    qf, kf, ve = _mla_prep(proj, W["wqb"], W["wkvb4"], W["qan"], W["kvan"], W["wq"], W["wk"], rope)
    oa = _attn_fwd(qf, kf, ve)
    ob = _pool_fwd(proj, W["pool_w"], W["pool_scale"])
    xc = _ssd_pre(proj, W["cw"], W["cb"])
    y, hprev = _ssd_core(xc, proj, W["dtb"], W["alog"], W["dskip"], E)
    oc = _ssd_post(y, proj, W["snw"])
    W.update(W["late"](oc))
    merged = _merge_fwd(oa, ob, oc, proj, W["wbr"])
    x1, out1 = _mm(merged, W["wo"], tk=1024, res=x, gate=g1, name=f"wout_{tag}")
    h2 = _ln_mod(x1, W["nw2"], sc2, sh2, name=f"ln2_{tag}")
    up = _up_fwd(h2, W["wup4"], name=f"up_{tag}")
    act = _ffn_act(up, W["fcw"], W["fcb"])
    x2, out2 = _mm(act, W["wdn"], tm=1024, res=x1, gate=g2, name=f"down_{tag}")
    saved = dict(x=x, h1=h1, proj=proj, qf=qf, kf=kf, ve=ve, oa=oa, ob=ob, oc=oc, xc=xc, hprev=hprev, y=y, merged=merged,
                 out1=out1, x1=x1, h2=h2, up=up, act=act, out2=out2)
    return x2, saved


def _layer_bwd(dx2, sv, mod8, W, rope, E, ET, tag, emit=None, mid=None):
    sc1, g1, sc2, g2 = mod8[1:2], mod8[2:3], mod8[4:5], mod8[5:6]
    proj = sv["proj"]
    dz2, dg2 = _gate_bwd(dx2, sv["out2"], g2, name=f"gate2_bwd_{tag}")
    dact = _mm(dz2, W["wdn"], "nt", tn=1408, tk=1024, name=f"down_dx_{tag}")
    dwdn = _mm(sv["act"], dz2, "tn", tm=1408, name=f"down_dw_{tag}")
    dup2, dfcw, dfcb = _ffn_act_bwd(sv["up"], dact, W["fcw"], W["fcb"])
    dh2 = _up_dx(dup2, W["wup4"], name=f"up_dx_{tag}")
    dwup4 = _up_dw(sv["h2"], dup2, name=f"up_dw_{tag}")
    dx1, st2 = _ln_mod_bwd(sv["x1"], dh2, dx2, W["nw2"], sc2, name=f"ln2_bwd_{tag}")
    dz1, dg1 = _gate_bwd(dx1, sv["out1"], g1, name=f"gate1_bwd_{tag}")
    dmerged = _mm(dz1, W["wo"], "nt", tk=1024, name=f"wout_dx_{tag}")
    dwo = _mm(sv["merged"], dz1, "tn", name=f"wout_dw_{tag}")
    dya, dyb, dyc, dproj, doa, dob, doc = _merge_bwd(dmerged, sv["oa"], sv["ob"], sv["oc"], proj, W["wbr"])
    dwba = _mm(sv["oa"], dya, "tn", name=f"wba_dw_{tag}")
    dwbb = _mm(sv["ob"], dyb, "tn", name=f"wbb_dw_{tag}")
    dwbc = _mm(sv["oc"], dyc, "tn", name=f"wbc_dw_{tag}")
    late = dict(w_branch=jnp.concatenate([dwba, dwbb, dwbc], 0).reshape(4, 512, D), w_out=dwo.reshape(4, 256, D), ffn_up=dwup4,
                ffn_down=dwdn.reshape(4, FFN // 4, D))
    snw = W["snw"]
    if emit is not None:
        token = emit(late)
        if token is not None:
            snw, doa = snw + token, doa + token
    dy, dproj, st_post = _ssd_post_bwd(doc, sv["y"], proj, snw, dproj)
    dxc, dproj, st_ssd = _ssd_core_bwd(sv["xc"], proj, sv["hprev"], dy, W["dtb"], W["alog"], W["dskip"], E, ET, dproj)
    dproj, dcw, dcb = _ssd_pre_bwd(proj, dxc, W["cw"], W["cb"], dproj)
    if mid is not None:
        token = mid(dcb)
        if token is not None:
            doa, dob = doa + token, dob + token
    dproj, dpw, dps = _pool_bwd(proj, dob, W["pool_w"], W["pool_scale"], dproj)
    dqf, dkf, dve = _attn_bwd(sv["qf"], sv["kf"], sv["ve"], doa)
    dproj, dwqb, dwkvb4, st_mla = _mla_prep_bwd(proj, dqf, dkf, dve, W["wqb"], W["wkvb4"], W["qan"], W["kvan"], W["wq"], W["wk"], rope, dproj)
    dh1 = _mm(dproj, W["win"], "nt", tk=1408, name=f"proj_dx_{tag}")
    dwin = _mm(sv["h1"], dproj, "tn", tn=640, name=f"proj_dw_{tag}")
    dx, st1 = _ln_mod_bwd(sv["x"], dh1, dx1, W["nw1"], sc1, name=f"ln1_bwd_{tag}")
    grads = dict(
        norm1_w=st1[2], norm2_w=st2[2], w_in=_win_unlayout(dwin),
        q_a_norm=st_mla[0, :Q_RANK], kv_a_norm=st_mla[1, :KV_RANK], q_norm=st_mla[2, :96], k_norm=st_mla[3, :96],
        w_q_b=jnp.transpose(dwqb.reshape(Q_RANK, 4, 2, 128)[:, :, :, :96], (1, 0, 2, 3)).reshape(4, Q_RANK, 192), w_kv_b=dwkvb4,
        pool_w=dpw, pool_scale=dps[0], ssd_conv_w=dcw, ssd_conv_b=dcb[0],
        ssd_dt_bias=st_ssd[0, :SSD_HEADS], ssd_a_log=st_ssd[1, :SSD_HEADS], ssd_d=st_ssd[2, :SSD_HEADS], ssd_norm_w=st_post[0],
        ffn_conv_w=jnp.transpose(dfcw, (1, 0, 2)).reshape(3, 2 * FFN), ffn_conv_b=dfcb.reshape(2 * FFN), **late)
    dmod = jnp.concatenate([st1[0:2], dg1[0:1], st2[0:2], dg2[0:1]], 0)
    return dx, grads, dmod


def _ssd_expand():
    E = (jnp.arange(SSD_INNER)[None, :] // SSD_P == jnp.arange(128)[:, None]).astype(F32)
    return E, E.T


def _rope_tables(positions):
    inv_freq = ROPE_THETA ** (-jnp.arange(0, ROPE_DIM, 2, dtype=F32) / ROPE_DIM)
    invf = jnp.concatenate([jnp.zeros((NOPE,), F32), inv_freq, inv_freq, jnp.zeros((32,), F32)]).reshape(1, 128)
    posb = jnp.broadcast_to(positions.astype(F32)[:, None], (positions.shape[0], 128))
    return _rope_tab(posb, invf)


def _local_step(x, target, positions, mods, get_layer, bwd_mod=None, emit=None, mid=None, done=None):
    rope = _rope_tables(positions)
    E, ET = _ssd_expand()
    Ws, saved, h = [], [], x
    for l in range(2):
        Ws.append(_prep_layer(*get_layer(l, h)))
        h, sv = _layer_fwd(h, mods[l], Ws[l], rope, E, l)
        saved.append(sv)
    dy, lpart = _loss_grad(h, target)
    grads, dmods = [None, None], [None, None]
    for l in (1, 0):
        mod8 = mods[l] if bwd_mod is None else bwd_mod(l)
        hook = lambda f: None if f is None else functools.partial(f, l)
        dy, grads[l], dmods[l] = _layer_bwd(dy, saved[l], mod8, Ws[l], rope, E, ET, l, hook(emit), hook(mid))
        if done is not None:
            token = done(l, grads[l])
            if token is not None:
                dy = dy + token
    return lpart[0, 0], dy, grads, dmods


_ANY = pl.BlockSpec(memory_space=pl.ANY)
_VMEM = pl.BlockSpec(memory_space=pltpu.VMEM)


def _place():
    x, y, c = lax.axis_index("x"), lax.axis_index("y"), lax.axis_index("c")
    return x, y, c, [(1 - x, y), (x, 1 - y), (1 - x, 1 - y)]


def _allgather8(v, name):
    m_per, n = v.shape

    def body(x_ref, out_ref, send_sems, recv_sems, local_sem):
        x, y, c, chips = _place()
        me, sibling = (x, y, c), (x, y, 1 - c)

        def rows(px, py, pc):
            return out_ref.at[pl.ds((4 * px + 2 * py + pc) * m_per, m_per), :]

        def copy(k, block, to, src=None):
            return pltpu.make_async_remote_copy(src_ref=rows(*block) if src is None else src, dst_ref=rows(*block),
                                                send_sem=send_sems.at[k], recv_sem=recv_sems.at[k], device_id=to, device_id_type=MESH)

        mine = pltpu.make_async_copy(x_ref, rows(*me), local_sem)
        mine.start()
        first = [copy(0, me, sibling, src=x_ref)] + [copy(1 + j, me, (*chip, c), src=x_ref) for j, chip in enumerate(chips)]
        for cp in first:
            cp.start()
        passed = [copy(4 + j, (*chip, c), sibling) for j, chip in enumerate(chips)]
        for j, chip in enumerate(chips):
            copy(1 + j, (*chip, c), me).wait_recv()
            passed[j].start()
        copy(0, sibling, me).wait_recv()
        for j, chip in enumerate(chips):
            copy(4 + j, (*chip, 1 - c), me).wait_recv()
        for cp in first + passed:
            cp.wait_send()
        mine.wait()

    return pl.pallas_call(
        body, out_shape=_sds((8 * m_per, n), v.dtype), in_specs=[_VMEM], out_specs=_VMEM,
        scratch_shapes=[pltpu.SemaphoreType.DMA((7,)), pltpu.SemaphoreType.DMA((7,)), pltpu.SemaphoreType.DMA], name=name)(v)


def _sems(n):
    return [pltpu.SemaphoreType.DMA((n,)), pltpu.SemaphoreType.DMA((n,))]


_HBM = pl.BlockSpec(memory_space=pltpu.HBM)
_SEM = pl.BlockSpec(memory_space=pltpu.SEMAPHORE)
_EFFECT = pltpu.CompilerParams(has_side_effects=pltpu.SideEffectType.DATAFLOW_SIDE_EFFECTING)


def _ici_copy(src_refs, land_refs, send_sems, recv_sems, a, j, slices, incoming):
    x, y, c, chips = _place()
    if _peers(slices) == 1:
        src, dst = slices(src_refs[a], land_refs[a], 1 - c if incoming else c)
        return pltpu.make_async_remote_copy(src_ref=src, dst_ref=dst, send_sem=send_sems.at[a], recv_sem=recv_sems.at[a],
                                            device_id=(x, y, 1 - c), device_id_type=MESH)
    me, other = 2 * x + y, 2 * chips[j][0] + chips[j][1]
    src, dst = slices(src_refs[a], land_refs[a], other, me, c) if incoming else slices(src_refs[a], land_refs[a], me, other, c)
    return pltpu.make_async_remote_copy(src_ref=src, dst_ref=dst, send_sem=send_sems.at[3 * a + j], recv_sem=recv_sems.at[3 * a + j],
                                        device_id=(*chips[j], c), device_id_type=MESH)


def _peers(slices):
    return 1 if slices is _halves_slices else 3


def _ici_start(srcs, land_shapes, slices, after, name):
    na = len(srcs)

    def body(*refs):
        src_refs, land_refs, send_sems, recv_sems = refs[:na], refs[na:2 * na], refs[2 * na + 1], refs[2 * na + 2]
        for a in range(na):
            for j in range(_peers(slices)):
                _ici_copy(src_refs, land_refs, send_sems, recv_sems, a, j, slices, False).start()
        refs[-1][...] = jnp.zeros_like(refs[-1])

    hbm = lambda v: pltpu.with_memory_space_constraint(v, pltpu.HBM)
    lands = [hbm(lax.empty(s.shape, s.dtype)) for s in land_shapes]
    return pl.pallas_call(
        body, name=name,
        out_shape=(pltpu.SemaphoreType.DMA((_peers(slices) * na,)), pltpu.SemaphoreType.DMA((_peers(slices) * na,)),
                   *[pltpu.HBM(v.shape, v.dtype) for v in srcs],
                   *[pltpu.HBM(s.shape, s.dtype) for s in land_shapes], _sds((8, 128))),
        in_specs=[_HBM] * (2 * na) + [_ANY], out_specs=(_SEM, _SEM, *[_HBM] * (2 * na), _VMEM),
        input_output_aliases={i: 2 + i for i in range(2 * na)}, compiler_params=_EFFECT)(*[hbm(v) for v in srcs], *lands, after)


def _ici_wait(handle, slices, after, name):
    na = (len(handle) - 3) // 2

    def body(*refs):
        src_refs, land_refs, send_sems, recv_sems = refs[:na], refs[na:2 * na], refs[2 * na], refs[2 * na + 1]
        for a in range(na):
            for j in range(_peers(slices)):
                _ici_copy(src_refs, land_refs, send_sems, recv_sems, a, j, slices, False).wait_send()
                _ici_copy(src_refs, land_refs, send_sems, recv_sems, a, j, slices, True).wait_recv()

    thru = handle[2:2 + 2 * na]
    outs = pl.pallas_call(
        body, name=name, out_shape=[pltpu.HBM(v.shape, v.dtype) for v in thru], in_specs=[_HBM] * (2 * na) + [_SEM, _SEM, _ANY],
        out_specs=[_HBM] * (2 * na), input_output_aliases={i: i for i in range(2 * na)}, compiler_params=_EFFECT)(
        *thru, handle[0], handle[1], after)
    return outs[:na], outs[na:]


def _gather_slices(p_ref, land_ref, sender, receiver, c):
    r2 = p_ref.shape[0] // 2
    return p_ref.at[pl.ds(c * r2, r2), :], land_ref.at[sender, pl.ds(c * r2, r2), :]


def _scatter_slices(a_ref, t_ref, sender, receiver, c):
    return a_ref.at[receiver], t_ref.at[sender]


def _halves_slices(g_ref, land_ref, sender_c):
    r2 = g_ref.shape[1] // 2
    return g_ref.at[:, pl.ds((1 - sender_c) * r2, r2), :], land_ref


def _gather_start(arrs, after, name):
    return _ici_start(arrs, [_sds((4,) + v.shape, v.dtype) for v in arrs], _gather_slices, after, name)


def _gather_finish(handle, after, name):
    arrs, stacks = _ici_wait(handle, _gather_slices, after, name + "_wait")
    na = len(stacks)

    def body(*refs):
        s_refs, o_refs, (send_sems, recv_sems) = refs[:na], refs[na:2 * na], refs[2 * na:]
        x, y, c, chips = _place()

        def copy(a, j, cc, to):
            r2 = s_refs[a].shape[1] // 2
            at = (2 * chips[j][0] + chips[j][1], pl.ds(cc * r2, r2), slice(None))
            return pltpu.make_async_remote_copy(src_ref=s_refs[a].at[at], dst_ref=o_refs[a].at[at], send_sem=send_sems.at[3 * a + j],
                                                recv_sem=recv_sems.at[3 * a + j], device_id=to, device_id_type=MESH)

        passed = [copy(a, j, c, (x, y, 1 - c)) for a in range(na) for j in range(3)]
        for cp in passed:
            cp.start()
        for a in range(na):
            for j in range(3):
                copy(a, j, 1 - c, (x, y, c)).wait_recv()
        for cp in passed:
            cp.wait_send()

    stacks = pl.pallas_call(
        body, out_shape=[_sds(v.shape, v.dtype) for v in stacks], in_specs=[_ANY] * na, out_specs=[_ANY] * na,
        input_output_aliases={i: i for i in range(na)}, scratch_shapes=_sems(3 * na), name=name + "_pass")(*stacks)
    chip = 2 * lax.axis_index("x") + lax.axis_index("y")
    return [lax.dynamic_update_slice(s, v[None], (chip, 0, 0)) for s, v in zip(stacks, arrs)]


def _halves_start(gs, after, tag):
    return _ici_start(gs, [_sds((4, v.shape[1] // 2, v.shape[2]), v.dtype) for v in gs], _halves_slices, after, f"rs_halves_{tag}_start")


def _join_halves(fs, name):
    na = len(fs)

    def body(*refs):
        f_refs, o_refs, (send_sems, recv_sems) = refs[:na], refs[na:2 * na], refs[2 * na:]
        x, y, c, _ = _place()

        def copy(a, cc, to):
            r2 = f_refs[a].shape[0] // 2
            return pltpu.make_async_remote_copy(src_ref=f_refs[a].at[pl.ds(cc * r2, r2), :], dst_ref=o_refs[a].at[pl.ds(cc * r2, r2), :],
                                                send_sem=send_sems.at[a], recv_sem=recv_sems.at[a], device_id=to, device_id_type=MESH)

        cps = [copy(a, c, (x, y, 1 - c)) for a in range(na)]
        for cp in cps:
            cp.start()
        for a in range(na):
            copy(a, 1 - c, (x, y, c)).wait_recv()
        for cp in cps:
            cp.wait_send()

    return pl.pallas_call(
        body, out_shape=[_sds(v.shape, v.dtype) for v in fs], in_specs=[_ANY] * na, out_specs=[_ANY] * na,
        input_output_aliases={i: i for i in range(na)}, scratch_shapes=_sems(na), name=name)(*fs)


def _sum_chips(a, t, chip, ci, name):
    _, r2, n = t.shape
    tm = _row_tile(r2)
    nb = r2 // tm

    def body(k_ref, a_ref, t1_ref, t2_ref, t3_ref, o_ref):
        o_ref[...] = ((a_ref[...].astype(F32) + t1_ref[...].astype(F32)) + t2_ref[...].astype(F32)) + t3_ref[...].astype(F32)

    def slot(j):
        return pl.BlockSpec((None, tm, n), lambda i, k_ref: (lax.rem(k_ref[0] + j, 4), i, 0))

    return pl.pallas_call(
        body, grid_spec=pltpu.PrefetchScalarGridSpec(num_scalar_prefetch=1, grid=(nb,), in_specs=[slot(0), slot(1), slot(2), slot(3)],
                                                     out_specs=pl.BlockSpec((tm, n), lambda i, k_ref: (k_ref[1] * nb + i, 0))),
        out_shape=_sds((2 * r2, n)), name=name, compiler_params=_PAR)(jnp.stack([chip, ci]).astype(jnp.int32), a, t, t, t)


def _add_cast(g, recv, c, name):
    _, r2, n = recv.shape

    def body(c_ref, a_ref, b_ref, o_ref):
        o_ref[...] = (a_ref[...] + b_ref[...]).astype(BF16)

    spec = pl.BlockSpec((None, r2, n), lambda k, c_ref: (k, 0, 0))
    return pl.pallas_call(
        body, grid_spec=pltpu.PrefetchScalarGridSpec(
            num_scalar_prefetch=1, grid=(4,), in_specs=[pl.BlockSpec((None, r2, n), lambda k, c_ref: (k, c_ref[0], 0)), spec], out_specs=spec),
        out_shape=_sds(recv.shape, BF16), name=name, compiler_params=_PAR)(c.reshape(1).astype(jnp.int32), g, recv)


def _sum_lead(t, name, tm=256):
    P, R, n = t.shape
    tm = _row_tile(R, tm)

    def body(t_ref, o_ref):
        acc = t_ref[0].astype(F32)
        for j in range(1, P):
            acc = acc + t_ref[j].astype(F32)
        o_ref[...] = acc

    return pl.pallas_call(body, grid=(R // tm,), in_specs=[pl.BlockSpec((P, tm, n), lambda i: (0, i, 0))],
                          out_specs=pl.BlockSpec((tm, n), lambda i: (i, 0)), out_shape=_sds((R, n)), name=name, compiler_params=_PAR)(t)


def _ada_fwd(c16, ada_w, ada_b_cols, tn=512):
    L, _, n = ada_w.shape

    def body(c_ref, w_ref, b_ref, o_ref):
        o_ref[0] = _dot(_silu(c_ref[...]).astype(BF16), w_ref[0].astype(BF16), _NN) + b_ref[0]

    return pl.pallas_call(
        body, grid=(L, n // tn),
        in_specs=[pl.BlockSpec((16, D), lambda l, j: (0, 0)), pl.BlockSpec((1, D, tn), lambda l, j: (l, 0, j)), pl.BlockSpec((1, 1, tn), lambda l, j: (l, 0, j))],
        out_specs=pl.BlockSpec((1, 16, tn), lambda l, j: (l, 0, j)), out_shape=_sds((L, 16, n)), name="ada_fwd",
        compiler_params=pltpu.CompilerParams(dimension_semantics=("parallel", "parallel")))(c16, ada_w, ada_b_cols)


def _ada_bwd(c16, dmod, tn=512):
    L, _, n = dmod.shape

    def body(c_ref, d_ref, o_ref):
        o_ref[0] = _dot(_silu(c_ref[...]).astype(BF16), d_ref[0].astype(BF16), _TN)

    return pl.pallas_call(
        body, grid=(L, n // tn), in_specs=[pl.BlockSpec((16, D), lambda l, j: (0, 0)), pl.BlockSpec((1, 16, tn), lambda l, j: (l, 0, j))],
        out_specs=pl.BlockSpec((1, D, tn), lambda l, j: (l, 0, j)), out_shape=_sds((L, D, n)), name="ada_bwd",
        compiler_params=pltpu.CompilerParams(dimension_semantics=("parallel", "parallel")))(c16, dmod)


def _adam_math(w, g, m, v):
    mn = ADAM_B1 * m + (1.0 - ADAM_B1) * g
    vn = ADAM_B2 * v + (1.0 - ADAM_B2) * (g * g)
    m_hat = mn / (1.0 - ADAM_B1 ** ADAM_STEP)
    v_hat = vn / (1.0 - ADAM_B2 ** ADAM_STEP)
    return -ADAM_LR * (m_hat / (jnp.sqrt(v_hat) + ADAM_EPS) + ADAM_WD * w), mn, vn


def _adamw(w, g, m, v, name):
    R, n = w.shape
    tm = _row_tile(R)

    def body(w_ref, g_ref, m_ref, v_ref, d_ref, nm_ref, nv_ref):
        d_ref[...], nm_ref[...], nv_ref[...] = _adam_math(w_ref[...], g_ref[...], m_ref[...], v_ref[...])

    spec = pl.BlockSpec((tm, n), lambda i: (i, 0))
    return pl.pallas_call(body, grid=(R // tm,), in_specs=[spec] * 4, out_specs=(spec,) * 3, out_shape=(_sds((R, n)),) * 3,
                          name=name, compiler_params=_PAR)(w, g, m, v)


def _adamw_cols(w, g0, g1, m, v, name):
    fwd, back = (lambda t: jnp.transpose(t, (2, 0, 1))), (lambda t: jnp.transpose(t, (1, 2, 0)))
    gt = jnp.stack([g0.T, g1.T], 1)
    n, _, r = gt.shape
    tr = max(t for t in range(1, 257) if n % t == 0)

    def body(w_ref, g_ref, m_ref, v_ref, d_ref, nm_ref, nv_ref):
        d_ref[...], nm_ref[...], nv_ref[...] = _adam_math(w_ref[...], g_ref[...], m_ref[...], v_ref[...])

    spec = pl.BlockSpec((tr, 2, r), lambda i: (i, 0, 0))
    outs = pl.pallas_call(body, grid=(n // tr,), in_specs=[spec] * 4, out_specs=(spec,) * 3, out_shape=(_sds(gt.shape),) * 3,
                          name=name, compiler_params=_PAR)(fwd(w), gt, fwd(m), fwd(v))
    return (back(gt), *[back(o) for o in outs])


def _adamw_layers(w, g0, g1, m, v, after, name):
    _, r, n = w.shape
    tm = _row_tile(r)
    nb = r // tm

    def body(w_ref, g0_ref, g1_ref, m_ref, v_ref, after_ref, g_ref, d_ref, nm_ref, nv_ref):
        gv = jnp.where(pl.program_id(0) == 0, g0_ref[...], g1_ref[...])
        g_ref[...] = gv
        d_ref[...], nm_ref[...], nv_ref[...] = _adam_math(w_ref[...], gv, m_ref[...], v_ref[...])

    spec = pl.BlockSpec((None, tm, n), lambda l, i: (l, i, 0))
    g0_spec = pl.BlockSpec((tm, n), lambda l, i: (i * (1 - l) + (nb - 1) * l, 0))
    g1_spec = pl.BlockSpec((tm, n), lambda l, i: (i * l, 0))
    return pl.pallas_call(body, grid=(2, nb), in_specs=[spec, g0_spec, g1_spec, spec, spec, _ANY], out_specs=(spec,) * 4,
                          out_shape=(_sds(w.shape),) * 4, name=name,
                          compiler_params=pltpu.CompilerParams(dimension_semantics=("arbitrary", "arbitrary")))(w, g0, g1, m, v, after)


_W_NAMES = ["ada_w", "ada_b", "norm1_w", "w_in", "q_a_norm", "w_q_b", "kv_a_norm", "w_kv_b", "q_norm", "k_norm", "pool_w",
            "pool_scale", "ssd_conv_w", "ssd_conv_b", "ssd_dt_bias", "ssd_a_log", "ssd_d", "ssd_norm_w", "w_branch", "w_out",
            "norm2_w", "ffn_up", "ffn_conv_w", "ffn_conv_b", "ffn_down"]
_BIG = [("w_in", (D, IN_DIM // 4), 1), ("w_q_b", (Q_RANK, 192), 1), ("w_kv_b", (KV_RANK, 256), 1), ("w_branch", (512, D), 0),
        ("w_out", (256, D), 0), ("ffn_up", (D, 2 * FFN // 4), 1), ("ffn_down", (FFN // 4, D), 0)]

_SMALL = [("norm1_w", (D,)), ("q_a_norm", (Q_RANK,)), ("kv_a_norm", (KV_RANK,)), ("q_norm", (96,)), ("k_norm", (96,)),
          ("pool_w", (4, 128, 128)), ("pool_scale", (512,)), ("ssd_conv_w", (4, SSD_CONV_DIM)), ("ssd_conv_b", (SSD_CONV_DIM,)),
          ("ssd_dt_bias", (SSD_HEADS,)), ("ssd_a_log", (SSD_HEADS,)), ("ssd_d", (SSD_HEADS,)), ("ssd_norm_w", (D,)), ("norm2_w", (D,)),
          ("ffn_conv_w", (3, 2 * FFN)), ("ffn_conv_b", (2 * FFN,))]
_CONV_SHARDED = {"ssd_conv_w": SSD_CONV_DIM // 4, "ffn_conv_w": 2 * FFN // 4}


def _pack_flat(arrs, mult):
    flat = jnp.concatenate([a.astype(F32).reshape(-1) for a in arrs])
    rows = -(-flat.shape[0] // (128 * mult)) * mult
    return jnp.pad(flat, (0, rows * 128 - flat.shape[0])).reshape(rows, 128), [a.shape for a in arrs]


def _unpack_flat(packed, shapes):
    flat, out, off = packed.reshape(-1), [], 0
    for s in shapes:
        n = 1
        for d in s:
            n *= d
        out.append(flat[off:off + n].reshape(s))
        off += n
    return out


_EARLY = ["w_in", "w_q_b", "w_kv_b"]
_LATE = ["w_branch", "w_out", "ffn_up", "ffn_down"]


def _early_weights(a, l, stacks, conv_full):
    p = {n: a[n][l] for n in _W_NAMES if n not in ("ada_w", "ada_b")}
    p.update({n: conv_full[n][l] for n in conv_full})
    p.update(w_in4=stacks[0], w_q_b4=stacks[1], w_kv_b4=stacks[2])
    return p


def _late_weights(stacks):
    return _prep_late(dict(w_branch=stacks[0].reshape(2048, D), w_out=stacks[1].reshape(D, D), ffn_up4=stacks[2],
                           ffn_down=stacks[3].reshape(FFN, D)))


def _reduce_start(halves, ci, after, tag):
    chip_sum = []
    for t, h in halves.items():
        gs, recv = _ici_wait(h, _halves_slices, after, f"rs_halves_{t}_wait")
        chip_sum += [_add_cast(g, r, ci, f"rs_add_{t}") for g, r in zip(gs, recv)]
    return _ici_start(chip_sum, [_sds(v.shape, v.dtype) for v in chip_sum], _scatter_slices, after, f"rs_scatter_{tag}_start")


def _reduce_finish(started, after, tag):
    chip_sum, got = _ici_wait(started, _scatter_slices, after, f"rs_scatter_{tag}_wait")
    chip, ci = 2 * lax.axis_index("x") + lax.axis_index("y"), lax.axis_index("c")
    return _join_halves([_sum_chips(s, t, chip, ci, f"rs_sum_{tag}") for s, t in zip(chip_sum, got)], f"rs_join_{tag}")


def kernel(x, c, positions, ada_w, ada_b, norm1_w, w_in, q_a_norm, w_q_b, kv_a_norm, w_kv_b, q_norm, k_norm, pool_w, pool_scale, ssd_conv_w, ssd_conv_b, ssd_dt_bias, ssd_a_log, ssd_d, ssd_norm_w, w_branch, w_out, norm2_w, ffn_up, ffn_conv_w, ffn_conv_b, ffn_down, loss_target, m_ada_w, m_ada_b, m_norm1_w, m_w_in, m_q_a_norm, m_w_q_b, m_kv_a_norm, m_w_kv_b, m_q_norm, m_k_norm, m_pool_w, m_pool_scale, m_ssd_conv_w, m_ssd_conv_b, m_ssd_dt_bias, m_ssd_a_log, m_ssd_d, m_ssd_norm_w, m_w_branch, m_w_out, m_norm2_w, m_ffn_up, m_ffn_conv_w, m_ffn_conv_b, m_ffn_down, v_ada_w, v_ada_b, v_norm1_w, v_w_in, v_q_a_norm, v_w_q_b, v_kv_a_norm, v_w_kv_b, v_q_norm, v_k_norm, v_pool_w, v_pool_scale, v_ssd_conv_w, v_ssd_conv_b, v_ssd_dt_bias, v_ssd_a_log, v_ssd_d, v_ssd_norm_w, v_w_branch, v_w_out, v_norm2_w, v_ffn_up, v_ffn_conv_w, v_ffn_conv_b, v_ffn_down):
    a = dict(locals())
    xi, yi, ci = lax.axis_index("x"), lax.axis_index("y"), lax.axis_index("c")
    chip = 2 * xi + yi
    dev = 2 * chip + ci
    ncol = 6 * D // 4

    c_all = _allgather8(c.reshape(8, 128), "gather_c").reshape(8, D)
    c16 = jnp.pad(c_all, ((0, 8), (0, 0)))
    ada_b_cols = lax.dynamic_slice_in_dim(ada_b, chip * ncol, ncol, axis=1).reshape(2, 1, ncol)
    mod_part = _ada_fwd(c16, ada_w, ada_b_cols)[:, :8]
    small1, shapes1 = _pack_flat([mod_part, ssd_conv_w, ffn_conv_w], 8)
    got1 = _allgather8(small1, "gather_mod").reshape(8, -1, 128)
    per_chip = [_unpack_flat(got1[2 * k], shapes1) for k in range(4)]
    mod_all = jnp.concatenate([per_chip[k][0] for k in range(4)], -1)
    conv_full = {"ssd_conv_w": jnp.concatenate([per_chip[k][1] for k in range(4)], -1),
                 "ffn_conv_w": jnp.concatenate([per_chip[k][2] for k in range(4)], -1)}
    mod_mine = lax.dynamic_index_in_dim(mod_all, dev, axis=1, keepdims=False).reshape(2, 6, D)
    mods = [jnp.pad(mod_mine[l], ((0, 2), (0, 0))) for l in range(2)]

    big = _EARLY + _LATE
    shard = lambda names, l: [a[n][l].astype(BF16) for n in names]
    g0a = _gather_start(shard(_EARLY, 0), mods[0], "gather_0a")
    g0b = _gather_start(shard(_LATE, 0), g0a[-1], "gather_0b")
    g1 = _gather_start(shard(big, 1), g0b[-1], "gather_1")
    mods[0] = mods[0] + g1[-1][0, 0]

    def get_layer(l, after):
        if l == 0:
            return (_early_weights(a, 0, _gather_finish(g0a, mods[0], "gather_0a"), conv_full),
                    lambda aft: _late_weights(_gather_finish(g0b, aft, "gather_0b")))
        stacks = _gather_finish(g1, after, "gather_1")
        return _early_weights(a, 1, stacks[:3], conv_full), lambda aft: _late_weights(stacks[3:])

    halves, scatters = {}, {}

    def bwd_mod(l):
        return mods[l] if l == 1 else mods[0] + halves["1a"][-1][0, 0]

    def emit(l, late):
        if l == 1:
            halves["1b"] = _halves_start([late[n] for n in _LATE], late["ffn_down"], "1b")
            return halves["1b"][-1][0, 0]
        scatters["1"] = _reduce_start({"1b": halves["1b"], "1a": halves["1a"]}, ci, late["ffn_down"], "1")
        halves["0b"] = _halves_start([late[n] for n in _LATE], scatters["1"][-1], "0b")
        return halves["0b"][-1][0, 0]

    def mid(l, after):
        if l == 0:
            scatters["0b"] = _reduce_start({"0b": halves["0b"]}, ci, after, "0b")
            return scatters["0b"][-1][0, 0]

    def done(l, grads_l):
        halves[f"{l}a"] = _halves_start([grads_l[n] for n in _EARLY], halves[f"{l}b"][-1], f"{l}a")
        return halves[f"{l}a"][-1][0, 0]

    lpart, grad_x, grads, dmods = _local_step(x[0], loss_target[0], positions[0], mods, get_layer, bwd_mod, emit, mid, done)
    loss = lax.psum(lpart, ("x", "y", "c"))
    red1 = dict(zip(_LATE + _EARLY, _reduce_finish(scatters["1"], grad_x, "1")))
    red0b = _reduce_finish(scatters["0b"], red1["w_in"], "0b")
    scatters["0a"] = _reduce_start({"0a": halves["0a"]}, ci, red0b[0], "0a")

    small2, shapes2 = _pack_flat([jnp.stack(dmods)] + [grads[l][n] for l in range(2) for n, _ in _SMALL], 16)
    got2 = _allgather8(small2, "gather_small").reshape(8, -1, 128)
    tot = _unpack_flat(_sum_lead(got2, "sum_small"), shapes2)
    g = {"ada_b": tot[0].reshape(2, 6 * D)}
    for i, (n, _) in enumerate(_SMALL):
        g[n] = jnp.stack([tot[1 + i], tot[1 + len(_SMALL) + i]])
    for n, w in _CONV_SHARDED.items():
        g[n] = lax.dynamic_slice_in_dim(g[n], chip * w, w, axis=2)
    nd = 2 * 6 * D // 128
    dmod_all = jnp.transpose(got2[:, :nd].reshape(8, 2, 6 * D), (1, 0, 2))
    dmod_cols = lax.dynamic_slice_in_dim(jnp.pad(dmod_all, ((0, 0), (0, 8), (0, 0))), chip * ncol, ncol, axis=2)
    g["ada_w"] = _ada_bwd(c16, dmod_cols)

    delta, new_m, new_v = {}, {}, {}
    token = scatters["0a"][-1]
    for n, r0 in zip(_LATE, red0b):
        g[n], delta[n], new_m[n], new_v[n] = _adamw_layers(a[n], r0, red1[n], a["m_" + n], a["v_" + n], token, f"adamw_{n}")
    shp = ada_w.shape
    r2 = lambda t: t.reshape(-1, shp[-1])
    delta["ada_w"], new_m["ada_w"], new_v["ada_w"] = (
        o.reshape(shp) for o in _adamw(r2(ada_w), r2(g["ada_w"]), r2(m_ada_w), r2(v_ada_w), "adamw_ada_w"))
    behind = (delta[_LATE[-1]][0, 0, :1] + delta["ada_w"][0, 0, :1]).reshape(1)
    red0a = _reduce_finish(scatters["0a"], behind, "0a")
    for n, r0 in zip(_EARLY, red0a):
        if n == "w_in":
            g[n], delta[n], new_m[n], new_v[n] = _adamw_cols(a[n], r0, red1[n], a["m_" + n], a["v_" + n], f"adamw_{n}")
        else:
            g[n], delta[n], new_m[n], new_v[n] = _adamw_layers(a[n], r0, red1[n], a["m_" + n], a["v_" + n], token, f"adamw_{n}")
    rest = [n for n in _W_NAMES if n not in big and n != "ada_w"]
    packs = [_pack_flat([t[n] if pre is None else t[pre + n] for n in rest], 128)[0]
             for t, pre in ((a, None), (g, None), (a, "m_"), (a, "v_"))]
    rest_shapes = [a[n].shape for n in rest]
    outs = [_unpack_flat(o, rest_shapes) for o in _adamw(*packs, "adamw_rest")]
    for i, n in enumerate(rest):
        delta[n], new_m[n], new_v[n] = outs[0][i], outs[1][i], outs[2][i]

    return (loss, grad_x[None], *[g[n] for n in _W_NAMES], *[delta[n] for n in _W_NAMES],
            *[new_m[n] for n in _W_NAMES], *[new_v[n] for n in _W_NAMES])
```

```python
import functools

import jax
import jax.numpy as jnp
from jax import lax
from jax.experimental import pallas as pl
from jax.experimental.pallas import tpu as pltpu

F32 = jnp.float32
BF16 = jnp.bfloat16
MESH = pl.DeviceIdType.MESH
HI = lax.Precision.HIGHEST

D = 1024
N_HEADS = 8
NOPE, ROPE_DIM = 64, 32
Q_RANK, KV_RANK = 384, 256
POOL_WINDOWS = (2, 4, 8, 16)
SSD_HEADS, SSD_P, SSD_N, SSD_L = 16, 64, 128, 128
SSD_INNER = 1024
SSD_CONV_DIM = 1536
FFN = 2816
EPS = 1e-6
ROPE_THETA = 10000.0
OG, OZ, OX, OP, OQ, OKV, ODT, IN_PAD = 0, 3072, 4096, 5632, 6144, 6528, 6912, 7040
IN_DIM = 6832
ADAM_LR, ADAM_B1, ADAM_B2, ADAM_EPS, ADAM_WD, ADAM_STEP = 0.001, 0.9, 0.999, 1e-08, 0.01, 10

_ARB = pltpu.CompilerParams(dimension_semantics=("arbitrary",))
_PAR = pltpu.CompilerParams(dimension_semantics=("parallel",))


def _pick(n, pref):
    if n <= pref:
        return n
    best = None
    for t in range(128, pref + 1, 128):
        if n % t == 0:
            best = t
    assert best is not None, (n, pref)
    return best


def _row_tile(r, cap=256):
    best = None
    for t in range(16, min(r, cap) + 1, 16):
        if r % t == 0:
            best = t
    assert best is not None, r
    return best


def _sds(shape, dtype=F32):
    return jax.ShapeDtypeStruct(tuple(shape), dtype)


def _iota(shape, dim):
    return lax.broadcasted_iota(jnp.int32, shape, dim)


def _sigmoid(x):
    return 0.5 * jnp.tanh(0.5 * x) + 0.5


def _silu(x):
    return x * _sigmoid(x)


def _dsilu(x):
    s = _sigmoid(x)
    return s * (1.0 + x * (1.0 - s))


def _dot(a, b, dims):
    return lax.dot_general(a, b, (dims, ((), ())), preferred_element_type=F32)


_NN, _NT, _TN = ((1,), (0,)), ((1,), (1,)), ((0,), (0,))


def _dot_hi(a, b, dims=_NN):
    return lax.dot_general(a, b, (dims, ((), ())), preferred_element_type=F32, precision=HI)


def _shift_down(x, j):
    n = x.shape[0]
    return jnp.where(_iota(x.shape, 0) >= j, pltpu.roll(x, j, 0), 0.0)


def _shift_up(x, j):
    n = x.shape[0]
    return jnp.where(_iota(x.shape, 0) < n - j, pltpu.roll(x, n - j, 0), 0.0)


def _mm(a, b, mode="nn", out_dtype=F32, tm=2048, tn=512, tk=4096, res=None, gate=None, name="mm"):
    if mode == "nn":
        (M, K), (K2, N) = a.shape, b.shape
    elif mode == "nt":
        (M, K), (N, K2) = a.shape, b.shape
    else:
        (K, M), (K2, N) = a.shape, b.shape
    assert K == K2, (a.shape, b.shape, mode)
    tm, tn, tk = _pick(M, tm), _pick(N, tn), _pick(K, tk)
    nk = K // tk
    dims = {"nn": _NN, "nt": _NT, "tn": _TN}[mode]
    fused = res is not None

    def body(*refs):
        a_ref, b_ref = refs[:2]

        def finish(acc):
            if fused:
                r_ref, g_ref, o_ref, raw_ref = refs[2:6]
                raw_ref[...] = acc
                o_ref[...] = r_ref[...] + g_ref[...] * acc
            else:
                refs[2][...] = acc.astype(out_dtype)

        _mm_steps(a_ref, b_ref, dims, nk, refs[-1] if nk > 1 else None, finish)

    if mode == "nn":
        a_spec = pl.BlockSpec((tm, tk), lambda i, j, k: (i, k))
        b_spec = pl.BlockSpec((tk, tn), lambda i, j, k: (k, j))
    elif mode == "nt":
        a_spec = pl.BlockSpec((tm, tk), lambda i, j, k: (i, k))
        b_spec = pl.BlockSpec((tn, tk), lambda i, j, k: (j, k))
    else:
        a_spec = pl.BlockSpec((tk, tm), lambda i, j, k: (k, i))
        b_spec = pl.BlockSpec((tk, tn), lambda i, j, k: (k, j))
    o_spec = pl.BlockSpec((tm, tn), lambda i, j, k: (i, j))
    in_specs, args = [a_spec, b_spec], [a, b]
    out_shape, out_specs = _sds((M, N), out_dtype), o_spec
    if fused:
        in_specs += [o_spec, pl.BlockSpec((1, tn), lambda i, j, k: (0, j))]
        args += [res, gate]
        out_shape, out_specs = (_sds((M, N)), _sds((M, N))), (o_spec, o_spec)
    return pl.pallas_call(
        body, grid=(M // tm, N // tn, nk), in_specs=in_specs, out_specs=out_specs, out_shape=out_shape,
        scratch_shapes=[pltpu.VMEM((tm, tn), F32)] if nk > 1 else [], name=name,
        compiler_params=pltpu.CompilerParams(dimension_semantics=("parallel", "parallel", "arbitrary")),
    )(*args)


def _mm_steps(a_ref, b_ref, dims, nk, acc_ref, finish):
    part = _dot(a_ref[...].astype(BF16), b_ref[...].astype(BF16), dims)
    if nk == 1:
        finish(part)
        return
    k = pl.program_id(2)

    @pl.when(k == 0)
    def _():
        acc_ref[...] = part

    @pl.when(k > 0)
    def _():
        acc_ref[...] += part

    @pl.when(k == nk - 1)
    def _():
        finish(acc_ref[...])


def _mm_blocks(a, b, dims, grid, a_spec, b_spec, o_spec, out_shape, acc_shape, name):
    nk = grid[2]

    def body(a_ref, b_ref, o_ref, *scratch):
        def finish(acc):
            o_ref[...] = acc.astype(o_ref.dtype)

        _mm_steps(a_ref, b_ref, dims, nk, scratch[0] if nk > 1 else None, finish)

    return pl.pallas_call(
        body, grid=grid, in_specs=[a_spec, b_spec], out_specs=o_spec, out_shape=out_shape,
        scratch_shapes=[pltpu.VMEM(acc_shape, F32)] if nk > 1 else [], name=name,
        compiler_params=pltpu.CompilerParams(dimension_semantics=("parallel", "parallel", "arbitrary")),
    )(a, b)


_UP_SHARD = 2 * FFN // 4


def _up_fwd(h2, wup4, name, tm=2048):
    S = h2.shape[0]
    tm = min(tm, S)
    return _mm_blocks(h2, wup4, _NN, (S // tm, 4, 1), pl.BlockSpec((tm, D), lambda i, j, k: (i, 0)),
                      pl.BlockSpec((None, D, _UP_SHARD), lambda i, j, k: (j, 0, 0)), pl.BlockSpec((tm, _UP_SHARD), lambda i, j, k: (i, j)),
                      _sds((S, 2 * FFN)), (tm, _UP_SHARD), name)


def _up_dx(dup2, wup4, name, tm=2048, tn=512):
    S = dup2.shape[1]
    tm = min(tm, S)
    return _mm_blocks(dup2, wup4, _NT, (S // tm, D // tn, 4), pl.BlockSpec((None, tm, _UP_SHARD), lambda i, j, k: (lax.div(k, 2), i, lax.rem(k, 2))),
                      pl.BlockSpec((None, tn, _UP_SHARD), lambda i, j, k: (k, j, 0)), pl.BlockSpec((tm, tn), lambda i, j, k: (i, j)),
                      _sds((S, D)), (tm, tn), name)


def _up_dw(h2, dup2, name, tm=1024, tk=4096):
    S = h2.shape[0]
    tk = min(tk, S)
    return _mm_blocks(h2, dup2, _TN, (D // tm, 4, S // tk), pl.BlockSpec((tk, tm), lambda i, j, k: (k, i)),
                      pl.BlockSpec((None, tk, _UP_SHARD), lambda i, j, k: (lax.div(j, 2), k, lax.rem(j, 2))),
                      pl.BlockSpec((None, tm, _UP_SHARD), lambda i, j, k: (j, i, 0)), _sds((4, D, _UP_SHARD)), (tm, _UP_SHARD), name)


def _row_spec(tm, n):
    return pl.BlockSpec((tm, n), lambda i: (i, 0))


def _vec_spec(n, rows=1):
    return pl.BlockSpec((rows, n), lambda i: (0, 0))


def _ln_mod(x, nw, sc, sh, name, tm=256):
    S = x.shape[0]

    def body(x_ref, nw_ref, sc_ref, sh_ref, o_ref):
        xv = x_ref[...]
        r = lax.rsqrt(jnp.mean(xv * xv, -1, keepdims=True) + EPS)
        o_ref[...] = ((xv * r * nw_ref[...]) * (1.0 + sc_ref[...]) + sh_ref[...]).astype(BF16)

    return pl.pallas_call(
        body, grid=(S // tm,), in_specs=[_row_spec(tm, D)] + [_vec_spec(D)] * 3, out_specs=_row_spec(tm, D),
        out_shape=_sds((S, D), BF16), name=name, compiler_params=_PAR)(x, nw, sc, sh)


def _ln_mod_bwd(x, dh, dres, nw, sc, name, tm=256):
    S = x.shape[0]

    def body(x_ref, dh_ref, dres_ref, nw_ref, sc_ref, dx_ref, st_ref):
        @pl.when(pl.program_id(0) == 0)
        def _():
            st_ref[...] = jnp.zeros_like(st_ref)

        xv, dhv, nwv = x_ref[...], dh_ref[...], nw_ref[...]
        r = lax.rsqrt(jnp.mean(xv * xv, -1, keepdims=True) + EPS)
        xhat = xv * r
        dn = dhv * (1.0 + sc_ref[...])
        g = dn * nwv
        dx_ref[...] = dres_ref[...] + r * (g - xhat * jnp.mean(g * xhat, -1, keepdims=True))
        st_ref[0:1, :] += jnp.sum(dhv, 0, keepdims=True)
        st_ref[1:2, :] += jnp.sum(dhv * (xhat * nwv), 0, keepdims=True)
        st_ref[2:3, :] += jnp.sum(dn * xhat, 0, keepdims=True)

    return pl.pallas_call(
        body, grid=(S // tm,), in_specs=[_row_spec(tm, D)] * 3 + [_vec_spec(D)] * 2,
        out_specs=(_row_spec(tm, D), _vec_spec(D, 8)), out_shape=(_sds((S, D)), _sds((8, D))),
        name=name, compiler_params=_ARB)(x, dh, dres, nw, sc)


def _gate_bwd(dx, out, g, name, tm=256):
    S = dx.shape[0]

    def body(dx_ref, o_ref, g_ref, dz_ref, dg_ref):
        @pl.when(pl.program_id(0) == 0)
        def _():
            dg_ref[...] = jnp.zeros_like(dg_ref)

        dxv = dx_ref[...]
        dz_ref[...] = (dxv * g_ref[...]).astype(BF16)
        dg_ref[0:1, :] += jnp.sum(dxv * o_ref[...], 0, keepdims=True)

    return pl.pallas_call(
        body, grid=(S // tm,), in_specs=[_row_spec(tm, D)] * 2 + [_vec_spec(D)],
        out_specs=(_row_spec(tm, D), _vec_spec(D, 8)), out_shape=(_sds((S, D), BF16), _sds((8, D))),
        name=name, compiler_params=_ARB)(dx, out, g)


def _loss_grad(y, t, tm=256):
    S = y.shape[0]

    def body(y_ref, t_ref, dy_ref, l_ref):
        @pl.when(pl.program_id(0) == 0)
        def _():
            l_ref[...] = jnp.zeros_like(l_ref)

        e = y_ref[...] - t_ref[...]
        dy_ref[...] = e * (1.0 / D)
        l_ref[...] += 0.5 * jnp.sum(jnp.mean(e * e, -1, keepdims=True), 0, keepdims=True)

    return pl.pallas_call(
        body, grid=(S // tm,), in_specs=[_row_spec(tm, D)] * 2,
        out_specs=(_row_spec(tm, D), pl.BlockSpec((8, 128), lambda i: (0, 0))),
        out_shape=(_sds((S, D)), _sds((8, 128))), name="loss_grad", compiler_params=_ARB)(y, t)


_CONV_ROWS = 256
_HALO = 8


def _rows(ref, lo, hi):
    S, c = ref.shape
    parts = [jnp.zeros((-lo, c), F32)] if lo < 0 else []
    parts.append(ref[max(lo, 0):min(hi, S), :])
    if hi > S:
        parts.append(jnp.zeros((hi - S, c), F32))
    return parts[0] if len(parts) == 1 else jnp.concatenate(parts, 0)


def _conv_rows(ext, w, b, n):
    K = w.shape[0]
    acc = ext[_HALO:_HALO + n] * w[K - 1:K, :] + b
    for j in range(1, K):
        acc = acc + pltpu.roll(ext, j, 0)[_HALO:_HALO + n] * w[K - 1 - j:K - j, :]
    return acc


def _conv_rows_bwd(ext, w, dc, n):
    K, m = w.shape[0], dc.shape[0]
    d0 = dc[0:n]
    dx = d0 * w[K - 1:K, :]
    dws = [jnp.sum(d0 * ext[_HALO:_HALO + n], 0, keepdims=True)]
    for j in range(1, K):
        dx = dx + pltpu.roll(dc, m - j, 0)[0:n] * w[K - 1 - j:K - j, :]
        dws.append(jnp.sum(d0 * pltpu.roll(ext, j, 0)[_HALO:_HALO + n], 0, keepdims=True))
    return dx, dws[::-1], jnp.sum(d0, 0, keepdims=True)


def _col_spec(S, tc, off=0):
    return pl.BlockSpec((S, tc), lambda j: (0, j + off))


def _ssd_pre(proj, cw, cb, tc=256):
    S, n = proj.shape[0], SSD_CONV_DIM
    R = min(_CONV_ROWS, S)

    def body(x_ref, w_ref, b_ref, o_ref):
        wv, bv = w_ref[...], b_ref[...]
        for r0 in range(0, S, R):
            o_ref[r0:r0 + R, :] = _silu(_conv_rows(_rows(x_ref, r0 - _HALO, r0 + R), wv, bv, R))

    return pl.pallas_call(
        body, grid=(n // tc,),
        in_specs=[_col_spec(S, tc, OX // tc), pl.BlockSpec((4, tc), lambda j: (0, j)), pl.BlockSpec((1, tc), lambda j: (0, j))],
        out_specs=_col_spec(S, tc), out_shape=_sds((S, n)), name="ssd_pre", compiler_params=_PAR)(proj, cw, cb)


def _ssd_pre_bwd(proj, dxc, cw, cb, dproj, tc=256):
    S, n = proj.shape[0], SSD_CONV_DIM
    R = min(_CONV_ROWS, S)

    def body(x_ref, d_ref, w_ref, b_ref, dproj_in, dx_ref, dw_ref, db_ref):
        wv, bv = w_ref[...], b_ref[...]
        acc = [jnp.zeros((1, tc), F32)] * 5
        for r0 in range(0, S, R):
            ext = _rows(x_ref, r0 - _HALO, r0 + R + _HALO)
            dc = _rows(d_ref, r0, r0 + R + _HALO) * _dsilu(_conv_rows(ext, wv, bv, R + _HALO))
            dx, dws, db = _conv_rows_bwd(ext, wv, dc, R)
            dx_ref[r0:r0 + R, :] = dx.astype(BF16)
            acc = [s + d for s, d in zip(acc, dws + [db])]
        for k in range(4):
            dw_ref[k:k + 1, :] = acc[k]
        db_ref[...] = acc[4]

    wspec, bspec = pl.BlockSpec((4, tc), lambda j: (0, j)), pl.BlockSpec((1, tc), lambda j: (0, j))
    return pl.pallas_call(
        body, grid=(n // tc,), in_specs=[_col_spec(S, tc, OX // tc), _col_spec(S, tc), wspec, bspec, _ANY],
        out_specs=(_col_spec(S, tc, OX // tc), wspec, bspec), out_shape=(_sds(dproj.shape, BF16), _sds((4, n)), _sds((1, n))),
        input_output_aliases={4: 0}, name="ssd_pre_bwd", compiler_params=_PAR)(proj, dxc, cw, cb, dproj)


def _ffn_act(up, cw, cb, tc=256):
    S, nb = up.shape[0], FFN // tc
    R = min(_CONV_ROWS, S)

    def body(g_ref, v_ref, wg_ref, wv_ref, bg_ref, bv_ref, o_ref):
        wg, wv, bg, bv = wg_ref[...], wv_ref[...], bg_ref[...], bv_ref[...]
        for r0 in range(0, S, R):
            cg = _conv_rows(_rows(g_ref, r0 - _HALO, r0 + R), wg, bg, R)
            cv = _conv_rows(_rows(v_ref, r0 - _HALO, r0 + R), wv, bv, R)
            o_ref[r0:r0 + R, :] = (_silu(cg) * cv).astype(BF16)

    def wspec(off):
        return pl.BlockSpec((3, tc), lambda j: (0, j + off))

    def bspec(off):
        return pl.BlockSpec((1, tc), lambda j: (0, j + off))

    return pl.pallas_call(
        body, grid=(nb,), in_specs=[_col_spec(S, tc), _col_spec(S, tc, nb), wspec(0), wspec(nb), bspec(0), bspec(nb)],
        out_specs=_col_spec(S, tc), out_shape=_sds((S, FFN), BF16), name="ffn_act", compiler_params=_PAR)(up, up, cw, cw, cb, cb)


def _ffn_act_bwd(up, dact, cw, cb, tc=256):
    S, nb = up.shape[0], FFN // tc
    R = min(_CONV_ROWS, S)

    def body(g_ref, v_ref, d_ref, wg_ref, wv_ref, bg_ref, bv_ref, dx_ref, dw_ref, db_ref):
        wg, wv, bg, bv = wg_ref[...], wv_ref[...], bg_ref[...], bv_ref[...]
        acc = [[jnp.zeros((1, tc), F32)] * 4, [jnp.zeros((1, tc), F32)] * 4]
        for r0 in range(0, S, R):
            eg, ev = _rows(g_ref, r0 - _HALO, r0 + R + _HALO), _rows(v_ref, r0 - _HALO, r0 + R + _HALO)
            da = _rows(d_ref, r0, r0 + R + _HALO)
            cg, cv = _conv_rows(eg, wg, bg, R + _HALO), _conv_rows(ev, wv, bv, R + _HALO)
            sg = _sigmoid(cg)
            for half, (ext, w, dc) in enumerate(((eg, wg, da * cv * (sg * (1.0 + cg * (1.0 - sg)))), (ev, wv, da * (cg * sg)))):
                dx, dws, db = _conv_rows_bwd(ext, w, dc, R)
                dx_ref[half, r0:r0 + R, :] = dx.astype(BF16)
                acc[half] = [s + d for s, d in zip(acc[half], dws + [db])]
        for half in range(2):
            for k in range(3):
                dw_ref[half, k:k + 1, :] = acc[half][k]
            db_ref[half] = acc[half][3]

    def wspec(off):
        return pl.BlockSpec((3, tc), lambda j: (0, j + off))

    def bspec(off):
        return pl.BlockSpec((1, tc), lambda j: (0, j + off))

    cs = _col_spec(S, tc)
    both = lambda r: pl.BlockSpec((2, r, tc), lambda j: (0, 0, j))
    return pl.pallas_call(
        body, grid=(nb,), in_specs=[cs, _col_spec(S, tc, nb), cs, wspec(0), wspec(nb), bspec(0), bspec(nb)],
        out_specs=(both(S), both(3), both(1)), out_shape=(_sds((2, S, FFN), BF16), _sds((2, 3, FFN)), _sds((2, 1, FFN))),
        name="ffn_act_bwd", compiler_params=_PAR)(up, up, dact, cw, cw, cb, cb)


def _window_sum(x, w, up=False):
    shift = _shift_up if up else _shift_down
    j = 1
    while j < w:
        x = x + shift(x, j)
        j *= 2
    return x


def _pool_fwd(proj, pool_w, pool_scale):
    S = proj.shape[0]

    def body(u_ref, w_ref, s_ref, o_ref):
        cnt_row = (_iota((S, 128), 0) + 1).astype(F32)
        for g, w in enumerate(POOL_WINDOWS):
            sl = slice(g * 128, (g + 1) * 128)
            u = u_ref[:, sl]
            pooled = _window_sum(u, w) / jnp.minimum(cnt_row, float(w)) - u
            mixed = _dot(pooled.astype(BF16), w_ref[g].astype(BF16), _NN)
            o_ref[:, sl] = (mixed * s_ref[:, sl]).astype(BF16)

    return pl.pallas_call(
        body, grid=(1,),
        in_specs=[pl.BlockSpec((S, 512), lambda i: (0, OP // 512)), pl.BlockSpec((4, 128, 128), lambda i: (0, 0, 0)), _vec_spec(512)],
        out_specs=pl.BlockSpec((S, 512), lambda i: (0, 0)), out_shape=_sds((S, 512), BF16), name="pool_fwd",
        compiler_params=_ARB)(proj, pool_w, pool_scale)


def _pool_bwd(proj, dob, pool_w, pool_scale, dproj):
    S = proj.shape[0]

    def body(u_ref, d_ref, w_ref, s_ref, dproj_in, du_ref, dw_ref, ds_ref):
        cnt_row = (_iota((S, 128), 0) + 1).astype(F32)
        for g, w in enumerate(POOL_WINDOWS):
            sl = slice(g * 128, (g + 1) * 128)
            u, dv, wv = u_ref[:, sl], d_ref[:, sl], w_ref[g].astype(BF16)
            cnt = jnp.minimum(cnt_row, float(w))
            pooled = (_window_sum(u, w) / cnt - u).astype(BF16)
            ds_ref[:, sl] = jnp.sum(dv * _dot(pooled, wv, _NN), 0, keepdims=True)
            dmix = (dv * s_ref[:, sl]).astype(BF16)
            dw_ref[g] = _dot(pooled, dmix, _TN)
            dp = _dot(dmix, wv, _NT)
            du_ref[:, sl] = (_window_sum(dp / cnt, w, up=True) - dp).astype(BF16)

    blk = pl.BlockSpec((S, 512), lambda i: (0, 0))
    wspec = pl.BlockSpec((4, 128, 128), lambda i: (0, 0, 0))
    return pl.pallas_call(
        body, grid=(1,), in_specs=[pl.BlockSpec((S, 512), lambda i: (0, OP // 512)), blk, wspec, _vec_spec(512), _ANY],
        out_specs=(pl.BlockSpec((S, 512), lambda i: (0, OP // 512)), wspec, _vec_spec(512)),
        out_shape=(_sds(dproj.shape, BF16), _sds((4, 128, 128)), _sds((1, 512))), input_output_aliases={4: 0},
        name="pool_bwd", compiler_params=_ARB)(proj, dob, pool_w, pool_scale, dproj)


def _branch_specs():
    return [pl.BlockSpec((512, D), lambda i: (0, 0)), pl.BlockSpec((512, D), lambda i: (1, 0)), pl.BlockSpec((1024, D), lambda i: (1, 0))]


def _merge_fwd(oa, ob, oc, proj, wbr, tm=256):
    S = oa.shape[0]

    def body(oa_ref, ob_ref, oc_ref, gl_ref, wa_ref, wb_ref, wc_ref, o_ref):
        acc = _sigmoid(gl_ref[:, 0:D]) * _dot(oa_ref[...], wa_ref[...], _NN)
        acc += _sigmoid(gl_ref[:, D:2 * D]) * _dot(ob_ref[...], wb_ref[...], _NN)
        acc += _sigmoid(gl_ref[:, 2 * D:3 * D]) * _dot(oc_ref[...], wc_ref[...], _NN)
        o_ref[...] = acc.astype(BF16)

    return pl.pallas_call(
        body, grid=(S // tm,),
        in_specs=[_row_spec(tm, 512), _row_spec(tm, 512), _row_spec(tm, D), _row_spec(tm, 3 * D)] + _branch_specs(),
        out_specs=_row_spec(tm, D), out_shape=_sds((S, D), BF16), name="merge_fwd", compiler_params=_PAR)(oa, ob, oc, proj, wbr, wbr, wbr)


def _merge_bwd(dm, oa, ob, oc, proj, wbr, tm=256):
    S = oa.shape[0]

    def body(dm_ref, oa_ref, ob_ref, oc_ref, gl_ref, wa_ref, wb_ref, wc_ref, dya_ref, dyb_ref, dyc_ref, dgl_ref, doa_ref, dob_ref, doc_ref):
        dmv = dm_ref[...]
        for i, (o_ref, w_ref, dy_ref, do_ref) in enumerate(
                ((oa_ref, wa_ref, dya_ref, doa_ref), (ob_ref, wb_ref, dyb_ref, dob_ref), (oc_ref, wc_ref, dyc_ref, doc_ref))):
            gt = _sigmoid(gl_ref[:, i * D:(i + 1) * D])
            wv = w_ref[...]
            yv = _dot(o_ref[...], wv, _NN)
            dy = (dmv * gt).astype(BF16)
            dy_ref[...] = dy
            dgl_ref[:, i * D:(i + 1) * D] = (dmv * yv * gt * (1.0 - gt)).astype(BF16)
            do_ref[...] = _dot(dy, wv, _NT)

    rs = _row_spec
    return pl.pallas_call(
        body, grid=(S // tm,),
        in_specs=[rs(tm, D), rs(tm, 512), rs(tm, 512), rs(tm, D), rs(tm, 3 * D)] + _branch_specs(),
        out_specs=(rs(tm, D), rs(tm, D), rs(tm, D), rs(tm, 3 * D), rs(tm, 512), rs(tm, 512), rs(tm, D)),
        out_shape=(_sds((S, D), BF16), _sds((S, D), BF16), _sds((S, D), BF16), _sds((S, IN_PAD), BF16), _sds((S, 512)), _sds((S, 512)), _sds((S, D))),
        name="merge_bwd", compiler_params=_PAR)(dm, oa, ob, oc, proj, wbr, wbr, wbr)


def _rope_tab(posb, invf, tm=256):
    S = posb.shape[0]

    def body(p_ref, f_ref, c_ref, s1_ref, s2_ref):
        ang = p_ref[...] * f_ref[...]
        lane = _iota(ang.shape, 1)
        cs, sn = jnp.cos(ang), jnp.sin(ang)
        c_ref[...] = jnp.where(lane < NOPE, 1.0, cs)
        s1_ref[...] = jnp.where((lane >= 64) & (lane < 80), -sn, 0.0)
        s2_ref[...] = jnp.where((lane >= 80) & (lane < 96), sn, 0.0)

    rs = _row_spec(tm, 128)
    return pl.pallas_call(body, grid=(S // tm,), in_specs=[rs, _vec_spec(128)], out_specs=(rs, rs, rs),
                          out_shape=(_sds((S, 128)),) * 3, name="rope_tab", compiler_params=_PAR)(posb, invf)


def _rope(u, C, S1, S2):
    return u * C + pltpu.roll(u, 112, 1) * S1 + pltpu.roll(u, 16, 1) * S2


def _rope_t(dy, C, S1, S2):
    return dy * C + pltpu.roll(dy * S1, 16, 1) + pltpu.roll(dy * S2, 112, 1)


def _seg_sum(v, mask):
    return jnp.sum(jnp.where(mask, v, 0.0), -1, keepdims=True)


def _mla_latents(pq_ref, pkv_ref, wqb_ref, wkvb_ref, qan_ref, kvan_ref):
    ql, kvl = pq_ref[...], pkv_ref[...]
    ckv, kr = kvl[:, 0:KV_RANK], kvl[:, KV_RANK:KV_RANK + 128]
    rq = lax.rsqrt(jnp.mean(ql * ql, -1, keepdims=True) + EPS)
    rkv = lax.rsqrt(jnp.mean(ckv * ckv, -1, keepdims=True) + EPS)
    nq = (ql * rq * qan_ref[...]).astype(BF16)
    nkv = (ckv * rkv * kvan_ref[...]).astype(BF16)
    kv = jnp.concatenate([_dot(nkv, wkvb_ref[k], _NN) for k in range(4)], 1)
    return ql, ckv, kr, rq, rkv, nq, nkv, _dot(nq, wqb_ref[...], _NN), kv


def _mla_specs(tm):
    full = lambda r, n: pl.BlockSpec((r, n), lambda i: (0, 0))
    return ([pl.BlockSpec((tm, 384), lambda i: (i, OQ // 384)), pl.BlockSpec((tm, 384), lambda i: (i, OKV // 384))],
            [full(Q_RANK, D), pl.BlockSpec((4, KV_RANK, 256), lambda i: (0, 0, 0)), _vec_spec(Q_RANK), _vec_spec(KV_RANK), _vec_spec(128), _vec_spec(128)]
            + [_row_spec(tm, 128)] * 3)


def _mla_prep(proj, wqb, wkvb, qan, kvan, wq, wk, rope, tm=256):
    S = proj.shape[0]

    def body(pq_ref, pkv_ref, wqb_ref, wkvb_ref, qan_ref, kvan_ref, wq_ref, wk_ref, c_ref, s1_ref, s2_ref, qf_ref, kf_ref, ve_ref):
        _, _, kr, _, _, _, _, q, kv = _mla_latents(pq_ref, pkv_ref, wqb_ref, wkvb_ref, qan_ref, kvan_ref)
        C, S1, S2, wqv, wkv = c_ref[...], s1_ref[...], s2_ref[...], wq_ref[...], wk_ref[...]
        lane = _iota((tm, 128), 1)
        mn, mr = lane < 64, (lane >= 64) & (lane < 96)
        rrk = lax.rsqrt(_seg_sum(kr * kr, mr) / ROPE_DIM + EPS)
        ykr = _rope(jnp.where(mr, kr * rrk * wkv, 0.0), C, S1, S2)
        for h in range(N_HEADS):
            sl = slice(h * 128, (h + 1) * 128)
            t = q[:, sl]
            rn = lax.rsqrt(_seg_sum(t * t, mn) / NOPE + EPS)
            rr = lax.rsqrt(_seg_sum(t * t, mr) / ROPE_DIM + EPS)
            qf_ref[:, sl] = _rope(t * jnp.where(mn, rn, jnp.where(mr, rr, 0.0)) * wqv, C, S1, S2).astype(BF16)
            t = kv[:, sl]
            rn = lax.rsqrt(_seg_sum(t * t, mn) / NOPE + EPS)
            kf_ref[:, sl] = (jnp.where(mn, t * rn * wkv, 0.0) + ykr).astype(BF16)
            ve_ref[:, sl] = (jnp.where(mn, pltpu.roll(t, 64, 1), 0.0) if h % 2 == 0 else jnp.where(mn, 0.0, t)).astype(BF16)

    pspecs, wspecs = _mla_specs(tm)
    rs = _row_spec(tm, D)
    return pl.pallas_call(body, grid=(S // tm,), in_specs=pspecs + wspecs, out_specs=(rs, rs, rs),
                          out_shape=(_sds((S, D), BF16),) * 3, name="mla_prep", compiler_params=_PAR)(
        proj, proj, wqb, wkvb, qan, kvan, wq, wk, *rope)


def _mla_prep_bwd(proj, dqf, dkf, dve, wqb, wkvb, qan, kvan, wq, wk, rope, dproj, tm=256):
    S = proj.shape[0]

    def body(pq_ref, pkv_ref, wqb_ref, wkvb_ref, qan_ref, kvan_ref, wq_ref, wk_ref, c_ref, s1_ref, s2_ref,
             dqf_ref, dkf_ref, dve_ref, dproj_in, dlat_ref, dwqb_ref, dwkvb_ref, st_ref, dq_scr, dkv_scr):
        dqs_ref, dkvs_ref = dlat_ref.at[:, 0:384], dlat_ref.at[:, 384:768]
        @pl.when(pl.program_id(0) == 0)
        def _():
            dwqb_ref[...] = jnp.zeros_like(dwqb_ref)
            dwkvb_ref[...] = jnp.zeros_like(dwkvb_ref)
            st_ref[...] = jnp.zeros_like(st_ref)

        ql, ckv, kr, rq, rkv, nq, nkv, q, kv = _mla_latents(pq_ref, pkv_ref, wqb_ref, wkvb_ref, qan_ref, kvan_ref)
        C, S1, S2, wqv, wkv = c_ref[...], s1_ref[...], s2_ref[...], wq_ref[...], wk_ref[...]
        lane = _iota((tm, 128), 1)
        mn, mr = lane < 64, (lane >= 64) & (lane < 96)
        dwq = jnp.zeros((1, 128), F32)
        dwk = jnp.zeros((1, 128), F32)
        dykr = jnp.zeros((tm, 128), F32)
        for h in range(N_HEADS):
            sl = slice(h * 128, (h + 1) * 128)
            t = q[:, sl]
            rn = lax.rsqrt(_seg_sum(t * t, mn) / NOPE + EPS)
            rr = lax.rsqrt(_seg_sum(t * t, mr) / ROPE_DIM + EPS)
            scale = jnp.where(mn, rn, jnp.where(mr, rr, 0.0))
            that = t * scale
            du = _rope_t(dqf_ref[:, sl], C, S1, S2)
            dwq += jnp.sum(du * that, 0, keepdims=True)
            g = du * wqv
            gt = g * that
            dq_scr[:, sl] = scale * (g - that * jnp.where(mn, _seg_sum(gt, mn) / NOPE, _seg_sum(gt, mr) / ROPE_DIM))
            t = kv[:, sl]
            rn = lax.rsqrt(_seg_sum(t * t, mn) / NOPE + EPS)
            that = jnp.where(mn, t * rn, 0.0)
            dkf = dkf_ref[:, sl]
            dykr += jnp.where(mr, dkf, 0.0)
            dkn = jnp.where(mn, dkf, 0.0)
            dwk += jnp.sum(dkn * that, 0, keepdims=True)
            g = dkn * wkv
            dve = dve_ref[:, sl]
            dkv_scr[:, sl] = jnp.where(mn, rn * (g - that * (jnp.sum(g * that, -1, keepdims=True) / NOPE)),
                                       pltpu.roll(dve, 64, 1) if h % 2 == 0 else dve)
        rrk = lax.rsqrt(_seg_sum(kr * kr, mr) / ROPE_DIM + EPS)
        that = jnp.where(mr, kr * rrk, 0.0)
        dukr = jnp.where(mr, _rope_t(dykr, C, S1, S2), 0.0)
        dwk += jnp.sum(dukr * that, 0, keepdims=True)
        g = dukr * wkv
        dkr = rrk * (g - that * (jnp.sum(g * that, -1, keepdims=True) / ROPE_DIM))
        dqv, dkvv = dq_scr[...].astype(BF16), dkv_scr[...].astype(BF16)
        dnq = _dot(dqv, wqb_ref[...], _NT)
        dwqb_ref[...] += _dot(nq, dqv, _TN)
        dnkv = jnp.zeros((tm, KV_RANK), F32)
        for k in range(4):
            dnkv += _dot(dkvv[:, k * 256:(k + 1) * 256], wkvb_ref[k], _NT)
            dwkvb_ref[k] += _dot(nkv, dkvv[:, k * 256:(k + 1) * 256], _TN)
        xhat = ql * rq
        st_ref[0:1, 0:Q_RANK] += jnp.sum(dnq * xhat, 0, keepdims=True)
        g = dnq * qan_ref[...]
        dqs_ref[...] = (rq * (g - xhat * jnp.mean(g * xhat, -1, keepdims=True))).astype(BF16)
        xhat = ckv * rkv
        st_ref[1:2, 0:KV_RANK] += jnp.sum(dnkv * xhat, 0, keepdims=True)
        g = dnkv * kvan_ref[...]
        dkvs_ref[:, 0:KV_RANK] = (rkv * (g - xhat * jnp.mean(g * xhat, -1, keepdims=True))).astype(BF16)
        dkvs_ref[:, KV_RANK:KV_RANK + 128] = dkr.astype(BF16)
        st_ref[2:3, 0:128] += dwq
        st_ref[3:4, 0:128] += dwk

    pspecs, wspecs = _mla_specs(tm)
    rs = _row_spec(tm, D)
    full = lambda r, n: pl.BlockSpec((r, n), lambda i: (0, 0))
    return pl.pallas_call(
        body, grid=(S // tm,), in_specs=pspecs + wspecs + [rs, rs, rs, _ANY],
        out_specs=(pl.BlockSpec((tm, 768), lambda i: (i, OQ // 768)), full(Q_RANK, D), pl.BlockSpec((4, KV_RANK, 256), lambda i: (0, 0, 0)), full(8, D)),
        out_shape=(_sds(dproj.shape, BF16), _sds((Q_RANK, D)), _sds((4, KV_RANK, 256)), _sds((8, D))), input_output_aliases={14: 0},
        scratch_shapes=[pltpu.VMEM((tm, D), F32), pltpu.VMEM((tm, D), F32)], name="mla_prep_bwd", compiler_params=_ARB)(
        proj, proj, wqb, wkvb, qan, kvan, wq, wk, *rope, dqf, dkf, dve, dproj)


_ATT_SCALE = (NOPE + ROPE_DIM) ** -0.5


def _att_probs(q, k, i, tq):
    n = k.shape[0]
    s = _dot(q, k, _NT) * _ATT_SCALE
    tri = _iota((tq, tq), 1) <= _iota((tq, tq), 0)
    diag = jnp.where(tri, s[:, n - tq:], -1e30)
    s = diag if n == tq else jnp.concatenate([s[:, :n - tq], diag], 1)
    p = jnp.exp(s - jnp.max(s, -1, keepdims=True))
    return p * (1.0 / jnp.sum(p, -1, keepdims=True))


def _attn_fwd(qf, kf, ve, tq=512):
    S = qf.shape[0]
    tq = min(tq, S)

    def body(q_ref, k_ref, v_ref, o_ref):
        for i in range(S // tq):
            n, rows = (i + 1) * tq, slice(i * tq, (i + 1) * tq)
            acc = jnp.zeros((tq, 128), F32)
            for hh in range(2):
                sl = slice(hh * 128, (hh + 1) * 128)
                p = _att_probs(q_ref[rows, sl], k_ref[0:n, sl], i, tq)
                acc += _dot(p.astype(BF16), v_ref[0:n, sl], _NN)
            o_ref[rows, :] = acc.astype(BF16)

    ps = pl.BlockSpec((S, 256), lambda h: (0, h))
    return pl.pallas_call(body, grid=(N_HEADS // 2,), in_specs=[ps, ps, ps], out_specs=pl.BlockSpec((S, 128), lambda h: (0, h)),
                          out_shape=_sds((S, 512), BF16), name="attn_fwd", compiler_params=_PAR)(qf, kf, ve)


def _attn_bwd(qf, kf, ve, do, tq=512):
    S = qf.shape[0]
    tq = min(tq, S)

    def body(q_ref, k_ref, v_ref, do_ref, dq_ref, dk_ref, dv_ref):
        dk_ref[...] = jnp.zeros_like(dk_ref)
        dv_ref[...] = jnp.zeros_like(dv_ref)
        for i in range(S // tq):
            n, rows = (i + 1) * tq, slice(i * tq, (i + 1) * tq)
            dob = do_ref[rows, :].astype(BF16)
            for hh in range(2):
                sl = slice(hh * 128, (hh + 1) * 128)
                q, k = q_ref[rows, sl], k_ref[0:n, sl]
                p = _att_probs(q, k, i, tq)
                dv_ref[0:n, sl] += _dot(p.astype(BF16), dob, _TN)
                dp = _dot(dob, v_ref[0:n, sl], _NT)
                ds = (p * (dp - jnp.sum(dp * p, -1, keepdims=True)) * _ATT_SCALE).astype(BF16)
                dq_ref[rows, sl] = _dot(ds, k, _NN)
                dk_ref[0:n, sl] += _dot(ds, q, _TN)

    ps = pl.BlockSpec((S, 256), lambda h: (0, h))
    return pl.pallas_call(body, grid=(N_HEADS // 2,), in_specs=[ps, ps, ps, pl.BlockSpec((S, 128), lambda h: (0, h))], out_specs=(ps, ps, ps),
                          out_shape=(_sds((S, D)),) * 3, name="attn_bwd", compiler_params=_PAR)(qf, kf, ve, do)


def _softplus(x):
    return jnp.maximum(x, 0.0) + jnp.log1p(jnp.exp(-jnp.abs(x)))


def _ssd_chunk(xc_ref, dtr_ref, dtb_ref, al_ref, e_ref):
    L = SSD_L
    a = -jnp.exp(al_ref[...])
    dtp = _softplus(dtr_ref[...] + dtb_ref[...])
    causal = _iota((L, L), 1) <= _iota((L, L), 0)
    cs = _dot_hi(causal.astype(F32), dtp * a)
    E = e_ref[...]
    dtx, csx = _dot_hi(dtp, E), _dot_hi(cs, E)
    X = xc_ref[:, 0:SSD_INNER]
    Xd = X * dtx
    dec_out = jnp.exp(csx)
    dec_st = jnp.exp(csx[L - 1:L, :] - csx)
    return a, dtp, causal, cs, cs.T, dtx, X, Xd, dec_out, dec_st


def _ssd_decay(causal, cs, cs_row, h):
    diff = cs[:, h:h + 1] - cs_row[h:h + 1, :]
    return jnp.where(causal, jnp.exp(jnp.where(causal, diff, 0.0)), 0.0)


def _ssd_groups(xc_ref, g):
    b0, c0 = SSD_INNER + g * SSD_N, SSD_INNER + 2 * SSD_N + g * SSD_N
    return xc_ref[:, b0:b0 + SSD_N].astype(BF16), xc_ref[:, c0:c0 + SSD_N].astype(BF16)


def _pair_decay(cs, pair):
    L = SSD_L
    return jnp.where(_iota((128, 128), 0) < 64, jnp.exp(cs[L - 1:L, 2 * pair:2 * pair + 1]), jnp.exp(cs[L - 1:L, 2 * pair + 1:2 * pair + 2]))


def _ssd_in_specs(nc, rev):
    idx = (lambda c: nc - 1 - c) if rev else (lambda c: c)
    return [pl.BlockSpec((SSD_L, SSD_CONV_DIM), lambda c: (idx(c), 0)), pl.BlockSpec((SSD_L, 128), lambda c: (idx(c), ODT // 128)),
            _vec_spec(128), _vec_spec(128), _vec_spec(SSD_INNER), pl.BlockSpec((128, SSD_INNER), lambda c: (0, 0))]


def _ssd_core(xc, proj, dtb, alog, dskip, E):
    S = xc.shape[0]
    nc = S // SSD_L

    def body(xc_ref, dtr_ref, dtb_ref, al_ref, dx_ref, e_ref, y_ref, hp_ref, h_scr):
        @pl.when(pl.program_id(0) == 0)
        def _():
            h_scr[...] = jnp.zeros_like(h_scr)

        hp_ref[0] = h_scr[...]
        _, _, causal, cs, cs_row, _, X, Xd, dec_out, dec_st = _ssd_chunk(xc_ref, dtr_ref, dtb_ref, al_ref, e_ref)
        Xs = Xd * dec_st
        lane = _iota((SSD_L, 128), 1)
        for g in range(2):
            Bg, Cg = _ssd_groups(xc_ref, g)
            CB = _dot(Cg, Bg, _NT)
            for pr in range(4):
                pair = g * 4 + pr
                psl = slice(pair * 128, (pair + 1) * 128)
                Xdp = Xd[:, psl].astype(BF16)
                r0 = _dot((CB * _ssd_decay(causal, cs, cs_row, 2 * pair)).astype(BF16), Xdp, _NN)
                r1 = _dot((CB * _ssd_decay(causal, cs, cs_row, 2 * pair + 1)).astype(BF16), Xdp, _NN)
                Hp = h_scr[psl, :]
                W = _dot(Cg, Hp.astype(BF16), _NT)
                y_ref[:, psl] = jnp.where(lane < 64, r0, r1) + W * dec_out[:, psl] + X[:, psl] * dx_ref[:, psl]
                h_scr[psl, :] = Hp * _pair_decay(cs, pair) + _dot(Xs[:, psl].astype(BF16), Bg, _TN)

    return pl.pallas_call(
        body, grid=(nc,), in_specs=_ssd_in_specs(nc, False),
        out_specs=(pl.BlockSpec((SSD_L, SSD_INNER), lambda c: (c, 0)), pl.BlockSpec((1, SSD_INNER, SSD_N), lambda c: (c, 0, 0))),
        out_shape=(_sds((S, SSD_INNER)), _sds((nc, SSD_INNER, SSD_N))), scratch_shapes=[pltpu.VMEM((SSD_INNER, SSD_N), F32)],
        name="ssd_core", compiler_params=_ARB)(xc, proj, dtb, alog, dskip, E)


def _ssd_core_bwd(xc, proj, hprev, dy, dtb, alog, dskip, E, ET, dproj):
    S = xc.shape[0]
    nc = S // SSD_L
    L = SSD_L

    def body(xc_ref, dtr_ref, dtb_ref, al_ref, dx_ref, e_ref, et_ref, hp_ref, dy_ref, dproj_in, dxc_ref, ddt_ref, st_ref, dh_scr, acc_scr):
        step = pl.program_id(0)

        @pl.when(step == 0)
        def _():
            dh_scr[...] = jnp.zeros_like(dh_scr)
            acc_scr[...] = jnp.zeros_like(acc_scr)
            st_ref[...] = jnp.zeros_like(st_ref)

        a, dtp, causal, cs, cs_row, dtx, X, Xd, dec_out, dec_st = _ssd_chunk(xc_ref, dtr_ref, dtb_ref, al_ref, e_ref)
        lane = _iota((L, 128), 1)
        sub = _iota((128, L), 0)
        dcs_col = jnp.zeros((L, 128), F32)
        dcs_row = jnp.zeros((128, L), F32)
        dcs_last = jnp.zeros((1, 128), F32)
        dcsx, ddtx, dlastx = [], [], []
        for g in range(2):
            Bg, Cg = _ssd_groups(xc_ref, g)
            CB = _dot(Cg, Bg, _NT)
            dCB = jnp.zeros((L, L), F32)
            dB = jnp.zeros((L, SSD_N), F32)
            dC = jnp.zeros((L, SSD_N), F32)
            for pr in range(4):
                pair = g * 4 + pr
                psl = slice(pair * 128, (pair + 1) * 128)
                dY, Xp, Xdp, dop, dsp = dy_ref[:, psl], X[:, psl], Xd[:, psl], dec_out[:, psl], dec_st[:, psl]
                Xdb = Xdp.astype(BF16)
                acc_scr[0:1, psl] += jnp.sum(dY * Xp, 0, keepdims=True)
                Hp = hp_ref[0, psl, :]
                Hb = Hp.astype(BF16)
                dW = (dY * dop).astype(BF16)
                dcx = dY * _dot(Cg, Hb, _NT) * dop
                dC += _dot(dW, Hb, _NN)
                dHp = _dot(dW, Cg, _TN)
                dHn = dh_scr[psl, :]
                cd = _pair_decay(cs, pair)
                dh_scr[psl, :] = dHp + dHn * cd
                rsum = jnp.sum(dHn * Hp * cd, -1, keepdims=True)
                half = _iota((128, 1), 0) < 64
                s0 = jnp.sum(jnp.where(half, rsum, 0.0), 0, keepdims=True)
                s1 = jnp.sum(jnp.where(half, 0.0, rsum), 0, keepdims=True)
                lane1 = _iota((1, 128), 1)
                dcs_last += jnp.where(lane1 == 2 * pair, s0, 0.0) + jnp.where(lane1 == 2 * pair + 1, s1, 0.0)
                dHb = dHn.astype(BF16)
                dXs = _dot(Bg, dHb, _NT)
                dB += _dot((Xdp * dsp).astype(BF16), dHb, _NN)
                dXd = dXs * dsp
                e_st = dXs * Xdp * dsp
                dcx -= e_st
                dlastx.append(jnp.sum(e_st, 0, keepdims=True))
                for i in range(2):
                    h = 2 * pair + i
                    Dm = _ssd_decay(causal, cs, cs_row, h)
                    M = CB * Dm
                    dYm = jnp.where((lane < 64) if i == 0 else (lane >= 64), dY, 0.0).astype(BF16)
                    dM = _dot(dYm, Xdb, _NT)
                    dXd += _dot(M.astype(BF16), dYm, _TN)
                    dCB += dM * Dm
                    Em = dM * M
                    dcs_col += jnp.where(lane == h, jnp.sum(Em, -1, keepdims=True), 0.0)
                    dcs_row += jnp.where(sub == h, jnp.sum(Em, 0, keepdims=True), 0.0)
                dxc_ref[:, psl] = dY * dx_ref[:, psl] + dXd * dtx[:, psl]
                ddtx.append(dXd * Xp)
                dcsx.append(dcx)
            dCBb = dCB.astype(BF16)
            b0, c0 = SSD_INNER + g * SSD_N, SSD_INNER + 2 * SSD_N + g * SSD_N
            dxc_ref[:, b0:b0 + SSD_N] = dB + _dot(dCBb, Cg, _TN)
            dxc_ref[:, c0:c0 + SSD_N] = dC + _dot(dCBb, Bg, _NN)
        ET = et_ref[...]
        dcs = dcs_col - dcs_row.T + _dot_hi(jnp.concatenate(dcsx, 1), ET)
        dlast = dcs_last + _dot_hi(jnp.broadcast_to(jnp.concatenate(dlastx, 1), (8, SSD_INNER)), ET)[0:1, :]
        dcs += jnp.where(_iota((L, 128), 0) == L - 1, dlast, 0.0)
        dda = _dot_hi((_iota((L, L), 1) >= _iota((L, L), 0)).astype(F32), dcs)
        ddtp = dda * a + _dot_hi(jnp.concatenate(ddtx, 1), ET)
        draw = ddtp * _sigmoid(dtr_ref[...] + dtb_ref[...])
        ddt_ref[...] = draw.astype(BF16)
        st_ref[0:1, :] += jnp.sum(draw, 0, keepdims=True)
        st_ref[1:2, :] += jnp.sum(dda * dtp, 0, keepdims=True) * a

        @pl.when(step == nc - 1)
        def _():
            st_ref[2:3, :] = _dot_hi(acc_scr[...], ET)[0:1, :]

    rev = lambda c: (nc - 1 - c, 0)
    return pl.pallas_call(
        body, grid=(nc,),
        in_specs=_ssd_in_specs(nc, True) + [pl.BlockSpec((SSD_INNER, 128), lambda c: (0, 0)),
                                            pl.BlockSpec((1, SSD_INNER, SSD_N), lambda c: (nc - 1 - c, 0, 0)),
                                            pl.BlockSpec((L, SSD_INNER), rev), _ANY],
        out_specs=(pl.BlockSpec((L, SSD_CONV_DIM), rev), pl.BlockSpec((L, 128), lambda c: (nc - 1 - c, ODT // 128)),
                   pl.BlockSpec((8, 128), lambda c: (0, 0))),
        out_shape=(_sds((S, SSD_CONV_DIM)), _sds(dproj.shape, BF16), _sds((8, 128))), input_output_aliases={9: 1},
        scratch_shapes=[pltpu.VMEM((SSD_INNER, SSD_N), F32), pltpu.VMEM((8, SSD_INNER), F32)],
        name="ssd_core_bwd", compiler_params=_ARB)(xc, proj, dtb, alog, dskip, E, ET, hprev, dy, dproj)


def _ssd_post(y, proj, nw, tm=256):
    S = y.shape[0]

    def body(y_ref, z_ref, nw_ref, o_ref):
        for g in range(2):
            sl = slice(g * 512, (g + 1) * 512)
            gated = y_ref[:, sl] * _silu(z_ref[:, sl])
            r = lax.rsqrt(jnp.mean(gated * gated, -1, keepdims=True) + EPS)
            o_ref[:, sl] = (gated * r * nw_ref[:, sl]).astype(BF16)

    return pl.pallas_call(
        body, grid=(S // tm,), in_specs=[_row_spec(tm, D), pl.BlockSpec((tm, D), lambda i: (i, OZ // D)), _vec_spec(D)],
        out_specs=_row_spec(tm, D), out_shape=_sds((S, D), BF16), name="ssd_post", compiler_params=_PAR)(y, proj, nw)


def _ssd_post_bwd(doc, y, proj, nw, dproj, tm=256):
    S = y.shape[0]

    def body(d_ref, y_ref, z_ref, nw_ref, dproj_in, dy_ref, dz_ref, st_ref):
        @pl.when(pl.program_id(0) == 0)
        def _():
            st_ref[...] = jnp.zeros_like(st_ref)

        for g in range(2):
            sl = slice(g * 512, (g + 1) * 512)
            yv, zv, dv = y_ref[:, sl], z_ref[:, sl], d_ref[:, sl]
            sz = _silu(zv)
            gated = yv * sz
            r = lax.rsqrt(jnp.mean(gated * gated, -1, keepdims=True) + EPS)
            ghat = gated * r
            st_ref[0:1, sl] += jnp.sum(dv * ghat, 0, keepdims=True)
            gg = dv * nw_ref[:, sl]
            dg = r * (gg - ghat * jnp.mean(gg * ghat, -1, keepdims=True))
            dy_ref[:, sl] = dg * sz
            dz_ref[:, sl] = (dg * yv * _dsilu(zv)).astype(BF16)

    zs = pl.BlockSpec((tm, D), lambda i: (i, OZ // D))
    return pl.pallas_call(
        body, grid=(S // tm,), in_specs=[_row_spec(tm, D), _row_spec(tm, D), zs, _vec_spec(D), _ANY],
        out_specs=(_row_spec(tm, D), zs, _vec_spec(D, 8)), out_shape=(_sds((S, D)), _sds(dproj.shape, BF16), _sds((8, D))),
        input_output_aliases={4: 1}, name="ssd_post_bwd", compiler_params=_ARB)(doc, y, proj, nw, dproj)


def _row(v, n=None):
    v = v.astype(F32).reshape(1, -1)
    return v if n is None else jnp.pad(v, ((0, 0), (0, n - v.shape[1])))


_IN_SEGMENTS = [(0, 384, OQ), (384, 640, OKV), (640, 672, OKV + 320), (672, 1184, OP), (1184, 2208, OZ), (2208, 3744, OX),
                (3744, 3760, ODT), (3760, IN_DIM, OG)]
_IN_ZEROS = [(OKV + 256, OKV + 320), (OKV + 352, OKV + 384), (ODT + 16, ODT + 128)]


def _in_pieces():
    w, out = IN_DIM // 4, []
    for a, b, d in _IN_SEGMENTS:
        while a < b:
            k = a // w
            e = min(b, (k + 1) * w)
            out.append((k, a - k * w, e - k * w, d))
            d, a = d + e - a, e
    return out


def _win_layout(w_in4, tm=256):
    def body(w_ref, o_ref):
        for k, s0, s1, d in _in_pieces():
            o_ref[:, d:d + s1 - s0] = w_ref[k, :, s0:s1]
        for z0, z1 in _IN_ZEROS:
            o_ref[:, z0:z1] = jnp.zeros((tm, z1 - z0), o_ref.dtype)

    return pl.pallas_call(
        body, grid=(D // tm,), in_specs=[pl.BlockSpec((4, tm, IN_DIM // 4), lambda i: (0, i, 0))],
        out_specs=pl.BlockSpec((tm, IN_PAD), lambda i: (i, 0)), out_shape=_sds((D, IN_PAD), w_in4.dtype), name="win_layout",
        compiler_params=_PAR)(w_in4)


def _win_unlayout(dwin, tm=256):
    def body(d_ref, o_ref):
        for k, s0, s1, d in _in_pieces():
            o_ref[k, :, s0:s1] = d_ref[:, d:d + s1 - s0]

    return pl.pallas_call(
        body, grid=(D // tm,), in_specs=[pl.BlockSpec((tm, IN_PAD), lambda i: (i, 0))],
        out_specs=pl.BlockSpec((4, tm, IN_DIM // 4), lambda i: (0, i, 0)), out_shape=_sds((4, D, IN_DIM // 4), dwin.dtype),
        name="win_unlayout", compiler_params=_PAR)(dwin)


def _prep_late(p):
    return dict(wbr=p["w_branch"].astype(BF16), wo=p["w_out"].astype(BF16), wup4=p["ffn_up4"].astype(BF16), wdn=p["ffn_down"].astype(BF16))


def _prep_layer(p, late=None):
    win = _win_layout(p["w_in4"].astype(BF16))
    wqb = jnp.transpose(p["w_q_b4"].astype(BF16).reshape(4, Q_RANK, 2, 96), (1, 0, 2, 3))
    return dict(
        win=win, wqb=jnp.pad(wqb, ((0, 0), (0, 0), (0, 0), (0, 32))).reshape(Q_RANK, D), wkvb4=p["w_kv_b4"].astype(BF16),
        late=late if late is not None else (lambda after: _prep_late(p)),
        nw1=_row(p["norm1_w"]), nw2=_row(p["norm2_w"]), qan=_row(p["q_a_norm"]), kvan=_row(p["kv_a_norm"]),
        wq=_row(p["q_norm"], 128), wk=_row(p["k_norm"], 128), pool_w=p["pool_w"].astype(F32), pool_scale=_row(p["pool_scale"]),
        cw=p["ssd_conv_w"].astype(F32), cb=_row(p["ssd_conv_b"]), dtb=_row(p["ssd_dt_bias"], 128), alog=_row(p["ssd_a_log"], 128),
        dskip=_row(jnp.repeat(p["ssd_d"].astype(F32), SSD_P)), snw=_row(p["ssd_norm_w"]),
        fcw=p["ffn_conv_w"].astype(F32), fcb=_row(p["ffn_conv_b"]))


def _layer_fwd(x, mod8, W, rope, E, tag):
    sh1, sc1, g1, sh2, sc2, g2 = (mod8[i:i + 1] for i in range(6))
    h1 = _ln_mod(x, W["nw1"], sc1, sh1, name=f"ln1_{tag}")
    proj = _mm(h1, W["win"], tn=640, tk=1024, name=f"proj_{tag}")
    qf, kf, ve = _mla_prep(proj, W["wqb"], W["wkvb4"], W["qan"], W["kvan"], W["wq"], W["wk"], rope)
    oa = _attn_fwd(qf, kf, ve)
    ob = _pool_fwd(proj, W["pool_w"], W["pool_scale"])
    xc = _ssd_pre(proj, W["cw"], W["cb"])
    y, hprev = _ssd_core(xc, proj, W["dtb"], W["alog"], W["dskip"], E)
    oc = _ssd_post(y, proj, W["snw"])
    W.update(W["late"](oc))
    merged = _merge_fwd(oa, ob, oc, proj, W["wbr"])
    x1, out1 = _mm(merged, W["wo"], tk=1024, res=x, gate=g1, name=f"wout_{tag}")
    h2 = _ln_mod(x1, W["nw2"], sc2, sh2, name=f"ln2_{tag}")
    up = _up_fwd(h2, W["wup4"], name=f"up_{tag}")
    act = _ffn_act(up, W["fcw"], W["fcb"])
    x2, out2 = _mm(act, W["wdn"], tm=1024, res=x1, gate=g2, name=f"down_{tag}")
    saved = dict(x=x, h1=h1, proj=proj, qf=qf, kf=kf, ve=ve, oa=oa, ob=ob, oc=oc, xc=xc, hprev=hprev, y=y, merged=merged,
                 out1=out1, x1=x1, h2=h2, up=up, act=act, out2=out2)
    return x2, saved


def _layer_bwd(dx2, sv, mod8, W, rope, E, ET, tag, emit=None, mid=None):
    sc1, g1, sc2, g2 = mod8[1:2], mod8[2:3], mod8[4:5], mod8[5:6]
    proj = sv["proj"]
    dz2, dg2 = _gate_bwd(dx2, sv["out2"], g2, name=f"gate2_bwd_{tag}")
    dact = _mm(dz2, W["wdn"], "nt", tn=1408, tk=1024, name=f"down_dx_{tag}")
    dwdn = _mm(sv["act"], dz2, "tn", tm=1408, name=f"down_dw_{tag}")
    dup2, dfcw, dfcb = _ffn_act_bwd(sv["up"], dact, W["fcw"], W["fcb"])
    dh2 = _up_dx(dup2, W["wup4"], name=f"up_dx_{tag}")
    dwup4 = _up_dw(sv["h2"], dup2, name=f"up_dw_{tag}")
    dx1, st2 = _ln_mod_bwd(sv["x1"], dh2, dx2, W["nw2"], sc2, name=f"ln2_bwd_{tag}")
    dz1, dg1 = _gate_bwd(dx1, sv["out1"], g1, name=f"gate1_bwd_{tag}")
    dmerged = _mm(dz1, W["wo"], "nt", tk=1024, name=f"wout_dx_{tag}")
    dwo = _mm(sv["merged"], dz1, "tn", name=f"wout_dw_{tag}")
    dya, dyb, dyc, dproj, doa, dob, doc = _merge_bwd(dmerged, sv["oa"], sv["ob"], sv["oc"], proj, W["wbr"])
    dwba = _mm(sv["oa"], dya, "tn", name=f"wba_dw_{tag}")
    dwbb = _mm(sv["ob"], dyb, "tn", name=f"wbb_dw_{tag}")
    dwbc = _mm(sv["oc"], dyc, "tn", name=f"wbc_dw_{tag}")
    late = dict(w_branch=jnp.concatenate([dwba, dwbb, dwbc], 0).reshape(4, 512, D), w_out=dwo.reshape(4, 256, D), ffn_up=dwup4,
                ffn_down=dwdn.reshape(4, FFN // 4, D))
    snw = W["snw"]
    if emit is not None:
        token = emit(late)
        if token is not None:
            snw, doa = snw + token, doa + token
    dy, dproj, st_post = _ssd_post_bwd(doc, sv["y"], proj, snw, dproj)
    dxc, dproj, st_ssd = _ssd_core_bwd(sv["xc"], proj, sv["hprev"], dy, W["dtb"], W["alog"], W["dskip"], E, ET, dproj)
    dproj, dcw, dcb = _ssd_pre_bwd(proj, dxc, W["cw"], W["cb"], dproj)
    if mid is not None:
        token = mid(dcb)
        if token is not None:
            doa, dob = doa + token, dob + token
    dproj, dpw, dps = _pool_bwd(proj, dob, W["pool_w"], W["pool_scale"], dproj)
    dqf, dkf, dve = _attn_bwd(sv["qf"], sv["kf"], sv["ve"], doa)
    dproj, dwqb, dwkvb4, st_mla = _mla_prep_bwd(proj, dqf, dkf, dve, W["wqb"], W["wkvb4"], W["qan"], W["kvan"], W["wq"], W["wk"], rope, dproj)
    dh1 = _mm(dproj, W["win"], "nt", tk=1408, name=f"proj_dx_{tag}")
    dwin = _mm(sv["h1"], dproj, "tn", tn=640, name=f"proj_dw_{tag}")
    dx, st1 = _ln_mod_bwd(sv["x"], dh1, dx1, W["nw1"], sc1, name=f"ln1_bwd_{tag}")
    grads = dict(
        norm1_w=st1[2], norm2_w=st2[2], w_in=_win_unlayout(dwin),
        q_a_norm=st_mla[0, :Q_RANK], kv_a_norm=st_mla[1, :KV_RANK], q_norm=st_mla[2, :96], k_norm=st_mla[3, :96],
        w_q_b=jnp.transpose(dwqb.reshape(Q_RANK, 4, 2, 128)[:, :, :, :96], (1, 0, 2, 3)).reshape(4, Q_RANK, 192), w_kv_b=dwkvb4,
        pool_w=dpw, pool_scale=dps[0], ssd_conv_w=dcw, ssd_conv_b=dcb[0],
        ssd_dt_bias=st_ssd[0, :SSD_HEADS], ssd_a_log=st_ssd[1, :SSD_HEADS], ssd_d=st_ssd[2, :SSD_HEADS], ssd_norm_w=st_post[0],
        ffn_conv_w=jnp.transpose(dfcw, (1, 0, 2)).reshape(3, 2 * FFN), ffn_conv_b=dfcb.reshape(2 * FFN), **late)
    dmod = jnp.concatenate([st1[0:2], dg1[0:1], st2[0:2], dg2[0:1]], 0)
    return dx, grads, dmod


def _ssd_expand():
    E = (jnp.arange(SSD_INNER)[None, :] // SSD_P == jnp.arange(128)[:, None]).astype(F32)
    return E, E.T


def _rope_tables(positions):
    inv_freq = ROPE_THETA ** (-jnp.arange(0, ROPE_DIM, 2, dtype=F32) / ROPE_DIM)
    invf = jnp.concatenate([jnp.zeros((NOPE,), F32), inv_freq, inv_freq, jnp.zeros((32,), F32)]).reshape(1, 128)
    posb = jnp.broadcast_to(positions.astype(F32)[:, None], (positions.shape[0], 128))
    return _rope_tab(posb, invf)


def _local_step(x, target, positions, mods, get_layer, bwd_mod=None, emit=None, mid=None, done=None):
    rope = _rope_tables(positions)
    E, ET = _ssd_expand()
    Ws, saved, h = [], [], x
    for l in range(2):
        Ws.append(_prep_layer(*get_layer(l, h)))
        h, sv = _layer_fwd(h, mods[l], Ws[l], rope, E, l)
        saved.append(sv)
    dy, lpart = _loss_grad(h, target)
    grads, dmods = [None, None], [None, None]
    for l in (1, 0):
        mod8 = mods[l] if bwd_mod is None else bwd_mod(l)
        hook = lambda f: None if f is None else functools.partial(f, l)
        dy, grads[l], dmods[l] = _layer_bwd(dy, saved[l], mod8, Ws[l], rope, E, ET, l, hook(emit), hook(mid))
        if done is not None:
            token = done(l, grads[l])
            if token is not None:
                dy = dy + token
    return lpart[0, 0], dy, grads, dmods


_ANY = pl.BlockSpec(memory_space=pl.ANY)
_VMEM = pl.BlockSpec(memory_space=pltpu.VMEM)


def _place():
    x, y, c = lax.axis_index("x"), lax.axis_index("y"), lax.axis_index("c")
    return x, y, c, [(1 - x, y), (x, 1 - y), (1 - x, 1 - y)]


def _allgather8(v, name):
    m_per, n = v.shape

    def body(x_ref, out_ref, send_sems, recv_sems, local_sem):
        x, y, c, chips = _place()
        me, sibling = (x, y, c), (x, y, 1 - c)

        def rows(px, py, pc):
            return out_ref.at[pl.ds((4 * px + 2 * py + pc) * m_per, m_per), :]

        def copy(k, block, to, src=None):
            return pltpu.make_async_remote_copy(src_ref=rows(*block) if src is None else src, dst_ref=rows(*block),
                                                send_sem=send_sems.at[k], recv_sem=recv_sems.at[k], device_id=to, device_id_type=MESH)

        mine = pltpu.make_async_copy(x_ref, rows(*me), local_sem)
        mine.start()
        first = [copy(0, me, sibling, src=x_ref)] + [copy(1 + j, me, (*chip, c), src=x_ref) for j, chip in enumerate(chips)]
        for cp in first:
            cp.start()
        passed = [copy(4 + j, (*chip, c), sibling) for j, chip in enumerate(chips)]
        for j, chip in enumerate(chips):
            copy(1 + j, (*chip, c), me).wait_recv()
            passed[j].start()
        copy(0, sibling, me).wait_recv()
        for j, chip in enumerate(chips):
            copy(4 + j, (*chip, 1 - c), me).wait_recv()
        for cp in first + passed:
            cp.wait_send()
        mine.wait()

    return pl.pallas_call(
        body, out_shape=_sds((8 * m_per, n), v.dtype), in_specs=[_VMEM], out_specs=_VMEM,
        scratch_shapes=[pltpu.SemaphoreType.DMA((7,)), pltpu.SemaphoreType.DMA((7,)), pltpu.SemaphoreType.DMA], name=name)(v)


def _sems(n):
    return [pltpu.SemaphoreType.DMA((n,)), pltpu.SemaphoreType.DMA((n,))]


_HBM = pl.BlockSpec(memory_space=pltpu.HBM)
_SEM = pl.BlockSpec(memory_space=pltpu.SEMAPHORE)
_EFFECT = pltpu.CompilerParams(has_side_effects=pltpu.SideEffectType.DATAFLOW_SIDE_EFFECTING)


def _ici_copy(src_refs, land_refs, send_sems, recv_sems, a, j, slices, incoming):
    x, y, c, chips = _place()
    if _peers(slices) == 1:
        src, dst = slices(src_refs[a], land_refs[a], 1 - c if incoming else c)
        return pltpu.make_async_remote_copy(src_ref=src, dst_ref=dst, send_sem=send_sems.at[a], recv_sem=recv_sems.at[a],
                                            device_id=(x, y, 1 - c), device_id_type=MESH)
    me, other = 2 * x + y, 2 * chips[j][0] + chips[j][1]
    src, dst = slices(src_refs[a], land_refs[a], other, me, c) if incoming else slices(src_refs[a], land_refs[a], me, other, c)
    return pltpu.make_async_remote_copy(src_ref=src, dst_ref=dst, send_sem=send_sems.at[3 * a + j], recv_sem=recv_sems.at[3 * a + j],
                                        device_id=(*chips[j], c), device_id_type=MESH)


def _peers(slices):
    return 1 if slices is _halves_slices else 3


def _ici_start(srcs, land_shapes, slices, after, name):
    na = len(srcs)

    def body(*refs):
        src_refs, land_refs, send_sems, recv_sems = refs[:na], refs[na:2 * na], refs[2 * na + 1], refs[2 * na + 2]
        for a in range(na):
            for j in range(_peers(slices)):
                _ici_copy(src_refs, land_refs, send_sems, recv_sems, a, j, slices, False).start()
        refs[-1][...] = jnp.zeros_like(refs[-1])

    hbm = lambda v: pltpu.with_memory_space_constraint(v, pltpu.HBM)
    lands = [hbm(lax.empty(s.shape, s.dtype)) for s in land_shapes]
    return pl.pallas_call(
        body, name=name,
        out_shape=(pltpu.SemaphoreType.DMA((_peers(slices) * na,)), pltpu.SemaphoreType.DMA((_peers(slices) * na,)),
                   *[pltpu.HBM(v.shape, v.dtype) for v in srcs],
                   *[pltpu.HBM(s.shape, s.dtype) for s in land_shapes], _sds((8, 128))),
        in_specs=[_HBM] * (2 * na) + [_ANY], out_specs=(_SEM, _SEM, *[_HBM] * (2 * na), _VMEM),
        input_output_aliases={i: 2 + i for i in range(2 * na)}, compiler_params=_EFFECT)(*[hbm(v) for v in srcs], *lands, after)


def _ici_wait(handle, slices, after, name):
    na = (len(handle) - 3) // 2

    def body(*refs):
        src_refs, land_refs, send_sems, recv_sems = refs[:na], refs[na:2 * na], refs[2 * na], refs[2 * na + 1]
        for a in range(na):
            for j in range(_peers(slices)):
                _ici_copy(src_refs, land_refs, send_sems, recv_sems, a, j, slices, False).wait_send()
                _ici_copy(src_refs, land_refs, send_sems, recv_sems, a, j, slices, True).wait_recv()

    thru = handle[2:2 + 2 * na]
    outs = pl.pallas_call(
        body, name=name, out_shape=[pltpu.HBM(v.shape, v.dtype) for v in thru], in_specs=[_HBM] * (2 * na) + [_SEM, _SEM, _ANY],
        out_specs=[_HBM] * (2 * na), input_output_aliases={i: i for i in range(2 * na)}, compiler_params=_EFFECT)(
        *thru, handle[0], handle[1], after)
    return outs[:na], outs[na:]


def _gather_slices(p_ref, land_ref, sender, receiver, c):
    r2 = p_ref.shape[0] // 2
    return p_ref.at[pl.ds(c * r2, r2), :], land_ref.at[sender, pl.ds(c * r2, r2), :]


def _scatter_slices(a_ref, t_ref, sender, receiver, c):
    return a_ref.at[receiver], t_ref.at[sender]


def _halves_slices(g_ref, land_ref, sender_c):
    r2 = g_ref.shape[1] // 2
    return g_ref.at[:, pl.ds((1 - sender_c) * r2, r2), :], land_ref


def _gather_start(arrs, after, name):
    return _ici_start(arrs, [_sds((4,) + v.shape, v.dtype) for v in arrs], _gather_slices, after, name)


def _gather_finish(handle, after, name):
    arrs, stacks = _ici_wait(handle, _gather_slices, after, name + "_wait")
    na = len(stacks)

    def body(*refs):
        s_refs, o_refs, (send_sems, recv_sems) = refs[:na], refs[na:2 * na], refs[2 * na:]
        x, y, c, chips = _place()

        def copy(a, j, cc, to):
            r2 = s_refs[a].shape[1] // 2
            at = (2 * chips[j][0] + chips[j][1], pl.ds(cc * r2, r2), slice(None))
            return pltpu.make_async_remote_copy(src_ref=s_refs[a].at[at], dst_ref=o_refs[a].at[at], send_sem=send_sems.at[3 * a + j],
                                                recv_sem=recv_sems.at[3 * a + j], device_id=to, device_id_type=MESH)

        passed = [copy(a, j, c, (x, y, 1 - c)) for a in range(na) for j in range(3)]
        for cp in passed:
            cp.start()
        for a in range(na):
            for j in range(3):
                copy(a, j, 1 - c, (x, y, c)).wait_recv()
        for cp in passed:
            cp.wait_send()

    stacks = pl.pallas_call(
        body, out_shape=[_sds(v.shape, v.dtype) for v in stacks], in_specs=[_ANY] * na, out_specs=[_ANY] * na,
        input_output_aliases={i: i for i in range(na)}, scratch_shapes=_sems(3 * na), name=name + "_pass")(*stacks)
    chip = 2 * lax.axis_index("x") + lax.axis_index("y")
    return [lax.dynamic_update_slice(s, v[None], (chip, 0, 0)) for s, v in zip(stacks, arrs)]


def _halves_start(gs, after, tag):
    return _ici_start(gs, [_sds((4, v.shape[1] // 2, v.shape[2]), v.dtype) for v in gs], _halves_slices, after, f"rs_halves_{tag}_start")


def _join_halves(fs, name):
    na = len(fs)

    def body(*refs):
        f_refs, o_refs, (send_sems, recv_sems) = refs[:na], refs[na:2 * na], refs[2 * na:]
        x, y, c, _ = _place()

        def copy(a, cc, to):
            r2 = f_refs[a].shape[0] // 2
            return pltpu.make_async_remote_copy(src_ref=f_refs[a].at[pl.ds(cc * r2, r2), :], dst_ref=o_refs[a].at[pl.ds(cc * r2, r2), :],
                                                send_sem=send_sems.at[a], recv_sem=recv_sems.at[a], device_id=to, device_id_type=MESH)

        cps = [copy(a, c, (x, y, 1 - c)) for a in range(na)]
        for cp in cps:
            cp.start()
        for a in range(na):
            copy(a, 1 - c, (x, y, c)).wait_recv()
        for cp in cps:
            cp.wait_send()

    return pl.pallas_call(
        body, out_shape=[_sds(v.shape, v.dtype) for v in fs], in_specs=[_ANY] * na, out_specs=[_ANY] * na,
        input_output_aliases={i: i for i in range(na)}, scratch_shapes=_sems(na), name=name)(*fs)


def _sum_chips(a, t, chip, ci, name):
    _, r2, n = t.shape
    tm = _row_tile(r2)
    nb = r2 // tm

    def body(k_ref, a_ref, t1_ref, t2_ref, t3_ref, o_ref):
        o_ref[...] = ((a_ref[...].astype(F32) + t1_ref[...].astype(F32)) + t2_ref[...].astype(F32)) + t3_ref[...].astype(F32)

    def slot(j):
        return pl.BlockSpec((None, tm, n), lambda i, k_ref: (lax.rem(k_ref[0] + j, 4), i, 0))

    return pl.pallas_call(
        body, grid_spec=pltpu.PrefetchScalarGridSpec(num_scalar_prefetch=1, grid=(nb,), in_specs=[slot(0), slot(1), slot(2), slot(3)],
                                                     out_specs=pl.BlockSpec((tm, n), lambda i, k_ref: (k_ref[1] * nb + i, 0))),
        out_shape=_sds((2 * r2, n)), name=name, compiler_params=_PAR)(jnp.stack([chip, ci]).astype(jnp.int32), a, t, t, t)


def _add_cast(g, recv, c, name):
    _, r2, n = recv.shape

    def body(c_ref, a_ref, b_ref, o_ref):
        o_ref[...] = (a_ref[...] + b_ref[...]).astype(BF16)

    spec = pl.BlockSpec((None, r2, n), lambda k, c_ref: (k, 0, 0))
    return pl.pallas_call(
        body, grid_spec=pltpu.PrefetchScalarGridSpec(
            num_scalar_prefetch=1, grid=(4,), in_specs=[pl.BlockSpec((None, r2, n), lambda k, c_ref: (k, c_ref[0], 0)), spec], out_specs=spec),
        out_shape=_sds(recv.shape, BF16), name=name, compiler_params=_PAR)(c.reshape(1).astype(jnp.int32), g, recv)


def _sum_lead(t, name, tm=256):
    P, R, n = t.shape
    tm = _row_tile(R, tm)

    def body(t_ref, o_ref):
        acc = t_ref[0].astype(F32)
        for j in range(1, P):
            acc = acc + t_ref[j].astype(F32)
        o_ref[...] = acc

    return pl.pallas_call(body, grid=(R // tm,), in_specs=[pl.BlockSpec((P, tm, n), lambda i: (0, i, 0))],
                          out_specs=pl.BlockSpec((tm, n), lambda i: (i, 0)), out_shape=_sds((R, n)), name=name, compiler_params=_PAR)(t)


def _ada_fwd(c16, ada_w, ada_b_cols, tn=512):
    L, _, n = ada_w.shape

    def body(c_ref, w_ref, b_ref, o_ref):
        o_ref[0] = _dot(_silu(c_ref[...]).astype(BF16), w_ref[0].astype(BF16), _NN) + b_ref[0]

    return pl.pallas_call(
        body, grid=(L, n // tn),
        in_specs=[pl.BlockSpec((16, D), lambda l, j: (0, 0)), pl.BlockSpec((1, D, tn), lambda l, j: (l, 0, j)), pl.BlockSpec((1, 1, tn), lambda l, j: (l, 0, j))],
        out_specs=pl.BlockSpec((1, 16, tn), lambda l, j: (l, 0, j)), out_shape=_sds((L, 16, n)), name="ada_fwd",
        compiler_params=pltpu.CompilerParams(dimension_semantics=("parallel", "parallel")))(c16, ada_w, ada_b_cols)


def _ada_bwd(c16, dmod, tn=512):
    L, _, n = dmod.shape

    def body(c_ref, d_ref, o_ref):
        o_ref[0] = _dot(_silu(c_ref[...]).astype(BF16), d_ref[0].astype(BF16), _TN)

    return pl.pallas_call(
        body, grid=(L, n // tn), in_specs=[pl.BlockSpec((16, D), lambda l, j: (0, 0)), pl.BlockSpec((1, 16, tn), lambda l, j: (l, 0, j))],
        out_specs=pl.BlockSpec((1, D, tn), lambda l, j: (l, 0, j)), out_shape=_sds((L, D, n)), name="ada_bwd",
        compiler_params=pltpu.CompilerParams(dimension_semantics=("parallel", "parallel")))(c16, dmod)


def _adam_math(w, g, m, v):
    mn = ADAM_B1 * m + (1.0 - ADAM_B1) * g
    vn = ADAM_B2 * v + (1.0 - ADAM_B2) * (g * g)
    m_hat = mn / (1.0 - ADAM_B1 ** ADAM_STEP)
    v_hat = vn / (1.0 - ADAM_B2 ** ADAM_STEP)
    return -ADAM_LR * (m_hat / (jnp.sqrt(v_hat) + ADAM_EPS) + ADAM_WD * w), mn, vn


def _adamw(w, g, m, v, name):
    R, n = w.shape
    tm = _row_tile(R)

    def body(w_ref, g_ref, m_ref, v_ref, d_ref, nm_ref, nv_ref):
        d_ref[...], nm_ref[...], nv_ref[...] = _adam_math(w_ref[...], g_ref[...], m_ref[...], v_ref[...])

    spec = pl.BlockSpec((tm, n), lambda i: (i, 0))
    return pl.pallas_call(body, grid=(R // tm,), in_specs=[spec] * 4, out_specs=(spec,) * 3, out_shape=(_sds((R, n)),) * 3,
                          name=name, compiler_params=_PAR)(w, g, m, v)


def _adamw_cols(w, g0, g1, m, v, name):
    fwd, back = (lambda t: jnp.transpose(t, (2, 0, 1))), (lambda t: jnp.transpose(t, (1, 2, 0)))
    gt = jnp.stack([g0.T, g1.T], 1)
    n, _, r = gt.shape
    tr = max(t for t in range(1, 257) if n % t == 0)

    def body(w_ref, g_ref, m_ref, v_ref, d_ref, nm_ref, nv_ref):
        d_ref[...], nm_ref[...], nv_ref[...] = _adam_math(w_ref[...], g_ref[...], m_ref[...], v_ref[...])

    spec = pl.BlockSpec((tr, 2, r), lambda i: (i, 0, 0))
    outs = pl.pallas_call(body, grid=(n // tr,), in_specs=[spec] * 4, out_specs=(spec,) * 3, out_shape=(_sds(gt.shape),) * 3,
                          name=name, compiler_params=_PAR)(fwd(w), gt, fwd(m), fwd(v))
    return (back(gt), *[back(o) for o in outs])


def _adamw_layers(w, g0, g1, m, v, after, name):
    _, r, n = w.shape
    tm = _row_tile(r)
    nb = r // tm

    def body(w_ref, g0_ref, g1_ref, m_ref, v_ref, after_ref, g_ref, d_ref, nm_ref, nv_ref):
        gv = jnp.where(pl.program_id(0) == 0, g0_ref[...], g1_ref[...])
        g_ref[...] = gv
        d_ref[...], nm_ref[...], nv_ref[...] = _adam_math(w_ref[...], gv, m_ref[...], v_ref[...])

    spec = pl.BlockSpec((None, tm, n), lambda l, i: (l, i, 0))
    g0_spec = pl.BlockSpec((tm, n), lambda l, i: (i * (1 - l) + (nb - 1) * l, 0))
    g1_spec = pl.BlockSpec((tm, n), lambda l, i: (i * l, 0))
    return pl.pallas_call(body, grid=(2, nb), in_specs=[spec, g0_spec, g1_spec, spec, spec, _ANY], out_specs=(spec,) * 4,
                          out_shape=(_sds(w.shape),) * 4, name=name,
                          compiler_params=pltpu.CompilerParams(dimension_semantics=("arbitrary", "arbitrary")))(w, g0, g1, m, v, after)


_W_NAMES = ["ada_w", "ada_b", "norm1_w", "w_in", "q_a_norm", "w_q_b", "kv_a_norm", "w_kv_b", "q_norm", "k_norm", "pool_w",
            "pool_scale", "ssd_conv_w", "ssd_conv_b", "ssd_dt_bias", "ssd_a_log", "ssd_d", "ssd_norm_w", "w_branch", "w_out",
            "norm2_w", "ffn_up", "ffn_conv_w", "ffn_conv_b", "ffn_down"]
_BIG = [("w_in", (D, IN_DIM // 4), 1), ("w_q_b", (Q_RANK, 192), 1), ("w_kv_b", (KV_RANK, 256), 1), ("w_branch", (512, D), 0),
        ("w_out", (256, D), 0), ("ffn_up", (D, 2 * FFN // 4), 1), ("ffn_down", (FFN // 4, D), 0)]

_SMALL = [("norm1_w", (D,)), ("q_a_norm", (Q_RANK,)), ("kv_a_norm", (KV_RANK,)), ("q_norm", (96,)), ("k_norm", (96,)),
          ("pool_w", (4, 128, 128)), ("pool_scale", (512,)), ("ssd_conv_w", (4, SSD_CONV_DIM)), ("ssd_conv_b", (SSD_CONV_DIM,)),
          ("ssd_dt_bias", (SSD_HEADS,)), ("ssd_a_log", (SSD_HEADS,)), ("ssd_d", (SSD_HEADS,)), ("ssd_norm_w", (D,)), ("norm2_w", (D,)),
          ("ffn_conv_w", (3, 2 * FFN)), ("ffn_conv_b", (2 * FFN,))]
_CONV_SHARDED = {"ssd_conv_w": SSD_CONV_DIM // 4, "ffn_conv_w": 2 * FFN // 4}


def _pack_flat(arrs, mult):
    flat = jnp.concatenate([a.astype(F32).reshape(-1) for a in arrs])
    rows = -(-flat.shape[0] // (128 * mult)) * mult
    return jnp.pad(flat, (0, rows * 128 - flat.shape[0])).reshape(rows, 128), [a.shape for a in arrs]


def _unpack_flat(packed, shapes):
    flat, out, off = packed.reshape(-1), [], 0
    for s in shapes:
        n = 1
        for d in s:
            n *= d
        out.append(flat[off:off + n].reshape(s))
        off += n
    return out


_EARLY = ["w_in", "w_q_b", "w_kv_b"]
_LATE = ["w_branch", "w_out", "ffn_up", "ffn_down"]


def _early_weights(a, l, stacks, conv_full):
    p = {n: a[n][l] for n in _W_NAMES if n not in ("ada_w", "ada_b")}
    p.update({n: conv_full[n][l] for n in conv_full})
    p.update(w_in4=stacks[0], w_q_b4=stacks[1], w_kv_b4=stacks[2])
    return p


def _late_weights(stacks):
    return _prep_late(dict(w_branch=stacks[0].reshape(2048, D), w_out=stacks[1].reshape(D, D), ffn_up4=stacks[2],
                           ffn_down=stacks[3].reshape(FFN, D)))


def _reduce_start(halves, ci, after, tag):
    chip_sum = []
    for t, h in halves.items():
        gs, recv = _ici_wait(h, _halves_slices, after, f"rs_halves_{t}_wait")
        chip_sum += [_add_cast(g, r, ci, f"rs_add_{t}") for g, r in zip(gs, recv)]
    return _ici_start(chip_sum, [_sds(v.shape, v.dtype) for v in chip_sum], _scatter_slices, after, f"rs_scatter_{tag}_start")


def _reduce_finish(started, after, tag):
    chip_sum, got = _ici_wait(started, _scatter_slices, after, f"rs_scatter_{tag}_wait")
    chip, ci = 2 * lax.axis_index("x") + lax.axis_index("y"), lax.axis_index("c")
    return _join_halves([_sum_chips(s, t, chip, ci, f"rs_sum_{tag}") for s, t in zip(chip_sum, got)], f"rs_join_{tag}")


def kernel(x, c, positions, ada_w, ada_b, norm1_w, w_in, q_a_norm, w_q_b, kv_a_norm, w_kv_b, q_norm, k_norm, pool_w, pool_scale, ssd_conv_w, ssd_conv_b, ssd_dt_bias, ssd_a_log, ssd_d, ssd_norm_w, w_branch, w_out, norm2_w, ffn_up, ffn_conv_w, ffn_conv_b, ffn_down, loss_target, m_ada_w, m_ada_b, m_norm1_w, m_w_in, m_q_a_norm, m_w_q_b, m_kv_a_norm, m_w_kv_b, m_q_norm, m_k_norm, m_pool_w, m_pool_scale, m_ssd_conv_w, m_ssd_conv_b, m_ssd_dt_bias, m_ssd_a_log, m_ssd_d, m_ssd_norm_w, m_w_branch, m_w_out, m_norm2_w, m_ffn_up, m_ffn_conv_w, m_ffn_conv_b, m_ffn_down, v_ada_w, v_ada_b, v_norm1_w, v_w_in, v_q_a_norm, v_w_q_b, v_kv_a_norm, v_w_kv_b, v_q_norm, v_k_norm, v_pool_w, v_pool_scale, v_ssd_conv_w, v_ssd_conv_b, v_ssd_dt_bias, v_ssd_a_log, v_ssd_d, v_ssd_norm_w, v_w_branch, v_w_out, v_norm2_w, v_ffn_up, v_ffn_conv_w, v_ffn_conv_b, v_ffn_down):
    a = dict(locals())
    xi, yi, ci = lax.axis_index("x"), lax.axis_index("y"), lax.axis_index("c")
    chip = 2 * xi + yi
    dev = 2 * chip + ci
    ncol = 6 * D // 4

    c_all = _allgather8(c.reshape(8, 128), "gather_c").reshape(8, D)
    c16 = jnp.pad(c_all, ((0, 8), (0, 0)))
    ada_b_cols = lax.dynamic_slice_in_dim(ada_b, chip * ncol, ncol, axis=1).reshape(2, 1, ncol)
    mod_part = _ada_fwd(c16, ada_w, ada_b_cols)[:, :8]
    small1, shapes1 = _pack_flat([mod_part, ssd_conv_w, ffn_conv_w], 8)
    got1 = _allgather8(small1, "gather_mod").reshape(8, -1, 128)
    per_chip = [_unpack_flat(got1[2 * k], shapes1) for k in range(4)]
    mod_all = jnp.concatenate([per_chip[k][0] for k in range(4)], -1)
    conv_full = {"ssd_conv_w": jnp.concatenate([per_chip[k][1] for k in range(4)], -1),
                 "ffn_conv_w": jnp.concatenate([per_chip[k][2] for k in range(4)], -1)}
    mod_mine = lax.dynamic_index_in_dim(mod_all, dev, axis=1, keepdims=False).reshape(2, 6, D)
    mods = [jnp.pad(mod_mine[l], ((0, 2), (0, 0))) for l in range(2)]

    big = _EARLY + _LATE
    shard = lambda names, l: [a[n][l].astype(BF16) for n in names]
    g0a = _gather_start(shard(_EARLY, 0), mods[0], "gather_0a")
    g0b = _gather_start(shard(_LATE, 0), g0a[-1], "gather_0b")
    g1 = _gather_start(shard(big, 1), g0b[-1], "gather_1")
    mods[0] = mods[0] + g1[-1][0, 0]

    def get_layer(l, after):
        if l == 0:
            return (_early_weights(a, 0, _gather_finish(g0a, mods[0], "gather_0a"), conv_full),
                    lambda aft: _late_weights(_gather_finish(g0b, aft, "gather_0b")))
        stacks = _gather_finish(g1, after, "gather_1")
        return _early_weights(a, 1, stacks[:3], conv_full), lambda aft: _late_weights(stacks[3:])

    halves, scatters = {}, {}

    def bwd_mod(l):
        return mods[l] if l == 1 else mods[0] + halves["1a"][-1][0, 0]

    def emit(l, late):
        if l == 1:
            halves["1b"] = _halves_start([late[n] for n in _LATE], late["ffn_down"], "1b")
            return halves["1b"][-1][0, 0]
        scatters["1"] = _reduce_start({"1b": halves["1b"], "1a": halves["1a"]}, ci, late["ffn_down"], "1")
        halves["0b"] = _halves_start([late[n] for n in _LATE], scatters["1"][-1], "0b")
        return halves["0b"][-1][0, 0]

    def mid(l, after):
        if l == 0:
            scatters["0b"] = _reduce_start({"0b": halves["0b"]}, ci, after, "0b")
            return scatters["0b"][-1][0, 0]

    def done(l, grads_l):
        halves[f"{l}a"] = _halves_start([grads_l[n] for n in _EARLY], halves[f"{l}b"][-1], f"{l}a")
        return halves[f"{l}a"][-1][0, 0]

    lpart, grad_x, grads, dmods = _local_step(x[0], loss_target[0], positions[0], mods, get_layer, bwd_mod, emit, mid, done)
    loss = lax.psum(lpart, ("x", "y", "c"))
    red1 = dict(zip(_LATE + _EARLY, _reduce_finish(scatters["1"], grad_x, "1")))
    red0b = _reduce_finish(scatters["0b"], red1["w_in"], "0b")
    scatters["0a"] = _reduce_start({"0a": halves["0a"]}, ci, red0b[0], "0a")

    small2, shapes2 = _pack_flat([jnp.stack(dmods)] + [grads[l][n] for l in range(2) for n, _ in _SMALL], 16)
    got2 = _allgather8(small2, "gather_small").reshape(8, -1, 128)
    tot = _unpack_flat(_sum_lead(got2, "sum_small"), shapes2)
    g = {"ada_b": tot[0].reshape(2, 6 * D)}
    for i, (n, _) in enumerate(_SMALL):
        g[n] = jnp.stack([tot[1 + i], tot[1 + len(_SMALL) + i]])
    for n, w in _CONV_SHARDED.items():
        g[n] = lax.dynamic_slice_in_dim(g[n], chip * w, w, axis=2)
    nd = 2 * 6 * D // 128
    dmod_all = jnp.transpose(got2[:, :nd].reshape(8, 2, 6 * D), (1, 0, 2))
    dmod_cols = lax.dynamic_slice_in_dim(jnp.pad(dmod_all, ((0, 0), (0, 8), (0, 0))), chip * ncol, ncol, axis=2)
    g["ada_w"] = _ada_bwd(c16, dmod_cols)

    delta, new_m, new_v = {}, {}, {}
    token = scatters["0a"][-1]
    for n, r0 in zip(_LATE, red0b):
        g[n], delta[n], new_m[n], new_v[n] = _adamw_layers(a[n], r0, red1[n], a["m_" + n], a["v_" + n], token, f"adamw_{n}")
    shp = ada_w.shape
    r2 = lambda t: t.reshape(-1, shp[-1])
    delta["ada_w"], new_m["ada_w"], new_v["ada_w"] = (
        o.reshape(shp) for o in _adamw(r2(ada_w), r2(g["ada_w"]), r2(m_ada_w), r2(v_ada_w), "adamw_ada_w"))
    behind = (delta[_LATE[-1]][0, 0, :1] + delta["ada_w"][0, 0, :1]).reshape(1)
    red0a = _reduce_finish(scatters["0a"], behind, "0a")
    for n, r0 in zip(_EARLY, red0a):
        if n == "w_in":
            g[n], delta[n], new_m[n], new_v[n] = _adamw_cols(a[n], r0, red1[n], a["m_" + n], a["v_" + n], f"adamw_{n}")
        else:
            g[n], delta[n], new_m[n], new_v[n] = _adamw_layers(a[n], r0, red1[n], a["m_" + n], a["v_" + n], token, f"adamw_{n}")
    rest = [n for n in _W_NAMES if n not in big and n != "ada_w"]
    packs = [_pack_flat([t[n] if pre is None else t[pre + n] for n in rest], 128)[0]
             for t, pre in ((a, None), (g, None), (a, "m_"), (a, "v_"))]
    rest_shapes = [a[n].shape for n in rest]
    outs = [_unpack_flat(o, rest_shapes) for o in _adamw(*packs, "adamw_rest")]
    for i, n in enumerate(rest):
        delta[n], new_m[n], new_v[n] = outs[0][i], outs[1][i], outs[2][i]

    return (loss, grad_x[None], *[g[n] for n in _W_NAMES], *[delta[n] for n in _W_NAMES],
            *[new_m[n] for n in _W_NAMES], *[new_v[n] for n in _W_NAMES])
```

```python
import functools

import jax
import jax.numpy as jnp
from jax import lax
from jax.experimental import pallas as pl
from jax.experimental.pallas import tpu as pltpu

F32 = jnp.float32
BF16 = jnp.bfloat16
MESH = pl.DeviceIdType.MESH
HI = lax.Precision.HIGHEST

D = 1024
N_HEADS = 8
NOPE, ROPE_DIM = 64, 32
Q_RANK, KV_RANK = 384, 256
POOL_WINDOWS = (2, 4, 8, 16)
SSD_HEADS, SSD_P, SSD_N, SSD_L = 16, 64, 128, 128
SSD_INNER = 1024
SSD_CONV_DIM = 1536
FFN = 2816
EPS = 1e-6
ROPE_THETA = 10000.0
OG, OZ, OX, OP, OQ, OKV, ODT, IN_PAD = 0, 3072, 4096, 5632, 6144, 6528, 6912, 7040
IN_DIM = 6832
ADAM_LR, ADAM_B1, ADAM_B2, ADAM_EPS, ADAM_WD, ADAM_STEP = 0.001, 0.9, 0.999, 1e-08, 0.01, 10

_ARB = pltpu.CompilerParams(dimension_semantics=("arbitrary",))
_PAR = pltpu.CompilerParams(dimension_semantics=("parallel",))


def _pick(n, pref):
    if n <= pref:
        return n
    best = None
    for t in range(128, pref + 1, 128):
        if n % t == 0:
            best = t
    assert best is not None, (n, pref)
    return best


def _row_tile(r, cap=256):
    best = None
    for t in range(16, min(r, cap) + 1, 16):
        if r % t == 0:
            best = t
    assert best is not None, r
    return best


def _sds(shape, dtype=F32):
    return jax.ShapeDtypeStruct(tuple(shape), dtype)


def _iota(shape, dim):
    return lax.broadcasted_iota(jnp.int32, shape, dim)


def _sigmoid(x):
    return 0.5 * jnp.tanh(0.5 * x) + 0.5


def _silu(x):
    return x * _sigmoid(x)


def _dsilu(x):
    s = _sigmoid(x)
    return s * (1.0 + x * (1.0 - s))


def _dot(a, b, dims):
    return lax.dot_general(a, b, (dims, ((), ())), preferred_element_type=F32)


_NN, _NT, _TN = ((1,), (0,)), ((1,), (1,)), ((0,), (0,))


def _dot_hi(a, b, dims=_NN):
    return lax.dot_general(a, b, (dims, ((), ())), preferred_element_type=F32, precision=HI)


def _shift_down(x, j):
    n = x.shape[0]
    return jnp.where(_iota(x.shape, 0) >= j, pltpu.roll(x, j, 0), 0.0)


def _shift_up(x, j):
    n = x.shape[0]
    return jnp.where(_iota(x.shape, 0) < n - j, pltpu.roll(x, n - j, 0), 0.0)


def _mm(a, b, mode="nn", out_dtype=F32, tm=2048, tn=512, tk=4096, res=None, gate=None, name="mm"):
    if mode == "nn":
        (M, K), (K2, N) = a.shape, b.shape
    elif mode == "nt":
        (M, K), (N, K2) = a.shape, b.shape
    else:
        (K, M), (K2, N) = a.shape, b.shape
    assert K == K2, (a.shape, b.shape, mode)
    tm, tn, tk = _pick(M, tm), _pick(N, tn), _pick(K, tk)
    nk = K // tk
    dims = {"nn": _NN, "nt": _NT, "tn": _TN}[mode]
    fused = res is not None

    def body(*refs):
        a_ref, b_ref = refs[:2]

        def finish(acc):
            if fused:
                r_ref, g_ref, o_ref, raw_ref = refs[2:6]
                raw_ref[...] = acc
                o_ref[...] = r_ref[...] + g_ref[...] * acc
            else:
                refs[2][...] = acc.astype(out_dtype)

        _mm_steps(a_ref, b_ref, dims, nk, refs[-1] if nk > 1 else None, finish)

    if mode == "nn":
        a_spec = pl.BlockSpec((tm, tk), lambda i, j, k: (i, k))
        b_spec = pl.BlockSpec((tk, tn), lambda i, j, k: (k, j))
    elif mode == "nt":
        a_spec = pl.BlockSpec((tm, tk), lambda i, j, k: (i, k))
        b_spec = pl.BlockSpec((tn, tk), lambda i, j, k: (j, k))
    else:
        a_spec = pl.BlockSpec((tk, tm), lambda i, j, k: (k, i))
        b_spec = pl.BlockSpec((tk, tn), lambda i, j, k: (k, j))
    o_spec = pl.BlockSpec((tm, tn), lambda i, j, k: (i, j))
    in_specs, args = [a_spec, b_spec], [a, b]
    out_shape, out_specs = _sds((M, N), out_dtype), o_spec
    if fused:
        in_specs += [o_spec, pl.BlockSpec((1, tn), lambda i, j, k: (0, j))]
        args += [res, gate]
        out_shape, out_specs = (_sds((M, N)), _sds((M, N))), (o_spec, o_spec)
    return pl.pallas_call(
        body, grid=(M // tm, N // tn, nk), in_specs=in_specs, out_specs=out_specs, out_shape=out_shape,
        scratch_shapes=[pltpu.VMEM((tm, tn), F32)] if nk > 1 else [], name=name,
        compiler_params=pltpu.CompilerParams(dimension_semantics=("parallel", "parallel", "arbitrary")),
    )(*args)


def _mm_steps(a_ref, b_ref, dims, nk, acc_ref, finish):
    part = _dot(a_ref[...].astype(BF16), b_ref[...].astype(BF16), dims)
    if nk == 1:
        finish(part)
        return
    k = pl.program_id(2)

    @pl.when(k == 0)
    def _():
        acc_ref[...] = part

    @pl.when(k > 0)
    def _():
        acc_ref[...] += part

    @pl.when(k == nk - 1)
    def _():
        finish(acc_ref[...])


def _mm_blocks(a, b, dims, grid, a_spec, b_spec, o_spec, out_shape, acc_shape, name, into=None):
    nk = grid[2]
    extra = 0 if into is None else 1

    def body(a_ref, b_ref, *refs):
        o_ref, scratch = refs[extra], refs[extra + 1:]

        def finish(acc):
            o_ref[...] = acc.astype(o_ref.dtype)

        _mm_steps(a_ref, b_ref, dims, nk, scratch[0] if nk > 1 else None, finish)

    return pl.pallas_call(
        body, grid=grid, in_specs=[a_spec, b_spec] + [pl.BlockSpec(memory_space=pl.ANY)] * extra, out_specs=o_spec, out_shape=out_shape,
        input_output_aliases={2: 0} if extra else {},
        scratch_shapes=[pltpu.VMEM(acc_shape, F32)] if nk > 1 else [], name=name,
        compiler_params=pltpu.CompilerParams(dimension_semantics=("parallel", "parallel", "arbitrary")),
    )(a, b, *([] if into is None else [into]))


def _branch_dw(o, dy, row0, into, name, tm=512, tn=512):
    S, M = o.shape
    return _mm_blocks(o, dy, _TN, (M // tm, D // tn, 1), pl.BlockSpec((S, tm), lambda i, j, k: (0, i)),
                      pl.BlockSpec((S, tn), lambda i, j, k: (0, j)), pl.BlockSpec((tm, tn), lambda i, j, k: (i + row0 // tm, j)),
                      _sds((2048, D)), (tm, tn), name, into)


_UP_SHARD = 2 * FFN // 4


def _up_fwd(h2, wup4, name, tm=2048):
    S = h2.shape[0]
    tm = min(tm, S)
    return _mm_blocks(h2, wup4, _NN, (S // tm, 4, 1), pl.BlockSpec((tm, D), lambda i, j, k: (i, 0)),
                      pl.BlockSpec((None, D, _UP_SHARD), lambda i, j, k: (j, 0, 0)), pl.BlockSpec((tm, _UP_SHARD), lambda i, j, k: (i, j)),
                      _sds((S, 2 * FFN)), (tm, _UP_SHARD), name)


def _up_dx(dup2, wup4, name, tm=2048, tn=512):
    S = dup2.shape[1]
    tm = min(tm, S)
    return _mm_blocks(dup2, wup4, _NT, (S // tm, D // tn, 4), pl.BlockSpec((None, tm, _UP_SHARD), lambda i, j, k: (lax.div(k, 2), i, lax.rem(k, 2))),
                      pl.BlockSpec((None, tn, _UP_SHARD), lambda i, j, k: (k, j, 0)), pl.BlockSpec((tm, tn), lambda i, j, k: (i, j)),
                      _sds((S, D)), (tm, tn), name)


def _up_dw(h2, dup2, name, tm=1024, tk=4096):
    S = h2.shape[0]
    tk = min(tk, S)
    return _mm_blocks(h2, dup2, _TN, (D // tm, 4, S // tk), pl.BlockSpec((tk, tm), lambda i, j, k: (k, i)),
                      pl.BlockSpec((None, tk, _UP_SHARD), lambda i, j, k: (lax.div(j, 2), k, lax.rem(j, 2))),
                      pl.BlockSpec((None, tm, _UP_SHARD), lambda i, j, k: (j, i, 0)), _sds((4, D, _UP_SHARD)), (tm, _UP_SHARD), name)


def _row_spec(tm, n):
    return pl.BlockSpec((tm, n), lambda i: (i, 0))


def _vec_spec(n, rows=1):
    return pl.BlockSpec((rows, n), lambda i: (0, 0))


def _ln_mod(x, nw, sc, sh, name, tm=256):
    S = x.shape[0]

    def body(x_ref, nw_ref, sc_ref, sh_ref, o_ref):
        xv = x_ref[...]
        r = lax.rsqrt(jnp.mean(xv * xv, -1, keepdims=True) + EPS)
        o_ref[...] = ((xv * r * nw_ref[...]) * (1.0 + sc_ref[...]) + sh_ref[...]).astype(BF16)

    return pl.pallas_call(
        body, grid=(S // tm,), in_specs=[_row_spec(tm, D)] + [_vec_spec(D)] * 3, out_specs=_row_spec(tm, D),
        out_shape=_sds((S, D), BF16), name=name, compiler_params=_PAR)(x, nw, sc, sh)


def _ln_mod_bwd(x, dh, dres, nw, sc, name, tm=256):
    S = x.shape[0]

    def body(x_ref, dh_ref, dres_ref, nw_ref, sc_ref, dx_ref, st_ref):
        @pl.when(pl.program_id(0) == 0)
        def _():
            st_ref[...] = jnp.zeros_like(st_ref)

        xv, dhv, nwv = x_ref[...], dh_ref[...], nw_ref[...]
        r = lax.rsqrt(jnp.mean(xv * xv, -1, keepdims=True) + EPS)
        xhat = xv * r
        dn = dhv * (1.0 + sc_ref[...])
        g = dn * nwv
        dx_ref[...] = dres_ref[...] + r * (g - xhat * jnp.mean(g * xhat, -1, keepdims=True))
        st_ref[0:1, :] += jnp.sum(dhv, 0, keepdims=True)
        st_ref[1:2, :] += jnp.sum(dhv * (xhat * nwv), 0, keepdims=True)
        st_ref[2:3, :] += jnp.sum(dn * xhat, 0, keepdims=True)

    return pl.pallas_call(
        body, grid=(S // tm,), in_specs=[_row_spec(tm, D)] * 3 + [_vec_spec(D)] * 2,
        out_specs=(_row_spec(tm, D), _vec_spec(D, 8)), out_shape=(_sds((S, D)), _sds((8, D))),
        name=name, compiler_params=_ARB)(x, dh, dres, nw, sc)


def _gate_bwd(dx, out, g, name, tm=256):
    S = dx.shape[0]

    def body(dx_ref, o_ref, g_ref, dz_ref, dg_ref):
        @pl.when(pl.program_id(0) == 0)
        def _():
            dg_ref[...] = jnp.zeros_like(dg_ref)

        dxv = dx_ref[...]
        dz_ref[...] = (dxv * g_ref[...]).astype(BF16)
        dg_ref[0:1, :] += jnp.sum(dxv * o_ref[...], 0, keepdims=True)

    return pl.pallas_call(
        body, grid=(S // tm,), in_specs=[_row_spec(tm, D)] * 2 + [_vec_spec(D)],
        out_specs=(_row_spec(tm, D), _vec_spec(D, 8)), out_shape=(_sds((S, D), BF16), _sds((8, D))),
        name=name, compiler_params=_ARB)(dx, out, g)


def _loss_grad(y, t, tm=256):
    S = y.shape[0]

    def body(y_ref, t_ref, dy_ref, l_ref):
        @pl.when(pl.program_id(0) == 0)
        def _():
            l_ref[...] = jnp.zeros_like(l_ref)

        e = y_ref[...] - t_ref[...]
        dy_ref[...] = e * (1.0 / D)
        l_ref[...] += 0.5 * jnp.sum(jnp.mean(e * e, -1, keepdims=True), 0, keepdims=True)

    return pl.pallas_call(
        body, grid=(S // tm,), in_specs=[_row_spec(tm, D)] * 2,
        out_specs=(_row_spec(tm, D), pl.BlockSpec((8, 128), lambda i: (0, 0))),
        out_shape=(_sds((S, D)), _sds((8, 128))), name="loss_grad", compiler_params=_ARB)(y, t)


_CONV_ROWS = 256
_HALO = 8


def _rows(ref, lo, hi):
    S, c = ref.shape
    parts = [jnp.zeros((-lo, c), F32)] if lo < 0 else []
    parts.append(ref[max(lo, 0):min(hi, S), :])
    if hi > S:
        parts.append(jnp.zeros((hi - S, c), F32))
    return parts[0] if len(parts) == 1 else jnp.concatenate(parts, 0)


def _conv_rows(ext, w, b, n):
    K = w.shape[0]
    acc = ext[_HALO:_HALO + n] * w[K - 1:K, :] + b
    for j in range(1, K):
        acc = acc + pltpu.roll(ext, j, 0)[_HALO:_HALO + n] * w[K - 1 - j:K - j, :]
    return acc


def _conv_rows_bwd(ext, w, dc, n):
    K, m = w.shape[0], dc.shape[0]
    d0 = dc[0:n]
    dx = d0 * w[K - 1:K, :]
    dws = [jnp.sum(d0 * ext[_HALO:_HALO + n], 0, keepdims=True)]
    for j in range(1, K):
        dx = dx + pltpu.roll(dc, m - j, 0)[0:n] * w[K - 1 - j:K - j, :]
        dws.append(jnp.sum(d0 * pltpu.roll(ext, j, 0)[_HALO:_HALO + n], 0, keepdims=True))
    return dx, dws[::-1], jnp.sum(d0, 0, keepdims=True)


def _col_spec(S, tc, off=0):
    return pl.BlockSpec((S, tc), lambda j: (0, j + off))


def _ssd_pre(proj, cw, cb, tc=256):
    S, n = proj.shape[0], SSD_CONV_DIM
    R = min(_CONV_ROWS, S)

    def body(x_ref, w_ref, b_ref, o_ref):
        wv, bv = w_ref[...], b_ref[...]
        for r0 in range(0, S, R):
            o_ref[r0:r0 + R, :] = _silu(_conv_rows(_rows(x_ref, r0 - _HALO, r0 + R), wv, bv, R))

    return pl.pallas_call(
        body, grid=(n // tc,),
        in_specs=[_col_spec(S, tc, OX // tc), pl.BlockSpec((4, tc), lambda j: (0, j)), pl.BlockSpec((1, tc), lambda j: (0, j))],
        out_specs=_col_spec(S, tc), out_shape=_sds((S, n)), name="ssd_pre", compiler_params=_PAR)(proj, cw, cb)


def _ssd_pre_bwd(proj, dxc, cw, cb, dproj, tc=256):
    S, n = proj.shape[0], SSD_CONV_DIM
    R = min(_CONV_ROWS, S)

    def body(x_ref, d_ref, w_ref, b_ref, dproj_in, dx_ref, dw_ref, db_ref):
        wv, bv = w_ref[...], b_ref[...]
        acc = [jnp.zeros((1, tc), F32)] * 5
        for r0 in range(0, S, R):
            ext = _rows(x_ref, r0 - _HALO, r0 + R + _HALO)
            dc = _rows(d_ref, r0, r0 + R + _HALO) * _dsilu(_conv_rows(ext, wv, bv, R + _HALO))
            dx, dws, db = _conv_rows_bwd(ext, wv, dc, R)
            dx_ref[r0:r0 + R, :] = dx.astype(BF16)
            acc = [s + d for s, d in zip(acc, dws + [db])]
        for k in range(4):
            dw_ref[k:k + 1, :] = acc[k]
        db_ref[...] = acc[4]

    wspec, bspec = pl.BlockSpec((4, tc), lambda j: (0, j)), pl.BlockSpec((1, tc), lambda j: (0, j))
    return pl.pallas_call(
        body, grid=(n // tc,), in_specs=[_col_spec(S, tc, OX // tc), _col_spec(S, tc), wspec, bspec, _ANY],
        out_specs=(_col_spec(S, tc, OX // tc), wspec, bspec), out_shape=(_sds(dproj.shape, BF16), _sds((4, n)), _sds((1, n))),
        input_output_aliases={4: 0}, name="ssd_pre_bwd", compiler_params=_PAR)(proj, dxc, cw, cb, dproj)


def _ffn_act(up, cw, cb, tc=256):
    S, nb = up.shape[0], FFN // tc
    R = min(_CONV_ROWS, S)

    def body(g_ref, v_ref, wg_ref, wv_ref, bg_ref, bv_ref, o_ref):
        wg, wv, bg, bv = wg_ref[...], wv_ref[...], bg_ref[...], bv_ref[...]
        for r0 in range(0, S, R):
            cg = _conv_rows(_rows(g_ref, r0 - _HALO, r0 + R), wg, bg, R)
            cv = _conv_rows(_rows(v_ref, r0 - _HALO, r0 + R), wv, bv, R)
            o_ref[r0:r0 + R, :] = (_silu(cg) * cv).astype(BF16)

    def wspec(off):
        return pl.BlockSpec((3, tc), lambda j: (0, j + off))

    def bspec(off):
        return pl.BlockSpec((1, tc), lambda j: (0, j + off))

    return pl.pallas_call(
        body, grid=(nb,), in_specs=[_col_spec(S, tc), _col_spec(S, tc, nb), wspec(0), wspec(nb), bspec(0), bspec(nb)],
        out_specs=_col_spec(S, tc), out_shape=_sds((S, FFN), BF16), name="ffn_act", compiler_params=_PAR)(up, up, cw, cw, cb, cb)


def _ffn_act_bwd(up, dact, cw, cb, tc=256):
    S, nb = up.shape[0], FFN // tc
    R = min(_CONV_ROWS, S)

    def body(g_ref, v_ref, d_ref, wg_ref, wv_ref, bg_ref, bv_ref, dx_ref, dw_ref, db_ref):
        wg, wv, bg, bv = wg_ref[...], wv_ref[...], bg_ref[...], bv_ref[...]
        acc = [[jnp.zeros((1, tc), F32)] * 4, [jnp.zeros((1, tc), F32)] * 4]
        for r0 in range(0, S, R):
            eg, ev = _rows(g_ref, r0 - _HALO, r0 + R + _HALO), _rows(v_ref, r0 - _HALO, r0 + R + _HALO)
            da = _rows(d_ref, r0, r0 + R + _HALO)
            cg, cv = _conv_rows(eg, wg, bg, R + _HALO), _conv_rows(ev, wv, bv, R + _HALO)
            sg = _sigmoid(cg)
            for half, (ext, w, dc) in enumerate(((eg, wg, da * cv * (sg * (1.0 + cg * (1.0 - sg)))), (ev, wv, da * (cg * sg)))):
                dx, dws, db = _conv_rows_bwd(ext, w, dc, R)
                dx_ref[half, r0:r0 + R, :] = dx.astype(BF16)
                acc[half] = [s + d for s, d in zip(acc[half], dws + [db])]
        for half in range(2):
            for k in range(3):
                dw_ref[half, k:k + 1, :] = acc[half][k]
            db_ref[half] = acc[half][3]

    def wspec(off):
        return pl.BlockSpec((3, tc), lambda j: (0, j + off))

    def bspec(off):
        return pl.BlockSpec((1, tc), lambda j: (0, j + off))

    cs = _col_spec(S, tc)
    both = lambda r: pl.BlockSpec((2, r, tc), lambda j: (0, 0, j))
    return pl.pallas_call(
        body, grid=(nb,), in_specs=[cs, _col_spec(S, tc, nb), cs, wspec(0), wspec(nb), bspec(0), bspec(nb)],
        out_specs=(both(S), both(3), both(1)), out_shape=(_sds((2, S, FFN), BF16), _sds((2, 3, FFN)), _sds((2, 1, FFN))),
        name="ffn_act_bwd", compiler_params=_PAR)(up, up, dact, cw, cw, cb, cb)


def _window_sum(x, w, up=False):
    shift = _shift_up if up else _shift_down
    j = 1
    while j < w:
        x = x + shift(x, j)
        j *= 2
    return x


def _pool_fwd(proj, pool_w, pool_scale):
    S = proj.shape[0]

    def body(u_ref, w_ref, s_ref, o_ref):
        cnt_row = (_iota((S, 128), 0) + 1).astype(F32)
        for g, w in enumerate(POOL_WINDOWS):
            sl = slice(g * 128, (g + 1) * 128)
            u = u_ref[:, sl]
            pooled = _window_sum(u, w) / jnp.minimum(cnt_row, float(w)) - u
            mixed = _dot(pooled.astype(BF16), w_ref[g].astype(BF16), _NN)
            o_ref[:, sl] = (mixed * s_ref[:, sl]).astype(BF16)

    return pl.pallas_call(
        body, grid=(1,),
        in_specs=[pl.BlockSpec((S, 512), lambda i: (0, OP // 512)), pl.BlockSpec((4, 128, 128), lambda i: (0, 0, 0)), _vec_spec(512)],
        out_specs=pl.BlockSpec((S, 512), lambda i: (0, 0)), out_shape=_sds((S, 512), BF16), name="pool_fwd",
        compiler_params=_ARB)(proj, pool_w, pool_scale)


def _pool_bwd(proj, dob, pool_w, pool_scale, dproj, after):
    S = proj.shape[0]

    def body(u_ref, d_ref, w_ref, s_ref, dproj_in, after_ref, du_ref, dw_ref, ds_ref):
        cnt_row = (_iota((S, 128), 0) + 1).astype(F32)
        for g, w in enumerate(POOL_WINDOWS):
            sl = slice(g * 128, (g + 1) * 128)
            u, dv, wv = u_ref[:, sl], d_ref[:, sl], w_ref[g].astype(BF16)
            cnt = jnp.minimum(cnt_row, float(w))
            pooled = (_window_sum(u, w) / cnt - u).astype(BF16)
            ds_ref[:, sl] = jnp.sum(dv * _dot(pooled, wv, _NN), 0, keepdims=True)
            dmix = (dv * s_ref[:, sl]).astype(BF16)
            dw_ref[g] = _dot(pooled, dmix, _TN)
            dp = _dot(dmix, wv, _NT)
            du_ref[:, sl] = (_window_sum(dp / cnt, w, up=True) - dp).astype(BF16)

    blk = pl.BlockSpec((S, 512), lambda i: (0, 0))
    wspec = pl.BlockSpec((4, 128, 128), lambda i: (0, 0, 0))
    return pl.pallas_call(
        body, grid=(1,), in_specs=[pl.BlockSpec((S, 512), lambda i: (0, OP // 512)), blk, wspec, _vec_spec(512), _ANY, _ANY],
        out_specs=(pl.BlockSpec((S, 512), lambda i: (0, OP // 512)), wspec, _vec_spec(512)),
        out_shape=(_sds(dproj.shape, BF16), _sds((4, 128, 128)), _sds((1, 512))), input_output_aliases={4: 0},
        name="pool_bwd", compiler_params=_ARB)(proj, dob, pool_w, pool_scale, dproj, after)


def _branch_specs():
    return [pl.BlockSpec((512, D), lambda i: (0, 0)), pl.BlockSpec((512, D), lambda i: (1, 0)), pl.BlockSpec((1024, D), lambda i: (1, 0))]


def _merge_fwd(oa, ob, oc, proj, wbr, tm=256):
    S = oa.shape[0]

    def body(oa_ref, ob_ref, oc_ref, gl_ref, wa_ref, wb_ref, wc_ref, o_ref):
        acc = _sigmoid(gl_ref[:, 0:D]) * _dot(oa_ref[...], wa_ref[...], _NN)
        acc += _sigmoid(gl_ref[:, D:2 * D]) * _dot(ob_ref[...], wb_ref[...], _NN)
        acc += _sigmoid(gl_ref[:, 2 * D:3 * D]) * _dot(oc_ref[...], wc_ref[...], _NN)
        o_ref[...] = acc.astype(BF16)

    return pl.pallas_call(
        body, grid=(S // tm,),
        in_specs=[_row_spec(tm, 512), _row_spec(tm, 512), _row_spec(tm, D), _row_spec(tm, 3 * D)] + _branch_specs(),
        out_specs=_row_spec(tm, D), out_shape=_sds((S, D), BF16), name="merge_fwd", compiler_params=_PAR)(oa, ob, oc, proj, wbr, wbr, wbr)


def _merge_bwd(dm, oa, ob, oc, proj, wbr, tm=256):
    S = oa.shape[0]

    def body(dm_ref, oa_ref, ob_ref, oc_ref, gl_ref, wa_ref, wb_ref, wc_ref, dya_ref, dyb_ref, dyc_ref, dgl_ref, doa_ref, dob_ref, doc_ref):
        dmv = dm_ref[...]
        for i, (o_ref, w_ref, dy_ref, do_ref) in enumerate(
                ((oa_ref, wa_ref, dya_ref, doa_ref), (ob_ref, wb_ref, dyb_ref, dob_ref), (oc_ref, wc_ref, dyc_ref, doc_ref))):
            gt = _sigmoid(gl_ref[:, i * D:(i + 1) * D])
            wv = w_ref[...]
            yv = _dot(o_ref[...], wv, _NN)
            dy = (dmv * gt).astype(BF16)
            dy_ref[...] = dy
            dgl_ref[:, i * D:(i + 1) * D] = (dmv * yv * gt * (1.0 - gt)).astype(BF16)
            do_ref[...] = _dot(dy, wv, _NT)

    rs = _row_spec
    return pl.pallas_call(
        body, grid=(S // tm,),
        in_specs=[rs(tm, D), rs(tm, 512), rs(tm, 512), rs(tm, D), rs(tm, 3 * D)] + _branch_specs(),
        out_specs=(rs(tm, D), rs(tm, D), rs(tm, D), rs(tm, 3 * D), rs(tm, 512), rs(tm, 512), rs(tm, D)),
        out_shape=(_sds((S, D), BF16), _sds((S, D), BF16), _sds((S, D), BF16), _sds((S, IN_PAD), BF16), _sds((S, 512)), _sds((S, 512)), _sds((S, D))),
        name="merge_bwd", compiler_params=_PAR)(dm, oa, ob, oc, proj, wbr, wbr, wbr)


def _rope_tab(posb, invf, tm=256):
    S = posb.shape[0]

    def body(p_ref, f_ref, c_ref, s1_ref, s2_ref):
        ang = p_ref[...] * f_ref[...]
        lane = _iota(ang.shape, 1)
        cs, sn = jnp.cos(ang), jnp.sin(ang)
        c_ref[...] = jnp.where(lane < NOPE, 1.0, cs)
        s1_ref[...] = jnp.where((lane >= 64) & (lane < 80), -sn, 0.0)
        s2_ref[...] = jnp.where((lane >= 80) & (lane < 96), sn, 0.0)

    rs = _row_spec(tm, 128)
    return pl.pallas_call(body, grid=(S // tm,), in_specs=[rs, _vec_spec(128)], out_specs=(rs, rs, rs),
                          out_shape=(_sds((S, 128)),) * 3, name="rope_tab", compiler_params=_PAR)(posb, invf)


def _rope(u, C, S1, S2):
    return u * C + pltpu.roll(u, 112, 1) * S1 + pltpu.roll(u, 16, 1) * S2


def _rope_t(dy, C, S1, S2):
    return dy * C + pltpu.roll(dy * S1, 16, 1) + pltpu.roll(dy * S2, 112, 1)


def _seg_sum(v, mask):
    return jnp.sum(jnp.where(mask, v, 0.0), -1, keepdims=True)


def _mla_latents(pq_ref, pkv_ref, wqb_ref, wkvb_ref, qan_ref, kvan_ref):
    ql, kvl = pq_ref[...], pkv_ref[...]
    ckv, kr = kvl[:, 0:KV_RANK], kvl[:, KV_RANK:KV_RANK + 128]
    rq = lax.rsqrt(jnp.mean(ql * ql, -1, keepdims=True) + EPS)
    rkv = lax.rsqrt(jnp.mean(ckv * ckv, -1, keepdims=True) + EPS)
    nq = (ql * rq * qan_ref[...]).astype(BF16)
    nkv = (ckv * rkv * kvan_ref[...]).astype(BF16)
    kv = jnp.concatenate([_dot(nkv, wkvb_ref[k], _NN) for k in range(4)], 1)
    return ql, ckv, kr, rq, rkv, nq, nkv, _dot(nq, wqb_ref[...], _NN), kv


def _mla_specs(tm):
    full = lambda r, n: pl.BlockSpec((r, n), lambda i: (0, 0))
    return ([pl.BlockSpec((tm, 384), lambda i: (i, OQ // 384)), pl.BlockSpec((tm, 384), lambda i: (i, OKV // 384))],
            [full(Q_RANK, D), pl.BlockSpec((4, KV_RANK, 256), lambda i: (0, 0, 0)), _vec_spec(Q_RANK), _vec_spec(KV_RANK), _vec_spec(128), _vec_spec(128)]
            + [_row_spec(tm, 128)] * 3)


def _mla_prep(proj, wqb, wkvb, qan, kvan, wq, wk, rope, tm=256):
    S = proj.shape[0]

    def body(pq_ref, pkv_ref, wqb_ref, wkvb_ref, qan_ref, kvan_ref, wq_ref, wk_ref, c_ref, s1_ref, s2_ref, qf_ref, kf_ref, ve_ref):
        _, _, kr, _, _, _, _, q, kv = _mla_latents(pq_ref, pkv_ref, wqb_ref, wkvb_ref, qan_ref, kvan_ref)
        C, S1, S2, wqv, wkv = c_ref[...], s1_ref[...], s2_ref[...], wq_ref[...], wk_ref[...]
        lane = _iota((tm, 128), 1)
        mn, mr = lane < 64, (lane >= 64) & (lane < 96)
        rrk = lax.rsqrt(_seg_sum(kr * kr, mr) / ROPE_DIM + EPS)
        ykr = _rope(jnp.where(mr, kr * rrk * wkv, 0.0), C, S1, S2)
        for h in range(N_HEADS):
            sl = slice(h * 128, (h + 1) * 128)
            t = q[:, sl]
            rn = lax.rsqrt(_seg_sum(t * t, mn) / NOPE + EPS)
            rr = lax.rsqrt(_seg_sum(t * t, mr) / ROPE_DIM + EPS)
            qf_ref[:, sl] = _rope(t * jnp.where(mn, rn, jnp.where(mr, rr, 0.0)) * wqv, C, S1, S2).astype(BF16)
            t = kv[:, sl]
            rn = lax.rsqrt(_seg_sum(t * t, mn) / NOPE + EPS)
            kf_ref[:, sl] = (jnp.where(mn, t * rn * wkv, 0.0) + ykr).astype(BF16)
            ve_ref[:, sl] = (jnp.where(mn, pltpu.roll(t, 64, 1), 0.0) if h % 2 == 0 else jnp.where(mn, 0.0, t)).astype(BF16)

    pspecs, wspecs = _mla_specs(tm)
    rs = _row_spec(tm, D)
    return pl.pallas_call(body, grid=(S // tm,), in_specs=pspecs + wspecs, out_specs=(rs, rs, rs),
                          out_shape=(_sds((S, D), BF16),) * 3, name="mla_prep", compiler_params=_PAR)(
        proj, proj, wqb, wkvb, qan, kvan, wq, wk, *rope)


def _mla_prep_bwd(proj, dqf, dkf, dve, wqb, wkvb, qan, kvan, wq, wk, rope, dproj, tm=256):
    S = proj.shape[0]

    def body(pq_ref, pkv_ref, wqb_ref, wkvb_ref, qan_ref, kvan_ref, wq_ref, wk_ref, c_ref, s1_ref, s2_ref,
             dqf_ref, dkf_ref, dve_ref, dproj_in, dlat_ref, dwqb_ref, dwkvb_ref, st_ref, dq_scr, dkv_scr):
        dqs_ref, dkvs_ref = dlat_ref.at[:, 0:384], dlat_ref.at[:, 384:768]
        @pl.when(pl.program_id(0) == 0)
        def _():
            dwqb_ref[...] = jnp.zeros_like(dwqb_ref)
            dwkvb_ref[...] = jnp.zeros_like(dwkvb_ref)
            st_ref[...] = jnp.zeros_like(st_ref)

        ql, ckv, kr, rq, rkv, nq, nkv, q, kv = _mla_latents(pq_ref, pkv_ref, wqb_ref, wkvb_ref, qan_ref, kvan_ref)
        C, S1, S2, wqv, wkv = c_ref[...], s1_ref[...], s2_ref[...], wq_ref[...], wk_ref[...]
        lane = _iota((tm, 128), 1)
        mn, mr = lane < 64, (lane >= 64) & (lane < 96)
        dwq = jnp.zeros((1, 128), F32)
        dwk = jnp.zeros((1, 128), F32)
        dykr = jnp.zeros((tm, 128), F32)
        for h in range(N_HEADS):
            sl = slice(h * 128, (h + 1) * 128)
            t = q[:, sl]
            rn = lax.rsqrt(_seg_sum(t * t, mn) / NOPE + EPS)
            rr = lax.rsqrt(_seg_sum(t * t, mr) / ROPE_DIM + EPS)
            scale = jnp.where(mn, rn, jnp.where(mr, rr, 0.0))
            that = t * scale
            du = _rope_t(dqf_ref[:, sl], C, S1, S2)
            dwq += jnp.sum(du * that, 0, keepdims=True)
            g = du * wqv
            gt = g * that
            dq_scr[:, sl] = scale * (g - that * jnp.where(mn, _seg_sum(gt, mn) / NOPE, _seg_sum(gt, mr) / ROPE_DIM))
            t = kv[:, sl]
            rn = lax.rsqrt(_seg_sum(t * t, mn) / NOPE + EPS)
            that = jnp.where(mn, t * rn, 0.0)
            dkf = dkf_ref[:, sl]
            dykr += jnp.where(mr, dkf, 0.0)
            dkn = jnp.where(mn, dkf, 0.0)
            dwk += jnp.sum(dkn * that, 0, keepdims=True)
            g = dkn * wkv
            dve = dve_ref[:, sl]
            dkv_scr[:, sl] = jnp.where(mn, rn * (g - that * (jnp.sum(g * that, -1, keepdims=True) / NOPE)),
                                       pltpu.roll(dve, 64, 1) if h % 2 == 0 else dve)
        rrk = lax.rsqrt(_seg_sum(kr * kr, mr) / ROPE_DIM + EPS)
        that = jnp.where(mr, kr * rrk, 0.0)
        dukr = jnp.where(mr, _rope_t(dykr, C, S1, S2), 0.0)
        dwk += jnp.sum(dukr * that, 0, keepdims=True)
        g = dukr * wkv
        dkr = rrk * (g - that * (jnp.sum(g * that, -1, keepdims=True) / ROPE_DIM))
        dqv, dkvv = dq_scr[...].astype(BF16), dkv_scr[...].astype(BF16)
        dnq = _dot(dqv, wqb_ref[...], _NT)
        dwqb_ref[...] += _dot(nq, dqv, _TN)
        dnkv = jnp.zeros((tm, KV_RANK), F32)
        for k in range(4):
            dnkv += _dot(dkvv[:, k * 256:(k + 1) * 256], wkvb_ref[k], _NT)
            dwkvb_ref[k] += _dot(nkv, dkvv[:, k * 256:(k + 1) * 256], _TN)
        xhat = ql * rq
        st_ref[0:1, 0:Q_RANK] += jnp.sum(dnq * xhat, 0, keepdims=True)
        g = dnq * qan_ref[...]
        dqs_ref[...] = (rq * (g - xhat * jnp.mean(g * xhat, -1, keepdims=True))).astype(BF16)
        xhat = ckv * rkv
        st_ref[1:2, 0:KV_RANK] += jnp.sum(dnkv * xhat, 0, keepdims=True)
        g = dnkv * kvan_ref[...]
        dkvs_ref[:, 0:KV_RANK] = (rkv * (g - xhat * jnp.mean(g * xhat, -1, keepdims=True))).astype(BF16)
        dkvs_ref[:, KV_RANK:KV_RANK + 128] = dkr.astype(BF16)
        st_ref[2:3, 0:128] += dwq
        st_ref[3:4, 0:128] += dwk

    pspecs, wspecs = _mla_specs(tm)
    rs = _row_spec(tm, D)
    full = lambda r, n: pl.BlockSpec((r, n), lambda i: (0, 0))
    return pl.pallas_call(
        body, grid=(S // tm,), in_specs=pspecs + wspecs + [rs, rs, rs, _ANY],
        out_specs=(pl.BlockSpec((tm, 768), lambda i: (i, OQ // 768)), full(Q_RANK, D), pl.BlockSpec((4, KV_RANK, 256), lambda i: (0, 0, 0)), full(8, D)),
        out_shape=(_sds(dproj.shape, BF16), _sds((Q_RANK, D)), _sds((4, KV_RANK, 256)), _sds((8, D))), input_output_aliases={14: 0},
        scratch_shapes=[pltpu.VMEM((tm, D), F32), pltpu.VMEM((tm, D), F32)], name="mla_prep_bwd", compiler_params=_ARB)(
        proj, proj, wqb, wkvb, qan, kvan, wq, wk, *rope, dqf, dkf, dve, dproj)


_ATT_SCALE = (NOPE + ROPE_DIM) ** -0.5


def _att_probs(q, k, i, tq):
    n = k.shape[0]
    s = _dot(q, k, _NT) * _ATT_SCALE
    tri = _iota((tq, tq), 1) <= _iota((tq, tq), 0)
    diag = jnp.where(tri, s[:, n - tq:], -1e30)
    s = diag if n == tq else jnp.concatenate([s[:, :n - tq], diag], 1)
    p = jnp.exp(s - jnp.max(s, -1, keepdims=True))
    return p * (1.0 / jnp.sum(p, -1, keepdims=True))


def _attn_fwd(qf, kf, ve, tq=512):
    S = qf.shape[0]
    tq = min(tq, S)

    def body(q_ref, k_ref, v_ref, o_ref):
        for i in range(S // tq):
            n, rows = (i + 1) * tq, slice(i * tq, (i + 1) * tq)
            acc = jnp.zeros((tq, 128), F32)
            for hh in range(2):
                sl = slice(hh * 128, (hh + 1) * 128)
                p = _att_probs(q_ref[rows, sl], k_ref[0:n, sl], i, tq)
                acc += _dot(p.astype(BF16), v_ref[0:n, sl], _NN)
            o_ref[rows, :] = acc.astype(BF16)

    ps = pl.BlockSpec((S, 256), lambda h: (0, h))
    return pl.pallas_call(body, grid=(N_HEADS // 2,), in_specs=[ps, ps, ps], out_specs=pl.BlockSpec((S, 128), lambda h: (0, h)),
                          out_shape=_sds((S, 512), BF16), name="attn_fwd", compiler_params=_PAR)(qf, kf, ve)


def _attn_bwd(qf, kf, ve, do, after, tq=512):
    S = qf.shape[0]
    tq = min(tq, S)

    def body(q_ref, k_ref, v_ref, do_ref, after_ref, dq_ref, dk_ref, dv_ref):
        dk_ref[...] = jnp.zeros_like(dk_ref)
        dv_ref[...] = jnp.zeros_like(dv_ref)
        for i in range(S // tq):
            n, rows = (i + 1) * tq, slice(i * tq, (i + 1) * tq)
            dob = do_ref[rows, :].astype(BF16)
            for hh in range(2):
                sl = slice(hh * 128, (hh + 1) * 128)
                q, k = q_ref[rows, sl], k_ref[0:n, sl]
                p = _att_probs(q, k, i, tq)
                dv_ref[0:n, sl] += _dot(p.astype(BF16), dob, _TN)
                dp = _dot(dob, v_ref[0:n, sl], _NT)
                ds = (p * (dp - jnp.sum(dp * p, -1, keepdims=True)) * _ATT_SCALE).astype(BF16)
                dq_ref[rows, sl] = _dot(ds, k, _NN)
                dk_ref[0:n, sl] += _dot(ds, q, _TN)

    ps = pl.BlockSpec((S, 256), lambda h: (0, h))
    return pl.pallas_call(body, grid=(N_HEADS // 2,), in_specs=[ps, ps, ps, pl.BlockSpec((S, 128), lambda h: (0, h)), _ANY],
                          out_specs=(ps, ps, ps), out_shape=(_sds((S, D)),) * 3, name="attn_bwd", compiler_params=_PAR)(qf, kf, ve, do, after)


def _softplus(x):
    return jnp.maximum(x, 0.0) + jnp.log1p(jnp.exp(-jnp.abs(x)))


def _ssd_chunk(xc_ref, dtr_ref, dtb_ref, al_ref, e_ref):
    L = SSD_L
    a = -jnp.exp(al_ref[...])
    dtp = _softplus(dtr_ref[...] + dtb_ref[...])
    causal = _iota((L, L), 1) <= _iota((L, L), 0)
    cs = _dot_hi(causal.astype(F32), dtp * a)
    E = e_ref[...]
    dtx, csx = _dot_hi(dtp, E), _dot_hi(cs, E)
    X = xc_ref[:, 0:SSD_INNER]
    Xd = X * dtx
    dec_out = jnp.exp(csx)
    dec_st = jnp.exp(csx[L - 1:L, :] - csx)
    return a, dtp, causal, cs, cs.T, dtx, X, Xd, dec_out, dec_st


def _ssd_decay(causal, cs, cs_row, h):
    diff = cs[:, h:h + 1] - cs_row[h:h + 1, :]
    return jnp.where(causal, jnp.exp(jnp.where(causal, diff, 0.0)), 0.0)


def _ssd_groups(xc_ref, g):
    b0, c0 = SSD_INNER + g * SSD_N, SSD_INNER + 2 * SSD_N + g * SSD_N
    return xc_ref[:, b0:b0 + SSD_N].astype(BF16), xc_ref[:, c0:c0 + SSD_N].astype(BF16)


def _pair_decay(cs, pair):
    L = SSD_L
    return jnp.where(_iota((128, 128), 0) < 64, jnp.exp(cs[L - 1:L, 2 * pair:2 * pair + 1]), jnp.exp(cs[L - 1:L, 2 * pair + 1:2 * pair + 2]))


def _ssd_in_specs(nc, rev):
    idx = (lambda c: nc - 1 - c) if rev else (lambda c: c)
    return [pl.BlockSpec((SSD_L, SSD_CONV_DIM), lambda c: (idx(c), 0)), pl.BlockSpec((SSD_L, 128), lambda c: (idx(c), ODT // 128)),
            _vec_spec(128), _vec_spec(128), _vec_spec(SSD_INNER), pl.BlockSpec((128, SSD_INNER), lambda c: (0, 0))]


def _ssd_core(xc, proj, dtb, alog, dskip, E):
    S = xc.shape[0]
    nc = S // SSD_L

    def body(xc_ref, dtr_ref, dtb_ref, al_ref, dx_ref, e_ref, y_ref, hp_ref, h_scr):
        @pl.when(pl.program_id(0) == 0)
        def _():
            h_scr[...] = jnp.zeros_like(h_scr)

        hp_ref[0] = h_scr[...]
        _, _, causal, cs, cs_row, _, X, Xd, dec_out, dec_st = _ssd_chunk(xc_ref, dtr_ref, dtb_ref, al_ref, e_ref)
        Xs = Xd * dec_st
        lane = _iota((SSD_L, 128), 1)
        for g in range(2):
            Bg, Cg = _ssd_groups(xc_ref, g)
            CB = _dot(Cg, Bg, _NT)
            for pr in range(4):
                pair = g * 4 + pr
                psl = slice(pair * 128, (pair + 1) * 128)
                Xdp = Xd[:, psl].astype(BF16)
                r0 = _dot((CB * _ssd_decay(causal, cs, cs_row, 2 * pair)).astype(BF16), Xdp, _NN)
                r1 = _dot((CB * _ssd_decay(causal, cs, cs_row, 2 * pair + 1)).astype(BF16), Xdp, _NN)
                Hp = h_scr[psl, :]
                W = _dot(Cg, Hp.astype(BF16), _NT)
                y_ref[:, psl] = jnp.where(lane < 64, r0, r1) + W * dec_out[:, psl] + X[:, psl] * dx_ref[:, psl]
                h_scr[psl, :] = Hp * _pair_decay(cs, pair) + _dot(Xs[:, psl].astype(BF16), Bg, _TN)

    return pl.pallas_call(
        body, grid=(nc,), in_specs=_ssd_in_specs(nc, False),
        out_specs=(pl.BlockSpec((SSD_L, SSD_INNER), lambda c: (c, 0)), pl.BlockSpec((1, SSD_INNER, SSD_N), lambda c: (c, 0, 0))),
        out_shape=(_sds((S, SSD_INNER)), _sds((nc, SSD_INNER, SSD_N))), scratch_shapes=[pltpu.VMEM((SSD_INNER, SSD_N), F32)],
        name="ssd_core", compiler_params=_ARB)(xc, proj, dtb, alog, dskip, E)


def _ssd_core_bwd(xc, proj, hprev, dy, dtb, alog, dskip, E, ET, dproj):
    S = xc.shape[0]
    nc = S // SSD_L
    L = SSD_L

    def body(xc_ref, dtr_ref, dtb_ref, al_ref, dx_ref, e_ref, et_ref, hp_ref, dy_ref, dproj_in, dxc_ref, ddt_ref, st_ref, dh_scr, acc_scr):
        step = pl.program_id(0)

        @pl.when(step == 0)
        def _():
            dh_scr[...] = jnp.zeros_like(dh_scr)
            acc_scr[...] = jnp.zeros_like(acc_scr)
            st_ref[...] = jnp.zeros_like(st_ref)

        a, dtp, causal, cs, cs_row, dtx, X, Xd, dec_out, dec_st = _ssd_chunk(xc_ref, dtr_ref, dtb_ref, al_ref, e_ref)
        lane = _iota((L, 128), 1)
        sub = _iota((128, L), 0)
        dcs_col = jnp.zeros((L, 128), F32)
        dcs_row = jnp.zeros((128, L), F32)
        dcs_last = jnp.zeros((1, 128), F32)
        dcsx, ddtx, dlastx = [], [], []
        for g in range(2):
            Bg, Cg = _ssd_groups(xc_ref, g)
            CB = _dot(Cg, Bg, _NT)
            dCB = jnp.zeros((L, L), F32)
            dB = jnp.zeros((L, SSD_N), F32)
            dC = jnp.zeros((L, SSD_N), F32)
            for pr in range(4):
                pair = g * 4 + pr
                psl = slice(pair * 128, (pair + 1) * 128)
                dY, Xp, Xdp, dop, dsp = dy_ref[:, psl], X[:, psl], Xd[:, psl], dec_out[:, psl], dec_st[:, psl]
                Xdb = Xdp.astype(BF16)
                acc_scr[0:1, psl] += jnp.sum(dY * Xp, 0, keepdims=True)
                Hp = hp_ref[0, psl, :]
                Hb = Hp.astype(BF16)
                dW = (dY * dop).astype(BF16)
                dcx = dY * _dot(Cg, Hb, _NT) * dop
                dC += _dot(dW, Hb, _NN)
                dHp = _dot(dW, Cg, _TN)
                dHn = dh_scr[psl, :]
                cd = _pair_decay(cs, pair)
                dh_scr[psl, :] = dHp + dHn * cd
                rsum = jnp.sum(dHn * Hp * cd, -1, keepdims=True)
                half = _iota((128, 1), 0) < 64
                s0 = jnp.sum(jnp.where(half, rsum, 0.0), 0, keepdims=True)
                s1 = jnp.sum(jnp.where(half, 0.0, rsum), 0, keepdims=True)
                lane1 = _iota((1, 128), 1)
                dcs_last += jnp.where(lane1 == 2 * pair, s0, 0.0) + jnp.where(lane1 == 2 * pair + 1, s1, 0.0)
                dHb = dHn.astype(BF16)
                dXs = _dot(Bg, dHb, _NT)
                dB += _dot((Xdp * dsp).astype(BF16), dHb, _NN)
                dXd = dXs * dsp
                e_st = dXs * Xdp * dsp
                dcx -= e_st
                dlastx.append(jnp.sum(e_st, 0, keepdims=True))
                for i in range(2):
                    h = 2 * pair + i
                    Dm = _ssd_decay(causal, cs, cs_row, h)
                    M = CB * Dm
                    dYm = jnp.where((lane < 64) if i == 0 else (lane >= 64), dY, 0.0).astype(BF16)
                    dM = _dot(dYm, Xdb, _NT)
                    dXd += _dot(M.astype(BF16), dYm, _TN)
                    dCB += dM * Dm
                    Em = dM * M
                    dcs_col += jnp.where(lane == h, jnp.sum(Em, -1, keepdims=True), 0.0)
                    dcs_row += jnp.where(sub == h, jnp.sum(Em, 0, keepdims=True), 0.0)
                dxc_ref[:, psl] = dY * dx_ref[:, psl] + dXd * dtx[:, psl]
                ddtx.append(dXd * Xp)
                dcsx.append(dcx)
            dCBb = dCB.astype(BF16)
            b0, c0 = SSD_INNER + g * SSD_N, SSD_INNER + 2 * SSD_N + g * SSD_N
            dxc_ref[:, b0:b0 + SSD_N] = dB + _dot(dCBb, Cg, _TN)
            dxc_ref[:, c0:c0 + SSD_N] = dC + _dot(dCBb, Bg, _NN)
        ET = et_ref[...]
        dcs = dcs_col - dcs_row.T + _dot_hi(jnp.concatenate(dcsx, 1), ET)
        dlast = dcs_last + _dot_hi(jnp.broadcast_to(jnp.concatenate(dlastx, 1), (8, SSD_INNER)), ET)[0:1, :]
        dcs += jnp.where(_iota((L, 128), 0) == L - 1, dlast, 0.0)
        dda = _dot_hi((_iota((L, L), 1) >= _iota((L, L), 0)).astype(F32), dcs)
        ddtp = dda * a + _dot_hi(jnp.concatenate(ddtx, 1), ET)
        draw = ddtp * _sigmoid(dtr_ref[...] + dtb_ref[...])
        ddt_ref[...] = draw.astype(BF16)
        st_ref[0:1, :] += jnp.sum(draw, 0, keepdims=True)
        st_ref[1:2, :] += jnp.sum(dda * dtp, 0, keepdims=True) * a

        @pl.when(step == nc - 1)
        def _():
            st_ref[2:3, :] = _dot_hi(acc_scr[...], ET)[0:1, :]

    rev = lambda c: (nc - 1 - c, 0)
    return pl.pallas_call(
        body, grid=(nc,),
        in_specs=_ssd_in_specs(nc, True) + [pl.BlockSpec((SSD_INNER, 128), lambda c: (0, 0)),
                                            pl.BlockSpec((1, SSD_INNER, SSD_N), lambda c: (nc - 1 - c, 0, 0)),
                                            pl.BlockSpec((L, SSD_INNER), rev), _ANY],
        out_specs=(pl.BlockSpec((L, SSD_CONV_DIM), rev), pl.BlockSpec((L, 128), lambda c: (nc - 1 - c, ODT // 128)),
                   pl.BlockSpec((8, 128), lambda c: (0, 0))),
        out_shape=(_sds((S, SSD_CONV_DIM)), _sds(dproj.shape, BF16), _sds((8, 128))), input_output_aliases={9: 1},
        scratch_shapes=[pltpu.VMEM((SSD_INNER, SSD_N), F32), pltpu.VMEM((8, SSD_INNER), F32)],
        name="ssd_core_bwd", compiler_params=_ARB)(xc, proj, dtb, alog, dskip, E, ET, hprev, dy, dproj)


def _ssd_post(y, proj, nw, tm=256):
    S = y.shape[0]

    def body(y_ref, z_ref, nw_ref, o_ref):
        for g in range(2):
            sl = slice(g * 512, (g + 1) * 512)
            gated = y_ref[:, sl] * _silu(z_ref[:, sl])
            r = lax.rsqrt(jnp.mean(gated * gated, -1, keepdims=True) + EPS)
            o_ref[:, sl] = (gated * r * nw_ref[:, sl]).astype(BF16)

    return pl.pallas_call(
        body, grid=(S // tm,), in_specs=[_row_spec(tm, D), pl.BlockSpec((tm, D), lambda i: (i, OZ // D)), _vec_spec(D)],
        out_specs=_row_spec(tm, D), out_shape=_sds((S, D), BF16), name="ssd_post", compiler_params=_PAR)(y, proj, nw)


def _ssd_post_bwd(doc, y, proj, nw, dproj, tm=256):
    S = y.shape[0]

    def body(d_ref, y_ref, z_ref, nw_ref, dproj_in, dy_ref, dz_ref, st_ref):
        @pl.when(pl.program_id(0) == 0)
        def _():
            st_ref[...] = jnp.zeros_like(st_ref)

        for g in range(2):
            sl = slice(g * 512, (g + 1) * 512)
            yv, zv, dv = y_ref[:, sl], z_ref[:, sl], d_ref[:, sl]
            sz = _silu(zv)
            gated = yv * sz
            r = lax.rsqrt(jnp.mean(gated * gated, -1, keepdims=True) + EPS)
            ghat = gated * r
            st_ref[0:1, sl] += jnp.sum(dv * ghat, 0, keepdims=True)
            gg = dv * nw_ref[:, sl]
            dg = r * (gg - ghat * jnp.mean(gg * ghat, -1, keepdims=True))
            dy_ref[:, sl] = dg * sz
            dz_ref[:, sl] = (dg * yv * _dsilu(zv)).astype(BF16)

    zs = pl.BlockSpec((tm, D), lambda i: (i, OZ // D))
    return pl.pallas_call(
        body, grid=(S // tm,), in_specs=[_row_spec(tm, D), _row_spec(tm, D), zs, _vec_spec(D), _ANY],
        out_specs=(_row_spec(tm, D), zs, _vec_spec(D, 8)), out_shape=(_sds((S, D)), _sds(dproj.shape, BF16), _sds((8, D))),
        input_output_aliases={4: 1}, name="ssd_post_bwd", compiler_params=_ARB)(doc, y, proj, nw, dproj)


def _row(v, n=None):
    v = v.astype(F32).reshape(1, -1)
    return v if n is None else jnp.pad(v, ((0, 0), (0, n - v.shape[1])))


_IN_SEGMENTS = [(0, 384, OQ), (384, 640, OKV), (640, 672, OKV + 320), (672, 1184, OP), (1184, 2208, OZ), (2208, 3744, OX),
                (3744, 3760, ODT), (3760, IN_DIM, OG)]
_IN_ZEROS = [(OKV + 256, OKV + 320), (OKV + 352, OKV + 384), (ODT + 16, ODT + 128)]


def _in_pieces():
    w, out = IN_DIM // 4, []
    for a, b, d in _IN_SEGMENTS:
        while a < b:
            k = a // w
            e = min(b, (k + 1) * w)
            out.append((k, a - k * w, e - k * w, d))
            d, a = d + e - a, e
    return out


def _win_layout(w_in4, tm=256):
    def body(w_ref, o_ref):
        for k, s0, s1, d in _in_pieces():
            o_ref[:, d:d + s1 - s0] = w_ref[k, :, s0:s1]
        for z0, z1 in _IN_ZEROS:
            o_ref[:, z0:z1] = jnp.zeros((tm, z1 - z0), o_ref.dtype)

    return pl.pallas_call(
        body, grid=(D // tm,), in_specs=[pl.BlockSpec((4, tm, IN_DIM // 4), lambda i: (0, i, 0))],
        out_specs=pl.BlockSpec((tm, IN_PAD), lambda i: (i, 0)), out_shape=_sds((D, IN_PAD), w_in4.dtype), name="win_layout",
        compiler_params=_PAR)(w_in4)


def _win_unlayout(dwin, tm=256):
    def body(d_ref, o_ref):
        for k, s0, s1, d in _in_pieces():
            o_ref[k, :, s0:s1] = d_ref[:, d:d + s1 - s0]

    return pl.pallas_call(
        body, grid=(D // tm,), in_specs=[pl.BlockSpec((tm, IN_PAD), lambda i: (i, 0))],
        out_specs=pl.BlockSpec((4, tm, IN_DIM // 4), lambda i: (0, i, 0)), out_shape=_sds((4, D, IN_DIM // 4), dwin.dtype),
        name="win_unlayout", compiler_params=_PAR)(dwin)


def _prep_late(p):
    return dict(wbr=p["w_branch"].astype(BF16), wo=p["w_out"].astype(BF16), wup4=p["ffn_up4"].astype(BF16), wdn=p["ffn_down"].astype(BF16))


def _prep_layer(p, late=None):
    win = _win_layout(p["w_in4"].astype(BF16))
    wqb = jnp.transpose(p["w_q_b4"].astype(BF16).reshape(4, Q_RANK, 2, 96), (1, 0, 2, 3))
    return dict(
        win=win, wqb=jnp.pad(wqb, ((0, 0), (0, 0), (0, 0), (0, 32))).reshape(Q_RANK, D), wkvb4=p["w_kv_b4"].astype(BF16),
        late=late if late is not None else (lambda after: _prep_late(p)),
        nw1=_row(p["norm1_w"]), nw2=_row(p["norm2_w"]), qan=_row(p["q_a_norm"]), kvan=_row(p["kv_a_norm"]),
        wq=_row(p["q_norm"], 128), wk=_row(p["k_norm"], 128), pool_w=p["pool_w"].astype(F32), pool_scale=_row(p["pool_scale"]),
        cw=p["ssd_conv_w"].astype(F32), cb=_row(p["ssd_conv_b"]), dtb=_row(p["ssd_dt_bias"], 128), alog=_row(p["ssd_a_log"], 128),
        dskip=_row(jnp.repeat(p["ssd_d"].astype(F32), SSD_P)), snw=_row(p["ssd_norm_w"]),
        fcw=p["ffn_conv_w"].astype(F32), fcb=_row(p["ffn_conv_b"]))


def _layer_fwd(x, mod8, W, rope, E, tag):
    sh1, sc1, g1, sh2, sc2, g2 = (mod8[i:i + 1] for i in range(6))
    h1 = _ln_mod(x, W["nw1"], sc1, sh1, name=f"ln1_{tag}")
    proj = _mm(h1, W["win"], tn=640, tk=1024, name=f"proj_{tag}")
    qf, kf, ve = _mla_prep(proj, W["wqb"], W["wkvb4"], W["qan"], W["kvan"], W["wq"], W["wk"], rope)
    oa = _attn_fwd(qf, kf, ve)
    ob = _pool_fwd(proj, W["pool_w"], W["pool_scale"])
    xc = _ssd_pre(proj, W["cw"], W["cb"])
    y, hprev = _ssd_core(xc, proj, W["dtb"], W["alog"], W["dskip"], E)
    oc = _ssd_post(y, proj, W["snw"])
    W.update(W["late"](oc))
    merged = _merge_fwd(oa, ob, oc, proj, W["wbr"])
    x1, out1 = _mm(merged, W["wo"], tk=1024, res=x, gate=g1, name=f"wout_{tag}")
    h2 = _ln_mod(x1, W["nw2"], sc2, sh2, name=f"ln2_{tag}")
    up = _up_fwd(h2, W["wup4"], name=f"up_{tag}")
    act = _ffn_act(up, W["fcw"], W["fcb"])
    x2, out2 = _mm(act, W["wdn"], tm=1024, res=x1, gate=g2, name=f"down_{tag}")
    saved = dict(x=x, h1=h1, proj=proj, qf=qf, kf=kf, ve=ve, oa=oa, ob=ob, oc=oc, xc=xc, hprev=hprev, y=y, merged=merged,
                 out1=out1, x1=x1, h2=h2, up=up, act=act, out2=out2)
    return x2, saved


def _layer_bwd(dx2, sv, mod8, W, rope, E, ET, tag, emit=None, mid=None):
    sc1, g1, sc2, g2 = mod8[1:2], mod8[2:3], mod8[4:5], mod8[5:6]
    proj = sv["proj"]
    dz2, dg2 = _gate_bwd(dx2, sv["out2"], g2, name=f"gate2_bwd_{tag}")
    dact = _mm(dz2, W["wdn"], "nt", tn=1408, tk=1024, name=f"down_dx_{tag}")
    dwdn = _mm(sv["act"], dz2, "tn", tm=1408, name=f"down_dw_{tag}")
    dup2, dfcw, dfcb = _ffn_act_bwd(sv["up"], dact, W["fcw"], W["fcb"])
    dh2 = _up_dx(dup2, W["wup4"], name=f"up_dx_{tag}")
    dwup4 = _up_dw(sv["h2"], dup2, name=f"up_dw_{tag}")
    dx1, st2 = _ln_mod_bwd(sv["x1"], dh2, dx2, W["nw2"], sc2, name=f"ln2_bwd_{tag}")
    dz1, dg1 = _gate_bwd(dx1, sv["out1"], g1, name=f"gate1_bwd_{tag}")
    dmerged = _mm(dz1, W["wo"], "nt", tk=1024, name=f"wout_dx_{tag}")
    dwo = _mm(sv["merged"], dz1, "tn", name=f"wout_dw_{tag}")
    dya, dyb, dyc, dproj, doa, dob, doc = _merge_bwd(dmerged, sv["oa"], sv["ob"], sv["oc"], proj, W["wbr"])
    dwbr = _branch_dw(sv["oa"], dya, 0, None, f"wba_dw_{tag}")
    dwbr = _branch_dw(sv["ob"], dyb, 512, dwbr, f"wbb_dw_{tag}")
    dwbr = _branch_dw(sv["oc"], dyc, 1024, dwbr, f"wbc_dw_{tag}")
    late = dict(w_branch=dwbr.reshape(4, 512, D), w_out=dwo.reshape(4, 256, D), ffn_up=dwup4, ffn_down=dwdn.reshape(4, FFN // 4, D))
    snw, tie = W["snw"], jnp.zeros((8, 128), F32)
    if emit is not None:
        token = emit(late)
        if token is not None:
            snw, tie = snw + token, tie + token
    dy, dproj, st_post = _ssd_post_bwd(doc, sv["y"], proj, snw, dproj)
    dxc, dproj, st_ssd = _ssd_core_bwd(sv["xc"], proj, sv["hprev"], dy, W["dtb"], W["alog"], W["dskip"], E, ET, dproj)
    dproj, dcw, dcb = _ssd_pre_bwd(proj, dxc, W["cw"], W["cb"], dproj)
    if mid is not None:
        token = mid(dcb)
        if token is not None:
            tie = tie + token
    dproj, dpw, dps = _pool_bwd(proj, dob, W["pool_w"], W["pool_scale"], dproj, tie)
    dqf, dkf, dve = _attn_bwd(sv["qf"], sv["kf"], sv["ve"], doa, tie)
    dproj, dwqb, dwkvb4, st_mla = _mla_prep_bwd(proj, dqf, dkf, dve, W["wqb"], W["wkvb4"], W["qan"], W["kvan"], W["wq"], W["wk"], rope, dproj)
    dh1 = _mm(dproj, W["win"], "nt", tk=1408, name=f"proj_dx_{tag}")
    dwin = _mm(sv["h1"], dproj, "tn", tn=640, name=f"proj_dw_{tag}")
    dx, st1 = _ln_mod_bwd(sv["x"], dh1, dx1, W["nw1"], sc1, name=f"ln1_bwd_{tag}")
    grads = dict(
        norm1_w=st1[2], norm2_w=st2[2], w_in=_win_unlayout(dwin),
        q_a_norm=st_mla[0, :Q_RANK], kv_a_norm=st_mla[1, :KV_RANK], q_norm=st_mla[2, :96], k_norm=st_mla[3, :96],
        w_q_b=jnp.transpose(dwqb.reshape(Q_RANK, 4, 2, 128)[:, :, :, :96], (1, 0, 2, 3)).reshape(4, Q_RANK, 192), w_kv_b=dwkvb4,
        pool_w=dpw, pool_scale=dps[0], ssd_conv_w=dcw, ssd_conv_b=dcb[0],
        ssd_dt_bias=st_ssd[0, :SSD_HEADS], ssd_a_log=st_ssd[1, :SSD_HEADS], ssd_d=st_ssd[2, :SSD_HEADS], ssd_norm_w=st_post[0],
        ffn_conv_w=jnp.transpose(dfcw, (1, 0, 2)).reshape(3, 2 * FFN), ffn_conv_b=dfcb.reshape(2 * FFN), **late)
    dmod = jnp.concatenate([st1[0:2], dg1[0:1], st2[0:2], dg2[0:1]], 0)
    return dx, grads, dmod


def _ssd_expand():
    E = (jnp.arange(SSD_INNER)[None, :] // SSD_P == jnp.arange(128)[:, None]).astype(F32)
    return E, E.T


def _rope_tables(positions):
    inv_freq = ROPE_THETA ** (-jnp.arange(0, ROPE_DIM, 2, dtype=F32) / ROPE_DIM)
    invf = jnp.concatenate([jnp.zeros((NOPE,), F32), inv_freq, inv_freq, jnp.zeros((32,), F32)]).reshape(1, 128)
    posb = jnp.broadcast_to(positions.astype(F32)[:, None], (positions.shape[0], 128))
    return _rope_tab(posb, invf)


def _local_step(x, target, positions, mods, get_layer, bwd_mod=None, emit=None, mid=None, done=None):
    rope = _rope_tables(positions)
    E, ET = _ssd_expand()
    Ws, saved, h = [], [], x
    for l in range(2):
        Ws.append(_prep_layer(*get_layer(l, h)))
        h, sv = _layer_fwd(h, mods[l], Ws[l], rope, E, l)
        saved.append(sv)
    dy, lpart = _loss_grad(h, target)
    grads, dmods = [None, None], [None, None]
    for l in (1, 0):
        mod8 = mods[l] if bwd_mod is None else bwd_mod(l)
        hook = lambda f: None if f is None else functools.partial(f, l)
        dy, grads[l], dmods[l] = _layer_bwd(dy, saved[l], mod8, Ws[l], rope, E, ET, l, hook(emit), hook(mid))
        if done is not None:
            done(l, grads[l])
    return lpart[0, 0], dy, grads, dmods


_ANY = pl.BlockSpec(memory_space=pl.ANY)
_VMEM = pl.BlockSpec(memory_space=pltpu.VMEM)


def _place():
    x, y, c = lax.axis_index("x"), lax.axis_index("y"), lax.axis_index("c")
    return x, y, c, [(1 - x, y), (x, 1 - y), (1 - x, 1 - y)]


def _allgather8(v, name):
    m_per, n = v.shape

    def body(x_ref, out_ref, send_sems, recv_sems, local_sem):
        x, y, c, chips = _place()
        me, sibling = (x, y, c), (x, y, 1 - c)

        def rows(px, py, pc):
            return out_ref.at[pl.ds((4 * px + 2 * py + pc) * m_per, m_per), :]

        def copy(k, block, to, src=None):
            return pltpu.make_async_remote_copy(src_ref=rows(*block) if src is None else src, dst_ref=rows(*block),
                                                send_sem=send_sems.at[k], recv_sem=recv_sems.at[k], device_id=to, device_id_type=MESH)

        mine = pltpu.make_async_copy(x_ref, rows(*me), local_sem)
        mine.start()
        first = [copy(0, me, sibling, src=x_ref)] + [copy(1 + j, me, (*chip, c), src=x_ref) for j, chip in enumerate(chips)]
        for cp in first:
            cp.start()
        passed = [copy(4 + j, (*chip, c), sibling) for j, chip in enumerate(chips)]
        for j, chip in enumerate(chips):
            copy(1 + j, (*chip, c), me).wait_recv()
            passed[j].start()
        copy(0, sibling, me).wait_recv()
        for j, chip in enumerate(chips):
            copy(4 + j, (*chip, 1 - c), me).wait_recv()
        for cp in first + passed:
            cp.wait_send()
        mine.wait()

    return pl.pallas_call(
        body, out_shape=_sds((8 * m_per, n), v.dtype), in_specs=[_VMEM], out_specs=_VMEM,
        scratch_shapes=[pltpu.SemaphoreType.DMA((7,)), pltpu.SemaphoreType.DMA((7,)), pltpu.SemaphoreType.DMA], name=name)(v)


def _sems(n):
    return [pltpu.SemaphoreType.DMA((n,)), pltpu.SemaphoreType.DMA((n,))]


_HBM = pl.BlockSpec(memory_space=pltpu.HBM)
_SEM = pl.BlockSpec(memory_space=pltpu.SEMAPHORE)
_EFFECT = pltpu.CompilerParams(has_side_effects=pltpu.SideEffectType.DATAFLOW_SIDE_EFFECTING)


def _ici_copy(src_refs, land_refs, send_sems, recv_sems, a, j, slices, incoming):
    x, y, c, chips = _place()
    if _peers(slices) == 1:
        src, dst = slices(src_refs[a], land_refs[a], 1 - c if incoming else c)
        return pltpu.make_async_remote_copy(src_ref=src, dst_ref=dst, send_sem=send_sems.at[a], recv_sem=recv_sems.at[a],
                                            device_id=(x, y, 1 - c), device_id_type=MESH)
    me, other = 2 * x + y, 2 * chips[j][0] + chips[j][1]
    src, dst = slices(src_refs[a], land_refs[a], other, me, c) if incoming else slices(src_refs[a], land_refs[a], me, other, c)
    return pltpu.make_async_remote_copy(src_ref=src, dst_ref=dst, send_sem=send_sems.at[3 * a + j], recv_sem=recv_sems.at[3 * a + j],
                                        device_id=(*chips[j], c), device_id_type=MESH)


def _peers(slices):
    return 1 if slices is _halves_slices else 3


def _ici_start(srcs, land_shapes, slices, after, name):
    na = len(srcs)

    def body(*refs):
        src_refs, land_refs, send_sems, recv_sems = refs[:na], refs[na:2 * na], refs[2 * na + 1], refs[2 * na + 2]
        for a in range(na):
            for j in range(_peers(slices)):
                _ici_copy(src_refs, land_refs, send_sems, recv_sems, a, j, slices, False).start()
        refs[-1][...] = jnp.zeros_like(refs[-1])

    hbm = lambda v: pltpu.with_memory_space_constraint(v, pltpu.HBM)
    lands = [hbm(lax.empty(s.shape, s.dtype)) for s in land_shapes]
    return pl.pallas_call(
        body, name=name,
        out_shape=(pltpu.SemaphoreType.DMA((_peers(slices) * na,)), pltpu.SemaphoreType.DMA((_peers(slices) * na,)),
                   *[pltpu.HBM(v.shape, v.dtype) for v in srcs],
                   *[pltpu.HBM(s.shape, s.dtype) for s in land_shapes], _sds((8, 128))),
        in_specs=[_HBM] * (2 * na) + [_ANY], out_specs=(_SEM, _SEM, *[_HBM] * (2 * na), _VMEM),
        input_output_aliases={i: 2 + i for i in range(2 * na)}, compiler_params=_EFFECT)(*[hbm(v) for v in srcs], *lands, after)


def _ici_wait(handle, slices, after, name):
    na = (len(handle) - 3) // 2

    def body(*refs):
        src_refs, land_refs, send_sems, recv_sems = refs[:na], refs[na:2 * na], refs[2 * na], refs[2 * na + 1]
        for a in range(na):
            for j in range(_peers(slices)):
                _ici_copy(src_refs, land_refs, send_sems, recv_sems, a, j, slices, False).wait_send()
                _ici_copy(src_refs, land_refs, send_sems, recv_sems, a, j, slices, True).wait_recv()

    thru = handle[2:2 + 2 * na]
    outs = pl.pallas_call(
        body, name=name, out_shape=[pltpu.HBM(v.shape, v.dtype) for v in thru], in_specs=[_HBM] * (2 * na) + [_SEM, _SEM, _ANY],
        out_specs=[_HBM] * (2 * na), input_output_aliases={i: i for i in range(2 * na)}, compiler_params=_EFFECT)(
        *thru, handle[0], handle[1], after)
    return outs[:na], outs[na:]


def _gather_slices(p_ref, land_ref, sender, receiver, c):
    r2 = p_ref.shape[0] // 2
    return p_ref.at[pl.ds(c * r2, r2), :], land_ref.at[sender, pl.ds(c * r2, r2), :]


def _scatter_slices(a_ref, t_ref, sender, receiver, c):
    return a_ref.at[receiver], t_ref.at[sender]


def _halves_slices(g_ref, land_ref, sender_c):
    r2 = g_ref.shape[1] // 2
    return g_ref.at[:, pl.ds((1 - sender_c) * r2, r2), :], land_ref


def _gather_start(arrs, after, name):
    return _ici_start(arrs, [_sds((4,) + v.shape, v.dtype) for v in arrs], _gather_slices, after, name)


def _gather_finish(handle, after, name):
    arrs, stacks = _ici_wait(handle, _gather_slices, after, name + "_wait")
    na = len(stacks)

    def body(*refs):
        s_refs, o_refs, (send_sems, recv_sems) = refs[:na], refs[na:2 * na], refs[2 * na:]
        x, y, c, chips = _place()

        def copy(a, j, cc, to):
            r2 = s_refs[a].shape[1] // 2
            at = (2 * chips[j][0] + chips[j][1], pl.ds(cc * r2, r2), slice(None))
            return pltpu.make_async_remote_copy(src_ref=s_refs[a].at[at], dst_ref=o_refs[a].at[at], send_sem=send_sems.at[3 * a + j],
                                                recv_sem=recv_sems.at[3 * a + j], device_id=to, device_id_type=MESH)

        passed = [copy(a, j, c, (x, y, 1 - c)) for a in range(na) for j in range(3)]
        for cp in passed:
            cp.start()
        for a in range(na):
            for j in range(3):
                copy(a, j, 1 - c, (x, y, c)).wait_recv()
        for cp in passed:
            cp.wait_send()

    stacks = pl.pallas_call(
        body, out_shape=[_sds(v.shape, v.dtype) for v in stacks], in_specs=[_ANY] * na, out_specs=[_ANY] * na,
        input_output_aliases={i: i for i in range(na)}, scratch_shapes=_sems(3 * na), name=name + "_pass")(*stacks)
    chip = 2 * lax.axis_index("x") + lax.axis_index("y")
    return [lax.dynamic_update_slice(s, v[None], (chip, 0, 0)) for s, v in zip(stacks, arrs)]


def _halves_start(gs, after, tag):
    return _ici_start(gs, [_sds((4, v.shape[1] // 2, v.shape[2]), v.dtype) for v in gs], _halves_slices, after, f"rs_halves_{tag}_start")


def _join_halves(fs, name):
    na = len(fs)

    def body(*refs):
        f_refs, o_refs, (send_sems, recv_sems) = refs[:na], refs[na:2 * na], refs[2 * na:]
        x, y, c, _ = _place()

        def copy(a, cc, to):
            r2 = f_refs[a].shape[0] // 2
            return pltpu.make_async_remote_copy(src_ref=f_refs[a].at[pl.ds(cc * r2, r2), :], dst_ref=o_refs[a].at[pl.ds(cc * r2, r2), :],
                                                send_sem=send_sems.at[a], recv_sem=recv_sems.at[a], device_id=to, device_id_type=MESH)

        cps = [copy(a, c, (x, y, 1 - c)) for a in range(na)]
        for cp in cps:
            cp.start()
        for a in range(na):
            copy(a, 1 - c, (x, y, c)).wait_recv()
        for cp in cps:
            cp.wait_send()

    return pl.pallas_call(
        body, out_shape=[_sds(v.shape, v.dtype) for v in fs], in_specs=[_ANY] * na, out_specs=[_ANY] * na,
        input_output_aliases={i: i for i in range(na)}, scratch_shapes=_sems(na), name=name)(*fs)


def _sum_chips(a, t, chip, ci, name):
    _, r2, n = t.shape
    tm = _row_tile(r2)
    nb = r2 // tm

    def body(k_ref, a_ref, t1_ref, t2_ref, t3_ref, o_ref):
        o_ref[...] = ((a_ref[...].astype(F32) + t1_ref[...].astype(F32)) + t2_ref[...].astype(F32)) + t3_ref[...].astype(F32)

    def slot(j):
        return pl.BlockSpec((None, tm, n), lambda i, k_ref: (lax.rem(k_ref[0] + j, 4), i, 0))

    return pl.pallas_call(
        body, grid_spec=pltpu.PrefetchScalarGridSpec(num_scalar_prefetch=1, grid=(nb,), in_specs=[slot(0), slot(1), slot(2), slot(3)],
                                                     out_specs=pl.BlockSpec((tm, n), lambda i, k_ref: (k_ref[1] * nb + i, 0))),
        out_shape=_sds((2 * r2, n)), name=name, compiler_params=_PAR)(jnp.stack([chip, ci]).astype(jnp.int32), a, t, t, t)


def _add_cast(g, recv, c, name):
    _, r2, n = recv.shape

    def body(c_ref, a_ref, b_ref, o_ref):
        o_ref[...] = (a_ref[...] + b_ref[...]).astype(BF16)

    spec = pl.BlockSpec((None, r2, n), lambda k, c_ref: (k, 0, 0))
    return pl.pallas_call(
        body, grid_spec=pltpu.PrefetchScalarGridSpec(
            num_scalar_prefetch=1, grid=(4,), in_specs=[pl.BlockSpec((None, r2, n), lambda k, c_ref: (k, c_ref[0], 0)), spec], out_specs=spec),
        out_shape=_sds(recv.shape, BF16), name=name, compiler_params=_PAR)(c.reshape(1).astype(jnp.int32), g, recv)


def _sum_lead(t, name, tm=256):
    P, R, n = t.shape
    tm = _row_tile(R, tm)

    def body(t_ref, o_ref):
        acc = t_ref[0].astype(F32)
        for j in range(1, P):
            acc = acc + t_ref[j].astype(F32)
        o_ref[...] = acc

    return pl.pallas_call(body, grid=(R // tm,), in_specs=[pl.BlockSpec((P, tm, n), lambda i: (0, i, 0))],
                          out_specs=pl.BlockSpec((tm, n), lambda i: (i, 0)), out_shape=_sds((R, n)), name=name, compiler_params=_PAR)(t)


def _ada_fwd(c16, ada_w, ada_b_cols, tn=512):
    L, _, n = ada_w.shape

    def body(c_ref, w_ref, b_ref, o_ref):
        o_ref[0] = _dot(_silu(c_ref[...]).astype(BF16), w_ref[0].astype(BF16), _NN) + b_ref[0]

    return pl.pallas_call(
        body, grid=(L, n // tn),
        in_specs=[pl.BlockSpec((16, D), lambda l, j: (0, 0)), pl.BlockSpec((1, D, tn), lambda l, j: (l, 0, j)), pl.BlockSpec((1, 1, tn), lambda l, j: (l, 0, j))],
        out_specs=pl.BlockSpec((1, 16, tn), lambda l, j: (l, 0, j)), out_shape=_sds((L, 16, n)), name="ada_fwd",
        compiler_params=pltpu.CompilerParams(dimension_semantics=("parallel", "parallel")))(c16, ada_w, ada_b_cols)


def _ada_bwd(c16, dmod, tn=512):
    L, _, n = dmod.shape

    def body(c_ref, d_ref, o_ref):
        o_ref[0] = _dot(_silu(c_ref[...]).astype(BF16), d_ref[0].astype(BF16), _TN)

    return pl.pallas_call(
        body, grid=(L, n // tn), in_specs=[pl.BlockSpec((16, D), lambda l, j: (0, 0)), pl.BlockSpec((1, 16, tn), lambda l, j: (l, 0, j))],
        out_specs=pl.BlockSpec((1, D, tn), lambda l, j: (l, 0, j)), out_shape=_sds((L, D, n)), name="ada_bwd",
        compiler_params=pltpu.CompilerParams(dimension_semantics=("parallel", "parallel")))(c16, dmod)


def _adam_math(w, g, m, v):
    mn = ADAM_B1 * m + (1.0 - ADAM_B1) * g
    vn = ADAM_B2 * v + (1.0 - ADAM_B2) * (g * g)
    m_hat = mn / (1.0 - ADAM_B1 ** ADAM_STEP)
    v_hat = vn / (1.0 - ADAM_B2 ** ADAM_STEP)
    return -ADAM_LR * (m_hat / (jnp.sqrt(v_hat) + ADAM_EPS) + ADAM_WD * w), mn, vn


def _adamw(w, g, m, v, name):
    R, n = w.shape
    tm = _row_tile(R)

    def body(w_ref, g_ref, m_ref, v_ref, d_ref, nm_ref, nv_ref):
        d_ref[...], nm_ref[...], nv_ref[...] = _adam_math(w_ref[...], g_ref[...], m_ref[...], v_ref[...])

    spec = pl.BlockSpec((tm, n), lambda i: (i, 0))
    return pl.pallas_call(body, grid=(R // tm,), in_specs=[spec] * 4, out_specs=(spec,) * 3, out_shape=(_sds((R, n)),) * 3,
                          name=name, compiler_params=_PAR)(w, g, m, v)


def _adamw_cols(w, g0, g1, m, v, name):
    fwd, back = (lambda t: jnp.transpose(t, (2, 0, 1))), (lambda t: jnp.transpose(t, (1, 2, 0)))
    gt = jnp.stack([g0.T, g1.T], 1)
    n, _, r = gt.shape
    tr = max(t for t in range(1, 257) if n % t == 0)

    def body(w_ref, g_ref, m_ref, v_ref, d_ref, nm_ref, nv_ref):
        d_ref[...], nm_ref[...], nv_ref[...] = _adam_math(w_ref[...], g_ref[...], m_ref[...], v_ref[...])

    spec = pl.BlockSpec((tr, 2, r), lambda i: (i, 0, 0))
    outs = pl.pallas_call(body, grid=(n // tr,), in_specs=[spec] * 4, out_specs=(spec,) * 3, out_shape=(_sds(gt.shape),) * 3,
                          name=name, compiler_params=_PAR)(fwd(w), gt, fwd(m), fwd(v))
    return (back(gt), *[back(o) for o in outs])


def _adamw_layers(w, g0, g1, m, v, after, name):
    _, r, n = w.shape
    tm = _row_tile(r)
    nb = r // tm

    def body(w_ref, g0_ref, g1_ref, m_ref, v_ref, after_ref, g_ref, d_ref, nm_ref, nv_ref):
        gv = jnp.where(pl.program_id(0) == 0, g0_ref[...], g1_ref[...])
        g_ref[...] = gv
        d_ref[...], nm_ref[...], nv_ref[...] = _adam_math(w_ref[...], gv, m_ref[...], v_ref[...])

    spec = pl.BlockSpec((None, tm, n), lambda l, i: (l, i, 0))
    g0_spec = pl.BlockSpec((tm, n), lambda l, i: (i * (1 - l) + (nb - 1) * l, 0))
    g1_spec = pl.BlockSpec((tm, n), lambda l, i: (i * l, 0))
    return pl.pallas_call(body, grid=(2, nb), in_specs=[spec, g0_spec, g1_spec, spec, spec, _ANY], out_specs=(spec,) * 4,
                          out_shape=(_sds(w.shape),) * 4, name=name,
                          compiler_params=pltpu.CompilerParams(dimension_semantics=("arbitrary", "arbitrary")))(w, g0, g1, m, v, after)


_W_NAMES = ["ada_w", "ada_b", "norm1_w", "w_in", "q_a_norm", "w_q_b", "kv_a_norm", "w_kv_b", "q_norm", "k_norm", "pool_w",
            "pool_scale", "ssd_conv_w", "ssd_conv_b", "ssd_dt_bias", "ssd_a_log", "ssd_d", "ssd_norm_w", "w_branch", "w_out",
            "norm2_w", "ffn_up", "ffn_conv_w", "ffn_conv_b", "ffn_down"]
_BIG = [("w_in", (D, IN_DIM // 4), 1), ("w_q_b", (Q_RANK, 192), 1), ("w_kv_b", (KV_RANK, 256), 1), ("w_branch", (512, D), 0),
        ("w_out", (256, D), 0), ("ffn_up", (D, 2 * FFN // 4), 1), ("ffn_down", (FFN // 4, D), 0)]

_SMALL = [("norm1_w", (D,)), ("q_a_norm", (Q_RANK,)), ("kv_a_norm", (KV_RANK,)), ("q_norm", (96,)), ("k_norm", (96,)),
          ("pool_w", (4, 128, 128)), ("pool_scale", (512,)), ("ssd_conv_w", (4, SSD_CONV_DIM)), ("ssd_conv_b", (SSD_CONV_DIM,)),
          ("ssd_dt_bias", (SSD_HEADS,)), ("ssd_a_log", (SSD_HEADS,)), ("ssd_d", (SSD_HEADS,)), ("ssd_norm_w", (D,)), ("norm2_w", (D,)),
          ("ffn_conv_w", (3, 2 * FFN)), ("ffn_conv_b", (2 * FFN,))]
_CONV_SHARDED = {"ssd_conv_w": SSD_CONV_DIM // 4, "ffn_conv_w": 2 * FFN // 4}


def _pack_flat(arrs, mult):
    flat = jnp.concatenate([a.astype(F32).reshape(-1) for a in arrs])
    rows = -(-flat.shape[0] // (128 * mult)) * mult
    return jnp.pad(flat, (0, rows * 128 - flat.shape[0])).reshape(rows, 128), [a.shape for a in arrs]


def _unpack_flat(packed, shapes):
    flat, out, off = packed.reshape(-1), [], 0
    for s in shapes:
        n = 1
        for d in s:
            n *= d
        out.append(flat[off:off + n].reshape(s))
        off += n
    return out


_EARLY = ["w_in", "w_q_b", "w_kv_b"]
_LATE = ["w_branch", "w_out", "ffn_up", "ffn_down"]


def _early_weights(a, l, stacks, conv_full):
    p = {n: a[n][l] for n in _W_NAMES if n not in ("ada_w", "ada_b")}
    p.update({n: conv_full[n][l] for n in conv_full})
    p.update(w_in4=stacks[0], w_q_b4=stacks[1], w_kv_b4=stacks[2])
    return p


def _late_weights(stacks):
    return _prep_late(dict(w_branch=stacks[0].reshape(2048, D), w_out=stacks[1].reshape(D, D), ffn_up4=stacks[2],
                           ffn_down=stacks[3].reshape(FFN, D)))


def _reduce_start(halves, ci, after, tag):
    chip_sum = []
    for t, h in halves.items():
        gs, recv = _ici_wait(h, _halves_slices, after, f"rs_halves_{t}_wait")
        chip_sum += [_add_cast(g, r, ci, f"rs_add_{t}") for g, r in zip(gs, recv)]
    return _ici_start(chip_sum, [_sds(v.shape, v.dtype) for v in chip_sum], _scatter_slices, after, f"rs_scatter_{tag}_start")


def _reduce_finish(started, after, tag):
    chip_sum, got = _ici_wait(started, _scatter_slices, after, f"rs_scatter_{tag}_wait")
    chip, ci = 2 * lax.axis_index("x") + lax.axis_index("y"), lax.axis_index("c")
    return _join_halves([_sum_chips(s, t, chip, ci, f"rs_sum_{tag}") for s, t in zip(chip_sum, got)], f"rs_join_{tag}")


def kernel(x, c, positions, ada_w, ada_b, norm1_w, w_in, q_a_norm, w_q_b, kv_a_norm, w_kv_b, q_norm, k_norm, pool_w, pool_scale, ssd_conv_w, ssd_conv_b, ssd_dt_bias, ssd_a_log, ssd_d, ssd_norm_w, w_branch, w_out, norm2_w, ffn_up, ffn_conv_w, ffn_conv_b, ffn_down, loss_target, m_ada_w, m_ada_b, m_norm1_w, m_w_in, m_q_a_norm, m_w_q_b, m_kv_a_norm, m_w_kv_b, m_q_norm, m_k_norm, m_pool_w, m_pool_scale, m_ssd_conv_w, m_ssd_conv_b, m_ssd_dt_bias, m_ssd_a_log, m_ssd_d, m_ssd_norm_w, m_w_branch, m_w_out, m_norm2_w, m_ffn_up, m_ffn_conv_w, m_ffn_conv_b, m_ffn_down, v_ada_w, v_ada_b, v_norm1_w, v_w_in, v_q_a_norm, v_w_q_b, v_kv_a_norm, v_w_kv_b, v_q_norm, v_k_norm, v_pool_w, v_pool_scale, v_ssd_conv_w, v_ssd_conv_b, v_ssd_dt_bias, v_ssd_a_log, v_ssd_d, v_ssd_norm_w, v_w_branch, v_w_out, v_norm2_w, v_ffn_up, v_ffn_conv_w, v_ffn_conv_b, v_ffn_down):
    a = dict(locals())
    xi, yi, ci = lax.axis_index("x"), lax.axis_index("y"), lax.axis_index("c")
    chip = 2 * xi + yi
    dev = 2 * chip + ci
    ncol = 6 * D // 4

    c_all = _allgather8(c.reshape(8, 128), "gather_c").reshape(8, D)
    c16 = jnp.pad(c_all, ((0, 8), (0, 0)))
    ada_b_cols = lax.dynamic_slice_in_dim(ada_b, chip * ncol, ncol, axis=1).reshape(2, 1, ncol)
    mod_part = _ada_fwd(c16, ada_w, ada_b_cols)[:, :8]
    small1, shapes1 = _pack_flat([mod_part, ssd_conv_w, ffn_conv_w], 8)
    got1 = _allgather8(small1, "gather_mod").reshape(8, -1, 128)
    per_chip = [_unpack_flat(got1[2 * k], shapes1) for k in range(4)]
    mod_all = jnp.concatenate([per_chip[k][0] for k in range(4)], -1)
    conv_full = {"ssd_conv_w": jnp.concatenate([per_chip[k][1] for k in range(4)], -1),
                 "ffn_conv_w": jnp.concatenate([per_chip[k][2] for k in range(4)], -1)}
    mod_mine = lax.dynamic_index_in_dim(mod_all, dev, axis=1, keepdims=False).reshape(2, 6, D)
    mods = [jnp.pad(mod_mine[l], ((0, 2), (0, 0))) for l in range(2)]

    big = _EARLY + _LATE
    shard = lambda names, l: [a[n][l].astype(BF16) for n in names]
    g0a = _gather_start(shard(_EARLY, 0), mods[0], "gather_0a")
    g0b = _gather_start(shard(_LATE, 0), g0a[-1], "gather_0b")
    g1 = _gather_start(shard(big, 1), g0b[-1], "gather_1")
    mods[0] = mods[0] + g1[-1][0, 0]

    def get_layer(l, after):
        if l == 0:
            return (_early_weights(a, 0, _gather_finish(g0a, mods[0], "gather_0a"), conv_full),
                    lambda aft: _late_weights(_gather_finish(g0b, aft, "gather_0b")))
        stacks = _gather_finish(g1, after, "gather_1")
        return _early_weights(a, 1, stacks[:3], conv_full), lambda aft: _late_weights(stacks[3:])

    halves, scatters = {}, {}

    def bwd_mod(l):
        return mods[l] if l == 1 else mods[0] + halves["1a"][-1][0, 0]

    def emit(l, late):
        if l == 1:
            halves["1b"] = _halves_start([late[n] for n in _LATE], late["ffn_down"], "1b")
            return halves["1b"][-1][0, 0]
        scatters["1"] = _reduce_start({"1b": halves["1b"], "1a": halves["1a"]}, ci, late["ffn_down"], "1")
        halves["0b"] = _halves_start([late[n] for n in _LATE], scatters["1"][-1], "0b")
        return halves["0b"][-1][0, 0]

    def mid(l, after):
        if l == 0:
            scatters["0b"] = _reduce_start({"0b": halves["0b"]}, ci, after, "0b")
            return scatters["0b"][-1][0, 0]

    def done(l, grads_l):
        halves[f"{l}a"] = _halves_start([grads_l[n] for n in _EARLY], halves[f"{l}b"][-1], f"{l}a")

    lpart, grad_x, grads, dmods = _local_step(x[0], loss_target[0], positions[0], mods, get_layer, bwd_mod, emit, mid, done)
    loss = lax.psum(lpart, ("x", "y", "c"))
    behind = grad_x[:1, :128] + halves["0a"][-1][:1]
    red1 = dict(zip(_LATE + _EARLY, _reduce_finish(scatters["1"], behind, "1")))
    red0b = _reduce_finish(scatters["0b"], red1["w_in"], "0b")
    scatters["0a"] = _reduce_start({"0a": halves["0a"]}, ci, red0b[0], "0a")

    small2, shapes2 = _pack_flat([jnp.stack(dmods)] + [grads[l][n] for l in range(2) for n, _ in _SMALL], 16)
    got2 = _allgather8(small2, "gather_small").reshape(8, -1, 128)
    tot = _unpack_flat(_sum_lead(got2, "sum_small"), shapes2)
    g = {"ada_b": tot[0].reshape(2, 6 * D)}
    for i, (n, _) in enumerate(_SMALL):
        g[n] = jnp.stack([tot[1 + i], tot[1 + len(_SMALL) + i]])
    for n, w in _CONV_SHARDED.items():
        g[n] = lax.dynamic_slice_in_dim(g[n], chip * w, w, axis=2)
    nd = 2 * 6 * D // 128
    dmod_all = jnp.transpose(got2[:, :nd].reshape(8, 2, 6 * D), (1, 0, 2))
    dmod_cols = lax.dynamic_slice_in_dim(jnp.pad(dmod_all, ((0, 0), (0, 8), (0, 0))), chip * ncol, ncol, axis=2)
    g["ada_w"] = _ada_bwd(c16, dmod_cols)

    delta, new_m, new_v = {}, {}, {}
    token = scatters["0a"][-1]
    for n, r0 in zip(_LATE, red0b):
        g[n], delta[n], new_m[n], new_v[n] = _adamw_layers(a[n], r0, red1[n], a["m_" + n], a["v_" + n], token, f"adamw_{n}")
    shp = ada_w.shape
    r2 = lambda t: t.reshape(-1, shp[-1])
    delta["ada_w"], new_m["ada_w"], new_v["ada_w"] = (
        o.reshape(shp) for o in _adamw(r2(ada_w), r2(g["ada_w"]), r2(m_ada_w), r2(v_ada_w), "adamw_ada_w"))
    behind = (delta[_LATE[-1]][0, 0, :1] + delta["ada_w"][0, 0, :1]).reshape(1)
    red0a = _reduce_finish(scatters["0a"], behind, "0a")
    for n, r0 in zip(_EARLY, red0a):
        if n == "w_in":
            g[n], delta[n], new_m[n], new_v[n] = _adamw_cols(a[n], r0, red1[n], a["m_" + n], a["v_" + n], f"adamw_{n}")
        else:
            g[n], delta[n], new_m[n], new_v[n] = _adamw_layers(a[n], r0, red1[n], a["m_" + n], a["v_" + n], token, f"adamw_{n}")
    rest = [n for n in _W_NAMES if n not in big and n != "ada_w"]
    packs = [_pack_flat([t[n] if pre is None else t[pre + n] for n in rest], 128)[0]
             for t, pre in ((a, None), (g, None), (a, "m_"), (a, "v_"))]
    rest_shapes = [a[n].shape for n in rest]
    outs = [_unpack_flat(o, rest_shapes) for o in _adamw(*packs, "adamw_rest")]
    for i, n in enumerate(rest):
        delta[n], new_m[n], new_v[n] = outs[0][i], outs[1][i], outs[2][i]

    return (loss, grad_x[None], *[g[n] for n in _W_NAMES], *[delta[n] for n in _W_NAMES],
            *[new_m[n] for n in _W_NAMES], *[new_v[n] for n in _W_NAMES])
```

```python
import functools

import jax
import jax.numpy as jnp
from jax import lax
from jax.experimental import pallas as pl
from jax.experimental.pallas import tpu as pltpu

F32 = jnp.float32
BF16 = jnp.bfloat16
MESH = pl.DeviceIdType.MESH
HI = lax.Precision.HIGHEST

D = 1024
N_HEADS = 8
NOPE, ROPE_DIM = 64, 32
Q_RANK, KV_RANK = 384, 256
POOL_WINDOWS = (2, 4, 8, 16)
SSD_HEADS, SSD_P, SSD_N, SSD_L = 16, 64, 128, 128
SSD_INNER = 1024
SSD_CONV_DIM = 1536
FFN = 2816
EPS = 1e-6
ROPE_THETA = 10000.0
OG, OZ, OX, OP, OQ, OKV, ODT, IN_PAD = 0, 3072, 4096, 5632, 6144, 6528, 6912, 7040
IN_DIM = 6832
ADAM_LR, ADAM_B1, ADAM_B2, ADAM_EPS, ADAM_WD, ADAM_STEP = 0.001, 0.9, 0.999, 1e-08, 0.01, 10

_ARB = pltpu.CompilerParams(dimension_semantics=("arbitrary",))
_PAR = pltpu.CompilerParams(dimension_semantics=("parallel",))


def _pick(n, pref):
    if n <= pref:
        return n
    best = None
    for t in range(128, pref + 1, 128):
        if n % t == 0:
            best = t
    assert best is not None, (n, pref)
    return best


def _row_tile(r, cap=256):
    best = None
    for t in range(16, min(r, cap) + 1, 16):
        if r % t == 0:
            best = t
    assert best is not None, r
    return best


def _sds(shape, dtype=F32):
    return jax.ShapeDtypeStruct(tuple(shape), dtype)


def _iota(shape, dim):
    return lax.broadcasted_iota(jnp.int32, shape, dim)


def _sigmoid(x):
    return 0.5 * jnp.tanh(0.5 * x) + 0.5


def _silu(x):
    return x * _sigmoid(x)


def _dsilu(x):
    s = _sigmoid(x)
    return s * (1.0 + x * (1.0 - s))


def _dot(a, b, dims):
    return lax.dot_general(a, b, (dims, ((), ())), preferred_element_type=F32)


_NN, _NT, _TN = ((1,), (0,)), ((1,), (1,)), ((0,), (0,))


def _dot_hi(a, b, dims=_NN):
    return lax.dot_general(a, b, (dims, ((), ())), preferred_element_type=F32, precision=HI)


def _shift_down(x, j):
    n = x.shape[0]
    return jnp.where(_iota(x.shape, 0) >= j, pltpu.roll(x, j, 0), 0.0)


def _shift_up(x, j):
    n = x.shape[0]
    return jnp.where(_iota(x.shape, 0) < n - j, pltpu.roll(x, n - j, 0), 0.0)


def _mm(a, b, mode="nn", out_dtype=F32, tm=2048, tn=512, tk=4096, res=None, gate=None, name="mm"):
    if mode == "nn":
        (M, K), (K2, N) = a.shape, b.shape
    elif mode == "nt":
        (M, K), (N, K2) = a.shape, b.shape
    else:
        (K, M), (K2, N) = a.shape, b.shape
    assert K == K2, (a.shape, b.shape, mode)
    tm, tn, tk = _pick(M, tm), _pick(N, tn), _pick(K, tk)
    nk = K // tk
    dims = {"nn": _NN, "nt": _NT, "tn": _TN}[mode]
    fused = res is not None

    def body(*refs):
        a_ref, b_ref = refs[:2]

        def finish(acc):
            if fused:
                r_ref, g_ref, o_ref, raw_ref = refs[2:6]
                raw_ref[...] = acc
                o_ref[...] = r_ref[...] + g_ref[...] * acc
            else:
                refs[2][...] = acc.astype(out_dtype)

        _mm_steps(a_ref, b_ref, dims, nk, refs[-1] if nk > 1 else None, finish)

    if mode == "nn":
        a_spec = pl.BlockSpec((tm, tk), lambda i, j, k: (i, k))
        b_spec = pl.BlockSpec((tk, tn), lambda i, j, k: (k, j))
    elif mode == "nt":
        a_spec = pl.BlockSpec((tm, tk), lambda i, j, k: (i, k))
        b_spec = pl.BlockSpec((tn, tk), lambda i, j, k: (j, k))
    else:
        a_spec = pl.BlockSpec((tk, tm), lambda i, j, k: (k, i))
        b_spec = pl.BlockSpec((tk, tn), lambda i, j, k: (k, j))
    o_spec = pl.BlockSpec((tm, tn), lambda i, j, k: (i, j))
    in_specs, args = [a_spec, b_spec], [a, b]
    out_shape, out_specs = _sds((M, N), out_dtype), o_spec
    if fused:
        in_specs += [o_spec, pl.BlockSpec((1, tn), lambda i, j, k: (0, j))]
        args += [res, gate]
        out_shape, out_specs = (_sds((M, N)), _sds((M, N))), (o_spec, o_spec)
    return pl.pallas_call(
        body, grid=(M // tm, N // tn, nk), in_specs=in_specs, out_specs=out_specs, out_shape=out_shape,
        scratch_shapes=[pltpu.VMEM((tm, tn), F32)] if nk > 1 else [], name=name,
        compiler_params=pltpu.CompilerParams(dimension_semantics=("parallel", "parallel", "arbitrary")),
    )(*args)


def _mm_steps(a_ref, b_ref, dims, nk, acc_ref, finish):
    part = _dot(a_ref[...].astype(BF16), b_ref[...].astype(BF16), dims)
    if nk == 1:
        finish(part)
        return
    k = pl.program_id(2)

    @pl.when(k == 0)
    def _():
        acc_ref[...] = part

    @pl.when(k > 0)
    def _():
        acc_ref[...] += part

    @pl.when(k == nk - 1)
    def _():
        finish(acc_ref[...])


def _mm_blocks(a, b, dims, grid, a_spec, b_spec, o_spec, out_shape, acc_shape, name, into=None):
    nk = grid[2]
    extra = 0 if into is None else 1

    def body(a_ref, b_ref, *refs):
        o_ref, scratch = refs[extra], refs[extra + 1:]

        def finish(acc):
            o_ref[...] = acc.astype(o_ref.dtype)

        _mm_steps(a_ref, b_ref, dims, nk, scratch[0] if nk > 1 else None, finish)

    return pl.pallas_call(
        body, grid=grid, in_specs=[a_spec, b_spec] + [pl.BlockSpec(memory_space=pl.ANY)] * extra, out_specs=o_spec, out_shape=out_shape,
        input_output_aliases={2: 0} if extra else {},
        scratch_shapes=[pltpu.VMEM(acc_shape, F32)] if nk > 1 else [], name=name,
        compiler_params=pltpu.CompilerParams(dimension_semantics=("parallel", "parallel", "arbitrary")),
    )(a, b, *([] if into is None else [into]))


def _branch_dw(o, dy, row0, into, name, tm=512, tn=512):
    S, M = o.shape
    return _mm_blocks(o, dy, _TN, (M // tm, D // tn, 1), pl.BlockSpec((S, tm), lambda i, j, k: (0, i)),
                      pl.BlockSpec((S, tn), lambda i, j, k: (0, j)), pl.BlockSpec((tm, tn), lambda i, j, k: (i + row0 // tm, j)),
                      _sds((2048, D)), (tm, tn), name, into)


_UP_SHARD = 2 * FFN // 4


def _up_fwd(h2, wup4, name, tm=2048):
    S = h2.shape[0]
    tm = min(tm, S)
    return _mm_blocks(h2, wup4, _NN, (S // tm, 4, 1), pl.BlockSpec((tm, D), lambda i, j, k: (i, 0)),
                      pl.BlockSpec((None, D, _UP_SHARD), lambda i, j, k: (j, 0, 0)), pl.BlockSpec((tm, _UP_SHARD), lambda i, j, k: (i, j)),
                      _sds((S, 2 * FFN)), (tm, _UP_SHARD), name)


def _up_dx(dup2, wup4, name, tm=2048, tn=512):
    S = dup2.shape[1]
    tm = min(tm, S)
    return _mm_blocks(dup2, wup4, _NT, (S // tm, D // tn, 4), pl.BlockSpec((None, tm, _UP_SHARD), lambda i, j, k: (lax.div(k, 2), i, lax.rem(k, 2))),
                      pl.BlockSpec((None, tn, _UP_SHARD), lambda i, j, k: (k, j, 0)), pl.BlockSpec((tm, tn), lambda i, j, k: (i, j)),
                      _sds((S, D)), (tm, tn), name)


def _up_dw(h2, dup2, name, tm=1024, tk=4096):
    S = h2.shape[0]
    tk = min(tk, S)
    return _mm_blocks(h2, dup2, _TN, (D // tm, 4, S // tk), pl.BlockSpec((tk, tm), lambda i, j, k: (k, i)),
                      pl.BlockSpec((None, tk, _UP_SHARD), lambda i, j, k: (lax.div(j, 2), k, lax.rem(j, 2))),
                      pl.BlockSpec((None, tm, _UP_SHARD), lambda i, j, k: (j, i, 0)), _sds((4, D, _UP_SHARD)), (tm, _UP_SHARD), name)


def _row_spec(tm, n):
    return pl.BlockSpec((tm, n), lambda i: (i, 0))


def _vec_spec(n, rows=1):
    return pl.BlockSpec((rows, n), lambda i: (0, 0))


def _ln_mod(x, nw, sc, sh, name, tm=256):
    S = x.shape[0]

    def body(x_ref, nw_ref, sc_ref, sh_ref, o_ref):
        xv = x_ref[...]
        r = lax.rsqrt(jnp.mean(xv * xv, -1, keepdims=True) + EPS)
        o_ref[...] = ((xv * r * nw_ref[...]) * (1.0 + sc_ref[...]) + sh_ref[...]).astype(BF16)

    return pl.pallas_call(
        body, grid=(S // tm,), in_specs=[_row_spec(tm, D)] + [_vec_spec(D)] * 3, out_specs=_row_spec(tm, D),
        out_shape=_sds((S, D), BF16), name=name, compiler_params=_PAR)(x, nw, sc, sh)


def _ln_mod_bwd(x, dh, dres, nw, sc, name, tm=256):
    S = x.shape[0]

    def body(x_ref, dh_ref, dres_ref, nw_ref, sc_ref, dx_ref, st_ref):
        @pl.when(pl.program_id(0) == 0)
        def _():
            st_ref[...] = jnp.zeros_like(st_ref)

        xv, dhv, nwv = x_ref[...], dh_ref[...], nw_ref[...]
        r = lax.rsqrt(jnp.mean(xv * xv, -1, keepdims=True) + EPS)
        xhat = xv * r
        dn = dhv * (1.0 + sc_ref[...])
        g = dn * nwv
        dx_ref[...] = dres_ref[...] + r * (g - xhat * jnp.mean(g * xhat, -1, keepdims=True))
        st_ref[0:1, :] += jnp.sum(dhv, 0, keepdims=True)
        st_ref[1:2, :] += jnp.sum(dhv * (xhat * nwv), 0, keepdims=True)
        st_ref[2:3, :] += jnp.sum(dn * xhat, 0, keepdims=True)

    return pl.pallas_call(
        body, grid=(S // tm,), in_specs=[_row_spec(tm, D)] * 3 + [_vec_spec(D)] * 2,
        out_specs=(_row_spec(tm, D), _vec_spec(D, 8)), out_shape=(_sds((S, D)), _sds((8, D))),
        name=name, compiler_params=_ARB)(x, dh, dres, nw, sc)


def _gate_bwd(dx, out, g, name, tm=256):
    S = dx.shape[0]

    def body(dx_ref, o_ref, g_ref, dz_ref, dg_ref):
        @pl.when(pl.program_id(0) == 0)
        def _():
            dg_ref[...] = jnp.zeros_like(dg_ref)

        dxv = dx_ref[...]
        dz_ref[...] = (dxv * g_ref[...]).astype(BF16)
        dg_ref[0:1, :] += jnp.sum(dxv * o_ref[...], 0, keepdims=True)

    return pl.pallas_call(
        body, grid=(S // tm,), in_specs=[_row_spec(tm, D)] * 2 + [_vec_spec(D)],
        out_specs=(_row_spec(tm, D), _vec_spec(D, 8)), out_shape=(_sds((S, D), BF16), _sds((8, D))),
        name=name, compiler_params=_ARB)(dx, out, g)


def _loss_grad(y, t, tm=256):
    S = y.shape[0]

    def body(y_ref, t_ref, dy_ref, l_ref):
        @pl.when(pl.program_id(0) == 0)
        def _():
            l_ref[...] = jnp.zeros_like(l_ref)

        e = y_ref[...] - t_ref[...]
        dy_ref[...] = e * (1.0 / D)
        l_ref[...] += 0.5 * jnp.sum(jnp.mean(e * e, -1, keepdims=True), 0, keepdims=True)

    return pl.pallas_call(
        body, grid=(S // tm,), in_specs=[_row_spec(tm, D)] * 2,
        out_specs=(_row_spec(tm, D), pl.BlockSpec((8, 128), lambda i: (0, 0))),
        out_shape=(_sds((S, D)), _sds((8, 128))), name="loss_grad", compiler_params=_ARB)(y, t)


_CONV_ROWS = 256
_HALO = 8


def _rows(ref, lo, hi):
    S, c = ref.shape
    parts = [jnp.zeros((-lo, c), F32)] if lo < 0 else []
    parts.append(ref[max(lo, 0):min(hi, S), :])
    if hi > S:
        parts.append(jnp.zeros((hi - S, c), F32))
    return parts[0] if len(parts) == 1 else jnp.concatenate(parts, 0)


def _conv_rows(ext, w, b, n):
    K = w.shape[0]
    acc = ext[_HALO:_HALO + n] * w[K - 1:K, :] + b
    for j in range(1, K):
        acc = acc + pltpu.roll(ext, j, 0)[_HALO:_HALO + n] * w[K - 1 - j:K - j, :]
    return acc


def _conv_rows_bwd(ext, w, dc, n):
    K, m = w.shape[0], dc.shape[0]
    d0 = dc[0:n]
    dx = d0 * w[K - 1:K, :]
    dws = [jnp.sum(d0 * ext[_HALO:_HALO + n], 0, keepdims=True)]
    for j in range(1, K):
        dx = dx + pltpu.roll(dc, m - j, 0)[0:n] * w[K - 1 - j:K - j, :]
        dws.append(jnp.sum(d0 * pltpu.roll(ext, j, 0)[_HALO:_HALO + n], 0, keepdims=True))
    return dx, dws[::-1], jnp.sum(d0, 0, keepdims=True)


def _col_spec(S, tc, off=0):
    return pl.BlockSpec((S, tc), lambda j: (0, j + off))


def _ssd_pre(proj, cw, cb, tc=256):
    S, n = proj.shape[0], SSD_CONV_DIM
    R = min(_CONV_ROWS, S)

    def body(x_ref, w_ref, b_ref, o_ref):
        wv, bv = w_ref[...], b_ref[...]
        for r0 in range(0, S, R):
            o_ref[r0:r0 + R, :] = _silu(_conv_rows(_rows(x_ref, r0 - _HALO, r0 + R), wv, bv, R))

    return pl.pallas_call(
        body, grid=(n // tc,),
        in_specs=[_col_spec(S, tc, OX // tc), pl.BlockSpec((4, tc), lambda j: (0, j)), pl.BlockSpec((1, tc), lambda j: (0, j))],
        out_specs=_col_spec(S, tc), out_shape=_sds((S, n)), name="ssd_pre", compiler_params=_PAR)(proj, cw, cb)


def _ssd_pre_bwd(proj, dxc, cw, cb, dproj, tc=256):
    S, n = proj.shape[0], SSD_CONV_DIM
    R = min(_CONV_ROWS, S)

    def body(x_ref, d_ref, w_ref, b_ref, dproj_in, dx_ref, dw_ref, db_ref):
        wv, bv = w_ref[...], b_ref[...]
        acc = [jnp.zeros((1, tc), F32)] * 5
        for r0 in range(0, S, R):
            ext = _rows(x_ref, r0 - _HALO, r0 + R + _HALO)
            dc = _rows(d_ref, r0, r0 + R + _HALO) * _dsilu(_conv_rows(ext, wv, bv, R + _HALO))
            dx, dws, db = _conv_rows_bwd(ext, wv, dc, R)
            dx_ref[r0:r0 + R, :] = dx.astype(BF16)
            acc = [s + d for s, d in zip(acc, dws + [db])]
        for k in range(4):
            dw_ref[k:k + 1, :] = acc[k]
        db_ref[...] = acc[4]

    wspec, bspec = pl.BlockSpec((4, tc), lambda j: (0, j)), pl.BlockSpec((1, tc), lambda j: (0, j))
    return pl.pallas_call(
        body, grid=(n // tc,), in_specs=[_col_spec(S, tc, OX // tc), _col_spec(S, tc), wspec, bspec, _ANY],
        out_specs=(_col_spec(S, tc, OX // tc), wspec, bspec), out_shape=(_sds(dproj.shape, BF16), _sds((4, n)), _sds((1, n))),
        input_output_aliases={4: 0}, name="ssd_pre_bwd", compiler_params=_PAR)(proj, dxc, cw, cb, dproj)


def _ffn_act(up, cw, cb, tc=256):
    S, nb = up.shape[0], FFN // tc
    R = min(_CONV_ROWS, S)

    def body(g_ref, v_ref, wg_ref, wv_ref, bg_ref, bv_ref, o_ref):
        wg, wv, bg, bv = wg_ref[...], wv_ref[...], bg_ref[...], bv_ref[...]
        for r0 in range(0, S, R):
            cg = _conv_rows(_rows(g_ref, r0 - _HALO, r0 + R), wg, bg, R)
            cv = _conv_rows(_rows(v_ref, r0 - _HALO, r0 + R), wv, bv, R)
            o_ref[r0:r0 + R, :] = (_silu(cg) * cv).astype(BF16)

    def wspec(off):
        return pl.BlockSpec((3, tc), lambda j: (0, j + off))

    def bspec(off):
        return pl.BlockSpec((1, tc), lambda j: (0, j + off))

    return pl.pallas_call(
        body, grid=(nb,), in_specs=[_col_spec(S, tc), _col_spec(S, tc, nb), wspec(0), wspec(nb), bspec(0), bspec(nb)],
        out_specs=_col_spec(S, tc), out_shape=_sds((S, FFN), BF16), name="ffn_act", compiler_params=_PAR)(up, up, cw, cw, cb, cb)


def _ffn_act_bwd(up, dact, cw, cb, tc=256):
    S, nb = up.shape[0], FFN // tc
    R = min(_CONV_ROWS, S)

    def body(g_ref, v_ref, d_ref, wg_ref, wv_ref, bg_ref, bv_ref, dx_ref, dw_ref, db_ref):
        wg, wv, bg, bv = wg_ref[...], wv_ref[...], bg_ref[...], bv_ref[...]
        acc = [[jnp.zeros((1, tc), F32)] * 4, [jnp.zeros((1, tc), F32)] * 4]
        for r0 in range(0, S, R):
            eg, ev = _rows(g_ref, r0 - _HALO, r0 + R + _HALO), _rows(v_ref, r0 - _HALO, r0 + R + _HALO)
            da = _rows(d_ref, r0, r0 + R + _HALO)
            cg, cv = _conv_rows(eg, wg, bg, R + _HALO), _conv_rows(ev, wv, bv, R + _HALO)
            sg = _sigmoid(cg)
            for half, (ext, w, dc) in enumerate(((eg, wg, da * cv * (sg * (1.0 + cg * (1.0 - sg)))), (ev, wv, da * (cg * sg)))):
                dx, dws, db = _conv_rows_bwd(ext, w, dc, R)
                dx_ref[half, r0:r0 + R, :] = dx.astype(BF16)
                acc[half] = [s + d for s, d in zip(acc[half], dws + [db])]
        for half in range(2):
            for k in range(3):
                dw_ref[half, k:k + 1, :] = acc[half][k]
            db_ref[half] = acc[half][3]

    def wspec(off):
        return pl.BlockSpec((3, tc), lambda j: (0, j + off))

    def bspec(off):
        return pl.BlockSpec((1, tc), lambda j: (0, j + off))

    cs = _col_spec(S, tc)
    both = lambda r: pl.BlockSpec((2, r, tc), lambda j: (0, 0, j))
    return pl.pallas_call(
        body, grid=(nb,), in_specs=[cs, _col_spec(S, tc, nb), cs, wspec(0), wspec(nb), bspec(0), bspec(nb)],
        out_specs=(both(S), both(3), both(1)), out_shape=(_sds((2, S, FFN), BF16), _sds((2, 3, FFN)), _sds((2, 1, FFN))),
        name="ffn_act_bwd", compiler_params=_PAR)(up, up, dact, cw, cw, cb, cb)


def _window_sum(x, w, up=False):
    shift = _shift_up if up else _shift_down
    j = 1
    while j < w:
        x = x + shift(x, j)
        j *= 2
    return x


def _pool_fwd(proj, pool_w, pool_scale):
    S = proj.shape[0]

    def body(u_ref, w_ref, s_ref, o_ref):
        cnt_row = (_iota((S, 128), 0) + 1).astype(F32)
        for g, w in enumerate(POOL_WINDOWS):
            sl = slice(g * 128, (g + 1) * 128)
            u = u_ref[:, sl]
            pooled = _window_sum(u, w) / jnp.minimum(cnt_row, float(w)) - u
            mixed = _dot(pooled.astype(BF16), w_ref[g].astype(BF16), _NN)
            o_ref[:, sl] = (mixed * s_ref[:, sl]).astype(BF16)

    return pl.pallas_call(
        body, grid=(1,),
        in_specs=[pl.BlockSpec((S, 512), lambda i: (0, OP // 512)), pl.BlockSpec((4, 128, 128), lambda i: (0, 0, 0)), _vec_spec(512)],
        out_specs=pl.BlockSpec((S, 512), lambda i: (0, 0)), out_shape=_sds((S, 512), BF16), name="pool_fwd",
        compiler_params=_ARB)(proj, pool_w, pool_scale)


def _pool_bwd(proj, dob, pool_w, pool_scale, dproj, after):
    S = proj.shape[0]

    def body(u_ref, d_ref, w_ref, s_ref, dproj_in, after_ref, du_ref, dw_ref, ds_ref):
        cnt_row = (_iota((S, 128), 0) + 1).astype(F32)
        for g, w in enumerate(POOL_WINDOWS):
            sl = slice(g * 128, (g + 1) * 128)
            u, dv, wv = u_ref[:, sl], d_ref[:, sl], w_ref[g].astype(BF16)
            cnt = jnp.minimum(cnt_row, float(w))
            pooled = (_window_sum(u, w) / cnt - u).astype(BF16)
            ds_ref[:, sl] = jnp.sum(dv * _dot(pooled, wv, _NN), 0, keepdims=True)
            dmix = (dv * s_ref[:, sl]).astype(BF16)
            dw_ref[g] = _dot(pooled, dmix, _TN)
            dp = _dot(dmix, wv, _NT)
            du_ref[:, sl] = (_window_sum(dp / cnt, w, up=True) - dp).astype(BF16)

    blk = pl.BlockSpec((S, 512), lambda i: (0, 0))
    wspec = pl.BlockSpec((4, 128, 128), lambda i: (0, 0, 0))
    return pl.pallas_call(
        body, grid=(1,), in_specs=[pl.BlockSpec((S, 512), lambda i: (0, OP // 512)), blk, wspec, _vec_spec(512), _ANY, _ANY],
        out_specs=(pl.BlockSpec((S, 512), lambda i: (0, OP // 512)), wspec, _vec_spec(512)),
        out_shape=(_sds(dproj.shape, BF16), _sds((4, 128, 128)), _sds((1, 512))), input_output_aliases={4: 0},
        name="pool_bwd", compiler_params=_ARB)(proj, dob, pool_w, pool_scale, dproj, after)


def _branch_specs():
    return [pl.BlockSpec((512, D), lambda i: (0, 0)), pl.BlockSpec((512, D), lambda i: (1, 0)), pl.BlockSpec((1024, D), lambda i: (1, 0))]


def _merge_fwd(oa, ob, oc, proj, wbr, tm=256):
    S = oa.shape[0]

    def body(oa_ref, ob_ref, oc_ref, gl_ref, wa_ref, wb_ref, wc_ref, o_ref):
        acc = _sigmoid(gl_ref[:, 0:D]) * _dot(oa_ref[...], wa_ref[...], _NN)
        acc += _sigmoid(gl_ref[:, D:2 * D]) * _dot(ob_ref[...], wb_ref[...], _NN)
        acc += _sigmoid(gl_ref[:, 2 * D:3 * D]) * _dot(oc_ref[...], wc_ref[...], _NN)
        o_ref[...] = acc.astype(BF16)

    return pl.pallas_call(
        body, grid=(S // tm,),
        in_specs=[_row_spec(tm, 512), _row_spec(tm, 512), _row_spec(tm, D), _row_spec(tm, 3 * D)] + _branch_specs(),
        out_specs=_row_spec(tm, D), out_shape=_sds((S, D), BF16), name="merge_fwd", compiler_params=_PAR)(oa, ob, oc, proj, wbr, wbr, wbr)


def _merge_bwd(dm, oa, ob, oc, proj, wbr, tm=256):
    S = oa.shape[0]

    def body(dm_ref, oa_ref, ob_ref, oc_ref, gl_ref, wa_ref, wb_ref, wc_ref, dya_ref, dyb_ref, dyc_ref, dgl_ref, doa_ref, dob_ref, doc_ref):
        dmv = dm_ref[...]
        for i, (o_ref, w_ref, dy_ref, do_ref) in enumerate(
                ((oa_ref, wa_ref, dya_ref, doa_ref), (ob_ref, wb_ref, dyb_ref, dob_ref), (oc_ref, wc_ref, dyc_ref, doc_ref))):
            gt = _sigmoid(gl_ref[:, i * D:(i + 1) * D])
            wv = w_ref[...]
            yv = _dot(o_ref[...], wv, _NN)
            dy = (dmv * gt).astype(BF16)
            dy_ref[...] = dy
            dgl_ref[:, i * D:(i + 1) * D] = (dmv * yv * gt * (1.0 - gt)).astype(BF16)
            do_ref[...] = _dot(dy, wv, _NT)

    rs = _row_spec
    return pl.pallas_call(
        body, grid=(S // tm,),
        in_specs=[rs(tm, D), rs(tm, 512), rs(tm, 512), rs(tm, D), rs(tm, 3 * D)] + _branch_specs(),
        out_specs=(rs(tm, D), rs(tm, D), rs(tm, D), rs(tm, 3 * D), rs(tm, 512), rs(tm, 512), rs(tm, D)),
        out_shape=(_sds((S, D), BF16), _sds((S, D), BF16), _sds((S, D), BF16), _sds((S, IN_PAD), BF16), _sds((S, 512)), _sds((S, 512)), _sds((S, D))),
        name="merge_bwd", compiler_params=_PAR)(dm, oa, ob, oc, proj, wbr, wbr, wbr)


def _rope_tab(posb, invf, tm=256):
    S = posb.shape[0]

    def body(p_ref, f_ref, c_ref, s1_ref, s2_ref):
        ang = p_ref[...] * f_ref[...]
        lane = _iota(ang.shape, 1)
        cs, sn = jnp.cos(ang), jnp.sin(ang)
        c_ref[...] = jnp.where(lane < NOPE, 1.0, cs)
        s1_ref[...] = jnp.where((lane >= 64) & (lane < 80), -sn, 0.0)
        s2_ref[...] = jnp.where((lane >= 80) & (lane < 96), sn, 0.0)

    rs = _row_spec(tm, 128)
    return pl.pallas_call(body, grid=(S // tm,), in_specs=[rs, _vec_spec(128)], out_specs=(rs, rs, rs),
                          out_shape=(_sds((S, 128)),) * 3, name="rope_tab", compiler_params=_PAR)(posb, invf)


def _rope(u, C, S1, S2):
    return u * C + pltpu.roll(u, 112, 1) * S1 + pltpu.roll(u, 16, 1) * S2


def _rope_t(dy, C, S1, S2):
    return dy * C + pltpu.roll(dy * S1, 16, 1) + pltpu.roll(dy * S2, 112, 1)


def _seg_sum(v, mask):
    return jnp.sum(jnp.where(mask, v, 0.0), -1, keepdims=True)


def _mla_latents(pq_ref, pkv_ref, wqb_ref, wkvb_ref, qan_ref, kvan_ref):
    ql, kvl = pq_ref[...], pkv_ref[...]
    ckv, kr = kvl[:, 0:KV_RANK], kvl[:, KV_RANK:KV_RANK + 128]
    rq = lax.rsqrt(jnp.mean(ql * ql, -1, keepdims=True) + EPS)
    rkv = lax.rsqrt(jnp.mean(ckv * ckv, -1, keepdims=True) + EPS)
    nq = (ql * rq * qan_ref[...]).astype(BF16)
    nkv = (ckv * rkv * kvan_ref[...]).astype(BF16)
    kv = jnp.concatenate([_dot(nkv, wkvb_ref[k], _NN) for k in range(4)], 1)
    return ql, ckv, kr, rq, rkv, nq, nkv, _dot(nq, wqb_ref[...], _NN), kv


def _mla_specs(tm):
    full = lambda r, n: pl.BlockSpec((r, n), lambda i: (0, 0))
    return ([pl.BlockSpec((tm, 384), lambda i: (i, OQ // 384)), pl.BlockSpec((tm, 384), lambda i: (i, OKV // 384))],
            [full(Q_RANK, D), pl.BlockSpec((4, KV_RANK, 256), lambda i: (0, 0, 0)), _vec_spec(Q_RANK), _vec_spec(KV_RANK), _vec_spec(128), _vec_spec(128)]
            + [_row_spec(tm, 128)] * 3)


def _mla_prep(proj, wqb, wkvb, qan, kvan, wq, wk, rope, tm=256):
    S = proj.shape[0]

    def body(pq_ref, pkv_ref, wqb_ref, wkvb_ref, qan_ref, kvan_ref, wq_ref, wk_ref, c_ref, s1_ref, s2_ref, qf_ref, kf_ref, ve_ref):
        _, _, kr, _, _, _, _, q, kv = _mla_latents(pq_ref, pkv_ref, wqb_ref, wkvb_ref, qan_ref, kvan_ref)
        C, S1, S2, wqv, wkv = c_ref[...], s1_ref[...], s2_ref[...], wq_ref[...], wk_ref[...]
        lane = _iota((tm, 128), 1)
        mn, mr = lane < 64, (lane >= 64) & (lane < 96)
        rrk = lax.rsqrt(_seg_sum(kr * kr, mr) / ROPE_DIM + EPS)
        ykr = _rope(jnp.where(mr, kr * rrk * wkv, 0.0), C, S1, S2)
        for h in range(N_HEADS):
            sl = slice(h * 128, (h + 1) * 128)
            t = q[:, sl]
            rn = lax.rsqrt(_seg_sum(t * t, mn) / NOPE + EPS)
            rr = lax.rsqrt(_seg_sum(t * t, mr) / ROPE_DIM + EPS)
            qf_ref[:, sl] = _rope(t * jnp.where(mn, rn, jnp.where(mr, rr, 0.0)) * wqv, C, S1, S2).astype(BF16)
            t = kv[:, sl]
            rn = lax.rsqrt(_seg_sum(t * t, mn) / NOPE + EPS)
            kf_ref[:, sl] = (jnp.where(mn, t * rn * wkv, 0.0) + ykr).astype(BF16)
            ve_ref[:, sl] = (jnp.where(mn, pltpu.roll(t, 64, 1), 0.0) if h % 2 == 0 else jnp.where(mn, 0.0, t)).astype(BF16)

    pspecs, wspecs = _mla_specs(tm)
    rs = _row_spec(tm, D)
    return pl.pallas_call(body, grid=(S // tm,), in_specs=pspecs + wspecs, out_specs=(rs, rs, rs),
                          out_shape=(_sds((S, D), BF16),) * 3, name="mla_prep", compiler_params=_PAR)(
        proj, proj, wqb, wkvb, qan, kvan, wq, wk, *rope)


def _mla_prep_bwd(proj, dqf, dkf, dve, wqb, wkvb, qan, kvan, wq, wk, rope, dproj, tm=256):
    S = proj.shape[0]

    def body(pq_ref, pkv_ref, wqb_ref, wkvb_ref, qan_ref, kvan_ref, wq_ref, wk_ref, c_ref, s1_ref, s2_ref,
             dqf_ref, dkf_ref, dve_ref, dproj_in, dlat_ref, dwqb_ref, dwkvb_ref, st_ref, dq_scr, dkv_scr):
        dqs_ref, dkvs_ref = dlat_ref.at[:, 0:384], dlat_ref.at[:, 384:768]
        @pl.when(pl.program_id(0) == 0)
        def _():
            dwqb_ref[...] = jnp.zeros_like(dwqb_ref)
            dwkvb_ref[...] = jnp.zeros_like(dwkvb_ref)
            st_ref[...] = jnp.zeros_like(st_ref)

        ql, ckv, kr, rq, rkv, nq, nkv, q, kv = _mla_latents(pq_ref, pkv_ref, wqb_ref, wkvb_ref, qan_ref, kvan_ref)
        C, S1, S2, wqv, wkv = c_ref[...], s1_ref[...], s2_ref[...], wq_ref[...], wk_ref[...]
        lane = _iota((tm, 128), 1)
        mn, mr = lane < 64, (lane >= 64) & (lane < 96)
        dwq = jnp.zeros((1, 128), F32)
        dwk = jnp.zeros((1, 128), F32)
        dykr = jnp.zeros((tm, 128), F32)
        for h in range(N_HEADS):
            sl = slice(h * 128, (h + 1) * 128)
            t = q[:, sl]
            rn = lax.rsqrt(_seg_sum(t * t, mn) / NOPE + EPS)
            rr = lax.rsqrt(_seg_sum(t * t, mr) / ROPE_DIM + EPS)
            scale = jnp.where(mn, rn, jnp.where(mr, rr, 0.0))
            that = t * scale
            du = _rope_t(dqf_ref[:, sl], C, S1, S2)
            dwq += jnp.sum(du * that, 0, keepdims=True)
            g = du * wqv
            gt = g * that
            dq_scr[:, sl] = scale * (g - that * jnp.where(mn, _seg_sum(gt, mn) / NOPE, _seg_sum(gt, mr) / ROPE_DIM))
            t = kv[:, sl]
            rn = lax.rsqrt(_seg_sum(t * t, mn) / NOPE + EPS)
            that = jnp.where(mn, t * rn, 0.0)
            dkf = dkf_ref[:, sl]
            dykr += jnp.where(mr, dkf, 0.0)
            dkn = jnp.where(mn, dkf, 0.0)
            dwk += jnp.sum(dkn * that, 0, keepdims=True)
            g = dkn * wkv
            dve = dve_ref[:, sl]
            dkv_scr[:, sl] = jnp.where(mn, rn * (g - that * (jnp.sum(g * that, -1, keepdims=True) / NOPE)),
                                       pltpu.roll(dve, 64, 1) if h % 2 == 0 else dve)
        rrk = lax.rsqrt(_seg_sum(kr * kr, mr) / ROPE_DIM + EPS)
        that = jnp.where(mr, kr * rrk, 0.0)
        dukr = jnp.where(mr, _rope_t(dykr, C, S1, S2), 0.0)
        dwk += jnp.sum(dukr * that, 0, keepdims=True)
        g = dukr * wkv
        dkr = rrk * (g - that * (jnp.sum(g * that, -1, keepdims=True) / ROPE_DIM))
        dqv, dkvv = dq_scr[...].astype(BF16), dkv_scr[...].astype(BF16)
        dnq = _dot(dqv, wqb_ref[...], _NT)
        dwqb_ref[...] += _dot(nq, dqv, _TN)
        dnkv = jnp.zeros((tm, KV_RANK), F32)
        for k in range(4):
            dnkv += _dot(dkvv[:, k * 256:(k + 1) * 256], wkvb_ref[k], _NT)
            dwkvb_ref[k] += _dot(nkv, dkvv[:, k * 256:(k + 1) * 256], _TN)
        xhat = ql * rq
        st_ref[0:1, 0:Q_RANK] += jnp.sum(dnq * xhat, 0, keepdims=True)
        g = dnq * qan_ref[...]
        dqs_ref[...] = (rq * (g - xhat * jnp.mean(g * xhat, -1, keepdims=True))).astype(BF16)
        xhat = ckv * rkv
        st_ref[1:2, 0:KV_RANK] += jnp.sum(dnkv * xhat, 0, keepdims=True)
        g = dnkv * kvan_ref[...]
        dkvs_ref[:, 0:KV_RANK] = (rkv * (g - xhat * jnp.mean(g * xhat, -1, keepdims=True))).astype(BF16)
        dkvs_ref[:, KV_RANK:KV_RANK + 128] = dkr.astype(BF16)
        st_ref[2:3, 0:128] += dwq
        st_ref[3:4, 0:128] += dwk

    pspecs, wspecs = _mla_specs(tm)
    rs = _row_spec(tm, D)
    full = lambda r, n: pl.BlockSpec((r, n), lambda i: (0, 0))
    return pl.pallas_call(
        body, grid=(S // tm,), in_specs=pspecs + wspecs + [rs, rs, rs, _ANY],
        out_specs=(pl.BlockSpec((tm, 768), lambda i: (i, OQ // 768)), full(Q_RANK, D), pl.BlockSpec((4, KV_RANK, 256), lambda i: (0, 0, 0)), full(8, D)),
        out_shape=(_sds(dproj.shape, BF16), _sds((Q_RANK, D)), _sds((4, KV_RANK, 256)), _sds((8, D))), input_output_aliases={14: 0},
        scratch_shapes=[pltpu.VMEM((tm, D), F32), pltpu.VMEM((tm, D), F32)], name="mla_prep_bwd", compiler_params=_ARB)(
        proj, proj, wqb, wkvb, qan, kvan, wq, wk, *rope, dqf, dkf, dve, dproj)


_ATT_SCALE = (NOPE + ROPE_DIM) ** -0.5


def _att_probs(q, k, i, tq):
    n = k.shape[0]
    s = _dot(q, k, _NT) * _ATT_SCALE
    tri = _iota((tq, tq), 1) <= _iota((tq, tq), 0)
    diag = jnp.where(tri, s[:, n - tq:], -1e30)
    s = diag if n == tq else jnp.concatenate([s[:, :n - tq], diag], 1)
    p = jnp.exp(s - jnp.max(s, -1, keepdims=True))
    return p * (1.0 / jnp.sum(p, -1, keepdims=True))


def _attn_fwd(qf, kf, ve, tq=512):
    S = qf.shape[0]
    tq = min(tq, S)

    def body(q_ref, k_ref, v_ref, o_ref):
        for i in range(S // tq):
            n, rows = (i + 1) * tq, slice(i * tq, (i + 1) * tq)
            acc = jnp.zeros((tq, 128), F32)
            for hh in range(2):
                sl = slice(hh * 128, (hh + 1) * 128)
                p = _att_probs(q_ref[rows, sl], k_ref[0:n, sl], i, tq)
                acc += _dot(p.astype(BF16), v_ref[0:n, sl], _NN)
            o_ref[rows, :] = acc.astype(BF16)

    ps = pl.BlockSpec((S, 256), lambda h: (0, h))
    return pl.pallas_call(body, grid=(N_HEADS // 2,), in_specs=[ps, ps, ps], out_specs=pl.BlockSpec((S, 128), lambda h: (0, h)),
                          out_shape=_sds((S, 512), BF16), name="attn_fwd", compiler_params=_PAR)(qf, kf, ve)


def _attn_bwd(qf, kf, ve, do, after, tq=512):
    S = qf.shape[0]
    tq = min(tq, S)

    def body(q_ref, k_ref, v_ref, do_ref, after_ref, dq_ref, dk_ref, dv_ref):
        dk_ref[...] = jnp.zeros_like(dk_ref)
        dv_ref[...] = jnp.zeros_like(dv_ref)
        for i in range(S // tq):
            n, rows = (i + 1) * tq, slice(i * tq, (i + 1) * tq)
            dob = do_ref[rows, :].astype(BF16)
            for hh in range(2):
                sl = slice(hh * 128, (hh + 1) * 128)
                q, k = q_ref[rows, sl], k_ref[0:n, sl]
                p = _att_probs(q, k, i, tq)
                dv_ref[0:n, sl] += _dot(p.astype(BF16), dob, _TN)
                dp = _dot(dob, v_ref[0:n, sl], _NT)
                ds = (p * (dp - jnp.sum(dp * p, -1, keepdims=True)) * _ATT_SCALE).astype(BF16)
                dq_ref[rows, sl] = _dot(ds, k, _NN)
                dk_ref[0:n, sl] += _dot(ds, q, _TN)

    ps = pl.BlockSpec((S, 256), lambda h: (0, h))
    return pl.pallas_call(body, grid=(N_HEADS // 2,), in_specs=[ps, ps, ps, pl.BlockSpec((S, 128), lambda h: (0, h)), _ANY],
                          out_specs=(ps, ps, ps), out_shape=(_sds((S, D)),) * 3, name="attn_bwd", compiler_params=_PAR)(qf, kf, ve, do, after)


def _softplus(x):
    return jnp.maximum(x, 0.0) + jnp.log1p(jnp.exp(-jnp.abs(x)))


def _ssd_chunk(xc_ref, dtr_ref, dtb_ref, al_ref, e_ref):
    L = SSD_L
    a = -jnp.exp(al_ref[...])
    dtp = _softplus(dtr_ref[...] + dtb_ref[...])
    causal = _iota((L, L), 1) <= _iota((L, L), 0)
    cs = _dot_hi(causal.astype(F32), dtp * a)
    E = e_ref[...]
    dtx, csx = _dot_hi(dtp, E), _dot_hi(cs, E)
    X = xc_ref[:, 0:SSD_INNER]
    Xd = X * dtx
    dec_out = jnp.exp(csx)
    dec_st = jnp.exp(csx[L - 1:L, :] - csx)
    return a, dtp, causal, cs, cs.T, dtx, X, Xd, dec_out, dec_st


def _ssd_decay(causal, cs, cs_row, h):
    diff = cs[:, h:h + 1] - cs_row[h:h + 1, :]
    return jnp.where(causal, jnp.exp(jnp.where(causal, diff, 0.0)), 0.0)


def _ssd_groups(xc_ref, g):
    b0, c0 = SSD_INNER + g * SSD_N, SSD_INNER + 2 * SSD_N + g * SSD_N
    return xc_ref[:, b0:b0 + SSD_N].astype(BF16), xc_ref[:, c0:c0 + SSD_N].astype(BF16)


def _pair_decay(cs, pair):
    L = SSD_L
    return jnp.where(_iota((128, 128), 0) < 64, jnp.exp(cs[L - 1:L, 2 * pair:2 * pair + 1]), jnp.exp(cs[L - 1:L, 2 * pair + 1:2 * pair + 2]))


def _ssd_in_specs(nc, rev):
    idx = (lambda c: nc - 1 - c) if rev else (lambda c: c)
    return [pl.BlockSpec((SSD_L, SSD_CONV_DIM), lambda c: (idx(c), 0)), pl.BlockSpec((SSD_L, 128), lambda c: (idx(c), ODT // 128)),
            _vec_spec(128), _vec_spec(128), _vec_spec(SSD_INNER), pl.BlockSpec((128, SSD_INNER), lambda c: (0, 0))]


def _ssd_core(xc, proj, dtb, alog, dskip, E):
    S = xc.shape[0]
    nc = S // SSD_L

    def body(xc_ref, dtr_ref, dtb_ref, al_ref, dx_ref, e_ref, y_ref, hp_ref, h_scr):
        @pl.when(pl.program_id(0) == 0)
        def _():
            h_scr[...] = jnp.zeros_like(h_scr)

        hp_ref[0] = h_scr[...]
        _, _, causal, cs, cs_row, _, X, Xd, dec_out, dec_st = _ssd_chunk(xc_ref, dtr_ref, dtb_ref, al_ref, e_ref)
        Xs = Xd * dec_st
        lane = _iota((SSD_L, 128), 1)
        for g in range(2):
            Bg, Cg = _ssd_groups(xc_ref, g)
            CB = _dot(Cg, Bg, _NT)
            for pr in range(4):
                pair = g * 4 + pr
                psl = slice(pair * 128, (pair + 1) * 128)
                Xdp = Xd[:, psl].astype(BF16)
                r0 = _dot((CB * _ssd_decay(causal, cs, cs_row, 2 * pair)).astype(BF16), Xdp, _NN)
                r1 = _dot((CB * _ssd_decay(causal, cs, cs_row, 2 * pair + 1)).astype(BF16), Xdp, _NN)
                Hp = h_scr[psl, :]
                W = _dot(Cg, Hp.astype(BF16), _NT)
                y_ref[:, psl] = jnp.where(lane < 64, r0, r1) + W * dec_out[:, psl] + X[:, psl] * dx_ref[:, psl]
                h_scr[psl, :] = Hp * _pair_decay(cs, pair) + _dot(Xs[:, psl].astype(BF16), Bg, _TN)

    return pl.pallas_call(
        body, grid=(nc,), in_specs=_ssd_in_specs(nc, False),
        out_specs=(pl.BlockSpec((SSD_L, SSD_INNER), lambda c: (c, 0)), pl.BlockSpec((1, SSD_INNER, SSD_N), lambda c: (c, 0, 0))),
        out_shape=(_sds((S, SSD_INNER)), _sds((nc, SSD_INNER, SSD_N))), scratch_shapes=[pltpu.VMEM((SSD_INNER, SSD_N), F32)],
        name="ssd_core", compiler_params=_ARB)(xc, proj, dtb, alog, dskip, E)


def _ssd_core_bwd(xc, proj, hprev, dy, dtb, alog, dskip, E, ET, dproj):
    S = xc.shape[0]
    nc = S // SSD_L
    L = SSD_L

    def body(xc_ref, dtr_ref, dtb_ref, al_ref, dx_ref, e_ref, et_ref, hp_ref, dy_ref, dproj_in, dxc_ref, ddt_ref, st_ref, dh_scr, acc_scr):
        step = pl.program_id(0)

        @pl.when(step == 0)
        def _():
            dh_scr[...] = jnp.zeros_like(dh_scr)
            acc_scr[...] = jnp.zeros_like(acc_scr)
            st_ref[...] = jnp.zeros_like(st_ref)

        a, dtp, causal, cs, cs_row, dtx, X, Xd, dec_out, dec_st = _ssd_chunk(xc_ref, dtr_ref, dtb_ref, al_ref, e_ref)
        lane = _iota((L, 128), 1)
        sub = _iota((128, L), 0)
        dcs_col = jnp.zeros((L, 128), F32)
        dcs_row = jnp.zeros((128, L), F32)
        dcs_last = jnp.zeros((1, 128), F32)
        dcsx, ddtx, dlastx = [], [], []
        for g in range(2):
            Bg, Cg = _ssd_groups(xc_ref, g)
            CB = _dot(Cg, Bg, _NT)
            dCB = jnp.zeros((L, L), F32)
            dB = jnp.zeros((L, SSD_N), F32)
            dC = jnp.zeros((L, SSD_N), F32)
            for pr in range(4):
                pair = g * 4 + pr
                psl = slice(pair * 128, (pair + 1) * 128)
                dY, Xp, Xdp, dop, dsp = dy_ref[:, psl], X[:, psl], Xd[:, psl], dec_out[:, psl], dec_st[:, psl]
                Xdb = Xdp.astype(BF16)
                acc_scr[0:1, psl] += jnp.sum(dY * Xp, 0, keepdims=True)
                Hp = hp_ref[0, psl, :]
                Hb = Hp.astype(BF16)
                dW = (dY * dop).astype(BF16)
                dcx = dY * _dot(Cg, Hb, _NT) * dop
                dC += _dot(dW, Hb, _NN)
                dHp = _dot(dW, Cg, _TN)
                dHn = dh_scr[psl, :]
                cd = _pair_decay(cs, pair)
                dh_scr[psl, :] = dHp + dHn * cd
                rsum = jnp.sum(dHn * Hp * cd, -1, keepdims=True)
                half = _iota((128, 1), 0) < 64
                s0 = jnp.sum(jnp.where(half, rsum, 0.0), 0, keepdims=True)
                s1 = jnp.sum(jnp.where(half, 0.0, rsum), 0, keepdims=True)
                lane1 = _iota((1, 128), 1)
                dcs_last += jnp.where(lane1 == 2 * pair, s0, 0.0) + jnp.where(lane1 == 2 * pair + 1, s1, 0.0)
                dHb = dHn.astype(BF16)
                dXs = _dot(Bg, dHb, _NT)
                dB += _dot((Xdp * dsp).astype(BF16), dHb, _NN)
                dXd = dXs * dsp
                e_st = dXs * Xdp * dsp
                dcx -= e_st
                dlastx.append(jnp.sum(e_st, 0, keepdims=True))
                for i in range(2):
                    h = 2 * pair + i
                    Dm = _ssd_decay(causal, cs, cs_row, h)
                    M = CB * Dm
                    dYm = jnp.where((lane < 64) if i == 0 else (lane >= 64), dY, 0.0).astype(BF16)
                    dM = _dot(dYm, Xdb, _NT)
                    dXd += _dot(M.astype(BF16), dYm, _TN)
                    dCB += dM * Dm
                    Em = dM * M
                    dcs_col += jnp.where(lane == h, jnp.sum(Em, -1, keepdims=True), 0.0)
                    dcs_row += jnp.where(sub == h, jnp.sum(Em, 0, keepdims=True), 0.0)
                dxc_ref[:, psl] = dY * dx_ref[:, psl] + dXd * dtx[:, psl]
                ddtx.append(dXd * Xp)
                dcsx.append(dcx)
            dCBb = dCB.astype(BF16)
            b0, c0 = SSD_INNER + g * SSD_N, SSD_INNER + 2 * SSD_N + g * SSD_N
            dxc_ref[:, b0:b0 + SSD_N] = dB + _dot(dCBb, Cg, _TN)
            dxc_ref[:, c0:c0 + SSD_N] = dC + _dot(dCBb, Bg, _NN)
        ET = et_ref[...]
        dcs = dcs_col - dcs_row.T + _dot_hi(jnp.concatenate(dcsx, 1), ET)
        dlast = dcs_last + _dot_hi(jnp.broadcast_to(jnp.concatenate(dlastx, 1), (8, SSD_INNER)), ET)[0:1, :]
        dcs += jnp.where(_iota((L, 128), 0) == L - 1, dlast, 0.0)
        dda = _dot_hi((_iota((L, L), 1) >= _iota((L, L), 0)).astype(F32), dcs)
        ddtp = dda * a + _dot_hi(jnp.concatenate(ddtx, 1), ET)
        draw = ddtp * _sigmoid(dtr_ref[...] + dtb_ref[...])
        ddt_ref[...] = draw.astype(BF16)
        st_ref[0:1, :] += jnp.sum(draw, 0, keepdims=True)
        st_ref[1:2, :] += jnp.sum(dda * dtp, 0, keepdims=True) * a

        @pl.when(step == nc - 1)
        def _():
            st_ref[2:3, :] = _dot_hi(acc_scr[...], ET)[0:1, :]

    rev = lambda c: (nc - 1 - c, 0)
    return pl.pallas_call(
        body, grid=(nc,),
        in_specs=_ssd_in_specs(nc, True) + [pl.BlockSpec((SSD_INNER, 128), lambda c: (0, 0)),
                                            pl.BlockSpec((1, SSD_INNER, SSD_N), lambda c: (nc - 1 - c, 0, 0)),
                                            pl.BlockSpec((L, SSD_INNER), rev), _ANY],
        out_specs=(pl.BlockSpec((L, SSD_CONV_DIM), rev), pl.BlockSpec((L, 128), lambda c: (nc - 1 - c, ODT // 128)),
                   pl.BlockSpec((8, 128), lambda c: (0, 0))),
        out_shape=(_sds((S, SSD_CONV_DIM)), _sds(dproj.shape, BF16), _sds((8, 128))), input_output_aliases={9: 1},
        scratch_shapes=[pltpu.VMEM((SSD_INNER, SSD_N), F32), pltpu.VMEM((8, SSD_INNER), F32)],
        name="ssd_core_bwd", compiler_params=_ARB)(xc, proj, dtb, alog, dskip, E, ET, hprev, dy, dproj)


def _ssd_post(y, proj, nw, tm=256):
    S = y.shape[0]

    def body(y_ref, z_ref, nw_ref, o_ref):
        for g in range(2):
            sl = slice(g * 512, (g + 1) * 512)
            gated = y_ref[:, sl] * _silu(z_ref[:, sl])
            r = lax.rsqrt(jnp.mean(gated * gated, -1, keepdims=True) + EPS)
            o_ref[:, sl] = (gated * r * nw_ref[:, sl]).astype(BF16)

    return pl.pallas_call(
        body, grid=(S // tm,), in_specs=[_row_spec(tm, D), pl.BlockSpec((tm, D), lambda i: (i, OZ // D)), _vec_spec(D)],
        out_specs=_row_spec(tm, D), out_shape=_sds((S, D), BF16), name="ssd_post", compiler_params=_PAR)(y, proj, nw)


def _ssd_post_bwd(doc, y, proj, nw, dproj, tm=256):
    S = y.shape[0]

    def body(d_ref, y_ref, z_ref, nw_ref, dproj_in, dy_ref, dz_ref, st_ref):
        @pl.when(pl.program_id(0) == 0)
        def _():
            st_ref[...] = jnp.zeros_like(st_ref)

        for g in range(2):
            sl = slice(g * 512, (g + 1) * 512)
            yv, zv, dv = y_ref[:, sl], z_ref[:, sl], d_ref[:, sl]
            sz = _silu(zv)
            gated = yv * sz
            r = lax.rsqrt(jnp.mean(gated * gated, -1, keepdims=True) + EPS)
            ghat = gated * r
            st_ref[0:1, sl] += jnp.sum(dv * ghat, 0, keepdims=True)
            gg = dv * nw_ref[:, sl]
            dg = r * (gg - ghat * jnp.mean(gg * ghat, -1, keepdims=True))
            dy_ref[:, sl] = dg * sz
            dz_ref[:, sl] = (dg * yv * _dsilu(zv)).astype(BF16)

    zs = pl.BlockSpec((tm, D), lambda i: (i, OZ // D))
    return pl.pallas_call(
        body, grid=(S // tm,), in_specs=[_row_spec(tm, D), _row_spec(tm, D), zs, _vec_spec(D), _ANY],
        out_specs=(_row_spec(tm, D), zs, _vec_spec(D, 8)), out_shape=(_sds((S, D)), _sds(dproj.shape, BF16), _sds((8, D))),
        input_output_aliases={4: 1}, name="ssd_post_bwd", compiler_params=_ARB)(doc, y, proj, nw, dproj)


def _row(v, n=None):
    v = v.astype(F32).reshape(1, -1)
    return v if n is None else jnp.pad(v, ((0, 0), (0, n - v.shape[1])))


_IN_SEGMENTS = [(0, 384, OQ), (384, 640, OKV), (640, 672, OKV + 320), (672, 1184, OP), (1184, 2208, OZ), (2208, 3744, OX),
                (3744, 3760, ODT), (3760, IN_DIM, OG)]
_IN_ZEROS = [(OKV + 256, OKV + 320), (OKV + 352, OKV + 384), (ODT + 16, ODT + 128)]


def _in_pieces():
    w, out = IN_DIM // 4, []
    for a, b, d in _IN_SEGMENTS:
        while a < b:
            k = a // w
            e = min(b, (k + 1) * w)
            out.append((k, a - k * w, e - k * w, d))
            d, a = d + e - a, e
    return out


def _win_layout(w_in4, tm=256):
    def body(w_ref, o_ref):
        for k, s0, s1, d in _in_pieces():
            o_ref[:, d:d + s1 - s0] = w_ref[k, :, s0:s1]
        for z0, z1 in _IN_ZEROS:
            o_ref[:, z0:z1] = jnp.zeros((tm, z1 - z0), o_ref.dtype)

    return pl.pallas_call(
        body, grid=(D // tm,), in_specs=[pl.BlockSpec((4, tm, IN_DIM // 4), lambda i: (0, i, 0))],
        out_specs=pl.BlockSpec((tm, IN_PAD), lambda i: (i, 0)), out_shape=_sds((D, IN_PAD), w_in4.dtype), name="win_layout",
        compiler_params=_PAR)(w_in4)


def _win_unlayout(dwin, tm=256):
    def body(d_ref, o_ref):
        for k, s0, s1, d in _in_pieces():
            o_ref[k, :, s0:s1] = d_ref[:, d:d + s1 - s0]

    return pl.pallas_call(
        body, grid=(D // tm,), in_specs=[pl.BlockSpec((tm, IN_PAD), lambda i: (i, 0))],
        out_specs=pl.BlockSpec((4, tm, IN_DIM // 4), lambda i: (0, i, 0)), out_shape=_sds((4, D, IN_DIM // 4), dwin.dtype),
        name="win_unlayout", compiler_params=_PAR)(dwin)


def _prep_late(p):
    return dict(wbr=p["w_branch"].astype(BF16), wo=p["w_out"].astype(BF16), wup4=p["ffn_up4"].astype(BF16), wdn=p["ffn_down"].astype(BF16))


def _prep_layer(p, late=None):
    win = _win_layout(p["w_in4"].astype(BF16))
    wqb = jnp.transpose(p["w_q_b4"].astype(BF16).reshape(4, Q_RANK, 2, 96), (1, 0, 2, 3))
    return dict(
        win=win, wqb=jnp.pad(wqb, ((0, 0), (0, 0), (0, 0), (0, 32))).reshape(Q_RANK, D), wkvb4=p["w_kv_b4"].astype(BF16),
        late=late if late is not None else (lambda after: _prep_late(p)),
        nw1=_row(p["norm1_w"]), nw2=_row(p["norm2_w"]), qan=_row(p["q_a_norm"]), kvan=_row(p["kv_a_norm"]),
        wq=_row(p["q_norm"], 128), wk=_row(p["k_norm"], 128), pool_w=p["pool_w"].astype(F32), pool_scale=_row(p["pool_scale"]),
        cw=p["ssd_conv_w"].astype(F32), cb=_row(p["ssd_conv_b"]), dtb=_row(p["ssd_dt_bias"], 128), alog=_row(p["ssd_a_log"], 128),
        dskip=_row(jnp.repeat(p["ssd_d"].astype(F32), SSD_P)), snw=_row(p["ssd_norm_w"]),
        fcw=p["ffn_conv_w"].astype(F32), fcb=_row(p["ffn_conv_b"]))


def _layer_fwd(x, mod8, W, rope, E, tag):
    sh1, sc1, g1, sh2, sc2, g2 = (mod8[i:i + 1] for i in range(6))
    h1 = _ln_mod(x, W["nw1"], sc1, sh1, name=f"ln1_{tag}")
    proj = _mm(h1, W["win"], tn=640, tk=1024, name=f"proj_{tag}")
    qf, kf, ve = _mla_prep(proj, W["wqb"], W["wkvb4"], W["qan"], W["kvan"], W["wq"], W["wk"], rope)
    oa = _attn_fwd(qf, kf, ve)
    ob = _pool_fwd(proj, W["pool_w"], W["pool_scale"])
    xc = _ssd_pre(proj, W["cw"], W["cb"])
    y, hprev = _ssd_core(xc, proj, W["dtb"], W["alog"], W["dskip"], E)
    oc = _ssd_post(y, proj, W["snw"])
    W.update(W["late"](oc))
    merged = _merge_fwd(oa, ob, oc, proj, W["wbr"])
    x1, out1 = _mm(merged, W["wo"], tk=1024, res=x, gate=g1, name=f"wout_{tag}")
    h2 = _ln_mod(x1, W["nw2"], sc2, sh2, name=f"ln2_{tag}")
    up = _up_fwd(h2, W["wup4"], name=f"up_{tag}")
    act = _ffn_act(up, W["fcw"], W["fcb"])
    x2, out2 = _mm(act, W["wdn"], tm=1024, res=x1, gate=g2, name=f"down_{tag}")
    saved = dict(x=x, h1=h1, proj=proj, qf=qf, kf=kf, ve=ve, oa=oa, ob=ob, oc=oc, xc=xc, hprev=hprev, y=y, merged=merged,
                 out1=out1, x1=x1, h2=h2, up=up, act=act, out2=out2)
    return x2, saved


def _layer_bwd(dx2, sv, mod8, W, rope, E, ET, tag, emit=None, mid=None):
    sc1, g1, sc2, g2 = mod8[1:2], mod8[2:3], mod8[4:5], mod8[5:6]
    proj = sv["proj"]
    dz2, dg2 = _gate_bwd(dx2, sv["out2"], g2, name=f"gate2_bwd_{tag}")
    dact = _mm(dz2, W["wdn"], "nt", tn=1408, tk=1024, name=f"down_dx_{tag}")
    dwdn = _mm(sv["act"], dz2, "tn", tm=1408, name=f"down_dw_{tag}")
    dup2, dfcw, dfcb = _ffn_act_bwd(sv["up"], dact, W["fcw"], W["fcb"])
    dh2 = _up_dx(dup2, W["wup4"], name=f"up_dx_{tag}")
    dwup4 = _up_dw(sv["h2"], dup2, name=f"up_dw_{tag}")
    dx1, st2 = _ln_mod_bwd(sv["x1"], dh2, dx2, W["nw2"], sc2, name=f"ln2_bwd_{tag}")
    dz1, dg1 = _gate_bwd(dx1, sv["out1"], g1, name=f"gate1_bwd_{tag}")
    dmerged = _mm(dz1, W["wo"], "nt", tk=1024, name=f"wout_dx_{tag}")
    dwo = _mm(sv["merged"], dz1, "tn", name=f"wout_dw_{tag}")
    dya, dyb, dyc, dproj, doa, dob, doc = _merge_bwd(dmerged, sv["oa"], sv["ob"], sv["oc"], proj, W["wbr"])
    dwbr = _branch_dw(sv["oa"], dya, 0, None, f"wba_dw_{tag}")
    dwbr = _branch_dw(sv["ob"], dyb, 512, dwbr, f"wbb_dw_{tag}")
    dwbr = _branch_dw(sv["oc"], dyc, 1024, dwbr, f"wbc_dw_{tag}")
    late = dict(w_branch=dwbr.reshape(4, 512, D), w_out=dwo.reshape(4, 256, D), ffn_up=dwup4, ffn_down=dwdn.reshape(4, FFN // 4, D))
    snw, tie = W["snw"], jnp.zeros((8, 128), F32)
    if emit is not None:
        token = emit(late)
        if token is not None:
            snw, tie = snw + token, tie + token
    dy, dproj, st_post = _ssd_post_bwd(doc, sv["y"], proj, snw, dproj)
    dxc, dproj, st_ssd = _ssd_core_bwd(sv["xc"], proj, sv["hprev"], dy, W["dtb"], W["alog"], W["dskip"], E, ET, dproj)
    dproj, dcw, dcb = _ssd_pre_bwd(proj, dxc, W["cw"], W["cb"], dproj)
    if mid is not None:
        token = mid(dcb)
        if token is not None:
            tie = tie + token
    dproj, dpw, dps = _pool_bwd(proj, dob, W["pool_w"], W["pool_scale"], dproj, tie)
    dqf, dkf, dve = _attn_bwd(sv["qf"], sv["kf"], sv["ve"], doa, tie)
    dproj, dwqb, dwkvb4, st_mla = _mla_prep_bwd(proj, dqf, dkf, dve, W["wqb"], W["wkvb4"], W["qan"], W["kvan"], W["wq"], W["wk"], rope, dproj)
    dh1 = _mm(dproj, W["win"], "nt", tk=1408, name=f"proj_dx_{tag}")
    dwin = _mm(sv["h1"], dproj, "tn", tn=640, name=f"proj_dw_{tag}")
    dx, st1 = _ln_mod_bwd(sv["x"], dh1, dx1, W["nw1"], sc1, name=f"ln1_bwd_{tag}")
    grads = dict(
        norm1_w=st1[2], norm2_w=st2[2], w_in=_win_unlayout(dwin),
        q_a_norm=st_mla[0, :Q_RANK], kv_a_norm=st_mla[1, :KV_RANK], q_norm=st_mla[2, :96], k_norm=st_mla[3, :96],
        w_q_b=jnp.transpose(dwqb.reshape(Q_RANK, 4, 2, 128)[:, :, :, :96], (1, 0, 2, 3)).reshape(4, Q_RANK, 192), w_kv_b=dwkvb4,
        pool_w=dpw, pool_scale=dps[0], ssd_conv_w=dcw, ssd_conv_b=dcb[0],
        ssd_dt_bias=st_ssd[0, :SSD_HEADS], ssd_a_log=st_ssd[1, :SSD_HEADS], ssd_d=st_ssd[2, :SSD_HEADS], ssd_norm_w=st_post[0],
        ffn_conv_w=jnp.transpose(dfcw, (1, 0, 2)).reshape(3, 2 * FFN), ffn_conv_b=dfcb.reshape(2 * FFN), **late)
    dmod = jnp.concatenate([st1[0:2], dg1[0:1], st2[0:2], dg2[0:1]], 0)
    return dx, grads, dmod


def _ssd_expand():
    E = (jnp.arange(SSD_INNER)[None, :] // SSD_P == jnp.arange(128)[:, None]).astype(F32)
    return E, E.T


def _rope_tables(positions):
    inv_freq = ROPE_THETA ** (-jnp.arange(0, ROPE_DIM, 2, dtype=F32) / ROPE_DIM)
    invf = jnp.concatenate([jnp.zeros((NOPE,), F32), inv_freq, inv_freq, jnp.zeros((32,), F32)]).reshape(1, 128)
    posb = jnp.broadcast_to(positions.astype(F32)[:, None], (positions.shape[0], 128))
    return _rope_tab(posb, invf)


def _local_step(x, target, positions, mods, get_layer, bwd_mod=None, emit=None, mid=None, done=None):
    rope = _rope_tables(positions)
    E, ET = _ssd_expand()
    Ws, saved, h = [], [], x
    for l in range(2):
        Ws.append(_prep_layer(*get_layer(l, h)))
        h, sv = _layer_fwd(h, mods[l], Ws[l], rope, E, l)
        saved.append(sv)
    dy, lpart = _loss_grad(h, target)
    grads, dmods = [None, None], [None, None]
    for l in (1, 0):
        mod8 = mods[l] if bwd_mod is None else bwd_mod(l)
        hook = lambda f: None if f is None else functools.partial(f, l)
        dy, grads[l], dmods[l] = _layer_bwd(dy, saved[l], mod8, Ws[l], rope, E, ET, l, hook(emit), hook(mid))
        if done is not None:
            done(l, grads[l])
    return lpart[0, 0], dy, grads, dmods


_ANY = pl.BlockSpec(memory_space=pl.ANY)
_VMEM = pl.BlockSpec(memory_space=pltpu.VMEM)


def _place():
    x, y, c = lax.axis_index("x"), lax.axis_index("y"), lax.axis_index("c")
    return x, y, c, [(1 - x, y), (x, 1 - y), (1 - x, 1 - y)]


def _allgather8(v, name):
    m_per, n = v.shape

    def body(x_ref, out_ref, send_sems, recv_sems, local_sem):
        x, y, c, chips = _place()
        me, sibling = (x, y, c), (x, y, 1 - c)

        def rows(px, py, pc):
            return out_ref.at[pl.ds((4 * px + 2 * py + pc) * m_per, m_per), :]

        def copy(k, block, to, src=None):
            return pltpu.make_async_remote_copy(src_ref=rows(*block) if src is None else src, dst_ref=rows(*block),
                                                send_sem=send_sems.at[k], recv_sem=recv_sems.at[k], device_id=to, device_id_type=MESH)

        mine = pltpu.make_async_copy(x_ref, rows(*me), local_sem)
        mine.start()
        first = [copy(0, me, sibling, src=x_ref)] + [copy(1 + j, me, (*chip, c), src=x_ref) for j, chip in enumerate(chips)]
        for cp in first:
            cp.start()
        passed = [copy(4 + j, (*chip, c), sibling) for j, chip in enumerate(chips)]
        for j, chip in enumerate(chips):
            copy(1 + j, (*chip, c), me).wait_recv()
            passed[j].start()
        copy(0, sibling, me).wait_recv()
        for j, chip in enumerate(chips):
            copy(4 + j, (*chip, 1 - c), me).wait_recv()
        for cp in first + passed:
            cp.wait_send()
        mine.wait()

    return pl.pallas_call(
        body, out_shape=_sds((8 * m_per, n), v.dtype), in_specs=[_VMEM], out_specs=_VMEM,
        scratch_shapes=[pltpu.SemaphoreType.DMA((7,)), pltpu.SemaphoreType.DMA((7,)), pltpu.SemaphoreType.DMA], name=name)(v)


def _sems(n):
    return [pltpu.SemaphoreType.DMA((n,)), pltpu.SemaphoreType.DMA((n,))]


_HBM = pl.BlockSpec(memory_space=pltpu.HBM)
_SEM = pl.BlockSpec(memory_space=pltpu.SEMAPHORE)
_EFFECT = pltpu.CompilerParams(has_side_effects=pltpu.SideEffectType.DATAFLOW_SIDE_EFFECTING)


def _ici_copy(src_refs, land_refs, send_sems, recv_sems, a, j, slices, incoming):
    x, y, c, chips = _place()
    if _peers(slices) == 1:
        src, dst = slices(src_refs[a], land_refs[a], 1 - c if incoming else c)
        return pltpu.make_async_remote_copy(src_ref=src, dst_ref=dst, send_sem=send_sems.at[a], recv_sem=recv_sems.at[a],
                                            device_id=(x, y, 1 - c), device_id_type=MESH)
    me, other = 2 * x + y, 2 * chips[j][0] + chips[j][1]
    src, dst = slices(src_refs[a], land_refs[a], other, me, c) if incoming else slices(src_refs[a], land_refs[a], me, other, c)
    return pltpu.make_async_remote_copy(src_ref=src, dst_ref=dst, send_sem=send_sems.at[3 * a + j], recv_sem=recv_sems.at[3 * a + j],
                                        device_id=(*chips[j], c), device_id_type=MESH)


def _peers(slices):
    return 1 if slices is _halves_slices else 3


def _ici_start(srcs, land_shapes, slices, after, name):
    na = len(srcs)

    def body(*refs):
        src_refs, land_refs, send_sems, recv_sems = refs[:na], refs[na:2 * na], refs[2 * na + 1], refs[2 * na + 2]
        for a in range(na):
            for j in range(_peers(slices)):
                _ici_copy(src_refs, land_refs, send_sems, recv_sems, a, j, slices, False).start()
        refs[-1][...] = jnp.zeros_like(refs[-1])

    hbm = lambda v: pltpu.with_memory_space_constraint(v, pltpu.HBM)
    lands = [hbm(lax.empty(s.shape, s.dtype)) for s in land_shapes]
    return pl.pallas_call(
        body, name=name,
        out_shape=(pltpu.SemaphoreType.DMA((_peers(slices) * na,)), pltpu.SemaphoreType.DMA((_peers(slices) * na,)),
                   *[pltpu.HBM(v.shape, v.dtype) for v in srcs],
                   *[pltpu.HBM(s.shape, s.dtype) for s in land_shapes], _sds((8, 128))),
        in_specs=[_HBM] * (2 * na) + [_ANY], out_specs=(_SEM, _SEM, *[_HBM] * (2 * na), _VMEM),
        input_output_aliases={i: 2 + i for i in range(2 * na)}, compiler_params=_EFFECT)(*[hbm(v) for v in srcs], *lands, after)


def _ici_wait(handle, slices, after, name):
    na = (len(handle) - 3) // 2

    def body(*refs):
        src_refs, land_refs, send_sems, recv_sems = refs[:na], refs[na:2 * na], refs[2 * na], refs[2 * na + 1]
        for a in range(na):
            for j in range(_peers(slices)):
                _ici_copy(src_refs, land_refs, send_sems, recv_sems, a, j, slices, False).wait_send()
                _ici_copy(src_refs, land_refs, send_sems, recv_sems, a, j, slices, True).wait_recv()

    thru = handle[2:2 + 2 * na]
    outs = pl.pallas_call(
        body, name=name, out_shape=[pltpu.HBM(v.shape, v.dtype) for v in thru], in_specs=[_HBM] * (2 * na) + [_SEM, _SEM, _ANY],
        out_specs=[_HBM] * (2 * na), input_output_aliases={i: i for i in range(2 * na)}, compiler_params=_EFFECT)(
        *thru, handle[0], handle[1], after)
    return outs[:na], outs[na:]


def _gather_slices(p_ref, land_ref, sender, receiver, c):
    r2 = p_ref.shape[0] // 2
    return p_ref.at[pl.ds(c * r2, r2), :], land_ref.at[sender, pl.ds(c * r2, r2), :]


def _scatter_slices(a_ref, t_ref, sender, receiver, c):
    return a_ref.at[receiver], t_ref.at[sender]


def _halves_slices(g_ref, land_ref, sender_c):
    r2 = g_ref.shape[1] // 2
    return g_ref.at[:, pl.ds((1 - sender_c) * r2, r2), :], land_ref


def _gather_start(arrs, after, name):
    return _ici_start(arrs, [_sds((4,) + v.shape, v.dtype) for v in arrs], _gather_slices, after, name)


def _gather_finish(handle, after, name):
    arrs, stacks = _ici_wait(handle, _gather_slices, after, name + "_wait")
    na = len(stacks)

    def body(*refs):
        s_refs, o_refs, (send_sems, recv_sems) = refs[:na], refs[na:2 * na], refs[2 * na:]
        x, y, c, chips = _place()

        def copy(a, j, cc, to):
            r2 = s_refs[a].shape[1] // 2
            at = (2 * chips[j][0] + chips[j][1], pl.ds(cc * r2, r2), slice(None))
            return pltpu.make_async_remote_copy(src_ref=s_refs[a].at[at], dst_ref=o_refs[a].at[at], send_sem=send_sems.at[3 * a + j],
                                                recv_sem=recv_sems.at[3 * a + j], device_id=to, device_id_type=MESH)

        passed = [copy(a, j, c, (x, y, 1 - c)) for a in range(na) for j in range(3)]
        for cp in passed:
            cp.start()
        for a in range(na):
            for j in range(3):
                copy(a, j, 1 - c, (x, y, c)).wait_recv()
        for cp in passed:
            cp.wait_send()

    stacks = pl.pallas_call(
        body, out_shape=[_sds(v.shape, v.dtype) for v in stacks], in_specs=[_ANY] * na, out_specs=[_ANY] * na,
        input_output_aliases={i: i for i in range(na)}, scratch_shapes=_sems(3 * na), name=name + "_pass")(*stacks)
    chip = 2 * lax.axis_index("x") + lax.axis_index("y")
    return [lax.dynamic_update_slice(s, v[None], (chip, 0, 0)) for s, v in zip(stacks, arrs)]


def _halves_start(gs, after, tag):
    return _ici_start(gs, [_sds((4, v.shape[1] // 2, v.shape[2]), v.dtype) for v in gs], _halves_slices, after, f"rs_halves_{tag}_start")


def _join_halves(fs, name):
    na = len(fs)

    def body(*refs):
        f_refs, o_refs, (send_sems, recv_sems) = refs[:na], refs[na:2 * na], refs[2 * na:]
        x, y, c, _ = _place()

        def copy(a, cc, to):
            r2 = f_refs[a].shape[0] // 2
            return pltpu.make_async_remote_copy(src_ref=f_refs[a].at[pl.ds(cc * r2, r2), :], dst_ref=o_refs[a].at[pl.ds(cc * r2, r2), :],
                                                send_sem=send_sems.at[a], recv_sem=recv_sems.at[a], device_id=to, device_id_type=MESH)

        cps = [copy(a, c, (x, y, 1 - c)) for a in range(na)]
        for cp in cps:
            cp.start()
        for a in range(na):
            copy(a, 1 - c, (x, y, c)).wait_recv()
        for cp in cps:
            cp.wait_send()

    return pl.pallas_call(
        body, out_shape=[_sds(v.shape, v.dtype) for v in fs], in_specs=[_ANY] * na, out_specs=[_ANY] * na,
        input_output_aliases={i: i for i in range(na)}, scratch_shapes=_sems(na), name=name)(*fs)


def _sum_chips(a, t, chip, ci, name):
    _, r2, n = t.shape
    tm = _row_tile(r2)
    nb = r2 // tm

    def body(k_ref, a_ref, t1_ref, t2_ref, t3_ref, o_ref):
        o_ref[...] = ((a_ref[...].astype(F32) + t1_ref[...].astype(F32)) + t2_ref[...].astype(F32)) + t3_ref[...].astype(F32)

    def slot(j):
        return pl.BlockSpec((None, tm, n), lambda i, k_ref: (lax.rem(k_ref[0] + j, 4), i, 0))

    return pl.pallas_call(
        body, grid_spec=pltpu.PrefetchScalarGridSpec(num_scalar_prefetch=1, grid=(nb,), in_specs=[slot(0), slot(1), slot(2), slot(3)],
                                                     out_specs=pl.BlockSpec((tm, n), lambda i, k_ref: (k_ref[1] * nb + i, 0))),
        out_shape=_sds((2 * r2, n)), name=name, compiler_params=_PAR)(jnp.stack([chip, ci]).astype(jnp.int32), a, t, t, t)


def _add_cast(g, recv, c, name):
    _, r2, n = recv.shape
    tm = _row_tile(r2)
    nb = r2 // tm

    def body(c_ref, a_ref, b_ref, o_ref):
        o_ref[...] = (a_ref[...] + b_ref[...]).astype(BF16)

    spec = pl.BlockSpec((None, tm, n), lambda k, i, c_ref: (k, i, 0))
    return pl.pallas_call(
        body, grid_spec=pltpu.PrefetchScalarGridSpec(
            num_scalar_prefetch=1, grid=(4, nb),
            in_specs=[pl.BlockSpec((None, tm, n), lambda k, i, c_ref: (k, c_ref[0] * nb + i, 0)), spec], out_specs=spec),
        out_shape=_sds(recv.shape, BF16), name=name,
        compiler_params=pltpu.CompilerParams(dimension_semantics=("parallel", "parallel")))(c.reshape(1).astype(jnp.int32), g, recv)


def _sum_lead(t, name, tm=256):
    P, R, n = t.shape
    tm = _row_tile(R, tm)

    def body(t_ref, o_ref):
        acc = t_ref[0].astype(F32)
        for j in range(1, P):
            acc = acc + t_ref[j].astype(F32)
        o_ref[...] = acc

    return pl.pallas_call(body, grid=(R // tm,), in_specs=[pl.BlockSpec((P, tm, n), lambda i: (0, i, 0))],
                          out_specs=pl.BlockSpec((tm, n), lambda i: (i, 0)), out_shape=_sds((R, n)), name=name, compiler_params=_PAR)(t)


def _ada_fwd(c16, ada_w, ada_b_cols, tn=512):
    L, _, n = ada_w.shape

    def body(c_ref, w_ref, b_ref, o_ref):
        o_ref[0] = _dot(_silu(c_ref[...]).astype(BF16), w_ref[0].astype(BF16), _NN) + b_ref[0]

    return pl.pallas_call(
        body, grid=(L, n // tn),
        in_specs=[pl.BlockSpec((16, D), lambda l, j: (0, 0)), pl.BlockSpec((1, D, tn), lambda l, j: (l, 0, j)), pl.BlockSpec((1, 1, tn), lambda l, j: (l, 0, j))],
        out_specs=pl.BlockSpec((1, 16, tn), lambda l, j: (l, 0, j)), out_shape=_sds((L, 16, n)), name="ada_fwd",
        compiler_params=pltpu.CompilerParams(dimension_semantics=("parallel", "parallel")))(c16, ada_w, ada_b_cols)


def _ada_bwd(c16, dmod, tn=512):
    L, _, n = dmod.shape

    def body(c_ref, d_ref, o_ref):
        o_ref[0] = _dot(_silu(c_ref[...]).astype(BF16), d_ref[0].astype(BF16), _TN)

    return pl.pallas_call(
        body, grid=(L, n // tn), in_specs=[pl.BlockSpec((16, D), lambda l, j: (0, 0)), pl.BlockSpec((1, 16, tn), lambda l, j: (l, 0, j))],
        out_specs=pl.BlockSpec((1, D, tn), lambda l, j: (l, 0, j)), out_shape=_sds((L, D, n)), name="ada_bwd",
        compiler_params=pltpu.CompilerParams(dimension_semantics=("parallel", "parallel")))(c16, dmod)


def _adam_math(w, g, m, v):
    mn = ADAM_B1 * m + (1.0 - ADAM_B1) * g
    vn = ADAM_B2 * v + (1.0 - ADAM_B2) * (g * g)
    m_hat = mn / (1.0 - ADAM_B1 ** ADAM_STEP)
    v_hat = vn / (1.0 - ADAM_B2 ** ADAM_STEP)
    return -ADAM_LR * (m_hat / (jnp.sqrt(v_hat) + ADAM_EPS) + ADAM_WD * w), mn, vn


def _adamw(w, g, m, v, name):
    R, n = w.shape
    tm = _row_tile(R)

    def body(w_ref, g_ref, m_ref, v_ref, d_ref, nm_ref, nv_ref):
        d_ref[...], nm_ref[...], nv_ref[...] = _adam_math(w_ref[...], g_ref[...], m_ref[...], v_ref[...])

    spec = pl.BlockSpec((tm, n), lambda i: (i, 0))
    return pl.pallas_call(body, grid=(R // tm,), in_specs=[spec] * 4, out_specs=(spec,) * 3, out_shape=(_sds((R, n)),) * 3,
                          name=name, compiler_params=_PAR)(w, g, m, v)


def _adamw_cols(w, g0, g1, m, v, name):
    fwd, back = (lambda t: jnp.transpose(t, (2, 0, 1))), (lambda t: jnp.transpose(t, (1, 2, 0)))
    gt = jnp.stack([g0.T, g1.T], 1)
    n, _, r = gt.shape
    tr = max(t for t in range(1, 257) if n % t == 0)

    def body(w_ref, g_ref, m_ref, v_ref, d_ref, nm_ref, nv_ref):
        d_ref[...], nm_ref[...], nv_ref[...] = _adam_math(w_ref[...], g_ref[...], m_ref[...], v_ref[...])

    spec = pl.BlockSpec((tr, 2, r), lambda i: (i, 0, 0))
    outs = pl.pallas_call(body, grid=(n // tr,), in_specs=[spec] * 4, out_specs=(spec,) * 3, out_shape=(_sds(gt.shape),) * 3,
                          name=name, compiler_params=_PAR)(fwd(w), gt, fwd(m), fwd(v))
    return (back(gt), *[back(o) for o in outs])


def _adamw_layers(w, g0, g1, m, v, after, name):
    _, r, n = w.shape
    tm = _row_tile(r)
    nb = r // tm

    def body(w_ref, g0_ref, g1_ref, m_ref, v_ref, after_ref, g_ref, d_ref, nm_ref, nv_ref):
        gv = jnp.where(pl.program_id(0) == 0, g0_ref[...], g1_ref[...])
        g_ref[...] = gv
        d_ref[...], nm_ref[...], nv_ref[...] = _adam_math(w_ref[...], gv, m_ref[...], v_ref[...])

    spec = pl.BlockSpec((None, tm, n), lambda l, i: (l, i, 0))
    g0_spec = pl.BlockSpec((tm, n), lambda l, i: (i * (1 - l) + (nb - 1) * l, 0))
    g1_spec = pl.BlockSpec((tm, n), lambda l, i: (i * l, 0))
    return pl.pallas_call(body, grid=(2, nb), in_specs=[spec, g0_spec, g1_spec, spec, spec, _ANY], out_specs=(spec,) * 4,
                          out_shape=(_sds(w.shape),) * 4, name=name,
                          compiler_params=pltpu.CompilerParams(dimension_semantics=("arbitrary", "arbitrary")))(w, g0, g1, m, v, after)


_W_NAMES = ["ada_w", "ada_b", "norm1_w", "w_in", "q_a_norm", "w_q_b", "kv_a_norm", "w_kv_b", "q_norm", "k_norm", "pool_w",
            "pool_scale", "ssd_conv_w", "ssd_conv_b", "ssd_dt_bias", "ssd_a_log", "ssd_d", "ssd_norm_w", "w_branch", "w_out",
            "norm2_w", "ffn_up", "ffn_conv_w", "ffn_conv_b", "ffn_down"]
_BIG = [("w_in", (D, IN_DIM // 4), 1), ("w_q_b", (Q_RANK, 192), 1), ("w_kv_b", (KV_RANK, 256), 1), ("w_branch", (512, D), 0),
        ("w_out", (256, D), 0), ("ffn_up", (D, 2 * FFN // 4), 1), ("ffn_down", (FFN // 4, D), 0)]

_SMALL = [("norm1_w", (D,)), ("q_a_norm", (Q_RANK,)), ("kv_a_norm", (KV_RANK,)), ("q_norm", (96,)), ("k_norm", (96,)),
          ("pool_w", (4, 128, 128)), ("pool_scale", (512,)), ("ssd_conv_w", (4, SSD_CONV_DIM)), ("ssd_conv_b", (SSD_CONV_DIM,)),
          ("ssd_dt_bias", (SSD_HEADS,)), ("ssd_a_log", (SSD_HEADS,)), ("ssd_d", (SSD_HEADS,)), ("ssd_norm_w", (D,)), ("norm2_w", (D,)),
          ("ffn_conv_w", (3, 2 * FFN)), ("ffn_conv_b", (2 * FFN,))]
_CONV_SHARDED = {"ssd_conv_w": SSD_CONV_DIM // 4, "ffn_conv_w": 2 * FFN // 4}


def _pack_flat(arrs, mult):
    flat = jnp.concatenate([a.astype(F32).reshape(-1) for a in arrs])
    rows = -(-flat.shape[0] // (128 * mult)) * mult
    return jnp.pad(flat, (0, rows * 128 - flat.shape[0])).reshape(rows, 128), [a.shape for a in arrs]


def _unpack_flat(packed, shapes):
    flat, out, off = packed.reshape(-1), [], 0
    for s in shapes:
        n = 1
        for d in s:
            n *= d
        out.append(flat[off:off + n].reshape(s))
        off += n
    return out


_EARLY = ["w_in", "w_q_b", "w_kv_b"]
_LATE = ["w_branch", "w_out", "ffn_up", "ffn_down"]


def _early_weights(a, l, stacks, conv_full):
    p = {n: a[n][l] for n in _W_NAMES if n not in ("ada_w", "ada_b")}
    p.update({n: conv_full[n][l] for n in conv_full})
    p.update(w_in4=stacks[0], w_q_b4=stacks[1], w_kv_b4=stacks[2])
    return p


def _late_weights(stacks):
    return _prep_late(dict(w_branch=stacks[0].reshape(2048, D), w_out=stacks[1].reshape(D, D), ffn_up4=stacks[2],
                           ffn_down=stacks[3].reshape(FFN, D)))


def _reduce_start(halves, ci, after, tag):
    chip_sum = []
    for t, h in halves.items():
        gs, recv = _ici_wait(h, _halves_slices, after, f"rs_halves_{t}_wait")
        chip_sum += [_add_cast(g, r, ci, f"rs_add_{t}") for g, r in zip(gs, recv)]
    return _ici_start(chip_sum, [_sds(v.shape, v.dtype) for v in chip_sum], _scatter_slices, after, f"rs_scatter_{tag}_start")


def _reduce_finish(started, after, tag):
    chip_sum, got = _ici_wait(started, _scatter_slices, after, f"rs_scatter_{tag}_wait")
    chip, ci = 2 * lax.axis_index("x") + lax.axis_index("y"), lax.axis_index("c")
    return _join_halves([_sum_chips(s, t, chip, ci, f"rs_sum_{tag}") for s, t in zip(chip_sum, got)], f"rs_join_{tag}")


def kernel(x, c, positions, ada_w, ada_b, norm1_w, w_in, q_a_norm, w_q_b, kv_a_norm, w_kv_b, q_norm, k_norm, pool_w, pool_scale, ssd_conv_w, ssd_conv_b, ssd_dt_bias, ssd_a_log, ssd_d, ssd_norm_w, w_branch, w_out, norm2_w, ffn_up, ffn_conv_w, ffn_conv_b, ffn_down, loss_target, m_ada_w, m_ada_b, m_norm1_w, m_w_in, m_q_a_norm, m_w_q_b, m_kv_a_norm, m_w_kv_b, m_q_norm, m_k_norm, m_pool_w, m_pool_scale, m_ssd_conv_w, m_ssd_conv_b, m_ssd_dt_bias, m_ssd_a_log, m_ssd_d, m_ssd_norm_w, m_w_branch, m_w_out, m_norm2_w, m_ffn_up, m_ffn_conv_w, m_ffn_conv_b, m_ffn_down, v_ada_w, v_ada_b, v_norm1_w, v_w_in, v_q_a_norm, v_w_q_b, v_kv_a_norm, v_w_kv_b, v_q_norm, v_k_norm, v_pool_w, v_pool_scale, v_ssd_conv_w, v_ssd_conv_b, v_ssd_dt_bias, v_ssd_a_log, v_ssd_d, v_ssd_norm_w, v_w_branch, v_w_out, v_norm2_w, v_ffn_up, v_ffn_conv_w, v_ffn_conv_b, v_ffn_down):
    a = dict(locals())
    xi, yi, ci = lax.axis_index("x"), lax.axis_index("y"), lax.axis_index("c")
    chip = 2 * xi + yi
    dev = 2 * chip + ci
    ncol = 6 * D // 4

    c_all = _allgather8(c.reshape(8, 128), "gather_c").reshape(8, D)
    c16 = jnp.pad(c_all, ((0, 8), (0, 0)))
    ada_b_cols = lax.dynamic_slice_in_dim(ada_b, chip * ncol, ncol, axis=1).reshape(2, 1, ncol)
    mod_part = _ada_fwd(c16, ada_w, ada_b_cols)[:, :8]
    small1, shapes1 = _pack_flat([mod_part, ssd_conv_w, ffn_conv_w], 8)
    got1 = _allgather8(small1, "gather_mod").reshape(8, -1, 128)
    per_chip = [_unpack_flat(got1[2 * k], shapes1) for k in range(4)]
    mod_all = jnp.concatenate([per_chip[k][0] for k in range(4)], -1)
    conv_full = {"ssd_conv_w": jnp.concatenate([per_chip[k][1] for k in range(4)], -1),
                 "ffn_conv_w": jnp.concatenate([per_chip[k][2] for k in range(4)], -1)}
    mod_mine = lax.dynamic_index_in_dim(mod_all, dev, axis=1, keepdims=False).reshape(2, 6, D)
    mods = [jnp.pad(mod_mine[l], ((0, 2), (0, 0))) for l in range(2)]

    big = _EARLY + _LATE
    shard = lambda names, l: [a[n][l].astype(BF16) for n in names]
    g0a = _gather_start(shard(_EARLY, 0), mods[0], "gather_0a")
    g0b = _gather_start(shard(_LATE, 0), g0a[-1], "gather_0b")
    g1 = _gather_start(shard(big, 1), g0b[-1], "gather_1")
    mods[0] = mods[0] + g1[-1][0, 0]

    def get_layer(l, after):
        if l == 0:
            return (_early_weights(a, 0, _gather_finish(g0a, mods[0], "gather_0a"), conv_full),
                    lambda aft: _late_weights(_gather_finish(g0b, aft, "gather_0b")))
        stacks = _gather_finish(g1, after, "gather_1")
        return _early_weights(a, 1, stacks[:3], conv_full), lambda aft: _late_weights(stacks[3:])

    halves, scatters = {}, {}

    def bwd_mod(l):
        return mods[l] if l == 1 else mods[0] + halves["1a"][-1][0, 0]

    def emit(l, late):
        if l == 1:
            halves["1b"] = _halves_start([late[n] for n in _LATE], late["ffn_down"], "1b")
            return halves["1b"][-1][0, 0]
        scatters["1"] = _reduce_start({"1b": halves["1b"], "1a": halves["1a"]}, ci, late["ffn_down"], "1")
        halves["0b"] = _halves_start([late[n] for n in _LATE], scatters["1"][-1], "0b")
        return halves["0b"][-1][0, 0]

    def mid(l, after):
        if l == 0:
            scatters["0b"] = _reduce_start({"0b": halves["0b"]}, ci, after, "0b")
            return scatters["0b"][-1][0, 0]

    def done(l, grads_l):
        halves[f"{l}a"] = _halves_start([grads_l[n] for n in _EARLY], halves[f"{l}b"][-1], f"{l}a")

    lpart, grad_x, grads, dmods = _local_step(x[0], loss_target[0], positions[0], mods, get_layer, bwd_mod, emit, mid, done)
    loss = lax.psum(lpart, ("x", "y", "c"))
    behind = grad_x[:1, :128] + halves["0a"][-1][:1]
    red1 = dict(zip(_LATE + _EARLY, _reduce_finish(scatters["1"], behind, "1")))
    red0b = _reduce_finish(scatters["0b"], red1["w_in"], "0b")
    scatters["0a"] = _reduce_start({"0a": halves["0a"]}, ci, red0b[0], "0a")

    small2, shapes2 = _pack_flat([jnp.stack(dmods)] + [grads[l][n] for l in range(2) for n, _ in _SMALL], 16)
    got2 = _allgather8(small2, "gather_small").reshape(8, -1, 128)
    tot = _unpack_flat(_sum_lead(got2, "sum_small"), shapes2)
    g = {"ada_b": tot[0].reshape(2, 6 * D)}
    for i, (n, _) in enumerate(_SMALL):
        g[n] = jnp.stack([tot[1 + i], tot[1 + len(_SMALL) + i]])
    for n, w in _CONV_SHARDED.items():
        g[n] = lax.dynamic_slice_in_dim(g[n], chip * w, w, axis=2)
    nd = 2 * 6 * D // 128
    dmod_all = jnp.transpose(got2[:, :nd].reshape(8, 2, 6 * D), (1, 0, 2))
    dmod_cols = lax.dynamic_slice_in_dim(jnp.pad(dmod_all, ((0, 0), (0, 8), (0, 0))), chip * ncol, ncol, axis=2)
    g["ada_w"] = _ada_bwd(c16, dmod_cols)

    delta, new_m, new_v = {}, {}, {}
    token = scatters["0a"][-1]
    for n, r0 in zip(_LATE, red0b):
        g[n], delta[n], new_m[n], new_v[n] = _adamw_layers(a[n], r0, red1[n], a["m_" + n], a["v_" + n], token, f"adamw_{n}")
    shp = ada_w.shape
    r2 = lambda t: t.reshape(-1, shp[-1])
    delta["ada_w"], new_m["ada_w"], new_v["ada_w"] = (
        o.reshape(shp) for o in _adamw(r2(ada_w), r2(g["ada_w"]), r2(m_ada_w), r2(v_ada_w), "adamw_ada_w"))
    behind = (delta[_LATE[-1]][0, 0, :1] + delta["ada_w"][0, 0, :1]).reshape(1)
    red0a = _reduce_finish(scatters["0a"], behind, "0a")
    for n, r0 in zip(_EARLY, red0a):
        if n == "w_in":
            g[n], delta[n], new_m[n], new_v[n] = _adamw_cols(a[n], r0, red1[n], a["m_" + n], a["v_" + n], f"adamw_{n}")
        else:
            g[n], delta[n], new_m[n], new_v[n] = _adamw_layers(a[n], r0, red1[n], a["m_" + n], a["v_" + n], token, f"adamw_{n}")
    rest = [n for n in _W_NAMES if n not in big and n != "ada_w"]
    packs = [_pack_flat([t[n] if pre is None else t[pre + n] for n in rest], 128)[0]
             for t, pre in ((a, None), (g, None), (a, "m_"), (a, "v_"))]
    rest_shapes = [a[n].shape for n in rest]
    outs = [_unpack_flat(o, rest_shapes) for o in _adamw(*packs, "adamw_rest")]
    for i, n in enumerate(rest):
        delta[n], new_m[n], new_v[n] = outs[0][i], outs[1][i], outs[2][i]

    return (loss, grad_x[None], *[g[n] for n in _W_NAMES], *[delta[n] for n in _W_NAMES],
            *[new_m[n] for n in _W_NAMES], *[new_v[n] for n in _W_NAMES])
```

```python
import functools

import jax
import jax.numpy as jnp
from jax import lax
from jax.experimental import pallas as pl
from jax.experimental.pallas import tpu as pltpu

F32 = jnp.float32
BF16 = jnp.bfloat16
MESH = pl.DeviceIdType.MESH
HI = lax.Precision.HIGHEST

D = 1024
N_HEADS = 8
NOPE, ROPE_DIM = 64, 32
Q_RANK, KV_RANK = 384, 256
POOL_WINDOWS = (2, 4, 8, 16)
SSD_HEADS, SSD_P, SSD_N, SSD_L = 16, 64, 128, 128
SSD_INNER = 1024
SSD_CONV_DIM = 1536
FFN = 2816
EPS = 1e-6
ROPE_THETA = 10000.0
OG, OZ, OX, OP, OQ, OKV, ODT, IN_PAD = 0, 3072, 4096, 5632, 6144, 6528, 6912, 7040
IN_DIM = 6832
ADAM_LR, ADAM_B1, ADAM_B2, ADAM_EPS, ADAM_WD, ADAM_STEP = 0.001, 0.9, 0.999, 1e-08, 0.01, 10

_ARB = pltpu.CompilerParams(dimension_semantics=("arbitrary",))
_PAR = pltpu.CompilerParams(dimension_semantics=("parallel",))


def _pick(n, pref):
    if n <= pref:
        return n
    best = None
    for t in range(128, pref + 1, 128):
        if n % t == 0:
            best = t
    assert best is not None, (n, pref)
    return best


def _row_tile(r, cap=256):
    best = None
    for t in range(16, min(r, cap) + 1, 16):
        if r % t == 0:
            best = t
    assert best is not None, r
    return best


def _sds(shape, dtype=F32):
    return jax.ShapeDtypeStruct(tuple(shape), dtype)


def _iota(shape, dim):
    return lax.broadcasted_iota(jnp.int32, shape, dim)


def _sigmoid(x):
    return 0.5 * jnp.tanh(0.5 * x) + 0.5


def _silu(x):
    return x * _sigmoid(x)


def _dsilu(x):
    s = _sigmoid(x)
    return s * (1.0 + x * (1.0 - s))


def _dot(a, b, dims):
    return lax.dot_general(a, b, (dims, ((), ())), preferred_element_type=F32)


_NN, _NT, _TN = ((1,), (0,)), ((1,), (1,)), ((0,), (0,))


def _dot_hi(a, b, dims=_NN):
    return lax.dot_general(a, b, (dims, ((), ())), preferred_element_type=F32, precision=HI)


def _shift_down(x, j):
    n = x.shape[0]
    return jnp.where(_iota(x.shape, 0) >= j, pltpu.roll(x, j, 0), 0.0)


def _shift_up(x, j):
    n = x.shape[0]
    return jnp.where(_iota(x.shape, 0) < n - j, pltpu.roll(x, n - j, 0), 0.0)


def _mm(a, b, mode="nn", out_dtype=F32, tm=2048, tn=512, tk=4096, res=None, gate=None, name="mm"):
    if mode == "nn":
        (M, K), (K2, N) = a.shape, b.shape
    elif mode == "nt":
        (M, K), (N, K2) = a.shape, b.shape
    else:
        (K, M), (K2, N) = a.shape, b.shape
    assert K == K2, (a.shape, b.shape, mode)
    tm, tn, tk = _pick(M, tm), _pick(N, tn), _pick(K, tk)
    nk = K // tk
    dims = {"nn": _NN, "nt": _NT, "tn": _TN}[mode]
    fused = res is not None

    def body(*refs):
        a_ref, b_ref = refs[:2]

        def finish(acc):
            if fused:
                r_ref, g_ref, o_ref, raw_ref = refs[2:6]
                raw_ref[...] = acc
                o_ref[...] = r_ref[...] + g_ref[...] * acc
            else:
                refs[2][...] = acc.astype(out_dtype)

        _mm_steps(a_ref, b_ref, dims, nk, refs[-1] if nk > 1 else None, finish)

    if mode == "nn":
        a_spec = pl.BlockSpec((tm, tk), lambda i, j, k: (i, k))
        b_spec = pl.BlockSpec((tk, tn), lambda i, j, k: (k, j))
    elif mode == "nt":
        a_spec = pl.BlockSpec((tm, tk), lambda i, j, k: (i, k))
        b_spec = pl.BlockSpec((tn, tk), lambda i, j, k: (j, k))
    else:
        a_spec = pl.BlockSpec((tk, tm), lambda i, j, k: (k, i))
        b_spec = pl.BlockSpec((tk, tn), lambda i, j, k: (k, j))
    o_spec = pl.BlockSpec((tm, tn), lambda i, j, k: (i, j))
    in_specs, args = [a_spec, b_spec], [a, b]
    out_shape, out_specs = _sds((M, N), out_dtype), o_spec
    if fused:
        in_specs += [o_spec, pl.BlockSpec((1, tn), lambda i, j, k: (0, j))]
        args += [res, gate]
        out_shape, out_specs = (_sds((M, N)), _sds((M, N))), (o_spec, o_spec)
    return pl.pallas_call(
        body, grid=(M // tm, N // tn, nk), in_specs=in_specs, out_specs=out_specs, out_shape=out_shape,
        scratch_shapes=[pltpu.VMEM((tm, tn), F32)] if nk > 1 else [], name=name,
        compiler_params=pltpu.CompilerParams(dimension_semantics=("parallel", "parallel", "arbitrary")),
    )(*args)


def _mm_steps(a_ref, b_ref, dims, nk, acc_ref, finish):
    part = _dot(a_ref[...].astype(BF16), b_ref[...].astype(BF16), dims)
    if nk == 1:
        finish(part)
        return
    k = pl.program_id(2)

    @pl.when(k == 0)
    def _():
        acc_ref[...] = part

    @pl.when(k > 0)
    def _():
        acc_ref[...] += part

    @pl.when(k == nk - 1)
    def _():
        finish(acc_ref[...])


def _mm_blocks(a, b, dims, grid, a_spec, b_spec, o_spec, out_shape, acc_shape, name, into=None):
    nk = grid[2]
    extra = 0 if into is None else 1

    def body(a_ref, b_ref, *refs):
        o_ref, scratch = refs[extra], refs[extra + 1:]

        def finish(acc):
            o_ref[...] = acc.astype(o_ref.dtype)

        _mm_steps(a_ref, b_ref, dims, nk, scratch[0] if nk > 1 else None, finish)

    return pl.pallas_call(
        body, grid=grid, in_specs=[a_spec, b_spec] + [pl.BlockSpec(memory_space=pl.ANY)] * extra, out_specs=o_spec, out_shape=out_shape,
        input_output_aliases={2: 0} if extra else {},
        scratch_shapes=[pltpu.VMEM(acc_shape, F32)] if nk > 1 else [], name=name,
        compiler_params=pltpu.CompilerParams(dimension_semantics=("parallel", "parallel", "arbitrary")),
    )(a, b, *([] if into is None else [into]))


def _branch_dw(o, dy, row0, into, name, tm=512, tn=512):
    S, M = o.shape
    return _mm_blocks(o, dy, _TN, (M // tm, D // tn, 1), pl.BlockSpec((S, tm), lambda i, j, k: (0, i)),
                      pl.BlockSpec((S, tn), lambda i, j, k: (0, j)), pl.BlockSpec((tm, tn), lambda i, j, k: (i + row0 // tm, j)),
                      _sds((2048, D)), (tm, tn), name, into)


_UP_SHARD = 2 * FFN // 4


def _up_fwd(h2, wup4, name, tm=2048):
    S = h2.shape[0]
    tm = min(tm, S)
    return _mm_blocks(h2, wup4, _NN, (S // tm, 4, 1), pl.BlockSpec((tm, D), lambda i, j, k: (i, 0)),
                      pl.BlockSpec((None, D, _UP_SHARD), lambda i, j, k: (j, 0, 0)), pl.BlockSpec((tm, _UP_SHARD), lambda i, j, k: (i, j)),
                      _sds((S, 2 * FFN)), (tm, _UP_SHARD), name)


def _up_dx(dup2, wup4, name, tm=2048, tn=512):
    S = dup2.shape[1]
    tm = min(tm, S)
    return _mm_blocks(dup2, wup4, _NT, (S // tm, D // tn, 4), pl.BlockSpec((None, tm, _UP_SHARD), lambda i, j, k: (lax.div(k, 2), i, lax.rem(k, 2))),
                      pl.BlockSpec((None, tn, _UP_SHARD), lambda i, j, k: (k, j, 0)), pl.BlockSpec((tm, tn), lambda i, j, k: (i, j)),
                      _sds((S, D)), (tm, tn), name)


def _up_dw(h2, dup2, name, tm=1024, tk=4096):
    S = h2.shape[0]
    tk = min(tk, S)
    return _mm_blocks(h2, dup2, _TN, (D // tm, 4, S // tk), pl.BlockSpec((tk, tm), lambda i, j, k: (k, i)),
                      pl.BlockSpec((None, tk, _UP_SHARD), lambda i, j, k: (lax.div(j, 2), k, lax.rem(j, 2))),
                      pl.BlockSpec((None, tm, _UP_SHARD), lambda i, j, k: (j, i, 0)), _sds((4, D, _UP_SHARD)), (tm, _UP_SHARD), name)


def _row_spec(tm, n):
    return pl.BlockSpec((tm, n), lambda i: (i, 0))


def _vec_spec(n, rows=1):
    return pl.BlockSpec((rows, n), lambda i: (0, 0))


def _ln_mod(x, nw, sc, sh, name, tm=256):
    S = x.shape[0]

    def body(x_ref, nw_ref, sc_ref, sh_ref, o_ref):
        xv = x_ref[...]
        r = lax.rsqrt(jnp.mean(xv * xv, -1, keepdims=True) + EPS)
        o_ref[...] = ((xv * r * nw_ref[...]) * (1.0 + sc_ref[...]) + sh_ref[...]).astype(BF16)

    return pl.pallas_call(
        body, grid=(S // tm,), in_specs=[_row_spec(tm, D)] + [_vec_spec(D)] * 3, out_specs=_row_spec(tm, D),
        out_shape=_sds((S, D), BF16), name=name, compiler_params=_PAR)(x, nw, sc, sh)


def _ln_mod_bwd(x, dh, dres, nw, sc, name, tm=256):
    S = x.shape[0]

    def body(x_ref, dh_ref, dres_ref, nw_ref, sc_ref, dx_ref, st_ref):
        @pl.when(pl.program_id(0) == 0)
        def _():
            st_ref[...] = jnp.zeros_like(st_ref)

        xv, dhv, nwv = x_ref[...], dh_ref[...], nw_ref[...]
        r = lax.rsqrt(jnp.mean(xv * xv, -1, keepdims=True) + EPS)
        xhat = xv * r
        dn = dhv * (1.0 + sc_ref[...])
        g = dn * nwv
        dx_ref[...] = dres_ref[...] + r * (g - xhat * jnp.mean(g * xhat, -1, keepdims=True))
        st_ref[0:1, :] += jnp.sum(dhv, 0, keepdims=True)
        st_ref[1:2, :] += jnp.sum(dhv * (xhat * nwv), 0, keepdims=True)
        st_ref[2:3, :] += jnp.sum(dn * xhat, 0, keepdims=True)

    return pl.pallas_call(
        body, grid=(S // tm,), in_specs=[_row_spec(tm, D)] * 3 + [_vec_spec(D)] * 2,
        out_specs=(_row_spec(tm, D), _vec_spec(D, 8)), out_shape=(_sds((S, D)), _sds((8, D))),
        name=name, compiler_params=_ARB)(x, dh, dres, nw, sc)


def _gate_bwd(dx, out, g, name, tm=256):
    S = dx.shape[0]

    def body(dx_ref, o_ref, g_ref, dz_ref, dg_ref):
        @pl.when(pl.program_id(0) == 0)
        def _():
            dg_ref[...] = jnp.zeros_like(dg_ref)

        dxv = dx_ref[...]
        dz_ref[...] = (dxv * g_ref[...]).astype(BF16)
        dg_ref[0:1, :] += jnp.sum(dxv * o_ref[...], 0, keepdims=True)

    return pl.pallas_call(
        body, grid=(S // tm,), in_specs=[_row_spec(tm, D)] * 2 + [_vec_spec(D)],
        out_specs=(_row_spec(tm, D), _vec_spec(D, 8)), out_shape=(_sds((S, D), BF16), _sds((8, D))),
        name=name, compiler_params=_ARB)(dx, out, g)


def _loss_grad(y, t, tm=256):
    S = y.shape[0]

    def body(y_ref, t_ref, dy_ref, l_ref):
        @pl.when(pl.program_id(0) == 0)
        def _():
            l_ref[...] = jnp.zeros_like(l_ref)

        e = y_ref[...] - t_ref[...]
        dy_ref[...] = e * (1.0 / D)
        l_ref[...] += 0.5 * jnp.sum(jnp.mean(e * e, -1, keepdims=True), 0, keepdims=True)

    return pl.pallas_call(
        body, grid=(S // tm,), in_specs=[_row_spec(tm, D)] * 2,
        out_specs=(_row_spec(tm, D), pl.BlockSpec((8, 128), lambda i: (0, 0))),
        out_shape=(_sds((S, D)), _sds((8, 128))), name="loss_grad", compiler_params=_ARB)(y, t)


_CONV_ROWS = 256
_HALO = 8


def _rows(ref, lo, hi):
    S, c = ref.shape
    parts = [jnp.zeros((-lo, c), F32)] if lo < 0 else []
    parts.append(ref[max(lo, 0):min(hi, S), :])
    if hi > S:
        parts.append(jnp.zeros((hi - S, c), F32))
    return parts[0] if len(parts) == 1 else jnp.concatenate(parts, 0)


def _conv_rows(ext, w, b, n):
    K = w.shape[0]
    acc = ext[_HALO:_HALO + n] * w[K - 1:K, :] + b
    for j in range(1, K):
        acc = acc + pltpu.roll(ext, j, 0)[_HALO:_HALO + n] * w[K - 1 - j:K - j, :]
    return acc


def _conv_rows_bwd(ext, w, dc, n):
    K, m = w.shape[0], dc.shape[0]
    d0 = dc[0:n]
    dx = d0 * w[K - 1:K, :]
    dws = [jnp.sum(d0 * ext[_HALO:_HALO + n], 0, keepdims=True)]
    for j in range(1, K):
        dx = dx + pltpu.roll(dc, m - j, 0)[0:n] * w[K - 1 - j:K - j, :]
        dws.append(jnp.sum(d0 * pltpu.roll(ext, j, 0)[_HALO:_HALO + n], 0, keepdims=True))
    return dx, dws[::-1], jnp.sum(d0, 0, keepdims=True)


def _col_spec(S, tc, off=0):
    return pl.BlockSpec((S, tc), lambda j: (0, j + off))


def _ssd_pre(proj, cw, cb, tc=256):
    S, n = proj.shape[0], SSD_CONV_DIM
    R = min(_CONV_ROWS, S)

    def body(x_ref, w_ref, b_ref, o_ref):
        wv, bv = w_ref[...], b_ref[...]
        for r0 in range(0, S, R):
            o_ref[r0:r0 + R, :] = _silu(_conv_rows(_rows(x_ref, r0 - _HALO, r0 + R), wv, bv, R))

    return pl.pallas_call(
        body, grid=(n // tc,),
        in_specs=[_col_spec(S, tc, OX // tc), pl.BlockSpec((4, tc), lambda j: (0, j)), pl.BlockSpec((1, tc), lambda j: (0, j))],
        out_specs=_col_spec(S, tc), out_shape=_sds((S, n)), name="ssd_pre", compiler_params=_PAR)(proj, cw, cb)


def _ssd_pre_bwd(proj, dxc, cw, cb, dproj, tc=256):
    S, n = proj.shape[0], SSD_CONV_DIM
    R = min(_CONV_ROWS, S)

    def body(x_ref, d_ref, w_ref, b_ref, dproj_in, dx_ref, dw_ref, db_ref):
        wv, bv = w_ref[...], b_ref[...]
        acc = [jnp.zeros((1, tc), F32)] * 5
        for r0 in range(0, S, R):
            ext = _rows(x_ref, r0 - _HALO, r0 + R + _HALO)
            dc = _rows(d_ref, r0, r0 + R + _HALO) * _dsilu(_conv_rows(ext, wv, bv, R + _HALO))
            dx, dws, db = _conv_rows_bwd(ext, wv, dc, R)
            dx_ref[r0:r0 + R, :] = dx.astype(BF16)
            acc = [s + d for s, d in zip(acc, dws + [db])]
        for k in range(4):
            dw_ref[k:k + 1, :] = acc[k]
        db_ref[...] = acc[4]

    wspec, bspec = pl.BlockSpec((4, tc), lambda j: (0, j)), pl.BlockSpec((1, tc), lambda j: (0, j))
    return pl.pallas_call(
        body, grid=(n // tc,), in_specs=[_col_spec(S, tc, OX // tc), _col_spec(S, tc), wspec, bspec, _ANY],
        out_specs=(_col_spec(S, tc, OX // tc), wspec, bspec), out_shape=(_sds(dproj.shape, BF16), _sds((4, n)), _sds((1, n))),
        input_output_aliases={4: 0}, name="ssd_pre_bwd", compiler_params=_PAR)(proj, dxc, cw, cb, dproj)


def _ffn_act(up, cw, cb, tc=256):
    S, nb = up.shape[0], FFN // tc
    R = min(_CONV_ROWS, S)

    def body(g_ref, v_ref, wg_ref, wv_ref, bg_ref, bv_ref, o_ref):
        wg, wv, bg, bv = wg_ref[...], wv_ref[...], bg_ref[...], bv_ref[...]
        for r0 in range(0, S, R):
            cg = _conv_rows(_rows(g_ref, r0 - _HALO, r0 + R), wg, bg, R)
            cv = _conv_rows(_rows(v_ref, r0 - _HALO, r0 + R), wv, bv, R)
            o_ref[r0:r0 + R, :] = (_silu(cg) * cv).astype(BF16)

    def wspec(off):
        return pl.BlockSpec((3, tc), lambda j: (0, j + off))

    def bspec(off):
        return pl.BlockSpec((1, tc), lambda j: (0, j + off))

    return pl.pallas_call(
        body, grid=(nb,), in_specs=[_col_spec(S, tc), _col_spec(S, tc, nb), wspec(0), wspec(nb), bspec(0), bspec(nb)],
        out_specs=_col_spec(S, tc), out_shape=_sds((S, FFN), BF16), name="ffn_act", compiler_params=_PAR)(up, up, cw, cw, cb, cb)


def _ffn_act_bwd(up, dact, cw, cb, tc=256):
    S, nb = up.shape[0], FFN // tc
    R = min(_CONV_ROWS, S)

    def body(g_ref, v_ref, d_ref, wg_ref, wv_ref, bg_ref, bv_ref, dx_ref, dw_ref, db_ref):
        wg, wv, bg, bv = wg_ref[...], wv_ref[...], bg_ref[...], bv_ref[...]
        acc = [[jnp.zeros((1, tc), F32)] * 4, [jnp.zeros((1, tc), F32)] * 4]
        for r0 in range(0, S, R):
            eg, ev = _rows(g_ref, r0 - _HALO, r0 + R + _HALO), _rows(v_ref, r0 - _HALO, r0 + R + _HALO)
            da = _rows(d_ref, r0, r0 + R + _HALO)
            cg, cv = _conv_rows(eg, wg, bg, R + _HALO), _conv_rows(ev, wv, bv, R + _HALO)
            sg = _sigmoid(cg)
            for half, (ext, w, dc) in enumerate(((eg, wg, da * cv * (sg * (1.0 + cg * (1.0 - sg)))), (ev, wv, da * (cg * sg)))):
                dx, dws, db = _conv_rows_bwd(ext, w, dc, R)
                dx_ref[half, r0:r0 + R, :] = dx.astype(BF16)
                acc[half] = [s + d for s, d in zip(acc[half], dws + [db])]
        for half in range(2):
            for k in range(3):
                dw_ref[half, k:k + 1, :] = acc[half][k]
            db_ref[half] = acc[half][3]

    def wspec(off):
        return pl.BlockSpec((3, tc), lambda j: (0, j + off))

    def bspec(off):
        return pl.BlockSpec((1, tc), lambda j: (0, j + off))

    cs = _col_spec(S, tc)
    both = lambda r: pl.BlockSpec((2, r, tc), lambda j: (0, 0, j))
    return pl.pallas_call(
        body, grid=(nb,), in_specs=[cs, _col_spec(S, tc, nb), cs, wspec(0), wspec(nb), bspec(0), bspec(nb)],
        out_specs=(both(S), both(3), both(1)), out_shape=(_sds((2, S, FFN), BF16), _sds((2, 3, FFN)), _sds((2, 1, FFN))),
        name="ffn_act_bwd", compiler_params=_PAR)(up, up, dact, cw, cw, cb, cb)


_POOL_HALO = 16


def _window_rows(ext, w, n, up=False):
    m, j = ext.shape[0], 1
    while j < w:
        ext = ext + pltpu.roll(ext, m - j if up else j, 0)
        j *= 2
    return ext[0:n] if up else ext[_POOL_HALO:_POOL_HALO + n]


def _pool_count(r0, n, w):
    return jnp.minimum((_iota((n, 128), 0) + (r0 + 1)).astype(F32), float(w))


def _pool_fwd(proj, pool_w, pool_scale):
    S = proj.shape[0]
    R = min(_CONV_ROWS, S)

    def body(u_ref, w_ref, s_ref, o_ref):
        for g, w in enumerate(POOL_WINDOWS):
            sl = slice(g * 128, (g + 1) * 128)
            ug, wv, sv = u_ref.at[:, sl], w_ref[g].astype(BF16), s_ref[:, sl]
            for r0 in range(0, S, R):
                ext = _rows(ug, r0 - _POOL_HALO, r0 + R)
                pooled = _window_rows(ext, w, R) / _pool_count(r0, R, w) - ext[_POOL_HALO:]
                o_ref[r0:r0 + R, sl] = (_dot(pooled.astype(BF16), wv, _NN) * sv).astype(BF16)

    return pl.pallas_call(
        body, grid=(1,),
        in_specs=[pl.BlockSpec((S, 512), lambda i: (0, OP // 512)), pl.BlockSpec((4, 128, 128), lambda i: (0, 0, 0)), _vec_spec(512)],
        out_specs=pl.BlockSpec((S, 512), lambda i: (0, 0)), out_shape=_sds((S, 512), BF16), name="pool_fwd",
        compiler_params=_ARB)(proj, pool_w, pool_scale)


def _pool_bwd(proj, dob, pool_w, pool_scale, dproj, after):
    S = proj.shape[0]

    R = min(_CONV_ROWS, S)
    H = _POOL_HALO

    def body(u_ref, d_ref, w_ref, s_ref, dproj_in, after_ref, du_ref, dw_ref, ds_ref):
        for g, w in enumerate(POOL_WINDOWS):
            sl = slice(g * 128, (g + 1) * 128)
            ug, dg, wv, sv = u_ref.at[:, sl], d_ref.at[:, sl], w_ref[g].astype(BF16), s_ref[:, sl]
            ds_acc, dw_acc = jnp.zeros((1, 128), F32), jnp.zeros((128, 128), F32)
            for r0 in range(0, S, R):
                ext = _rows(ug, r0 - H, r0 + R + H)
                dv = _rows(dg, r0, r0 + R + H)
                cnt = _pool_count(r0, R + H, w)
                pooled = (_window_rows(ext, w, R + H) / cnt - ext[H:]).astype(BF16)
                ds_acc += jnp.sum(dv[0:R] * _dot(pooled[0:R], wv, _NN), 0, keepdims=True)
                dmix = (dv * sv).astype(BF16)
                dw_acc += _dot(pooled[0:R], dmix[0:R], _TN)
                dp = _dot(dmix, wv, _NT)
                du_ref[r0:r0 + R, sl] = (_window_rows(dp / cnt, w, R, up=True) - dp[0:R]).astype(BF16)
            ds_ref[:, sl] = ds_acc
            dw_ref[g] = dw_acc

    blk = pl.BlockSpec((S, 512), lambda i: (0, 0))
    wspec = pl.BlockSpec((4, 128, 128), lambda i: (0, 0, 0))
    return pl.pallas_call(
        body, grid=(1,), in_specs=[pl.BlockSpec((S, 512), lambda i: (0, OP // 512)), blk, wspec, _vec_spec(512), _ANY, _ANY],
        out_specs=(pl.BlockSpec((S, 512), lambda i: (0, OP // 512)), wspec, _vec_spec(512)),
        out_shape=(_sds(dproj.shape, BF16), _sds((4, 128, 128)), _sds((1, 512))), input_output_aliases={4: 0},
        name="pool_bwd", compiler_params=_ARB)(proj, dob, pool_w, pool_scale, dproj, after)


def _branch_specs():
    return [pl.BlockSpec((512, D), lambda i: (0, 0)), pl.BlockSpec((512, D), lambda i: (1, 0)), pl.BlockSpec((1024, D), lambda i: (1, 0))]


def _merge_fwd(oa, ob, oc, proj, wbr, tm=256):
    S = oa.shape[0]

    def body(oa_ref, ob_ref, oc_ref, gl_ref, wa_ref, wb_ref, wc_ref, o_ref):
        acc = _sigmoid(gl_ref[:, 0:D]) * _dot(oa_ref[...], wa_ref[...], _NN)
        acc += _sigmoid(gl_ref[:, D:2 * D]) * _dot(ob_ref[...], wb_ref[...], _NN)
        acc += _sigmoid(gl_ref[:, 2 * D:3 * D]) * _dot(oc_ref[...], wc_ref[...], _NN)
        o_ref[...] = acc.astype(BF16)

    return pl.pallas_call(
        body, grid=(S // tm,),
        in_specs=[_row_spec(tm, 512), _row_spec(tm, 512), _row_spec(tm, D), _row_spec(tm, 3 * D)] + _branch_specs(),
        out_specs=_row_spec(tm, D), out_shape=_sds((S, D), BF16), name="merge_fwd", compiler_params=_PAR)(oa, ob, oc, proj, wbr, wbr, wbr)


def _merge_bwd(dm, oa, ob, oc, proj, wbr, tm=256):
    S = oa.shape[0]

    def body(dm_ref, oa_ref, ob_ref, oc_ref, gl_ref, wa_ref, wb_ref, wc_ref, dya_ref, dyb_ref, dyc_ref, dgl_ref, doa_ref, dob_ref, doc_ref):
        dmv = dm_ref[...]
        for i, (o_ref, w_ref, dy_ref, do_ref) in enumerate(
                ((oa_ref, wa_ref, dya_ref, doa_ref), (ob_ref, wb_ref, dyb_ref, dob_ref), (oc_ref, wc_ref, dyc_ref, doc_ref))):
            gt = _sigmoid(gl_ref[:, i * D:(i + 1) * D])
            wv = w_ref[...]
            yv = _dot(o_ref[...], wv, _NN)
            dy = (dmv * gt).astype(BF16)
            dy_ref[...] = dy
            dgl_ref[:, i * D:(i + 1) * D] = (dmv * yv * gt * (1.0 - gt)).astype(BF16)
            do_ref[...] = _dot(dy, wv, _NT)

    rs = _row_spec
    return pl.pallas_call(
        body, grid=(S // tm,),
        in_specs=[rs(tm, D), rs(tm, 512), rs(tm, 512), rs(tm, D), rs(tm, 3 * D)] + _branch_specs(),
        out_specs=(rs(tm, D), rs(tm, D), rs(tm, D), rs(tm, 3 * D), rs(tm, 512), rs(tm, 512), rs(tm, D)),
        out_shape=(_sds((S, D), BF16), _sds((S, D), BF16), _sds((S, D), BF16), _sds((S, IN_PAD), BF16), _sds((S, 512)), _sds((S, 512)), _sds((S, D))),
        name="merge_bwd", compiler_params=_PAR)(dm, oa, ob, oc, proj, wbr, wbr, wbr)


def _rope_tab(posb, invf, tm=256):
    S = posb.shape[0]

    def body(p_ref, f_ref, c_ref, s1_ref, s2_ref):
        ang = p_ref[...] * f_ref[...]
        lane = _iota(ang.shape, 1)
        cs, sn = jnp.cos(ang), jnp.sin(ang)
        c_ref[...] = jnp.where(lane < NOPE, 1.0, cs)
        s1_ref[...] = jnp.where((lane >= 64) & (lane < 80), -sn, 0.0)
        s2_ref[...] = jnp.where((lane >= 80) & (lane < 96), sn, 0.0)

    rs = _row_spec(tm, 128)
    return pl.pallas_call(body, grid=(S // tm,), in_specs=[rs, _vec_spec(128)], out_specs=(rs, rs, rs),
                          out_shape=(_sds((S, 128)),) * 3, name="rope_tab", compiler_params=_PAR)(posb, invf)


def _rope(u, C, S1, S2):
    return u * C + pltpu.roll(u, 112, 1) * S1 + pltpu.roll(u, 16, 1) * S2


def _rope_t(dy, C, S1, S2):
    return dy * C + pltpu.roll(dy * S1, 16, 1) + pltpu.roll(dy * S2, 112, 1)


def _seg_sum(v, mask):
    return jnp.sum(jnp.where(mask, v, 0.0), -1, keepdims=True)


def _mla_latents(pq_ref, pkv_ref, wqb_ref, wkvb_ref, qan_ref, kvan_ref):
    ql, kvl = pq_ref[...], pkv_ref[...]
    ckv, kr = kvl[:, 0:KV_RANK], kvl[:, KV_RANK:KV_RANK + 128]
    rq = lax.rsqrt(jnp.mean(ql * ql, -1, keepdims=True) + EPS)
    rkv = lax.rsqrt(jnp.mean(ckv * ckv, -1, keepdims=True) + EPS)
    nq = (ql * rq * qan_ref[...]).astype(BF16)
    nkv = (ckv * rkv * kvan_ref[...]).astype(BF16)
    kv = jnp.concatenate([_dot(nkv, wkvb_ref[k], _NN) for k in range(4)], 1)
    return ql, ckv, kr, rq, rkv, nq, nkv, _dot(nq, wqb_ref[...], _NN), kv


def _mla_specs(tm):
    full = lambda r, n: pl.BlockSpec((r, n), lambda i: (0, 0))
    return ([pl.BlockSpec((tm, 384), lambda i: (i, OQ // 384)), pl.BlockSpec((tm, 384), lambda i: (i, OKV // 384))],
            [full(Q_RANK, D), pl.BlockSpec((4, KV_RANK, 256), lambda i: (0, 0, 0)), _vec_spec(Q_RANK), _vec_spec(KV_RANK), _vec_spec(128), _vec_spec(128)]
            + [_row_spec(tm, 128)] * 3)


def _mla_prep(proj, wqb, wkvb, qan, kvan, wq, wk, rope, tm=256):
    S = proj.shape[0]

    def body(pq_ref, pkv_ref, wqb_ref, wkvb_ref, qan_ref, kvan_ref, wq_ref, wk_ref, c_ref, s1_ref, s2_ref, qf_ref, kf_ref, ve_ref):
        _, _, kr, _, _, _, _, q, kv = _mla_latents(pq_ref, pkv_ref, wqb_ref, wkvb_ref, qan_ref, kvan_ref)
        C, S1, S2, wqv, wkv = c_ref[...], s1_ref[...], s2_ref[...], wq_ref[...], wk_ref[...]
        lane = _iota((tm, 128), 1)
        mn, mr = lane < 64, (lane >= 64) & (lane < 96)
        rrk = lax.rsqrt(_seg_sum(kr * kr, mr) / ROPE_DIM + EPS)
        ykr = _rope(jnp.where(mr, kr * rrk * wkv, 0.0), C, S1, S2)
        for h in range(N_HEADS):
            sl = slice(h * 128, (h + 1) * 128)
            t = q[:, sl]
            rn = lax.rsqrt(_seg_sum(t * t, mn) / NOPE + EPS)
            rr = lax.rsqrt(_seg_sum(t * t, mr) / ROPE_DIM + EPS)
            qf_ref[:, sl] = _rope(t * jnp.where(mn, rn, jnp.where(mr, rr, 0.0)) * wqv, C, S1, S2).astype(BF16)
            t = kv[:, sl]
            rn = lax.rsqrt(_seg_sum(t * t, mn) / NOPE + EPS)
            kf_ref[:, sl] = (jnp.where(mn, t * rn * wkv, 0.0) + ykr).astype(BF16)
            ve_ref[:, sl] = (jnp.where(mn, pltpu.roll(t, 64, 1), 0.0) if h % 2 == 0 else jnp.where(mn, 0.0, t)).astype(BF16)

    pspecs, wspecs = _mla_specs(tm)
    rs = _row_spec(tm, D)
    return pl.pallas_call(body, grid=(S // tm,), in_specs=pspecs + wspecs, out_specs=(rs, rs, rs),
                          out_shape=(_sds((S, D), BF16),) * 3, name="mla_prep", compiler_params=_PAR)(
        proj, proj, wqb, wkvb, qan, kvan, wq, wk, *rope)


def _mla_prep_bwd(proj, dqf, dkf, dve, wqb, wkvb, qan, kvan, wq, wk, rope, dproj, tm=256):
    S = proj.shape[0]

    def body(pq_ref, pkv_ref, wqb_ref, wkvb_ref, qan_ref, kvan_ref, wq_ref, wk_ref, c_ref, s1_ref, s2_ref,
             dqf_ref, dkf_ref, dve_ref, dproj_in, dlat_ref, dwqb_ref, dwkvb_ref, st_ref, dq_scr, dkv_scr):
        dqs_ref, dkvs_ref = dlat_ref.at[:, 0:384], dlat_ref.at[:, 384:768]
        @pl.when(pl.program_id(0) == 0)
        def _():
            dwqb_ref[...] = jnp.zeros_like(dwqb_ref)
            dwkvb_ref[...] = jnp.zeros_like(dwkvb_ref)
            st_ref[...] = jnp.zeros_like(st_ref)

        ql, ckv, kr, rq, rkv, nq, nkv, q, kv = _mla_latents(pq_ref, pkv_ref, wqb_ref, wkvb_ref, qan_ref, kvan_ref)
        C, S1, S2, wqv, wkv = c_ref[...], s1_ref[...], s2_ref[...], wq_ref[...], wk_ref[...]
        lane = _iota((tm, 128), 1)
        mn, mr = lane < 64, (lane >= 64) & (lane < 96)
        dwq = jnp.zeros((1, 128), F32)
        dwk = jnp.zeros((1, 128), F32)
        dykr = jnp.zeros((tm, 128), F32)
        for h in range(N_HEADS):
            sl = slice(h * 128, (h + 1) * 128)
            t = q[:, sl]
            rn = lax.rsqrt(_seg_sum(t * t, mn) / NOPE + EPS)
            rr = lax.rsqrt(_seg_sum(t * t, mr) / ROPE_DIM + EPS)
            scale = jnp.where(mn, rn, jnp.where(mr, rr, 0.0))
            that = t * scale
            du = _rope_t(dqf_ref[:, sl], C, S1, S2)
            dwq += jnp.sum(du * that, 0, keepdims=True)
            g = du * wqv
            gt = g * that
            dq_scr[:, sl] = scale * (g - that * jnp.where(mn, _seg_sum(gt, mn) / NOPE, _seg_sum(gt, mr) / ROPE_DIM))
            t = kv[:, sl]
            rn = lax.rsqrt(_seg_sum(t * t, mn) / NOPE + EPS)
            that = jnp.where(mn, t * rn, 0.0)
            dkf = dkf_ref[:, sl]
            dykr += jnp.where(mr, dkf, 0.0)
            dkn = jnp.where(mn, dkf, 0.0)
            dwk += jnp.sum(dkn * that, 0, keepdims=True)
            g = dkn * wkv
            dve = dve_ref[:, sl]
            dkv_scr[:, sl] = jnp.where(mn, rn * (g - that * (jnp.sum(g * that, -1, keepdims=True) / NOPE)),
                                       pltpu.roll(dve, 64, 1) if h % 2 == 0 else dve)
        rrk = lax.rsqrt(_seg_sum(kr * kr, mr) / ROPE_DIM + EPS)
        that = jnp.where(mr, kr * rrk, 0.0)
        dukr = jnp.where(mr, _rope_t(dykr, C, S1, S2), 0.0)
        dwk += jnp.sum(dukr * that, 0, keepdims=True)
        g = dukr * wkv
        dkr = rrk * (g - that * (jnp.sum(g * that, -1, keepdims=True) / ROPE_DIM))
        dqv, dkvv = dq_scr[...].astype(BF16), dkv_scr[...].astype(BF16)
        dnq = _dot(dqv, wqb_ref[...], _NT)
        dwqb_ref[...] += _dot(nq, dqv, _TN)
        dnkv = jnp.zeros((tm, KV_RANK), F32)
        for k in range(4):
            dnkv += _dot(dkvv[:, k * 256:(k + 1) * 256], wkvb_ref[k], _NT)
            dwkvb_ref[k] += _dot(nkv, dkvv[:, k * 256:(k + 1) * 256], _TN)
        xhat = ql * rq
        st_ref[0:1, 0:Q_RANK] += jnp.sum(dnq * xhat, 0, keepdims=True)
        g = dnq * qan_ref[...]
        dqs_ref[...] = (rq * (g - xhat * jnp.mean(g * xhat, -1, keepdims=True))).astype(BF16)
        xhat = ckv * rkv
        st_ref[1:2, 0:KV_RANK] += jnp.sum(dnkv * xhat, 0, keepdims=True)
        g = dnkv * kvan_ref[...]
        dkvs_ref[:, 0:KV_RANK] = (rkv * (g - xhat * jnp.mean(g * xhat, -1, keepdims=True))).astype(BF16)
        dkvs_ref[:, KV_RANK:KV_RANK + 128] = dkr.astype(BF16)
        st_ref[2:3, 0:128] += dwq
        st_ref[3:4, 0:128] += dwk

    pspecs, wspecs = _mla_specs(tm)
    rs = _row_spec(tm, D)
    full = lambda r, n: pl.BlockSpec((r, n), lambda i: (0, 0))
    return pl.pallas_call(
        body, grid=(S // tm,), in_specs=pspecs + wspecs + [rs, rs, rs, _ANY],
        out_specs=(pl.BlockSpec((tm, 768), lambda i: (i, OQ // 768)), full(Q_RANK, D), pl.BlockSpec((4, KV_RANK, 256), lambda i: (0, 0, 0)), full(8, D)),
        out_shape=(_sds(dproj.shape, BF16), _sds((Q_RANK, D)), _sds((4, KV_RANK, 256)), _sds((8, D))), input_output_aliases={14: 0},
        scratch_shapes=[pltpu.VMEM((tm, D), F32), pltpu.VMEM((tm, D), F32)], name="mla_prep_bwd", compiler_params=_ARB)(
        proj, proj, wqb, wkvb, qan, kvan, wq, wk, *rope, dqf, dkf, dve, dproj)


_ATT_SCALE = (NOPE + ROPE_DIM) ** -0.5


def _att_probs(q, k, i, tq):
    n = k.shape[0]
    s = _dot(q, k, _NT) * _ATT_SCALE
    tri = _iota((tq, tq), 1) <= _iota((tq, tq), 0)
    diag = jnp.where(tri, s[:, n - tq:], -1e30)
    s = diag if n == tq else jnp.concatenate([s[:, :n - tq], diag], 1)
    p = jnp.exp(s - jnp.max(s, -1, keepdims=True))
    return p * (1.0 / jnp.sum(p, -1, keepdims=True))


def _attn_fwd(qf, kf, ve, tq=512):
    S = qf.shape[0]
    tq = min(tq, S)

    def body(q_ref, k_ref, v_ref, o_ref):
        for i in range(S // tq):
            n, rows = (i + 1) * tq, slice(i * tq, (i + 1) * tq)
            acc = jnp.zeros((tq, 128), F32)
            for hh in range(2):
                sl = slice(hh * 128, (hh + 1) * 128)
                p = _att_probs(q_ref[rows, sl], k_ref[0:n, sl], i, tq)
                acc += _dot(p.astype(BF16), v_ref[0:n, sl], _NN)
            o_ref[rows, :] = acc.astype(BF16)

    ps = pl.BlockSpec((S, 256), lambda h: (0, h))
    return pl.pallas_call(body, grid=(N_HEADS // 2,), in_specs=[ps, ps, ps], out_specs=pl.BlockSpec((S, 128), lambda h: (0, h)),
                          out_shape=_sds((S, 512), BF16), name="attn_fwd", compiler_params=_PAR)(qf, kf, ve)


def _attn_bwd(qf, kf, ve, do, after, tq=512):
    S = qf.shape[0]
    tq = min(tq, S)

    def body(q_ref, k_ref, v_ref, do_ref, after_ref, dq_ref, dk_ref, dv_ref):
        dk_ref[...] = jnp.zeros_like(dk_ref)
        dv_ref[...] = jnp.zeros_like(dv_ref)
        for i in range(S // tq):
            n, rows = (i + 1) * tq, slice(i * tq, (i + 1) * tq)
            dob = do_ref[rows, :].astype(BF16)
            for hh in range(2):
                sl = slice(hh * 128, (hh + 1) * 128)
                q, k = q_ref[rows, sl], k_ref[0:n, sl]
                p = _att_probs(q, k, i, tq)
                dv_ref[0:n, sl] += _dot(p.astype(BF16), dob, _TN)
                dp = _dot(dob, v_ref[0:n, sl], _NT)
                ds = (p * (dp - jnp.sum(dp * p, -1, keepdims=True)) * _ATT_SCALE).astype(BF16)
                dq_ref[rows, sl] = _dot(ds, k, _NN)
                dk_ref[0:n, sl] += _dot(ds, q, _TN)

    ps = pl.BlockSpec((S, 256), lambda h: (0, h))
    return pl.pallas_call(body, grid=(N_HEADS // 2,), in_specs=[ps, ps, ps, pl.BlockSpec((S, 128), lambda h: (0, h)), _ANY],
                          out_specs=(ps, ps, ps), out_shape=(_sds((S, D)),) * 3, name="attn_bwd", compiler_params=_PAR)(qf, kf, ve, do, after)


def _softplus(x):
    return jnp.maximum(x, 0.0) + jnp.log1p(jnp.exp(-jnp.abs(x)))


def _ssd_chunk(xc_ref, dtr_ref, dtb_ref, al_ref, e_ref):
    L = SSD_L
    a = -jnp.exp(al_ref[...])
    dtp = _softplus(dtr_ref[...] + dtb_ref[...])
    causal = _iota((L, L), 1) <= _iota((L, L), 0)
    cs = _dot_hi(causal.astype(F32), dtp * a)
    E = e_ref[...]
    dtx, csx = _dot_hi(dtp, E), _dot_hi(cs, E)
    X = xc_ref[:, 0:SSD_INNER]
    Xd = X * dtx
    dec_out = jnp.exp(csx)
    dec_st = jnp.exp(csx[L - 1:L, :] - csx)
    return a, dtp, causal, cs, cs.T, dtx, X, Xd, dec_out, dec_st


def _ssd_decay(causal, cs, cs_row, h):
    diff = cs[:, h:h + 1] - cs_row[h:h + 1, :]
    return jnp.where(causal, jnp.exp(jnp.where(causal, diff, 0.0)), 0.0)


def _ssd_groups(xc_ref, g):
    b0, c0 = SSD_INNER + g * SSD_N, SSD_INNER + 2 * SSD_N + g * SSD_N
    return xc_ref[:, b0:b0 + SSD_N].astype(BF16), xc_ref[:, c0:c0 + SSD_N].astype(BF16)


def _pair_decay(cs, pair):
    L = SSD_L
    return jnp.where(_iota((128, 128), 0) < 64, jnp.exp(cs[L - 1:L, 2 * pair:2 * pair + 1]), jnp.exp(cs[L - 1:L, 2 * pair + 1:2 * pair + 2]))


def _ssd_in_specs(nc, rev):
    idx = (lambda c: nc - 1 - c) if rev else (lambda c: c)
    return [pl.BlockSpec((SSD_L, SSD_CONV_DIM), lambda c: (idx(c), 0)), pl.BlockSpec((SSD_L, 128), lambda c: (idx(c), ODT // 128)),
            _vec_spec(128), _vec_spec(128), _vec_spec(SSD_INNER), pl.BlockSpec((128, SSD_INNER), lambda c: (0, 0))]


def _ssd_core(xc, proj, dtb, alog, dskip, E):
    S = xc.shape[0]
    nc = S // SSD_L

    def body(xc_ref, dtr_ref, dtb_ref, al_ref, dx_ref, e_ref, y_ref, hp_ref, h_scr):
        @pl.when(pl.program_id(0) == 0)
        def _():
            h_scr[...] = jnp.zeros_like(h_scr)

        hp_ref[0] = h_scr[...]
        _, _, causal, cs, cs_row, _, X, Xd, dec_out, dec_st = _ssd_chunk(xc_ref, dtr_ref, dtb_ref, al_ref, e_ref)
        Xs = Xd * dec_st
        lane = _iota((SSD_L, 128), 1)
        for g in range(2):
            Bg, Cg = _ssd_groups(xc_ref, g)
            CB = _dot(Cg, Bg, _NT)
            for pr in range(4):
                pair = g * 4 + pr
                psl = slice(pair * 128, (pair + 1) * 128)
                Xdp = Xd[:, psl].astype(BF16)
                r0 = _dot((CB * _ssd_decay(causal, cs, cs_row, 2 * pair)).astype(BF16), Xdp, _NN)
                r1 = _dot((CB * _ssd_decay(causal, cs, cs_row, 2 * pair + 1)).astype(BF16), Xdp, _NN)
                Hp = h_scr[psl, :]
                W = _dot(Cg, Hp.astype(BF16), _NT)
                y_ref[:, psl] = jnp.where(lane < 64, r0, r1) + W * dec_out[:, psl] + X[:, psl] * dx_ref[:, psl]
                h_scr[psl, :] = Hp * _pair_decay(cs, pair) + _dot(Xs[:, psl].astype(BF16), Bg, _TN)

    return pl.pallas_call(
        body, grid=(nc,), in_specs=_ssd_in_specs(nc, False),
        out_specs=(pl.BlockSpec((SSD_L, SSD_INNER), lambda c: (c, 0)), pl.BlockSpec((1, SSD_INNER, SSD_N), lambda c: (c, 0, 0))),
        out_shape=(_sds((S, SSD_INNER)), _sds((nc, SSD_INNER, SSD_N))), scratch_shapes=[pltpu.VMEM((SSD_INNER, SSD_N), F32)],
        name="ssd_core", compiler_params=_ARB)(xc, proj, dtb, alog, dskip, E)


def _ssd_core_bwd(xc, proj, hprev, dy, dtb, alog, dskip, E, ET, dproj):
    S = xc.shape[0]
    nc = S // SSD_L
    L = SSD_L

    def body(xc_ref, dtr_ref, dtb_ref, al_ref, dx_ref, e_ref, et_ref, hp_ref, dy_ref, dproj_in, dxc_ref, ddt_ref, st_ref, dh_scr, acc_scr):
        step = pl.program_id(0)

        @pl.when(step == 0)
        def _():
            dh_scr[...] = jnp.zeros_like(dh_scr)
            acc_scr[...] = jnp.zeros_like(acc_scr)
            st_ref[...] = jnp.zeros_like(st_ref)

        a, dtp, causal, cs, cs_row, dtx, X, Xd, dec_out, dec_st = _ssd_chunk(xc_ref, dtr_ref, dtb_ref, al_ref, e_ref)
        lane = _iota((L, 128), 1)
        sub = _iota((128, L), 0)
        dcs_col = jnp.zeros((L, 128), F32)
        dcs_row = jnp.zeros((128, L), F32)
        dcs_last = jnp.zeros((1, 128), F32)
        dcsx, ddtx, dlastx = [], [], []
        for g in range(2):
            Bg, Cg = _ssd_groups(xc_ref, g)
            CB = _dot(Cg, Bg, _NT)
            dCB = jnp.zeros((L, L), F32)
            dB = jnp.zeros((L, SSD_N), F32)
            dC = jnp.zeros((L, SSD_N), F32)
            for pr in range(4):
                pair = g * 4 + pr
                psl = slice(pair * 128, (pair + 1) * 128)
                dY, Xp, Xdp, dop, dsp = dy_ref[:, psl], X[:, psl], Xd[:, psl], dec_out[:, psl], dec_st[:, psl]
                Xdb = Xdp.astype(BF16)
                acc_scr[0:1, psl] += jnp.sum(dY * Xp, 0, keepdims=True)
                Hp = hp_ref[0, psl, :]
                Hb = Hp.astype(BF16)
                dW = (dY * dop).astype(BF16)
                dcx = dY * _dot(Cg, Hb, _NT) * dop
                dC += _dot(dW, Hb, _NN)
                dHp = _dot(dW, Cg, _TN)
                dHn = dh_scr[psl, :]
                cd = _pair_decay(cs, pair)
                dh_scr[psl, :] = dHp + dHn * cd
                rsum = jnp.sum(dHn * Hp * cd, -1, keepdims=True)
                half = _iota((128, 1), 0) < 64
                s0 = jnp.sum(jnp.where(half, rsum, 0.0), 0, keepdims=True)
                s1 = jnp.sum(jnp.where(half, 0.0, rsum), 0, keepdims=True)
                lane1 = _iota((1, 128), 1)
                dcs_last += jnp.where(lane1 == 2 * pair, s0, 0.0) + jnp.where(lane1 == 2 * pair + 1, s1, 0.0)
                dHb = dHn.astype(BF16)
                dXs = _dot(Bg, dHb, _NT)
                dB += _dot((Xdp * dsp).astype(BF16), dHb, _NN)
                dXd = dXs * dsp
                e_st = dXs * Xdp * dsp
                dcx -= e_st
                dlastx.append(jnp.sum(e_st, 0, keepdims=True))
                for i in range(2):
                    h = 2 * pair + i
                    Dm = _ssd_decay(causal, cs, cs_row, h)
                    M = CB * Dm
                    dYm = jnp.where((lane < 64) if i == 0 else (lane >= 64), dY, 0.0).astype(BF16)
                    dM = _dot(dYm, Xdb, _NT)
                    dXd += _dot(M.astype(BF16), dYm, _TN)
                    dCB += dM * Dm
                    Em = dM * M
                    dcs_col += jnp.where(lane == h, jnp.sum(Em, -1, keepdims=True), 0.0)
                    dcs_row += jnp.where(sub == h, jnp.sum(Em, 0, keepdims=True), 0.0)
                dxc_ref[:, psl] = dY * dx_ref[:, psl] + dXd * dtx[:, psl]
                ddtx.append(dXd * Xp)
                dcsx.append(dcx)
            dCBb = dCB.astype(BF16)
            b0, c0 = SSD_INNER + g * SSD_N, SSD_INNER + 2 * SSD_N + g * SSD_N
            dxc_ref[:, b0:b0 + SSD_N] = dB + _dot(dCBb, Cg, _TN)
            dxc_ref[:, c0:c0 + SSD_N] = dC + _dot(dCBb, Bg, _NN)
        ET = et_ref[...]
        dcs = dcs_col - dcs_row.T + _dot_hi(jnp.concatenate(dcsx, 1), ET)
        dlast = dcs_last + _dot_hi(jnp.broadcast_to(jnp.concatenate(dlastx, 1), (8, SSD_INNER)), ET)[0:1, :]
        dcs += jnp.where(_iota((L, 128), 0) == L - 1, dlast, 0.0)
        dda = _dot_hi((_iota((L, L), 1) >= _iota((L, L), 0)).astype(F32), dcs)
        ddtp = dda * a + _dot_hi(jnp.concatenate(ddtx, 1), ET)
        draw = ddtp * _sigmoid(dtr_ref[...] + dtb_ref[...])
        ddt_ref[...] = draw.astype(BF16)
        st_ref[0:1, :] += jnp.sum(draw, 0, keepdims=True)
        st_ref[1:2, :] += jnp.sum(dda * dtp, 0, keepdims=True) * a

        @pl.when(step == nc - 1)
        def _():
            st_ref[2:3, :] = _dot_hi(acc_scr[...], ET)[0:1, :]

    rev = lambda c: (nc - 1 - c, 0)
    return pl.pallas_call(
        body, grid=(nc,),
        in_specs=_ssd_in_specs(nc, True) + [pl.BlockSpec((SSD_INNER, 128), lambda c: (0, 0)),
                                            pl.BlockSpec((1, SSD_INNER, SSD_N), lambda c: (nc - 1 - c, 0, 0)),
                                            pl.BlockSpec((L, SSD_INNER), rev), _ANY],
        out_specs=(pl.BlockSpec((L, SSD_CONV_DIM), rev), pl.BlockSpec((L, 128), lambda c: (nc - 1 - c, ODT // 128)),
                   pl.BlockSpec((8, 128), lambda c: (0, 0))),
        out_shape=(_sds((S, SSD_CONV_DIM)), _sds(dproj.shape, BF16), _sds((8, 128))), input_output_aliases={9: 1},
        scratch_shapes=[pltpu.VMEM((SSD_INNER, SSD_N), F32), pltpu.VMEM((8, SSD_INNER), F32)],
        name="ssd_core_bwd", compiler_params=_ARB)(xc, proj, dtb, alog, dskip, E, ET, hprev, dy, dproj)


def _ssd_post(y, proj, nw, tm=256):
    S = y.shape[0]

    def body(y_ref, z_ref, nw_ref, o_ref):
        for g in range(2):
            sl = slice(g * 512, (g + 1) * 512)
            gated = y_ref[:, sl] * _silu(z_ref[:, sl])
            r = lax.rsqrt(jnp.mean(gated * gated, -1, keepdims=True) + EPS)
            o_ref[:, sl] = (gated * r * nw_ref[:, sl]).astype(BF16)

    return pl.pallas_call(
        body, grid=(S // tm,), in_specs=[_row_spec(tm, D), pl.BlockSpec((tm, D), lambda i: (i, OZ // D)), _vec_spec(D)],
        out_specs=_row_spec(tm, D), out_shape=_sds((S, D), BF16), name="ssd_post", compiler_params=_PAR)(y, proj, nw)


def _ssd_post_bwd(doc, y, proj, nw, dproj, tm=256):
    S = y.shape[0]

    def body(d_ref, y_ref, z_ref, nw_ref, dproj_in, dy_ref, dz_ref, st_ref):
        @pl.when(pl.program_id(0) == 0)
        def _():
            st_ref[...] = jnp.zeros_like(st_ref)

        for g in range(2):
            sl = slice(g * 512, (g + 1) * 512)
            yv, zv, dv = y_ref[:, sl], z_ref[:, sl], d_ref[:, sl]
            sz = _silu(zv)
            gated = yv * sz
            r = lax.rsqrt(jnp.mean(gated * gated, -1, keepdims=True) + EPS)
            ghat = gated * r
            st_ref[0:1, sl] += jnp.sum(dv * ghat, 0, keepdims=True)
            gg = dv * nw_ref[:, sl]
            dg = r * (gg - ghat * jnp.mean(gg * ghat, -1, keepdims=True))
            dy_ref[:, sl] = dg * sz
            dz_ref[:, sl] = (dg * yv * _dsilu(zv)).astype(BF16)

    zs = pl.BlockSpec((tm, D), lambda i: (i, OZ // D))
    return pl.pallas_call(
        body, grid=(S // tm,), in_specs=[_row_spec(tm, D), _row_spec(tm, D), zs, _vec_spec(D), _ANY],
        out_specs=(_row_spec(tm, D), zs, _vec_spec(D, 8)), out_shape=(_sds((S, D)), _sds(dproj.shape, BF16), _sds((8, D))),
        input_output_aliases={4: 1}, name="ssd_post_bwd", compiler_params=_ARB)(doc, y, proj, nw, dproj)


def _row(v, n=None):
    v = v.astype(F32).reshape(1, -1)
    return v if n is None else jnp.pad(v, ((0, 0), (0, n - v.shape[1])))


_IN_SEGMENTS = [(0, 384, OQ), (384, 640, OKV), (640, 672, OKV + 320), (672, 1184, OP), (1184, 2208, OZ), (2208, 3744, OX),
                (3744, 3760, ODT), (3760, IN_DIM, OG)]
_IN_ZEROS = [(OKV + 256, OKV + 320), (OKV + 352, OKV + 384), (ODT + 16, ODT + 128)]


def _in_pieces():
    w, out = IN_DIM // 4, []
    for a, b, d in _IN_SEGMENTS:
        while a < b:
            k = a // w
            e = min(b, (k + 1) * w)
            out.append((k, a - k * w, e - k * w, d))
            d, a = d + e - a, e
    return out


def _win_layout(w_in4, tm=256):
    def body(w_ref, o_ref):
        for k, s0, s1, d in _in_pieces():
            o_ref[:, d:d + s1 - s0] = w_ref[k, :, s0:s1]
        for z0, z1 in _IN_ZEROS:
            o_ref[:, z0:z1] = jnp.zeros((tm, z1 - z0), o_ref.dtype)

    return pl.pallas_call(
        body, grid=(D // tm,), in_specs=[pl.BlockSpec((4, tm, IN_DIM // 4), lambda i: (0, i, 0))],
        out_specs=pl.BlockSpec((tm, IN_PAD), lambda i: (i, 0)), out_shape=_sds((D, IN_PAD), w_in4.dtype), name="win_layout",
        compiler_params=_PAR)(w_in4)


def _win_unlayout(dwin, tm=256):
    def body(d_ref, o_ref):
        for k, s0, s1, d in _in_pieces():
            o_ref[k, :, s0:s1] = d_ref[:, d:d + s1 - s0]

    return pl.pallas_call(
        body, grid=(D // tm,), in_specs=[pl.BlockSpec((tm, IN_PAD), lambda i: (i, 0))],
        out_specs=pl.BlockSpec((4, tm, IN_DIM // 4), lambda i: (0, i, 0)), out_shape=_sds((4, D, IN_DIM // 4), dwin.dtype),
        name="win_unlayout", compiler_params=_PAR)(dwin)


def _prep_late(p):
    return dict(wbr=p["w_branch"].astype(BF16), wo=p["w_out"].astype(BF16), wup4=p["ffn_up4"].astype(BF16), wdn=p["ffn_down"].astype(BF16))


def _prep_layer(p, late=None):
    win = _win_layout(p["w_in4"].astype(BF16))
    wqb = jnp.transpose(p["w_q_b4"].astype(BF16).reshape(4, Q_RANK, 2, 96), (1, 0, 2, 3))
    return dict(
        win=win, wqb=jnp.pad(wqb, ((0, 0), (0, 0), (0, 0), (0, 32))).reshape(Q_RANK, D), wkvb4=p["w_kv_b4"].astype(BF16),
        late=late if late is not None else (lambda after: _prep_late(p)),
        nw1=_row(p["norm1_w"]), nw2=_row(p["norm2_w"]), qan=_row(p["q_a_norm"]), kvan=_row(p["kv_a_norm"]),
        wq=_row(p["q_norm"], 128), wk=_row(p["k_norm"], 128), pool_w=p["pool_w"].astype(F32), pool_scale=_row(p["pool_scale"]),
        cw=p["ssd_conv_w"].astype(F32), cb=_row(p["ssd_conv_b"]), dtb=_row(p["ssd_dt_bias"], 128), alog=_row(p["ssd_a_log"], 128),
        dskip=_row(jnp.repeat(p["ssd_d"].astype(F32), SSD_P)), snw=_row(p["ssd_norm_w"]),
        fcw=p["ffn_conv_w"].astype(F32), fcb=_row(p["ffn_conv_b"]))


def _layer_fwd(x, mod8, W, rope, E, tag):
    sh1, sc1, g1, sh2, sc2, g2 = (mod8[i:i + 1] for i in range(6))
    h1 = _ln_mod(x, W["nw1"], sc1, sh1, name=f"ln1_{tag}")
    proj = _mm(h1, W["win"], tn=640, tk=1024, name=f"proj_{tag}")
    qf, kf, ve = _mla_prep(proj, W["wqb"], W["wkvb4"], W["qan"], W["kvan"], W["wq"], W["wk"], rope)
    oa = _attn_fwd(qf, kf, ve)
    ob = _pool_fwd(proj, W["pool_w"], W["pool_scale"])
    xc = _ssd_pre(proj, W["cw"], W["cb"])
    y, hprev = _ssd_core(xc, proj, W["dtb"], W["alog"], W["dskip"], E)
    oc = _ssd_post(y, proj, W["snw"])
    W.update(W["late"](oc))
    merged = _merge_fwd(oa, ob, oc, proj, W["wbr"])
    x1, out1 = _mm(merged, W["wo"], tk=1024, res=x, gate=g1, name=f"wout_{tag}")
    h2 = _ln_mod(x1, W["nw2"], sc2, sh2, name=f"ln2_{tag}")
    up = _up_fwd(h2, W["wup4"], name=f"up_{tag}")
    act = _ffn_act(up, W["fcw"], W["fcb"])
    x2, out2 = _mm(act, W["wdn"], tm=1024, res=x1, gate=g2, name=f"down_{tag}")
    saved = dict(x=x, h1=h1, proj=proj, qf=qf, kf=kf, ve=ve, oa=oa, ob=ob, oc=oc, xc=xc, hprev=hprev, y=y, merged=merged,
                 out1=out1, x1=x1, h2=h2, up=up, act=act, out2=out2)
    return x2, saved


def _layer_bwd(dx2, sv, mod8, W, rope, E, ET, tag, emit=None, mid=None):
    sc1, g1, sc2, g2 = mod8[1:2], mod8[2:3], mod8[4:5], mod8[5:6]
    proj = sv["proj"]
    dz2, dg2 = _gate_bwd(dx2, sv["out2"], g2, name=f"gate2_bwd_{tag}")
    dact = _mm(dz2, W["wdn"], "nt", tn=1408, tk=1024, name=f"down_dx_{tag}")
    dwdn = _mm(sv["act"], dz2, "tn", tm=1408, name=f"down_dw_{tag}")
    dup2, dfcw, dfcb = _ffn_act_bwd(sv["up"], dact, W["fcw"], W["fcb"])
    dh2 = _up_dx(dup2, W["wup4"], name=f"up_dx_{tag}")
    dwup4 = _up_dw(sv["h2"], dup2, name=f"up_dw_{tag}")
    dx1, st2 = _ln_mod_bwd(sv["x1"], dh2, dx2, W["nw2"], sc2, name=f"ln2_bwd_{tag}")
    dz1, dg1 = _gate_bwd(dx1, sv["out1"], g1, name=f"gate1_bwd_{tag}")
    dmerged = _mm(dz1, W["wo"], "nt", tk=1024, name=f"wout_dx_{tag}")
    dwo = _mm(sv["merged"], dz1, "tn", name=f"wout_dw_{tag}")
    dya, dyb, dyc, dproj, doa, dob, doc = _merge_bwd(dmerged, sv["oa"], sv["ob"], sv["oc"], proj, W["wbr"])
    dwbr = _branch_dw(sv["oa"], dya, 0, None, f"wba_dw_{tag}")
    dwbr = _branch_dw(sv["ob"], dyb, 512, dwbr, f"wbb_dw_{tag}")
    dwbr = _branch_dw(sv["oc"], dyc, 1024, dwbr, f"wbc_dw_{tag}")
    late = dict(w_branch=dwbr.reshape(4, 512, D), w_out=dwo.reshape(4, 256, D), ffn_up=dwup4, ffn_down=dwdn.reshape(4, FFN // 4, D))
    snw, tie = W["snw"], jnp.zeros((8, 128), F32)
    if emit is not None:
        token = emit(late)
        if token is not None:
            snw, tie = snw + token, tie + token
    dy, dproj, st_post = _ssd_post_bwd(doc, sv["y"], proj, snw, dproj)
    dxc, dproj, st_ssd = _ssd_core_bwd(sv["xc"], proj, sv["hprev"], dy, W["dtb"], W["alog"], W["dskip"], E, ET, dproj)
    dproj, dcw, dcb = _ssd_pre_bwd(proj, dxc, W["cw"], W["cb"], dproj)
    if mid is not None:
        token = mid(dcb)
        if token is not None:
            tie = tie + token
    dproj, dpw, dps = _pool_bwd(proj, dob, W["pool_w"], W["pool_scale"], dproj, tie)
    dqf, dkf, dve = _attn_bwd(sv["qf"], sv["kf"], sv["ve"], doa, tie)
    dproj, dwqb, dwkvb4, st_mla = _mla_prep_bwd(proj, dqf, dkf, dve, W["wqb"], W["wkvb4"], W["qan"], W["kvan"], W["wq"], W["wk"], rope, dproj)
    dh1 = _mm(dproj, W["win"], "nt", tk=1408, name=f"proj_dx_{tag}")
    dwin = _mm(sv["h1"], dproj, "tn", tn=640, name=f"proj_dw_{tag}")
    dx, st1 = _ln_mod_bwd(sv["x"], dh1, dx1, W["nw1"], sc1, name=f"ln1_bwd_{tag}")
    grads = dict(
        norm1_w=st1[2], norm2_w=st2[2], w_in=_win_unlayout(dwin),
        q_a_norm=st_mla[0, :Q_RANK], kv_a_norm=st_mla[1, :KV_RANK], q_norm=st_mla[2, :96], k_norm=st_mla[3, :96],
        w_q_b=jnp.transpose(dwqb.reshape(Q_RANK, 4, 2, 128)[:, :, :, :96], (1, 0, 2, 3)).reshape(4, Q_RANK, 192), w_kv_b=dwkvb4,
        pool_w=dpw, pool_scale=dps[0], ssd_conv_w=dcw, ssd_conv_b=dcb[0],
        ssd_dt_bias=st_ssd[0, :SSD_HEADS], ssd_a_log=st_ssd[1, :SSD_HEADS], ssd_d=st_ssd[2, :SSD_HEADS], ssd_norm_w=st_post[0],
        ffn_conv_w=jnp.transpose(dfcw, (1, 0, 2)).reshape(3, 2 * FFN), ffn_conv_b=dfcb.reshape(2 * FFN), **late)
    dmod = jnp.concatenate([st1[0:2], dg1[0:1], st2[0:2], dg2[0:1]], 0)
    return dx, grads, dmod


def _ssd_expand():
    E = (jnp.arange(SSD_INNER)[None, :] // SSD_P == jnp.arange(128)[:, None]).astype(F32)
    return E, E.T


def _rope_tables(positions):
    inv_freq = ROPE_THETA ** (-jnp.arange(0, ROPE_DIM, 2, dtype=F32) / ROPE_DIM)
    invf = jnp.concatenate([jnp.zeros((NOPE,), F32), inv_freq, inv_freq, jnp.zeros((32,), F32)]).reshape(1, 128)
    posb = jnp.broadcast_to(positions.astype(F32)[:, None], (positions.shape[0], 128))
    return _rope_tab(posb, invf)


def _local_step(x, target, positions, mods, get_layer, bwd_mod=None, emit=None, mid=None, done=None):
    rope = _rope_tables(positions)
    E, ET = _ssd_expand()
    Ws, saved, h = [], [], x
    for l in range(2):
        Ws.append(_prep_layer(*get_layer(l, h)))
        h, sv = _layer_fwd(h, mods[l], Ws[l], rope, E, l)
        saved.append(sv)
    dy, lpart = _loss_grad(h, target)
    grads, dmods = [None, None], [None, None]
    for l in (1, 0):
        mod8 = mods[l] if bwd_mod is None else bwd_mod(l)
        hook = lambda f: None if f is None else functools.partial(f, l)
        dy, grads[l], dmods[l] = _layer_bwd(dy, saved[l], mod8, Ws[l], rope, E, ET, l, hook(emit), hook(mid))
        if done is not None:
            done(l, grads[l])
    return lpart[0, 0], dy, grads, dmods


_ANY = pl.BlockSpec(memory_space=pl.ANY)
_VMEM = pl.BlockSpec(memory_space=pltpu.VMEM)


def _place():
    x, y, c = lax.axis_index("x"), lax.axis_index("y"), lax.axis_index("c")
    return x, y, c, [(1 - x, y), (x, 1 - y), (1 - x, 1 - y)]


def _allgather8(v, name):
    m_per, n = v.shape

    def body(x_ref, out_ref, send_sems, recv_sems, local_sem):
        x, y, c, chips = _place()
        me, sibling = (x, y, c), (x, y, 1 - c)

        def rows(px, py, pc):
            return out_ref.at[pl.ds((4 * px + 2 * py + pc) * m_per, m_per), :]

        def copy(k, block, to, src=None):
            return pltpu.make_async_remote_copy(src_ref=rows(*block) if src is None else src, dst_ref=rows(*block),
                                                send_sem=send_sems.at[k], recv_sem=recv_sems.at[k], device_id=to, device_id_type=MESH)

        mine = pltpu.make_async_copy(x_ref, rows(*me), local_sem)
        mine.start()
        first = [copy(0, me, sibling, src=x_ref)] + [copy(1 + j, me, (*chip, c), src=x_ref) for j, chip in enumerate(chips)]
        for cp in first:
            cp.start()
        passed = [copy(4 + j, (*chip, c), sibling) for j, chip in enumerate(chips)]
        for j, chip in enumerate(chips):
            copy(1 + j, (*chip, c), me).wait_recv()
            passed[j].start()
        copy(0, sibling, me).wait_recv()
        for j, chip in enumerate(chips):
            copy(4 + j, (*chip, 1 - c), me).wait_recv()
        for cp in first + passed:
            cp.wait_send()
        mine.wait()

    return pl.pallas_call(
        body, out_shape=_sds((8 * m_per, n), v.dtype), in_specs=[_VMEM], out_specs=_VMEM,
        scratch_shapes=[pltpu.SemaphoreType.DMA((7,)), pltpu.SemaphoreType.DMA((7,)), pltpu.SemaphoreType.DMA], name=name)(v)


def _sems(n):
    return [pltpu.SemaphoreType.DMA((n,)), pltpu.SemaphoreType.DMA((n,))]


_HBM = pl.BlockSpec(memory_space=pltpu.HBM)
_SEM = pl.BlockSpec(memory_space=pltpu.SEMAPHORE)
_EFFECT = pltpu.CompilerParams(has_side_effects=pltpu.SideEffectType.DATAFLOW_SIDE_EFFECTING)


def _ici_copy(src_refs, land_refs, send_sems, recv_sems, a, j, slices, incoming):
    x, y, c, chips = _place()
    if _peers(slices) == 1:
        src, dst = slices(src_refs[a], land_refs[a], 1 - c if incoming else c)
        return pltpu.make_async_remote_copy(src_ref=src, dst_ref=dst, send_sem=send_sems.at[a], recv_sem=recv_sems.at[a],
                                            device_id=(x, y, 1 - c), device_id_type=MESH)
    me, other = 2 * x + y, 2 * chips[j][0] + chips[j][1]
    src, dst = slices(src_refs[a], land_refs[a], other, me, c) if incoming else slices(src_refs[a], land_refs[a], me, other, c)
    return pltpu.make_async_remote_copy(src_ref=src, dst_ref=dst, send_sem=send_sems.at[3 * a + j], recv_sem=recv_sems.at[3 * a + j],
                                        device_id=(*chips[j], c), device_id_type=MESH)


def _peers(slices):
    return 1 if slices is _halves_slices else 3


def _ici_start(srcs, land_shapes, slices, after, name):
    na = len(srcs)

    def body(*refs):
        src_refs, land_refs, send_sems, recv_sems = refs[:na], refs[na:2 * na], refs[2 * na + 1], refs[2 * na + 2]
        for a in range(na):
            for j in range(_peers(slices)):
                _ici_copy(src_refs, land_refs, send_sems, recv_sems, a, j, slices, False).start()
        refs[-1][...] = jnp.zeros_like(refs[-1])

    hbm = lambda v: pltpu.with_memory_space_constraint(v, pltpu.HBM)
    lands = [hbm(lax.empty(s.shape, s.dtype)) for s in land_shapes]
    return pl.pallas_call(
        body, name=name,
        out_shape=(pltpu.SemaphoreType.DMA((_peers(slices) * na,)), pltpu.SemaphoreType.DMA((_peers(slices) * na,)),
                   *[pltpu.HBM(v.shape, v.dtype) for v in srcs],
                   *[pltpu.HBM(s.shape, s.dtype) for s in land_shapes], _sds((8, 128))),
        in_specs=[_HBM] * (2 * na) + [_ANY], out_specs=(_SEM, _SEM, *[_HBM] * (2 * na), _VMEM),
        input_output_aliases={i: 2 + i for i in range(2 * na)}, compiler_params=_EFFECT)(*[hbm(v) for v in srcs], *lands, after)


def _ici_wait(handle, slices, after, name):
    na = (len(handle) - 3) // 2

    def body(*refs):
        src_refs, land_refs, send_sems, recv_sems = refs[:na], refs[na:2 * na], refs[2 * na], refs[2 * na + 1]
        for a in range(na):
            for j in range(_peers(slices)):
                _ici_copy(src_refs, land_refs, send_sems, recv_sems, a, j, slices, False).wait_send()
                _ici_copy(src_refs, land_refs, send_sems, recv_sems, a, j, slices, True).wait_recv()

    thru = handle[2:2 + 2 * na]
    outs = pl.pallas_call(
        body, name=name, out_shape=[pltpu.HBM(v.shape, v.dtype) for v in thru], in_specs=[_HBM] * (2 * na) + [_SEM, _SEM, _ANY],
        out_specs=[_HBM] * (2 * na), input_output_aliases={i: i for i in range(2 * na)}, compiler_params=_EFFECT)(
        *thru, handle[0], handle[1], after)
    return outs[:na], outs[na:]


def _gather_slices(p_ref, land_ref, sender, receiver, c):
    r2 = p_ref.shape[0] // 2
    return p_ref.at[pl.ds(c * r2, r2), :], land_ref.at[sender, pl.ds(c * r2, r2), :]


def _scatter_slices(a_ref, t_ref, sender, receiver, c):
    return a_ref.at[receiver], t_ref.at[sender]


def _halves_slices(g_ref, land_ref, sender_c):
    r2 = g_ref.shape[1] // 2
    return g_ref.at[:, pl.ds((1 - sender_c) * r2, r2), :], land_ref


def _gather_start(arrs, after, name):
    return _ici_start(arrs, [_sds((4,) + v.shape, v.dtype) for v in arrs], _gather_slices, after, name)


def _gather_finish(handle, after, name):
    arrs, stacks = _ici_wait(handle, _gather_slices, after, name + "_wait")
    na = len(stacks)

    def body(*refs):
        s_refs, o_refs, (send_sems, recv_sems) = refs[:na], refs[na:2 * na], refs[2 * na:]
        x, y, c, chips = _place()

        def copy(a, j, cc, to):
            r2 = s_refs[a].shape[1] // 2
            at = (2 * chips[j][0] + chips[j][1], pl.ds(cc * r2, r2), slice(None))
            return pltpu.make_async_remote_copy(src_ref=s_refs[a].at[at], dst_ref=o_refs[a].at[at], send_sem=send_sems.at[3 * a + j],
                                                recv_sem=recv_sems.at[3 * a + j], device_id=to, device_id_type=MESH)

        passed = [copy(a, j, c, (x, y, 1 - c)) for a in range(na) for j in range(3)]
        for cp in passed:
            cp.start()
        for a in range(na):
            for j in range(3):
                copy(a, j, 1 - c, (x, y, c)).wait_recv()
        for cp in passed:
            cp.wait_send()

    stacks = pl.pallas_call(
        body, out_shape=[_sds(v.shape, v.dtype) for v in stacks], in_specs=[_ANY] * na, out_specs=[_ANY] * na,
        input_output_aliases={i: i for i in range(na)}, scratch_shapes=_sems(3 * na), name=name + "_pass")(*stacks)
    chip = 2 * lax.axis_index("x") + lax.axis_index("y")
    return [lax.dynamic_update_slice(s, v[None], (chip, 0, 0)) for s, v in zip(stacks, arrs)]


def _halves_start(gs, after, tag):
    return _ici_start(gs, [_sds((4, v.shape[1] // 2, v.shape[2]), v.dtype) for v in gs], _halves_slices, after, f"rs_halves_{tag}_start")


def _join_halves(fs, name):
    na = len(fs)

    def body(*refs):
        f_refs, o_refs, (send_sems, recv_sems) = refs[:na], refs[na:2 * na], refs[2 * na:]
        x, y, c, _ = _place()

        def copy(a, cc, to):
            r2 = f_refs[a].shape[0] // 2
            return pltpu.make_async_remote_copy(src_ref=f_refs[a].at[pl.ds(cc * r2, r2), :], dst_ref=o_refs[a].at[pl.ds(cc * r2, r2), :],
                                                send_sem=send_sems.at[a], recv_sem=recv_sems.at[a], device_id=to, device_id_type=MESH)

        cps = [copy(a, c, (x, y, 1 - c)) for a in range(na)]
        for cp in cps:
            cp.start()
        for a in range(na):
            copy(a, 1 - c, (x, y, c)).wait_recv()
        for cp in cps:
            cp.wait_send()

    return pl.pallas_call(
        body, out_shape=[_sds(v.shape, v.dtype) for v in fs], in_specs=[_ANY] * na, out_specs=[_ANY] * na,
        input_output_aliases={i: i for i in range(na)}, scratch_shapes=_sems(na), name=name)(*fs)


def _sum_chips(a, t, chip, ci, name):
    _, r2, n = t.shape
    tm = _row_tile(r2)
    nb = r2 // tm

    def body(k_ref, a_ref, t1_ref, t2_ref, t3_ref, o_ref):
        o_ref[...] = ((a_ref[...].astype(F32) + t1_ref[...].astype(F32)) + t2_ref[...].astype(F32)) + t3_ref[...].astype(F32)

    def slot(j):
        return pl.BlockSpec((None, tm, n), lambda i, k_ref: (lax.rem(k_ref[0] + j, 4), i, 0))

    return pl.pallas_call(
        body, grid_spec=pltpu.PrefetchScalarGridSpec(num_scalar_prefetch=1, grid=(nb,), in_specs=[slot(0), slot(1), slot(2), slot(3)],
                                                     out_specs=pl.BlockSpec((tm, n), lambda i, k_ref: (k_ref[1] * nb + i, 0))),
        out_shape=_sds((2 * r2, n)), name=name, compiler_params=_PAR)(jnp.stack([chip, ci]).astype(jnp.int32), a, t, t, t)


def _add_cast(g, recv, c, name):
    _, r2, n = recv.shape

    def body(c_ref, a_ref, b_ref, o_ref):
        o_ref[...] = (a_ref[...] + b_ref[...]).astype(BF16)

    spec = pl.BlockSpec((None, r2, n), lambda k, c_ref: (k, 0, 0))
    return pl.pallas_call(
        body, grid_spec=pltpu.PrefetchScalarGridSpec(
            num_scalar_prefetch=1, grid=(4,), in_specs=[pl.BlockSpec((None, r2, n), lambda k, c_ref: (k, c_ref[0], 0)), spec], out_specs=spec),
        out_shape=_sds(recv.shape, BF16), name=name, compiler_params=_PAR)(c.reshape(1).astype(jnp.int32), g, recv)


def _sum_lead(t, name, tm=256):
    P, R, n = t.shape
    tm = _row_tile(R, tm)

    def body(t_ref, o_ref):
        acc = t_ref[0].astype(F32)
        for j in range(1, P):
            acc = acc + t_ref[j].astype(F32)
        o_ref[...] = acc

    return pl.pallas_call(body, grid=(R // tm,), in_specs=[pl.BlockSpec((P, tm, n), lambda i: (0, i, 0))],
                          out_specs=pl.BlockSpec((tm, n), lambda i: (i, 0)), out_shape=_sds((R, n)), name=name, compiler_params=_PAR)(t)


def _ada_fwd(c16, ada_w, ada_b_cols, tn=512):
    L, _, n = ada_w.shape

    def body(c_ref, w_ref, b_ref, o_ref):
        o_ref[0] = _dot(_silu(c_ref[...]).astype(BF16), w_ref[0].astype(BF16), _NN) + b_ref[0]

    return pl.pallas_call(
        body, grid=(L, n // tn),
        in_specs=[pl.BlockSpec((16, D), lambda l, j: (0, 0)), pl.BlockSpec((1, D, tn), lambda l, j: (l, 0, j)), pl.BlockSpec((1, 1, tn), lambda l, j: (l, 0, j))],
        out_specs=pl.BlockSpec((1, 16, tn), lambda l, j: (l, 0, j)), out_shape=_sds((L, 16, n)), name="ada_fwd",
        compiler_params=pltpu.CompilerParams(dimension_semantics=("parallel", "parallel")))(c16, ada_w, ada_b_cols)


def _ada_bwd(c16, dmod, tn=512):
    L, _, n = dmod.shape

    def body(c_ref, d_ref, o_ref):
        o_ref[0] = _dot(_silu(c_ref[...]).astype(BF16), d_ref[0].astype(BF16), _TN)

    return pl.pallas_call(
        body, grid=(L, n // tn), in_specs=[pl.BlockSpec((16, D), lambda l, j: (0, 0)), pl.BlockSpec((1, 16, tn), lambda l, j: (l, 0, j))],
        out_specs=pl.BlockSpec((1, D, tn), lambda l, j: (l, 0, j)), out_shape=_sds((L, D, n)), name="ada_bwd",
        compiler_params=pltpu.CompilerParams(dimension_semantics=("parallel", "parallel")))(c16, dmod)


def _adam_math(w, g, m, v):
    mn = ADAM_B1 * m + (1.0 - ADAM_B1) * g
    vn = ADAM_B2 * v + (1.0 - ADAM_B2) * (g * g)
    m_hat = mn / (1.0 - ADAM_B1 ** ADAM_STEP)
    v_hat = vn / (1.0 - ADAM_B2 ** ADAM_STEP)
    return -ADAM_LR * (m_hat / (jnp.sqrt(v_hat) + ADAM_EPS) + ADAM_WD * w), mn, vn


def _adamw(w, g, m, v, name):
    R, n = w.shape
    tm = _row_tile(R)

    def body(w_ref, g_ref, m_ref, v_ref, d_ref, nm_ref, nv_ref):
        d_ref[...], nm_ref[...], nv_ref[...] = _adam_math(w_ref[...], g_ref[...], m_ref[...], v_ref[...])

    spec = pl.BlockSpec((tm, n), lambda i: (i, 0))
    return pl.pallas_call(body, grid=(R // tm,), in_specs=[spec] * 4, out_specs=(spec,) * 3, out_shape=(_sds((R, n)),) * 3,
                          name=name, compiler_params=_PAR)(w, g, m, v)


def _adamw_cols(w, g0, g1, m, v, name):
    fwd, back = (lambda t: jnp.transpose(t, (2, 0, 1))), (lambda t: jnp.transpose(t, (1, 2, 0)))
    gt = jnp.stack([g0.T, g1.T], 1)
    n, _, r = gt.shape
    tr = max(t for t in range(1, 257) if n % t == 0)

    def body(w_ref, g_ref, m_ref, v_ref, d_ref, nm_ref, nv_ref):
        d_ref[...], nm_ref[...], nv_ref[...] = _adam_math(w_ref[...], g_ref[...], m_ref[...], v_ref[...])

    spec = pl.BlockSpec((tr, 2, r), lambda i: (i, 0, 0))
    outs = pl.pallas_call(body, grid=(n // tr,), in_specs=[spec] * 4, out_specs=(spec,) * 3, out_shape=(_sds(gt.shape),) * 3,
                          name=name, compiler_params=_PAR)(fwd(w), gt, fwd(m), fwd(v))
    return (back(gt), *[back(o) for o in outs])


def _adamw_layers(w, g0, g1, m, v, after, name):
    _, r, n = w.shape
    tm = _row_tile(r)
    nb = r // tm

    def body(w_ref, g0_ref, g1_ref, m_ref, v_ref, after_ref, g_ref, d_ref, nm_ref, nv_ref):
        gv = jnp.where(pl.program_id(0) == 0, g0_ref[...], g1_ref[...])
        g_ref[...] = gv
        d_ref[...], nm_ref[...], nv_ref[...] = _adam_math(w_ref[...], gv, m_ref[...], v_ref[...])

    spec = pl.BlockSpec((None, tm, n), lambda l, i: (l, i, 0))
    g0_spec = pl.BlockSpec((tm, n), lambda l, i: (i * (1 - l) + (nb - 1) * l, 0))
    g1_spec = pl.BlockSpec((tm, n), lambda l, i: (i * l, 0))
    return pl.pallas_call(body, grid=(2, nb), in_specs=[spec, g0_spec, g1_spec, spec, spec, _ANY], out_specs=(spec,) * 4,
                          out_shape=(_sds(w.shape),) * 4, name=name,
                          compiler_params=pltpu.CompilerParams(dimension_semantics=("arbitrary", "arbitrary")))(w, g0, g1, m, v, after)


_W_NAMES = ["ada_w", "ada_b", "norm1_w", "w_in", "q_a_norm", "w_q_b", "kv_a_norm", "w_kv_b", "q_norm", "k_norm", "pool_w",
            "pool_scale", "ssd_conv_w", "ssd_conv_b", "ssd_dt_bias", "ssd_a_log", "ssd_d", "ssd_norm_w", "w_branch", "w_out",
            "norm2_w", "ffn_up", "ffn_conv_w", "ffn_conv_b", "ffn_down"]
_BIG = [("w_in", (D, IN_DIM // 4), 1), ("w_q_b", (Q_RANK, 192), 1), ("w_kv_b", (KV_RANK, 256), 1), ("w_branch", (512, D), 0),
        ("w_out", (256, D), 0), ("ffn_up", (D, 2 * FFN // 4), 1), ("ffn_down", (FFN // 4, D), 0)]

_SMALL = [("norm1_w", (D,)), ("q_a_norm", (Q_RANK,)), ("kv_a_norm", (KV_RANK,)), ("q_norm", (96,)), ("k_norm", (96,)),
          ("pool_w", (4, 128, 128)), ("pool_scale", (512,)), ("ssd_conv_w", (4, SSD_CONV_DIM)), ("ssd_conv_b", (SSD_CONV_DIM,)),
          ("ssd_dt_bias", (SSD_HEADS,)), ("ssd_a_log", (SSD_HEADS,)), ("ssd_d", (SSD_HEADS,)), ("ssd_norm_w", (D,)), ("norm2_w", (D,)),
          ("ffn_conv_w", (3, 2 * FFN)), ("ffn_conv_b", (2 * FFN,))]
_CONV_SHARDED = {"ssd_conv_w": SSD_CONV_DIM // 4, "ffn_conv_w": 2 * FFN // 4}


def _pack_flat(arrs, mult):
    flat = jnp.concatenate([a.astype(F32).reshape(-1) for a in arrs])
    rows = -(-flat.shape[0] // (128 * mult)) * mult
    return jnp.pad(flat, (0, rows * 128 - flat.shape[0])).reshape(rows, 128), [a.shape for a in arrs]


def _unpack_flat(packed, shapes):
    flat, out, off = packed.reshape(-1), [], 0
    for s in shapes:
        n = 1
        for d in s:
            n *= d
        out.append(flat[off:off + n].reshape(s))
        off += n
    return out


_EARLY = ["w_in", "w_q_b", "w_kv_b"]
_LATE = ["w_branch", "w_out", "ffn_up", "ffn_down"]


def _early_weights(a, l, stacks, conv_full):
    p = {n: a[n][l] for n in _W_NAMES if n not in ("ada_w", "ada_b")}
    p.update({n: conv_full[n][l] for n in conv_full})
    p.update(w_in4=stacks[0], w_q_b4=stacks[1], w_kv_b4=stacks[2])
    return p


def _late_weights(stacks):
    return _prep_late(dict(w_branch=stacks[0].reshape(2048, D), w_out=stacks[1].reshape(D, D), ffn_up4=stacks[2],
                           ffn_down=stacks[3].reshape(FFN, D)))


def _reduce_start(halves, ci, after, tag):
    chip_sum = []
    for t, h in halves.items():
        gs, recv = _ici_wait(h, _halves_slices, after, f"rs_halves_{t}_wait")
        chip_sum += [_add_cast(g, r, ci, f"rs_add_{t}") for g, r in zip(gs, recv)]
    return _ici_start(chip_sum, [_sds(v.shape, v.dtype) for v in chip_sum], _scatter_slices, after, f"rs_scatter_{tag}_start")


def _reduce_finish(started, after, tag):
    chip_sum, got = _ici_wait(started, _scatter_slices, after, f"rs_scatter_{tag}_wait")
    chip, ci = 2 * lax.axis_index("x") + lax.axis_index("y"), lax.axis_index("c")
    return _join_halves([_sum_chips(s, t, chip, ci, f"rs_sum_{tag}") for s, t in zip(chip_sum, got)], f"rs_join_{tag}")


def kernel(x, c, positions, ada_w, ada_b, norm1_w, w_in, q_a_norm, w_q_b, kv_a_norm, w_kv_b, q_norm, k_norm, pool_w, pool_scale, ssd_conv_w, ssd_conv_b, ssd_dt_bias, ssd_a_log, ssd_d, ssd_norm_w, w_branch, w_out, norm2_w, ffn_up, ffn_conv_w, ffn_conv_b, ffn_down, loss_target, m_ada_w, m_ada_b, m_norm1_w, m_w_in, m_q_a_norm, m_w_q_b, m_kv_a_norm, m_w_kv_b, m_q_norm, m_k_norm, m_pool_w, m_pool_scale, m_ssd_conv_w, m_ssd_conv_b, m_ssd_dt_bias, m_ssd_a_log, m_ssd_d, m_ssd_norm_w, m_w_branch, m_w_out, m_norm2_w, m_ffn_up, m_ffn_conv_w, m_ffn_conv_b, m_ffn_down, v_ada_w, v_ada_b, v_norm1_w, v_w_in, v_q_a_norm, v_w_q_b, v_kv_a_norm, v_w_kv_b, v_q_norm, v_k_norm, v_pool_w, v_pool_scale, v_ssd_conv_w, v_ssd_conv_b, v_ssd_dt_bias, v_ssd_a_log, v_ssd_d, v_ssd_norm_w, v_w_branch, v_w_out, v_norm2_w, v_ffn_up, v_ffn_conv_w, v_ffn_conv_b, v_ffn_down):
    a = dict(locals())
    xi, yi, ci = lax.axis_index("x"), lax.axis_index("y"), lax.axis_index("c")
    chip = 2 * xi + yi
    dev = 2 * chip + ci
    ncol = 6 * D // 4

    c_all = _allgather8(c.reshape(8, 128), "gather_c").reshape(8, D)
    c16 = jnp.pad(c_all, ((0, 8), (0, 0)))
    ada_b_cols = lax.dynamic_slice_in_dim(ada_b, chip * ncol, ncol, axis=1).reshape(2, 1, ncol)
    mod_part = _ada_fwd(c16, ada_w, ada_b_cols)[:, :8]
    small1, shapes1 = _pack_flat([mod_part, ssd_conv_w, ffn_conv_w], 8)
    got1 = _allgather8(small1, "gather_mod").reshape(8, -1, 128)
    per_chip = [_unpack_flat(got1[2 * k], shapes1) for k in range(4)]
    mod_all = jnp.concatenate([per_chip[k][0] for k in range(4)], -1)
    conv_full = {"ssd_conv_w": jnp.concatenate([per_chip[k][1] for k in range(4)], -1),
                 "ffn_conv_w": jnp.concatenate([per_chip[k][2] for k in range(4)], -1)}
    mod_mine = lax.dynamic_index_in_dim(mod_all, dev, axis=1, keepdims=False).reshape(2, 6, D)
    mods = [jnp.pad(mod_mine[l], ((0, 2), (0, 0))) for l in range(2)]

    big = _EARLY + _LATE
    shard = lambda names, l: [a[n][l].astype(BF16) for n in names]
    g0a = _gather_start(shard(_EARLY, 0), mods[0], "gather_0a")
    g0b = _gather_start(shard(_LATE, 0), g0a[-1], "gather_0b")
    g1 = _gather_start(shard(big, 1), g0b[-1], "gather_1")
    mods[0] = mods[0] + g1[-1][0, 0]

    def get_layer(l, after):
        if l == 0:
            return (_early_weights(a, 0, _gather_finish(g0a, mods[0], "gather_0a"), conv_full),
                    lambda aft: _late_weights(_gather_finish(g0b, aft, "gather_0b")))
        stacks = _gather_finish(g1, after, "gather_1")
        return _early_weights(a, 1, stacks[:3], conv_full), lambda aft: _late_weights(stacks[3:])

    halves, scatters = {}, {}

    def bwd_mod(l):
        return mods[l] if l == 1 else mods[0] + halves["1a"][-1][0, 0]

    def emit(l, late):
        if l == 1:
            halves["1b"] = _halves_start([late[n] for n in _LATE], late["ffn_down"], "1b")
            return halves["1b"][-1][0, 0]
        scatters["1"] = _reduce_start({"1b": halves["1b"], "1a": halves["1a"]}, ci, late["ffn_down"], "1")
        halves["0b"] = _halves_start([late[n] for n in _LATE], scatters["1"][-1], "0b")
        return halves["0b"][-1][0, 0]

    def mid(l, after):
        if l == 0:
            scatters["0b"] = _reduce_start({"0b": halves["0b"]}, ci, after, "0b")
            return scatters["0b"][-1][0, 0]

    def done(l, grads_l):
        halves[f"{l}a"] = _halves_start([grads_l[n] for n in _EARLY], halves[f"{l}b"][-1], f"{l}a")

    lpart, grad_x, grads, dmods = _local_step(x[0], loss_target[0], positions[0], mods, get_layer, bwd_mod, emit, mid, done)
    loss = lax.psum(lpart, ("x", "y", "c"))
    behind = grad_x[:1, :128] + halves["0a"][-1][:1]
    red1 = dict(zip(_LATE + _EARLY, _reduce_finish(scatters["1"], behind, "1")))
    red0b = _reduce_finish(scatters["0b"], red1["w_in"], "0b")
    scatters["0a"] = _reduce_start({"0a": halves["0a"]}, ci, red0b[0], "0a")

    small2, shapes2 = _pack_flat([jnp.stack(dmods)] + [grads[l][n] for l in range(2) for n, _ in _SMALL], 16)
    got2 = _allgather8(small2, "gather_small").reshape(8, -1, 128)
    tot = _unpack_flat(_sum_lead(got2, "sum_small"), shapes2)
    g = {"ada_b": tot[0].reshape(2, 6 * D)}
    for i, (n, _) in enumerate(_SMALL):
        g[n] = jnp.stack([tot[1 + i], tot[1 + len(_SMALL) + i]])
    for n, w in _CONV_SHARDED.items():
        g[n] = lax.dynamic_slice_in_dim(g[n], chip * w, w, axis=2)
    nd = 2 * 6 * D // 128
    dmod_all = jnp.transpose(got2[:, :nd].reshape(8, 2, 6 * D), (1, 0, 2))
    dmod_cols = lax.dynamic_slice_in_dim(jnp.pad(dmod_all, ((0, 0), (0, 8), (0, 0))), chip * ncol, ncol, axis=2)
    g["ada_w"] = _ada_bwd(c16, dmod_cols)

    delta, new_m, new_v = {}, {}, {}
    token = scatters["0a"][-1]
    for n, r0 in zip(_LATE, red0b):
        g[n], delta[n], new_m[n], new_v[n] = _adamw_layers(a[n], r0, red1[n], a["m_" + n], a["v_" + n], token, f"adamw_{n}")
    shp = ada_w.shape
    r2 = lambda t: t.reshape(-1, shp[-1])
    delta["ada_w"], new_m["ada_w"], new_v["ada_w"] = (
        o.reshape(shp) for o in _adamw(r2(ada_w), r2(g["ada_w"]), r2(m_ada_w), r2(v_ada_w), "adamw_ada_w"))
    behind = (delta[_LATE[-1]][0, 0, :1] + delta["ada_w"][0, 0, :1]).reshape(1)
    red0a = _reduce_finish(scatters["0a"], behind, "0a")
    for n, r0 in zip(_EARLY, red0a):
        if n == "w_in":
            g[n], delta[n], new_m[n], new_v[n] = _adamw_cols(a[n], r0, red1[n], a["m_" + n], a["v_" + n], f"adamw_{n}")
        else:
            g[n], delta[n], new_m[n], new_v[n] = _adamw_layers(a[n], r0, red1[n], a["m_" + n], a["v_" + n], token, f"adamw_{n}")
    rest = [n for n in _W_NAMES if n not in big and n != "ada_w"]
    packs = [_pack_flat([t[n] if pre is None else t[pre + n] for n in rest], 128)[0]
             for t, pre in ((a, None), (g, None), (a, "m_"), (a, "v_"))]
    rest_shapes = [a[n].shape for n in rest]
    outs = [_unpack_flat(o, rest_shapes) for o in _adamw(*packs, "adamw_rest")]
    for i, n in enumerate(rest):
        delta[n], new_m[n], new_v[n] = outs[0][i], outs[1][i], outs[2][i]

    return (loss, grad_x[None], *[g[n] for n in _W_NAMES], *[delta[n] for n in _W_NAMES],
            *[new_m[n] for n in _W_NAMES], *[new_v[n] for n in _W_NAMES])
```
